```python
import math
import jax, jax.numpy as jnp
from jax import lax
import numpy as np

D_MODEL = 1024
BATCH = 8
SEQ = 4096
DEPTH = 2

N_MEM = 256
RMS_EPS = 1e-6
MAX_POS_OFFSET = 1024
MASK_VALUE = -1e30

SSM_GROUPS = 32
SSM_GROUP_CH = 16
SSM_WIDTH = SSM_GROUPS * SSM_GROUP_CH
SSM_STATE = 64

MLA_HEADS = 8
MLA_Q_RANK = 512
MLA_KV_RANK = 256
MLA_NOPE = 64
MLA_ROPE = 32
MLA_V = 64
MLA_WIDTH = MLA_HEADS * MLA_V
ROPE_THETA = 10000.0
Q_BLOCK = 128

HG_HEADS = 4
HG_DK = 128
HG_DV = 128
HG_WIDTH = HG_HEADS * HG_DK
HG_CHUNK = 64

X_HEADS = 4
X_HEAD_DIM = 128
X_WIDTH = X_HEADS * X_HEAD_DIM

D_FF = -(-(8 * D_MODEL) // (3 * 256)) * 256

N_BRANCH = 3
IN_SPLITS = [SSM_WIDTH, MLA_Q_RANK, MLA_KV_RANK, MLA_ROPE,
             HG_HEADS * HG_DK, HG_HEADS * HG_DK, HG_HEADS * HG_DV, HG_HEADS * HG_DV,
             N_BRANCH * D_MODEL]
D_IN = sum(IN_SPLITS)

kernel_name = "hybrid_s5_mla_hgrn2_gated_block"


def rmsnorm(x, g):
    xf = x.astype(jnp.float32)
    y = xf * lax.rsqrt(jnp.mean(xf * xf, axis=-1, keepdims=True) + RMS_EPS)
    return (y * g.astype(jnp.float32)).astype(x.dtype)


def rope_tables(positions):
    half = MLA_ROPE // 2
    inv_freq = ROPE_THETA ** (-jnp.arange(half, dtype=jnp.float32) / half)
    ang = positions.astype(jnp.float32)[..., None] * inv_freq
    return jnp.cos(ang), jnp.sin(ang)


def apply_rope(x, cos, sin):
    xf = x.astype(jnp.float32)
    x1, x2 = jnp.split(xf, 2, axis=-1)
    return jnp.concatenate([x1 * cos - x2 * sin, x2 * cos + x1 * sin], axis=-1).astype(x.dtype)


def s5_mixer(u, lam_re, lam_im, b_re, b_im, c_re, c_im, d_skip, log_step, w_glu):
    bsz, s, _ = u.shape
    uf = u.astype(jnp.float32).reshape(bsz, s, SSM_GROUPS, SSM_GROUP_CH)
    lam = lax.complex(lam_re.astype(jnp.float32), lam_im.astype(jnp.float32))
    step = jnp.exp(log_step.astype(jnp.float32))[:, None]
    lam_bar = jnp.exp(lam * step)
    b_mat = lax.complex(b_re.astype(jnp.float32), b_im.astype(jnp.float32))
    b_bar = ((lam_bar - 1.0) / lam)[..., None] * b_mat
    bu = jnp.einsum('bsgh,gph->bsgp', uf.astype(jnp.complex64), b_bar)
    a = jnp.broadcast_to(lam_bar, bu.shape)

    def combine(left, right):
        a_l, b_l = left
        a_r, b_r = right
        return a_r * a_l, a_r * b_l + b_r

    _, states = lax.associative_scan(combine, (a, bu), axis=1)
    c_mat = lax.complex(c_re.astype(jnp.float32), c_im.astype(jnp.float32))
    y = jnp.real(jnp.einsum('bsgp,ghp->bsgh', states, c_mat)) + d_skip.astype(jnp.float32) * uf
    y = jax.nn.gelu(y.reshape(bsz, s, SSM_WIDTH)).astype(u.dtype)
    z_out, z_gate = jnp.split(y @ w_glu, 2, axis=-1)
    return z_out * jax.nn.sigmoid(z_gate)


def blocked_causal_attention(q, k, v, scale):
    bsz, s, h, dqk = q.shape
    nb = s // Q_BLOCK
    qb = q.reshape(bsz, nb, Q_BLOCK, h, dqk).transpose(1, 0, 2, 3, 4)
    kpos = jnp.arange(s)

    def one_block(args):
        q_blk, start = args
        sc = jnp.einsum('bqhd,bkhd->bhqk', q_blk, k, preferred_element_type=jnp.float32) * scale
        qpos = start + jnp.arange(Q_BLOCK)
        sc = jnp.where(kpos[None, :] <= qpos[:, None], sc, MASK_VALUE)
        p = jax.nn.softmax(sc, axis=-1).astype(v.dtype)
        return jnp.einsum('bhqk,bkhd->bqhd', p, v)

    ob = lax.map(one_block, (qb, jnp.arange(nb) * Q_BLOCK))
    return ob.transpose(1, 0, 2, 3, 4).reshape(bsz, s, h, v.shape[-1])


def mla_mixer(q_lat, kv_lat, k_rope, cos, sin, q_norm, kv_norm, w_uq, w_ukv, w_o):
    bsz, s, _ = q_lat.shape
    q = (rmsnorm(q_lat, q_norm) @ w_uq).reshape(bsz, s, MLA_HEADS, MLA_NOPE + MLA_ROPE)
    q_nope, q_pe = q[..., :MLA_NOPE], q[..., MLA_NOPE:]
    q_pe = apply_rope(q_pe, cos[:, :, None, :], sin[:, :, None, :])
    kv = (rmsnorm(kv_lat, kv_norm) @ w_ukv).reshape(bsz, s, MLA_HEADS, MLA_NOPE + MLA_V)
    k_nope, v = kv[..., :MLA_NOPE], kv[..., MLA_NOPE:]
    k_pe = apply_rope(k_rope, cos, sin)
    k = jnp.concatenate([k_nope, jnp.broadcast_to(k_pe[:, :, None, :], (bsz, s, MLA_HEADS, MLA_ROPE))], axis=-1)
    q = jnp.concatenate([q_nope, q_pe], axis=-1)
    o = blocked_causal_attention(q, k, v, 1.0 / math.sqrt(MLA_NOPE + MLA_ROPE))
    return o.reshape(bsz, s, MLA_WIDTH) @ w_o


def hgrn2_mixer(q, f_logit, i_in, g, lb, g_norm, w_o):
    bsz, s, _ = q.shape
    n_chunks = s // HG_CHUNK

    def to_chunks(t):
        return t.reshape(bsz, n_chunks, HG_CHUNK, HG_HEADS, -1).transpose(1, 0, 3, 2, 4)

    lbf = lb.astype(jnp.float32)
    sig = jax.nn.sigmoid(f_logit.astype(jnp.float32))
    f = lbf + (1.0 - lbf) * sig
    log_f = jnp.log(f)
    k = 1.0 - f
    xs = (to_chunks(jax.nn.silu(q.astype(jnp.float32))), to_chunks(k),
          to_chunks(i_in.astype(jnp.float32)), to_chunks(log_f))
    causal = jnp.tril(jnp.ones((HG_CHUNK, HG_CHUNK), dtype=bool))[:, :, None]

    def chunk_step(state, chunk):
        q_n, k_n, v_n, lf_n = chunk
        b = jnp.cumsum(lf_n, axis=2)
        diff = b[:, :, :, None, :] - b[:, :, None, :, :]
        decay = jnp.where(causal, jnp.exp(jnp.where(causal, diff, 0.0)), 0.0)
        attn = jnp.einsum('bhtd,bhsd,bhtsd->bhts', q_n, k_n, decay)
        o_n = jnp.einsum('bhts,bhse->bhte', attn, v_n) + jnp.einsum('bhtd,bhde->bhte', q_n * jnp.exp(b), state)
        b_last = b[:, :, -1, :]
        k_dec = k_n * jnp.exp(b_last[:, :, None, :] - b)
        state = jnp.exp(b_last)[..., None] * state + jnp.einsum('bhsd,bhse->bhde', k_dec, v_n)
        return state, o_n

    s0 = jnp.zeros((bsz, HG_HEADS, HG_DK, HG_DV), jnp.float32)
    _, o = lax.scan(chunk_step, s0, xs)
    o = o.transpose(1, 0, 3, 2, 4).reshape(bsz, s, HG_HEADS, HG_DV)
    gate = g.astype(jnp.float32).reshape(bsz, s, HG_HEADS, HG_DV)
    o = rmsnorm(o, g_norm) * jax.nn.silu(gate)
    return o.reshape(bsz, s, HG_HEADS * HG_DV).astype(g.dtype) @ w_o


def memory_cross_attention(h, mem_n, w_q, w_kv, w_o):
    bsz, s, _ = h.shape
    m = mem_n.shape[1]
    q = (h @ w_q).reshape(bsz, s, X_HEADS, X_HEAD_DIM)
    k, v = jnp.split(mem_n @ w_kv, 2, axis=-1)
    k = k.reshape(bsz, m, X_HEADS, X_HEAD_DIM)
    v = v.reshape(bsz, m, X_HEADS, X_HEAD_DIM)
    sc = jnp.einsum('bqhd,bkhd->bhqk', q, k, preferred_element_type=jnp.float32) / math.sqrt(X_HEAD_DIM)
    p = jax.nn.softmax(sc, axis=-1).astype(v.dtype)
    o = jnp.einsum('bhqk,bkhd->bqhd', p, v)
    return o.reshape(bsz, s, X_WIDTH) @ w_o


def swiglu_ffn(h, w_gate_up, w_down):
    gt, up = jnp.split(h @ w_gate_up, 2, axis=-1)
    return (jax.nn.silu(gt) * up) @ w_down


def _fwd_setup_inputs(seed: int = 0) -> dict:
    key = jax.random.key(seed)
    keys = list(jax.random.split(key, 48))
    L = DEPTH

    def nrm(shape, scale):
        return scale * jax.random.normal(keys.pop(), shape, jnp.float32)

    def gain(shape):
        return 1.0 + 0.05 * jax.random.normal(keys.pop(), shape, jnp.float32)

    x = nrm((BATCH, SEQ, D_MODEL), 1.0)
    mem = nrm((BATCH, N_MEM, D_MODEL), 1.0)
    offsets = jax.random.randint(keys.pop(), (BATCH, 1), 0, MAX_POS_OFFSET, dtype=jnp.int32)
    positions = (offsets + jnp.arange(SEQ, dtype=jnp.int32)[None, :]).astype(jnp.int32)
    state_idx = jnp.arange(SSM_STATE, dtype=jnp.float32)
    return {
        "x": x,
        "mem": mem,
        "positions": positions,
        "norm_mix": gain((L, D_MODEL)),
        "w_in": nrm((L, D_MODEL, D_IN), D_MODEL ** -0.5),
        "ssm_lam_re": -0.5 + nrm((L, SSM_GROUPS, SSM_STATE), 0.01),
        "ssm_lam_im": math.pi * state_idx + nrm((L, SSM_GROUPS, SSM_STATE), 0.01),
        "ssm_b_re": nrm((L, SSM_GROUPS, SSM_STATE, SSM_GROUP_CH), (2.0 * SSM_GROUP_CH) ** -0.5),
        "ssm_b_im": nrm((L, SSM_GROUPS, SSM_STATE, SSM_GROUP_CH), (2.0 * SSM_GROUP_CH) ** -0.5),
        "ssm_c_re": nrm((L, SSM_GROUPS, SSM_GROUP_CH, SSM_STATE), (2.0 * SSM_STATE) ** -0.5),
        "ssm_c_im": nrm((L, SSM_GROUPS, SSM_GROUP_CH, SSM_STATE), (2.0 * SSM_STATE) ** -0.5),
        "ssm_d": nrm((L, SSM_GROUPS, SSM_GROUP_CH), 1.0),
        "ssm_log_step": jax.random.uniform(keys.pop(), (L, SSM_GROUPS), jnp.float32, math.log(1e-3), math.log(1e-1)),
        "ssm_w_glu": nrm((L, SSM_WIDTH, 2 * D_MODEL), SSM_WIDTH ** -0.5),
        "mla_q_norm": gain((L, MLA_Q_RANK)),
        "mla_kv_norm": gain((L, MLA_KV_RANK)),
        "mla_w_uq": nrm((L, MLA_Q_RANK, MLA_HEADS * (MLA_NOPE + MLA_ROPE)), MLA_Q_RANK ** -0.5),
        "mla_w_ukv": nrm((L, MLA_KV_RANK, MLA_HEADS * (MLA_NOPE + MLA_V)), MLA_KV_RANK ** -0.5),
        "mla_w_o": nrm((L, MLA_WIDTH, D_MODEL), MLA_WIDTH ** -0.5),
        "hg_lb": nrm((L, HG_HEADS * HG_DK), 1.0),
        "hg_g_norm": gain((L, HG_DV)),
        "hg_w_o": nrm((L, HG_HEADS * HG_DV, D_MODEL), (HG_HEADS * HG_DV) ** -0.5),
        "w_out": nrm((L, D_MODEL, D_MODEL), D_MODEL ** -0.5),
        "norm_cross": gain((L, D_MODEL)),
        "norm_mem": gain((L, D_MODEL)),
        "x_w_q": nrm((L, D_MODEL, X_WIDTH), D_MODEL ** -0.5),
        "x_w_kv": nrm((L, D_MODEL, 2 * X_WIDTH), D_MODEL ** -0.5),
        "x_w_o": nrm((L, X_WIDTH, D_MODEL), X_WIDTH ** -0.5),
        "norm_ffn": gain((L, D_MODEL)),
        "ffn_w_gate_up": nrm((L, D_MODEL, 2 * D_FF), D_MODEL ** -0.5),
        "ffn_w_down": nrm((L, D_FF, D_MODEL), D_FF ** -0.5),
        "norm_final": gain((D_MODEL,)),
    }


def _fwd_reference(x, mem, positions, norm_mix, w_in, ssm_lam_re, ssm_lam_im, ssm_b_re, ssm_b_im,
              ssm_c_re, ssm_c_im, ssm_d, ssm_log_step, ssm_w_glu, mla_q_norm, mla_kv_norm,
              mla_w_uq, mla_w_ukv, mla_w_o, hg_lb, hg_g_norm, hg_w_o, w_out, norm_cross, norm_mem,
              x_w_q, x_w_kv, x_w_o, norm_ffn, ffn_w_gate_up, ffn_w_down, norm_final):
    bsz, s, _ = x.shape
    cos, sin = rope_tables(positions)
    lb_p = jax.nn.softmax(hg_lb.astype(jnp.float32), axis=0)
    lower_bounds = jnp.cumsum(lb_p, axis=0) - lb_p[0:1]
    split_at = np.cumsum(IN_SPLITS)[:-1].tolist()

    for l in range(DEPTH):
        h = rmsnorm(x, norm_mix[l])
        (u_ssm, q_lat, kv_lat, k_rope, hg_q, hg_f, hg_i, hg_g, gate_logits) = jnp.split(h @ w_in[l], split_at, axis=-1)
        y_ssm = s5_mixer(u_ssm, ssm_lam_re[l], ssm_lam_im[l], ssm_b_re[l], ssm_b_im[l],
                         ssm_c_re[l], ssm_c_im[l], ssm_d[l], ssm_log_step[l], ssm_w_glu[l])
        y_mla = mla_mixer(q_lat, kv_lat, k_rope, cos, sin, mla_q_norm[l], mla_kv_norm[l],
                          mla_w_uq[l], mla_w_ukv[l], mla_w_o[l])
        y_hg = hgrn2_mixer(hg_q, hg_f, hg_i, hg_g, lower_bounds[l], hg_g_norm[l], hg_w_o[l])
        gates = jax.nn.sigmoid(gate_logits.astype(jnp.float32)).reshape(bsz, s, N_BRANCH, D_MODEL)
        merged = (gates[:, :, 0] * y_ssm.astype(jnp.float32)
                  + gates[:, :, 1] * y_mla.astype(jnp.float32)
                  + gates[:, :, 2] * y_hg.astype(jnp.float32)).astype(x.dtype)
        x = x + merged @ w_out[l]
        x = x + memory_cross_attention(rmsnorm(x, norm_cross[l]), rmsnorm(mem, norm_mem[l]),
                                       x_w_q[l], x_w_kv[l], x_w_o[l])
        x = x + swiglu_ffn(rmsnorm(x, norm_ffn[l]), ffn_w_gate_up[l], ffn_w_down[l])
    return rmsnorm(x, norm_final)


import jax as _jax
import jax.numpy as _jnp

TWIN_FORMAT = 'train_step'
FWD_PARAMS = ['x', 'mem', 'positions', 'norm_mix', 'w_in', 'ssm_lam_re', 'ssm_lam_im', 'ssm_b_re', 'ssm_b_im', 'ssm_c_re', 'ssm_c_im', 'ssm_d', 'ssm_log_step', 'ssm_w_glu', 'mla_q_norm', 'mla_kv_norm', 'mla_w_uq', 'mla_w_ukv', 'mla_w_o', 'hg_lb', 'hg_g_norm', 'hg_w_o', 'w_out', 'norm_cross', 'norm_mem', 'x_w_q', 'x_w_kv', 'x_w_o', 'norm_ffn', 'ffn_w_gate_up', 'ffn_w_down', 'norm_final']
TWIN_WEIGHTS = ['norm_mix', 'w_in', 'ssm_lam_re', 'ssm_lam_im', 'ssm_b_re', 'ssm_b_im', 'ssm_c_re', 'ssm_c_im', 'ssm_d', 'ssm_log_step', 'ssm_w_glu', 'mla_q_norm', 'mla_kv_norm', 'mla_w_uq', 'mla_w_ukv', 'mla_w_o', 'hg_lb', 'hg_g_norm', 'hg_w_o', 'w_out', 'norm_cross', 'norm_mem', 'x_w_q', 'x_w_kv', 'x_w_o', 'norm_ffn', 'ffn_w_gate_up', 'ffn_w_down', 'norm_final']
TWIN_DIFF_INPUT = 'x'
TWIN_INPUTS = ['x', 'mem', 'positions', 'norm_mix', 'w_in', 'ssm_lam_re', 'ssm_lam_im', 'ssm_b_re', 'ssm_b_im', 'ssm_c_re', 'ssm_c_im', 'ssm_d', 'ssm_log_step', 'ssm_w_glu', 'mla_q_norm', 'mla_kv_norm', 'mla_w_uq', 'mla_w_ukv', 'mla_w_o', 'hg_lb', 'hg_g_norm', 'hg_w_o', 'w_out', 'norm_cross', 'norm_mem', 'x_w_q', 'x_w_kv', 'x_w_o', 'norm_ffn', 'ffn_w_gate_up', 'ffn_w_down', 'norm_final', 'loss_target', 'm_norm_mix', 'm_w_in', 'm_ssm_lam_re', 'm_ssm_lam_im', 'm_ssm_b_re', 'm_ssm_b_im', 'm_ssm_c_re', 'm_ssm_c_im', 'm_ssm_d', 'm_ssm_log_step', 'm_ssm_w_glu', 'm_mla_q_norm', 'm_mla_kv_norm', 'm_mla_w_uq', 'm_mla_w_ukv', 'm_mla_w_o', 'm_hg_lb', 'm_hg_g_norm', 'm_hg_w_o', 'm_w_out', 'm_norm_cross', 'm_norm_mem', 'm_x_w_q', 'm_x_w_kv', 'm_x_w_o', 'm_norm_ffn', 'm_ffn_w_gate_up', 'm_ffn_w_down', 'm_norm_final', 'v_norm_mix', 'v_w_in', 'v_ssm_lam_re', 'v_ssm_lam_im', 'v_ssm_b_re', 'v_ssm_b_im', 'v_ssm_c_re', 'v_ssm_c_im', 'v_ssm_d', 'v_ssm_log_step', 'v_ssm_w_glu', 'v_mla_q_norm', 'v_mla_kv_norm', 'v_mla_w_uq', 'v_mla_w_ukv', 'v_mla_w_o', 'v_hg_lb', 'v_hg_g_norm', 'v_hg_w_o', 'v_w_out', 'v_norm_cross', 'v_norm_mem', 'v_x_w_q', 'v_x_w_kv', 'v_x_w_o', 'v_norm_ffn', 'v_ffn_w_gate_up', 'v_ffn_w_down', 'v_norm_final']
TWIN_OUTPUTS = ['loss', 'grad_x', 'grad_norm_mix', 'grad_w_in', 'grad_ssm_lam_re', 'grad_ssm_lam_im', 'grad_ssm_b_re', 'grad_ssm_b_im', 'grad_ssm_c_re', 'grad_ssm_c_im', 'grad_ssm_d', 'grad_ssm_log_step', 'grad_ssm_w_glu', 'grad_mla_q_norm', 'grad_mla_kv_norm', 'grad_mla_w_uq', 'grad_mla_w_ukv', 'grad_mla_w_o', 'grad_hg_lb', 'grad_hg_g_norm', 'grad_hg_w_o', 'grad_w_out', 'grad_norm_cross', 'grad_norm_mem', 'grad_x_w_q', 'grad_x_w_kv', 'grad_x_w_o', 'grad_norm_ffn', 'grad_ffn_w_gate_up', 'grad_ffn_w_down', 'grad_norm_final', 'delta_norm_mix', 'delta_w_in', 'delta_ssm_lam_re', 'delta_ssm_lam_im', 'delta_ssm_b_re', 'delta_ssm_b_im', 'delta_ssm_c_re', 'delta_ssm_c_im', 'delta_ssm_d', 'delta_ssm_log_step', 'delta_ssm_w_glu', 'delta_mla_q_norm', 'delta_mla_kv_norm', 'delta_mla_w_uq', 'delta_mla_w_ukv', 'delta_mla_w_o', 'delta_hg_lb', 'delta_hg_g_norm', 'delta_hg_w_o', 'delta_w_out', 'delta_norm_cross', 'delta_norm_mem', 'delta_x_w_q', 'delta_x_w_kv', 'delta_x_w_o', 'delta_norm_ffn', 'delta_ffn_w_gate_up', 'delta_ffn_w_down', 'delta_norm_final', 'new_m_norm_mix', 'new_m_w_in', 'new_m_ssm_lam_re', 'new_m_ssm_lam_im', 'new_m_ssm_b_re', 'new_m_ssm_b_im', 'new_m_ssm_c_re', 'new_m_ssm_c_im', 'new_m_ssm_d', 'new_m_ssm_log_step', 'new_m_ssm_w_glu', 'new_m_mla_q_norm', 'new_m_mla_kv_norm', 'new_m_mla_w_uq', 'new_m_mla_w_ukv', 'new_m_mla_w_o', 'new_m_hg_lb', 'new_m_hg_g_norm', 'new_m_hg_w_o', 'new_m_w_out', 'new_m_norm_cross', 'new_m_norm_mem', 'new_m_x_w_q', 'new_m_x_w_kv', 'new_m_x_w_o', 'new_m_norm_ffn', 'new_m_ffn_w_gate_up', 'new_m_ffn_w_down', 'new_m_norm_final', 'new_v_norm_mix', 'new_v_w_in', 'new_v_ssm_lam_re', 'new_v_ssm_lam_im', 'new_v_ssm_b_re', 'new_v_ssm_b_im', 'new_v_ssm_c_re', 'new_v_ssm_c_im', 'new_v_ssm_d', 'new_v_ssm_log_step', 'new_v_ssm_w_glu', 'new_v_mla_q_norm', 'new_v_mla_kv_norm', 'new_v_mla_w_uq', 'new_v_mla_w_ukv', 'new_v_mla_w_o', 'new_v_hg_lb', 'new_v_hg_g_norm', 'new_v_hg_w_o', 'new_v_w_out', 'new_v_norm_cross', 'new_v_norm_mem', 'new_v_x_w_q', 'new_v_x_w_kv', 'new_v_x_w_o', 'new_v_norm_ffn', 'new_v_ffn_w_gate_up', 'new_v_ffn_w_down', 'new_v_norm_final']
TWIN_LEAF_KINDS = {'loss': 'loss', 'grad_x': 'grad_x', 'grad_norm_mix': 'grad_w', 'grad_w_in': 'grad_w', 'grad_ssm_lam_re': 'grad_w', 'grad_ssm_lam_im': 'grad_w', 'grad_ssm_b_re': 'grad_w', 'grad_ssm_b_im': 'grad_w', 'grad_ssm_c_re': 'grad_w', 'grad_ssm_c_im': 'grad_w', 'grad_ssm_d': 'grad_w', 'grad_ssm_log_step': 'grad_w', 'grad_ssm_w_glu': 'grad_w', 'grad_mla_q_norm': 'grad_w', 'grad_mla_kv_norm': 'grad_w', 'grad_mla_w_uq': 'grad_w', 'grad_mla_w_ukv': 'grad_w', 'grad_mla_w_o': 'grad_w', 'grad_hg_lb': 'grad_w', 'grad_hg_g_norm': 'grad_w', 'grad_hg_w_o': 'grad_w', 'grad_w_out': 'grad_w', 'grad_norm_cross': 'grad_w', 'grad_norm_mem': 'grad_w', 'grad_x_w_q': 'grad_w', 'grad_x_w_kv': 'grad_w', 'grad_x_w_o': 'grad_w', 'grad_norm_ffn': 'grad_w', 'grad_ffn_w_gate_up': 'grad_w', 'grad_ffn_w_down': 'grad_w', 'grad_norm_final': 'grad_w', 'delta_norm_mix': 'delta_w', 'delta_w_in': 'delta_w', 'delta_ssm_lam_re': 'delta_w', 'delta_ssm_lam_im': 'delta_w', 'delta_ssm_b_re': 'delta_w', 'delta_ssm_b_im': 'delta_w', 'delta_ssm_c_re': 'delta_w', 'delta_ssm_c_im': 'delta_w', 'delta_ssm_d': 'delta_w', 'delta_ssm_log_step': 'delta_w', 'delta_ssm_w_glu': 'delta_w', 'delta_mla_q_norm': 'delta_w', 'delta_mla_kv_norm': 'delta_w', 'delta_mla_w_uq': 'delta_w', 'delta_mla_w_ukv': 'delta_w', 'delta_mla_w_o': 'delta_w', 'delta_hg_lb': 'delta_w', 'delta_hg_g_norm': 'delta_w', 'delta_hg_w_o': 'delta_w', 'delta_w_out': 'delta_w', 'delta_norm_cross': 'delta_w', 'delta_norm_mem': 'delta_w', 'delta_x_w_q': 'delta_w', 'delta_x_w_kv': 'delta_w', 'delta_x_w_o': 'delta_w', 'delta_norm_ffn': 'delta_w', 'delta_ffn_w_gate_up': 'delta_w', 'delta_ffn_w_down': 'delta_w', 'delta_norm_final': 'delta_w', 'new_m_norm_mix': 'new_m', 'new_m_w_in': 'new_m', 'new_m_ssm_lam_re': 'new_m', 'new_m_ssm_lam_im': 'new_m', 'new_m_ssm_b_re': 'new_m', 'new_m_ssm_b_im': 'new_m', 'new_m_ssm_c_re': 'new_m', 'new_m_ssm_c_im': 'new_m', 'new_m_ssm_d': 'new_m', 'new_m_ssm_log_step': 'new_m', 'new_m_ssm_w_glu': 'new_m', 'new_m_mla_q_norm': 'new_m', 'new_m_mla_kv_norm': 'new_m', 'new_m_mla_w_uq': 'new_m', 'new_m_mla_w_ukv': 'new_m', 'new_m_mla_w_o': 'new_m', 'new_m_hg_lb': 'new_m', 'new_m_hg_g_norm': 'new_m', 'new_m_hg_w_o': 'new_m', 'new_m_w_out': 'new_m', 'new_m_norm_cross': 'new_m', 'new_m_norm_mem': 'new_m', 'new_m_x_w_q': 'new_m', 'new_m_x_w_kv': 'new_m', 'new_m_x_w_o': 'new_m', 'new_m_norm_ffn': 'new_m', 'new_m_ffn_w_gate_up': 'new_m', 'new_m_ffn_w_down': 'new_m', 'new_m_norm_final': 'new_m', 'new_v_norm_mix': 'new_v', 'new_v_w_in': 'new_v', 'new_v_ssm_lam_re': 'new_v', 'new_v_ssm_lam_im': 'new_v', 'new_v_ssm_b_re': 'new_v', 'new_v_ssm_b_im': 'new_v', 'new_v_ssm_c_re': 'new_v', 'new_v_ssm_c_im': 'new_v', 'new_v_ssm_d': 'new_v', 'new_v_ssm_log_step': 'new_v', 'new_v_ssm_w_glu': 'new_v', 'new_v_mla_q_norm': 'new_v', 'new_v_mla_kv_norm': 'new_v', 'new_v_mla_w_uq': 'new_v', 'new_v_mla_w_ukv': 'new_v', 'new_v_mla_w_o': 'new_v', 'new_v_hg_lb': 'new_v', 'new_v_hg_g_norm': 'new_v', 'new_v_hg_w_o': 'new_v', 'new_v_w_out': 'new_v', 'new_v_norm_cross': 'new_v', 'new_v_norm_mem': 'new_v', 'new_v_x_w_q': 'new_v', 'new_v_x_w_kv': 'new_v', 'new_v_x_w_o': 'new_v', 'new_v_norm_ffn': 'new_v', 'new_v_ffn_w_gate_up': 'new_v', 'new_v_ffn_w_down': 'new_v', 'new_v_norm_final': 'new_v'}


def _forward(args):
    return _fwd_reference(*[args[k] for k in FWD_PARAMS])


def _output_shape():
    def fwd():
        inp = _fwd_setup_inputs(0)
        return _fwd_reference(*[inp[k] for k in FWD_PARAMS])
    out = _jax.eval_shape(fwd)
    return out.shape, out.dtype

N_MICROBATCH = 1
ADAM_LR = 0.001
ADAM_B1 = 0.9
ADAM_B2 = 0.999
ADAM_EPS = 1e-08
ADAM_WD = 0.01
ADAM_STEP = 10
PER_EXAMPLE_BATCH_AXIS = {'x': 0, 'mem': 0, 'positions': 0, 'loss_target': 0}
SHARED_INPUTS = []
_WEIGHT_DTYPES = {'norm_mix': _jnp.float32, 'w_in': _jnp.float32, 'ssm_lam_re': _jnp.float32, 'ssm_lam_im': _jnp.float32, 'ssm_b_re': _jnp.float32, 'ssm_b_im': _jnp.float32, 'ssm_c_re': _jnp.float32, 'ssm_c_im': _jnp.float32, 'ssm_d': _jnp.float32, 'ssm_log_step': _jnp.float32, 'ssm_w_glu': _jnp.float32, 'mla_q_norm': _jnp.float32, 'mla_kv_norm': _jnp.float32, 'mla_w_uq': _jnp.float32, 'mla_w_ukv': _jnp.float32, 'mla_w_o': _jnp.float32, 'hg_lb': _jnp.float32, 'hg_g_norm': _jnp.float32, 'hg_w_o': _jnp.float32, 'w_out': _jnp.float32, 'norm_cross': _jnp.float32, 'norm_mem': _jnp.float32, 'x_w_q': _jnp.float32, 'x_w_kv': _jnp.float32, 'x_w_o': _jnp.float32, 'norm_ffn': _jnp.float32, 'ffn_w_gate_up': _jnp.float32, 'ffn_w_down': _jnp.float32, 'norm_final': _jnp.float32}
MOMENT_SCALE = {'norm_mix': 1.032730e-01, 'w_in': 4.129989e-02, 'ssm_lam_re': 2.553106e-03, 'ssm_lam_im': 2.688266e-03, 'ssm_b_re': 1.691992e-03, 'ssm_b_im': 1.700635e-03, 'ssm_c_re': 3.376447e-03, 'ssm_c_im': 3.383515e-03, 'ssm_d': 6.006047e-02, 'ssm_log_step': 1.871775e+00, 'ssm_w_glu': 2.940594e-02, 'mla_q_norm': 2.483696e-02, 'mla_kv_norm': 5.669900e-02, 'mla_w_uq': 2.080075e-02, 'mla_w_ukv': 2.763124e-02, 'mla_w_o': 2.311038e-02, 'hg_lb': 6.011794e-03, 'hg_g_norm': 1.867251e-01, 'hg_w_o': 5.990772e-02, 'w_out': 7.315590e-02, 'norm_cross': 1.939017e-02, 'norm_mem': 2.978209e-02, 'x_w_q': 2.728112e-02, 'x_w_kv': 2.842996e-02, 'x_w_o': 2.057609e-02, 'norm_ffn': 1.262291e-01, 'ffn_w_gate_up': 5.399085e-02, 'ffn_w_down': 8.856624e-02, 'norm_final': 3.208380e+01}


def _to_microbatches(a, axis):
    t = _jnp.moveaxis(a, axis, 0)
    t = t.reshape((N_MICROBATCH, t.shape[0] // N_MICROBATCH) + t.shape[1:])
    return _jnp.moveaxis(t, 1, axis + 1)


def setup_inputs(seed: int = 0) -> dict:
    inp = _fwd_setup_inputs(seed)
    key = _jax.random.fold_in(_jax.random.key(seed), 7919)
    shape, _ = _output_shape()
    out = dict(inp)
    out["loss_target"] = _jax.random.normal(_jax.random.fold_in(key, 0), shape, _jnp.float32)
    for i, name in enumerate(TWIN_WEIGHTS):
        w = inp[name].astype(_jnp.float32)
        if MOMENT_SCALE is None:
            s = _jnp.sqrt(_jnp.mean(_jnp.square(w)) + 1e-30)
        else:
            s = MOMENT_SCALE[name]
        km, kv = _jax.random.split(_jax.random.fold_in(key, i + 1))
        out[name] = w
        out["m_" + name] = s * _jax.random.normal(km, w.shape, _jnp.float32)
        out["v_" + name] = (s * s) * _jax.random.uniform(kv, w.shape, _jnp.float32, 0.5, 1.5)
    if N_MICROBATCH > 1:
        for name, axis in PER_EXAMPLE_BATCH_AXIS.items():
            out[name] = _to_microbatches(out[name], axis)
    return {'x': out['x'], 'mem': out['mem'], 'positions': out['positions'], 'norm_mix': out['norm_mix'], 'w_in': out['w_in'], 'ssm_lam_re': out['ssm_lam_re'], 'ssm_lam_im': out['ssm_lam_im'], 'ssm_b_re': out['ssm_b_re'], 'ssm_b_im': out['ssm_b_im'], 'ssm_c_re': out['ssm_c_re'], 'ssm_c_im': out['ssm_c_im'], 'ssm_d': out['ssm_d'], 'ssm_log_step': out['ssm_log_step'], 'ssm_w_glu': out['ssm_w_glu'], 'mla_q_norm': out['mla_q_norm'], 'mla_kv_norm': out['mla_kv_norm'], 'mla_w_uq': out['mla_w_uq'], 'mla_w_ukv': out['mla_w_ukv'], 'mla_w_o': out['mla_w_o'], 'hg_lb': out['hg_lb'], 'hg_g_norm': out['hg_g_norm'], 'hg_w_o': out['hg_w_o'], 'w_out': out['w_out'], 'norm_cross': out['norm_cross'], 'norm_mem': out['norm_mem'], 'x_w_q': out['x_w_q'], 'x_w_kv': out['x_w_kv'], 'x_w_o': out['x_w_o'], 'norm_ffn': out['norm_ffn'], 'ffn_w_gate_up': out['ffn_w_gate_up'], 'ffn_w_down': out['ffn_w_down'], 'norm_final': out['norm_final'], 'loss_target': out['loss_target'], 'm_norm_mix': out['m_norm_mix'], 'm_w_in': out['m_w_in'], 'm_ssm_lam_re': out['m_ssm_lam_re'], 'm_ssm_lam_im': out['m_ssm_lam_im'], 'm_ssm_b_re': out['m_ssm_b_re'], 'm_ssm_b_im': out['m_ssm_b_im'], 'm_ssm_c_re': out['m_ssm_c_re'], 'm_ssm_c_im': out['m_ssm_c_im'], 'm_ssm_d': out['m_ssm_d'], 'm_ssm_log_step': out['m_ssm_log_step'], 'm_ssm_w_glu': out['m_ssm_w_glu'], 'm_mla_q_norm': out['m_mla_q_norm'], 'm_mla_kv_norm': out['m_mla_kv_norm'], 'm_mla_w_uq': out['m_mla_w_uq'], 'm_mla_w_ukv': out['m_mla_w_ukv'], 'm_mla_w_o': out['m_mla_w_o'], 'm_hg_lb': out['m_hg_lb'], 'm_hg_g_norm': out['m_hg_g_norm'], 'm_hg_w_o': out['m_hg_w_o'], 'm_w_out': out['m_w_out'], 'm_norm_cross': out['m_norm_cross'], 'm_norm_mem': out['m_norm_mem'], 'm_x_w_q': out['m_x_w_q'], 'm_x_w_kv': out['m_x_w_kv'], 'm_x_w_o': out['m_x_w_o'], 'm_norm_ffn': out['m_norm_ffn'], 'm_ffn_w_gate_up': out['m_ffn_w_gate_up'], 'm_ffn_w_down': out['m_ffn_w_down'], 'm_norm_final': out['m_norm_final'], 'v_norm_mix': out['v_norm_mix'], 'v_w_in': out['v_w_in'], 'v_ssm_lam_re': out['v_ssm_lam_re'], 'v_ssm_lam_im': out['v_ssm_lam_im'], 'v_ssm_b_re': out['v_ssm_b_re'], 'v_ssm_b_im': out['v_ssm_b_im'], 'v_ssm_c_re': out['v_ssm_c_re'], 'v_ssm_c_im': out['v_ssm_c_im'], 'v_ssm_d': out['v_ssm_d'], 'v_ssm_log_step': out['v_ssm_log_step'], 'v_ssm_w_glu': out['v_ssm_w_glu'], 'v_mla_q_norm': out['v_mla_q_norm'], 'v_mla_kv_norm': out['v_mla_kv_norm'], 'v_mla_w_uq': out['v_mla_w_uq'], 'v_mla_w_ukv': out['v_mla_w_ukv'], 'v_mla_w_o': out['v_mla_w_o'], 'v_hg_lb': out['v_hg_lb'], 'v_hg_g_norm': out['v_hg_g_norm'], 'v_hg_w_o': out['v_hg_w_o'], 'v_w_out': out['v_w_out'], 'v_norm_cross': out['v_norm_cross'], 'v_norm_mem': out['v_norm_mem'], 'v_x_w_q': out['v_x_w_q'], 'v_x_w_kv': out['v_x_w_kv'], 'v_x_w_o': out['v_x_w_o'], 'v_norm_ffn': out['v_norm_ffn'], 'v_ffn_w_gate_up': out['v_ffn_w_gate_up'], 'v_ffn_w_down': out['v_ffn_w_down'], 'v_norm_final': out['v_norm_final']}


def _loss(weights, diff, rest, loss_target):
    with _jax.named_scope("forward"):
        args = {**rest, TWIN_DIFF_INPUT: diff, **{k: w.astype(_WEIGHT_DTYPES[k]) for k, w in weights.items()}}
        y = _forward(args)
    with _jax.named_scope("loss_head"):
        err = _jnp.square(y.astype(_jnp.float32) - loss_target)
        return 0.5 * _jnp.sum(_jnp.mean(err, axis=-1)) if err.ndim else 0.5 * err


def _adamw(w, g, m, v):
    m = ADAM_B1 * m + (1.0 - ADAM_B1) * g
    v = ADAM_B2 * v + (1.0 - ADAM_B2) * _jnp.square(g)
    m_hat = m / (1.0 - ADAM_B1 ** ADAM_STEP)
    v_hat = v / (1.0 - ADAM_B2 ** ADAM_STEP)
    delta = -ADAM_LR * (m_hat / (_jnp.sqrt(v_hat) + ADAM_EPS) + ADAM_WD * w)
    return delta, m, v


def reference(x, mem, positions, norm_mix, w_in, ssm_lam_re, ssm_lam_im, ssm_b_re, ssm_b_im, ssm_c_re, ssm_c_im, ssm_d, ssm_log_step, ssm_w_glu, mla_q_norm, mla_kv_norm, mla_w_uq, mla_w_ukv, mla_w_o, hg_lb, hg_g_norm, hg_w_o, w_out, norm_cross, norm_mem, x_w_q, x_w_kv, x_w_o, norm_ffn, ffn_w_gate_up, ffn_w_down, norm_final, loss_target, m_norm_mix, m_w_in, m_ssm_lam_re, m_ssm_lam_im, m_ssm_b_re, m_ssm_b_im, m_ssm_c_re, m_ssm_c_im, m_ssm_d, m_ssm_log_step, m_ssm_w_glu, m_mla_q_norm, m_mla_kv_norm, m_mla_w_uq, m_mla_w_ukv, m_mla_w_o, m_hg_lb, m_hg_g_norm, m_hg_w_o, m_w_out, m_norm_cross, m_norm_mem, m_x_w_q, m_x_w_kv, m_x_w_o, m_norm_ffn, m_ffn_w_gate_up, m_ffn_w_down, m_norm_final, v_norm_mix, v_w_in, v_ssm_lam_re, v_ssm_lam_im, v_ssm_b_re, v_ssm_b_im, v_ssm_c_re, v_ssm_c_im, v_ssm_d, v_ssm_log_step, v_ssm_w_glu, v_mla_q_norm, v_mla_kv_norm, v_mla_w_uq, v_mla_w_ukv, v_mla_w_o, v_hg_lb, v_hg_g_norm, v_hg_w_o, v_w_out, v_norm_cross, v_norm_mem, v_x_w_q, v_x_w_kv, v_x_w_o, v_norm_ffn, v_ffn_w_gate_up, v_ffn_w_down, v_norm_final):
    given = dict(x=x, mem=mem, positions=positions, norm_mix=norm_mix, w_in=w_in, ssm_lam_re=ssm_lam_re, ssm_lam_im=ssm_lam_im, ssm_b_re=ssm_b_re, ssm_b_im=ssm_b_im, ssm_c_re=ssm_c_re, ssm_c_im=ssm_c_im, ssm_d=ssm_d, ssm_log_step=ssm_log_step, ssm_w_glu=ssm_w_glu, mla_q_norm=mla_q_norm, mla_kv_norm=mla_kv_norm, mla_w_uq=mla_w_uq, mla_w_ukv=mla_w_ukv, mla_w_o=mla_w_o, hg_lb=hg_lb, hg_g_norm=hg_g_norm, hg_w_o=hg_w_o, w_out=w_out, norm_cross=norm_cross, norm_mem=norm_mem, x_w_q=x_w_q, x_w_kv=x_w_kv, x_w_o=x_w_o, norm_ffn=norm_ffn, ffn_w_gate_up=ffn_w_gate_up, ffn_w_down=ffn_w_down, norm_final=norm_final, loss_target=loss_target, m_norm_mix=m_norm_mix, m_w_in=m_w_in, m_ssm_lam_re=m_ssm_lam_re, m_ssm_lam_im=m_ssm_lam_im, m_ssm_b_re=m_ssm_b_re, m_ssm_b_im=m_ssm_b_im, m_ssm_c_re=m_ssm_c_re, m_ssm_c_im=m_ssm_c_im, m_ssm_d=m_ssm_d, m_ssm_log_step=m_ssm_log_step, m_ssm_w_glu=m_ssm_w_glu, m_mla_q_norm=m_mla_q_norm, m_mla_kv_norm=m_mla_kv_norm, m_mla_w_uq=m_mla_w_uq, m_mla_w_ukv=m_mla_w_ukv, m_mla_w_o=m_mla_w_o, m_hg_lb=m_hg_lb, m_hg_g_norm=m_hg_g_norm, m_hg_w_o=m_hg_w_o, m_w_out=m_w_out, m_norm_cross=m_norm_cross, m_norm_mem=m_norm_mem, m_x_w_q=m_x_w_q, m_x_w_kv=m_x_w_kv, m_x_w_o=m_x_w_o, m_norm_ffn=m_norm_ffn, m_ffn_w_gate_up=m_ffn_w_gate_up, m_ffn_w_down=m_ffn_w_down, m_norm_final=m_norm_final, v_norm_mix=v_norm_mix, v_w_in=v_w_in, v_ssm_lam_re=v_ssm_lam_re, v_ssm_lam_im=v_ssm_lam_im, v_ssm_b_re=v_ssm_b_re, v_ssm_b_im=v_ssm_b_im, v_ssm_c_re=v_ssm_c_re, v_ssm_c_im=v_ssm_c_im, v_ssm_d=v_ssm_d, v_ssm_log_step=v_ssm_log_step, v_ssm_w_glu=v_ssm_w_glu, v_mla_q_norm=v_mla_q_norm, v_mla_kv_norm=v_mla_kv_norm, v_mla_w_uq=v_mla_w_uq, v_mla_w_ukv=v_mla_w_ukv, v_mla_w_o=v_mla_w_o, v_hg_lb=v_hg_lb, v_hg_g_norm=v_hg_g_norm, v_hg_w_o=v_hg_w_o, v_w_out=v_w_out, v_norm_cross=v_norm_cross, v_norm_mem=v_norm_mem, v_x_w_q=v_x_w_q, v_x_w_kv=v_x_w_kv, v_x_w_o=v_x_w_o, v_norm_ffn=v_norm_ffn, v_ffn_w_gate_up=v_ffn_w_gate_up, v_ffn_w_down=v_ffn_w_down, v_norm_final=v_norm_final)
    weights = {n: given[n] for n in TWIN_WEIGHTS}
    shared = {n: given[n] for n in SHARED_INPUTS}
    per_example = {n: given[n] for n in ['x', 'mem', 'positions']}
    grad_fn = _jax.value_and_grad(_loss, argnums=(0, 1))

    def one_microbatch(ex, loss_target):
        ex = dict(ex)
        diff = ex.pop(TWIN_DIFF_INPUT)
        return grad_fn(weights, diff, {**shared, **ex}, loss_target)

    if N_MICROBATCH == 1:
        loss, (grad_w, grad_x) = one_microbatch(per_example, given["loss_target"])
    else:
        def body(carry, xs):
            loss_sum, grad_sum = carry
            l_k, (gw_k, gx_k) = one_microbatch(xs[0], xs[1])
            with _jax.named_scope("update"):
                return (loss_sum + l_k, _jax.tree.map(_jnp.add, grad_sum, gw_k)), gx_k

        init = (_jnp.zeros((), _jnp.float32), _jax.tree.map(_jnp.zeros_like, weights))
        (loss, grad_w), grad_x = _jax.lax.scan(body, init, (per_example, given["loss_target"]))
    with _jax.named_scope("update"):
        delta_w, new_m, new_v = {}, {}, {}
        for n in TWIN_WEIGHTS:
            delta_w[n], new_m[n], new_v[n] = _adamw(weights[n], grad_w[n], given["m_" + n], given["v_" + n])
    return (loss, grad_x, *[grad_w[n] for n in TWIN_WEIGHTS], *[delta_w[n] for n in TWIN_WEIGHTS],
            *[new_m[n] for n in TWIN_WEIGHTS], *[new_v[n] for n in TWIN_WEIGHTS])
```

```python
import functools
import math

import jax
import jax.numpy as jnp
from jax import lax
from jax.experimental import pallas as pl
from jax.experimental.pallas import tpu as pltpu

F32 = jnp.float32
BF16 = jnp.bfloat16
MESH = pl.DeviceIdType.MESH

D_MODEL = 1024
DEPTH = 2
RMS_EPS = 1e-6
SSM_GROUPS, SSM_GROUP_CH, SSM_STATE = 32, 16, 64
SSM_WIDTH = SSM_GROUPS * SSM_GROUP_CH
SSM_LANES = SSM_GROUPS * SSM_STATE
MLA_HEADS, MLA_Q_RANK, MLA_KV_RANK, MLA_NOPE, MLA_ROPE, MLA_V = 8, 512, 256, 64, 32, 64
ROPE_THETA = 10000.0
HG_HEADS, HG_DK, HG_DV, HG_CHUNK, HG_SUB = 4, 128, 128, 64, 16
X_HEADS, X_HEAD_DIM = 4, 128
D_FF = 2816
D_IN = 6432
D_IN_PAD = 6656
MASK_VALUE = -1e30

ADAM_LR, ADAM_B1, ADAM_B2, ADAM_EPS, ADAM_WD, ADAM_STEP = 0.001, 0.9, 0.999, 1e-08, 0.01, 10

VMEM_LIMIT = 48 * 1024 * 1024


def _pcall(body, **kw):
    return pl.pallas_call(body, **kw)


def _params(sem):
    return pltpu.CompilerParams(dimension_semantics=sem, vmem_limit_bytes=VMEM_LIMIT)


def _tile(n, cap, align=128):
    if n <= cap:
        return n
    t = (cap // align) * align
    while t >= align:
        if n % t == 0:
            return t
        t -= align
    return n


def _mm(a, b, *, ta=False, tb=False, out_dtype=F32, name="mm"):
    if ta:
        K, M = a.shape
    else:
        M, K = a.shape
    if tb:
        N, Kb = b.shape
    else:
        Kb, N = b.shape
    assert K == Kb, (a.shape, b.shape, ta, tb)
    tm = _tile(M, 512)
    tn = _tile(N, 512)
    tk = _tile(K, 2048)
    nk = K // tk
    dims = (((0,) if ta else (1,), (1,) if tb else (0,)), ((), ()))

    def body(a_ref, b_ref, o_ref, *acc):
        av = a_ref[...].astype(BF16)
        bv = b_ref[...].astype(BF16)
        part = lax.dot_general(av, bv, dims, preferred_element_type=F32)
        if nk == 1:
            o_ref[...] = part.astype(o_ref.dtype)
        else:
            acc_ref = acc[0]
            k = pl.program_id(2)

            @pl.when(k == 0)
            def _():
                acc_ref[...] = part

            @pl.when(k > 0)
            def _():
                acc_ref[...] += part

            @pl.when(k == nk - 1)
            def _():
                o_ref[...] = acc_ref[...].astype(o_ref.dtype)

    a_spec = pl.BlockSpec((tk, tm), lambda i, j, k: (k, i)) if ta else pl.BlockSpec((tm, tk), lambda i, j, k: (i, k))
    b_spec = pl.BlockSpec((tn, tk), lambda i, j, k: (j, k)) if tb else pl.BlockSpec((tk, tn), lambda i, j, k: (k, j))
    return _pcall(
        body,
        name=name,
        grid=(M // tm, N // tn, nk),
        in_specs=[a_spec, b_spec],
        out_specs=pl.BlockSpec((tm, tn), lambda i, j, k: (i, j)),
        out_shape=jax.ShapeDtypeStruct((M, N), out_dtype),
        scratch_shapes=[pltpu.VMEM((tm, tn), F32)] if nk > 1 else [],
        compiler_params=_params(("parallel", "parallel", "arbitrary")),
    )(a, b)


@functools.partial(jax.custom_vjp, nondiff_argnums=(2,))
def matmul(a, w, name):
    return _mm(a, w, name=name + "_f")


def _matmul_fwd(a, w, name):
    return _mm(a, w, name=name + "_f"), (a, w)


def _matmul_bwd(name, res, g):
    a, w = res
    return _mm(g, w, tb=True, name=name + "_da"), _mm(a, g, ta=True, name=name + "_dw")


matmul.defvjp(_matmul_fwd, _matmul_bwd)


@functools.partial(jax.custom_vjp, nondiff_argnums=(3,))
def matmul_w(a, w16, tok, name):
    return _mm(a, w16, name=name + "_f")


def _matmul_w_fwd(a, w16, tok, name):
    return _mm(a, w16, name=name + "_f"), (a, w16)


def _matmul_w_bwd(name, res, g):
    a, w16 = res
    return _mm(g, w16, tb=True, name=name + "_da"), jnp.zeros_like(w16), _mm(a, g, ta=True, name=name + "_dw")


matmul_w.defvjp(_matmul_w_fwd, _matmul_w_bwd)


def _rms_fwd_call(x, g, name):
    rows, d = x.shape
    tr = _tile(rows, 512, 8)

    def body(x_ref, g_ref, y_ref):
        xv = x_ref[...]
        rstd = lax.rsqrt(jnp.mean(xv * xv, axis=-1, keepdims=True) + RMS_EPS)
        y_ref[...] = xv * rstd * g_ref[...]

    return _pcall(
        body,
        name=name,
        grid=(rows // tr,),
        in_specs=[pl.BlockSpec((tr, d), lambda i: (i, 0)), pl.BlockSpec((1, d), lambda i: (0, 0))],
        out_specs=pl.BlockSpec((tr, d), lambda i: (i, 0)),
        out_shape=jax.ShapeDtypeStruct((rows, d), F32),
        compiler_params=_params(("parallel",)),
    )(x, g.reshape(1, d))


def _rms_bwd_call(x, g, dy, name):
    rows, d = x.shape
    tr = _tile(rows, 512, 8)
    nb = rows // tr

    def body(x_ref, g_ref, dy_ref, dx_ref, dg_ref):
        xv = x_ref[...]
        rstd = lax.rsqrt(jnp.mean(xv * xv, axis=-1, keepdims=True) + RMS_EPS)
        xh = xv * rstd
        dyv = dy_ref[...]
        dyg = dyv * g_ref[...]
        dx_ref[...] = rstd * (dyg - xh * jnp.mean(dyg * xh, axis=-1, keepdims=True))
        dg_ref[...] = jnp.sum((dyv * xh).reshape(tr // 8, 8, d), axis=0)

    dx, dgp = _pcall(
        body,
        name=name,
        grid=(nb,),
        in_specs=[
            pl.BlockSpec((tr, d), lambda i: (i, 0)),
            pl.BlockSpec((1, d), lambda i: (0, 0)),
            pl.BlockSpec((tr, d), lambda i: (i, 0)),
        ],
        out_specs=[pl.BlockSpec((tr, d), lambda i: (i, 0)), pl.BlockSpec((8, d), lambda i: (i, 0))],
        out_shape=[jax.ShapeDtypeStruct((rows, d), F32), jax.ShapeDtypeStruct((nb * 8, d), F32)],
        compiler_params=_params(("parallel",)),
    )(x, g.reshape(1, d), dy)
    return dx, jnp.sum(dgp, axis=0)


@functools.partial(jax.custom_vjp, nondiff_argnums=(2,))
def rmsnorm(x, g, name):
    return _rms_fwd_call(x, g, name + "_f")


def _rmsnorm_fwd(x, g, name):
    return _rms_fwd_call(x, g, name + "_f"), (x, g)


def _rmsnorm_bwd(name, res, dy):
    x, g = res
    return _rms_bwd_call(x, g, dy, name + "_b")


rmsnorm.defvjp(_rmsnorm_fwd, _rmsnorm_bwd)


def _attn_blocks(sq, sk):
    return _tile(sq, 512), _tile(sk, 512)


def _attn_scores(q_ref, k_ref, scale, causal, qi, ki, bq, bk):
    s = lax.dot_general(q_ref[0].astype(BF16), k_ref[0].astype(BF16), (((1,), (1,)), ((), ())),
                        preferred_element_type=F32) * scale
    if causal:
        qpos = qi * bq + lax.broadcasted_iota(jnp.int32, (bq, bk), 0)
        kpos = ki * bk + lax.broadcasted_iota(jnp.int32, (bq, bk), 1)
        s = jnp.where(kpos <= qpos, s, MASK_VALUE)
    return s


def _attn_fwd_call(q, k, v, scale, causal, name):
    h, sq, dq = q.shape
    _, sk, dv = v.shape
    bq, bk = _attn_blocks(sq, sk)
    nq, nk = sq // bq, sk // bk

    def body(q_ref, k_ref, v_ref, o_ref, lse_ref, m_sc, l_sc, acc_sc):
        qi, ki = pl.program_id(1), pl.program_id(2)

        @pl.when(ki == 0)
        def _():
            m_sc[...] = jnp.full(m_sc.shape, -jnp.inf, F32)
            l_sc[...] = jnp.zeros(l_sc.shape, F32)
            acc_sc[...] = jnp.zeros(acc_sc.shape, F32)

        def step():
            s = _attn_scores(q_ref, k_ref, scale, causal, qi, ki, bq, bk)
            m_prev = m_sc[...]
            m_new = jnp.maximum(m_prev, jnp.max(s, axis=-1, keepdims=True))
            alpha = jnp.exp(m_prev - m_new)
            p = jnp.exp(s - m_new)
            l_sc[...] = alpha * l_sc[...] + jnp.sum(p, axis=-1, keepdims=True)
            acc_sc[...] = alpha * acc_sc[...] + jnp.dot(p.astype(BF16), v_ref[0].astype(BF16),
                                                        preferred_element_type=F32)
            m_sc[...] = m_new

        if causal:
            pl.when(ki * bk <= qi * bq + bq - 1)(step)
        else:
            step()

        @pl.when(ki == nk - 1)
        def _():
            o_ref[0] = acc_sc[...] / l_sc[...]
            lse_ref[0] = m_sc[...] + jnp.log(l_sc[...])

    if causal:
        kmap = lambda hh, i, j: (hh, jnp.minimum(j, (i * bq + bq - 1) // bk), 0)
    else:
        kmap = lambda hh, i, j: (hh, j, 0)
    return _pcall(
        body,
        name=name,
        grid=(h, nq, nk),
        in_specs=[
            pl.BlockSpec((1, bq, dq), lambda hh, i, j: (hh, i, 0)),
            pl.BlockSpec((1, bk, dq), kmap),
            pl.BlockSpec((1, bk, dv), kmap),
        ],
        out_specs=[
            pl.BlockSpec((1, bq, dv), lambda hh, i, j: (hh, i, 0)),
            pl.BlockSpec((1, bq, 1), lambda hh, i, j: (hh, i, 0)),
        ],
        out_shape=[jax.ShapeDtypeStruct((h, sq, dv), F32), jax.ShapeDtypeStruct((h, sq, 1), F32)],
        scratch_shapes=[pltpu.VMEM((bq, 1), F32), pltpu.VMEM((bq, 1), F32), pltpu.VMEM((bq, dv), F32)],
        compiler_params=_params(("parallel", "parallel", "arbitrary")),
    )(q, k, v)


def _attn_dq_call(q, k, v, do, lse, delta, scale, causal, name):
    h, sq, dq = q.shape
    _, sk, dv = v.shape
    bq, bk = _attn_blocks(sq, sk)
    nq, nk = sq // bq, sk // bk

    def body(q_ref, k_ref, v_ref, do_ref, lse_ref, dl_ref, dq_ref, acc_sc):
        qi, ki = pl.program_id(1), pl.program_id(2)

        @pl.when(ki == 0)
        def _():
            acc_sc[...] = jnp.zeros(acc_sc.shape, F32)

        def step():
            s = _attn_scores(q_ref, k_ref, scale, causal, qi, ki, bq, bk)
            p = jnp.exp(s - lse_ref[0])
            dp = lax.dot_general(do_ref[0].astype(BF16), v_ref[0].astype(BF16), (((1,), (1,)), ((), ())),
                                 preferred_element_type=F32)
            ds = p * (dp - dl_ref[0])
            acc_sc[...] += jnp.dot(ds.astype(BF16), k_ref[0].astype(BF16), preferred_element_type=F32)

        if causal:
            pl.when(ki * bk <= qi * bq + bq - 1)(step)
        else:
            step()

        @pl.when(ki == nk - 1)
        def _():
            dq_ref[0] = acc_sc[...] * scale

    if causal:
        kmap = lambda hh, i, j: (hh, jnp.minimum(j, (i * bq + bq - 1) // bk), 0)
    else:
        kmap = lambda hh, i, j: (hh, j, 0)
    qmap = lambda hh, i, j: (hh, i, 0)
    return _pcall(
        body,
        name=name,
        grid=(h, nq, nk),
        in_specs=[
            pl.BlockSpec((1, bq, dq), qmap),
            pl.BlockSpec((1, bk, dq), kmap),
            pl.BlockSpec((1, bk, dv), kmap),
            pl.BlockSpec((1, bq, dv), qmap),
            pl.BlockSpec((1, bq, 1), qmap),
            pl.BlockSpec((1, bq, 1), qmap),
        ],
        out_specs=pl.BlockSpec((1, bq, dq), qmap),
        out_shape=jax.ShapeDtypeStruct((h, sq, dq), F32),
        scratch_shapes=[pltpu.VMEM((bq, dq), F32)],
        compiler_params=_params(("parallel", "parallel", "arbitrary")),
    )(q, k, v, do, lse, delta)


def _attn_dkv_call(q, k, v, do, lse, delta, scale, causal, name):
    h, sq, dq = q.shape
    _, sk, dv = v.shape
    bq, bk = _attn_blocks(sq, sk)
    nq, nk = sq // bq, sk // bk

    def body(q_ref, k_ref, v_ref, do_ref, lse_ref, dl_ref, dk_ref, dv_ref, dk_sc, dv_sc):
        ki, qi = pl.program_id(1), pl.program_id(2)

        @pl.when(qi == 0)
        def _():
            dk_sc[...] = jnp.zeros(dk_sc.shape, F32)
            dv_sc[...] = jnp.zeros(dv_sc.shape, F32)

        def step():
            s = _attn_scores(q_ref, k_ref, scale, causal, qi, ki, bq, bk)
            p = jnp.exp(s - lse_ref[0])
            dob = do_ref[0].astype(BF16)
            dv_sc[...] += lax.dot_general(p.astype(BF16), dob, (((0,), (0,)), ((), ())), preferred_element_type=F32)
            dp = lax.dot_general(dob, v_ref[0].astype(BF16), (((1,), (1,)), ((), ())), preferred_element_type=F32)
            ds = p * (dp - dl_ref[0])
            dk_sc[...] += lax.dot_general(ds.astype(BF16), q_ref[0].astype(BF16), (((0,), (0,)), ((), ())),
                                          preferred_element_type=F32)

        if causal:
            pl.when(qi * bq + bq - 1 >= ki * bk)(step)
        else:
            step()

        @pl.when(qi == nq - 1)
        def _():
            dk_ref[0] = dk_sc[...] * scale
            dv_ref[0] = dv_sc[...]

    if causal:
        qmap = lambda hh, j, i: (hh, jnp.maximum(i, (j * bk) // bq), 0)
    else:
        qmap = lambda hh, j, i: (hh, i, 0)
    kmap = lambda hh, j, i: (hh, j, 0)
    return _pcall(
        body,
        name=name,
        grid=(h, nk, nq),
        in_specs=[
            pl.BlockSpec((1, bq, dq), qmap),
            pl.BlockSpec((1, bk, dq), kmap),
            pl.BlockSpec((1, bk, dv), kmap),
            pl.BlockSpec((1, bq, dv), qmap),
            pl.BlockSpec((1, bq, 1), qmap),
            pl.BlockSpec((1, bq, 1), qmap),
        ],
        out_specs=[pl.BlockSpec((1, bk, dq), kmap), pl.BlockSpec((1, bk, dv), kmap)],
        out_shape=[jax.ShapeDtypeStruct((h, sk, dq), F32), jax.ShapeDtypeStruct((h, sk, dv), F32)],
        scratch_shapes=[pltpu.VMEM((bk, dq), F32), pltpu.VMEM((bk, dv), F32)],
        compiler_params=_params(("parallel", "parallel", "arbitrary")),
    )(q, k, v, do, lse, delta)


@functools.partial(jax.custom_vjp, nondiff_argnums=(3, 4, 5))
def attention(q, k, v, scale, causal, name):
    return _attn_fwd_call(q, k, v, scale, causal, name + "_f")[0]


def _attention_fwd(q, k, v, scale, causal, name):
    o, lse = _attn_fwd_call(q, k, v, scale, causal, name + "_f")
    return o, (q, k, v, o, lse)


def _attention_bwd(scale, causal, name, res, do):
    q, k, v, o, lse = res
    delta = jnp.sum(do * o, axis=-1, keepdims=True)
    dq = _attn_dq_call(q, k, v, do, lse, delta, scale, causal, name + "_dq")
    dk, dv = _attn_dkv_call(q, k, v, do, lse, delta, scale, causal, name + "_dkv")
    return dq, dk, dv


attention.defvjp(_attention_fwd, _attention_bwd)


SCAN_ROWS = 8
SCAN_LANES = 256
SCAN_TBLOCK = 512


def _scan_tables(lr, li):
    pr, pi = [lr], [li]
    for _ in range(SCAN_ROWS - 1):
        pr, pi = pr + [pr[-1] * lr - pi[-1] * li], pi + [pr[-1] * li + pi[-1] * lr]
    return jnp.concatenate(pr, axis=0), jnp.concatenate(pi, axis=0)


def _scan_call(ur, ui, lr, li, reverse, name):
    s, n = ur.shape
    tb = _tile(s, SCAN_TBLOCK, 8)
    nt, nl = s // tb, n // SCAN_LANES
    ntile = tb // SCAN_ROWS
    pr, pi = _scan_tables(lr, li)
    if reverse:
        pr, pi = pr[::-1], pi[::-1]
    shape = (SCAN_ROWS, SCAN_LANES)

    def body(ur_ref, ui_ref, pr_ref, pi_ref, xr_ref, xi_ref, cr_sc, ci_sc):
        @pl.when(pl.program_id(1) == 0)
        def _():
            cr_sc[...] = jnp.zeros(shape, F32)
            ci_sc[...] = jnp.zeros(shape, F32)

        prv, piv = pr_ref[...], pi_ref[...]
        rows = lax.broadcasted_iota(jnp.int32, shape, 0)

        def powers(k):
            r = (SCAN_ROWS - k) if reverse else (k - 1)
            return jnp.broadcast_to(prv[r:r + 1], shape), jnp.broadcast_to(piv[r:r + 1], shape)

        pw = [powers(k) for k in (1, 2, 4)]

        def tile(i, carry):
            cr, ci = carry
            j = (ntile - 1 - i) if reverse else i
            sl = pl.ds(pl.multiple_of(j * SCAN_ROWS, SCAN_ROWS), SCAN_ROWS)
            xr, xi = ur_ref[sl, :], ui_ref[sl, :]
            for (ar, ai), k in zip(pw, (1, 2, 4)):
                if reverse:
                    keep = rows < SCAN_ROWS - k
                    sr = jnp.where(keep, pltpu.roll(xr, SCAN_ROWS - k, 0), 0.0)
                    si = jnp.where(keep, pltpu.roll(xi, SCAN_ROWS - k, 0), 0.0)
                else:
                    keep = rows >= k
                    sr = jnp.where(keep, pltpu.roll(xr, k, 0), 0.0)
                    si = jnp.where(keep, pltpu.roll(xi, k, 0), 0.0)
                xr, xi = xr + (ar * sr - ai * si), xi + (ar * si + ai * sr)
            xr, xi = xr + (prv * cr - piv * ci), xi + (prv * ci + piv * cr)
            xr_ref[sl, :] = xr
            xi_ref[sl, :] = xi
            e = 0 if reverse else SCAN_ROWS - 1
            return jnp.broadcast_to(xr[e:e + 1], shape), jnp.broadcast_to(xi[e:e + 1], shape)

        cr, ci = lax.fori_loop(0, ntile, tile, (cr_sc[...], ci_sc[...]))
        cr_sc[...] = cr
        ci_sc[...] = ci

    tmap = (lambda l, t: (nt - 1 - t, l)) if reverse else (lambda l, t: (t, l))
    blk = pl.BlockSpec((tb, SCAN_LANES), tmap)
    tab = pl.BlockSpec((SCAN_ROWS, SCAN_LANES), lambda l, t: (0, l))
    return _pcall(
        body,
        name=name,
        grid=(nl, nt),
        in_specs=[blk, blk, tab, tab],
        out_specs=[blk, blk],
        out_shape=[jax.ShapeDtypeStruct((s, n), F32), jax.ShapeDtypeStruct((s, n), F32)],
        scratch_shapes=[pltpu.VMEM(shape, F32), pltpu.VMEM(shape, F32)],
        compiler_params=_params(("parallel", "arbitrary")),
    )(ur, ui, pr, pi)


@functools.partial(jax.custom_vjp, nondiff_argnums=(4,))
def s5_scan(ur, ui, lr, li, name):
    return tuple(_scan_call(ur, ui, lr, li, False, name + "_f"))


def _s5_scan_fwd(ur, ui, lr, li, name):
    xr, xi = _scan_call(ur, ui, lr, li, False, name + "_f")
    return (xr, xi), (xr, xi, lr, li)


def _s5_scan_bwd(name, res, g):
    xr, xi, lr, li = res
    gr, gi = _scan_call(g[0], g[1], lr, -li, True, name + "_b")
    dlr = jnp.sum(xr[:-1] * gr[1:] + xi[:-1] * gi[1:], axis=0, keepdims=True)
    dli = jnp.sum(xr[:-1] * gi[1:] - xi[:-1] * gr[1:], axis=0, keepdims=True)
    return gr, gi, dlr, dli


s5_scan.defvjp(_s5_scan_fwd, _s5_scan_bwd)


def _hg_chunk(q, k, v, lf, st):
    c = HG_CHUNK
    row = lax.broadcasted_iota(jnp.int32, (c, c), 0)
    col = lax.broadcasted_iota(jnp.int32, (c, c), 1)
    tri = (col <= row).astype(F32)
    b = jnp.dot(tri, lf, precision=lax.Precision.HIGHEST, preferred_element_type=F32)
    qb = q * jnp.exp(b)
    o = lax.dot_general(qb.astype(BF16), st.astype(BF16), (((1,), (1,)), ((), ())), preferred_element_type=F32)
    nsub = c // HG_SUB
    srow = lax.broadcasted_iota(jnp.int32, (HG_SUB, HG_SUB, 1), 0)
    scol = lax.broadcasted_iota(jnp.int32, (HG_SUB, HG_SUB, 1), 1)
    smask = scol <= srow
    outs = []
    for i in range(nsub):
        lo = i * HG_SUB
        bi, qi, ki = b[lo:lo + HG_SUB], q[lo:lo + HG_SUB], k[lo:lo + HG_SUB]
        diff = jnp.where(smask, bi[:, None, :] - bi[None, :, :], 0.0)
        e = jnp.where(smask, jnp.exp(diff), 0.0)
        a_ii = jnp.sum(qi[:, None, :] * ki[None, :, :] * e, axis=-1)
        oi = jnp.dot(a_ii.astype(BF16), v[lo:lo + HG_SUB].astype(BF16), preferred_element_type=F32)
        if i > 0:
            r = b[lo - 1:lo]
            qt = qi * jnp.exp(bi - r)
            kt = k[:lo] * jnp.exp(r - b[:lo])
            a_ij = lax.dot_general(qt.astype(BF16), kt.astype(BF16), (((1,), (1,)), ((), ())),
                                   preferred_element_type=F32)
            oi = oi + jnp.dot(a_ij.astype(BF16), v[:lo].astype(BF16), preferred_element_type=F32)
        outs.append(oi)
    o = o + jnp.concatenate(outs, axis=0)
    bl = b[c - 1:c]
    kd = k * jnp.exp(bl - b)
    st_new = st * jnp.exp(bl) + lax.dot_general(v.astype(BF16), kd.astype(BF16), (((0,), (0,)), ((), ())),
                                                preferred_element_type=F32)
    return o, st_new


def _hg_fwd_call(q, k, v, lf, name):
    s, w = q.shape
    nc = s // HG_CHUNK
    blk = pl.BlockSpec((HG_CHUNK, w), lambda i: (i, 0))

    def body(q_ref, k_ref, v_ref, lf_ref, o_ref, st_ref, st_sc):
        @pl.when(pl.program_id(0) == 0)
        def _():
            st_sc[...] = jnp.zeros(st_sc.shape, F32)

        for hh in range(HG_HEADS):
            cs = slice(hh * HG_DK, (hh + 1) * HG_DK)
            st = st_sc[hh]
            st_ref[0, hh] = st
            o, st_new = _hg_chunk(q_ref[:, cs], k_ref[:, cs], v_ref[:, cs], lf_ref[:, cs], st)
            o_ref[:, cs] = o
            st_sc[hh] = st_new

    return _pcall(
        body,
        name=name,
        grid=(nc,),
        in_specs=[blk, blk, blk, blk],
        out_specs=[blk, pl.BlockSpec((1, HG_HEADS, HG_DV, HG_DK), lambda i: (i, 0, 0, 0))],
        out_shape=[jax.ShapeDtypeStruct((s, w), F32), jax.ShapeDtypeStruct((nc, HG_HEADS, HG_DV, HG_DK), F32)],
        scratch_shapes=[pltpu.VMEM((HG_HEADS, HG_DV, HG_DK), F32)],
        compiler_params=_params(("arbitrary",)),
    )(q, k, v, lf)


def _hg_bwd_call(q, k, v, lf, sts, do, name):
    s, w = q.shape
    nc = s // HG_CHUNK
    blk = pl.BlockSpec((HG_CHUNK, w), lambda i: (nc - 1 - i, 0))

    def body(q_ref, k_ref, v_ref, lf_ref, st_ref, do_ref, dq_ref, dk_ref, dv_ref, dlf_ref, dst_sc):
        @pl.when(pl.program_id(0) == 0)
        def _():
            dst_sc[...] = jnp.zeros(dst_sc.shape, F32)

        for hh in range(HG_HEADS):
            cs = slice(hh * HG_DK, (hh + 1) * HG_DK)
            _, vjp = jax.vjp(_hg_chunk, q_ref[:, cs], k_ref[:, cs], v_ref[:, cs], lf_ref[:, cs], st_ref[0, hh])
            dq, dk, dv, dlf, dst = vjp((do_ref[:, cs], dst_sc[hh]))
            dq_ref[:, cs] = dq
            dk_ref[:, cs] = dk
            dv_ref[:, cs] = dv
            dlf_ref[:, cs] = dlf
            dst_sc[hh] = dst

    out = jax.ShapeDtypeStruct((s, w), F32)
    return _pcall(
        body,
        name=name,
        grid=(nc,),
        in_specs=[blk, blk, blk, blk, pl.BlockSpec((1, HG_HEADS, HG_DV, HG_DK), lambda i: (nc - 1 - i, 0, 0, 0)), blk],
        out_specs=[blk, blk, blk, blk],
        out_shape=[out, out, out, out],
        scratch_shapes=[pltpu.VMEM((HG_HEADS, HG_DV, HG_DK), F32)],
        compiler_params=_params(("arbitrary",)),
    )(q, k, v, lf, sts, do)


@functools.partial(jax.custom_vjp, nondiff_argnums=(4,))
def hgrn_core(q, k, v, lf, name):
    return _hg_fwd_call(q, k, v, lf, name + "_f")[0]


def _hgrn_core_fwd(q, k, v, lf, name):
    o, sts = _hg_fwd_call(q, k, v, lf, name + "_f")
    return o, (q, k, v, lf, sts)


def _hgrn_core_bwd(name, res, do):
    q, k, v, lf, sts = res
    return tuple(_hg_bwd_call(q, k, v, lf, sts, do, name + "_b"))


hgrn_core.defvjp(_hgrn_core_fwd, _hgrn_core_bwd)


def _adamw_call(w, g, m, v, name):
    rows, cols = w.shape
    tr = _tile(rows, 256, 8)
    c1 = 1.0 / (1.0 - ADAM_B1 ** ADAM_STEP)
    c2 = 1.0 / (1.0 - ADAM_B2 ** ADAM_STEP)

    def body(w_ref, g_ref, m_ref, v_ref, d_ref, mo_ref, vo_ref):
        gv = g_ref[...]
        mn = ADAM_B1 * m_ref[...] + (1.0 - ADAM_B1) * gv
        vn = ADAM_B2 * v_ref[...] + (1.0 - ADAM_B2) * (gv * gv)
        d_ref[...] = -ADAM_LR * ((mn * c1) / (jnp.sqrt(vn * c2) + ADAM_EPS) + ADAM_WD * w_ref[...])
        mo_ref[...] = mn
        vo_ref[...] = vn

    blk = pl.BlockSpec((tr, cols), lambda i: (i, 0))
    out = jax.ShapeDtypeStruct((rows, cols), F32)
    return _pcall(
        body,
        name=name,
        grid=(rows // tr,),
        in_specs=[blk, blk, blk, blk],
        out_specs=[blk, blk, blk],
        out_shape=[out, out, out],
        compiler_params=_params(("parallel",)),
    )(w, g, m, v)


ANY = pl.BlockSpec(memory_space=pl.ANY)


def _place():
    return lax.axis_index("x"), lax.axis_index("y"), lax.axis_index("c")


def _other_chips(x, y):
    return [(1 - x, y), (x, 1 - y), (1 - x, 1 - y)]


def _gather_weights(wsh):
    r, wd = wsh.shape
    hh = r // 2

    def body(w_ref, g_ref, send_sems, recv_sems, local_sem):
        x, y, c = _place()
        myq = 2 * x + y
        sibling = (x, y, 1 - c)
        chips = _other_chips(x, y)
        mine = pl.ds(c * hh, hh)
        theirs = pl.ds((1 - c) * hh, hh)

        def copy(k, src, dst, to):
            return pltpu.make_async_remote_copy(src_ref=src, dst_ref=dst, send_sem=send_sems.at[k],
                                                recv_sem=recv_sems.at[k], device_id=to, device_id_type=MESH)

        own = pltpu.make_async_copy(w_ref, g_ref.at[myq], local_sem)
        own.start()
        first = [copy(j, w_ref.at[mine], g_ref.at[myq, mine], (px, py, c)) for j, (px, py) in enumerate(chips)]
        for cp in first:
            cp.start()
        passed = []
        for j, (px, py) in enumerate(chips):
            q = 2 * px + py
            copy(j, w_ref.at[mine], g_ref.at[q, mine], (px, py, c)).wait_recv()
            fwd = copy(3 + j, g_ref.at[q, mine], g_ref.at[q, mine], sibling)
            fwd.start()
            passed.append(fwd)
        for j, (px, py) in enumerate(chips):
            q = 2 * px + py
            copy(3 + j, g_ref.at[q, theirs], g_ref.at[q, theirs], sibling).wait_recv()
        for cp in first + passed:
            cp.wait_send()
        own.wait()

    return _pcall(
        body,
        name="gather_weights",
        in_specs=[ANY],
        out_specs=ANY,
        out_shape=jax.ShapeDtypeStruct((4, r, wd), wsh.dtype),
        scratch_shapes=[pltpu.SemaphoreType.DMA((6,)), pltpu.SemaphoreType.DMA((6,)), pltpu.SemaphoreType.DMA],
    )(wsh)


def _swap_halves_to_sibling(gr):
    nq, r, wd = gr.shape
    hh = r // 2

    def body(g_ref, t_ref, send_sem, recv_sem):
        x, y, c = _place()
        cp = pltpu.make_async_remote_copy(src_ref=g_ref.at[:, pl.ds((1 - c) * hh, hh)], dst_ref=t_ref,
                                          send_sem=send_sem, recv_sem=recv_sem, device_id=(x, y, 1 - c),
                                          device_id_type=MESH)
        cp.start()
        cp.wait()

    return _pcall(
        body,
        name="grad_d2d",
        in_specs=[ANY],
        out_specs=ANY,
        out_shape=jax.ShapeDtypeStruct((nq, hh, wd), gr.dtype),
        scratch_shapes=[pltpu.SemaphoreType.DMA, pltpu.SemaphoreType.DMA],
    )(gr)


def _add_half(gr, t1, c_arr):
    nq, r, wd = gr.shape
    hh = r // 2
    tr = _tile(hh, 512, 8)
    nb = hh // tr

    def body(c_ref, g_ref, t_ref, o_ref):
        o_ref[...] = g_ref[...] + t_ref[...]

    return _pcall(
        body,
        name="grad_add_half",
        grid_spec=pltpu.PrefetchScalarGridSpec(
            num_scalar_prefetch=1,
            grid=(nq, nb),
            in_specs=[
                pl.BlockSpec((1, tr, wd), lambda q, i, c_ref: (q, c_ref[0] * nb + i, 0)),
                pl.BlockSpec((1, tr, wd), lambda q, i, c_ref: (q, i, 0)),
            ],
            out_specs=pl.BlockSpec((1, tr, wd), lambda q, i, c_ref: (q, i, 0)),
        ),
        out_shape=jax.ShapeDtypeStruct((nq, hh, wd), F32),
        compiler_params=_params(("parallel", "parallel")),
    )(c_arr, gr, t1)


def _scatter_partials(p):
    nq, hh, wd = p.shape

    def body(p_ref, t_ref, send_sems, recv_sems):
        x, y, c = _place()
        cps = []
        for j, (px, py) in enumerate(_other_chips(x, y)):
            cp = pltpu.make_async_remote_copy(src_ref=p_ref.at[2 * px + py], dst_ref=t_ref.at[j],
                                              send_sem=send_sems.at[j], recv_sem=recv_sems.at[j],
                                              device_id=(px, py, c), device_id_type=MESH)
            cp.start()
            cps.append(cp)
        for cp in cps:
            cp.wait()

    return _pcall(
        body,
        name="grad_ici",
        in_specs=[ANY],
        out_specs=ANY,
        out_shape=jax.ShapeDtypeStruct((3, hh, wd), p.dtype),
        scratch_shapes=[pltpu.SemaphoreType.DMA((3,)), pltpu.SemaphoreType.DMA((3,))],
    )(p)


def _add_partials(p, t2, q_arr):
    nq, hh, wd = p.shape
    tr = _tile(hh, 512, 8)

    def body(q_ref, p_ref, t_ref, o_ref):
        o_ref[...] = ((p_ref[0] + t_ref[0]) + t_ref[1]) + t_ref[2]

    return _pcall(
        body,
        name="grad_add_partials",
        grid_spec=pltpu.PrefetchScalarGridSpec(
            num_scalar_prefetch=1,
            grid=(hh // tr,),
            in_specs=[
                pl.BlockSpec((1, tr, wd), lambda i, q_ref: (q_ref[0], i, 0)),
                pl.BlockSpec((3, tr, wd), lambda i, q_ref: (0, i, 0)),
            ],
            out_specs=pl.BlockSpec((tr, wd), lambda i, q_ref: (i, 0)),
        ),
        out_shape=jax.ShapeDtypeStruct((hh, wd), F32),
        compiler_params=_params(("parallel",)),
    )(q_arr, p, t2)


def _join_halves(half):
    hh, wd = half.shape

    def body(h_ref, f_ref, send_sem, recv_sem, local_sem):
        x, y, c = _place()
        mine = pl.ds(c * hh, hh)
        own = pltpu.make_async_copy(h_ref, f_ref.at[mine], local_sem)
        own.start()
        cp = pltpu.make_async_remote_copy(src_ref=h_ref, dst_ref=f_ref.at[mine], send_sem=send_sem,
                                          recv_sem=recv_sem, device_id=(x, y, 1 - c), device_id_type=MESH)
        cp.start()
        cp.wait_send()
        pltpu.make_async_remote_copy(src_ref=h_ref, dst_ref=f_ref.at[pl.ds((1 - c) * hh, hh)], send_sem=send_sem,
                                     recv_sem=recv_sem, device_id=(x, y, 1 - c), device_id_type=MESH).wait_recv()
        own.wait()

    return _pcall(
        body,
        name="grad_join",
        in_specs=[ANY],
        out_specs=ANY,
        out_shape=jax.ShapeDtypeStruct((2 * hh, wd), half.dtype),
        scratch_shapes=[pltpu.SemaphoreType.DMA, pltpu.SemaphoreType.DMA, pltpu.SemaphoreType.DMA],
    )(half)


N_DEV = 8


def _allreduce_small(buf):
    rows, wd = buf.shape
    n = rows // N_DEV

    def body(x_ref, o_ref, stage_ref, send_sems, recv_sems):
        x, y, c = _place()
        me = 4 * x + 2 * y + c

        def peer(k):
            return (x ^ (k >> 2), y ^ ((k >> 1) & 1), c ^ (k & 1))

        def chunk(ref, d):
            return ref.at[pl.ds(pl.multiple_of(d * n, 8), n)]

        first = []
        for k in range(1, N_DEV):
            px, py, pc = peer(k)
            cp = pltpu.make_async_remote_copy(src_ref=chunk(x_ref, 4 * px + 2 * py + pc), dst_ref=stage_ref.at[k],
                                              send_sem=send_sems.at[k - 1], recv_sem=recv_sems.at[k - 1],
                                              device_id=(px, py, pc), device_id_type=MESH)
            cp.start()
            first.append(cp)
        acc = chunk(x_ref, me)[...]
        for k in range(1, N_DEV):
            first[k - 1].wait_recv()
            acc = acc + stage_ref[k]
        chunk(o_ref, me)[...] = acc
        second = []
        for k in range(1, N_DEV):
            cp = pltpu.make_async_remote_copy(src_ref=chunk(o_ref, me), dst_ref=chunk(o_ref, me),
                                              send_sem=send_sems.at[6 + k], recv_sem=recv_sems.at[6 + k],
                                              device_id=peer(k), device_id_type=MESH)
            cp.start()
            second.append(cp)
        for k in range(1, N_DEV):
            px, py, pc = peer(k)
            them = 4 * px + 2 * py + pc
            pltpu.make_async_remote_copy(src_ref=chunk(o_ref, them), dst_ref=chunk(o_ref, them),
                                         send_sem=send_sems.at[6 + k], recv_sem=recv_sems.at[6 + k],
                                         device_id=(px, py, pc), device_id_type=MESH).wait_recv()
        for cp in first + second:
            cp.wait_send()

    vmem = pl.BlockSpec(memory_space=pltpu.VMEM)
    return _pcall(
        body,
        name="allreduce_small",
        in_specs=[vmem],
        out_specs=vmem,
        out_shape=jax.ShapeDtypeStruct((rows, wd), F32),
        scratch_shapes=[pltpu.VMEM((N_DEV, n, wd), F32), pltpu.SemaphoreType.DMA((14,)),
                        pltpu.SemaphoreType.DMA((14,))],
    )(buf)


def _rope(t, cos, sin):
    half = t.shape[-1] // 2
    t1, t2 = t[..., :half], t[..., half:]
    return jnp.concatenate([t1 * cos - t2 * sin, t2 * cos + t1 * sin], axis=-1)


def _s5_mixer(u, sp, l, w16, toks):
    g, p_, h = SSM_GROUPS, SSM_STATE, SSM_GROUP_CH
    lam_re, lam_im = sp["ssm_lam_re"][l], sp["ssm_lam_im"][l]
    step = jnp.exp(sp["ssm_log_step"][l])[:, None]
    mag, ang = jnp.exp(lam_re * step), lam_im * step
    lbr, lbi = mag * jnp.cos(ang), mag * jnp.sin(ang)
    den = lam_re * lam_re + lam_im * lam_im
    nr, ni = lbr - 1.0, lbi
    qr, qi = (nr * lam_re + ni * lam_im) / den, (ni * lam_re - nr * lam_im) / den
    b_re, b_im = sp["ssm_b_re"][l], sp["ssm_b_im"][l]
    bbr = qr[..., None] * b_re - qi[..., None] * b_im
    bbi = qr[..., None] * b_im + qi[..., None] * b_re
    eye = jnp.eye(g, dtype=F32)
    bmat_r = jnp.einsum("gph,gk->ghkp", bbr, eye).reshape(g * h, g * p_)
    bmat_i = jnp.einsum("gph,gk->ghkp", bbi, eye).reshape(g * h, g * p_)
    cmat_r = jnp.einsum("ghp,gk->gpkh", sp["ssm_c_re"][l], eye).reshape(g * p_, g * h)
    cmat_i = jnp.einsum("ghp,gk->gpkh", -sp["ssm_c_im"][l], eye).reshape(g * p_, g * h)
    nm = "ssm%d" % l
    ur = matmul(u, bmat_r, nm + "_bur")
    ui = matmul(u, bmat_i, nm + "_bui")
    xr, xi = s5_scan(ur, ui, lbr.reshape(1, g * p_), lbi.reshape(1, g * p_), nm + "_scan")
    y = matmul(xr, cmat_r, nm + "_cr") + matmul(xi, cmat_i, nm + "_ci") + sp["ssm_d"][l].reshape(1, g * h) * u
    y = jax.nn.gelu(y)
    zz = matmul_w(y, w16["ssm_w_glu", l], toks["ssm_w_glu", l], nm + "_glu")
    return zz[:, :D_MODEL] * jax.nn.sigmoid(zz[:, D_MODEL:])


def _mla_mixer(q_lat, kv_lat, k_rope, cos, sin, sp, l, w16, toks):
    s = q_lat.shape[0]
    nm = "mla%d" % l
    dqk = MLA_NOPE + MLA_ROPE
    q = matmul_w(rmsnorm(q_lat, sp["mla_q_norm"][l], nm + "_qn"), w16["mla_w_uq", l], toks["mla_w_uq", l], nm + "_uq")
    q = q.reshape(s, MLA_HEADS, dqk)
    q = jnp.concatenate([q[..., :MLA_NOPE], _rope(q[..., MLA_NOPE:], cos[:, None, :], sin[:, None, :])], axis=-1)
    kv = matmul_w(rmsnorm(kv_lat, sp["mla_kv_norm"][l], nm + "_kvn"), w16["mla_w_ukv", l], toks["mla_w_ukv", l],
                  nm + "_ukv").reshape(s, MLA_HEADS, MLA_NOPE + MLA_V)
    k_pe = _rope(k_rope, cos, sin)
    k = jnp.concatenate([kv[..., :MLA_NOPE], jnp.broadcast_to(k_pe[:, None, :], (s, MLA_HEADS, MLA_ROPE))], axis=-1)
    v = kv[..., MLA_NOPE:]
    o = attention(q.transpose(1, 0, 2), k.transpose(1, 0, 2), v.transpose(1, 0, 2), 1.0 / math.sqrt(dqk), True,
                  nm + "_att")
    o = o.transpose(1, 0, 2).reshape(s, MLA_HEADS * MLA_V)
    return matmul_w(o, w16["mla_w_o", l], toks["mla_w_o", l], nm + "_o")


def _hgrn_mixer(q, f_logit, i_in, g, lb, sp, l, w16, toks):
    s = q.shape[0]
    nm = "hg%d" % l
    f = lb + (1.0 - lb) * jax.nn.sigmoid(f_logit)
    o = hgrn_core(jax.nn.silu(q), 1.0 - f, i_in, jnp.log(f), nm + "_core")
    o = rmsnorm(o.reshape(s * HG_HEADS, HG_DV), sp["hg_g_norm"][l], nm + "_gn").reshape(s, HG_HEADS * HG_DV)
    return matmul_w(o * jax.nn.silu(g), w16["hg_w_o", l], toks["hg_w_o", l], nm + "_o")


def _cross_attention(h, mem_n, l, w16, toks):
    s, m = h.shape[0], mem_n.shape[0]
    nm = "xat%d" % l
    xw = X_HEADS * X_HEAD_DIM
    q = matmul_w(h, w16["x_w_q", l], toks["x_w_q", l], nm + "_q").reshape(s, X_HEADS, X_HEAD_DIM)
    kv = matmul_w(mem_n, w16["x_w_kv", l], toks["x_w_kv", l], nm + "_kv")
    k = kv[:, :xw].reshape(m, X_HEADS, X_HEAD_DIM)
    v = kv[:, xw:].reshape(m, X_HEADS, X_HEAD_DIM)
    o = attention(q.transpose(1, 0, 2), k.transpose(1, 0, 2), v.transpose(1, 0, 2), 1.0 / math.sqrt(X_HEAD_DIM),
                  False, nm + "_att")
    return matmul_w(o.transpose(1, 0, 2).reshape(s, xw), w16["x_w_o", l], toks["x_w_o", l], nm + "_o")


_IN_HG = SSM_WIDTH + MLA_Q_RANK + MLA_KV_RANK
_IN_GATE = _IN_HG + 4 * HG_HEADS * HG_DK
_IN_ROPE = _IN_GATE + 3 * D_MODEL


def _perm_in(w):
    pad = jnp.zeros(w.shape[:-1] + (D_IN_PAD - D_IN,), w.dtype)
    return jnp.concatenate([w[..., :_IN_HG], w[..., _IN_HG + MLA_ROPE:], w[..., _IN_HG:_IN_HG + MLA_ROPE], pad], axis=-1)


def _unperm_in(g):
    return jnp.concatenate([g[..., :_IN_HG], g[..., _IN_ROPE:_IN_ROPE + MLA_ROPE], g[..., _IN_HG:_IN_ROPE]], axis=-1)


def _layer(x, mem, cos, sin, lb, l, sp, w16, toks):
    s = x.shape[0]
    nm = "l%d" % l
    h = rmsnorm(x, sp["norm_mix"][l], nm + "_nmix")
    z = matmul_w(h, w16["w_in", l], toks["w_in", l], nm + "_in")
    hw = HG_HEADS * HG_DK
    u, q_lat, kv_lat = z[:, :SSM_WIDTH], z[:, SSM_WIDTH:SSM_WIDTH + MLA_Q_RANK], z[:, SSM_WIDTH + MLA_Q_RANK:_IN_HG]
    hg = [z[:, _IN_HG + i * hw:_IN_HG + (i + 1) * hw] for i in range(4)]
    gates = jax.nn.sigmoid(z[:, _IN_GATE:_IN_ROPE])
    k_rope = z[:, _IN_ROPE:_IN_ROPE + MLA_ROPE]
    y_ssm = _s5_mixer(u, sp, l, w16, toks)
    y_mla = _mla_mixer(q_lat, kv_lat, k_rope, cos, sin, sp, l, w16, toks)
    y_hg = _hgrn_mixer(hg[0], hg[1], hg[2], hg[3], lb, sp, l, w16, toks)
    merged = (gates[:, :D_MODEL] * y_ssm + gates[:, D_MODEL:2 * D_MODEL] * y_mla + gates[:, 2 * D_MODEL:] * y_hg)
    x = x + matmul_w(merged, w16["w_out", l], toks["w_out", l], nm + "_out")
    hc = rmsnorm(x, sp["norm_cross"][l], nm + "_ncross")
    mem_n = rmsnorm(mem, sp["norm_mem"][l], nm + "_nmem")
    x = x + _cross_attention(hc, mem_n, l, w16, toks)
    hf = rmsnorm(x, sp["norm_ffn"][l], nm + "_nffn")
    gu = matmul_w(hf, w16["ffn_w_gate_up", l], toks["ffn_w_gate_up", l], nm + "_gu")
    act = jax.nn.silu(gu[:, :D_FF]) * gu[:, D_FF:]
    return x + matmul_w(act, w16["ffn_w_down", l], toks["ffn_w_down", l], nm + "_down")


def _loss_fn(sp, toks, x, w16, mem, pos, target):
    half = MLA_ROPE // 2
    inv_freq = ROPE_THETA ** (-jnp.arange(half, dtype=F32) / half)
    ang = pos.astype(F32)[:, None] * inv_freq
    cos, sin = jnp.cos(ang), jnp.sin(ang)
    lb_p = jax.nn.softmax(sp["hg_lb"], axis=0)
    lower = jnp.cumsum(lb_p, axis=0) - lb_p[0:1]
    for l in range(DEPTH):
        x = _layer(x, mem, cos, sin, lower[l], l, sp, w16, toks)
    y = rmsnorm(x, sp["norm_final"], "nfinal")
    return 0.5 * jnp.sum(jnp.mean(jnp.square(y - target), axis=-1))


WEIGHTS = ["norm_mix", "w_in", "ssm_lam_re", "ssm_lam_im", "ssm_b_re", "ssm_b_im", "ssm_c_re", "ssm_c_im", "ssm_d",
           "ssm_log_step", "ssm_w_glu", "mla_q_norm", "mla_kv_norm", "mla_w_uq", "mla_w_ukv", "mla_w_o", "hg_lb",
           "hg_g_norm", "hg_w_o", "w_out", "norm_cross", "norm_mem", "x_w_q", "x_w_kv", "x_w_o", "norm_ffn",
           "ffn_w_gate_up", "ffn_w_down", "norm_final"]
INPUTS = ["x", "mem", "positions"] + WEIGHTS + ["loss_target"] + ["m_" + n for n in WEIGHTS] + ["v_" + n for n in WEIGHTS]
BIG = [("w_in", 2), ("ssm_w_glu", 2), ("mla_w_uq", 2), ("mla_w_ukv", 2), ("mla_w_o", 2), ("hg_w_o", 2), ("w_out", 1),
       ("x_w_q", 1), ("x_w_kv", 1), ("x_w_o", 2), ("ffn_w_gate_up", 2), ("ffn_w_down", 1)]
SMALL = [n for n in WEIGHTS if n not in dict(BIG)]
PACK_W = 1024
SMALL_W = 128
N_CHIPS = 4


def _pack(arrs, rows, width, dtype):
    flat = jnp.concatenate([a.astype(dtype).reshape(-1) for a in arrs])
    return jnp.pad(flat, (0, rows * width - flat.shape[0])).reshape(rows, width)


def _unpack(buf, like):
    flat = buf.reshape(-1)
    out, off = [], 0
    for a in like:
        n = math.prod(a.shape)
        out.append(flat[off:off + n].reshape(a.shape))
        off += n
    return out


def kernel(
    x, mem, positions, norm_mix, w_in, ssm_lam_re, ssm_lam_im, ssm_b_re, ssm_b_im, ssm_c_re, ssm_c_im, ssm_d,
    ssm_log_step, ssm_w_glu, mla_q_norm, mla_kv_norm, mla_w_uq, mla_w_ukv, mla_w_o, hg_lb, hg_g_norm, hg_w_o,
    w_out, norm_cross, norm_mem, x_w_q, x_w_kv, x_w_o, norm_ffn, ffn_w_gate_up, ffn_w_down, norm_final,
    loss_target, m_norm_mix, m_w_in, m_ssm_lam_re, m_ssm_lam_im, m_ssm_b_re, m_ssm_b_im, m_ssm_c_re, m_ssm_c_im,
    m_ssm_d, m_ssm_log_step, m_ssm_w_glu, m_mla_q_norm, m_mla_kv_norm, m_mla_w_uq, m_mla_w_ukv, m_mla_w_o,
    m_hg_lb, m_hg_g_norm, m_hg_w_o, m_w_out, m_norm_cross, m_norm_mem, m_x_w_q, m_x_w_kv, m_x_w_o, m_norm_ffn,
    m_ffn_w_gate_up, m_ffn_w_down, m_norm_final, v_norm_mix, v_w_in, v_ssm_lam_re, v_ssm_lam_im, v_ssm_b_re,
    v_ssm_b_im, v_ssm_c_re, v_ssm_c_im, v_ssm_d, v_ssm_log_step, v_ssm_w_glu, v_mla_q_norm, v_mla_kv_norm,
    v_mla_w_uq, v_mla_w_ukv, v_mla_w_o, v_hg_lb, v_hg_g_norm, v_hg_w_o, v_w_out, v_norm_cross, v_norm_mem,
    v_x_w_q, v_x_w_kv, v_x_w_o, v_norm_ffn, v_ffn_w_gate_up, v_ffn_w_down, v_norm_final):
    given = locals()
    a = {n: given[n] for n in INPUTS}
    x, mem, pos, target = a["x"][0], a["mem"][0], a["positions"][0], a["loss_target"][0]
    my_c = lax.axis_index("c").astype(jnp.int32).reshape(1)
    my_q = (2 * lax.axis_index("x") + lax.axis_index("y")).astype(jnp.int32).reshape(1)

    n_big = sum(math.prod(a[n].shape) for n, _ in BIG)
    rows = -(-n_big // (PACK_W * 32)) * 32
    gathered = _gather_weights(_pack([a[n] for n, _ in BIG], rows, PACK_W, BF16)).reshape(N_CHIPS, rows * PACK_W)
    w16, toks, off = {}, {}, 0
    for n, ax in BIG:
        shp = a[n].shape[1:]
        per = math.prod(shp)
        for l in range(DEPTH):
            full = jnp.concatenate([gathered[q, off:off + per].reshape(shp) for q in range(N_CHIPS)], axis=ax - 1)
            if n == "w_in":
                full = _perm_in(full)
            w16[n, l] = full
            toks[n, l] = jnp.zeros(full.shape, F32)
            off += per
    sp = {n: a[n] for n in SMALL}

    loss, (g_small, g_tok, g_x) = jax.value_and_grad(_loss_fn, argnums=(0, 1, 2))(sp, toks, x, w16, mem, pos, target)

    blocks = []
    for q in range(N_CHIPS):
        parts = []
        for n, ax in BIG:
            for l in range(DEPTH):
                g = _unperm_in(g_tok[n, l]) if n == "w_in" else g_tok[n, l]
                parts.append(jnp.split(g, N_CHIPS, axis=ax - 1)[q])
        blocks.append(_pack(parts, rows, PACK_W, F32))
    grads = jnp.stack(blocks)
    chip_sum = _add_half(grads, _swap_halves_to_sibling(grads), my_c)
    total = _add_partials(chip_sum, _scatter_partials(chip_sum), my_q)
    g_big = dict(zip([n for n, _ in BIG], _unpack(_join_halves(total), [a[n] for n, _ in BIG])))

    small_like = [a[n] for n in SMALL] + [jnp.zeros((1,), F32)]
    n_small = sum(math.prod(t.shape) for t in small_like)
    srows = -(-n_small // (SMALL_W * 8 * N_DEV)) * 8 * N_DEV
    reduced = _allreduce_small(_pack([g_small[n] for n in SMALL] + [loss.reshape(1)], srows, SMALL_W, F32))
    red = _unpack(reduced, small_like)
    g_sm = dict(zip(SMALL, red[:-1]))
    loss = red[-1].reshape(())

    out_g, out_d, out_m, out_v = {}, {}, {}, {}
    for n, _ in BIG:
        shp = a[n].shape
        two = lambda t: t.reshape(shp[0] * shp[1], shp[2])
        d, m, v = _adamw_call(two(a[n]), two(g_big[n]), two(a["m_" + n]), two(a["v_" + n]), "adamw_" + n)
        out_g[n], out_d[n], out_m[n], out_v[n] = g_big[n], d.reshape(shp), m.reshape(shp), v.reshape(shp)
    small_arrs = [a[n] for n in SMALL]
    d, m, v = _adamw_call(
        _pack(small_arrs, srows, SMALL_W, F32), reduced,
        _pack([a["m_" + n] for n in SMALL], srows, SMALL_W, F32),
        _pack([a["v_" + n] for n in SMALL], srows, SMALL_W, F32), "adamw_small")
    for n, gg, dd, mm_, vv in zip(SMALL, red[:-1], _unpack(d, small_arrs), _unpack(m, small_arrs), _unpack(v, small_arrs)):
        out_g[n], out_d[n], out_m[n], out_v[n] = gg, dd, mm_, vv

    return (loss, g_x[None], *[out_g[n] for n in WEIGHTS], *[out_d[n] for n in WEIGHTS],
            *[out_m[n] for n in WEIGHTS], *[out_v[n] for n in WEIGHTS])
```

```python
import functools
import math

import jax
import jax.numpy as jnp
from jax import lax
from jax.experimental import pallas as pl
from jax.experimental.pallas import tpu as pltpu

F32 = jnp.float32
BF16 = jnp.bfloat16
MESH = pl.DeviceIdType.MESH

D_MODEL = 1024
DEPTH = 2
RMS_EPS = 1e-6
SSM_GROUPS, SSM_GROUP_CH, SSM_STATE = 32, 16, 64
SSM_WIDTH = SSM_GROUPS * SSM_GROUP_CH
SSM_LANES = SSM_GROUPS * SSM_STATE
MLA_HEADS, MLA_Q_RANK, MLA_KV_RANK, MLA_NOPE, MLA_ROPE, MLA_V = 8, 512, 256, 64, 32, 64
ROPE_THETA = 10000.0
HG_HEADS, HG_DK, HG_DV, HG_CHUNK, HG_SUB = 4, 128, 128, 64, 16
X_HEADS, X_HEAD_DIM = 4, 128
D_FF = 2816
D_IN = 6432
D_IN_PAD = 6656
MASK_VALUE = -1e30

ADAM_LR, ADAM_B1, ADAM_B2, ADAM_EPS, ADAM_WD, ADAM_STEP = 0.001, 0.9, 0.999, 1e-08, 0.01, 10

VMEM_LIMIT = 48 * 1024 * 1024


def _pcall(body, **kw):
    return pl.pallas_call(body, **kw)


def _params(sem):
    return pltpu.CompilerParams(dimension_semantics=sem, vmem_limit_bytes=VMEM_LIMIT)


def _tile(n, cap, align=128):
    if n <= cap:
        return n
    t = (cap // align) * align
    while t >= align:
        if n % t == 0:
            return t
        t -= align
    return n


def _mm(a, b, *, ta=False, tb=False, out_dtype=F32, name="mm"):
    if ta:
        K, M = a.shape
    else:
        M, K = a.shape
    if tb:
        N, Kb = b.shape
    else:
        Kb, N = b.shape
    assert K == Kb, (a.shape, b.shape, ta, tb)
    tm = _tile(M, 512)
    tn = _tile(N, 512)
    tk = _tile(K, 2048)
    nk = K // tk
    dims = (((0,) if ta else (1,), (1,) if tb else (0,)), ((), ()))

    def body(a_ref, b_ref, o_ref, *acc):
        av = a_ref[...].astype(BF16)
        bv = b_ref[...].astype(BF16)
        part = lax.dot_general(av, bv, dims, preferred_element_type=F32)
        if nk == 1:
            o_ref[...] = part.astype(o_ref.dtype)
        else:
            acc_ref = acc[0]
            k = pl.program_id(2)

            @pl.when(k == 0)
            def _():
                acc_ref[...] = part

            @pl.when(k > 0)
            def _():
                acc_ref[...] += part

            @pl.when(k == nk - 1)
            def _():
                o_ref[...] = acc_ref[...].astype(o_ref.dtype)

    a_spec = pl.BlockSpec((tk, tm), lambda i, j, k: (k, i)) if ta else pl.BlockSpec((tm, tk), lambda i, j, k: (i, k))
    b_spec = pl.BlockSpec((tn, tk), lambda i, j, k: (j, k)) if tb else pl.BlockSpec((tk, tn), lambda i, j, k: (k, j))
    return _pcall(
        body,
        name=name,
        grid=(M // tm, N // tn, nk),
        in_specs=[a_spec, b_spec],
        out_specs=pl.BlockSpec((tm, tn), lambda i, j, k: (i, j)),
        out_shape=jax.ShapeDtypeStruct((M, N), out_dtype),
        scratch_shapes=[pltpu.VMEM((tm, tn), F32)] if nk > 1 else [],
        compiler_params=_params(("parallel", "parallel", "arbitrary")),
    )(a, b)


@functools.partial(jax.custom_vjp, nondiff_argnums=(2,))
def matmul(a, w, name):
    return _mm(a, w, name=name + "_f")


def _matmul_fwd(a, w, name):
    return _mm(a, w, name=name + "_f"), (a, w)


def _matmul_bwd(name, res, g):
    a, w = res
    return _mm(g, w, tb=True, name=name + "_da"), _mm(a, g, ta=True, name=name + "_dw")


matmul.defvjp(_matmul_fwd, _matmul_bwd)


@functools.partial(jax.custom_vjp, nondiff_argnums=(3,))
def matmul_w(a, w16, tok, name):
    return _mm(a, w16, name=name + "_f")


def _matmul_w_fwd(a, w16, tok, name):
    return _mm(a, w16, name=name + "_f"), (a, w16)


def _matmul_w_bwd(name, res, g):
    a, w16 = res
    return _mm(g, w16, tb=True, name=name + "_da"), jnp.zeros_like(w16), _mm(a, g, ta=True, name=name + "_dw")


matmul_w.defvjp(_matmul_w_fwd, _matmul_w_bwd)


@functools.partial(jax.custom_vjp, nondiff_argnums=(3,))
def matmul_w_t(a, w16, tok, name):
    return _mm(w16, a, ta=True, tb=True, name=name + "_f")


def _matmul_w_t_fwd(a, w16, tok, name):
    return _mm(w16, a, ta=True, tb=True, name=name + "_f"), (a, w16)


def _matmul_w_t_bwd(name, res, g):
    a, w16 = res
    return (_mm(g, w16, ta=True, tb=True, name=name + "_da"), jnp.zeros_like(w16),
            _mm(a, g, ta=True, tb=True, name=name + "_dw"))


matmul_w_t.defvjp(_matmul_w_t_fwd, _matmul_w_t_bwd)


@functools.partial(jax.custom_vjp, nondiff_argnums=(3,))
def matmul_w_at(at, w16, tok, name):
    return _mm(at, w16, ta=True, name=name + "_f")


def _matmul_w_at_fwd(at, w16, tok, name):
    return _mm(at, w16, ta=True, name=name + "_f"), (at, w16)


def _matmul_w_at_bwd(name, res, g):
    at, w16 = res
    return _mm(w16, g, tb=True, name=name + "_da"), jnp.zeros_like(w16), _mm(at, g, name=name + "_dw")


matmul_w_at.defvjp(_matmul_w_at_fwd, _matmul_w_at_bwd)


def _rms_fwd_call(x, g, name):
    rows, d = x.shape
    tr = _tile(rows, 512, 8)

    def body(x_ref, g_ref, y_ref):
        xv = x_ref[...]
        rstd = lax.rsqrt(jnp.mean(xv * xv, axis=-1, keepdims=True) + RMS_EPS)
        y_ref[...] = xv * rstd * g_ref[...]

    return _pcall(
        body,
        name=name,
        grid=(rows // tr,),
        in_specs=[pl.BlockSpec((tr, d), lambda i: (i, 0)), pl.BlockSpec((1, d), lambda i: (0, 0))],
        out_specs=pl.BlockSpec((tr, d), lambda i: (i, 0)),
        out_shape=jax.ShapeDtypeStruct((rows, d), F32),
        compiler_params=_params(("parallel",)),
    )(x, g.reshape(1, d))


def _rms_bwd_call(x, g, dy, name):
    rows, d = x.shape
    tr = _tile(rows, 512, 8)
    nb = rows // tr

    def body(x_ref, g_ref, dy_ref, dx_ref, dg_ref):
        xv = x_ref[...]
        rstd = lax.rsqrt(jnp.mean(xv * xv, axis=-1, keepdims=True) + RMS_EPS)
        xh = xv * rstd
        dyv = dy_ref[...]
        dyg = dyv * g_ref[...]
        dx_ref[...] = rstd * (dyg - xh * jnp.mean(dyg * xh, axis=-1, keepdims=True))
        dg_ref[...] = jnp.sum((dyv * xh).reshape(tr // 8, 8, d), axis=0)

    dx, dgp = _pcall(
        body,
        name=name,
        grid=(nb,),
        in_specs=[
            pl.BlockSpec((tr, d), lambda i: (i, 0)),
            pl.BlockSpec((1, d), lambda i: (0, 0)),
            pl.BlockSpec((tr, d), lambda i: (i, 0)),
        ],
        out_specs=[pl.BlockSpec((tr, d), lambda i: (i, 0)), pl.BlockSpec((8, d), lambda i: (i, 0))],
        out_shape=[jax.ShapeDtypeStruct((rows, d), F32), jax.ShapeDtypeStruct((nb * 8, d), F32)],
        compiler_params=_params(("parallel",)),
    )(x, g.reshape(1, d), dy)
    return dx, jnp.sum(dgp, axis=0)


@functools.partial(jax.custom_vjp, nondiff_argnums=(2,))
def rmsnorm(x, g, name):
    return _rms_fwd_call(x, g, name + "_f")


def _rmsnorm_fwd(x, g, name):
    return _rms_fwd_call(x, g, name + "_f"), (x, g)


def _rmsnorm_bwd(name, res, dy):
    x, g = res
    return _rms_bwd_call(x, g, dy, name + "_b")


rmsnorm.defvjp(_rmsnorm_fwd, _rmsnorm_bwd)


def _attn_blocks(sq, sk):
    return _tile(sq, 512), _tile(sk, 512)


def _attn_pt(qs, k_ref, qi, ki, bq, bk, masked, shift):
    st = jnp.dot(k_ref[0].astype(BF16), qs, preferred_element_type=F32)
    if masked:
        kpos = ki * bk + lax.broadcasted_iota(jnp.int32, (bk, bq), 0)
        qpos = qi * bq + lax.broadcasted_iota(jnp.int32, (bk, bq), 1)
        st = jnp.where(kpos <= qpos, st, MASK_VALUE)
    return st if shift is None else jnp.exp(st - shift)


def _attn_cases(causal, qi, ki, bq, bk, step):
    if not causal:
        step(False)
        return
    pl.when(ki * bk + bk - 1 <= qi * bq)(functools.partial(step, False))
    pl.when(jnp.logical_and(ki * bk + bk - 1 > qi * bq, ki * bk <= qi * bq + bq - 1))(functools.partial(step, True))


def _attn_fwd_call(qt, k, vt, scale, causal, name):
    h, sk, dq = k.shape
    sq = qt.shape[1]
    dv = vt.shape[0] // h
    bq, bk = _attn_blocks(sq, sk)
    nq, nk = sq // bq, sk // bk

    def body(q_ref, k_ref, v_ref, o_ref, lse_ref, m_sc, l_sc, acc_sc):
        qi, ki = pl.program_id(1), pl.program_id(2)

        @pl.when(ki == 0)
        def _():
            m_sc[...] = jnp.full(m_sc.shape, -jnp.inf, F32)
            l_sc[...] = jnp.zeros(l_sc.shape, F32)
            acc_sc[...] = jnp.zeros(acc_sc.shape, F32)

        def step(masked):
            qs = (q_ref[...] * scale).astype(BF16)
            st = _attn_pt(qs, k_ref, qi, ki, bq, bk, masked, None)
            m_prev = m_sc[...]
            m_new = jnp.maximum(m_prev, jnp.max(st, axis=0, keepdims=True))
            alpha = jnp.exp(m_prev - m_new)
            pt = jnp.exp(st - m_new)
            l_sc[...] = alpha * l_sc[...] + jnp.sum(pt, axis=0, keepdims=True)
            acc_sc[...] = alpha * acc_sc[...] + jnp.dot(v_ref[...].astype(BF16), pt.astype(BF16),
                                                        preferred_element_type=F32)
            m_sc[...] = m_new

        _attn_cases(causal, qi, ki, bq, bk, step)

        @pl.when(ki == nk - 1)
        def _():
            o_ref[...] = acc_sc[...] / l_sc[...]
            lse_ref[0] = m_sc[...] + jnp.log(l_sc[...])

    last = (lambda i, j: jnp.minimum(j, (i * bq + bq - 1) // bk)) if causal else (lambda i, j: j)
    qmap = lambda hh, i, j: (hh, i)
    return _pcall(
        body,
        name=name,
        grid=(h, nq, nk),
        in_specs=[
            pl.BlockSpec((dq, bq), qmap),
            pl.BlockSpec((1, bk, dq), lambda hh, i, j: (hh, last(i, j), 0)),
            pl.BlockSpec((dv, bk), lambda hh, i, j: (hh, last(i, j))),
        ],
        out_specs=[pl.BlockSpec((dv, bq), qmap), pl.BlockSpec((1, 1, bq), lambda hh, i, j: (hh, 0, i))],
        out_shape=[jax.ShapeDtypeStruct((h * dv, sq), F32), jax.ShapeDtypeStruct((h, 1, sq), F32)],
        scratch_shapes=[pltpu.VMEM((1, bq), F32), pltpu.VMEM((1, bq), F32), pltpu.VMEM((dv, bq), F32)],
        compiler_params=_params(("parallel", "parallel", "arbitrary")),
    )(qt, k, vt)


_TN = (((0,), (0,)), ((), ()))
_NT = (((1,), (1,)), ((), ()))


def _attn_dst(qs, k_ref, v_ref, do_ref, lse_ref, dl_ref, qi, ki, bq, bk, masked):
    pt = _attn_pt(qs, k_ref, qi, ki, bq, bk, masked, lse_ref[0])
    dpt = lax.dot_general(v_ref[...].astype(BF16), do_ref[...].astype(BF16), _TN, preferred_element_type=F32)
    return pt, pt * (dpt - dl_ref[0])


def _attn_dq_call(qt, k, vt, dot, lse, delta, scale, causal, name):
    h, sk, dq = k.shape
    sq = qt.shape[1]
    dv = vt.shape[0] // h
    bq, bk = _attn_blocks(sq, sk)
    nq, nk = sq // bq, sk // bk

    def body(q_ref, k_ref, v_ref, do_ref, lse_ref, dl_ref, dq_ref, acc_sc):
        qi, ki = pl.program_id(1), pl.program_id(2)

        @pl.when(ki == 0)
        def _():
            acc_sc[...] = jnp.zeros(acc_sc.shape, F32)

        def step(masked):
            qs = (q_ref[...] * scale).astype(BF16)
            _, dst = _attn_dst(qs, k_ref, v_ref, do_ref, lse_ref, dl_ref, qi, ki, bq, bk, masked)
            acc_sc[...] += lax.dot_general(k_ref[0].astype(BF16), dst.astype(BF16), _TN, preferred_element_type=F32)

        _attn_cases(causal, qi, ki, bq, bk, step)

        @pl.when(ki == nk - 1)
        def _():
            dq_ref[...] = acc_sc[...] * scale

    last = (lambda i, j: jnp.minimum(j, (i * bq + bq - 1) // bk)) if causal else (lambda i, j: j)
    qmap = lambda hh, i, j: (hh, i)
    rowmap = lambda hh, i, j: (hh, 0, i)
    return _pcall(
        body,
        name=name,
        grid=(h, nq, nk),
        in_specs=[
            pl.BlockSpec((dq, bq), qmap),
            pl.BlockSpec((1, bk, dq), lambda hh, i, j: (hh, last(i, j), 0)),
            pl.BlockSpec((dv, bk), lambda hh, i, j: (hh, last(i, j))),
            pl.BlockSpec((dv, bq), qmap),
            pl.BlockSpec((1, 1, bq), rowmap),
            pl.BlockSpec((1, 1, bq), rowmap),
        ],
        out_specs=pl.BlockSpec((dq, bq), qmap),
        out_shape=jax.ShapeDtypeStruct((h * dq, sq), F32),
        scratch_shapes=[pltpu.VMEM((dq, bq), F32)],
        compiler_params=_params(("parallel", "parallel", "arbitrary")),
    )(qt, k, vt, dot, lse, delta)


def _attn_dkv_call(qt, k, vt, dot, lse, delta, scale, causal, name):
    h, sk, dq = k.shape
    sq = qt.shape[1]
    dv = vt.shape[0] // h
    bq, bk = _attn_blocks(sq, sk)
    nq, nk = sq // bq, sk // bk

    def body(q_ref, k_ref, v_ref, do_ref, lse_ref, dl_ref, dk_ref, dv_ref, dk_sc, dv_sc):
        ki, qi = pl.program_id(1), pl.program_id(2)

        @pl.when(qi == 0)
        def _():
            dk_sc[...] = jnp.zeros(dk_sc.shape, F32)
            dv_sc[...] = jnp.zeros(dv_sc.shape, F32)

        def step(masked):
            qs = (q_ref[...] * scale).astype(BF16)
            pt, dst = _attn_dst(qs, k_ref, v_ref, do_ref, lse_ref, dl_ref, qi, ki, bq, bk, masked)
            dv_sc[...] += lax.dot_general(do_ref[...].astype(BF16), pt.astype(BF16), _NT, preferred_element_type=F32)
            dk_sc[...] += lax.dot_general(dst.astype(BF16), qs, _NT, preferred_element_type=F32)

        _attn_cases(causal, qi, ki, bq, bk, step)

        @pl.when(qi == nq - 1)
        def _():
            dk_ref[0] = dk_sc[...]
            dv_ref[...] = dv_sc[...]

    first = (lambda j, i: jnp.maximum(i, (j * bk) // bq)) if causal else (lambda j, i: i)
    qmap = lambda hh, j, i: (hh, first(j, i))
    rowmap = lambda hh, j, i: (hh, 0, first(j, i))
    return _pcall(
        body,
        name=name,
        grid=(h, nk, nq),
        in_specs=[
            pl.BlockSpec((dq, bq), qmap),
            pl.BlockSpec((1, bk, dq), lambda hh, j, i: (hh, j, 0)),
            pl.BlockSpec((dv, bk), lambda hh, j, i: (hh, j)),
            pl.BlockSpec((dv, bq), qmap),
            pl.BlockSpec((1, 1, bq), rowmap),
            pl.BlockSpec((1, 1, bq), rowmap),
        ],
        out_specs=[pl.BlockSpec((1, bk, dq), lambda hh, j, i: (hh, j, 0)), pl.BlockSpec((dv, bk), lambda hh, j, i: (hh, j))],
        out_shape=[jax.ShapeDtypeStruct((h, sk, dq), F32), jax.ShapeDtypeStruct((h * dv, sk), F32)],
        scratch_shapes=[pltpu.VMEM((bk, dq), F32), pltpu.VMEM((dv, bk), F32)],
        compiler_params=_params(("parallel", "parallel", "arbitrary")),
    )(qt, k, vt, dot, lse, delta)


@functools.partial(jax.custom_vjp, nondiff_argnums=(3, 4, 5))
def attention(qt, k, vt, scale, causal, name):
    return _attn_fwd_call(qt, k, vt, scale, causal, name + "_f")[0]


def _attention_fwd(qt, k, vt, scale, causal, name):
    ot, lse = _attn_fwd_call(qt, k, vt, scale, causal, name + "_f")
    return ot, (qt, k, vt, ot, lse)


def _attention_bwd(scale, causal, name, res, dot):
    qt, k, vt, ot, lse = res
    h = k.shape[0]
    delta = jnp.sum((dot * ot).reshape(h, ot.shape[0] // h, ot.shape[1]), axis=1, keepdims=True)
    dqt = _attn_dq_call(qt, k, vt, dot, lse, delta, scale, causal, name + "_dq")
    dk, dvt = _attn_dkv_call(qt, k, vt, dot, lse, delta, scale, causal, name + "_dkv")
    return dqt, dk, dvt


attention.defvjp(_attention_fwd, _attention_bwd)


SCAN_ROWS = 8
SCAN_LANES = 256
SCAN_TBLOCK = 512


def _scan_tables(lr, li):
    pr, pi = [lr], [li]
    for _ in range(SCAN_ROWS - 1):
        pr, pi = pr + [pr[-1] * lr - pi[-1] * li], pi + [pr[-1] * li + pi[-1] * lr]
    return jnp.concatenate(pr, axis=0), jnp.concatenate(pi, axis=0)


def _scan_call(ur, ui, lr, li, reverse, name):
    s, n = ur.shape
    tb = _tile(s, SCAN_TBLOCK, 8)
    nt, nl = s // tb, n // SCAN_LANES
    ntile = tb // SCAN_ROWS
    pr, pi = _scan_tables(lr, li)
    if reverse:
        pr, pi = pr[::-1], pi[::-1]
    shape = (SCAN_ROWS, SCAN_LANES)

    def body(ur_ref, ui_ref, pr_ref, pi_ref, xr_ref, xi_ref, cr_sc, ci_sc):
        @pl.when(pl.program_id(1) == 0)
        def _():
            cr_sc[...] = jnp.zeros(shape, F32)
            ci_sc[...] = jnp.zeros(shape, F32)

        prv, piv = pr_ref[...], pi_ref[...]
        rows = lax.broadcasted_iota(jnp.int32, shape, 0)

        def powers(k):
            r = (SCAN_ROWS - k) if reverse else (k - 1)
            return jnp.broadcast_to(prv[r:r + 1], shape), jnp.broadcast_to(piv[r:r + 1], shape)

        pw = [powers(k) for k in (1, 2, 4)]

        def tile(i, carry):
            cr, ci = carry
            j = (ntile - 1 - i) if reverse else i
            sl = pl.ds(pl.multiple_of(j * SCAN_ROWS, SCAN_ROWS), SCAN_ROWS)
            xr, xi = ur_ref[sl, :], ui_ref[sl, :]
            for (ar, ai), k in zip(pw, (1, 2, 4)):
                if reverse:
                    keep = rows < SCAN_ROWS - k
                    sr = jnp.where(keep, pltpu.roll(xr, SCAN_ROWS - k, 0), 0.0)
                    si = jnp.where(keep, pltpu.roll(xi, SCAN_ROWS - k, 0), 0.0)
                else:
                    keep = rows >= k
                    sr = jnp.where(keep, pltpu.roll(xr, k, 0), 0.0)
                    si = jnp.where(keep, pltpu.roll(xi, k, 0), 0.0)
                xr, xi = xr + (ar * sr - ai * si), xi + (ar * si + ai * sr)
            xr, xi = xr + (prv * cr - piv * ci), xi + (prv * ci + piv * cr)
            xr_ref[sl, :] = xr
            xi_ref[sl, :] = xi
            e = 0 if reverse else SCAN_ROWS - 1
            return jnp.broadcast_to(xr[e:e + 1], shape), jnp.broadcast_to(xi[e:e + 1], shape)

        cr, ci = lax.fori_loop(0, ntile, tile, (cr_sc[...], ci_sc[...]))
        cr_sc[...] = cr
        ci_sc[...] = ci

    tmap = (lambda l, t: (nt - 1 - t, l)) if reverse else (lambda l, t: (t, l))
    blk = pl.BlockSpec((tb, SCAN_LANES), tmap)
    tab = pl.BlockSpec((SCAN_ROWS, SCAN_LANES), lambda l, t: (0, l))
    return _pcall(
        body,
        name=name,
        grid=(nl, nt),
        in_specs=[blk, blk, tab, tab],
        out_specs=[blk, blk],
        out_shape=[jax.ShapeDtypeStruct((s, n), F32), jax.ShapeDtypeStruct((s, n), F32)],
        scratch_shapes=[pltpu.VMEM(shape, F32), pltpu.VMEM(shape, F32)],
        compiler_params=_params(("parallel", "arbitrary")),
    )(ur, ui, pr, pi)


@functools.partial(jax.custom_vjp, nondiff_argnums=(4,))
def s5_scan(ur, ui, lr, li, name):
    return tuple(_scan_call(ur, ui, lr, li, False, name + "_f"))


def _s5_scan_fwd(ur, ui, lr, li, name):
    xr, xi = _scan_call(ur, ui, lr, li, False, name + "_f")
    return (xr, xi), (xr, xi, lr, li)


def _s5_scan_bwd(name, res, g):
    xr, xi, lr, li = res
    gr, gi = _scan_call(g[0], g[1], lr, -li, True, name + "_b")
    dlr = jnp.sum(xr[:-1] * gr[1:] + xi[:-1] * gi[1:], axis=0, keepdims=True)
    dli = jnp.sum(xr[:-1] * gi[1:] - xi[:-1] * gr[1:], axis=0, keepdims=True)
    return gr, gi, dlr, dli


s5_scan.defvjp(_s5_scan_fwd, _s5_scan_bwd)


def _hg_chunk(q, k, v, lf, st):
    c = HG_CHUNK
    row = lax.broadcasted_iota(jnp.int32, (c, c), 0)
    col = lax.broadcasted_iota(jnp.int32, (c, c), 1)
    tri = (col <= row).astype(F32)
    b = jnp.dot(tri, lf, precision=lax.Precision.HIGHEST, preferred_element_type=F32)
    qb = q * jnp.exp(b)
    o = lax.dot_general(qb.astype(BF16), st.astype(BF16), (((1,), (1,)), ((), ())), preferred_element_type=F32)
    nsub = c // HG_SUB
    srow = lax.broadcasted_iota(jnp.int32, (HG_SUB, HG_SUB, 1), 0)
    scol = lax.broadcasted_iota(jnp.int32, (HG_SUB, HG_SUB, 1), 1)
    smask = scol <= srow
    outs = []
    for i in range(nsub):
        lo = i * HG_SUB
        bi, qi, ki = b[lo:lo + HG_SUB], q[lo:lo + HG_SUB], k[lo:lo + HG_SUB]
        diff = jnp.where(smask, bi[:, None, :] - bi[None, :, :], 0.0)
        e = jnp.where(smask, jnp.exp(diff), 0.0)
        a_ii = jnp.sum(qi[:, None, :] * ki[None, :, :] * e, axis=-1)
        oi = jnp.dot(a_ii.astype(BF16), v[lo:lo + HG_SUB].astype(BF16), preferred_element_type=F32)
        if i > 0:
            r = b[lo - 1:lo]
            qt = qi * jnp.exp(bi - r)
            kt = k[:lo] * jnp.exp(r - b[:lo])
            a_ij = lax.dot_general(qt.astype(BF16), kt.astype(BF16), (((1,), (1,)), ((), ())),
                                   preferred_element_type=F32)
            oi = oi + jnp.dot(a_ij.astype(BF16), v[:lo].astype(BF16), preferred_element_type=F32)
        outs.append(oi)
    o = o + jnp.concatenate(outs, axis=0)
    bl = b[c - 1:c]
    kd = k * jnp.exp(bl - b)
    st_new = st * jnp.exp(bl) + lax.dot_general(v.astype(BF16), kd.astype(BF16), (((0,), (0,)), ((), ())),
                                                preferred_element_type=F32)
    return o, st_new


def _hg_fwd_call(q, k, v, lf, name):
    s, w = q.shape
    nc = s // HG_CHUNK
    blk = pl.BlockSpec((HG_CHUNK, w), lambda i: (i, 0))

    def body(q_ref, k_ref, v_ref, lf_ref, o_ref, st_ref, st_sc):
        @pl.when(pl.program_id(0) == 0)
        def _():
            st_sc[...] = jnp.zeros(st_sc.shape, F32)

        for hh in range(HG_HEADS):
            cs = slice(hh * HG_DK, (hh + 1) * HG_DK)
            st = st_sc[hh]
            st_ref[0, hh] = st
            o, st_new = _hg_chunk(q_ref[:, cs], k_ref[:, cs], v_ref[:, cs], lf_ref[:, cs], st)
            o_ref[:, cs] = o
            st_sc[hh] = st_new

    return _pcall(
        body,
        name=name,
        grid=(nc,),
        in_specs=[blk, blk, blk, blk],
        out_specs=[blk, pl.BlockSpec((1, HG_HEADS, HG_DV, HG_DK), lambda i: (i, 0, 0, 0))],
        out_shape=[jax.ShapeDtypeStruct((s, w), F32), jax.ShapeDtypeStruct((nc, HG_HEADS, HG_DV, HG_DK), F32)],
        scratch_shapes=[pltpu.VMEM((HG_HEADS, HG_DV, HG_DK), F32)],
        compiler_params=_params(("arbitrary",)),
    )(q, k, v, lf)


def _hg_bwd_call(q, k, v, lf, sts, do, name):
    s, w = q.shape
    nc = s // HG_CHUNK
    blk = pl.BlockSpec((HG_CHUNK, w), lambda i: (nc - 1 - i, 0))

    def body(q_ref, k_ref, v_ref, lf_ref, st_ref, do_ref, dq_ref, dk_ref, dv_ref, dlf_ref, dst_sc):
        @pl.when(pl.program_id(0) == 0)
        def _():
            dst_sc[...] = jnp.zeros(dst_sc.shape, F32)

        for hh in range(HG_HEADS):
            cs = slice(hh * HG_DK, (hh + 1) * HG_DK)
            _, vjp = jax.vjp(_hg_chunk, q_ref[:, cs], k_ref[:, cs], v_ref[:, cs], lf_ref[:, cs], st_ref[0, hh])
            dq, dk, dv, dlf, dst = vjp((do_ref[:, cs], dst_sc[hh]))
            dq_ref[:, cs] = dq
            dk_ref[:, cs] = dk
            dv_ref[:, cs] = dv
            dlf_ref[:, cs] = dlf
            dst_sc[hh] = dst

    out = jax.ShapeDtypeStruct((s, w), F32)
    return _pcall(
        body,
        name=name,
        grid=(nc,),
        in_specs=[blk, blk, blk, blk, pl.BlockSpec((1, HG_HEADS, HG_DV, HG_DK), lambda i: (nc - 1 - i, 0, 0, 0)), blk],
        out_specs=[blk, blk, blk, blk],
        out_shape=[out, out, out, out],
        scratch_shapes=[pltpu.VMEM((HG_HEADS, HG_DV, HG_DK), F32)],
        compiler_params=_params(("arbitrary",)),
    )(q, k, v, lf, sts, do)


@functools.partial(jax.custom_vjp, nondiff_argnums=(4,))
def hgrn_core(q, k, v, lf, name):
    return _hg_fwd_call(q, k, v, lf, name + "_f")[0]


def _hgrn_core_fwd(q, k, v, lf, name):
    o, sts = _hg_fwd_call(q, k, v, lf, name + "_f")
    return o, (q, k, v, lf, sts)


def _hgrn_core_bwd(name, res, do):
    q, k, v, lf, sts = res
    return tuple(_hg_bwd_call(q, k, v, lf, sts, do, name + "_b"))


hgrn_core.defvjp(_hgrn_core_fwd, _hgrn_core_bwd)


def _adamw_call(w, g, m, v, name):
    rows, cols = w.shape
    tr = _tile(rows, 256, 8)
    c1 = 1.0 / (1.0 - ADAM_B1 ** ADAM_STEP)
    c2 = 1.0 / (1.0 - ADAM_B2 ** ADAM_STEP)

    def body(w_ref, g_ref, m_ref, v_ref, d_ref, mo_ref, vo_ref):
        gv = g_ref[...]
        mn = ADAM_B1 * m_ref[...] + (1.0 - ADAM_B1) * gv
        vn = ADAM_B2 * v_ref[...] + (1.0 - ADAM_B2) * (gv * gv)
        d_ref[...] = -ADAM_LR * ((mn * c1) / (jnp.sqrt(vn * c2) + ADAM_EPS) + ADAM_WD * w_ref[...])
        mo_ref[...] = mn
        vo_ref[...] = vn

    blk = pl.BlockSpec((tr, cols), lambda i: (i, 0))
    out = jax.ShapeDtypeStruct((rows, cols), F32)
    return _pcall(
        body,
        name=name,
        grid=(rows // tr,),
        in_specs=[blk, blk, blk, blk],
        out_specs=[blk, blk, blk],
        out_shape=[out, out, out],
        compiler_params=_params(("parallel",)),
    )(w, g, m, v)


ANY = pl.BlockSpec(memory_space=pl.ANY)


def _place():
    return lax.axis_index("x"), lax.axis_index("y"), lax.axis_index("c")


def _other_chips(x, y):
    return [(1 - x, y), (x, 1 - y), (1 - x, 1 - y)]


def _gather_weights(wsh):
    r, wd = wsh.shape
    hh = r // 2

    def body(w_ref, g_ref, send_sems, recv_sems, local_sem):
        x, y, c = _place()
        myq = 2 * x + y
        sibling = (x, y, 1 - c)
        chips = _other_chips(x, y)
        mine = pl.ds(c * hh, hh)
        theirs = pl.ds((1 - c) * hh, hh)

        def copy(k, src, dst, to):
            return pltpu.make_async_remote_copy(src_ref=src, dst_ref=dst, send_sem=send_sems.at[k],
                                                recv_sem=recv_sems.at[k], device_id=to, device_id_type=MESH)

        own = pltpu.make_async_copy(w_ref, g_ref.at[myq], local_sem)
        own.start()
        first = [copy(j, w_ref.at[mine], g_ref.at[myq, mine], (px, py, c)) for j, (px, py) in enumerate(chips)]
        for cp in first:
            cp.start()
        passed = []
        for j, (px, py) in enumerate(chips):
            q = 2 * px + py
            copy(j, w_ref.at[mine], g_ref.at[q, mine], (px, py, c)).wait_recv()
            fwd = copy(3 + j, g_ref.at[q, mine], g_ref.at[q, mine], sibling)
            fwd.start()
            passed.append(fwd)
        for j, (px, py) in enumerate(chips):
            q = 2 * px + py
            copy(3 + j, g_ref.at[q, theirs], g_ref.at[q, theirs], sibling).wait_recv()
        for cp in first + passed:
            cp.wait_send()
        own.wait()

    return _pcall(
        body,
        name="gather_weights",
        in_specs=[ANY],
        out_specs=ANY,
        out_shape=jax.ShapeDtypeStruct((4, r, wd), wsh.dtype),
        scratch_shapes=[pltpu.SemaphoreType.DMA((6,)), pltpu.SemaphoreType.DMA((6,)), pltpu.SemaphoreType.DMA],
    )(wsh)


def _swap_halves_to_sibling(gr):
    nq, r, wd = gr.shape
    hh = r // 2

    def body(g_ref, t_ref, send_sem, recv_sem):
        x, y, c = _place()
        cp = pltpu.make_async_remote_copy(src_ref=g_ref.at[:, pl.ds((1 - c) * hh, hh)], dst_ref=t_ref,
                                          send_sem=send_sem, recv_sem=recv_sem, device_id=(x, y, 1 - c),
                                          device_id_type=MESH)
        cp.start()
        cp.wait()

    return _pcall(
        body,
        name="grad_d2d",
        in_specs=[ANY],
        out_specs=ANY,
        out_shape=jax.ShapeDtypeStruct((nq, hh, wd), gr.dtype),
        scratch_shapes=[pltpu.SemaphoreType.DMA, pltpu.SemaphoreType.DMA],
    )(gr)


def _add_half(gr, t1, place):
    nq, r, wd = gr.shape
    hh = r // 2
    tr = _tile(hh, 512, 16)
    nb = hh // tr

    def body(p_ref, g_ref, t_ref, o_ref):
        o_ref[...] = (g_ref[...] + t_ref[...]).astype(BF16)

    return _pcall(
        body,
        name="grad_add_half",
        grid_spec=pltpu.PrefetchScalarGridSpec(
            num_scalar_prefetch=1,
            grid=(nq, nb),
            in_specs=[
                pl.BlockSpec((1, tr, wd), lambda q, i, p_ref: (q, p_ref[0] * nb + i, 0)),
                pl.BlockSpec((1, tr, wd), lambda q, i, p_ref: (q, i, 0)),
            ],
            out_specs=pl.BlockSpec((1, tr, wd), lambda q, i, p_ref: (q, i, 0)),
        ),
        out_shape=jax.ShapeDtypeStruct((nq, hh, wd), BF16),
        compiler_params=_params(("parallel", "parallel")),
    )(place, gr, t1)


def _scatter_partials(p):
    nq, hh, wd = p.shape

    def body(p_ref, t_ref, send_sems, recv_sems):
        x, y, c = _place()
        cps = []
        for j, (px, py) in enumerate(_other_chips(x, y)):
            cp = pltpu.make_async_remote_copy(src_ref=p_ref.at[2 * px + py], dst_ref=t_ref.at[j],
                                              send_sem=send_sems.at[j], recv_sem=recv_sems.at[j],
                                              device_id=(px, py, c), device_id_type=MESH)
            cp.start()
            cps.append(cp)
        for cp in cps:
            cp.wait()

    return _pcall(
        body,
        name="grad_ici",
        in_specs=[ANY],
        out_specs=ANY,
        out_shape=jax.ShapeDtypeStruct((3, hh, wd), p.dtype),
        scratch_shapes=[pltpu.SemaphoreType.DMA((3,)), pltpu.SemaphoreType.DMA((3,))],
    )(p)


def _add_partials(gr, t1, t2, place):
    nq, r, wd = gr.shape
    hh = r // 2
    tr = _tile(hh, 512, 16)
    nb = hh // tr

    def body(p_ref, g_ref, s_ref, t_ref, o_ref):
        o_ref[...] = (((g_ref[0] + s_ref[0]) + t_ref[0].astype(F32)) + t_ref[1].astype(F32)) + t_ref[2].astype(F32)

    return _pcall(
        body,
        name="grad_add_partials",
        grid_spec=pltpu.PrefetchScalarGridSpec(
            num_scalar_prefetch=1,
            grid=(nb,),
            in_specs=[
                pl.BlockSpec((1, tr, wd), lambda i, p_ref: (p_ref[1], p_ref[0] * nb + i, 0)),
                pl.BlockSpec((1, tr, wd), lambda i, p_ref: (p_ref[1], i, 0)),
                pl.BlockSpec((3, tr, wd), lambda i, p_ref: (0, i, 0)),
            ],
            out_specs=pl.BlockSpec((tr, wd), lambda i, p_ref: (i, 0)),
        ),
        out_shape=jax.ShapeDtypeStruct((hh, wd), F32),
        compiler_params=_params(("parallel",)),
    )(place, gr, t1, t2)


def _join_halves(half):
    hh, wd = half.shape

    def body(h_ref, f_ref, send_sem, recv_sem, local_sem):
        x, y, c = _place()
        mine = pl.ds(c * hh, hh)
        own = pltpu.make_async_copy(h_ref, f_ref.at[mine], local_sem)
        own.start()
        cp = pltpu.make_async_remote_copy(src_ref=h_ref, dst_ref=f_ref.at[mine], send_sem=send_sem,
                                          recv_sem=recv_sem, device_id=(x, y, 1 - c), device_id_type=MESH)
        cp.start()
        cp.wait_send()
        pltpu.make_async_remote_copy(src_ref=h_ref, dst_ref=f_ref.at[pl.ds((1 - c) * hh, hh)], send_sem=send_sem,
                                     recv_sem=recv_sem, device_id=(x, y, 1 - c), device_id_type=MESH).wait_recv()
        own.wait()

    return _pcall(
        body,
        name="grad_join",
        in_specs=[ANY],
        out_specs=ANY,
        out_shape=jax.ShapeDtypeStruct((2 * hh, wd), half.dtype),
        scratch_shapes=[pltpu.SemaphoreType.DMA, pltpu.SemaphoreType.DMA, pltpu.SemaphoreType.DMA],
    )(half)


N_DEV = 8


def _allreduce_small(buf):
    rows, wd = buf.shape
    n = rows // N_DEV

    def body(x_ref, o_ref, stage_ref, send_sems, recv_sems):
        x, y, c = _place()
        me = 4 * x + 2 * y + c

        def peer(k):
            return (x ^ (k >> 2), y ^ ((k >> 1) & 1), c ^ (k & 1))

        def chunk(ref, d):
            return ref.at[pl.ds(pl.multiple_of(d * n, 8), n)]

        first = []
        for k in range(1, N_DEV):
            px, py, pc = peer(k)
            cp = pltpu.make_async_remote_copy(src_ref=chunk(x_ref, 4 * px + 2 * py + pc), dst_ref=stage_ref.at[k],
                                              send_sem=send_sems.at[k - 1], recv_sem=recv_sems.at[k - 1],
                                              device_id=(px, py, pc), device_id_type=MESH)
            cp.start()
            first.append(cp)
        acc = chunk(x_ref, me)[...]
        for k in range(1, N_DEV):
            first[k - 1].wait_recv()
            acc = acc + stage_ref[k]
        chunk(o_ref, me)[...] = acc
        second = []
        for k in range(1, N_DEV):
            cp = pltpu.make_async_remote_copy(src_ref=chunk(o_ref, me), dst_ref=chunk(o_ref, me),
                                              send_sem=send_sems.at[6 + k], recv_sem=recv_sems.at[6 + k],
                                              device_id=peer(k), device_id_type=MESH)
            cp.start()
            second.append(cp)
        for k in range(1, N_DEV):
            px, py, pc = peer(k)
            them = 4 * px + 2 * py + pc
            pltpu.make_async_remote_copy(src_ref=chunk(o_ref, them), dst_ref=chunk(o_ref, them),
                                         send_sem=send_sems.at[6 + k], recv_sem=recv_sems.at[6 + k],
                                         device_id=(px, py, pc), device_id_type=MESH).wait_recv()
        for cp in first + second:
            cp.wait_send()

    vmem = pl.BlockSpec(memory_space=pltpu.VMEM)
    return _pcall(
        body,
        name="allreduce_small",
        in_specs=[vmem],
        out_specs=vmem,
        out_shape=jax.ShapeDtypeStruct((rows, wd), F32),
        scratch_shapes=[pltpu.VMEM((N_DEV, n, wd), F32), pltpu.SemaphoreType.DMA((14,)),
                        pltpu.SemaphoreType.DMA((14,))],
    )(buf)


def _rope(t, cos, sin):
    half = t.shape[-1] // 2
    t1, t2 = t[..., :half], t[..., half:]
    return jnp.concatenate([t1 * cos - t2 * sin, t2 * cos + t1 * sin], axis=-1)


def _s5_mixer(u, sp, l, w16, toks):
    g, p_, h = SSM_GROUPS, SSM_STATE, SSM_GROUP_CH
    lam_re, lam_im = sp["ssm_lam_re", l], sp["ssm_lam_im", l]
    step = jnp.exp(sp["ssm_log_step", l])[:, None]
    mag, ang = jnp.exp(lam_re * step), lam_im * step
    lbr, lbi = mag * jnp.cos(ang), mag * jnp.sin(ang)
    den = lam_re * lam_re + lam_im * lam_im
    nr, ni = lbr - 1.0, lbi
    qr, qi = (nr * lam_re + ni * lam_im) / den, (ni * lam_re - nr * lam_im) / den
    b_re, b_im = sp["ssm_b_re", l], sp["ssm_b_im", l]
    bbr = qr[..., None] * b_re - qi[..., None] * b_im
    bbi = qr[..., None] * b_im + qi[..., None] * b_re
    eye = jnp.eye(g, dtype=F32)
    bmat_r = jnp.einsum("gph,gk->ghkp", bbr, eye).reshape(g * h, g * p_)
    bmat_i = jnp.einsum("gph,gk->ghkp", bbi, eye).reshape(g * h, g * p_)
    cmat_r = jnp.einsum("ghp,gk->gpkh", sp["ssm_c_re", l], eye).reshape(g * p_, g * h)
    cmat_i = jnp.einsum("ghp,gk->gpkh", -sp["ssm_c_im", l], eye).reshape(g * p_, g * h)
    nm = "ssm%d" % l
    ur = matmul(u, bmat_r, nm + "_bur")
    ui = matmul(u, bmat_i, nm + "_bui")
    xr, xi = s5_scan(ur, ui, lbr.reshape(1, g * p_), lbi.reshape(1, g * p_), nm + "_scan")
    y = matmul(xr, cmat_r, nm + "_cr") + matmul(xi, cmat_i, nm + "_ci") + sp["ssm_d", l].reshape(1, g * h) * u
    y = jax.nn.gelu(y)
    zz = matmul_w(y, w16["ssm_w_glu", l], toks["ssm_w_glu", l], nm + "_glu")
    return zz[:, :D_MODEL] * jax.nn.sigmoid(zz[:, D_MODEL:])


def _mla_mixer(q_lat, kv_lat, k_rope, cos, sin, sp, l, w16, toks):
    s = q_lat.shape[0]
    nm = "mla%d" % l
    dqk = MLA_NOPE + MLA_ROPE
    qt = matmul_w_t(rmsnorm(q_lat, sp["mla_q_norm", l], nm + "_qn"), w16["mla_w_uq", l], toks["mla_w_uq", l],
                    nm + "_uq").reshape(MLA_HEADS, dqk, s)
    half = MLA_ROPE // 2
    q1, q2 = qt[:, MLA_NOPE:MLA_NOPE + half], qt[:, MLA_NOPE + half:]
    ct, st = cos.T[None], sin.T[None]
    qt = jnp.concatenate([qt[:, :MLA_NOPE], q1 * ct - q2 * st, q2 * ct + q1 * st], axis=1).reshape(MLA_HEADS * dqk, s)
    kvt = matmul_w_t(rmsnorm(kv_lat, sp["mla_kv_norm", l], nm + "_kvn"), w16["mla_w_ukv", l], toks["mla_w_ukv", l],
                     nm + "_ukv").reshape(MLA_HEADS, MLA_NOPE + MLA_V, s)
    k_pe = _rope(k_rope, cos, sin)
    k = jnp.concatenate([kvt[:, :MLA_NOPE].transpose(0, 2, 1),
                         jnp.broadcast_to(k_pe[None], (MLA_HEADS, s, MLA_ROPE))], axis=-1)
    vt = kvt[:, MLA_NOPE:].reshape(MLA_HEADS * MLA_V, s)
    ot = attention(qt, k, vt, 1.0 / math.sqrt(dqk), True, nm + "_att")
    return matmul_w_at(ot, w16["mla_w_o", l], toks["mla_w_o", l], nm + "_o")


def _hgrn_mixer(q, f_logit, i_in, g, lb, sp, l, w16, toks):
    s = q.shape[0]
    nm = "hg%d" % l
    f = lb + (1.0 - lb) * jax.nn.sigmoid(f_logit)
    o = hgrn_core(jax.nn.silu(q), 1.0 - f, i_in, jnp.log(f), nm + "_core")
    o = rmsnorm(o.reshape(s * HG_HEADS, HG_DV), sp["hg_g_norm", l], nm + "_gn").reshape(s, HG_HEADS * HG_DV)
    return matmul_w(o * jax.nn.silu(g), w16["hg_w_o", l], toks["hg_w_o", l], nm + "_o")


def _cross_attention(h, mem_n, l, w16, toks):
    s, m = h.shape[0], mem_n.shape[0]
    nm = "xat%d" % l
    xw = X_HEADS * X_HEAD_DIM
    qt = matmul_w_t(h, w16["x_w_q", l], toks["x_w_q", l], nm + "_q")
    kv = matmul_w(mem_n, w16["x_w_kv", l], toks["x_w_kv", l], nm + "_kv")
    k = kv[:, :xw].reshape(m, X_HEADS, X_HEAD_DIM).transpose(1, 0, 2)
    ot = attention(qt, k, kv[:, xw:].T, 1.0 / math.sqrt(X_HEAD_DIM), False, nm + "_att")
    return matmul_w_at(ot, w16["x_w_o", l], toks["x_w_o", l], nm + "_o")


_IN_HG = SSM_WIDTH + MLA_Q_RANK + MLA_KV_RANK
_IN_GATE = _IN_HG + 4 * HG_HEADS * HG_DK
_IN_ROPE = _IN_GATE + 3 * D_MODEL


def _perm_in(w):
    pad = jnp.zeros(w.shape[:-1] + (D_IN_PAD - D_IN,), w.dtype)
    return jnp.concatenate([w[..., :_IN_HG], w[..., _IN_HG + MLA_ROPE:], w[..., _IN_HG:_IN_HG + MLA_ROPE], pad], axis=-1)


def _unperm_in(g):
    return jnp.concatenate([g[..., :_IN_HG], g[..., _IN_ROPE:_IN_ROPE + MLA_ROPE], g[..., _IN_HG:_IN_ROPE]], axis=-1)


def _layer(x, mem, cos, sin, lb, l, sp, w16, toks):
    s = x.shape[0]
    nm = "l%d" % l
    h = rmsnorm(x, sp["norm_mix", l], nm + "_nmix")
    z = matmul_w(h, w16["w_in", l], toks["w_in", l], nm + "_in")
    hw = HG_HEADS * HG_DK
    u, q_lat, kv_lat = z[:, :SSM_WIDTH], z[:, SSM_WIDTH:SSM_WIDTH + MLA_Q_RANK], z[:, SSM_WIDTH + MLA_Q_RANK:_IN_HG]
    hg = [z[:, _IN_HG + i * hw:_IN_HG + (i + 1) * hw] for i in range(4)]
    gates = jax.nn.sigmoid(z[:, _IN_GATE:_IN_ROPE])
    k_rope = z[:, _IN_ROPE:_IN_ROPE + MLA_ROPE]
    y_ssm = _s5_mixer(u, sp, l, w16, toks)
    y_mla = _mla_mixer(q_lat, kv_lat, k_rope, cos, sin, sp, l, w16, toks)
    y_hg = _hgrn_mixer(hg[0], hg[1], hg[2], hg[3], lb, sp, l, w16, toks)
    merged = (gates[:, :D_MODEL] * y_ssm + gates[:, D_MODEL:2 * D_MODEL] * y_mla + gates[:, 2 * D_MODEL:] * y_hg)
    x = x + matmul_w(merged, w16["w_out", l], toks["w_out", l], nm + "_out")
    hc = rmsnorm(x, sp["norm_cross", l], nm + "_ncross")
    mem_n = rmsnorm(mem, sp["norm_mem", l], nm + "_nmem")
    x = x + _cross_attention(hc, mem_n, l, w16, toks)
    hf = rmsnorm(x, sp["norm_ffn", l], nm + "_nffn")
    gu = matmul_w(hf, w16["ffn_w_gate_up", l], toks["ffn_w_gate_up", l], nm + "_gu")
    act = jax.nn.silu(gu[:, :D_FF]) * gu[:, D_FF:]
    return x + matmul_w(act, w16["ffn_w_down", l], toks["ffn_w_down", l], nm + "_down")


def _loss_fn(sp, toks, x, w16, mem, pos, target):
    half = MLA_ROPE // 2
    inv_freq = ROPE_THETA ** (-jnp.arange(half, dtype=F32) / half)
    ang = pos.astype(F32)[:, None] * inv_freq
    cos, sin = jnp.cos(ang), jnp.sin(ang)
    lb_p = jax.nn.softmax(sp["hg_lb", ALL], axis=0)
    lower = jnp.cumsum(lb_p, axis=0) - lb_p[0:1]
    for l in range(DEPTH):
        x = _layer(x, mem, cos, sin, lower[l], l, sp, w16, toks)
    y = rmsnorm(x, sp["norm_final", ALL], "nfinal")
    return 0.5 * jnp.sum(jnp.mean(jnp.square(y - target), axis=-1))


WEIGHTS = ["norm_mix", "w_in", "ssm_lam_re", "ssm_lam_im", "ssm_b_re", "ssm_b_im", "ssm_c_re", "ssm_c_im", "ssm_d",
           "ssm_log_step", "ssm_w_glu", "mla_q_norm", "mla_kv_norm", "mla_w_uq", "mla_w_ukv", "mla_w_o", "hg_lb",
           "hg_g_norm", "hg_w_o", "w_out", "norm_cross", "norm_mem", "x_w_q", "x_w_kv", "x_w_o", "norm_ffn",
           "ffn_w_gate_up", "ffn_w_down", "norm_final"]
INPUTS = ["x", "mem", "positions"] + WEIGHTS + ["loss_target"] + ["m_" + n for n in WEIGHTS] + ["v_" + n for n in WEIGHTS]
BIG = [("w_in", 2), ("ssm_w_glu", 2), ("mla_w_uq", 2), ("mla_w_ukv", 2), ("mla_w_o", 2), ("hg_w_o", 2), ("w_out", 1),
       ("x_w_q", 1), ("x_w_kv", 1), ("x_w_o", 2), ("ffn_w_gate_up", 2), ("ffn_w_down", 1)]
SMALL = [n for n in WEIGHTS if n not in dict(BIG)]
WHOLE = ("hg_lb", "norm_final")
ALL = -1
PACK_W = 1024
SMALL_W = 128
N_CHIPS = 4


def _pack(arrs, rows, width, dtype):
    flat = jnp.concatenate([a.astype(dtype).reshape(-1) for a in arrs])
    return jnp.pad(flat, (0, rows * width - flat.shape[0])).reshape(rows, width)


def _unpack(buf, like):
    flat = buf.reshape(-1)
    out, off = [], 0
    for a in like:
        n = math.prod(a.shape)
        out.append(flat[off:off + n].reshape(a.shape))
        off += n
    return out


def kernel(
    x, mem, positions, norm_mix, w_in, ssm_lam_re, ssm_lam_im, ssm_b_re, ssm_b_im, ssm_c_re, ssm_c_im, ssm_d,
    ssm_log_step, ssm_w_glu, mla_q_norm, mla_kv_norm, mla_w_uq, mla_w_ukv, mla_w_o, hg_lb, hg_g_norm, hg_w_o,
    w_out, norm_cross, norm_mem, x_w_q, x_w_kv, x_w_o, norm_ffn, ffn_w_gate_up, ffn_w_down, norm_final,
    loss_target, m_norm_mix, m_w_in, m_ssm_lam_re, m_ssm_lam_im, m_ssm_b_re, m_ssm_b_im, m_ssm_c_re, m_ssm_c_im,
    m_ssm_d, m_ssm_log_step, m_ssm_w_glu, m_mla_q_norm, m_mla_kv_norm, m_mla_w_uq, m_mla_w_ukv, m_mla_w_o,
    m_hg_lb, m_hg_g_norm, m_hg_w_o, m_w_out, m_norm_cross, m_norm_mem, m_x_w_q, m_x_w_kv, m_x_w_o, m_norm_ffn,
    m_ffn_w_gate_up, m_ffn_w_down, m_norm_final, v_norm_mix, v_w_in, v_ssm_lam_re, v_ssm_lam_im, v_ssm_b_re,
    v_ssm_b_im, v_ssm_c_re, v_ssm_c_im, v_ssm_d, v_ssm_log_step, v_ssm_w_glu, v_mla_q_norm, v_mla_kv_norm,
    v_mla_w_uq, v_mla_w_ukv, v_mla_w_o, v_hg_lb, v_hg_g_norm, v_hg_w_o, v_w_out, v_norm_cross, v_norm_mem,
    v_x_w_q, v_x_w_kv, v_x_w_o, v_norm_ffn, v_ffn_w_gate_up, v_ffn_w_down, v_norm_final):
    given = locals()
    a = {n: given[n] for n in INPUTS}
    x, mem, pos, target = a["x"][0], a["mem"][0], a["positions"][0], a["loss_target"][0]
    place = jnp.stack([lax.axis_index("c"), 2 * lax.axis_index("x") + lax.axis_index("y")]).astype(jnp.int32)

    n_big = sum(math.prod(a[n].shape) for n, _ in BIG)
    rows = -(-n_big // (PACK_W * 32)) * 32
    gathered = _gather_weights(_pack([a[n] for n, _ in BIG], rows, PACK_W, BF16)).reshape(N_CHIPS, rows * PACK_W)
    w16, toks, off = {}, {}, 0
    for n, ax in BIG:
        shp = a[n].shape[1:]
        per = math.prod(shp)
        for l in range(DEPTH):
            full = jnp.concatenate([gathered[q, off:off + per].reshape(shp) for q in range(N_CHIPS)], axis=ax - 1)
            if n == "w_in":
                full = _perm_in(full)
            w16[n, l] = full
            toks[n, l] = jnp.zeros(full.shape, F32)
            off += per
    sp = {}
    for n in SMALL:
        if n in WHOLE:
            sp[n, ALL] = a[n]
        else:
            for l in range(DEPTH):
                sp[n, l] = a[n][l]

    loss, (g_sp, g_tok, g_x) = jax.value_and_grad(_loss_fn, argnums=(0, 1, 2))(sp, toks, x, w16, mem, pos, target)
    g_small = {n: g_sp[n, ALL] if n in WHOLE else jnp.stack([g_sp[n, l] for l in range(DEPTH)]) for n in SMALL}

    blocks = []
    for q in range(N_CHIPS):
        parts = []
        for n, ax in BIG:
            for l in range(DEPTH):
                g = _unperm_in(g_tok[n, l]) if n == "w_in" else g_tok[n, l]
                parts.append(jnp.split(g, N_CHIPS, axis=ax - 1)[q])
        blocks.append(_pack(parts, rows, PACK_W, F32))
    grads = jnp.stack(blocks)
    from_sibling = _swap_halves_to_sibling(grads)
    from_chips = _scatter_partials(_add_half(grads, from_sibling, place))
    total = _add_partials(grads, from_sibling, from_chips, place)
    g_big = dict(zip([n for n, _ in BIG], _unpack(_join_halves(total), [a[n] for n, _ in BIG])))

    small_like = [a[n] for n in SMALL] + [jnp.zeros((1,), F32)]
    n_small = sum(math.prod(t.shape) for t in small_like)
    srows = -(-n_small // (SMALL_W * 8 * N_DEV)) * 8 * N_DEV
    reduced = _allreduce_small(_pack([g_small[n] for n in SMALL] + [loss.reshape(1)], srows, SMALL_W, F32))
    red = _unpack(reduced, small_like)
    g_sm = dict(zip(SMALL, red[:-1]))
    loss = red[-1].reshape(())

    out_g, out_d, out_m, out_v = {}, {}, {}, {}
    for n, _ in BIG:
        shp = a[n].shape
        two = lambda t: t.reshape(shp[0] * shp[1], shp[2])
        d, m, v = _adamw_call(two(a[n]), two(g_big[n]), two(a["m_" + n]), two(a["v_" + n]), "adamw_" + n)
        out_g[n], out_d[n], out_m[n], out_v[n] = g_big[n], d.reshape(shp), m.reshape(shp), v.reshape(shp)
    small_arrs = [a[n] for n in SMALL]
    d, m, v = _adamw_call(
        _pack(small_arrs, srows, SMALL_W, F32), reduced,
        _pack([a["m_" + n] for n in SMALL], srows, SMALL_W, F32),
        _pack([a["v_" + n] for n in SMALL], srows, SMALL_W, F32), "adamw_small")
    for n, gg, dd, mm_, vv in zip(SMALL, red[:-1], _unpack(d, small_arrs), _unpack(m, small_arrs), _unpack(v, small_arrs)):
        out_g[n], out_d[n], out_m[n], out_v[n] = gg, dd, mm_, vv

    return (loss, g_x[None], *[out_g[n] for n in WEIGHTS], *[out_d[n] for n in WEIGHTS],
            *[out_m[n] for n in WEIGHTS], *[out_v[n] for n in WEIGHTS])
```

```python
import functools
import math

import jax
import jax.numpy as jnp
from jax import lax
from jax.experimental import pallas as pl
from jax.experimental.pallas import tpu as pltpu

F32 = jnp.float32
BF16 = jnp.bfloat16
MESH = pl.DeviceIdType.MESH

D_MODEL = 1024
DEPTH = 2
RMS_EPS = 1e-6
SSM_GROUPS, SSM_GROUP_CH, SSM_STATE = 32, 16, 64
SSM_WIDTH = SSM_GROUPS * SSM_GROUP_CH
SSM_LANES = SSM_GROUPS * SSM_STATE
SSM_PACK = 8
MLA_HEADS, MLA_Q_RANK, MLA_KV_RANK, MLA_NOPE, MLA_ROPE, MLA_V = 8, 512, 256, 64, 32, 64
ROPE_THETA = 10000.0
HG_HEADS, HG_DK, HG_DV, HG_CHUNK, HG_SUB = 4, 128, 128, 64, 16
X_HEADS, X_HEAD_DIM = 4, 128
D_FF = 2816
D_IN = 6432
D_IN_PAD = 6656
MASK_VALUE = -1e30

ADAM_LR, ADAM_B1, ADAM_B2, ADAM_EPS, ADAM_WD, ADAM_STEP = 0.001, 0.9, 0.999, 1e-08, 0.01, 10

VMEM_LIMIT = 48 * 1024 * 1024


def _pcall(body, **kw):
    return pl.pallas_call(body, **kw)


def _params(sem):
    return pltpu.CompilerParams(dimension_semantics=sem, vmem_limit_bytes=VMEM_LIMIT)


def _tile(n, cap, align=128):
    if n <= cap:
        return n
    t = (cap // align) * align
    while t >= align:
        if n % t == 0:
            return t
        t -= align
    return n


def _mm(a, b, *, ta=False, tb=False, out_dtype=F32, name="mm"):
    if ta:
        K, M = a.shape
    else:
        M, K = a.shape
    if tb:
        N, Kb = b.shape
    else:
        Kb, N = b.shape
    assert K == Kb, (a.shape, b.shape, ta, tb)
    tm = _tile(M, 512)
    tn = _tile(N, 512)
    tk = _tile(K, 2048)
    nk = K // tk
    dims = (((0,) if ta else (1,), (1,) if tb else (0,)), ((), ()))

    def body(a_ref, b_ref, o_ref, *acc):
        av = a_ref[...].astype(BF16)
        bv = b_ref[...].astype(BF16)
        part = lax.dot_general(av, bv, dims, preferred_element_type=F32)
        if nk == 1:
            o_ref[...] = part.astype(o_ref.dtype)
        else:
            acc_ref = acc[0]
            k = pl.program_id(2)

            @pl.when(k == 0)
            def _():
                acc_ref[...] = part

            @pl.when(k > 0)
            def _():
                acc_ref[...] += part

            @pl.when(k == nk - 1)
            def _():
                o_ref[...] = acc_ref[...].astype(o_ref.dtype)

    a_spec = pl.BlockSpec((tk, tm), lambda i, j, k: (k, i)) if ta else pl.BlockSpec((tm, tk), lambda i, j, k: (i, k))
    b_spec = pl.BlockSpec((tn, tk), lambda i, j, k: (j, k)) if tb else pl.BlockSpec((tk, tn), lambda i, j, k: (k, j))
    return _pcall(
        body,
        name=name,
        grid=(M // tm, N // tn, nk),
        in_specs=[a_spec, b_spec],
        out_specs=pl.BlockSpec((tm, tn), lambda i, j, k: (i, j)),
        out_shape=jax.ShapeDtypeStruct((M, N), out_dtype),
        scratch_shapes=[pltpu.VMEM((tm, tn), F32)] if nk > 1 else [],
        compiler_params=_params(("parallel", "parallel", "arbitrary")),
    )(a, b)


def _bd_call(a, bc, transpose, name):
    m = a.shape[0]
    nb, ka, kb = bc.shape
    win, wout = (kb, ka) if transpose else (ka, kb)
    assert a.shape[1] == nb * win
    tm = _tile(m, 1024)
    dims = _NT if transpose else (((1,), (0,)), ((), ()))

    def body(a_ref, b_ref, o_ref):
        o_ref[...] = lax.dot_general(a_ref[...].astype(BF16), b_ref[0].astype(BF16), dims, preferred_element_type=F32)

    return _pcall(
        body,
        name=name,
        grid=(m // tm, nb),
        in_specs=[pl.BlockSpec((tm, win), lambda i, j: (i, j)), pl.BlockSpec((1, ka, kb), lambda i, j: (j, 0, 0))],
        out_specs=pl.BlockSpec((tm, wout), lambda i, j: (i, j)),
        out_shape=jax.ShapeDtypeStruct((m, nb * wout), F32),
        compiler_params=_params(("parallel", "parallel")),
    )(a, bc)


def _bd_dw_call(a, g, nb, name):
    m = a.shape[0]
    ka, kb = a.shape[1] // nb, g.shape[1] // nb
    tk = _tile(m, 1024)
    nk = m // tk

    def body(a_ref, g_ref, o_ref):
        part = lax.dot_general(a_ref[...].astype(BF16), g_ref[...].astype(BF16), _TN, preferred_element_type=F32)
        k = pl.program_id(1)

        @pl.when(k == 0)
        def _():
            o_ref[0] = part

        @pl.when(k > 0)
        def _():
            o_ref[0] += part

    return _pcall(
        body,
        name=name,
        grid=(nb, nk),
        in_specs=[pl.BlockSpec((tk, ka), lambda j, k: (k, j)), pl.BlockSpec((tk, kb), lambda j, k: (k, j))],
        out_specs=pl.BlockSpec((1, ka, kb), lambda j, k: (j, 0, 0)),
        out_shape=jax.ShapeDtypeStruct((nb, ka, kb), F32),
        compiler_params=_params(("parallel", "arbitrary")),
    )(a, g)


@functools.partial(jax.custom_vjp, nondiff_argnums=(2,))
def bd_matmul(a, bc, name):
    return _bd_call(a, bc, False, name + "_f")


def _bd_matmul_fwd(a, bc, name):
    return _bd_call(a, bc, False, name + "_f"), (a, bc)


def _bd_matmul_bwd(name, res, g):
    a, bc = res
    return _bd_call(g, bc, True, name + "_da"), _bd_dw_call(a, g, bc.shape[0], name + "_dw")


bd_matmul.defvjp(_bd_matmul_fwd, _bd_matmul_bwd)


@functools.partial(jax.custom_vjp, nondiff_argnums=(3,))
def matmul_w(a, w16, tok, name):
    return _mm(a, w16, name=name + "_f")


def _matmul_w_fwd(a, w16, tok, name):
    return _mm(a, w16, name=name + "_f"), (a, w16)


def _matmul_w_bwd(name, res, g):
    a, w16 = res
    return _mm(g, w16, tb=True, name=name + "_da"), jnp.zeros_like(w16), _mm(a, g, ta=True, name=name + "_dw")


matmul_w.defvjp(_matmul_w_fwd, _matmul_w_bwd)


@functools.partial(jax.custom_vjp, nondiff_argnums=(3,))
def matmul_w_t(a, w16, tok, name):
    return _mm(w16, a, ta=True, tb=True, name=name + "_f")


def _matmul_w_t_fwd(a, w16, tok, name):
    return _mm(w16, a, ta=True, tb=True, name=name + "_f"), (a, w16)


def _matmul_w_t_bwd(name, res, g):
    a, w16 = res
    return (_mm(g, w16, ta=True, tb=True, name=name + "_da"), jnp.zeros_like(w16),
            _mm(a, g, ta=True, tb=True, name=name + "_dw"))


matmul_w_t.defvjp(_matmul_w_t_fwd, _matmul_w_t_bwd)


@functools.partial(jax.custom_vjp, nondiff_argnums=(3,))
def matmul_w_at(at, w16, tok, name):
    return _mm(at, w16, ta=True, name=name + "_f")


def _matmul_w_at_fwd(at, w16, tok, name):
    return _mm(at, w16, ta=True, name=name + "_f"), (at, w16)


def _matmul_w_at_bwd(name, res, g):
    at, w16 = res
    return _mm(w16, g, tb=True, name=name + "_da"), jnp.zeros_like(w16), _mm(at, g, name=name + "_dw")


matmul_w_at.defvjp(_matmul_w_at_fwd, _matmul_w_at_bwd)


def _rms_fwd_call(x, g, name):
    rows, d = x.shape
    tr = _tile(rows, 512, 8)

    def body(x_ref, g_ref, y_ref):
        xv = x_ref[...]
        rstd = lax.rsqrt(jnp.mean(xv * xv, axis=-1, keepdims=True) + RMS_EPS)
        y_ref[...] = xv * rstd * g_ref[...]

    return _pcall(
        body,
        name=name,
        grid=(rows // tr,),
        in_specs=[pl.BlockSpec((tr, d), lambda i: (i, 0)), pl.BlockSpec((1, d), lambda i: (0, 0))],
        out_specs=pl.BlockSpec((tr, d), lambda i: (i, 0)),
        out_shape=jax.ShapeDtypeStruct((rows, d), F32),
        compiler_params=_params(("parallel",)),
    )(x, g.reshape(1, d))


def _rms_bwd_call(x, g, dy, name):
    rows, d = x.shape
    tr = _tile(rows, 512, 8)
    nb = rows // tr

    def body(x_ref, g_ref, dy_ref, dx_ref, dg_ref):
        xv = x_ref[...]
        rstd = lax.rsqrt(jnp.mean(xv * xv, axis=-1, keepdims=True) + RMS_EPS)
        xh = xv * rstd
        dyv = dy_ref[...]
        dyg = dyv * g_ref[...]
        dx_ref[...] = rstd * (dyg - xh * jnp.mean(dyg * xh, axis=-1, keepdims=True))
        dg_ref[...] = jnp.sum((dyv * xh).reshape(tr // 8, 8, d), axis=0)

    dx, dgp = _pcall(
        body,
        name=name,
        grid=(nb,),
        in_specs=[
            pl.BlockSpec((tr, d), lambda i: (i, 0)),
            pl.BlockSpec((1, d), lambda i: (0, 0)),
            pl.BlockSpec((tr, d), lambda i: (i, 0)),
        ],
        out_specs=[pl.BlockSpec((tr, d), lambda i: (i, 0)), pl.BlockSpec((8, d), lambda i: (i, 0))],
        out_shape=[jax.ShapeDtypeStruct((rows, d), F32), jax.ShapeDtypeStruct((nb * 8, d), F32)],
        compiler_params=_params(("parallel",)),
    )(x, g.reshape(1, d), dy)
    return dx, jnp.sum(dgp, axis=0)


@functools.partial(jax.custom_vjp, nondiff_argnums=(2,))
def rmsnorm(x, g, name):
    return _rms_fwd_call(x, g, name + "_f")


def _rmsnorm_fwd(x, g, name):
    return _rms_fwd_call(x, g, name + "_f"), (x, g)


def _rmsnorm_bwd(name, res, dy):
    x, g = res
    return _rms_bwd_call(x, g, dy, name + "_b")


rmsnorm.defvjp(_rmsnorm_fwd, _rmsnorm_bwd)


def _attn_blocks(sq, sk):
    return _tile(sq, 512), _tile(sk, 512)


def _attn_pt(qs, k_ref, qi, ki, bq, bk, masked, shift):
    st = jnp.dot(k_ref[0].astype(BF16), qs, preferred_element_type=F32)
    if masked:
        kpos = ki * bk + lax.broadcasted_iota(jnp.int32, (bk, bq), 0)
        qpos = qi * bq + lax.broadcasted_iota(jnp.int32, (bk, bq), 1)
        st = jnp.where(kpos <= qpos, st, MASK_VALUE)
    return st if shift is None else jnp.exp(st - shift)


def _attn_cases(causal, qi, ki, bq, bk, step):
    if not causal:
        step(False)
        return
    pl.when(ki * bk + bk - 1 <= qi * bq)(functools.partial(step, False))
    pl.when(jnp.logical_and(ki * bk + bk - 1 > qi * bq, ki * bk <= qi * bq + bq - 1))(functools.partial(step, True))


def _attn_fwd_call(qt, k, vt, scale, causal, name):
    h, sk, dq = k.shape
    sq = qt.shape[1]
    dv = vt.shape[0] // h
    bq, bk = _attn_blocks(sq, sk)
    nq, nk = sq // bq, sk // bk

    def body(q_ref, k_ref, v_ref, o_ref, lse_ref, m_sc, l_sc, acc_sc):
        qi, ki = pl.program_id(1), pl.program_id(2)

        @pl.when(ki == 0)
        def _():
            m_sc[...] = jnp.full(m_sc.shape, -jnp.inf, F32)
            l_sc[...] = jnp.zeros(l_sc.shape, F32)
            acc_sc[...] = jnp.zeros(acc_sc.shape, F32)

        def step(masked):
            qs = (q_ref[...] * scale).astype(BF16)
            st = _attn_pt(qs, k_ref, qi, ki, bq, bk, masked, None)
            m_prev = m_sc[...]
            m_new = jnp.maximum(m_prev, jnp.max(st, axis=0, keepdims=True))
            alpha = jnp.exp(m_prev - m_new)
            pt = jnp.exp(st - m_new)
            l_sc[...] = alpha * l_sc[...] + jnp.sum(pt, axis=0, keepdims=True)
            acc_sc[...] = alpha * acc_sc[...] + jnp.dot(v_ref[...].astype(BF16), pt.astype(BF16),
                                                        preferred_element_type=F32)
            m_sc[...] = m_new

        _attn_cases(causal, qi, ki, bq, bk, step)

        @pl.when(ki == nk - 1)
        def _():
            o_ref[...] = acc_sc[...] / l_sc[...]
            lse_ref[0] = m_sc[...] + jnp.log(l_sc[...])

    last = (lambda i, j: jnp.minimum(j, (i * bq + bq - 1) // bk)) if causal else (lambda i, j: j)
    qmap = lambda hh, i, j: (hh, i)
    return _pcall(
        body,
        name=name,
        grid=(h, nq, nk),
        in_specs=[
            pl.BlockSpec((dq, bq), qmap),
            pl.BlockSpec((1, bk, dq), lambda hh, i, j: (hh, last(i, j), 0)),
            pl.BlockSpec((dv, bk), lambda hh, i, j: (hh, last(i, j))),
        ],
        out_specs=[pl.BlockSpec((dv, bq), qmap), pl.BlockSpec((1, 1, bq), lambda hh, i, j: (hh, 0, i))],
        out_shape=[jax.ShapeDtypeStruct((h * dv, sq), F32), jax.ShapeDtypeStruct((h, 1, sq), F32)],
        scratch_shapes=[pltpu.VMEM((1, bq), F32), pltpu.VMEM((1, bq), F32), pltpu.VMEM((dv, bq), F32)],
        compiler_params=_params(("parallel", "parallel", "arbitrary")),
    )(qt, k, vt)


_TN = (((0,), (0,)), ((), ()))
_NT = (((1,), (1,)), ((), ()))


def _attn_dst(qs, k_ref, v_ref, do_ref, lse_ref, dl_ref, qi, ki, bq, bk, masked):
    pt = _attn_pt(qs, k_ref, qi, ki, bq, bk, masked, lse_ref[0])
    dpt = lax.dot_general(v_ref[...].astype(BF16), do_ref[...].astype(BF16), _TN, preferred_element_type=F32)
    return pt, pt * (dpt - dl_ref[0])


def _attn_dq_call(qt, k, vt, dot, lse, delta, scale, causal, name):
    h, sk, dq = k.shape
    sq = qt.shape[1]
    dv = vt.shape[0] // h
    bq, bk = _attn_blocks(sq, sk)
    nq, nk = sq // bq, sk // bk

    def body(q_ref, k_ref, v_ref, do_ref, lse_ref, dl_ref, dq_ref, acc_sc):
        qi, ki = pl.program_id(1), pl.program_id(2)

        @pl.when(ki == 0)
        def _():
            acc_sc[...] = jnp.zeros(acc_sc.shape, F32)

        def step(masked):
            qs = (q_ref[...] * scale).astype(BF16)
            _, dst = _attn_dst(qs, k_ref, v_ref, do_ref, lse_ref, dl_ref, qi, ki, bq, bk, masked)
            acc_sc[...] += lax.dot_general(k_ref[0].astype(BF16), dst.astype(BF16), _TN, preferred_element_type=F32)

        _attn_cases(causal, qi, ki, bq, bk, step)

        @pl.when(ki == nk - 1)
        def _():
            dq_ref[...] = acc_sc[...] * scale

    last = (lambda i, j: jnp.minimum(j, (i * bq + bq - 1) // bk)) if causal else (lambda i, j: j)
    qmap = lambda hh, i, j: (hh, i)
    rowmap = lambda hh, i, j: (hh, 0, i)
    return _pcall(
        body,
        name=name,
        grid=(h, nq, nk),
        in_specs=[
            pl.BlockSpec((dq, bq), qmap),
            pl.BlockSpec((1, bk, dq), lambda hh, i, j: (hh, last(i, j), 0)),
            pl.BlockSpec((dv, bk), lambda hh, i, j: (hh, last(i, j))),
            pl.BlockSpec((dv, bq), qmap),
            pl.BlockSpec((1, 1, bq), rowmap),
            pl.BlockSpec((1, 1, bq), rowmap),
        ],
        out_specs=pl.BlockSpec((dq, bq), qmap),
        out_shape=jax.ShapeDtypeStruct((h * dq, sq), F32),
        scratch_shapes=[pltpu.VMEM((dq, bq), F32)],
        compiler_params=_params(("parallel", "parallel", "arbitrary")),
    )(qt, k, vt, dot, lse, delta)


def _attn_dkv_call(qt, k, vt, dot, lse, delta, scale, causal, name):
    h, sk, dq = k.shape
    sq = qt.shape[1]
    dv = vt.shape[0] // h
    bq, bk = _attn_blocks(sq, sk)
    nq, nk = sq // bq, sk // bk

    def body(q_ref, k_ref, v_ref, do_ref, lse_ref, dl_ref, dk_ref, dv_ref, dk_sc, dv_sc):
        ki, qi = pl.program_id(1), pl.program_id(2)

        @pl.when(qi == 0)
        def _():
            dk_sc[...] = jnp.zeros(dk_sc.shape, F32)
            dv_sc[...] = jnp.zeros(dv_sc.shape, F32)

        def step(masked):
            qs = (q_ref[...] * scale).astype(BF16)
            pt, dst = _attn_dst(qs, k_ref, v_ref, do_ref, lse_ref, dl_ref, qi, ki, bq, bk, masked)
            dv_sc[...] += lax.dot_general(do_ref[...].astype(BF16), pt.astype(BF16), _NT, preferred_element_type=F32)
            dk_sc[...] += lax.dot_general(dst.astype(BF16), qs, _NT, preferred_element_type=F32)

        _attn_cases(causal, qi, ki, bq, bk, step)

        @pl.when(qi == nq - 1)
        def _():
            dk_ref[0] = dk_sc[...]
            dv_ref[...] = dv_sc[...]

    first = (lambda j, i: jnp.maximum(i, (j * bk) // bq)) if causal else (lambda j, i: i)
    qmap = lambda hh, j, i: (hh, first(j, i))
    rowmap = lambda hh, j, i: (hh, 0, first(j, i))
    return _pcall(
        body,
        name=name,
        grid=(h, nk, nq),
        in_specs=[
            pl.BlockSpec((dq, bq), qmap),
            pl.BlockSpec((1, bk, dq), lambda hh, j, i: (hh, j, 0)),
            pl.BlockSpec((dv, bk), lambda hh, j, i: (hh, j)),
            pl.BlockSpec((dv, bq), qmap),
            pl.BlockSpec((1, 1, bq), rowmap),
            pl.BlockSpec((1, 1, bq), rowmap),
        ],
        out_specs=[pl.BlockSpec((1, bk, dq), lambda hh, j, i: (hh, j, 0)), pl.BlockSpec((dv, bk), lambda hh, j, i: (hh, j))],
        out_shape=[jax.ShapeDtypeStruct((h, sk, dq), F32), jax.ShapeDtypeStruct((h * dv, sk), F32)],
        scratch_shapes=[pltpu.VMEM((bk, dq), F32), pltpu.VMEM((dv, bk), F32)],
        compiler_params=_params(("parallel", "parallel", "arbitrary")),
    )(qt, k, vt, dot, lse, delta)


@functools.partial(jax.custom_vjp, nondiff_argnums=(3, 4, 5))
def attention(qt, k, vt, scale, causal, name):
    return _attn_fwd_call(qt, k, vt, scale, causal, name + "_f")[0]


def _attention_fwd(qt, k, vt, scale, causal, name):
    ot, lse = _attn_fwd_call(qt, k, vt, scale, causal, name + "_f")
    return ot, (qt, k, vt, ot, lse)


def _attention_bwd(scale, causal, name, res, dot):
    qt, k, vt, ot, lse = res
    h = k.shape[0]
    delta = jnp.sum((dot * ot).reshape(h, ot.shape[0] // h, ot.shape[1]), axis=1, keepdims=True)
    dqt = _attn_dq_call(qt, k, vt, dot, lse, delta, scale, causal, name + "_dq")
    dk, dvt = _attn_dkv_call(qt, k, vt, dot, lse, delta, scale, causal, name + "_dkv")
    return dqt, dk, dvt


attention.defvjp(_attention_fwd, _attention_bwd)


SCAN_ROWS = 8
SCAN_LANES = 256
SCAN_TBLOCK = 512


def _scan_tables(lr, li):
    pr, pi = [lr], [li]
    for _ in range(SCAN_ROWS - 1):
        pr, pi = pr + [pr[-1] * lr - pi[-1] * li], pi + [pr[-1] * li + pi[-1] * lr]
    return jnp.concatenate(pr, axis=0), jnp.concatenate(pi, axis=0)


def _scan_call(ur, ui, lr, li, reverse, name, states=None):
    s, n = ur.shape
    tb = _tile(s, SCAN_TBLOCK, 8)
    nt, nl = s // tb, n // SCAN_LANES
    ntile = tb // SCAN_ROWS
    pr, pi = _scan_tables(lr, li)
    if reverse:
        pr, pi = pr[::-1], pi[::-1]
    shape = (SCAN_ROWS, SCAN_LANES)
    with_states = states is not None
    assert reverse or not with_states

    def body(*refs):
        if with_states:
            ur_ref, ui_ref, pr_ref, pi_ref, sr_ref, si_ref, xr_ref, xi_ref, dr_ref, di_ref, cr_sc, ci_sc, ar_sc, ai_sc = refs
        else:
            ur_ref, ui_ref, pr_ref, pi_ref, xr_ref, xi_ref, cr_sc, ci_sc, ar_sc, ai_sc = refs

        @pl.when(pl.program_id(1) == 0)
        def _():
            for sc in (cr_sc, ci_sc, ar_sc, ai_sc):
                sc[...] = jnp.zeros(shape, F32)

        prv, piv = pr_ref[...], pi_ref[...]
        rows = lax.broadcasted_iota(jnp.int32, shape, 0)

        def powers(k):
            r = (SCAN_ROWS - k) if reverse else (k - 1)
            return jnp.broadcast_to(prv[r:r + 1], shape), jnp.broadcast_to(piv[r:r + 1], shape)

        pw = [powers(k) for k in (1, 2, 4)]

        def tile(i, carry):
            cr, ci, acr, aci = carry
            j = (ntile - 1 - i) if reverse else i
            sl = pl.ds(pl.multiple_of(j * SCAN_ROWS, SCAN_ROWS), SCAN_ROWS)
            xr, xi = ur_ref[sl, :], ui_ref[sl, :]
            for (ar, ai), k in zip(pw, (1, 2, 4)):
                if reverse:
                    keep = rows < SCAN_ROWS - k
                    sr = jnp.where(keep, pltpu.roll(xr, SCAN_ROWS - k, 0), 0.0)
                    si = jnp.where(keep, pltpu.roll(xi, SCAN_ROWS - k, 0), 0.0)
                else:
                    keep = rows >= k
                    sr = jnp.where(keep, pltpu.roll(xr, k, 0), 0.0)
                    si = jnp.where(keep, pltpu.roll(xi, k, 0), 0.0)
                xr, xi = xr + (ar * sr - ai * si), xi + (ar * si + ai * sr)
            xr, xi = xr + (prv * cr - piv * ci), xi + (prv * ci + piv * cr)
            xr_ref[sl, :] = xr
            xi_ref[sl, :] = xi
            if with_states:
                last = rows == SCAN_ROWS - 1
                nr = jnp.where(last, cr, pltpu.roll(xr, SCAN_ROWS - 1, 0))
                ni = jnp.where(last, ci, pltpu.roll(xi, SCAN_ROWS - 1, 0))
                sr, si = sr_ref[sl, :], si_ref[sl, :]
                acr, aci = acr + (sr * nr + si * ni), aci + (sr * ni - si * nr)
            e = 0 if reverse else SCAN_ROWS - 1
            return jnp.broadcast_to(xr[e:e + 1], shape), jnp.broadcast_to(xi[e:e + 1], shape), acr, aci

        cr, ci, acr, aci = lax.fori_loop(0, ntile, tile, (cr_sc[...], ci_sc[...], ar_sc[...], ai_sc[...]))
        cr_sc[...] = cr
        ci_sc[...] = ci
        if with_states:
            ar_sc[...] = acr
            ai_sc[...] = aci

            @pl.when(pl.program_id(1) == nt - 1)
            def _():
                dr_ref[...] = acr
                di_ref[...] = aci

    tmap = (lambda l, t: (nt - 1 - t, l)) if reverse else (lambda l, t: (t, l))
    blk = pl.BlockSpec((tb, SCAN_LANES), tmap)
    tab = pl.BlockSpec((SCAN_ROWS, SCAN_LANES), lambda l, t: (0, l))
    full = jax.ShapeDtypeStruct((s, n), F32)
    return _pcall(
        body,
        name=name,
        grid=(nl, nt),
        in_specs=[blk, blk, tab, tab] + ([blk, blk] if with_states else []),
        out_specs=[blk, blk] + ([tab, tab] if with_states else []),
        out_shape=[full, full] + ([jax.ShapeDtypeStruct((SCAN_ROWS, n), F32)] * 2 if with_states else []),
        scratch_shapes=[pltpu.VMEM(shape, F32)] * 4,
        compiler_params=_params(("parallel", "arbitrary")),
    )(ur, ui, pr, pi, *(states if with_states else ()))


@functools.partial(jax.custom_vjp, nondiff_argnums=(4,))
def s5_scan(ur, ui, lr, li, name):
    return tuple(_scan_call(ur, ui, lr, li, False, name + "_f"))


def _s5_scan_fwd(ur, ui, lr, li, name):
    xr, xi = _scan_call(ur, ui, lr, li, False, name + "_f")
    return (xr, xi), (xr, xi, lr, li)


def _s5_scan_bwd(name, res, g):
    xr, xi, lr, li = res
    gr, gi, dlr, dli = _scan_call(g[0], g[1], lr, -li, True, name + "_b", states=(xr, xi))
    return gr, gi, jnp.sum(dlr, axis=0, keepdims=True), jnp.sum(dli, axis=0, keepdims=True)


s5_scan.defvjp(_s5_scan_fwd, _s5_scan_bwd)


def _hg_chunk(q, k, v, lf, st):
    c = HG_CHUNK
    row = lax.broadcasted_iota(jnp.int32, (c, c), 0)
    col = lax.broadcasted_iota(jnp.int32, (c, c), 1)
    tri = (col <= row).astype(F32)
    b = jnp.dot(tri, lf, precision=lax.Precision.HIGHEST, preferred_element_type=F32)
    qb = q * jnp.exp(b)
    o = lax.dot_general(qb.astype(BF16), st.astype(BF16), (((1,), (1,)), ((), ())), preferred_element_type=F32)
    nsub = c // HG_SUB
    srow = lax.broadcasted_iota(jnp.int32, (HG_SUB, HG_SUB, 1), 0)
    scol = lax.broadcasted_iota(jnp.int32, (HG_SUB, HG_SUB, 1), 1)
    smask = scol <= srow
    outs = []
    for i in range(nsub):
        lo = i * HG_SUB
        bi, qi, ki = b[lo:lo + HG_SUB], q[lo:lo + HG_SUB], k[lo:lo + HG_SUB]
        diff = jnp.where(smask, bi[:, None, :] - bi[None, :, :], 0.0)
        e = jnp.where(smask, jnp.exp(diff), 0.0)
        a_ii = jnp.sum(qi[:, None, :] * ki[None, :, :] * e, axis=-1)
        oi = jnp.dot(a_ii.astype(BF16), v[lo:lo + HG_SUB].astype(BF16), preferred_element_type=F32)
        if i > 0:
            r = b[lo - 1:lo]
            qt = qi * jnp.exp(bi - r)
            kt = k[:lo] * jnp.exp(r - b[:lo])
            a_ij = lax.dot_general(qt.astype(BF16), kt.astype(BF16), (((1,), (1,)), ((), ())),
                                   preferred_element_type=F32)
            oi = oi + jnp.dot(a_ij.astype(BF16), v[:lo].astype(BF16), preferred_element_type=F32)
        outs.append(oi)
    o = o + jnp.concatenate(outs, axis=0)
    bl = b[c - 1:c]
    kd = k * jnp.exp(bl - b)
    st_new = st * jnp.exp(bl) + lax.dot_general(v.astype(BF16), kd.astype(BF16), (((0,), (0,)), ((), ())),
                                                preferred_element_type=F32)
    return o, st_new


def _hg_fwd_call(q, k, v, lf, name):
    s, w = q.shape
    nc = s // HG_CHUNK
    blk = pl.BlockSpec((HG_CHUNK, w), lambda i: (i, 0))

    def body(q_ref, k_ref, v_ref, lf_ref, o_ref, st_ref, st_sc):
        @pl.when(pl.program_id(0) == 0)
        def _():
            st_sc[...] = jnp.zeros(st_sc.shape, F32)

        for hh in range(HG_HEADS):
            cs = slice(hh * HG_DK, (hh + 1) * HG_DK)
            st = st_sc[hh]
            st_ref[0, hh] = st
            o, st_new = _hg_chunk(q_ref[:, cs], k_ref[:, cs], v_ref[:, cs], lf_ref[:, cs], st)
            o_ref[:, cs] = o
            st_sc[hh] = st_new

    return _pcall(
        body,
        name=name,
        grid=(nc,),
        in_specs=[blk, blk, blk, blk],
        out_specs=[blk, pl.BlockSpec((1, HG_HEADS, HG_DV, HG_DK), lambda i: (i, 0, 0, 0))],
        out_shape=[jax.ShapeDtypeStruct((s, w), F32), jax.ShapeDtypeStruct((nc, HG_HEADS, HG_DV, HG_DK), F32)],
        scratch_shapes=[pltpu.VMEM((HG_HEADS, HG_DV, HG_DK), F32)],
        compiler_params=_params(("arbitrary",)),
    )(q, k, v, lf)


def _hg_bwd_call(q, k, v, lf, sts, do, name):
    s, w = q.shape
    nc = s // HG_CHUNK
    blk = pl.BlockSpec((HG_CHUNK, w), lambda i: (nc - 1 - i, 0))

    def body(q_ref, k_ref, v_ref, lf_ref, st_ref, do_ref, dq_ref, dk_ref, dv_ref, dlf_ref, dst_sc):
        @pl.when(pl.program_id(0) == 0)
        def _():
            dst_sc[...] = jnp.zeros(dst_sc.shape, F32)

        for hh in range(HG_HEADS):
            cs = slice(hh * HG_DK, (hh + 1) * HG_DK)
            _, vjp = jax.vjp(_hg_chunk, q_ref[:, cs], k_ref[:, cs], v_ref[:, cs], lf_ref[:, cs], st_ref[0, hh])
            dq, dk, dv, dlf, dst = vjp((do_ref[:, cs], dst_sc[hh]))
            dq_ref[:, cs] = dq
            dk_ref[:, cs] = dk
            dv_ref[:, cs] = dv
            dlf_ref[:, cs] = dlf
            dst_sc[hh] = dst

    out = jax.ShapeDtypeStruct((s, w), F32)
    return _pcall(
        body,
        name=name,
        grid=(nc,),
        in_specs=[blk, blk, blk, blk, pl.BlockSpec((1, HG_HEADS, HG_DV, HG_DK), lambda i: (nc - 1 - i, 0, 0, 0)), blk],
        out_specs=[blk, blk, blk, blk],
        out_shape=[out, out, out, out],
        scratch_shapes=[pltpu.VMEM((HG_HEADS, HG_DV, HG_DK), F32)],
        compiler_params=_params(("arbitrary",)),
    )(q, k, v, lf, sts, do)


@functools.partial(jax.custom_vjp, nondiff_argnums=(4,))
def hgrn_core(q, k, v, lf, name):
    return _hg_fwd_call(q, k, v, lf, name + "_f")[0]


def _hgrn_core_fwd(q, k, v, lf, name):
    o, sts = _hg_fwd_call(q, k, v, lf, name + "_f")
    return o, (q, k, v, lf, sts)


def _hgrn_core_bwd(name, res, do):
    q, k, v, lf, sts = res
    return tuple(_hg_bwd_call(q, k, v, lf, sts, do, name + "_b"))


hgrn_core.defvjp(_hgrn_core_fwd, _hgrn_core_bwd)


def _adamw_call(w, g, m, v, name):
    rows, cols = w.shape
    tr = _tile(rows, 256, 8)
    c1 = 1.0 / (1.0 - ADAM_B1 ** ADAM_STEP)
    c2 = 1.0 / (1.0 - ADAM_B2 ** ADAM_STEP)

    def body(w_ref, g_ref, m_ref, v_ref, d_ref, mo_ref, vo_ref):
        gv = g_ref[...]
        mn = ADAM_B1 * m_ref[...] + (1.0 - ADAM_B1) * gv
        vn = ADAM_B2 * v_ref[...] + (1.0 - ADAM_B2) * (gv * gv)
        d_ref[...] = -ADAM_LR * ((mn * c1) / (jnp.sqrt(vn * c2) + ADAM_EPS) + ADAM_WD * w_ref[...])
        mo_ref[...] = mn
        vo_ref[...] = vn

    blk = pl.BlockSpec((tr, cols), lambda i: (i, 0))
    out = jax.ShapeDtypeStruct((rows, cols), F32)
    return _pcall(
        body,
        name=name,
        grid=(rows // tr,),
        in_specs=[blk, blk, blk, blk],
        out_specs=[blk, blk, blk],
        out_shape=[out, out, out],
        compiler_params=_params(("parallel",)),
    )(w, g, m, v)


ANY = pl.BlockSpec(memory_space=pl.ANY)


def _place():
    return lax.axis_index("x"), lax.axis_index("y"), lax.axis_index("c")


def _other_chips(x, y):
    return [(1 - x, y), (x, 1 - y), (1 - x, 1 - y)]


def _gather_weights(wsh):
    r, wd = wsh.shape
    hh = r // 2

    def body(w_ref, g_ref, send_sems, recv_sems):
        x, y, c = _place()
        myq = 2 * x + y
        sibling = (x, y, 1 - c)
        chips = _other_chips(x, y)
        mine = pl.ds(c * hh, hh)
        theirs = pl.ds((1 - c) * hh, hh)

        def copy(k, src, dst, to):
            return pltpu.make_async_remote_copy(src_ref=src, dst_ref=dst, send_sem=send_sems.at[k],
                                                recv_sem=recv_sems.at[k], device_id=to, device_id_type=MESH)

        first = [copy(j, w_ref.at[mine], g_ref.at[myq, mine], (px, py, c)) for j, (px, py) in enumerate(chips)]
        for cp in first:
            cp.start()
        passed = []
        for j, (px, py) in enumerate(chips):
            q = 2 * px + py
            copy(j, w_ref.at[mine], g_ref.at[q, mine], (px, py, c)).wait_recv()
            fwd = copy(3 + j, g_ref.at[q, mine], g_ref.at[q, mine], sibling)
            fwd.start()
            passed.append(fwd)
        for j, (px, py) in enumerate(chips):
            q = 2 * px + py
            copy(3 + j, g_ref.at[q, theirs], g_ref.at[q, theirs], sibling).wait_recv()
        for cp in first + passed:
            cp.wait_send()

    got = _pcall(
        body,
        name="gather_weights",
        in_specs=[ANY],
        out_specs=ANY,
        out_shape=jax.ShapeDtypeStruct((4, r, wd), wsh.dtype),
        scratch_shapes=[pltpu.SemaphoreType.DMA((6,)), pltpu.SemaphoreType.DMA((6,))],
    )(wsh)
    my_chip = 2 * lax.axis_index("x") + lax.axis_index("y")
    return lax.dynamic_update_slice(got, wsh[None], (my_chip, 0, 0))


def _swap_halves_to_sibling(gr):
    nq, r, wd = gr.shape
    hh = r // 2

    def body(g_ref, t_ref, send_sem, recv_sem):
        x, y, c = _place()
        cp = pltpu.make_async_remote_copy(src_ref=g_ref.at[:, pl.ds((1 - c) * hh, hh)], dst_ref=t_ref,
                                          send_sem=send_sem, recv_sem=recv_sem, device_id=(x, y, 1 - c),
                                          device_id_type=MESH)
        cp.start()
        cp.wait()

    return _pcall(
        body,
        name="grad_d2d",
        in_specs=[ANY],
        out_specs=ANY,
        out_shape=jax.ShapeDtypeStruct((nq, hh, wd), gr.dtype),
        scratch_shapes=[pltpu.SemaphoreType.DMA, pltpu.SemaphoreType.DMA],
    )(gr)


def _add_half(gr, t1, place):
    nq, r, wd = gr.shape
    hh = r // 2
    tr = _tile(hh, 512, 16)
    nb = hh // tr

    def body(p_ref, g_ref, t_ref, o_ref):
        o_ref[...] = (g_ref[...] + t_ref[...]).astype(BF16)

    return _pcall(
        body,
        name="grad_add_half",
        grid_spec=pltpu.PrefetchScalarGridSpec(
            num_scalar_prefetch=1,
            grid=(nq, nb),
            in_specs=[
                pl.BlockSpec((1, tr, wd), lambda q, i, p_ref: (q, p_ref[0] * nb + i, 0)),
                pl.BlockSpec((1, tr, wd), lambda q, i, p_ref: (q, i, 0)),
            ],
            out_specs=pl.BlockSpec((1, tr, wd), lambda q, i, p_ref: (q, i, 0)),
        ),
        out_shape=jax.ShapeDtypeStruct((nq, hh, wd), BF16),
        compiler_params=_params(("parallel", "parallel")),
    )(place, gr, t1)


def _scatter_partials(p):
    nq, hh, wd = p.shape

    def body(p_ref, t_ref, send_sems, recv_sems):
        x, y, c = _place()
        cps = []
        for j, (px, py) in enumerate(_other_chips(x, y)):
            cp = pltpu.make_async_remote_copy(src_ref=p_ref.at[2 * px + py], dst_ref=t_ref.at[j],
                                              send_sem=send_sems.at[j], recv_sem=recv_sems.at[j],
                                              device_id=(px, py, c), device_id_type=MESH)
            cp.start()
            cps.append(cp)
        for cp in cps:
            cp.wait()

    return _pcall(
        body,
        name="grad_ici",
        in_specs=[ANY],
        out_specs=ANY,
        out_shape=jax.ShapeDtypeStruct((3, hh, wd), p.dtype),
        scratch_shapes=[pltpu.SemaphoreType.DMA((3,)), pltpu.SemaphoreType.DMA((3,))],
    )(p)


def _add_partials(gr, t1, t2, place):
    nq, r, wd = gr.shape
    hh = r // 2
    tr = _tile(hh, 512, 16)
    nb = hh // tr

    def body(p_ref, g_ref, s_ref, t_ref, o_ref):
        o_ref[...] = (((g_ref[0] + s_ref[0]) + t_ref[0].astype(F32)) + t_ref[1].astype(F32)) + t_ref[2].astype(F32)

    return _pcall(
        body,
        name="grad_add_partials",
        grid_spec=pltpu.PrefetchScalarGridSpec(
            num_scalar_prefetch=1,
            grid=(nb,),
            in_specs=[
                pl.BlockSpec((1, tr, wd), lambda i, p_ref: (p_ref[1], p_ref[0] * nb + i, 0)),
                pl.BlockSpec((1, tr, wd), lambda i, p_ref: (p_ref[1], i, 0)),
                pl.BlockSpec((3, tr, wd), lambda i, p_ref: (0, i, 0)),
            ],
            out_specs=pl.BlockSpec((tr, wd), lambda i, p_ref: (i, 0)),
        ),
        out_shape=jax.ShapeDtypeStruct((hh, wd), F32),
        compiler_params=_params(("parallel",)),
    )(place, gr, t1, t2)


def _join_halves(half):
    hh, wd = half.shape

    def body(h_ref, f_ref, send_sem, recv_sem):
        x, y, c = _place()
        cp = pltpu.make_async_remote_copy(src_ref=h_ref, dst_ref=f_ref.at[pl.ds(c * hh, hh)], send_sem=send_sem,
                                          recv_sem=recv_sem, device_id=(x, y, 1 - c), device_id_type=MESH)
        cp.start()
        cp.wait_send()
        pltpu.make_async_remote_copy(src_ref=h_ref, dst_ref=f_ref.at[pl.ds((1 - c) * hh, hh)], send_sem=send_sem,
                                     recv_sem=recv_sem, device_id=(x, y, 1 - c), device_id_type=MESH).wait_recv()

    got = _pcall(
        body,
        name="grad_join",
        in_specs=[ANY],
        out_specs=ANY,
        out_shape=jax.ShapeDtypeStruct((2 * hh, wd), half.dtype),
        scratch_shapes=[pltpu.SemaphoreType.DMA, pltpu.SemaphoreType.DMA],
    )(half)
    return lax.dynamic_update_slice(got, half, (lax.axis_index("c") * hh, 0))


N_DEV = 8


def _allreduce_small(buf):
    rows, wd = buf.shape
    n = rows // N_DEV

    def body(x_ref, o_ref, stage_ref, send_sems, recv_sems):
        x, y, c = _place()
        me = 4 * x + 2 * y + c

        def peer(k):
            return (x ^ (k >> 2), y ^ ((k >> 1) & 1), c ^ (k & 1))

        def chunk(ref, d):
            return ref.at[pl.ds(pl.multiple_of(d * n, 8), n)]

        first = []
        for k in range(1, N_DEV):
            px, py, pc = peer(k)
            cp = pltpu.make_async_remote_copy(src_ref=chunk(x_ref, 4 * px + 2 * py + pc), dst_ref=stage_ref.at[k],
                                              send_sem=send_sems.at[k - 1], recv_sem=recv_sems.at[k - 1],
                                              device_id=(px, py, pc), device_id_type=MESH)
            cp.start()
            first.append(cp)
        acc = chunk(x_ref, me)[...]
        for k in range(1, N_DEV):
            first[k - 1].wait_recv()
            acc = acc + stage_ref[k]
        chunk(o_ref, me)[...] = acc
        second = []
        for k in range(1, N_DEV):
            cp = pltpu.make_async_remote_copy(src_ref=chunk(o_ref, me), dst_ref=chunk(o_ref, me),
                                              send_sem=send_sems.at[6 + k], recv_sem=recv_sems.at[6 + k],
                                              device_id=peer(k), device_id_type=MESH)
            cp.start()
            second.append(cp)
        for k in range(1, N_DEV):
            px, py, pc = peer(k)
            them = 4 * px + 2 * py + pc
            pltpu.make_async_remote_copy(src_ref=chunk(o_ref, them), dst_ref=chunk(o_ref, them),
                                         send_sem=send_sems.at[6 + k], recv_sem=recv_sems.at[6 + k],
                                         device_id=(px, py, pc), device_id_type=MESH).wait_recv()
        for cp in first + second:
            cp.wait_send()

    vmem = pl.BlockSpec(memory_space=pltpu.VMEM)
    return _pcall(
        body,
        name="allreduce_small",
        in_specs=[vmem],
        out_specs=vmem,
        out_shape=jax.ShapeDtypeStruct((rows, wd), F32),
        scratch_shapes=[pltpu.VMEM((N_DEV, n, wd), F32), pltpu.SemaphoreType.DMA((14,)),
                        pltpu.SemaphoreType.DMA((14,))],
    )(buf)


@functools.partial(jax.custom_vjp, nondiff_argnums=(1,))
def split_cols(z, spec):
    outs, off = [], 0
    for n in spec[0]:
        outs.append(z[:, off:off + n])
        off += n
    return tuple(outs)


def _split_cols_fwd(z, spec):
    return split_cols(z, spec), None


def _split_cols_bwd(spec, _, gs):
    parts = list(gs)
    if spec[1]:
        parts.append(jnp.zeros((parts[0].shape[0], spec[1]), parts[0].dtype))
    return (jnp.concatenate(parts, axis=1),)


split_cols.defvjp(_split_cols_fwd, _split_cols_bwd)


def _rope(t, cos, sin):
    half = t.shape[-1] // 2
    t1, t2 = t[..., :half], t[..., half:]
    return jnp.concatenate([t1 * cos - t2 * sin, t2 * cos + t1 * sin], axis=-1)


def _s5_mixer(u, sp, l, w16, toks):
    g, p_, h = SSM_GROUPS, SSM_STATE, SSM_GROUP_CH
    lam_re, lam_im = sp["ssm_lam_re", l], sp["ssm_lam_im", l]
    step = jnp.exp(sp["ssm_log_step", l])[:, None]
    mag, ang = jnp.exp(lam_re * step), lam_im * step
    lbr, lbi = mag * jnp.cos(ang), mag * jnp.sin(ang)
    den = lam_re * lam_re + lam_im * lam_im
    nr, ni = lbr - 1.0, lbi
    qr, qi = (nr * lam_re + ni * lam_im) / den, (ni * lam_re - nr * lam_im) / den
    b_re, b_im = sp["ssm_b_re", l], sp["ssm_b_im", l]
    bbr = qr[..., None] * b_re - qi[..., None] * b_im
    bbi = qr[..., None] * b_im + qi[..., None] * b_re
    nb = g // SSM_PACK
    eye = jnp.eye(SSM_PACK, dtype=F32)

    def in_blocks(t):
        return jnp.einsum("jgph,gk->jghkp", t.reshape(nb, SSM_PACK, p_, h), eye).reshape(nb, SSM_PACK * h, SSM_PACK * p_)

    def out_blocks(t):
        return jnp.einsum("jghp,gk->jgpkh", t.reshape(nb, SSM_PACK, h, p_), eye).reshape(nb, SSM_PACK * p_, SSM_PACK * h)

    nm = "ssm%d" % l
    ur = bd_matmul(u, in_blocks(bbr), nm + "_bur")
    ui = bd_matmul(u, in_blocks(bbi), nm + "_bui")
    xr, xi = s5_scan(ur, ui, lbr.reshape(1, g * p_), lbi.reshape(1, g * p_), nm + "_scan")
    y = (bd_matmul(xr, out_blocks(sp["ssm_c_re", l]), nm + "_cr") + bd_matmul(xi, out_blocks(-sp["ssm_c_im", l]), nm + "_ci")
         + sp["ssm_d", l].reshape(1, g * h) * u)
    y = jax.nn.gelu(y)
    zz = matmul_w(y, w16["ssm_w_glu", l], toks["ssm_w_glu", l], nm + "_glu")
    z_out, z_gate = split_cols(zz, ((D_MODEL, D_MODEL), 0))
    return z_out * jax.nn.sigmoid(z_gate)


def _mla_mixer(q_lat, kv_lat, k_rope, cos, sin, sp, l, w16, toks):
    s = q_lat.shape[0]
    nm = "mla%d" % l
    dqk = MLA_NOPE + MLA_ROPE
    qt = matmul_w_t(rmsnorm(q_lat, sp["mla_q_norm", l], nm + "_qn"), w16["mla_w_uq", l], toks["mla_w_uq", l],
                    nm + "_uq").reshape(MLA_HEADS, dqk, s)
    half = MLA_ROPE // 2
    q1, q2 = qt[:, MLA_NOPE:MLA_NOPE + half], qt[:, MLA_NOPE + half:]
    ct, st = cos.T[None], sin.T[None]
    qt = jnp.concatenate([qt[:, :MLA_NOPE], q1 * ct - q2 * st, q2 * ct + q1 * st], axis=1).reshape(MLA_HEADS * dqk, s)
    kvt = matmul_w_t(rmsnorm(kv_lat, sp["mla_kv_norm", l], nm + "_kvn"), w16["mla_w_ukv", l], toks["mla_w_ukv", l],
                     nm + "_ukv").reshape(MLA_HEADS, MLA_NOPE + MLA_V, s)
    k_pe = _rope(k_rope, cos, sin)
    k = jnp.concatenate([kvt[:, :MLA_NOPE].transpose(0, 2, 1),
                         jnp.broadcast_to(k_pe[None], (MLA_HEADS, s, MLA_ROPE))], axis=-1)
    vt = kvt[:, MLA_NOPE:].reshape(MLA_HEADS * MLA_V, s)
    ot = attention(qt, k, vt, 1.0 / math.sqrt(dqk), True, nm + "_att")
    return matmul_w_at(ot, w16["mla_w_o", l], toks["mla_w_o", l], nm + "_o")


def _hgrn_mixer(q, f_logit, i_in, g, lb, sp, l, w16, toks):
    s = q.shape[0]
    nm = "hg%d" % l
    f = lb + (1.0 - lb) * jax.nn.sigmoid(f_logit)
    o = hgrn_core(jax.nn.silu(q), 1.0 - f, i_in, jnp.log(f), nm + "_core")
    o = rmsnorm(o.reshape(s * HG_HEADS, HG_DV), sp["hg_g_norm", l], nm + "_gn").reshape(s, HG_HEADS * HG_DV)
    return matmul_w(o * jax.nn.silu(g), w16["hg_w_o", l], toks["hg_w_o", l], nm + "_o")


def _cross_attention(h, mem_n, l, w16, toks):
    s, m = h.shape[0], mem_n.shape[0]
    nm = "xat%d" % l
    xw = X_HEADS * X_HEAD_DIM
    qt = matmul_w_t(h, w16["x_w_q", l], toks["x_w_q", l], nm + "_q")
    kv = matmul_w(mem_n, w16["x_w_kv", l], toks["x_w_kv", l], nm + "_kv")
    k, v = split_cols(kv, ((xw, xw), 0))
    k = k.reshape(m, X_HEADS, X_HEAD_DIM).transpose(1, 0, 2)
    ot = attention(qt, k, v.T, 1.0 / math.sqrt(X_HEAD_DIM), False, nm + "_att")
    return matmul_w_at(ot, w16["x_w_o", l], toks["x_w_o", l], nm + "_o")


_IN_HG = SSM_WIDTH + MLA_Q_RANK + MLA_KV_RANK
_IN_GATE = _IN_HG + 4 * HG_HEADS * HG_DK
_IN_ROPE = _IN_GATE + 3 * D_MODEL


def _perm_in(w):
    pad = jnp.zeros(w.shape[:-1] + (D_IN_PAD - D_IN,), w.dtype)
    return jnp.concatenate([w[..., :_IN_HG], w[..., _IN_HG + MLA_ROPE:], w[..., _IN_HG:_IN_HG + MLA_ROPE], pad], axis=-1)


def _unperm_in(g):
    return jnp.concatenate([g[..., :_IN_HG], g[..., _IN_ROPE:_IN_ROPE + MLA_ROPE], g[..., _IN_HG:_IN_ROPE]], axis=-1)


def _layer(x, mem, cos, sin, lb, l, sp, w16, toks):
    s = x.shape[0]
    nm = "l%d" % l
    h = rmsnorm(x, sp["norm_mix", l], nm + "_nmix")
    z = matmul_w(h, w16["w_in", l], toks["w_in", l], nm + "_in")
    hw = HG_HEADS * HG_DK
    sizes = (SSM_WIDTH, MLA_Q_RANK, MLA_KV_RANK, hw, hw, hw, hw, 3 * D_MODEL, MLA_ROPE)
    u, q_lat, kv_lat, hg_q, hg_f, hg_i, hg_g, gate_logits, k_rope = split_cols(z, (sizes, D_IN_PAD - D_IN))
    g_ssm, g_mla, g_hg = split_cols(jax.nn.sigmoid(gate_logits), ((D_MODEL,) * 3, 0))
    y_ssm = _s5_mixer(u, sp, l, w16, toks)
    y_mla = _mla_mixer(q_lat, kv_lat, k_rope, cos, sin, sp, l, w16, toks)
    y_hg = _hgrn_mixer(hg_q, hg_f, hg_i, hg_g, lb, sp, l, w16, toks)
    merged = g_ssm * y_ssm + g_mla * y_mla + g_hg * y_hg
    x = x + matmul_w(merged, w16["w_out", l], toks["w_out", l], nm + "_out")
    hc = rmsnorm(x, sp["norm_cross", l], nm + "_ncross")
    mem_n = rmsnorm(mem, sp["norm_mem", l], nm + "_nmem")
    x = x + _cross_attention(hc, mem_n, l, w16, toks)
    hf = rmsnorm(x, sp["norm_ffn", l], nm + "_nffn")
    gu = matmul_w(hf, w16["ffn_w_gate_up", l], toks["ffn_w_gate_up", l], nm + "_gu")
    gt, up = split_cols(gu, ((D_FF, D_FF), 0))
    return x + matmul_w(jax.nn.silu(gt) * up, w16["ffn_w_down", l], toks["ffn_w_down", l], nm + "_down")


def _loss_fn(sp, toks, x, w16, mem, pos, target):
    half = MLA_ROPE // 2
    inv_freq = ROPE_THETA ** (-jnp.arange(half, dtype=F32) / half)
    ang = pos.astype(F32)[:, None] * inv_freq
    cos, sin = jnp.cos(ang), jnp.sin(ang)
    lb_p = jax.nn.softmax(sp["hg_lb", ALL], axis=0)
    lower = jnp.cumsum(lb_p, axis=0) - lb_p[0:1]
    for l in range(DEPTH):
        x = _layer(x, mem, cos, sin, lower[l], l, sp, w16, toks)
    y = rmsnorm(x, sp["norm_final", ALL], "nfinal")
    return 0.5 * jnp.sum(jnp.mean(jnp.square(y - target), axis=-1))


WEIGHTS = ["norm_mix", "w_in", "ssm_lam_re", "ssm_lam_im", "ssm_b_re", "ssm_b_im", "ssm_c_re", "ssm_c_im", "ssm_d",
           "ssm_log_step", "ssm_w_glu", "mla_q_norm", "mla_kv_norm", "mla_w_uq", "mla_w_ukv", "mla_w_o", "hg_lb",
           "hg_g_norm", "hg_w_o", "w_out", "norm_cross", "norm_mem", "x_w_q", "x_w_kv", "x_w_o", "norm_ffn",
           "ffn_w_gate_up", "ffn_w_down", "norm_final"]
INPUTS = ["x", "mem", "positions"] + WEIGHTS + ["loss_target"] + ["m_" + n for n in WEIGHTS] + ["v_" + n for n in WEIGHTS]
BIG = [("w_in", 2), ("ssm_w_glu", 2), ("mla_w_uq", 2), ("mla_w_ukv", 2), ("mla_w_o", 2), ("hg_w_o", 2), ("w_out", 1),
       ("x_w_q", 1), ("x_w_kv", 1), ("x_w_o", 2), ("ffn_w_gate_up", 2), ("ffn_w_down", 1)]
SMALL = [n for n in WEIGHTS if n not in dict(BIG)]
WHOLE = ("hg_lb", "norm_final")
ALL = -1
PACK_W = 1024
SMALL_W = 128
N_CHIPS = 4


def _pack(arrs, rows, width, dtype):
    flat = jnp.concatenate([a.astype(dtype).reshape(-1) for a in arrs])
    return jnp.pad(flat, (0, rows * width - flat.shape[0])).reshape(rows, width)


def _unpack(buf, like):
    flat = buf.reshape(-1)
    out, off = [], 0
    for a in like:
        n = math.prod(a.shape)
        out.append(flat[off:off + n].reshape(a.shape))
        off += n
    return out


def kernel(
    x, mem, positions, norm_mix, w_in, ssm_lam_re, ssm_lam_im, ssm_b_re, ssm_b_im, ssm_c_re, ssm_c_im, ssm_d,
    ssm_log_step, ssm_w_glu, mla_q_norm, mla_kv_norm, mla_w_uq, mla_w_ukv, mla_w_o, hg_lb, hg_g_norm, hg_w_o,
    w_out, norm_cross, norm_mem, x_w_q, x_w_kv, x_w_o, norm_ffn, ffn_w_gate_up, ffn_w_down, norm_final,
    loss_target, m_norm_mix, m_w_in, m_ssm_lam_re, m_ssm_lam_im, m_ssm_b_re, m_ssm_b_im, m_ssm_c_re, m_ssm_c_im,
    m_ssm_d, m_ssm_log_step, m_ssm_w_glu, m_mla_q_norm, m_mla_kv_norm, m_mla_w_uq, m_mla_w_ukv, m_mla_w_o,
    m_hg_lb, m_hg_g_norm, m_hg_w_o, m_w_out, m_norm_cross, m_norm_mem, m_x_w_q, m_x_w_kv, m_x_w_o, m_norm_ffn,
    m_ffn_w_gate_up, m_ffn_w_down, m_norm_final, v_norm_mix, v_w_in, v_ssm_lam_re, v_ssm_lam_im, v_ssm_b_re,
    v_ssm_b_im, v_ssm_c_re, v_ssm_c_im, v_ssm_d, v_ssm_log_step, v_ssm_w_glu, v_mla_q_norm, v_mla_kv_norm,
    v_mla_w_uq, v_mla_w_ukv, v_mla_w_o, v_hg_lb, v_hg_g_norm, v_hg_w_o, v_w_out, v_norm_cross, v_norm_mem,
    v_x_w_q, v_x_w_kv, v_x_w_o, v_norm_ffn, v_ffn_w_gate_up, v_ffn_w_down, v_norm_final):
    given = locals()
    a = {n: given[n] for n in INPUTS}
    x, mem, pos, target = a["x"][0], a["mem"][0], a["positions"][0], a["loss_target"][0]
    place = jnp.stack([lax.axis_index("c"), 2 * lax.axis_index("x") + lax.axis_index("y")]).astype(jnp.int32)

    n_big = sum(math.prod(a[n].shape) for n, _ in BIG)
    rows = -(-n_big // (PACK_W * 32)) * 32
    gathered = _gather_weights(_pack([a[n] for n, _ in BIG], rows, PACK_W, BF16)).reshape(N_CHIPS, rows * PACK_W)
    w16, toks, off = {}, {}, 0
    for n, ax in BIG:
        shp = a[n].shape[1:]
        per = math.prod(shp)
        for l in range(DEPTH):
            full = jnp.concatenate([gathered[q, off:off + per].reshape(shp) for q in range(N_CHIPS)], axis=ax - 1)
            if n == "w_in":
                full = _perm_in(full)
            w16[n, l] = full
            toks[n, l] = jnp.zeros(full.shape, F32)
            off += per
    sp = {}
    for n in SMALL:
        if n in WHOLE:
            sp[n, ALL] = a[n]
        else:
            for l in range(DEPTH):
                sp[n, l] = a[n][l]

    loss, (g_sp, g_tok, g_x) = jax.value_and_grad(_loss_fn, argnums=(0, 1, 2))(sp, toks, x, w16, mem, pos, target)
    g_small = {n: g_sp[n, ALL] if n in WHOLE else jnp.stack([g_sp[n, l] for l in range(DEPTH)]) for n in SMALL}

    blocks = []
    for q in range(N_CHIPS):
        parts = []
        for n, ax in BIG:
            for l in range(DEPTH):
                g = _unperm_in(g_tok[n, l]) if n == "w_in" else g_tok[n, l]
                parts.append(jnp.split(g, N_CHIPS, axis=ax - 1)[q])
        blocks.append(_pack(parts, rows, PACK_W, F32))
    grads = jnp.stack(blocks)
    from_sibling = _swap_halves_to_sibling(grads)
    from_chips = _scatter_partials(_add_half(grads, from_sibling, place))
    total = _add_partials(grads, from_sibling, from_chips, place)
    g_big = dict(zip([n for n, _ in BIG], _unpack(_join_halves(total), [a[n] for n, _ in BIG])))

    small_like = [a[n] for n in SMALL] + [jnp.zeros((1,), F32)]
    n_small = sum(math.prod(t.shape) for t in small_like)
    srows = -(-n_small // (SMALL_W * 8 * N_DEV)) * 8 * N_DEV
    reduced = _allreduce_small(_pack([g_small[n] for n in SMALL] + [loss.reshape(1)], srows, SMALL_W, F32))
    red = _unpack(reduced, small_like)
    g_sm = dict(zip(SMALL, red[:-1]))
    loss = red[-1].reshape(())

    out_g, out_d, out_m, out_v = {}, {}, {}, {}
    for n, _ in BIG:
        shp = a[n].shape
        two = lambda t: t.reshape(shp[0] * shp[1], shp[2])
        d, m, v = _adamw_call(two(a[n]), two(g_big[n]), two(a["m_" + n]), two(a["v_" + n]), "adamw_" + n)
        out_g[n], out_d[n], out_m[n], out_v[n] = g_big[n], d.reshape(shp), m.reshape(shp), v.reshape(shp)
    small_arrs = [a[n] for n in SMALL]
    d, m, v = _adamw_call(
        _pack(small_arrs, srows, SMALL_W, F32), reduced,
        _pack([a["m_" + n] for n in SMALL], srows, SMALL_W, F32),
        _pack([a["v_" + n] for n in SMALL], srows, SMALL_W, F32), "adamw_small")
    for n, gg, dd, mm_, vv in zip(SMALL, red[:-1], _unpack(d, small_arrs), _unpack(m, small_arrs), _unpack(v, small_arrs)):
        out_g[n], out_d[n], out_m[n], out_v[n] = gg, dd, mm_, vv

    return (loss, g_x[None], *[out_g[n] for n in WEIGHTS], *[out_d[n] for n in WEIGHTS],
            *[out_m[n] for n in WEIGHTS], *[out_v[n] for n in WEIGHTS])
```

```python
import functools
import math

import jax
import jax.numpy as jnp
from jax import lax
from jax.experimental import pallas as pl
from jax.experimental.pallas import tpu as pltpu

F32 = jnp.float32
BF16 = jnp.bfloat16
MESH = pl.DeviceIdType.MESH

D_MODEL = 1024
DEPTH = 2
RMS_EPS = 1e-6
SSM_GROUPS, SSM_GROUP_CH, SSM_STATE = 32, 16, 64
SSM_WIDTH = SSM_GROUPS * SSM_GROUP_CH
SSM_LANES = SSM_GROUPS * SSM_STATE
SSM_PACK = 8
MLA_HEADS, MLA_Q_RANK, MLA_KV_RANK, MLA_NOPE, MLA_ROPE, MLA_V = 8, 512, 256, 64, 32, 64
ROPE_THETA = 10000.0
HG_HEADS, HG_DK, HG_DV, HG_CHUNK, HG_SUB = 4, 128, 128, 64, 16
X_HEADS, X_HEAD_DIM = 4, 128
D_FF = 2816
D_IN = 6432
D_IN_PAD = 6656
MASK_VALUE = -1e30

ADAM_LR, ADAM_B1, ADAM_B2, ADAM_EPS, ADAM_WD, ADAM_STEP = 0.001, 0.9, 0.999, 1e-08, 0.01, 10

VMEM_LIMIT = 48 * 1024 * 1024


def _pcall(body, **kw):
    return pl.pallas_call(body, **kw)


def _params(sem):
    return pltpu.CompilerParams(dimension_semantics=sem, vmem_limit_bytes=VMEM_LIMIT)


def _tile(n, cap, align=128):
    if n <= cap:
        return n
    t = (cap // align) * align
    while t >= align:
        if n % t == 0:
            return t
        t -= align
    return n


def _mm(a, b, *, ta=False, tb=False, out_dtype=F32, name="mm"):
    if ta:
        K, M = a.shape
    else:
        M, K = a.shape
    if tb:
        N, Kb = b.shape
    else:
        Kb, N = b.shape
    assert K == Kb, (a.shape, b.shape, ta, tb)
    tm = _tile(M, 512)
    tn = _tile(N, 512)
    tk = _tile(K, 2048)
    nk = K // tk
    dims = (((0,) if ta else (1,), (1,) if tb else (0,)), ((), ()))

    def body(a_ref, b_ref, o_ref, *acc):
        av = a_ref[...].astype(BF16)
        bv = b_ref[...].astype(BF16)
        part = lax.dot_general(av, bv, dims, preferred_element_type=F32)
        if nk == 1:
            o_ref[...] = part.astype(o_ref.dtype)
        else:
            acc_ref = acc[0]
            k = pl.program_id(2)

            @pl.when(k == 0)
            def _():
                acc_ref[...] = part

            @pl.when(k > 0)
            def _():
                acc_ref[...] += part

            @pl.when(k == nk - 1)
            def _():
                o_ref[...] = acc_ref[...].astype(o_ref.dtype)

    a_spec = pl.BlockSpec((tk, tm), lambda i, j, k: (k, i)) if ta else pl.BlockSpec((tm, tk), lambda i, j, k: (i, k))
    b_spec = pl.BlockSpec((tn, tk), lambda i, j, k: (j, k)) if tb else pl.BlockSpec((tk, tn), lambda i, j, k: (k, j))
    return _pcall(
        body,
        name=name,
        grid=(M // tm, N // tn, nk),
        in_specs=[a_spec, b_spec],
        out_specs=pl.BlockSpec((tm, tn), lambda i, j, k: (i, j)),
        out_shape=jax.ShapeDtypeStruct((M, N), out_dtype),
        scratch_shapes=[pltpu.VMEM((tm, tn), F32)] if nk > 1 else [],
        compiler_params=_params(("parallel", "parallel", "arbitrary")),
    )(a, b)


def _bd_call(a, bc, transpose, name):
    m = a.shape[0]
    nb, ka, kb = bc.shape
    win, wout = (kb, ka) if transpose else (ka, kb)
    assert a.shape[1] == nb * win
    tm = _tile(m, 1024)
    dims = _NT if transpose else (((1,), (0,)), ((), ()))

    def body(a_ref, b_ref, o_ref):
        o_ref[...] = lax.dot_general(a_ref[...].astype(BF16), b_ref[0].astype(BF16), dims, preferred_element_type=F32)

    return _pcall(
        body,
        name=name,
        grid=(m // tm, nb),
        in_specs=[pl.BlockSpec((tm, win), lambda i, j: (i, j)), pl.BlockSpec((1, ka, kb), lambda i, j: (j, 0, 0))],
        out_specs=pl.BlockSpec((tm, wout), lambda i, j: (i, j)),
        out_shape=jax.ShapeDtypeStruct((m, nb * wout), F32),
        compiler_params=_params(("parallel", "parallel")),
    )(a, bc)


def _bd_dw_call(a, g, nb, name):
    m = a.shape[0]
    ka, kb = a.shape[1] // nb, g.shape[1] // nb
    tk = _tile(m, 1024)
    nk = m // tk

    def body(a_ref, g_ref, o_ref):
        part = lax.dot_general(a_ref[...].astype(BF16), g_ref[...].astype(BF16), _TN, preferred_element_type=F32)
        k = pl.program_id(1)

        @pl.when(k == 0)
        def _():
            o_ref[0] = part

        @pl.when(k > 0)
        def _():
            o_ref[0] += part

    return _pcall(
        body,
        name=name,
        grid=(nb, nk),
        in_specs=[pl.BlockSpec((tk, ka), lambda j, k: (k, j)), pl.BlockSpec((tk, kb), lambda j, k: (k, j))],
        out_specs=pl.BlockSpec((1, ka, kb), lambda j, k: (j, 0, 0)),
        out_shape=jax.ShapeDtypeStruct((nb, ka, kb), F32),
        compiler_params=_params(("parallel", "arbitrary")),
    )(a, g)


@functools.partial(jax.custom_vjp, nondiff_argnums=(2,))
def bd_matmul(a, bc, name):
    return _bd_call(a, bc, False, name + "_f")


def _bd_matmul_fwd(a, bc, name):
    return _bd_call(a, bc, False, name + "_f"), (a, bc)


def _bd_matmul_bwd(name, res, g):
    a, bc = res
    return _bd_call(g, bc, True, name + "_da"), _bd_dw_call(a, g, bc.shape[0], name + "_dw")


bd_matmul.defvjp(_bd_matmul_fwd, _bd_matmul_bwd)


@functools.partial(jax.custom_vjp, nondiff_argnums=(3,))
def matmul_w(a, w16, tok, name):
    return _mm(a, w16, name=name + "_f")


def _matmul_w_fwd(a, w16, tok, name):
    return _mm(a, w16, name=name + "_f"), (a, w16)


def _matmul_w_bwd(name, res, g):
    a, w16 = res
    return _mm(g, w16, tb=True, name=name + "_da"), jnp.zeros_like(w16), _mm(a, g, ta=True, name=name + "_dw")


matmul_w.defvjp(_matmul_w_fwd, _matmul_w_bwd)


@functools.partial(jax.custom_vjp, nondiff_argnums=(3,))
def matmul_w_t(a, w16, tok, name):
    return _mm(w16, a, ta=True, tb=True, name=name + "_f")


def _matmul_w_t_fwd(a, w16, tok, name):
    return _mm(w16, a, ta=True, tb=True, name=name + "_f"), (a, w16)


def _matmul_w_t_bwd(name, res, g):
    a, w16 = res
    return (_mm(g, w16, ta=True, tb=True, name=name + "_da"), jnp.zeros_like(w16),
            _mm(a, g, ta=True, tb=True, name=name + "_dw"))


matmul_w_t.defvjp(_matmul_w_t_fwd, _matmul_w_t_bwd)


@functools.partial(jax.custom_vjp, nondiff_argnums=(3,))
def matmul_w_at(at, w16, tok, name):
    return _mm(at, w16, ta=True, name=name + "_f")


def _matmul_w_at_fwd(at, w16, tok, name):
    return _mm(at, w16, ta=True, name=name + "_f"), (at, w16)


def _matmul_w_at_bwd(name, res, g):
    at, w16 = res
    return _mm(w16, g, tb=True, name=name + "_da"), jnp.zeros_like(w16), _mm(at, g, name=name + "_dw")


matmul_w_at.defvjp(_matmul_w_at_fwd, _matmul_w_at_bwd)


def _rms_fwd_call(x, g, name):
    rows, d = x.shape
    tr = _tile(rows, 512, 8)

    def body(x_ref, g_ref, y_ref):
        xv = x_ref[...]
        rstd = lax.rsqrt(jnp.mean(xv * xv, axis=-1, keepdims=True) + RMS_EPS)
        y_ref[...] = xv * rstd * g_ref[...]

    return _pcall(
        body,
        name=name,
        grid=(rows // tr,),
        in_specs=[pl.BlockSpec((tr, d), lambda i: (i, 0)), pl.BlockSpec((1, d), lambda i: (0, 0))],
        out_specs=pl.BlockSpec((tr, d), lambda i: (i, 0)),
        out_shape=jax.ShapeDtypeStruct((rows, d), F32),
        compiler_params=_params(("parallel",)),
    )(x, g.reshape(1, d))


def _rms_bwd_call(x, g, dy, name):
    rows, d = x.shape
    tr = _tile(rows, 512, 8)
    nb = rows // tr

    def body(x_ref, g_ref, dy_ref, dx_ref, dg_ref):
        xv = x_ref[...]
        rstd = lax.rsqrt(jnp.mean(xv * xv, axis=-1, keepdims=True) + RMS_EPS)
        xh = xv * rstd
        dyv = dy_ref[...]
        dyg = dyv * g_ref[...]
        dx_ref[...] = rstd * (dyg - xh * jnp.mean(dyg * xh, axis=-1, keepdims=True))
        dg_ref[...] = jnp.sum((dyv * xh).reshape(tr // 8, 8, d), axis=0)

    dx, dgp = _pcall(
        body,
        name=name,
        grid=(nb,),
        in_specs=[
            pl.BlockSpec((tr, d), lambda i: (i, 0)),
            pl.BlockSpec((1, d), lambda i: (0, 0)),
            pl.BlockSpec((tr, d), lambda i: (i, 0)),
        ],
        out_specs=[pl.BlockSpec((tr, d), lambda i: (i, 0)), pl.BlockSpec((8, d), lambda i: (i, 0))],
        out_shape=[jax.ShapeDtypeStruct((rows, d), F32), jax.ShapeDtypeStruct((nb * 8, d), F32)],
        compiler_params=_params(("parallel",)),
    )(x, g.reshape(1, d), dy)
    return dx, jnp.sum(dgp, axis=0)


@functools.partial(jax.custom_vjp, nondiff_argnums=(2,))
def rmsnorm(x, g, name):
    return _rms_fwd_call(x, g, name + "_f")


def _rmsnorm_fwd(x, g, name):
    return _rms_fwd_call(x, g, name + "_f"), (x, g)


def _rmsnorm_bwd(name, res, dy):
    x, g = res
    return _rms_bwd_call(x, g, dy, name + "_b")


rmsnorm.defvjp(_rmsnorm_fwd, _rmsnorm_bwd)


def _attn_blocks(sq, sk):
    return _tile(sq, 512), _tile(sk, 512)


def _attn_pt(qs, k_ref, qi, ki, bq, bk, masked, shift):
    st = jnp.dot(k_ref[0].astype(BF16), qs, preferred_element_type=F32)
    if masked:
        kpos = ki * bk + lax.broadcasted_iota(jnp.int32, (bk, bq), 0)
        qpos = qi * bq + lax.broadcasted_iota(jnp.int32, (bk, bq), 1)
        st = jnp.where(kpos <= qpos, st, MASK_VALUE)
    return st if shift is None else jnp.exp(st - shift)


def _attn_cases(causal, qi, ki, bq, bk, step):
    if not causal:
        step(False)
        return
    pl.when(ki * bk + bk - 1 <= qi * bq)(functools.partial(step, False))
    pl.when(jnp.logical_and(ki * bk + bk - 1 > qi * bq, ki * bk <= qi * bq + bq - 1))(functools.partial(step, True))


def _attn_fwd_call(qt, k, vt, scale, causal, name):
    h, sk, dq = k.shape
    sq = qt.shape[1]
    dv = vt.shape[0] // h
    bq, bk = _attn_blocks(sq, sk)
    nq, nk = sq // bq, sk // bk

    def body(q_ref, k_ref, v_ref, o_ref, lse_ref, m_sc, l_sc, acc_sc):
        qi, ki = pl.program_id(1), pl.program_id(2)

        @pl.when(ki == 0)
        def _():
            m_sc[...] = jnp.full(m_sc.shape, -jnp.inf, F32)
            l_sc[...] = jnp.zeros(l_sc.shape, F32)
            acc_sc[...] = jnp.zeros(acc_sc.shape, F32)

        def step(masked):
            qs = (q_ref[...] * scale).astype(BF16)
            st = _attn_pt(qs, k_ref, qi, ki, bq, bk, masked, None)
            m_prev = m_sc[...]
            m_new = jnp.maximum(m_prev, jnp.max(st, axis=0, keepdims=True))
            alpha = jnp.exp(m_prev - m_new)
            pt = jnp.exp(st - m_new)
            l_sc[...] = alpha * l_sc[...] + jnp.sum(pt, axis=0, keepdims=True)
            acc_sc[...] = alpha * acc_sc[...] + jnp.dot(v_ref[...].astype(BF16), pt.astype(BF16),
                                                        preferred_element_type=F32)
            m_sc[...] = m_new

        _attn_cases(causal, qi, ki, bq, bk, step)

        @pl.when(ki == nk - 1)
        def _():
            o_ref[...] = acc_sc[...] / l_sc[...]
            lse_ref[0] = m_sc[...] + jnp.log(l_sc[...])

    last = (lambda i, j: jnp.minimum(j, (i * bq + bq - 1) // bk)) if causal else (lambda i, j: j)
    qmap = lambda hh, i, j: (hh, i)
    return _pcall(
        body,
        name=name,
        grid=(h, nq, nk),
        in_specs=[
            pl.BlockSpec((dq, bq), qmap),
            pl.BlockSpec((1, bk, dq), lambda hh, i, j: (hh, last(i, j), 0)),
            pl.BlockSpec((dv, bk), lambda hh, i, j: (hh, last(i, j))),
        ],
        out_specs=[pl.BlockSpec((dv, bq), qmap), pl.BlockSpec((1, 1, bq), lambda hh, i, j: (hh, 0, i))],
        out_shape=[jax.ShapeDtypeStruct((h * dv, sq), F32), jax.ShapeDtypeStruct((h, 1, sq), F32)],
        scratch_shapes=[pltpu.VMEM((1, bq), F32), pltpu.VMEM((1, bq), F32), pltpu.VMEM((dv, bq), F32)],
        compiler_params=_params(("parallel", "parallel", "arbitrary")),
    )(qt, k, vt)


_TN = (((0,), (0,)), ((), ()))
_NT = (((1,), (1,)), ((), ()))


def _attn_dst(qs, k_ref, v_ref, do_ref, lse_ref, dl_ref, qi, ki, bq, bk, masked):
    pt = _attn_pt(qs, k_ref, qi, ki, bq, bk, masked, lse_ref[0])
    dpt = lax.dot_general(v_ref[...].astype(BF16), do_ref[...].astype(BF16), _TN, preferred_element_type=F32)
    return pt, pt * (dpt - dl_ref[0])


def _attn_dq_call(qt, k, vt, dot, lse, delta, scale, causal, name):
    h, sk, dq = k.shape
    sq = qt.shape[1]
    dv = vt.shape[0] // h
    bq, bk = _attn_blocks(sq, sk)
    nq, nk = sq // bq, sk // bk

    def body(q_ref, k_ref, v_ref, do_ref, lse_ref, dl_ref, dq_ref, acc_sc):
        qi, ki = pl.program_id(1), pl.program_id(2)

        @pl.when(ki == 0)
        def _():
            acc_sc[...] = jnp.zeros(acc_sc.shape, F32)

        def step(masked):
            qs = (q_ref[...] * scale).astype(BF16)
            _, dst = _attn_dst(qs, k_ref, v_ref, do_ref, lse_ref, dl_ref, qi, ki, bq, bk, masked)
            acc_sc[...] += lax.dot_general(k_ref[0].astype(BF16), dst.astype(BF16), _TN, preferred_element_type=F32)

        _attn_cases(causal, qi, ki, bq, bk, step)

        @pl.when(ki == nk - 1)
        def _():
            dq_ref[...] = acc_sc[...] * scale

    last = (lambda i, j: jnp.minimum(j, (i * bq + bq - 1) // bk)) if causal else (lambda i, j: j)
    qmap = lambda hh, i, j: (hh, i)
    rowmap = lambda hh, i, j: (hh, 0, i)
    return _pcall(
        body,
        name=name,
        grid=(h, nq, nk),
        in_specs=[
            pl.BlockSpec((dq, bq), qmap),
            pl.BlockSpec((1, bk, dq), lambda hh, i, j: (hh, last(i, j), 0)),
            pl.BlockSpec((dv, bk), lambda hh, i, j: (hh, last(i, j))),
            pl.BlockSpec((dv, bq), qmap),
            pl.BlockSpec((1, 1, bq), rowmap),
            pl.BlockSpec((1, 1, bq), rowmap),
        ],
        out_specs=pl.BlockSpec((dq, bq), qmap),
        out_shape=jax.ShapeDtypeStruct((h * dq, sq), F32),
        scratch_shapes=[pltpu.VMEM((dq, bq), F32)],
        compiler_params=_params(("parallel", "parallel", "arbitrary")),
    )(qt, k, vt, dot, lse, delta)


def _attn_dkv_call(qt, k, vt, dot, lse, delta, scale, causal, name):
    h, sk, dq = k.shape
    sq = qt.shape[1]
    dv = vt.shape[0] // h
    bq, bk = _attn_blocks(sq, sk)
    nq, nk = sq // bq, sk // bk

    def body(q_ref, k_ref, v_ref, do_ref, lse_ref, dl_ref, dk_ref, dv_ref, dk_sc, dv_sc):
        ki, qi = pl.program_id(1), pl.program_id(2)

        @pl.when(qi == 0)
        def _():
            dk_sc[...] = jnp.zeros(dk_sc.shape, F32)
            dv_sc[...] = jnp.zeros(dv_sc.shape, F32)

        def step(masked):
            qs = (q_ref[...] * scale).astype(BF16)
            pt, dst = _attn_dst(qs, k_ref, v_ref, do_ref, lse_ref, dl_ref, qi, ki, bq, bk, masked)
            dv_sc[...] += lax.dot_general(do_ref[...].astype(BF16), pt.astype(BF16), _NT, preferred_element_type=F32)
            dk_sc[...] += lax.dot_general(dst.astype(BF16), qs, _NT, preferred_element_type=F32)

        _attn_cases(causal, qi, ki, bq, bk, step)

        @pl.when(qi == nq - 1)
        def _():
            dk_ref[0] = dk_sc[...]
            dv_ref[...] = dv_sc[...]

    first = (lambda j, i: jnp.maximum(i, (j * bk) // bq)) if causal else (lambda j, i: i)
    qmap = lambda hh, j, i: (hh, first(j, i))
    rowmap = lambda hh, j, i: (hh, 0, first(j, i))
    return _pcall(
        body,
        name=name,
        grid=(h, nk, nq),
        in_specs=[
            pl.BlockSpec((dq, bq), qmap),
            pl.BlockSpec((1, bk, dq), lambda hh, j, i: (hh, j, 0)),
            pl.BlockSpec((dv, bk), lambda hh, j, i: (hh, j)),
            pl.BlockSpec((dv, bq), qmap),
            pl.BlockSpec((1, 1, bq), rowmap),
            pl.BlockSpec((1, 1, bq), rowmap),
        ],
        out_specs=[pl.BlockSpec((1, bk, dq), lambda hh, j, i: (hh, j, 0)), pl.BlockSpec((dv, bk), lambda hh, j, i: (hh, j))],
        out_shape=[jax.ShapeDtypeStruct((h, sk, dq), F32), jax.ShapeDtypeStruct((h * dv, sk), F32)],
        scratch_shapes=[pltpu.VMEM((bk, dq), F32), pltpu.VMEM((dv, bk), F32)],
        compiler_params=_params(("parallel", "parallel", "arbitrary")),
    )(qt, k, vt, dot, lse, delta)


@functools.partial(jax.custom_vjp, nondiff_argnums=(3, 4, 5))
def attention(qt, k, vt, scale, causal, name):
    return _attn_fwd_call(qt, k, vt, scale, causal, name + "_f")[0]


def _attention_fwd(qt, k, vt, scale, causal, name):
    ot, lse = _attn_fwd_call(qt, k, vt, scale, causal, name + "_f")
    return ot, (qt, k, vt, ot, lse)


def _attention_bwd(scale, causal, name, res, dot):
    qt, k, vt, ot, lse = res
    h = k.shape[0]
    delta = jnp.sum((dot * ot).reshape(h, ot.shape[0] // h, ot.shape[1]), axis=1, keepdims=True)
    dqt = _attn_dq_call(qt, k, vt, dot, lse, delta, scale, causal, name + "_dq")
    dk, dvt = _attn_dkv_call(qt, k, vt, dot, lse, delta, scale, causal, name + "_dkv")
    return dqt, dk, dvt


attention.defvjp(_attention_fwd, _attention_bwd)


SCAN_ROWS = 8
SCAN_LANES = 256
SCAN_TBLOCK = 512


def _scan_tables(lr, li):
    pr, pi = [lr], [li]
    for _ in range(SCAN_ROWS - 1):
        pr, pi = pr + [pr[-1] * lr - pi[-1] * li], pi + [pr[-1] * li + pi[-1] * lr]
    return jnp.concatenate(pr, axis=0), jnp.concatenate(pi, axis=0)


def _scan_call(ur, ui, lr, li, reverse, name, states=None):
    s, n = ur.shape
    tb = _tile(s, SCAN_TBLOCK, 8)
    nt, nl = s // tb, n // SCAN_LANES
    ntile = tb // SCAN_ROWS
    pr, pi = _scan_tables(lr, li)
    if reverse:
        pr, pi = pr[::-1], pi[::-1]
    shape = (SCAN_ROWS, SCAN_LANES)
    with_states = states is not None
    assert reverse or not with_states

    def body(*refs):
        if with_states:
            ur_ref, ui_ref, pr_ref, pi_ref, sr_ref, si_ref, xr_ref, xi_ref, dr_ref, di_ref, cr_sc, ci_sc, ar_sc, ai_sc = refs
        else:
            ur_ref, ui_ref, pr_ref, pi_ref, xr_ref, xi_ref, cr_sc, ci_sc, ar_sc, ai_sc = refs

        @pl.when(pl.program_id(1) == 0)
        def _():
            for sc in (cr_sc, ci_sc, ar_sc, ai_sc):
                sc[...] = jnp.zeros(shape, F32)

        prv, piv = pr_ref[...], pi_ref[...]
        rows = lax.broadcasted_iota(jnp.int32, shape, 0)

        def powers(k):
            r = (SCAN_ROWS - k) if reverse else (k - 1)
            return jnp.broadcast_to(prv[r:r + 1], shape), jnp.broadcast_to(piv[r:r + 1], shape)

        pw = [powers(k) for k in (1, 2, 4)]

        def tile(i, carry):
            cr, ci, acr, aci = carry
            j = (ntile - 1 - i) if reverse else i
            sl = pl.ds(pl.multiple_of(j * SCAN_ROWS, SCAN_ROWS), SCAN_ROWS)
            xr, xi = ur_ref[sl, :], ui_ref[sl, :]
            for (ar, ai), k in zip(pw, (1, 2, 4)):
                if reverse:
                    keep = rows < SCAN_ROWS - k
                    sr = jnp.where(keep, pltpu.roll(xr, SCAN_ROWS - k, 0), 0.0)
                    si = jnp.where(keep, pltpu.roll(xi, SCAN_ROWS - k, 0), 0.0)
                else:
                    keep = rows >= k
                    sr = jnp.where(keep, pltpu.roll(xr, k, 0), 0.0)
                    si = jnp.where(keep, pltpu.roll(xi, k, 0), 0.0)
                xr, xi = xr + (ar * sr - ai * si), xi + (ar * si + ai * sr)
            xr, xi = xr + (prv * cr - piv * ci), xi + (prv * ci + piv * cr)
            xr_ref[sl, :] = xr
            xi_ref[sl, :] = xi
            if with_states:
                last = rows == SCAN_ROWS - 1
                nr = jnp.where(last, cr, pltpu.roll(xr, SCAN_ROWS - 1, 0))
                ni = jnp.where(last, ci, pltpu.roll(xi, SCAN_ROWS - 1, 0))
                sr, si = sr_ref[sl, :], si_ref[sl, :]
                acr, aci = acr + (sr * nr + si * ni), aci + (sr * ni - si * nr)
            e = 0 if reverse else SCAN_ROWS - 1
            return jnp.broadcast_to(xr[e:e + 1], shape), jnp.broadcast_to(xi[e:e + 1], shape), acr, aci

        cr, ci, acr, aci = lax.fori_loop(0, ntile, tile, (cr_sc[...], ci_sc[...], ar_sc[...], ai_sc[...]))
        cr_sc[...] = cr
        ci_sc[...] = ci
        if with_states:
            ar_sc[...] = acr
            ai_sc[...] = aci

            @pl.when(pl.program_id(1) == nt - 1)
            def _():
                dr_ref[...] = acr
                di_ref[...] = aci

    tmap = (lambda l, t: (nt - 1 - t, l)) if reverse else (lambda l, t: (t, l))
    blk = pl.BlockSpec((tb, SCAN_LANES), tmap)
    tab = pl.BlockSpec((SCAN_ROWS, SCAN_LANES), lambda l, t: (0, l))
    full = jax.ShapeDtypeStruct((s, n), F32)
    return _pcall(
        body,
        name=name,
        grid=(nl, nt),
        in_specs=[blk, blk, tab, tab] + ([blk, blk] if with_states else []),
        out_specs=[blk, blk] + ([tab, tab] if with_states else []),
        out_shape=[full, full] + ([jax.ShapeDtypeStruct((SCAN_ROWS, n), F32)] * 2 if with_states else []),
        scratch_shapes=[pltpu.VMEM(shape, F32)] * 4,
        compiler_params=_params(("parallel", "arbitrary")),
    )(ur, ui, pr, pi, *(states if with_states else ()))


@functools.partial(jax.custom_vjp, nondiff_argnums=(4,))
def s5_scan(ur, ui, lr, li, name):
    return tuple(_scan_call(ur, ui, lr, li, False, name + "_f"))


def _s5_scan_fwd(ur, ui, lr, li, name):
    xr, xi = _scan_call(ur, ui, lr, li, False, name + "_f")
    return (xr, xi), (xr, xi, lr, li)


def _s5_scan_bwd(name, res, g):
    xr, xi, lr, li = res
    gr, gi, dlr, dli = _scan_call(g[0], g[1], lr, -li, True, name + "_b", states=(xr, xi))
    return gr, gi, jnp.sum(dlr, axis=0, keepdims=True), jnp.sum(dli, axis=0, keepdims=True)


s5_scan.defvjp(_s5_scan_fwd, _s5_scan_bwd)


def _hg_chunk(q, k, v, lf, st):
    c = HG_CHUNK
    row = lax.broadcasted_iota(jnp.int32, (c, c), 0)
    col = lax.broadcasted_iota(jnp.int32, (c, c), 1)
    tri = (col <= row).astype(F32)
    b = jnp.dot(tri, lf, precision=lax.Precision.HIGHEST, preferred_element_type=F32)
    qb = q * jnp.exp(b)
    o = lax.dot_general(qb.astype(BF16), st.astype(BF16), (((1,), (1,)), ((), ())), preferred_element_type=F32)
    nsub = c // HG_SUB
    srow = lax.broadcasted_iota(jnp.int32, (HG_SUB, HG_SUB, 1), 0)
    scol = lax.broadcasted_iota(jnp.int32, (HG_SUB, HG_SUB, 1), 1)
    smask = scol <= srow
    outs = []
    for i in range(nsub):
        lo = i * HG_SUB
        bi, qi, ki = b[lo:lo + HG_SUB], q[lo:lo + HG_SUB], k[lo:lo + HG_SUB]
        diff = jnp.where(smask, bi[:, None, :] - bi[None, :, :], 0.0)
        e = jnp.where(smask, jnp.exp(diff), 0.0)
        a_ii = jnp.sum(qi[:, None, :] * ki[None, :, :] * e, axis=-1)
        oi = jnp.dot(a_ii.astype(BF16), v[lo:lo + HG_SUB].astype(BF16), preferred_element_type=F32)
        if i > 0:
            r = b[lo - 1:lo]
            qt = qi * jnp.exp(bi - r)
            kt = k[:lo] * jnp.exp(r - b[:lo])
            a_ij = lax.dot_general(qt.astype(BF16), kt.astype(BF16), (((1,), (1,)), ((), ())),
                                   preferred_element_type=F32)
            oi = oi + jnp.dot(a_ij.astype(BF16), v[:lo].astype(BF16), preferred_element_type=F32)
        outs.append(oi)
    o = o + jnp.concatenate(outs, axis=0)
    bl = b[c - 1:c]
    kd = k * jnp.exp(bl - b)
    st_new = st * jnp.exp(bl) + lax.dot_general(v.astype(BF16), kd.astype(BF16), (((0,), (0,)), ((), ())),
                                                preferred_element_type=F32)
    return o, st_new


def _hg_fwd_call(q, k, v, lf, name):
    s, w = q.shape
    nc = s // HG_CHUNK
    blk = pl.BlockSpec((HG_CHUNK, w), lambda i: (i, 0))

    def body(q_ref, k_ref, v_ref, lf_ref, o_ref, st_ref, st_sc):
        @pl.when(pl.program_id(0) == 0)
        def _():
            st_sc[...] = jnp.zeros(st_sc.shape, F32)

        for hh in range(HG_HEADS):
            cs = slice(hh * HG_DK, (hh + 1) * HG_DK)
            st = st_sc[hh]
            st_ref[0, hh] = st
            o, st_new = _hg_chunk(q_ref[:, cs], k_ref[:, cs], v_ref[:, cs], lf_ref[:, cs], st)
            o_ref[:, cs] = o
            st_sc[hh] = st_new

    return _pcall(
        body,
        name=name,
        grid=(nc,),
        in_specs=[blk, blk, blk, blk],
        out_specs=[blk, pl.BlockSpec((1, HG_HEADS, HG_DV, HG_DK), lambda i: (i, 0, 0, 0))],
        out_shape=[jax.ShapeDtypeStruct((s, w), F32), jax.ShapeDtypeStruct((nc, HG_HEADS, HG_DV, HG_DK), F32)],
        scratch_shapes=[pltpu.VMEM((HG_HEADS, HG_DV, HG_DK), F32)],
        compiler_params=_params(("arbitrary",)),
    )(q, k, v, lf)


def _hg_bwd_call(q, k, v, lf, sts, do, name):
    s, w = q.shape
    nc = s // HG_CHUNK
    blk = pl.BlockSpec((HG_CHUNK, w), lambda i: (nc - 1 - i, 0))

    def body(q_ref, k_ref, v_ref, lf_ref, st_ref, do_ref, dq_ref, dk_ref, dv_ref, dlf_ref, dst_sc):
        @pl.when(pl.program_id(0) == 0)
        def _():
            dst_sc[...] = jnp.zeros(dst_sc.shape, F32)

        for hh in range(HG_HEADS):
            cs = slice(hh * HG_DK, (hh + 1) * HG_DK)
            _, vjp = jax.vjp(_hg_chunk, q_ref[:, cs], k_ref[:, cs], v_ref[:, cs], lf_ref[:, cs], st_ref[0, hh])
            dq, dk, dv, dlf, dst = vjp((do_ref[:, cs], dst_sc[hh]))
            dq_ref[:, cs] = dq
            dk_ref[:, cs] = dk
            dv_ref[:, cs] = dv
            dlf_ref[:, cs] = dlf
            dst_sc[hh] = dst

    out = jax.ShapeDtypeStruct((s, w), F32)
    return _pcall(
        body,
        name=name,
        grid=(nc,),
        in_specs=[blk, blk, blk, blk, pl.BlockSpec((1, HG_HEADS, HG_DV, HG_DK), lambda i: (nc - 1 - i, 0, 0, 0)), blk],
        out_specs=[blk, blk, blk, blk],
        out_shape=[out, out, out, out],
        scratch_shapes=[pltpu.VMEM((HG_HEADS, HG_DV, HG_DK), F32)],
        compiler_params=_params(("arbitrary",)),
    )(q, k, v, lf, sts, do)


@functools.partial(jax.custom_vjp, nondiff_argnums=(4,))
def hgrn_core(q, k, v, lf, name):
    return _hg_fwd_call(q, k, v, lf, name + "_f")[0]


def _hgrn_core_fwd(q, k, v, lf, name):
    o, sts = _hg_fwd_call(q, k, v, lf, name + "_f")
    return o, (q, k, v, lf, sts)


def _hgrn_core_bwd(name, res, do):
    q, k, v, lf, sts = res
    return tuple(_hg_bwd_call(q, k, v, lf, sts, do, name + "_b"))


hgrn_core.defvjp(_hgrn_core_fwd, _hgrn_core_bwd)


def _adamw_call(w, g, m, v, name):
    rows, cols = w.shape
    tr = _tile(rows, 256, 8)
    c1 = 1.0 / (1.0 - ADAM_B1 ** ADAM_STEP)
    c2 = 1.0 / (1.0 - ADAM_B2 ** ADAM_STEP)

    def body(w_ref, g_ref, m_ref, v_ref, d_ref, mo_ref, vo_ref):
        gv = g_ref[...]
        mn = ADAM_B1 * m_ref[...] + (1.0 - ADAM_B1) * gv
        vn = ADAM_B2 * v_ref[...] + (1.0 - ADAM_B2) * (gv * gv)
        d_ref[...] = -ADAM_LR * ((mn * c1) / (jnp.sqrt(vn * c2) + ADAM_EPS) + ADAM_WD * w_ref[...])
        mo_ref[...] = mn
        vo_ref[...] = vn

    blk = pl.BlockSpec((tr, cols), lambda i: (i, 0))
    out = jax.ShapeDtypeStruct((rows, cols), F32)
    return _pcall(
        body,
        name=name,
        grid=(rows // tr,),
        in_specs=[blk, blk, blk, blk],
        out_specs=[blk, blk, blk],
        out_shape=[out, out, out],
        compiler_params=_params(("parallel",)),
    )(w, g, m, v)


ANY = pl.BlockSpec(memory_space=pl.ANY)


def _place():
    return lax.axis_index("x"), lax.axis_index("y"), lax.axis_index("c")


def _other_chips(x, y):
    return [(1 - x, y), (x, 1 - y), (1 - x, 1 - y)]


def _gather_weights(shards):
    nw = len(shards)

    def body(*refs):
        w_refs, g_refs, (send_sems, recv_sems) = refs[:nw], refs[nw:2 * nw], refs[2 * nw:]
        x, y, c = _place()
        myq = 2 * x + y
        sibling = (x, y, 1 - c)
        chips = _other_chips(x, y)

        def copy(k, src, dst, to):
            return pltpu.make_async_remote_copy(src_ref=src, dst_ref=dst, send_sem=send_sems.at[k],
                                                recv_sem=recv_sems.at[k], device_id=to, device_id_type=MESH)

        def halves(i):
            hh = shards[i].shape[0] // 2
            return pl.ds(c * hh, hh), pl.ds((1 - c) * hh, hh)

        first = []
        for i in range(nw):
            mine, _ = halves(i)
            for j, (px, py) in enumerate(chips):
                first.append(copy(6 * i + j, w_refs[i].at[mine], g_refs[i].at[myq, mine], (px, py, c)))
                first[-1].start()
        passed = []
        for i in range(nw):
            mine, _ = halves(i)
            for j, (px, py) in enumerate(chips):
                q = 2 * px + py
                copy(6 * i + j, w_refs[i].at[mine], g_refs[i].at[q, mine], (px, py, c)).wait_recv()
                passed.append(copy(6 * i + 3 + j, g_refs[i].at[q, mine], g_refs[i].at[q, mine], sibling))
                passed[-1].start()
        for i in range(nw):
            _, theirs = halves(i)
            for j, (px, py) in enumerate(chips):
                q = 2 * px + py
                copy(6 * i + 3 + j, g_refs[i].at[q, theirs], g_refs[i].at[q, theirs], sibling).wait_recv()
        for cp in first + passed:
            cp.wait_send()

    got = _pcall(
        body,
        name="gather_weights",
        in_specs=[ANY] * nw,
        out_specs=[ANY] * nw,
        out_shape=[jax.ShapeDtypeStruct((4,) + s.shape, s.dtype) for s in shards],
        scratch_shapes=[pltpu.SemaphoreType.DMA((6 * nw,)), pltpu.SemaphoreType.DMA((6 * nw,))],
    )(*shards)
    my_chip = 2 * lax.axis_index("x") + lax.axis_index("y")
    return [lax.dynamic_update_slice(g, s[None], (my_chip, 0, 0)) for g, s in zip(got, shards)]


def _swap_halves_to_sibling(grs):
    nw = len(grs)

    def body(*refs):
        g_refs, t_refs, (send_sems, recv_sems) = refs[:nw], refs[nw:2 * nw], refs[2 * nw:]
        x, y, c = _place()
        cps = []
        for i in range(nw):
            hh = grs[i].shape[1] // 2
            cps.append(pltpu.make_async_remote_copy(
                src_ref=g_refs[i].at[:, pl.ds((1 - c) * hh, hh)], dst_ref=t_refs[i], send_sem=send_sems.at[i],
                recv_sem=recv_sems.at[i], device_id=(x, y, 1 - c), device_id_type=MESH))
            cps[-1].start()
        for cp in cps:
            cp.wait()

    return _pcall(
        body,
        name="grad_d2d",
        in_specs=[ANY] * nw,
        out_specs=[ANY] * nw,
        out_shape=[jax.ShapeDtypeStruct((g.shape[0], g.shape[1] // 2, g.shape[2]), g.dtype) for g in grs],
        scratch_shapes=[pltpu.SemaphoreType.DMA((nw,)), pltpu.SemaphoreType.DMA((nw,))],
    )(*grs)


def _add_half(gr, t1, place, name):
    nq, r, wd = gr.shape
    hh = r // 2
    tr = _tile(hh, 512, 16)
    nb = hh // tr

    def body(p_ref, g_ref, t_ref, o_ref):
        o_ref[...] = (g_ref[...] + t_ref[...]).astype(BF16)

    return _pcall(
        body,
        name=name,
        grid_spec=pltpu.PrefetchScalarGridSpec(
            num_scalar_prefetch=1,
            grid=(nq, nb),
            in_specs=[
                pl.BlockSpec((1, tr, wd), lambda q, i, p_ref: (q, p_ref[0] * nb + i, 0)),
                pl.BlockSpec((1, tr, wd), lambda q, i, p_ref: (q, i, 0)),
            ],
            out_specs=pl.BlockSpec((1, tr, wd), lambda q, i, p_ref: (q, i, 0)),
        ),
        out_shape=jax.ShapeDtypeStruct((nq, hh, wd), BF16),
        compiler_params=_params(("parallel", "parallel")),
    )(place, gr, t1)


def _scatter_partials(ps):
    nw = len(ps)

    def body(*refs):
        p_refs, t_refs, (send_sems, recv_sems) = refs[:nw], refs[nw:2 * nw], refs[2 * nw:]
        x, y, c = _place()
        cps = []
        for i in range(nw):
            for j, (px, py) in enumerate(_other_chips(x, y)):
                cps.append(pltpu.make_async_remote_copy(
                    src_ref=p_refs[i].at[2 * px + py], dst_ref=t_refs[i].at[j], send_sem=send_sems.at[3 * i + j],
                    recv_sem=recv_sems.at[3 * i + j], device_id=(px, py, c), device_id_type=MESH))
                cps[-1].start()
        for cp in cps:
            cp.wait()

    return _pcall(
        body,
        name="grad_ici",
        in_specs=[ANY] * nw,
        out_specs=[ANY] * nw,
        out_shape=[jax.ShapeDtypeStruct((3,) + p.shape[1:], p.dtype) for p in ps],
        scratch_shapes=[pltpu.SemaphoreType.DMA((3 * nw,)), pltpu.SemaphoreType.DMA((3 * nw,))],
    )(*ps)


def _add_partials(gr, t1, t2, place, name):
    nq, r, wd = gr.shape
    hh = r // 2
    tr = _tile(hh, 512, 16)
    nb = hh // tr

    def body(p_ref, g_ref, s_ref, t_ref, o_ref):
        o_ref[...] = (((g_ref[0] + s_ref[0]) + t_ref[0].astype(F32)) + t_ref[1].astype(F32)) + t_ref[2].astype(F32)

    return _pcall(
        body,
        name=name,
        grid_spec=pltpu.PrefetchScalarGridSpec(
            num_scalar_prefetch=1,
            grid=(nb,),
            in_specs=[
                pl.BlockSpec((1, tr, wd), lambda i, p_ref: (p_ref[1], p_ref[0] * nb + i, 0)),
                pl.BlockSpec((1, tr, wd), lambda i, p_ref: (p_ref[1], i, 0)),
                pl.BlockSpec((3, tr, wd), lambda i, p_ref: (0, i, 0)),
            ],
            out_specs=pl.BlockSpec((tr, wd), lambda i, p_ref: (i, 0)),
        ),
        out_shape=jax.ShapeDtypeStruct((hh, wd), F32),
        compiler_params=_params(("parallel",)),
    )(place, gr, t1, t2)


def _join_halves(halves):
    nw = len(halves)

    def body(*refs):
        h_refs, f_refs, (send_sems, recv_sems) = refs[:nw], refs[nw:2 * nw], refs[2 * nw:]
        x, y, c = _place()

        def copy(i, rows):
            hh = halves[i].shape[0]
            return pltpu.make_async_remote_copy(src_ref=h_refs[i], dst_ref=f_refs[i].at[pl.ds(rows * hh, hh)],
                                                send_sem=send_sems.at[i], recv_sem=recv_sems.at[i],
                                                device_id=(x, y, 1 - c), device_id_type=MESH)

        for i in range(nw):
            copy(i, c).start()
        for i in range(nw):
            copy(i, c).wait_send()
            copy(i, 1 - c).wait_recv()

    got = _pcall(
        body,
        name="grad_join",
        in_specs=[ANY] * nw,
        out_specs=[ANY] * nw,
        out_shape=[jax.ShapeDtypeStruct((2 * h.shape[0], h.shape[1]), h.dtype) for h in halves],
        scratch_shapes=[pltpu.SemaphoreType.DMA((nw,)), pltpu.SemaphoreType.DMA((nw,))],
    )(*halves)
    c = lax.axis_index("c")
    return [lax.dynamic_update_slice(g, h, (c * h.shape[0], 0)) for g, h in zip(got, halves)]


N_DEV = 8


def _allreduce_small(buf):
    rows, wd = buf.shape
    n = rows // N_DEV

    def body(x_ref, o_ref, stage_ref, send_sems, recv_sems):
        x, y, c = _place()
        me = 4 * x + 2 * y + c

        def peer(k):
            return (x ^ (k >> 2), y ^ ((k >> 1) & 1), c ^ (k & 1))

        def chunk(ref, d):
            return ref.at[pl.ds(pl.multiple_of(d * n, 8), n)]

        first = []
        for k in range(1, N_DEV):
            px, py, pc = peer(k)
            cp = pltpu.make_async_remote_copy(src_ref=chunk(x_ref, 4 * px + 2 * py + pc), dst_ref=stage_ref.at[k],
                                              send_sem=send_sems.at[k - 1], recv_sem=recv_sems.at[k - 1],
                                              device_id=(px, py, pc), device_id_type=MESH)
            cp.start()
            first.append(cp)
        acc = chunk(x_ref, me)[...]
        for k in range(1, N_DEV):
            first[k - 1].wait_recv()
            acc = acc + stage_ref[k]
        chunk(o_ref, me)[...] = acc
        second = []
        for k in range(1, N_DEV):
            cp = pltpu.make_async_remote_copy(src_ref=chunk(o_ref, me), dst_ref=chunk(o_ref, me),
                                              send_sem=send_sems.at[6 + k], recv_sem=recv_sems.at[6 + k],
                                              device_id=peer(k), device_id_type=MESH)
            cp.start()
            second.append(cp)
        for k in range(1, N_DEV):
            px, py, pc = peer(k)
            them = 4 * px + 2 * py + pc
            pltpu.make_async_remote_copy(src_ref=chunk(o_ref, them), dst_ref=chunk(o_ref, them),
                                         send_sem=send_sems.at[6 + k], recv_sem=recv_sems.at[6 + k],
                                         device_id=(px, py, pc), device_id_type=MESH).wait_recv()
        for cp in first + second:
            cp.wait_send()

    vmem = pl.BlockSpec(memory_space=pltpu.VMEM)
    return _pcall(
        body,
        name="allreduce_small",
        in_specs=[vmem],
        out_specs=vmem,
        out_shape=jax.ShapeDtypeStruct((rows, wd), F32),
        scratch_shapes=[pltpu.VMEM((N_DEV, n, wd), F32), pltpu.SemaphoreType.DMA((14,)),
                        pltpu.SemaphoreType.DMA((14,))],
    )(buf)


@functools.partial(jax.custom_vjp, nondiff_argnums=(1,))
def split_cols(z, spec):
    outs, off = [], 0
    for n in spec[0]:
        outs.append(z[:, off:off + n])
        off += n
    return tuple(outs)


def _split_cols_fwd(z, spec):
    return split_cols(z, spec), None


def _split_cols_bwd(spec, _, gs):
    parts = list(gs)
    if spec[1]:
        parts.append(jnp.zeros((parts[0].shape[0], spec[1]), parts[0].dtype))
    return (jnp.concatenate(parts, axis=1),)


split_cols.defvjp(_split_cols_fwd, _split_cols_bwd)


def _rope(t, cos, sin):
    half = t.shape[-1] // 2
    t1, t2 = t[..., :half], t[..., half:]
    return jnp.concatenate([t1 * cos - t2 * sin, t2 * cos + t1 * sin], axis=-1)


def _s5_mixer(u, sp, l, w16, toks):
    g, p_, h = SSM_GROUPS, SSM_STATE, SSM_GROUP_CH
    lam_re, lam_im = sp["ssm_lam_re", l], sp["ssm_lam_im", l]
    step = jnp.exp(sp["ssm_log_step", l])[:, None]
    mag, ang = jnp.exp(lam_re * step), lam_im * step
    lbr, lbi = mag * jnp.cos(ang), mag * jnp.sin(ang)
    den = lam_re * lam_re + lam_im * lam_im
    nr, ni = lbr - 1.0, lbi
    qr, qi = (nr * lam_re + ni * lam_im) / den, (ni * lam_re - nr * lam_im) / den
    b_re, b_im = sp["ssm_b_re", l], sp["ssm_b_im", l]
    bbr = qr[..., None] * b_re - qi[..., None] * b_im
    bbi = qr[..., None] * b_im + qi[..., None] * b_re
    nb = g // SSM_PACK
    eye = jnp.eye(SSM_PACK, dtype=F32)

    def in_blocks(t):
        return jnp.einsum("jgph,gk->jghkp", t.reshape(nb, SSM_PACK, p_, h), eye).reshape(nb, SSM_PACK * h, SSM_PACK * p_)

    def out_blocks(t):
        return jnp.einsum("jghp,gk->jgpkh", t.reshape(nb, SSM_PACK, h, p_), eye).reshape(nb, SSM_PACK * p_, SSM_PACK * h)

    nm = "ssm%d" % l
    ur = bd_matmul(u, in_blocks(bbr), nm + "_bur")
    ui = bd_matmul(u, in_blocks(bbi), nm + "_bui")
    xr, xi = s5_scan(ur, ui, lbr.reshape(1, g * p_), lbi.reshape(1, g * p_), nm + "_scan")
    y = (bd_matmul(xr, out_blocks(sp["ssm_c_re", l]), nm + "_cr") + bd_matmul(xi, out_blocks(-sp["ssm_c_im", l]), nm + "_ci")
         + sp["ssm_d", l].reshape(1, g * h) * u)
    y = jax.nn.gelu(y)
    zz = matmul_w(y, w16["ssm_w_glu", l], toks["ssm_w_glu", l], nm + "_glu")
    z_out, z_gate = split_cols(zz, ((D_MODEL, D_MODEL), 0))
    return z_out * jax.nn.sigmoid(z_gate)


def _mla_mixer(q_lat, kv_lat, k_rope, cos, sin, sp, l, w16, toks):
    s = q_lat.shape[0]
    nm = "mla%d" % l
    dqk = MLA_NOPE + MLA_ROPE
    qt = matmul_w_t(rmsnorm(q_lat, sp["mla_q_norm", l], nm + "_qn"), w16["mla_w_uq", l], toks["mla_w_uq", l],
                    nm + "_uq").reshape(MLA_HEADS, dqk, s)
    half = MLA_ROPE // 2
    q1, q2 = qt[:, MLA_NOPE:MLA_NOPE + half], qt[:, MLA_NOPE + half:]
    ct, st = cos.T[None], sin.T[None]
    qt = jnp.concatenate([qt[:, :MLA_NOPE], q1 * ct - q2 * st, q2 * ct + q1 * st], axis=1).reshape(MLA_HEADS * dqk, s)
    kvt = matmul_w_t(rmsnorm(kv_lat, sp["mla_kv_norm", l], nm + "_kvn"), w16["mla_w_ukv", l], toks["mla_w_ukv", l],
                     nm + "_ukv").reshape(MLA_HEADS, MLA_NOPE + MLA_V, s)
    k_pe = _rope(k_rope, cos, sin)
    k = jnp.concatenate([kvt[:, :MLA_NOPE].transpose(0, 2, 1),
                         jnp.broadcast_to(k_pe[None], (MLA_HEADS, s, MLA_ROPE))], axis=-1)
    vt = kvt[:, MLA_NOPE:].reshape(MLA_HEADS * MLA_V, s)
    ot = attention(qt, k, vt, 1.0 / math.sqrt(dqk), True, nm + "_att")
    return matmul_w_at(ot, w16["mla_w_o", l], toks["mla_w_o", l], nm + "_o")


def _hgrn_mixer(q, f_logit, i_in, g, lb, sp, l, w16, toks):
    s = q.shape[0]
    nm = "hg%d" % l
    f = lb + (1.0 - lb) * jax.nn.sigmoid(f_logit)
    o = hgrn_core(jax.nn.silu(q), 1.0 - f, i_in, jnp.log(f), nm + "_core")
    o = rmsnorm(o.reshape(s * HG_HEADS, HG_DV), sp["hg_g_norm", l], nm + "_gn").reshape(s, HG_HEADS * HG_DV)
    return matmul_w(o * jax.nn.silu(g), w16["hg_w_o", l], toks["hg_w_o", l], nm + "_o")


def _cross_attention(h, mem_n, l, w16, toks):
    s, m = h.shape[0], mem_n.shape[0]
    nm = "xat%d" % l
    xw = X_HEADS * X_HEAD_DIM
    qt = matmul_w_t(h, w16["x_w_q", l], toks["x_w_q", l], nm + "_q")
    kv = matmul_w(mem_n, w16["x_w_kv", l], toks["x_w_kv", l], nm + "_kv")
    k, v = split_cols(kv, ((xw, xw), 0))
    k = k.reshape(m, X_HEADS, X_HEAD_DIM).transpose(1, 0, 2)
    ot = attention(qt, k, v.T, 1.0 / math.sqrt(X_HEAD_DIM), False, nm + "_att")
    return matmul_w_at(ot, w16["x_w_o", l], toks["x_w_o", l], nm + "_o")


_IN_HG = SSM_WIDTH + MLA_Q_RANK + MLA_KV_RANK
_IN_GATE = _IN_HG + 4 * HG_HEADS * HG_DK
_IN_ROPE = _IN_GATE + 3 * D_MODEL


def _perm_in(w):
    pad = jnp.zeros(w.shape[:-1] + (D_IN_PAD - D_IN,), w.dtype)
    return jnp.concatenate([w[..., :_IN_HG], w[..., _IN_HG + MLA_ROPE:], w[..., _IN_HG:_IN_HG + MLA_ROPE], pad], axis=-1)


def _unperm_in(g):
    return jnp.concatenate([g[..., :_IN_HG], g[..., _IN_ROPE:_IN_ROPE + MLA_ROPE], g[..., _IN_HG:_IN_ROPE]], axis=-1)


def _layer(x, mem, cos, sin, lb, l, sp, w16, toks):
    s = x.shape[0]
    nm = "l%d" % l
    h = rmsnorm(x, sp["norm_mix", l], nm + "_nmix")
    z = matmul_w(h, w16["w_in", l], toks["w_in", l], nm + "_in")
    hw = HG_HEADS * HG_DK
    sizes = (SSM_WIDTH, MLA_Q_RANK, MLA_KV_RANK, hw, hw, hw, hw, 3 * D_MODEL, MLA_ROPE)
    u, q_lat, kv_lat, hg_q, hg_f, hg_i, hg_g, gate_logits, k_rope = split_cols(z, (sizes, D_IN_PAD - D_IN))
    g_ssm, g_mla, g_hg = split_cols(jax.nn.sigmoid(gate_logits), ((D_MODEL,) * 3, 0))
    y_ssm = _s5_mixer(u, sp, l, w16, toks)
    y_mla = _mla_mixer(q_lat, kv_lat, k_rope, cos, sin, sp, l, w16, toks)
    y_hg = _hgrn_mixer(hg_q, hg_f, hg_i, hg_g, lb, sp, l, w16, toks)
    merged = g_ssm * y_ssm + g_mla * y_mla + g_hg * y_hg
    x = x + matmul_w(merged, w16["w_out", l], toks["w_out", l], nm + "_out")
    hc = rmsnorm(x, sp["norm_cross", l], nm + "_ncross")
    mem_n = rmsnorm(mem, sp["norm_mem", l], nm + "_nmem")
    x = x + _cross_attention(hc, mem_n, l, w16, toks)
    hf = rmsnorm(x, sp["norm_ffn", l], nm + "_nffn")
    gu = matmul_w(hf, w16["ffn_w_gate_up", l], toks["ffn_w_gate_up", l], nm + "_gu")
    gt, up = split_cols(gu, ((D_FF, D_FF), 0))
    return x + matmul_w(jax.nn.silu(gt) * up, w16["ffn_w_down", l], toks["ffn_w_down", l], nm + "_down")


def _loss_fn(sp, toks, x, w16, mem, pos, target):
    half = MLA_ROPE // 2
    inv_freq = ROPE_THETA ** (-jnp.arange(half, dtype=F32) / half)
    ang = pos.astype(F32)[:, None] * inv_freq
    cos, sin = jnp.cos(ang), jnp.sin(ang)
    lb_p = jax.nn.softmax(sp["hg_lb", ALL], axis=0)
    lower = jnp.cumsum(lb_p, axis=0) - lb_p[0:1]
    for l in range(DEPTH):
        x = _layer(x, mem, cos, sin, lower[l], l, sp, w16, toks)
    y = rmsnorm(x, sp["norm_final", ALL], "nfinal")
    return 0.5 * jnp.sum(jnp.mean(jnp.square(y - target), axis=-1))


WEIGHTS = ["norm_mix", "w_in", "ssm_lam_re", "ssm_lam_im", "ssm_b_re", "ssm_b_im", "ssm_c_re", "ssm_c_im", "ssm_d",
           "ssm_log_step", "ssm_w_glu", "mla_q_norm", "mla_kv_norm", "mla_w_uq", "mla_w_ukv", "mla_w_o", "hg_lb",
           "hg_g_norm", "hg_w_o", "w_out", "norm_cross", "norm_mem", "x_w_q", "x_w_kv", "x_w_o", "norm_ffn",
           "ffn_w_gate_up", "ffn_w_down", "norm_final"]
INPUTS = ["x", "mem", "positions"] + WEIGHTS + ["loss_target"] + ["m_" + n for n in WEIGHTS] + ["v_" + n for n in WEIGHTS]
BIG = [("w_in", 2), ("ssm_w_glu", 2), ("mla_w_uq", 2), ("mla_w_ukv", 2), ("mla_w_o", 2), ("hg_w_o", 2), ("w_out", 1),
       ("x_w_q", 1), ("x_w_kv", 1), ("x_w_o", 2), ("ffn_w_gate_up", 2), ("ffn_w_down", 1)]
SMALL = [n for n in WEIGHTS if n not in dict(BIG)]
WHOLE = ("hg_lb", "norm_final")
ALL = -1
SMALL_W = 128
N_CHIPS = 4


def _pack(arrs, rows, width, dtype):
    flat = jnp.concatenate([a.astype(dtype).reshape(-1) for a in arrs])
    return jnp.pad(flat, (0, rows * width - flat.shape[0])).reshape(rows, width)


def _unpack(buf, like):
    flat = buf.reshape(-1)
    out, off = [], 0
    for a in like:
        n = math.prod(a.shape)
        out.append(flat[off:off + n].reshape(a.shape))
        off += n
    return out


def kernel(
    x, mem, positions, norm_mix, w_in, ssm_lam_re, ssm_lam_im, ssm_b_re, ssm_b_im, ssm_c_re, ssm_c_im, ssm_d,
    ssm_log_step, ssm_w_glu, mla_q_norm, mla_kv_norm, mla_w_uq, mla_w_ukv, mla_w_o, hg_lb, hg_g_norm, hg_w_o,
    w_out, norm_cross, norm_mem, x_w_q, x_w_kv, x_w_o, norm_ffn, ffn_w_gate_up, ffn_w_down, norm_final,
    loss_target, m_norm_mix, m_w_in, m_ssm_lam_re, m_ssm_lam_im, m_ssm_b_re, m_ssm_b_im, m_ssm_c_re, m_ssm_c_im,
    m_ssm_d, m_ssm_log_step, m_ssm_w_glu, m_mla_q_norm, m_mla_kv_norm, m_mla_w_uq, m_mla_w_ukv, m_mla_w_o,
    m_hg_lb, m_hg_g_norm, m_hg_w_o, m_w_out, m_norm_cross, m_norm_mem, m_x_w_q, m_x_w_kv, m_x_w_o, m_norm_ffn,
    m_ffn_w_gate_up, m_ffn_w_down, m_norm_final, v_norm_mix, v_w_in, v_ssm_lam_re, v_ssm_lam_im, v_ssm_b_re,
    v_ssm_b_im, v_ssm_c_re, v_ssm_c_im, v_ssm_d, v_ssm_log_step, v_ssm_w_glu, v_mla_q_norm, v_mla_kv_norm,
    v_mla_w_uq, v_mla_w_ukv, v_mla_w_o, v_hg_lb, v_hg_g_norm, v_hg_w_o, v_w_out, v_norm_cross, v_norm_mem,
    v_x_w_q, v_x_w_kv, v_x_w_o, v_norm_ffn, v_ffn_w_gate_up, v_ffn_w_down, v_norm_final):
    given = locals()
    a = {n: given[n] for n in INPUTS}
    x, mem, pos, target = a["x"][0], a["mem"][0], a["positions"][0], a["loss_target"][0]
    place = jnp.stack([lax.axis_index("c"), 2 * lax.axis_index("x") + lax.axis_index("y")]).astype(jnp.int32)

    two = lambda t: t.reshape(t.shape[0] * t.shape[1], t.shape[2])
    gathered = _gather_weights([two(a[n]).astype(BF16) for n, _ in BIG])
    w16, toks = {}, {}
    for (n, ax), got in zip(BIG, gathered):
        per = a[n].shape[1]
        for l in range(DEPTH):
            blk = [got[q, l * per:(l + 1) * per] for q in range(N_CHIPS)]
            if n == "w_in":
                cut = _IN_HG + MLA_ROPE
                pad = jnp.zeros((per, D_IN_PAD - D_IN), BF16)
                full = jnp.concatenate([blk[0][:, :_IN_HG], blk[0][:, cut:], blk[1], blk[2], blk[3],
                                        blk[0][:, _IN_HG:cut], pad], axis=1)
            else:
                full = jnp.concatenate(blk, axis=ax - 1)
            w16[n, l] = full
            toks[n, l] = jnp.zeros(full.shape, F32)
    sp = {}
    for n in SMALL:
        if n in WHOLE:
            sp[n, ALL] = a[n]
        else:
            for l in range(DEPTH):
                sp[n, l] = a[n][l]

    loss, (g_sp, g_tok, g_x) = jax.value_and_grad(_loss_fn, argnums=(0, 1, 2))(sp, toks, x, w16, mem, pos, target)
    g_small = {n: g_sp[n, ALL] if n in WHOLE else jnp.stack([g_sp[n, l] for l in range(DEPTH)]) for n in SMALL}

    grads = []
    for n, ax in BIG:
        layers = []
        for l in range(DEPTH):
            g = _unperm_in(g_tok[n, l]) if n == "w_in" else g_tok[n, l]
            if ax == 2:
                layers.append(g.reshape(g.shape[0], N_CHIPS, g.shape[1] // N_CHIPS).transpose(1, 0, 2))
            else:
                layers.append(g.reshape(N_CHIPS, g.shape[0] // N_CHIPS, g.shape[1]))
        grads.append(jnp.concatenate(layers, axis=1))
    names = [n for n, _ in BIG]
    from_sibling = _swap_halves_to_sibling(grads)
    chip_sums = [_add_half(g, t, place, "grad_add_half_" + n) for n, g, t in zip(names, grads, from_sibling)]
    from_chips = _scatter_partials(chip_sums)
    totals = [_add_partials(g, t1, t2, place, "grad_add_partials_" + n)
              for n, g, t1, t2 in zip(names, grads, from_sibling, from_chips)]
    g_big = {n: g.reshape(a[n].shape) for n, g in zip(names, _join_halves(totals))}

    small_like = [a[n] for n in SMALL] + [jnp.zeros((1,), F32)]
    n_small = sum(math.prod(t.shape) for t in small_like)
    srows = -(-n_small // (SMALL_W * 8 * N_DEV)) * 8 * N_DEV
    reduced = _allreduce_small(_pack([g_small[n] for n in SMALL] + [loss.reshape(1)], srows, SMALL_W, F32))
    red = _unpack(reduced, small_like)
    g_sm = dict(zip(SMALL, red[:-1]))
    loss = red[-1].reshape(())

    out_g, out_d, out_m, out_v = {}, {}, {}, {}
    for n, _ in BIG:
        shp = a[n].shape
        two = lambda t: t.reshape(shp[0] * shp[1], shp[2])
        d, m, v = _adamw_call(two(a[n]), two(g_big[n]), two(a["m_" + n]), two(a["v_" + n]), "adamw_" + n)
        out_g[n], out_d[n], out_m[n], out_v[n] = g_big[n], d.reshape(shp), m.reshape(shp), v.reshape(shp)
    small_arrs = [a[n] for n in SMALL]
    d, m, v = _adamw_call(
        _pack(small_arrs, srows, SMALL_W, F32), reduced,
        _pack([a["m_" + n] for n in SMALL], srows, SMALL_W, F32),
        _pack([a["v_" + n] for n in SMALL], srows, SMALL_W, F32), "adamw_small")
    for n, gg, dd, mm_, vv in zip(SMALL, red[:-1], _unpack(d, small_arrs), _unpack(m, small_arrs), _unpack(v, small_arrs)):
        out_g[n], out_d[n], out_m[n], out_v[n] = gg, dd, mm_, vv

    return (loss, g_x[None], *[out_g[n] for n in WEIGHTS], *[out_d[n] for n in WEIGHTS],
            *[out_m[n] for n in WEIGHTS], *[out_v[n] for n in WEIGHTS])
```

```python
import functools
import math

import jax
import jax.numpy as jnp
from jax import lax
from jax.experimental import pallas as pl
from jax.experimental.pallas import tpu as pltpu

F32 = jnp.float32
BF16 = jnp.bfloat16
MESH = pl.DeviceIdType.MESH

D_MODEL = 1024
DEPTH = 2
RMS_EPS = 1e-6
SSM_GROUPS, SSM_GROUP_CH, SSM_STATE = 32, 16, 64
SSM_WIDTH = SSM_GROUPS * SSM_GROUP_CH
SSM_LANES = SSM_GROUPS * SSM_STATE
SSM_PACK = 8
MLA_HEADS, MLA_Q_RANK, MLA_KV_RANK, MLA_NOPE, MLA_ROPE, MLA_V = 8, 512, 256, 64, 32, 64
ROPE_THETA = 10000.0
HG_HEADS, HG_DK, HG_DV, HG_CHUNK, HG_SUB = 4, 128, 128, 64, 16
X_HEADS, X_HEAD_DIM = 4, 128
D_FF = 2816
D_IN = 6432
D_IN_PAD = 6656
MASK_VALUE = -1e30

ADAM_LR, ADAM_B1, ADAM_B2, ADAM_EPS, ADAM_WD, ADAM_STEP = 0.001, 0.9, 0.999, 1e-08, 0.01, 10

VMEM_LIMIT = 48 * 1024 * 1024


def _pcall(body, **kw):
    return pl.pallas_call(body, **kw)


def _params(sem):
    return pltpu.CompilerParams(dimension_semantics=sem, vmem_limit_bytes=VMEM_LIMIT)


def _tile(n, cap, align=128):
    if n <= cap:
        return n
    t = (cap // align) * align
    while t >= align:
        if n % t == 0:
            return t
        t -= align
    return n


def _mm(a, b, *, ta=False, tb=False, out_dtype=F32, name="mm"):
    if ta:
        K, M = a.shape
    else:
        M, K = a.shape
    if tb:
        N, Kb = b.shape
    else:
        Kb, N = b.shape
    assert K == Kb, (a.shape, b.shape, ta, tb)
    tm = _tile(M, 512)
    tn = _tile(N, 512)
    tk = _tile(K, 2048)
    nk = K // tk
    dims = (((0,) if ta else (1,), (1,) if tb else (0,)), ((), ()))

    def body(a_ref, b_ref, o_ref, *acc):
        av = a_ref[...].astype(BF16)
        bv = b_ref[...].astype(BF16)
        part = lax.dot_general(av, bv, dims, preferred_element_type=F32)
        if nk == 1:
            o_ref[...] = part.astype(o_ref.dtype)
        else:
            acc_ref = acc[0]
            k = pl.program_id(2)

            @pl.when(k == 0)
            def _():
                acc_ref[...] = part

            @pl.when(k > 0)
            def _():
                acc_ref[...] += part

            @pl.when(k == nk - 1)
            def _():
                o_ref[...] = acc_ref[...].astype(o_ref.dtype)

    a_spec = pl.BlockSpec((tk, tm), lambda i, j, k: (k, i)) if ta else pl.BlockSpec((tm, tk), lambda i, j, k: (i, k))
    b_spec = pl.BlockSpec((tn, tk), lambda i, j, k: (j, k)) if tb else pl.BlockSpec((tk, tn), lambda i, j, k: (k, j))
    return _pcall(
        body,
        name=name,
        grid=(M // tm, N // tn, nk),
        in_specs=[a_spec, b_spec],
        out_specs=pl.BlockSpec((tm, tn), lambda i, j, k: (i, j)),
        out_shape=jax.ShapeDtypeStruct((M, N), out_dtype),
        scratch_shapes=[pltpu.VMEM((tm, tn), F32)] if nk > 1 else [],
        compiler_params=_params(("parallel", "parallel", "arbitrary")),
    )(a, b)


def _bd_call(a, bc, transpose, name):
    m = a.shape[0]
    nb, ka, kb = bc.shape
    win, wout = (kb, ka) if transpose else (ka, kb)
    assert a.shape[1] == nb * win
    tm = _tile(m, 1024)
    dims = _NT if transpose else (((1,), (0,)), ((), ()))

    def body(a_ref, b_ref, o_ref):
        o_ref[...] = lax.dot_general(a_ref[...].astype(BF16), b_ref[0].astype(BF16), dims, preferred_element_type=F32)

    return _pcall(
        body,
        name=name,
        grid=(m // tm, nb),
        in_specs=[pl.BlockSpec((tm, win), lambda i, j: (i, j)), pl.BlockSpec((1, ka, kb), lambda i, j: (j, 0, 0))],
        out_specs=pl.BlockSpec((tm, wout), lambda i, j: (i, j)),
        out_shape=jax.ShapeDtypeStruct((m, nb * wout), F32),
        compiler_params=_params(("parallel", "parallel")),
    )(a, bc)


def _bd_dw_call(a, g, nb, name):
    m = a.shape[0]
    ka, kb = a.shape[1] // nb, g.shape[1] // nb
    tk = _tile(m, 1024)
    nk = m // tk

    def body(a_ref, g_ref, o_ref):
        part = lax.dot_general(a_ref[...].astype(BF16), g_ref[...].astype(BF16), _TN, preferred_element_type=F32)
        k = pl.program_id(1)

        @pl.when(k == 0)
        def _():
            o_ref[0] = part

        @pl.when(k > 0)
        def _():
            o_ref[0] += part

    return _pcall(
        body,
        name=name,
        grid=(nb, nk),
        in_specs=[pl.BlockSpec((tk, ka), lambda j, k: (k, j)), pl.BlockSpec((tk, kb), lambda j, k: (k, j))],
        out_specs=pl.BlockSpec((1, ka, kb), lambda j, k: (j, 0, 0)),
        out_shape=jax.ShapeDtypeStruct((nb, ka, kb), F32),
        compiler_params=_params(("parallel", "arbitrary")),
    )(a, g)


@functools.partial(jax.custom_vjp, nondiff_argnums=(2,))
def bd_matmul(a, bc, name):
    return _bd_call(a, bc, False, name + "_f")


def _bd_matmul_fwd(a, bc, name):
    return _bd_call(a, bc, False, name + "_f"), (a, bc)


def _bd_matmul_bwd(name, res, g):
    a, bc = res
    return _bd_call(g, bc, True, name + "_da"), _bd_dw_call(a, g, bc.shape[0], name + "_dw")


bd_matmul.defvjp(_bd_matmul_fwd, _bd_matmul_bwd)


@functools.partial(jax.custom_vjp, nondiff_argnums=(3,))
def matmul_w(a, w16, tok, name):
    return _mm(a, w16, name=name + "_f")


def _matmul_w_fwd(a, w16, tok, name):
    return _mm(a, w16, name=name + "_f"), (a, w16)


def _matmul_w_bwd(name, res, g):
    a, w16 = res
    return _mm(g, w16, tb=True, name=name + "_da"), jnp.zeros_like(w16), _mm(a, g, ta=True, name=name + "_dw")


matmul_w.defvjp(_matmul_w_fwd, _matmul_w_bwd)


@functools.partial(jax.custom_vjp, nondiff_argnums=(3,))
def matmul_w_t(a, w16, tok, name):
    return _mm(w16, a, ta=True, tb=True, name=name + "_f")


def _matmul_w_t_fwd(a, w16, tok, name):
    return _mm(w16, a, ta=True, tb=True, name=name + "_f"), (a, w16)


def _matmul_w_t_bwd(name, res, g):
    a, w16 = res
    return (_mm(g, w16, ta=True, tb=True, name=name + "_da"), jnp.zeros_like(w16),
            _mm(a, g, ta=True, tb=True, name=name + "_dw"))


matmul_w_t.defvjp(_matmul_w_t_fwd, _matmul_w_t_bwd)


@functools.partial(jax.custom_vjp, nondiff_argnums=(3,))
def matmul_w_at(at, w16, tok, name):
    return _mm(at, w16, ta=True, name=name + "_f")


def _matmul_w_at_fwd(at, w16, tok, name):
    return _mm(at, w16, ta=True, name=name + "_f"), (at, w16)


def _matmul_w_at_bwd(name, res, g):
    at, w16 = res
    return _mm(w16, g, tb=True, name=name + "_da"), jnp.zeros_like(w16), _mm(at, g, name=name + "_dw")


matmul_w_at.defvjp(_matmul_w_at_fwd, _matmul_w_at_bwd)


def _rms_fwd_call(x, g, name):
    rows, d = x.shape
    tr = _tile(rows, 512, 8)

    def body(x_ref, g_ref, y_ref):
        xv = x_ref[...]
        rstd = lax.rsqrt(jnp.mean(xv * xv, axis=-1, keepdims=True) + RMS_EPS)
        y_ref[...] = xv * rstd * g_ref[...]

    return _pcall(
        body,
        name=name,
        grid=(rows // tr,),
        in_specs=[pl.BlockSpec((tr, d), lambda i: (i, 0)), pl.BlockSpec((1, d), lambda i: (0, 0))],
        out_specs=pl.BlockSpec((tr, d), lambda i: (i, 0)),
        out_shape=jax.ShapeDtypeStruct((rows, d), F32),
        compiler_params=_params(("parallel",)),
    )(x, g.reshape(1, d))


def _rms_bwd_call(x, g, dy, name):
    rows, d = x.shape
    tr = _tile(rows, 512, 8)
    nb = rows // tr

    def body(x_ref, g_ref, dy_ref, dx_ref, dg_ref):
        xv = x_ref[...]
        rstd = lax.rsqrt(jnp.mean(xv * xv, axis=-1, keepdims=True) + RMS_EPS)
        xh = xv * rstd
        dyv = dy_ref[...]
        dyg = dyv * g_ref[...]
        dx_ref[...] = rstd * (dyg - xh * jnp.mean(dyg * xh, axis=-1, keepdims=True))
        dg_ref[...] = jnp.sum((dyv * xh).reshape(tr // 8, 8, d), axis=0)

    dx, dgp = _pcall(
        body,
        name=name,
        grid=(nb,),
        in_specs=[
            pl.BlockSpec((tr, d), lambda i: (i, 0)),
            pl.BlockSpec((1, d), lambda i: (0, 0)),
            pl.BlockSpec((tr, d), lambda i: (i, 0)),
        ],
        out_specs=[pl.BlockSpec((tr, d), lambda i: (i, 0)), pl.BlockSpec((8, d), lambda i: (i, 0))],
        out_shape=[jax.ShapeDtypeStruct((rows, d), F32), jax.ShapeDtypeStruct((nb * 8, d), F32)],
        compiler_params=_params(("parallel",)),
    )(x, g.reshape(1, d), dy)
    return dx, jnp.sum(dgp, axis=0)


@functools.partial(jax.custom_vjp, nondiff_argnums=(2,))
def rmsnorm(x, g, name):
    return _rms_fwd_call(x, g, name + "_f")


def _rmsnorm_fwd(x, g, name):
    return _rms_fwd_call(x, g, name + "_f"), (x, g)


def _rmsnorm_bwd(name, res, dy):
    x, g = res
    return _rms_bwd_call(x, g, dy, name + "_b")


rmsnorm.defvjp(_rmsnorm_fwd, _rmsnorm_bwd)


def _attn_blocks(sq, sk):
    return _tile(sq, 512), _tile(sk, 512)


def _attn_pt(qs, k_ref, qi, ki, bq, bk, masked, shift):
    st = jnp.dot(k_ref[0].astype(BF16), qs, preferred_element_type=F32)
    if masked:
        kpos = ki * bk + lax.broadcasted_iota(jnp.int32, (bk, bq), 0)
        qpos = qi * bq + lax.broadcasted_iota(jnp.int32, (bk, bq), 1)
        st = jnp.where(kpos <= qpos, st, MASK_VALUE)
    return st if shift is None else jnp.exp(st - shift)


def _attn_pairs(nq, nk, bq, bk, causal, k_major):
    ok = lambda i, j: (not causal) or j * bk <= i * bq + bq - 1
    if k_major:
        pairs = [(i, j) for j in range(nk) for i in range(nq) if ok(i, j)]
    else:
        pairs = [(i, j) for i in range(nq) for j in range(nk) if ok(i, j)]
    qtab = jnp.asarray([p[0] for p in pairs], jnp.int32)
    ktab = jnp.asarray([p[1] for p in pairs], jnp.int32)
    first_q = (lambda j: (j * bk) // bq) if causal else (lambda j: 0)
    last_k = (lambda i: jnp.minimum((i * bq + bq - 1) // bk, nk - 1)) if causal else (lambda i: nk - 1)
    return qtab, ktab, first_q, last_k


def _attn_cases(causal, qi, ki, bq, bk, step):
    if not causal:
        step(False)
        return
    pl.when(ki * bk + bk - 1 <= qi * bq)(functools.partial(step, False))
    pl.when(ki * bk + bk - 1 > qi * bq)(functools.partial(step, True))


def _attn_fwd_call(qt, k, vt, scale, causal, name):
    h, sk, dq = k.shape
    sq = qt.shape[1]
    dv = vt.shape[0] // h
    bq, bk = _attn_blocks(sq, sk)
    nq, nk = sq // bq, sk // bk

    qtab, ktab, _, last_k = _attn_pairs(nq, nk, bq, bk, causal, False)

    def body(qi_ref, ki_ref, q_ref, k_ref, v_ref, o_ref, lse_ref, m_sc, l_sc, acc_sc):
        qi, ki = qi_ref[pl.program_id(1)], ki_ref[pl.program_id(1)]

        @pl.when(ki == 0)
        def _():
            m_sc[...] = jnp.full(m_sc.shape, -jnp.inf, F32)
            l_sc[...] = jnp.zeros(l_sc.shape, F32)
            acc_sc[...] = jnp.zeros(acc_sc.shape, F32)

        def step(masked):
            qs = (q_ref[...] * scale).astype(BF16)
            st = _attn_pt(qs, k_ref, qi, ki, bq, bk, masked, None)
            m_prev = m_sc[...]
            m_new = jnp.maximum(m_prev, jnp.max(st, axis=0, keepdims=True))
            alpha = jnp.exp(m_prev - m_new)
            pt = jnp.exp(st - m_new)
            l_sc[...] = alpha * l_sc[...] + jnp.sum(pt, axis=0, keepdims=True)
            acc_sc[...] = alpha * acc_sc[...] + jnp.dot(v_ref[...].astype(BF16), pt.astype(BF16),
                                                        preferred_element_type=F32)
            m_sc[...] = m_new

        _attn_cases(causal, qi, ki, bq, bk, step)

        @pl.when(ki == last_k(qi))
        def _():
            o_ref[...] = acc_sc[...] / l_sc[...]
            lse_ref[0] = m_sc[...] + jnp.log(l_sc[...])

    qmap = lambda hh, p, qi_ref, ki_ref: (hh, qi_ref[p])
    return _pcall(
        body,
        name=name,
        grid_spec=pltpu.PrefetchScalarGridSpec(
            num_scalar_prefetch=2,
            grid=(h, qtab.shape[0]),
            in_specs=[
                pl.BlockSpec((dq, bq), qmap),
                pl.BlockSpec((1, bk, dq), lambda hh, p, qi_ref, ki_ref: (hh, ki_ref[p], 0)),
                pl.BlockSpec((dv, bk), lambda hh, p, qi_ref, ki_ref: (hh, ki_ref[p])),
            ],
            out_specs=[pl.BlockSpec((dv, bq), qmap),
                       pl.BlockSpec((1, 1, bq), lambda hh, p, qi_ref, ki_ref: (hh, 0, qi_ref[p]))],
            scratch_shapes=[pltpu.VMEM((1, bq), F32), pltpu.VMEM((1, bq), F32), pltpu.VMEM((dv, bq), F32)],
        ),
        out_shape=[jax.ShapeDtypeStruct((h * dv, sq), F32), jax.ShapeDtypeStruct((h, 1, sq), F32)],
        compiler_params=_params(("parallel", "arbitrary")),
    )(qtab, ktab, qt, k, vt)


_TN = (((0,), (0,)), ((), ()))
_NT = (((1,), (1,)), ((), ()))


def _attn_dst(qs, k_ref, v_ref, do_ref, lse_ref, dl_ref, qi, ki, bq, bk, masked):
    pt = _attn_pt(qs, k_ref, qi, ki, bq, bk, masked, lse_ref[0])
    dpt = lax.dot_general(v_ref[...].astype(BF16), do_ref[...].astype(BF16), _TN, preferred_element_type=F32)
    return pt, pt * (dpt - dl_ref[0])


def _attn_dq_call(qt, k, vt, dot, lse, delta, scale, causal, name):
    h, sk, dq = k.shape
    sq = qt.shape[1]
    dv = vt.shape[0] // h
    bq, bk = _attn_blocks(sq, sk)
    nq, nk = sq // bq, sk // bk

    qtab, ktab, _, last_k = _attn_pairs(nq, nk, bq, bk, causal, False)

    def body(qi_ref, ki_ref, q_ref, k_ref, v_ref, do_ref, lse_ref, dl_ref, dq_ref, acc_sc):
        qi, ki = qi_ref[pl.program_id(1)], ki_ref[pl.program_id(1)]

        @pl.when(ki == 0)
        def _():
            acc_sc[...] = jnp.zeros(acc_sc.shape, F32)

        def step(masked):
            qs = (q_ref[...] * scale).astype(BF16)
            _, dst = _attn_dst(qs, k_ref, v_ref, do_ref, lse_ref, dl_ref, qi, ki, bq, bk, masked)
            acc_sc[...] += lax.dot_general(k_ref[0].astype(BF16), dst.astype(BF16), _TN, preferred_element_type=F32)

        _attn_cases(causal, qi, ki, bq, bk, step)

        @pl.when(ki == last_k(qi))
        def _():
            dq_ref[...] = acc_sc[...] * scale

    qmap = lambda hh, p, qi_ref, ki_ref: (hh, qi_ref[p])
    rowmap = lambda hh, p, qi_ref, ki_ref: (hh, 0, qi_ref[p])
    return _pcall(
        body,
        name=name,
        grid_spec=pltpu.PrefetchScalarGridSpec(
            num_scalar_prefetch=2,
            grid=(h, qtab.shape[0]),
            in_specs=[
                pl.BlockSpec((dq, bq), qmap),
                pl.BlockSpec((1, bk, dq), lambda hh, p, qi_ref, ki_ref: (hh, ki_ref[p], 0)),
                pl.BlockSpec((dv, bk), lambda hh, p, qi_ref, ki_ref: (hh, ki_ref[p])),
                pl.BlockSpec((dv, bq), qmap),
                pl.BlockSpec((1, 1, bq), rowmap),
                pl.BlockSpec((1, 1, bq), rowmap),
            ],
            out_specs=pl.BlockSpec((dq, bq), qmap),
            scratch_shapes=[pltpu.VMEM((dq, bq), F32)],
        ),
        out_shape=jax.ShapeDtypeStruct((h * dq, sq), F32),
        compiler_params=_params(("parallel", "arbitrary")),
    )(qtab, ktab, qt, k, vt, dot, lse, delta)


def _attn_dkv_call(qt, k, vt, dot, lse, delta, scale, causal, name):
    h, sk, dq = k.shape
    sq = qt.shape[1]
    dv = vt.shape[0] // h
    bq, bk = _attn_blocks(sq, sk)
    nq, nk = sq // bq, sk // bk

    qtab, ktab, first_q, _ = _attn_pairs(nq, nk, bq, bk, causal, True)

    def body(qi_ref, ki_ref, q_ref, k_ref, v_ref, do_ref, lse_ref, dl_ref, dk_ref, dv_ref, dk_sc, dv_sc):
        qi, ki = qi_ref[pl.program_id(1)], ki_ref[pl.program_id(1)]

        @pl.when(qi == first_q(ki))
        def _():
            dk_sc[...] = jnp.zeros(dk_sc.shape, F32)
            dv_sc[...] = jnp.zeros(dv_sc.shape, F32)

        def step(masked):
            qs = (q_ref[...] * scale).astype(BF16)
            pt, dst = _attn_dst(qs, k_ref, v_ref, do_ref, lse_ref, dl_ref, qi, ki, bq, bk, masked)
            dv_sc[...] += lax.dot_general(do_ref[...].astype(BF16), pt.astype(BF16), _NT, preferred_element_type=F32)
            dk_sc[...] += lax.dot_general(dst.astype(BF16), qs, _NT, preferred_element_type=F32)

        _attn_cases(causal, qi, ki, bq, bk, step)

        @pl.when(qi == nq - 1)
        def _():
            dk_ref[0] = dk_sc[...]
            dv_ref[...] = dv_sc[...]

    qmap = lambda hh, p, qi_ref, ki_ref: (hh, qi_ref[p])
    rowmap = lambda hh, p, qi_ref, ki_ref: (hh, 0, qi_ref[p])
    kmap = lambda hh, p, qi_ref, ki_ref: (hh, ki_ref[p], 0)
    vmap = lambda hh, p, qi_ref, ki_ref: (hh, ki_ref[p])
    return _pcall(
        body,
        name=name,
        grid_spec=pltpu.PrefetchScalarGridSpec(
            num_scalar_prefetch=2,
            grid=(h, qtab.shape[0]),
            in_specs=[
                pl.BlockSpec((dq, bq), qmap),
                pl.BlockSpec((1, bk, dq), kmap),
                pl.BlockSpec((dv, bk), vmap),
                pl.BlockSpec((dv, bq), qmap),
                pl.BlockSpec((1, 1, bq), rowmap),
                pl.BlockSpec((1, 1, bq), rowmap),
            ],
            out_specs=[pl.BlockSpec((1, bk, dq), kmap), pl.BlockSpec((dv, bk), vmap)],
            scratch_shapes=[pltpu.VMEM((bk, dq), F32), pltpu.VMEM((dv, bk), F32)],
        ),
        out_shape=[jax.ShapeDtypeStruct((h, sk, dq), F32), jax.ShapeDtypeStruct((h * dv, sk), F32)],
        compiler_params=_params(("parallel", "arbitrary")),
    )(qtab, ktab, qt, k, vt, dot, lse, delta)


@functools.partial(jax.custom_vjp, nondiff_argnums=(3, 4, 5))
def attention(qt, k, vt, scale, causal, name):
    return _attn_fwd_call(qt, k, vt, scale, causal, name + "_f")[0]


def _attention_fwd(qt, k, vt, scale, causal, name):
    ot, lse = _attn_fwd_call(qt, k, vt, scale, causal, name + "_f")
    return ot, (qt, k, vt, ot, lse)


def _attention_bwd(scale, causal, name, res, dot):
    qt, k, vt, ot, lse = res
    h = k.shape[0]
    delta = jnp.sum((dot * ot).reshape(h, ot.shape[0] // h, ot.shape[1]), axis=1, keepdims=True)
    dqt = _attn_dq_call(qt, k, vt, dot, lse, delta, scale, causal, name + "_dq")
    dk, dvt = _attn_dkv_call(qt, k, vt, dot, lse, delta, scale, causal, name + "_dkv")
    return dqt, dk, dvt


attention.defvjp(_attention_fwd, _attention_bwd)


SCAN_ROWS = 8
SCAN_LANES = 256
SCAN_TBLOCK = 512


def _scan_tables(lr, li):
    pr, pi = [lr], [li]
    for _ in range(SCAN_ROWS - 1):
        pr, pi = pr + [pr[-1] * lr - pi[-1] * li], pi + [pr[-1] * li + pi[-1] * lr]
    return jnp.concatenate(pr, axis=0), jnp.concatenate(pi, axis=0)


def _scan_call(ur, ui, lr, li, reverse, name, states=None):
    s, n = ur.shape
    tb = _tile(s, SCAN_TBLOCK, 8)
    nt, nl = s // tb, n // SCAN_LANES
    ntile = tb // SCAN_ROWS
    pr, pi = _scan_tables(lr, li)
    if reverse:
        pr, pi = pr[::-1], pi[::-1]
    shape = (SCAN_ROWS, SCAN_LANES)
    with_states = states is not None
    assert reverse or not with_states

    def body(*refs):
        if with_states:
            ur_ref, ui_ref, pr_ref, pi_ref, sr_ref, si_ref, xr_ref, xi_ref, dr_ref, di_ref, cr_sc, ci_sc, ar_sc, ai_sc = refs
        else:
            ur_ref, ui_ref, pr_ref, pi_ref, xr_ref, xi_ref, cr_sc, ci_sc, ar_sc, ai_sc = refs

        @pl.when(pl.program_id(1) == 0)
        def _():
            for sc in (cr_sc, ci_sc, ar_sc, ai_sc):
                sc[...] = jnp.zeros(shape, F32)

        prv, piv = pr_ref[...], pi_ref[...]
        rows = lax.broadcasted_iota(jnp.int32, shape, 0)

        def powers(k):
            r = (SCAN_ROWS - k) if reverse else (k - 1)
            return jnp.broadcast_to(prv[r:r + 1], shape), jnp.broadcast_to(piv[r:r + 1], shape)

        pw = [powers(k) for k in (1, 2, 4)]

        def tile(i, carry):
            cr, ci, acr, aci = carry
            j = (ntile - 1 - i) if reverse else i
            sl = pl.ds(pl.multiple_of(j * SCAN_ROWS, SCAN_ROWS), SCAN_ROWS)
            xr, xi = ur_ref[sl, :], ui_ref[sl, :]
            for (ar, ai), k in zip(pw, (1, 2, 4)):
                if reverse:
                    keep = rows < SCAN_ROWS - k
                    sr = jnp.where(keep, pltpu.roll(xr, SCAN_ROWS - k, 0), 0.0)
                    si = jnp.where(keep, pltpu.roll(xi, SCAN_ROWS - k, 0), 0.0)
                else:
                    keep = rows >= k
                    sr = jnp.where(keep, pltpu.roll(xr, k, 0), 0.0)
                    si = jnp.where(keep, pltpu.roll(xi, k, 0), 0.0)
                xr, xi = xr + (ar * sr - ai * si), xi + (ar * si + ai * sr)
            xr, xi = xr + (prv * cr - piv * ci), xi + (prv * ci + piv * cr)
            xr_ref[sl, :] = xr
            xi_ref[sl, :] = xi
            if with_states:
                last = rows == SCAN_ROWS - 1
                nr = jnp.where(last, cr, pltpu.roll(xr, SCAN_ROWS - 1, 0))
                ni = jnp.where(last, ci, pltpu.roll(xi, SCAN_ROWS - 1, 0))
                sr, si = sr_ref[sl, :], si_ref[sl, :]
                acr, aci = acr + (sr * nr + si * ni), aci + (sr * ni - si * nr)
            e = 0 if reverse else SCAN_ROWS - 1
            return jnp.broadcast_to(xr[e:e + 1], shape), jnp.broadcast_to(xi[e:e + 1], shape), acr, aci

        cr, ci, acr, aci = lax.fori_loop(0, ntile, tile, (cr_sc[...], ci_sc[...], ar_sc[...], ai_sc[...]))
        cr_sc[...] = cr
        ci_sc[...] = ci
        if with_states:
            ar_sc[...] = acr
            ai_sc[...] = aci

            @pl.when(pl.program_id(1) == nt - 1)
            def _():
                dr_ref[...] = acr
                di_ref[...] = aci

    tmap = (lambda l, t: (nt - 1 - t, l)) if reverse else (lambda l, t: (t, l))
    blk = pl.BlockSpec((tb, SCAN_LANES), tmap)
    tab = pl.BlockSpec((SCAN_ROWS, SCAN_LANES), lambda l, t: (0, l))
    full = jax.ShapeDtypeStruct((s, n), F32)
    return _pcall(
        body,
        name=name,
        grid=(nl, nt),
        in_specs=[blk, blk, tab, tab] + ([blk, blk] if with_states else []),
        out_specs=[blk, blk] + ([tab, tab] if with_states else []),
        out_shape=[full, full] + ([jax.ShapeDtypeStruct((SCAN_ROWS, n), F32)] * 2 if with_states else []),
        scratch_shapes=[pltpu.VMEM(shape, F32)] * 4,
        compiler_params=_params(("parallel", "arbitrary")),
    )(ur, ui, pr, pi, *(states if with_states else ()))


@functools.partial(jax.custom_vjp, nondiff_argnums=(4,))
def s5_scan(ur, ui, lr, li, name):
    return tuple(_scan_call(ur, ui, lr, li, False, name + "_f"))


def _s5_scan_fwd(ur, ui, lr, li, name):
    xr, xi = _scan_call(ur, ui, lr, li, False, name + "_f")
    return (xr, xi), (xr, xi, lr, li)


def _s5_scan_bwd(name, res, g):
    xr, xi, lr, li = res
    gr, gi, dlr, dli = _scan_call(g[0], g[1], lr, -li, True, name + "_b", states=(xr, xi))
    return gr, gi, jnp.sum(dlr, axis=0, keepdims=True), jnp.sum(dli, axis=0, keepdims=True)


s5_scan.defvjp(_s5_scan_fwd, _s5_scan_bwd)


def _hg_chunk(q, k, v, lf, st):
    c = HG_CHUNK
    row = lax.broadcasted_iota(jnp.int32, (c, c), 0)
    col = lax.broadcasted_iota(jnp.int32, (c, c), 1)
    tri = (col <= row).astype(F32)
    b = jnp.dot(tri, lf, precision=lax.Precision.HIGHEST, preferred_element_type=F32)
    qb = q * jnp.exp(b)
    o = lax.dot_general(qb.astype(BF16), st.astype(BF16), (((1,), (1,)), ((), ())), preferred_element_type=F32)
    nsub = c // HG_SUB
    srow = lax.broadcasted_iota(jnp.int32, (HG_SUB, HG_SUB, 1), 0)
    scol = lax.broadcasted_iota(jnp.int32, (HG_SUB, HG_SUB, 1), 1)
    smask = scol <= srow
    outs = []
    for i in range(nsub):
        lo = i * HG_SUB
        bi, qi, ki = b[lo:lo + HG_SUB], q[lo:lo + HG_SUB], k[lo:lo + HG_SUB]
        diff = jnp.where(smask, bi[:, None, :] - bi[None, :, :], 0.0)
        e = jnp.where(smask, jnp.exp(diff), 0.0)
        a_ii = jnp.sum(qi[:, None, :] * ki[None, :, :] * e, axis=-1)
        oi = jnp.dot(a_ii.astype(BF16), v[lo:lo + HG_SUB].astype(BF16), preferred_element_type=F32)
        if i > 0:
            r = b[lo - 1:lo]
            qt = qi * jnp.exp(bi - r)
            kt = k[:lo] * jnp.exp(r - b[:lo])
            a_ij = lax.dot_general(qt.astype(BF16), kt.astype(BF16), (((1,), (1,)), ((), ())),
                                   preferred_element_type=F32)
            oi = oi + jnp.dot(a_ij.astype(BF16), v[:lo].astype(BF16), preferred_element_type=F32)
        outs.append(oi)
    o = o + jnp.concatenate(outs, axis=0)
    bl = b[c - 1:c]
    kd = k * jnp.exp(bl - b)
    st_new = st * jnp.exp(bl) + lax.dot_general(v.astype(BF16), kd.astype(BF16), (((0,), (0,)), ((), ())),
                                                preferred_element_type=F32)
    return o, st_new


def _hg_fwd_call(q, k, v, lf, name):
    s, w = q.shape
    nc = s // HG_CHUNK
    blk = pl.BlockSpec((HG_CHUNK, w), lambda i: (i, 0))

    def body(q_ref, k_ref, v_ref, lf_ref, o_ref, st_ref, st_sc):
        @pl.when(pl.program_id(0) == 0)
        def _():
            st_sc[...] = jnp.zeros(st_sc.shape, F32)

        for hh in range(HG_HEADS):
            cs = slice(hh * HG_DK, (hh + 1) * HG_DK)
            st = st_sc[hh]
            st_ref[0, hh] = st
            o, st_new = _hg_chunk(q_ref[:, cs], k_ref[:, cs], v_ref[:, cs], lf_ref[:, cs], st)
            o_ref[:, cs] = o
            st_sc[hh] = st_new

    return _pcall(
        body,
        name=name,
        grid=(nc,),
        in_specs=[blk, blk, blk, blk],
        out_specs=[blk, pl.BlockSpec((1, HG_HEADS, HG_DV, HG_DK), lambda i: (i, 0, 0, 0))],
        out_shape=[jax.ShapeDtypeStruct((s, w), F32), jax.ShapeDtypeStruct((nc, HG_HEADS, HG_DV, HG_DK), F32)],
        scratch_shapes=[pltpu.VMEM((HG_HEADS, HG_DV, HG_DK), F32)],
        compiler_params=_params(("arbitrary",)),
    )(q, k, v, lf)


def _hg_bwd_call(q, k, v, lf, sts, do, name):
    s, w = q.shape
    nc = s // HG_CHUNK
    blk = pl.BlockSpec((HG_CHUNK, w), lambda i: (nc - 1 - i, 0))

    def body(q_ref, k_ref, v_ref, lf_ref, st_ref, do_ref, dq_ref, dk_ref, dv_ref, dlf_ref, dst_sc):
        @pl.when(pl.program_id(0) == 0)
        def _():
            dst_sc[...] = jnp.zeros(dst_sc.shape, F32)

        for hh in range(HG_HEADS):
            cs = slice(hh * HG_DK, (hh + 1) * HG_DK)
            _, vjp = jax.vjp(_hg_chunk, q_ref[:, cs], k_ref[:, cs], v_ref[:, cs], lf_ref[:, cs], st_ref[0, hh])
            dq, dk, dv, dlf, dst = vjp((do_ref[:, cs], dst_sc[hh]))
            dq_ref[:, cs] = dq
            dk_ref[:, cs] = dk
            dv_ref[:, cs] = dv
            dlf_ref[:, cs] = dlf
            dst_sc[hh] = dst

    out = jax.ShapeDtypeStruct((s, w), F32)
    return _pcall(
        body,
        name=name,
        grid=(nc,),
        in_specs=[blk, blk, blk, blk, pl.BlockSpec((1, HG_HEADS, HG_DV, HG_DK), lambda i: (nc - 1 - i, 0, 0, 0)), blk],
        out_specs=[blk, blk, blk, blk],
        out_shape=[out, out, out, out],
        scratch_shapes=[pltpu.VMEM((HG_HEADS, HG_DV, HG_DK), F32)],
        compiler_params=_params(("arbitrary",)),
    )(q, k, v, lf, sts, do)


@functools.partial(jax.custom_vjp, nondiff_argnums=(4,))
def hgrn_core(q, k, v, lf, name):
    return _hg_fwd_call(q, k, v, lf, name + "_f")[0]


def _hgrn_core_fwd(q, k, v, lf, name):
    o, sts = _hg_fwd_call(q, k, v, lf, name + "_f")
    return o, (q, k, v, lf, sts)


def _hgrn_core_bwd(name, res, do):
    q, k, v, lf, sts = res
    return tuple(_hg_bwd_call(q, k, v, lf, sts, do, name + "_b"))


hgrn_core.defvjp(_hgrn_core_fwd, _hgrn_core_bwd)


def _adamw_call(w, g, m, v, name):
    rows, cols = w.shape
    tr = _tile(rows, 256, 8)
    c1 = 1.0 / (1.0 - ADAM_B1 ** ADAM_STEP)
    c2 = 1.0 / (1.0 - ADAM_B2 ** ADAM_STEP)

    def body(w_ref, g_ref, m_ref, v_ref, d_ref, mo_ref, vo_ref):
        gv = g_ref[...]
        mn = ADAM_B1 * m_ref[...] + (1.0 - ADAM_B1) * gv
        vn = ADAM_B2 * v_ref[...] + (1.0 - ADAM_B2) * (gv * gv)
        d_ref[...] = -ADAM_LR * ((mn * c1) / (jnp.sqrt(vn * c2) + ADAM_EPS) + ADAM_WD * w_ref[...])
        mo_ref[...] = mn
        vo_ref[...] = vn

    blk = pl.BlockSpec((tr, cols), lambda i: (i, 0))
    out = jax.ShapeDtypeStruct((rows, cols), F32)
    return _pcall(
        body,
        name=name,
        grid=(rows // tr,),
        in_specs=[blk, blk, blk, blk],
        out_specs=[blk, blk, blk],
        out_shape=[out, out, out],
        compiler_params=_params(("parallel",)),
    )(w, g, m, v)


ANY = pl.BlockSpec(memory_space=pl.ANY)


def _place():
    return lax.axis_index("x"), lax.axis_index("y"), lax.axis_index("c")


def _other_chips(x, y):
    return [(1 - x, y), (x, 1 - y), (1 - x, 1 - y)]


def _gather_weights(shards):
    nw = len(shards)

    def body(*refs):
        w_refs, g_refs, (send_sems, recv_sems) = refs[:nw], refs[nw:2 * nw], refs[2 * nw:]
        x, y, c = _place()
        myq = 2 * x + y
        sibling = (x, y, 1 - c)
        chips = _other_chips(x, y)

        def copy(k, src, dst, to):
            return pltpu.make_async_remote_copy(src_ref=src, dst_ref=dst, send_sem=send_sems.at[k],
                                                recv_sem=recv_sems.at[k], device_id=to, device_id_type=MESH)

        def halves(i):
            hh = shards[i].shape[0] // 2
            return pl.ds(c * hh, hh), pl.ds((1 - c) * hh, hh)

        first = []
        for i in range(nw):
            mine, _ = halves(i)
            for j, (px, py) in enumerate(chips):
                first.append(copy(6 * i + j, w_refs[i].at[mine], g_refs[i].at[myq, mine], (px, py, c)))
                first[-1].start()
        passed = []
        for i in range(nw):
            mine, _ = halves(i)
            for j, (px, py) in enumerate(chips):
                q = 2 * px + py
                copy(6 * i + j, w_refs[i].at[mine], g_refs[i].at[q, mine], (px, py, c)).wait_recv()
                passed.append(copy(6 * i + 3 + j, g_refs[i].at[q, mine], g_refs[i].at[q, mine], sibling))
                passed[-1].start()
        for i in range(nw):
            _, theirs = halves(i)
            for j, (px, py) in enumerate(chips):
                q = 2 * px + py
                copy(6 * i + 3 + j, g_refs[i].at[q, theirs], g_refs[i].at[q, theirs], sibling).wait_recv()
        for cp in first + passed:
            cp.wait_send()

    got = _pcall(
        body,
        name="gather_weights",
        in_specs=[ANY] * nw,
        out_specs=[ANY] * nw,
        out_shape=[jax.ShapeDtypeStruct((4,) + s.shape, s.dtype) for s in shards],
        scratch_shapes=[pltpu.SemaphoreType.DMA((6 * nw,)), pltpu.SemaphoreType.DMA((6 * nw,))],
    )(*shards)
    my_chip = 2 * lax.axis_index("x") + lax.axis_index("y")
    return [lax.dynamic_update_slice(g, s[None], (my_chip, 0, 0)) for g, s in zip(got, shards)]


def _swap_halves_to_sibling(grs):
    nw = len(grs)

    def body(*refs):
        g_refs, t_refs, (send_sems, recv_sems) = refs[:nw], refs[nw:2 * nw], refs[2 * nw:]
        x, y, c = _place()
        cps = []
        for i in range(nw):
            hh = grs[i].shape[1] // 2
            cps.append(pltpu.make_async_remote_copy(
                src_ref=g_refs[i].at[:, pl.ds((1 - c) * hh, hh)], dst_ref=t_refs[i], send_sem=send_sems.at[i],
                recv_sem=recv_sems.at[i], device_id=(x, y, 1 - c), device_id_type=MESH))
            cps[-1].start()
        for cp in cps:
            cp.wait()

    return _pcall(
        body,
        name="grad_d2d",
        in_specs=[ANY] * nw,
        out_specs=[ANY] * nw,
        out_shape=[jax.ShapeDtypeStruct((g.shape[0], g.shape[1] // 2, g.shape[2]), g.dtype) for g in grs],
        scratch_shapes=[pltpu.SemaphoreType.DMA((nw,)), pltpu.SemaphoreType.DMA((nw,))],
    )(*grs)


def _add_half(gr, t1, place, name):
    nq, r, wd = gr.shape
    hh = r // 2
    tr = _tile(hh, 512, 16)
    nb = hh // tr

    def body(p_ref, g_ref, t_ref, o_ref):
        o_ref[...] = (g_ref[...] + t_ref[...]).astype(BF16)

    return _pcall(
        body,
        name=name,
        grid_spec=pltpu.PrefetchScalarGridSpec(
            num_scalar_prefetch=1,
            grid=(nq, nb),
            in_specs=[
                pl.BlockSpec((1, tr, wd), lambda q, i, p_ref: (q, p_ref[0] * nb + i, 0)),
                pl.BlockSpec((1, tr, wd), lambda q, i, p_ref: (q, i, 0)),
            ],
            out_specs=pl.BlockSpec((1, tr, wd), lambda q, i, p_ref: (q, i, 0)),
        ),
        out_shape=jax.ShapeDtypeStruct((nq, hh, wd), BF16),
        compiler_params=_params(("parallel", "parallel")),
    )(place, gr, t1)


def _scatter_partials(ps):
    nw = len(ps)

    def body(*refs):
        p_refs, t_refs, (send_sems, recv_sems) = refs[:nw], refs[nw:2 * nw], refs[2 * nw:]
        x, y, c = _place()
        cps = []
        for i in range(nw):
            for j, (px, py) in enumerate(_other_chips(x, y)):
                cps.append(pltpu.make_async_remote_copy(
                    src_ref=p_refs[i].at[2 * px + py], dst_ref=t_refs[i].at[j], send_sem=send_sems.at[3 * i + j],
                    recv_sem=recv_sems.at[3 * i + j], device_id=(px, py, c), device_id_type=MESH))
                cps[-1].start()
        for cp in cps:
            cp.wait()

    return _pcall(
        body,
        name="grad_ici",
        in_specs=[ANY] * nw,
        out_specs=[ANY] * nw,
        out_shape=[jax.ShapeDtypeStruct((3,) + p.shape[1:], p.dtype) for p in ps],
        scratch_shapes=[pltpu.SemaphoreType.DMA((3 * nw,)), pltpu.SemaphoreType.DMA((3 * nw,))],
    )(*ps)


def _add_partials(gr, t1, t2, place, name):
    nq, r, wd = gr.shape
    hh = r // 2
    tr = _tile(hh, 512, 16)
    nb = hh // tr

    def body(p_ref, g_ref, s_ref, t_ref, o_ref):
        o_ref[...] = (((g_ref[0] + s_ref[0]) + t_ref[0].astype(F32)) + t_ref[1].astype(F32)) + t_ref[2].astype(F32)

    return _pcall(
        body,
        name=name,
        grid_spec=pltpu.PrefetchScalarGridSpec(
            num_scalar_prefetch=1,
            grid=(nb,),
            in_specs=[
                pl.BlockSpec((1, tr, wd), lambda i, p_ref: (p_ref[1], p_ref[0] * nb + i, 0)),
                pl.BlockSpec((1, tr, wd), lambda i, p_ref: (p_ref[1], i, 0)),
                pl.BlockSpec((3, tr, wd), lambda i, p_ref: (0, i, 0)),
            ],
            out_specs=pl.BlockSpec((tr, wd), lambda i, p_ref: (i, 0)),
        ),
        out_shape=jax.ShapeDtypeStruct((hh, wd), F32),
        compiler_params=_params(("parallel",)),
    )(place, gr, t1, t2)


def _join_halves(halves):
    nw = len(halves)

    def body(*refs):
        h_refs, f_refs, (send_sems, recv_sems) = refs[:nw], refs[nw:2 * nw], refs[2 * nw:]
        x, y, c = _place()

        def copy(i, rows):
            hh = halves[i].shape[0]
            return pltpu.make_async_remote_copy(src_ref=h_refs[i], dst_ref=f_refs[i].at[pl.ds(rows * hh, hh)],
                                                send_sem=send_sems.at[i], recv_sem=recv_sems.at[i],
                                                device_id=(x, y, 1 - c), device_id_type=MESH)

        for i in range(nw):
            copy(i, c).start()
        for i in range(nw):
            copy(i, c).wait_send()
            copy(i, 1 - c).wait_recv()

    got = _pcall(
        body,
        name="grad_join",
        in_specs=[ANY] * nw,
        out_specs=[ANY] * nw,
        out_shape=[jax.ShapeDtypeStruct((2 * h.shape[0], h.shape[1]), h.dtype) for h in halves],
        scratch_shapes=[pltpu.SemaphoreType.DMA((nw,)), pltpu.SemaphoreType.DMA((nw,))],
    )(*halves)
    c = lax.axis_index("c")
    return [lax.dynamic_update_slice(g, h, (c * h.shape[0], 0)) for g, h in zip(got, halves)]


N_DEV = 8


def _allreduce_small(buf):
    rows, wd = buf.shape
    n = rows // N_DEV

    def body(x_ref, o_ref, stage_ref, send_sems, recv_sems):
        x, y, c = _place()
        me = 4 * x + 2 * y + c

        def peer(k):
            return (x ^ (k >> 2), y ^ ((k >> 1) & 1), c ^ (k & 1))

        def chunk(ref, d):
            return ref.at[pl.ds(pl.multiple_of(d * n, 8), n)]

        first = []
        for k in range(1, N_DEV):
            px, py, pc = peer(k)
            cp = pltpu.make_async_remote_copy(src_ref=chunk(x_ref, 4 * px + 2 * py + pc), dst_ref=stage_ref.at[k],
                                              send_sem=send_sems.at[k - 1], recv_sem=recv_sems.at[k - 1],
                                              device_id=(px, py, pc), device_id_type=MESH)
            cp.start()
            first.append(cp)
        acc = chunk(x_ref, me)[...]
        for k in range(1, N_DEV):
            first[k - 1].wait_recv()
            acc = acc + stage_ref[k]
        chunk(o_ref, me)[...] = acc
        second = []
        for k in range(1, N_DEV):
            cp = pltpu.make_async_remote_copy(src_ref=chunk(o_ref, me), dst_ref=chunk(o_ref, me),
                                              send_sem=send_sems.at[6 + k], recv_sem=recv_sems.at[6 + k],
                                              device_id=peer(k), device_id_type=MESH)
            cp.start()
            second.append(cp)
        for k in range(1, N_DEV):
            px, py, pc = peer(k)
            them = 4 * px + 2 * py + pc
            pltpu.make_async_remote_copy(src_ref=chunk(o_ref, them), dst_ref=chunk(o_ref, them),
                                         send_sem=send_sems.at[6 + k], recv_sem=recv_sems.at[6 + k],
                                         device_id=(px, py, pc), device_id_type=MESH).wait_recv()
        for cp in first + second:
            cp.wait_send()

    vmem = pl.BlockSpec(memory_space=pltpu.VMEM)
    return _pcall(
        body,
        name="allreduce_small",
        in_specs=[vmem],
        out_specs=vmem,
        out_shape=jax.ShapeDtypeStruct((rows, wd), F32),
        scratch_shapes=[pltpu.VMEM((N_DEV, n, wd), F32), pltpu.SemaphoreType.DMA((14,)),
                        pltpu.SemaphoreType.DMA((14,))],
    )(buf)


@functools.partial(jax.custom_vjp, nondiff_argnums=(1,))
def split_cols(z, spec):
    outs, off = [], 0
    for n in spec[0]:
        outs.append(z[:, off:off + n])
        off += n
    return tuple(outs)


def _split_cols_fwd(z, spec):
    return split_cols(z, spec), None


def _split_cols_bwd(spec, _, gs):
    parts = list(gs)
    if spec[1]:
        parts.append(jnp.zeros((parts[0].shape[0], spec[1]), parts[0].dtype))
    return (jnp.concatenate(parts, axis=1),)


split_cols.defvjp(_split_cols_fwd, _split_cols_bwd)


def _rope(t, cos, sin):
    half = t.shape[-1] // 2
    t1, t2 = t[..., :half], t[..., half:]
    return jnp.concatenate([t1 * cos - t2 * sin, t2 * cos + t1 * sin], axis=-1)


def _s5_mixer(u, sp, l, w16, toks):
    g, p_, h = SSM_GROUPS, SSM_STATE, SSM_GROUP_CH
    lam_re, lam_im = sp["ssm_lam_re", l], sp["ssm_lam_im", l]
    step = jnp.exp(sp["ssm_log_step", l])[:, None]
    mag, ang = jnp.exp(lam_re * step), lam_im * step
    lbr, lbi = mag * jnp.cos(ang), mag * jnp.sin(ang)
    den = lam_re * lam_re + lam_im * lam_im
    nr, ni = lbr - 1.0, lbi
    qr, qi = (nr * lam_re + ni * lam_im) / den, (ni * lam_re - nr * lam_im) / den
    b_re, b_im = sp["ssm_b_re", l], sp["ssm_b_im", l]
    bbr = qr[..., None] * b_re - qi[..., None] * b_im
    bbi = qr[..., None] * b_im + qi[..., None] * b_re
    nb = g // SSM_PACK
    eye = jnp.eye(SSM_PACK, dtype=F32)

    def in_blocks(t):
        return jnp.einsum("jgph,gk->jghkp", t.reshape(nb, SSM_PACK, p_, h), eye).reshape(nb, SSM_PACK * h, SSM_PACK * p_)

    def out_blocks(t):
        return jnp.einsum("jghp,gk->jgpkh", t.reshape(nb, SSM_PACK, h, p_), eye).reshape(nb, SSM_PACK * p_, SSM_PACK * h)

    nm = "ssm%d" % l
    ur = bd_matmul(u, in_blocks(bbr), nm + "_bur")
    ui = bd_matmul(u, in_blocks(bbi), nm + "_bui")
    xr, xi = s5_scan(ur, ui, lbr.reshape(1, g * p_), lbi.reshape(1, g * p_), nm + "_scan")
    y = (bd_matmul(xr, out_blocks(sp["ssm_c_re", l]), nm + "_cr") + bd_matmul(xi, out_blocks(-sp["ssm_c_im", l]), nm + "_ci")
         + sp["ssm_d", l].reshape(1, g * h) * u)
    y = jax.nn.gelu(y)
    zz = matmul_w(y, w16["ssm_w_glu", l], toks["ssm_w_glu", l], nm + "_glu")
    z_out, z_gate = split_cols(zz, ((D_MODEL, D_MODEL), 0))
    return z_out * jax.nn.sigmoid(z_gate)


def _mla_mixer(q_lat, kv_lat, k_rope, cos, sin, sp, l, w16, toks):
    s = q_lat.shape[0]
    nm = "mla%d" % l
    dqk = MLA_NOPE + MLA_ROPE
    qt = matmul_w_t(rmsnorm(q_lat, sp["mla_q_norm", l], nm + "_qn"), w16["mla_w_uq", l], toks["mla_w_uq", l],
                    nm + "_uq").reshape(MLA_HEADS, dqk, s)
    half = MLA_ROPE // 2
    q1, q2 = qt[:, MLA_NOPE:MLA_NOPE + half], qt[:, MLA_NOPE + half:]
    ct, st = cos.T[None], sin.T[None]
    qt = jnp.concatenate([qt[:, :MLA_NOPE], q1 * ct - q2 * st, q2 * ct + q1 * st], axis=1).reshape(MLA_HEADS * dqk, s)
    kvt = matmul_w_t(rmsnorm(kv_lat, sp["mla_kv_norm", l], nm + "_kvn"), w16["mla_w_ukv", l], toks["mla_w_ukv", l],
                     nm + "_ukv").reshape(MLA_HEADS, MLA_NOPE + MLA_V, s)
    k_pe = _rope(k_rope, cos, sin)
    k = jnp.concatenate([kvt[:, :MLA_NOPE].transpose(0, 2, 1),
                         jnp.broadcast_to(k_pe[None], (MLA_HEADS, s, MLA_ROPE))], axis=-1)
    vt = kvt[:, MLA_NOPE:].reshape(MLA_HEADS * MLA_V, s)
    ot = attention(qt, k, vt, 1.0 / math.sqrt(dqk), True, nm + "_att")
    return matmul_w_at(ot, w16["mla_w_o", l], toks["mla_w_o", l], nm + "_o")


def _hgrn_mixer(q, f_logit, i_in, g, lb, sp, l, w16, toks):
    s = q.shape[0]
    nm = "hg%d" % l
    f = lb + (1.0 - lb) * jax.nn.sigmoid(f_logit)
    o = hgrn_core(jax.nn.silu(q), 1.0 - f, i_in, jnp.log(f), nm + "_core")
    o = rmsnorm(o.reshape(s * HG_HEADS, HG_DV), sp["hg_g_norm", l], nm + "_gn").reshape(s, HG_HEADS * HG_DV)
    return matmul_w(o * jax.nn.silu(g), w16["hg_w_o", l], toks["hg_w_o", l], nm + "_o")


def _cross_attention(h, mem_n, l, w16, toks):
    s, m = h.shape[0], mem_n.shape[0]
    nm = "xat%d" % l
    xw = X_HEADS * X_HEAD_DIM
    qt = matmul_w_t(h, w16["x_w_q", l], toks["x_w_q", l], nm + "_q")
    kv = matmul_w(mem_n, w16["x_w_kv", l], toks["x_w_kv", l], nm + "_kv")
    k, v = split_cols(kv, ((xw, xw), 0))
    k = k.reshape(m, X_HEADS, X_HEAD_DIM).transpose(1, 0, 2)
    ot = attention(qt, k, v.T, 1.0 / math.sqrt(X_HEAD_DIM), False, nm + "_att")
    return matmul_w_at(ot, w16["x_w_o", l], toks["x_w_o", l], nm + "_o")


_IN_HG = SSM_WIDTH + MLA_Q_RANK + MLA_KV_RANK
_IN_GATE = _IN_HG + 4 * HG_HEADS * HG_DK
_IN_ROPE = _IN_GATE + 3 * D_MODEL


def _perm_in(w):
    pad = jnp.zeros(w.shape[:-1] + (D_IN_PAD - D_IN,), w.dtype)
    return jnp.concatenate([w[..., :_IN_HG], w[..., _IN_HG + MLA_ROPE:], w[..., _IN_HG:_IN_HG + MLA_ROPE], pad], axis=-1)


def _unperm_in(g):
    return jnp.concatenate([g[..., :_IN_HG], g[..., _IN_ROPE:_IN_ROPE + MLA_ROPE], g[..., _IN_HG:_IN_ROPE]], axis=-1)


def _layer(x, mem, cos, sin, lb, l, sp, w16, toks):
    s = x.shape[0]
    nm = "l%d" % l
    h = rmsnorm(x, sp["norm_mix", l], nm + "_nmix")
    z = matmul_w(h, w16["w_in", l], toks["w_in", l], nm + "_in")
    hw = HG_HEADS * HG_DK
    sizes = (SSM_WIDTH, MLA_Q_RANK, MLA_KV_RANK, hw, hw, hw, hw, 3 * D_MODEL, MLA_ROPE)
    u, q_lat, kv_lat, hg_q, hg_f, hg_i, hg_g, gate_logits, k_rope = split_cols(z, (sizes, D_IN_PAD - D_IN))
    g_ssm, g_mla, g_hg = split_cols(jax.nn.sigmoid(gate_logits), ((D_MODEL,) * 3, 0))
    y_ssm = _s5_mixer(u, sp, l, w16, toks)
    y_mla = _mla_mixer(q_lat, kv_lat, k_rope, cos, sin, sp, l, w16, toks)
    y_hg = _hgrn_mixer(hg_q, hg_f, hg_i, hg_g, lb, sp, l, w16, toks)
    merged = g_ssm * y_ssm + g_mla * y_mla + g_hg * y_hg
    x = x + matmul_w(merged, w16["w_out", l], toks["w_out", l], nm + "_out")
    hc = rmsnorm(x, sp["norm_cross", l], nm + "_ncross")
    mem_n = rmsnorm(mem, sp["norm_mem", l], nm + "_nmem")
    x = x + _cross_attention(hc, mem_n, l, w16, toks)
    hf = rmsnorm(x, sp["norm_ffn", l], nm + "_nffn")
    gu = matmul_w(hf, w16["ffn_w_gate_up", l], toks["ffn_w_gate_up", l], nm + "_gu")
    gt, up = split_cols(gu, ((D_FF, D_FF), 0))
    return x + matmul_w(jax.nn.silu(gt) * up, w16["ffn_w_down", l], toks["ffn_w_down", l], nm + "_down")


def _loss_fn(sp, toks, x, w16, mem, pos, target):
    half = MLA_ROPE // 2
    inv_freq = ROPE_THETA ** (-jnp.arange(half, dtype=F32) / half)
    ang = pos.astype(F32)[:, None] * inv_freq
    cos, sin = jnp.cos(ang), jnp.sin(ang)
    lb_p = jax.nn.softmax(sp["hg_lb", ALL], axis=0)
    lower = jnp.cumsum(lb_p, axis=0) - lb_p[0:1]
    for l in range(DEPTH):
        x = _layer(x, mem, cos, sin, lower[l], l, sp, w16, toks)
    y = rmsnorm(x, sp["norm_final", ALL], "nfinal")
    return 0.5 * jnp.sum(jnp.mean(jnp.square(y - target), axis=-1))


WEIGHTS = ["norm_mix", "w_in", "ssm_lam_re", "ssm_lam_im", "ssm_b_re", "ssm_b_im", "ssm_c_re", "ssm_c_im", "ssm_d",
           "ssm_log_step", "ssm_w_glu", "mla_q_norm", "mla_kv_norm", "mla_w_uq", "mla_w_ukv", "mla_w_o", "hg_lb",
           "hg_g_norm", "hg_w_o", "w_out", "norm_cross", "norm_mem", "x_w_q", "x_w_kv", "x_w_o", "norm_ffn",
           "ffn_w_gate_up", "ffn_w_down", "norm_final"]
INPUTS = ["x", "mem", "positions"] + WEIGHTS + ["loss_target"] + ["m_" + n for n in WEIGHTS] + ["v_" + n for n in WEIGHTS]
BIG = [("w_in", 2), ("ssm_w_glu", 2), ("mla_w_uq", 2), ("mla_w_ukv", 2), ("mla_w_o", 2), ("hg_w_o", 2), ("w_out", 1),
       ("x_w_q", 1), ("x_w_kv", 1), ("x_w_o", 2), ("ffn_w_gate_up", 2), ("ffn_w_down", 1)]
SMALL = [n for n in WEIGHTS if n not in dict(BIG)]
WHOLE = ("hg_lb", "norm_final")
ALL = -1
SMALL_W = 128
N_CHIPS = 4


def _pack(arrs, rows, width, dtype):
    flat = jnp.concatenate([a.astype(dtype).reshape(-1) for a in arrs])
    return jnp.pad(flat, (0, rows * width - flat.shape[0])).reshape(rows, width)


def _unpack(buf, like):
    flat = buf.reshape(-1)
    out, off = [], 0
    for a in like:
        n = math.prod(a.shape)
        out.append(flat[off:off + n].reshape(a.shape))
        off += n
    return out


def kernel(
    x, mem, positions, norm_mix, w_in, ssm_lam_re, ssm_lam_im, ssm_b_re, ssm_b_im, ssm_c_re, ssm_c_im, ssm_d,
    ssm_log_step, ssm_w_glu, mla_q_norm, mla_kv_norm, mla_w_uq, mla_w_ukv, mla_w_o, hg_lb, hg_g_norm, hg_w_o,
    w_out, norm_cross, norm_mem, x_w_q, x_w_kv, x_w_o, norm_ffn, ffn_w_gate_up, ffn_w_down, norm_final,
    loss_target, m_norm_mix, m_w_in, m_ssm_lam_re, m_ssm_lam_im, m_ssm_b_re, m_ssm_b_im, m_ssm_c_re, m_ssm_c_im,
    m_ssm_d, m_ssm_log_step, m_ssm_w_glu, m_mla_q_norm, m_mla_kv_norm, m_mla_w_uq, m_mla_w_ukv, m_mla_w_o,
    m_hg_lb, m_hg_g_norm, m_hg_w_o, m_w_out, m_norm_cross, m_norm_mem, m_x_w_q, m_x_w_kv, m_x_w_o, m_norm_ffn,
    m_ffn_w_gate_up, m_ffn_w_down, m_norm_final, v_norm_mix, v_w_in, v_ssm_lam_re, v_ssm_lam_im, v_ssm_b_re,
    v_ssm_b_im, v_ssm_c_re, v_ssm_c_im, v_ssm_d, v_ssm_log_step, v_ssm_w_glu, v_mla_q_norm, v_mla_kv_norm,
    v_mla_w_uq, v_mla_w_ukv, v_mla_w_o, v_hg_lb, v_hg_g_norm, v_hg_w_o, v_w_out, v_norm_cross, v_norm_mem,
    v_x_w_q, v_x_w_kv, v_x_w_o, v_norm_ffn, v_ffn_w_gate_up, v_ffn_w_down, v_norm_final):
    given = locals()
    a = {n: given[n] for n in INPUTS}
    x, mem, pos, target = a["x"][0], a["mem"][0], a["positions"][0], a["loss_target"][0]
    place = jnp.stack([lax.axis_index("c"), 2 * lax.axis_index("x") + lax.axis_index("y")]).astype(jnp.int32)

    two = lambda t: t.reshape(t.shape[0] * t.shape[1], t.shape[2])
    gathered = _gather_weights([two(a[n]).astype(BF16) for n, _ in BIG])
    w16, toks = {}, {}
    for (n, ax), got in zip(BIG, gathered):
        per = a[n].shape[1]
        for l in range(DEPTH):
            blk = [got[q, l * per:(l + 1) * per] for q in range(N_CHIPS)]
            if n == "w_in":
                cut = _IN_HG + MLA_ROPE
                pad = jnp.zeros((per, D_IN_PAD - D_IN), BF16)
                full = jnp.concatenate([blk[0][:, :_IN_HG], blk[0][:, cut:], blk[1], blk[2], blk[3],
                                        blk[0][:, _IN_HG:cut], pad], axis=1)
            else:
                full = jnp.concatenate(blk, axis=ax - 1)
            w16[n, l] = full
            toks[n, l] = jnp.zeros(full.shape, F32)
    sp = {}
    for n in SMALL:
        if n in WHOLE:
            sp[n, ALL] = a[n]
        else:
            for l in range(DEPTH):
                sp[n, l] = a[n][l]

    loss, (g_sp, g_tok, g_x) = jax.value_and_grad(_loss_fn, argnums=(0, 1, 2))(sp, toks, x, w16, mem, pos, target)
    g_small = {n: g_sp[n, ALL] if n in WHOLE else jnp.stack([g_sp[n, l] for l in range(DEPTH)]) for n in SMALL}

    grads = []
    for n, ax in BIG:
        layers = []
        for l in range(DEPTH):
            g = _unperm_in(g_tok[n, l]) if n == "w_in" else g_tok[n, l]
            if ax == 2:
                layers.append(g.reshape(g.shape[0], N_CHIPS, g.shape[1] // N_CHIPS).transpose(1, 0, 2))
            else:
                layers.append(g.reshape(N_CHIPS, g.shape[0] // N_CHIPS, g.shape[1]))
        grads.append(jnp.concatenate(layers, axis=1))
    names = [n for n, _ in BIG]
    from_sibling = _swap_halves_to_sibling(grads)
    chip_sums = [_add_half(g, t, place, "grad_add_half_" + n) for n, g, t in zip(names, grads, from_sibling)]
    from_chips = _scatter_partials(chip_sums)
    totals = [_add_partials(g, t1, t2, place, "grad_add_partials_" + n)
              for n, g, t1, t2 in zip(names, grads, from_sibling, from_chips)]
    g_big = {n: g.reshape(a[n].shape) for n, g in zip(names, _join_halves(totals))}

    small_like = [a[n] for n in SMALL] + [jnp.zeros((1,), F32)]
    n_small = sum(math.prod(t.shape) for t in small_like)
    srows = -(-n_small // (SMALL_W * 8 * N_DEV)) * 8 * N_DEV
    reduced = _allreduce_small(_pack([g_small[n] for n in SMALL] + [loss.reshape(1)], srows, SMALL_W, F32))
    red = _unpack(reduced, small_like)
    g_sm = dict(zip(SMALL, red[:-1]))
    loss = red[-1].reshape(())

    out_g, out_d, out_m, out_v = {}, {}, {}, {}
    for n, _ in BIG:
        shp = a[n].shape
        two = lambda t: t.reshape(shp[0] * shp[1], shp[2])
        d, m, v = _adamw_call(two(a[n]), two(g_big[n]), two(a["m_" + n]), two(a["v_" + n]), "adamw_" + n)
        out_g[n], out_d[n], out_m[n], out_v[n] = g_big[n], d.reshape(shp), m.reshape(shp), v.reshape(shp)
    small_arrs = [a[n] for n in SMALL]
    d, m, v = _adamw_call(
        _pack(small_arrs, srows, SMALL_W, F32), reduced,
        _pack([a["m_" + n] for n in SMALL], srows, SMALL_W, F32),
        _pack([a["v_" + n] for n in SMALL], srows, SMALL_W, F32), "adamw_small")
    for n, gg, dd, mm_, vv in zip(SMALL, red[:-1], _unpack(d, small_arrs), _unpack(m, small_arrs), _unpack(v, small_arrs)):
        out_g[n], out_d[n], out_m[n], out_v[n] = gg, dd, mm_, vv

    return (loss, g_x[None], *[out_g[n] for n in WEIGHTS], *[out_d[n] for n in WEIGHTS],
            *[out_m[n] for n in WEIGHTS], *[out_v[n] for n in WEIGHTS])
```

```python
import functools
import math

import jax
import jax.numpy as jnp
from jax import lax
from jax.experimental import pallas as pl
from jax.experimental.pallas import tpu as pltpu

F32 = jnp.float32
BF16 = jnp.bfloat16
MESH = pl.DeviceIdType.MESH

D_MODEL = 1024
DEPTH = 2
RMS_EPS = 1e-6
SSM_GROUPS, SSM_GROUP_CH, SSM_STATE = 32, 16, 64
SSM_WIDTH = SSM_GROUPS * SSM_GROUP_CH
SSM_LANES = SSM_GROUPS * SSM_STATE
SSM_PACK = 8
MLA_HEADS, MLA_Q_RANK, MLA_KV_RANK, MLA_NOPE, MLA_ROPE, MLA_V = 8, 512, 256, 64, 32, 64
ROPE_THETA = 10000.0
HG_HEADS, HG_DK, HG_DV, HG_CHUNK, HG_SUB = 4, 128, 128, 64, 16
X_HEADS, X_HEAD_DIM = 4, 128
D_FF = 2816
D_IN = 6432
D_IN_PAD = 6656
MASK_VALUE = -1e30

ADAM_LR, ADAM_B1, ADAM_B2, ADAM_EPS, ADAM_WD, ADAM_STEP = 0.001, 0.9, 0.999, 1e-08, 0.01, 10

VMEM_LIMIT = 48 * 1024 * 1024


def _pcall(body, **kw):
    return pl.pallas_call(body, **kw)


def _params(sem):
    return pltpu.CompilerParams(dimension_semantics=sem, vmem_limit_bytes=VMEM_LIMIT)


def _tile(n, cap, align=128):
    if n <= cap:
        return n
    t = (cap // align) * align
    while t >= align:
        if n % t == 0:
            return t
        t -= align
    return n


MM_VMEM_BUDGET = 36 * 1024 * 1024
MM_TILE_CAP = 2048


def _divisors(n, cap, align=128):
    out = [n] if n <= cap else []
    out += [t for t in range(align, min(n, cap + 1), align) if n % t == 0]
    return sorted(set(out), reverse=True)


@functools.lru_cache(maxsize=None)
def _mm_tiles(m, n, k, a_bytes, b_bytes, o_bytes):
    best = None
    for tm in _divisors(m, MM_TILE_CAP):
        for tn in _divisors(n, MM_TILE_CAP):
            for tk in _divisors(k, MM_TILE_CAP):
                nk = k // tk
                need = 2 * (tm * tk * a_bytes + tk * tn * b_bytes + tm * tn * o_bytes)
                need += tm * tn * 4 * (2 if nk > 1 else 1)
                need += (tm * tk * 2 if a_bytes > 2 else 0) + (tk * tn * 2 if b_bytes > 2 else 0)
                if need > MM_VMEM_BUDGET:
                    continue
                key = ((m // tm) * (n // tn) * nk, nk, -tn)
                if best is None or key < best[0]:
                    best = (key, (tm, tn, tk))
    assert best is not None, (m, n, k)
    return best[1]
def _mm(a, b, *, ta=False, tb=False, out_dtype=F32, name="mm"):
    if ta:
        K, M = a.shape
    else:
        M, K = a.shape
    if tb:
        N, Kb = b.shape
    else:
        Kb, N = b.shape
    assert K == Kb, (a.shape, b.shape, ta, tb)
    tm, tn, tk = _mm_tiles(M, N, K, a.dtype.itemsize, b.dtype.itemsize, jnp.dtype(out_dtype).itemsize)
    nk = K // tk
    dims = (((0,) if ta else (1,), (1,) if tb else (0,)), ((), ()))

    def body(a_ref, b_ref, o_ref, *acc):
        av = a_ref[...].astype(BF16)
        bv = b_ref[...].astype(BF16)
        part = lax.dot_general(av, bv, dims, preferred_element_type=F32)
        if nk == 1:
            o_ref[...] = part.astype(o_ref.dtype)
        else:
            acc_ref = acc[0]
            k = pl.program_id(2)

            @pl.when(k == 0)
            def _():
                acc_ref[...] = part

            @pl.when(k > 0)
            def _():
                acc_ref[...] += part

            @pl.when(k == nk - 1)
            def _():
                o_ref[...] = acc_ref[...].astype(o_ref.dtype)

    a_spec = pl.BlockSpec((tk, tm), lambda i, j, k: (k, i)) if ta else pl.BlockSpec((tm, tk), lambda i, j, k: (i, k))
    b_spec = pl.BlockSpec((tn, tk), lambda i, j, k: (j, k)) if tb else pl.BlockSpec((tk, tn), lambda i, j, k: (k, j))
    return _pcall(
        body,
        name=name,
        grid=(M // tm, N // tn, nk),
        in_specs=[a_spec, b_spec],
        out_specs=pl.BlockSpec((tm, tn), lambda i, j, k: (i, j)),
        out_shape=jax.ShapeDtypeStruct((M, N), out_dtype),
        scratch_shapes=[pltpu.VMEM((tm, tn), F32)] if nk > 1 else [],
        compiler_params=_params(("parallel", "parallel", "arbitrary")),
    )(a, b)


def _bd_call(a, bc, transpose, name):
    m = a.shape[0]
    nb, ka, kb = bc.shape
    win, wout = (kb, ka) if transpose else (ka, kb)
    assert a.shape[1] == nb * win
    tm = _tile(m, 1024)
    dims = _NT if transpose else (((1,), (0,)), ((), ()))

    def body(a_ref, b_ref, o_ref):
        o_ref[...] = lax.dot_general(a_ref[...].astype(BF16), b_ref[0].astype(BF16), dims, preferred_element_type=F32)

    return _pcall(
        body,
        name=name,
        grid=(m // tm, nb),
        in_specs=[pl.BlockSpec((tm, win), lambda i, j: (i, j)), pl.BlockSpec((1, ka, kb), lambda i, j: (j, 0, 0))],
        out_specs=pl.BlockSpec((tm, wout), lambda i, j: (i, j)),
        out_shape=jax.ShapeDtypeStruct((m, nb * wout), F32),
        compiler_params=_params(("parallel", "parallel")),
    )(a, bc)


def _bd_dw_call(a, g, nb, name):
    m = a.shape[0]
    ka, kb = a.shape[1] // nb, g.shape[1] // nb
    tk = _tile(m, 1024)
    nk = m // tk

    def body(a_ref, g_ref, o_ref):
        part = lax.dot_general(a_ref[...].astype(BF16), g_ref[...].astype(BF16), _TN, preferred_element_type=F32)
        k = pl.program_id(1)

        @pl.when(k == 0)
        def _():
            o_ref[0] = part

        @pl.when(k > 0)
        def _():
            o_ref[0] += part

    return _pcall(
        body,
        name=name,
        grid=(nb, nk),
        in_specs=[pl.BlockSpec((tk, ka), lambda j, k: (k, j)), pl.BlockSpec((tk, kb), lambda j, k: (k, j))],
        out_specs=pl.BlockSpec((1, ka, kb), lambda j, k: (j, 0, 0)),
        out_shape=jax.ShapeDtypeStruct((nb, ka, kb), F32),
        compiler_params=_params(("parallel", "arbitrary")),
    )(a, g)


@functools.partial(jax.custom_vjp, nondiff_argnums=(2,))
def bd_matmul(a, bc, name):
    return _bd_call(a, bc, False, name + "_f")


def _bd_matmul_fwd(a, bc, name):
    return _bd_call(a, bc, False, name + "_f"), (a, bc)


def _bd_matmul_bwd(name, res, g):
    a, bc = res
    return _bd_call(g, bc, True, name + "_da"), _bd_dw_call(a, g, bc.shape[0], name + "_dw")


bd_matmul.defvjp(_bd_matmul_fwd, _bd_matmul_bwd)


@functools.partial(jax.custom_vjp, nondiff_argnums=(3,))
def matmul_w(a, w16, tok, name):
    return _mm(a, w16, name=name + "_f")


def _matmul_w_fwd(a, w16, tok, name):
    return _mm(a, w16, name=name + "_f"), (a, w16)


def _matmul_w_bwd(name, res, g):
    a, w16 = res
    return _mm(g, w16, tb=True, name=name + "_da"), jnp.zeros_like(w16), _mm(a, g, ta=True, name=name + "_dw")


matmul_w.defvjp(_matmul_w_fwd, _matmul_w_bwd)


@functools.partial(jax.custom_vjp, nondiff_argnums=(3,))
def matmul_w_t(a, w16, tok, name):
    return _mm(w16, a, ta=True, tb=True, name=name + "_f")


def _matmul_w_t_fwd(a, w16, tok, name):
    return _mm(w16, a, ta=True, tb=True, name=name + "_f"), (a, w16)


def _matmul_w_t_bwd(name, res, g):
    a, w16 = res
    return (_mm(g, w16, ta=True, tb=True, name=name + "_da"), jnp.zeros_like(w16),
            _mm(a, g, ta=True, tb=True, name=name + "_dw"))


matmul_w_t.defvjp(_matmul_w_t_fwd, _matmul_w_t_bwd)


@functools.partial(jax.custom_vjp, nondiff_argnums=(3,))
def matmul_w_at(at, w16, tok, name):
    return _mm(at, w16, ta=True, name=name + "_f")


def _matmul_w_at_fwd(at, w16, tok, name):
    return _mm(at, w16, ta=True, name=name + "_f"), (at, w16)


def _matmul_w_at_bwd(name, res, g):
    at, w16 = res
    return _mm(w16, g, tb=True, name=name + "_da"), jnp.zeros_like(w16), _mm(at, g, name=name + "_dw")


matmul_w_at.defvjp(_matmul_w_at_fwd, _matmul_w_at_bwd)


def _rms_fwd_call(x, g, name):
    rows, d = x.shape
    tr = _tile(rows, 512, 8)

    def body(x_ref, g_ref, y_ref):
        xv = x_ref[...]
        rstd = lax.rsqrt(jnp.mean(xv * xv, axis=-1, keepdims=True) + RMS_EPS)
        y_ref[...] = xv * rstd * g_ref[...]

    return _pcall(
        body,
        name=name,
        grid=(rows // tr,),
        in_specs=[pl.BlockSpec((tr, d), lambda i: (i, 0)), pl.BlockSpec((1, d), lambda i: (0, 0))],
        out_specs=pl.BlockSpec((tr, d), lambda i: (i, 0)),
        out_shape=jax.ShapeDtypeStruct((rows, d), F32),
        compiler_params=_params(("parallel",)),
    )(x, g.reshape(1, d))


def _rms_bwd_call(x, g, dy, name):
    rows, d = x.shape
    tr = _tile(rows, 512, 8)
    nb = rows // tr

    def body(x_ref, g_ref, dy_ref, dx_ref, dg_ref):
        xv = x_ref[...]
        rstd = lax.rsqrt(jnp.mean(xv * xv, axis=-1, keepdims=True) + RMS_EPS)
        xh = xv * rstd
        dyv = dy_ref[...]
        dyg = dyv * g_ref[...]
        dx_ref[...] = rstd * (dyg - xh * jnp.mean(dyg * xh, axis=-1, keepdims=True))
        dg_ref[...] = jnp.sum((dyv * xh).reshape(tr // 8, 8, d), axis=0)

    dx, dgp = _pcall(
        body,
        name=name,
        grid=(nb,),
        in_specs=[
            pl.BlockSpec((tr, d), lambda i: (i, 0)),
            pl.BlockSpec((1, d), lambda i: (0, 0)),
            pl.BlockSpec((tr, d), lambda i: (i, 0)),
        ],
        out_specs=[pl.BlockSpec((tr, d), lambda i: (i, 0)), pl.BlockSpec((8, d), lambda i: (i, 0))],
        out_shape=[jax.ShapeDtypeStruct((rows, d), F32), jax.ShapeDtypeStruct((nb * 8, d), F32)],
        compiler_params=_params(("parallel",)),
    )(x, g.reshape(1, d), dy)
    return dx, jnp.sum(dgp, axis=0)


@functools.partial(jax.custom_vjp, nondiff_argnums=(2,))
def rmsnorm(x, g, name):
    return _rms_fwd_call(x, g, name + "_f")


def _rmsnorm_fwd(x, g, name):
    return _rms_fwd_call(x, g, name + "_f"), (x, g)


def _rmsnorm_bwd(name, res, dy):
    x, g = res
    return _rms_bwd_call(x, g, dy, name + "_b")


rmsnorm.defvjp(_rmsnorm_fwd, _rmsnorm_bwd)


def _attn_blocks(sq, sk):
    return _tile(sq, 1024), _tile(sk, 512)


def _attn_pt(qs, k_ref, qi, ki, bq, bk, masked, shift):
    st = jnp.dot(k_ref[0].astype(BF16), qs, preferred_element_type=F32)
    if masked:
        kpos = ki * bk + lax.broadcasted_iota(jnp.int32, (bk, bq), 0)
        qpos = qi * bq + lax.broadcasted_iota(jnp.int32, (bk, bq), 1)
        st = jnp.where(kpos <= qpos, st, MASK_VALUE)
    return st if shift is None else jnp.exp(st - shift)


def _attn_pairs(nq, nk, bq, bk, causal, k_major):
    ok = lambda i, j: (not causal) or j * bk <= i * bq + bq - 1
    if k_major:
        pairs = [(i, j) for j in range(nk) for i in range(nq) if ok(i, j)]
    else:
        pairs = [(i, j) for i in range(nq) for j in range(nk) if ok(i, j)]
    qtab = jnp.asarray([p[0] for p in pairs], jnp.int32)
    ktab = jnp.asarray([p[1] for p in pairs], jnp.int32)
    first_q = (lambda j: (j * bk) // bq) if causal else (lambda j: 0)
    last_k = (lambda i: jnp.minimum((i * bq + bq - 1) // bk, nk - 1)) if causal else (lambda i: nk - 1)
    return qtab, ktab, first_q, last_k


def _attn_cases(causal, qi, ki, bq, bk, step):
    if not causal:
        step(False)
        return
    pl.when(ki * bk + bk - 1 <= qi * bq)(functools.partial(step, False))
    pl.when(ki * bk + bk - 1 > qi * bq)(functools.partial(step, True))


def _attn_fwd_call(qt, k, vt, scale, causal, name):
    h, sk, dq = k.shape
    sq = qt.shape[1]
    dv = vt.shape[0] // h
    bq, bk = _attn_blocks(sq, sk)
    nq, nk = sq // bq, sk // bk

    qtab, ktab, _, last_k = _attn_pairs(nq, nk, bq, bk, causal, False)

    def body(qi_ref, ki_ref, q_ref, k_ref, v_ref, o_ref, lse_ref, m_sc, l_sc, acc_sc):
        qi, ki = qi_ref[pl.program_id(1)], ki_ref[pl.program_id(1)]

        @pl.when(ki == 0)
        def _():
            m_sc[...] = jnp.full(m_sc.shape, -jnp.inf, F32)
            l_sc[...] = jnp.zeros(l_sc.shape, F32)
            acc_sc[...] = jnp.zeros(acc_sc.shape, F32)

        def step(masked):
            qs = (q_ref[...] * scale).astype(BF16)
            st = _attn_pt(qs, k_ref, qi, ki, bq, bk, masked, None)
            m_prev = m_sc[...]
            m_new = jnp.maximum(m_prev, jnp.max(st, axis=0, keepdims=True))
            alpha = jnp.exp(m_prev - m_new)
            pt = jnp.exp(st - m_new)
            l_sc[...] = alpha * l_sc[...] + jnp.sum(pt, axis=0, keepdims=True)
            acc_sc[...] = alpha * acc_sc[...] + jnp.dot(v_ref[...].astype(BF16), pt.astype(BF16),
                                                        preferred_element_type=F32)
            m_sc[...] = m_new

        _attn_cases(causal, qi, ki, bq, bk, step)

        @pl.when(ki == last_k(qi))
        def _():
            o_ref[...] = acc_sc[...] / l_sc[...]
            lse_ref[0] = m_sc[...] + jnp.log(l_sc[...])

    qmap = lambda hh, p, qi_ref, ki_ref: (hh, qi_ref[p])
    return _pcall(
        body,
        name=name,
        grid_spec=pltpu.PrefetchScalarGridSpec(
            num_scalar_prefetch=2,
            grid=(h, qtab.shape[0]),
            in_specs=[
                pl.BlockSpec((dq, bq), qmap),
                pl.BlockSpec((1, bk, dq), lambda hh, p, qi_ref, ki_ref: (hh, ki_ref[p], 0)),
                pl.BlockSpec((dv, bk), lambda hh, p, qi_ref, ki_ref: (hh, ki_ref[p])),
            ],
            out_specs=[pl.BlockSpec((dv, bq), qmap),
                       pl.BlockSpec((1, 1, bq), lambda hh, p, qi_ref, ki_ref: (hh, 0, qi_ref[p]))],
            scratch_shapes=[pltpu.VMEM((1, bq), F32), pltpu.VMEM((1, bq), F32), pltpu.VMEM((dv, bq), F32)],
        ),
        out_shape=[jax.ShapeDtypeStruct((h * dv, sq), F32), jax.ShapeDtypeStruct((h, 1, sq), F32)],
        compiler_params=_params(("parallel", "arbitrary")),
    )(qtab, ktab, qt, k, vt)


_TN = (((0,), (0,)), ((), ()))
_NT = (((1,), (1,)), ((), ()))


def _attn_dst(qs, k_ref, v_ref, do_ref, lse_ref, dl_ref, qi, ki, bq, bk, masked):
    pt = _attn_pt(qs, k_ref, qi, ki, bq, bk, masked, lse_ref[0])
    dpt = lax.dot_general(v_ref[...].astype(BF16), do_ref[...].astype(BF16), _TN, preferred_element_type=F32)
    return pt, pt * (dpt - dl_ref[0])


def _attn_dq_call(qt, k, vt, dot, lse, delta, scale, causal, name):
    h, sk, dq = k.shape
    sq = qt.shape[1]
    dv = vt.shape[0] // h
    bq, bk = _attn_blocks(sq, sk)
    nq, nk = sq // bq, sk // bk

    qtab, ktab, _, last_k = _attn_pairs(nq, nk, bq, bk, causal, False)

    def body(qi_ref, ki_ref, q_ref, k_ref, v_ref, do_ref, lse_ref, dl_ref, dq_ref, acc_sc):
        qi, ki = qi_ref[pl.program_id(1)], ki_ref[pl.program_id(1)]

        @pl.when(ki == 0)
        def _():
            acc_sc[...] = jnp.zeros(acc_sc.shape, F32)

        def step(masked):
            qs = (q_ref[...] * scale).astype(BF16)
            _, dst = _attn_dst(qs, k_ref, v_ref, do_ref, lse_ref, dl_ref, qi, ki, bq, bk, masked)
            acc_sc[...] += lax.dot_general(k_ref[0].astype(BF16), dst.astype(BF16), _TN, preferred_element_type=F32)

        _attn_cases(causal, qi, ki, bq, bk, step)

        @pl.when(ki == last_k(qi))
        def _():
            dq_ref[...] = acc_sc[...] * scale

    qmap = lambda hh, p, qi_ref, ki_ref: (hh, qi_ref[p])
    rowmap = lambda hh, p, qi_ref, ki_ref: (hh, 0, qi_ref[p])
    return _pcall(
        body,
        name=name,
        grid_spec=pltpu.PrefetchScalarGridSpec(
            num_scalar_prefetch=2,
            grid=(h, qtab.shape[0]),
            in_specs=[
                pl.BlockSpec((dq, bq), qmap),
                pl.BlockSpec((1, bk, dq), lambda hh, p, qi_ref, ki_ref: (hh, ki_ref[p], 0)),
                pl.BlockSpec((dv, bk), lambda hh, p, qi_ref, ki_ref: (hh, ki_ref[p])),
                pl.BlockSpec((dv, bq), qmap),
                pl.BlockSpec((1, 1, bq), rowmap),
                pl.BlockSpec((1, 1, bq), rowmap),
            ],
            out_specs=pl.BlockSpec((dq, bq), qmap),
            scratch_shapes=[pltpu.VMEM((dq, bq), F32)],
        ),
        out_shape=jax.ShapeDtypeStruct((h * dq, sq), F32),
        compiler_params=_params(("parallel", "arbitrary")),
    )(qtab, ktab, qt, k, vt, dot, lse, delta)


def _attn_dkv_call(qt, k, vt, dot, lse, delta, scale, causal, name):
    h, sk, dq = k.shape
    sq = qt.shape[1]
    dv = vt.shape[0] // h
    bq, bk = _attn_blocks(sq, sk)
    nq, nk = sq // bq, sk // bk

    qtab, ktab, first_q, _ = _attn_pairs(nq, nk, bq, bk, causal, True)

    def body(qi_ref, ki_ref, q_ref, k_ref, v_ref, do_ref, lse_ref, dl_ref, dk_ref, dv_ref, dk_sc, dv_sc):
        qi, ki = qi_ref[pl.program_id(1)], ki_ref[pl.program_id(1)]

        @pl.when(qi == first_q(ki))
        def _():
            dk_sc[...] = jnp.zeros(dk_sc.shape, F32)
            dv_sc[...] = jnp.zeros(dv_sc.shape, F32)

        def step(masked):
            qs = (q_ref[...] * scale).astype(BF16)
            pt, dst = _attn_dst(qs, k_ref, v_ref, do_ref, lse_ref, dl_ref, qi, ki, bq, bk, masked)
            dv_sc[...] += lax.dot_general(do_ref[...].astype(BF16), pt.astype(BF16), _NT, preferred_element_type=F32)
            dk_sc[...] += lax.dot_general(dst.astype(BF16), qs, _NT, preferred_element_type=F32)

        _attn_cases(causal, qi, ki, bq, bk, step)

        @pl.when(qi == nq - 1)
        def _():
            dk_ref[0] = dk_sc[...]
            dv_ref[...] = dv_sc[...]

    qmap = lambda hh, p, qi_ref, ki_ref: (hh, qi_ref[p])
    rowmap = lambda hh, p, qi_ref, ki_ref: (hh, 0, qi_ref[p])
    kmap = lambda hh, p, qi_ref, ki_ref: (hh, ki_ref[p], 0)
    vmap = lambda hh, p, qi_ref, ki_ref: (hh, ki_ref[p])
    return _pcall(
        body,
        name=name,
        grid_spec=pltpu.PrefetchScalarGridSpec(
            num_scalar_prefetch=2,
            grid=(h, qtab.shape[0]),
            in_specs=[
                pl.BlockSpec((dq, bq), qmap),
                pl.BlockSpec((1, bk, dq), kmap),
                pl.BlockSpec((dv, bk), vmap),
                pl.BlockSpec((dv, bq), qmap),
                pl.BlockSpec((1, 1, bq), rowmap),
                pl.BlockSpec((1, 1, bq), rowmap),
            ],
            out_specs=[pl.BlockSpec((1, bk, dq), kmap), pl.BlockSpec((dv, bk), vmap)],
            scratch_shapes=[pltpu.VMEM((bk, dq), F32), pltpu.VMEM((dv, bk), F32)],
        ),
        out_shape=[jax.ShapeDtypeStruct((h, sk, dq), F32), jax.ShapeDtypeStruct((h * dv, sk), F32)],
        compiler_params=_params(("parallel", "arbitrary")),
    )(qtab, ktab, qt, k, vt, dot, lse, delta)


@functools.partial(jax.custom_vjp, nondiff_argnums=(3, 4, 5))
def attention(qt, k, vt, scale, causal, name):
    return _attn_fwd_call(qt, k, vt, scale, causal, name + "_f")[0]


def _attention_fwd(qt, k, vt, scale, causal, name):
    ot, lse = _attn_fwd_call(qt, k, vt, scale, causal, name + "_f")
    return ot, (qt, k, vt, ot, lse)


def _attention_bwd(scale, causal, name, res, dot):
    qt, k, vt, ot, lse = res
    h = k.shape[0]
    delta = jnp.sum((dot * ot).reshape(h, ot.shape[0] // h, ot.shape[1]), axis=1, keepdims=True)
    dqt = _attn_dq_call(qt, k, vt, dot, lse, delta, scale, causal, name + "_dq")
    dk, dvt = _attn_dkv_call(qt, k, vt, dot, lse, delta, scale, causal, name + "_dkv")
    return dqt, dk, dvt


attention.defvjp(_attention_fwd, _attention_bwd)


SCAN_ROWS = 8
SCAN_LANES = 256
SCAN_TBLOCK = 512


def _scan_tables(lr, li):
    pr, pi = [lr], [li]
    for _ in range(SCAN_ROWS - 1):
        pr, pi = pr + [pr[-1] * lr - pi[-1] * li], pi + [pr[-1] * li + pi[-1] * lr]
    return jnp.concatenate(pr, axis=0), jnp.concatenate(pi, axis=0)


def _scan_call(ur, ui, lr, li, reverse, name, states=None):
    s, n = ur.shape
    tb = _tile(s, SCAN_TBLOCK, 8)
    nt, nl = s // tb, n // SCAN_LANES
    ntile = tb // SCAN_ROWS
    pr, pi = _scan_tables(lr, li)
    if reverse:
        pr, pi = pr[::-1], pi[::-1]
    shape = (SCAN_ROWS, SCAN_LANES)
    with_states = states is not None
    assert reverse or not with_states

    def body(*refs):
        if with_states:
            ur_ref, ui_ref, pr_ref, pi_ref, sr_ref, si_ref, xr_ref, xi_ref, dr_ref, di_ref, cr_sc, ci_sc, ar_sc, ai_sc = refs
        else:
            ur_ref, ui_ref, pr_ref, pi_ref, xr_ref, xi_ref, cr_sc, ci_sc, ar_sc, ai_sc = refs

        @pl.when(pl.program_id(1) == 0)
        def _():
            for sc in (cr_sc, ci_sc, ar_sc, ai_sc):
                sc[...] = jnp.zeros(shape, F32)

        prv, piv = pr_ref[...], pi_ref[...]
        rows = lax.broadcasted_iota(jnp.int32, shape, 0)

        def powers(k):
            r = (SCAN_ROWS - k) if reverse else (k - 1)
            return jnp.broadcast_to(prv[r:r + 1], shape), jnp.broadcast_to(piv[r:r + 1], shape)

        pw = [powers(k) for k in (1, 2, 4)]

        def tile(i, carry):
            cr, ci, acr, aci = carry
            j = (ntile - 1 - i) if reverse else i
            sl = pl.ds(pl.multiple_of(j * SCAN_ROWS, SCAN_ROWS), SCAN_ROWS)
            xr, xi = ur_ref[sl, :], ui_ref[sl, :]
            for (ar, ai), k in zip(pw, (1, 2, 4)):
                if reverse:
                    keep = rows < SCAN_ROWS - k
                    sr = jnp.where(keep, pltpu.roll(xr, SCAN_ROWS - k, 0), 0.0)
                    si = jnp.where(keep, pltpu.roll(xi, SCAN_ROWS - k, 0), 0.0)
                else:
                    keep = rows >= k
                    sr = jnp.where(keep, pltpu.roll(xr, k, 0), 0.0)
                    si = jnp.where(keep, pltpu.roll(xi, k, 0), 0.0)
                xr, xi = xr + (ar * sr - ai * si), xi + (ar * si + ai * sr)
            xr, xi = xr + (prv * cr - piv * ci), xi + (prv * ci + piv * cr)
            xr_ref[sl, :] = xr
            xi_ref[sl, :] = xi
            if with_states:
                last = rows == SCAN_ROWS - 1
                nr = jnp.where(last, cr, pltpu.roll(xr, SCAN_ROWS - 1, 0))
                ni = jnp.where(last, ci, pltpu.roll(xi, SCAN_ROWS - 1, 0))
                sr, si = sr_ref[sl, :], si_ref[sl, :]
                acr, aci = acr + (sr * nr + si * ni), aci + (sr * ni - si * nr)
            e = 0 if reverse else SCAN_ROWS - 1
            return jnp.broadcast_to(xr[e:e + 1], shape), jnp.broadcast_to(xi[e:e + 1], shape), acr, aci

        cr, ci, acr, aci = lax.fori_loop(0, ntile, tile, (cr_sc[...], ci_sc[...], ar_sc[...], ai_sc[...]))
        cr_sc[...] = cr
        ci_sc[...] = ci
        if with_states:
            ar_sc[...] = acr
            ai_sc[...] = aci

            @pl.when(pl.program_id(1) == nt - 1)
            def _():
                dr_ref[...] = acr
                di_ref[...] = aci

    tmap = (lambda l, t: (nt - 1 - t, l)) if reverse else (lambda l, t: (t, l))
    blk = pl.BlockSpec((tb, SCAN_LANES), tmap)
    tab = pl.BlockSpec((SCAN_ROWS, SCAN_LANES), lambda l, t: (0, l))
    full = jax.ShapeDtypeStruct((s, n), F32)
    return _pcall(
        body,
        name=name,
        grid=(nl, nt),
        in_specs=[blk, blk, tab, tab] + ([blk, blk] if with_states else []),
        out_specs=[blk, blk] + ([tab, tab] if with_states else []),
        out_shape=[full, full] + ([jax.ShapeDtypeStruct((SCAN_ROWS, n), F32)] * 2 if with_states else []),
        scratch_shapes=[pltpu.VMEM(shape, F32)] * 4,
        compiler_params=_params(("parallel", "arbitrary")),
    )(ur, ui, pr, pi, *(states if with_states else ()))


@functools.partial(jax.custom_vjp, nondiff_argnums=(4,))
def s5_scan(ur, ui, lr, li, name):
    return tuple(_scan_call(ur, ui, lr, li, False, name + "_f"))


def _s5_scan_fwd(ur, ui, lr, li, name):
    xr, xi = _scan_call(ur, ui, lr, li, False, name + "_f")
    return (xr, xi), (xr, xi, lr, li)


def _s5_scan_bwd(name, res, g):
    xr, xi, lr, li = res
    gr, gi, dlr, dli = _scan_call(g[0], g[1], lr, -li, True, name + "_b", states=(xr, xi))
    return gr, gi, jnp.sum(dlr, axis=0, keepdims=True), jnp.sum(dli, axis=0, keepdims=True)


s5_scan.defvjp(_s5_scan_fwd, _s5_scan_bwd)


def _hg_chunk(q, k, v, lf, st):
    c = HG_CHUNK
    row = lax.broadcasted_iota(jnp.int32, (c, c), 0)
    col = lax.broadcasted_iota(jnp.int32, (c, c), 1)
    tri = (col <= row).astype(F32)
    b = jnp.dot(tri, lf, precision=lax.Precision.HIGHEST, preferred_element_type=F32)
    qb = q * jnp.exp(b)
    o = lax.dot_general(qb.astype(BF16), st.astype(BF16), (((1,), (1,)), ((), ())), preferred_element_type=F32)
    nsub = c // HG_SUB
    srow = lax.broadcasted_iota(jnp.int32, (HG_SUB, HG_SUB, 1), 0)
    scol = lax.broadcasted_iota(jnp.int32, (HG_SUB, HG_SUB, 1), 1)
    smask = scol <= srow
    outs = []
    for i in range(nsub):
        lo = i * HG_SUB
        bi, qi, ki = b[lo:lo + HG_SUB], q[lo:lo + HG_SUB], k[lo:lo + HG_SUB]
        diff = jnp.where(smask, bi[:, None, :] - bi[None, :, :], 0.0)
        e = jnp.where(smask, jnp.exp(diff), 0.0)
        a_ii = jnp.sum(qi[:, None, :] * ki[None, :, :] * e, axis=-1)
        oi = jnp.dot(a_ii.astype(BF16), v[lo:lo + HG_SUB].astype(BF16), preferred_element_type=F32)
        if i > 0:
            r = b[lo - 1:lo]
            qt = qi * jnp.exp(bi - r)
            kt = k[:lo] * jnp.exp(r - b[:lo])
            a_ij = lax.dot_general(qt.astype(BF16), kt.astype(BF16), (((1,), (1,)), ((), ())),
                                   preferred_element_type=F32)
            oi = oi + jnp.dot(a_ij.astype(BF16), v[:lo].astype(BF16), preferred_element_type=F32)
        outs.append(oi)
    o = o + jnp.concatenate(outs, axis=0)
    bl = b[c - 1:c]
    kd = k * jnp.exp(bl - b)
    st_new = st * jnp.exp(bl) + lax.dot_general(v.astype(BF16), kd.astype(BF16), (((0,), (0,)), ((), ())),
                                                preferred_element_type=F32)
    return o, st_new


def _hg_fwd_call(q, k, v, lf, name):
    s, w = q.shape
    nc = s // HG_CHUNK
    blk = pl.BlockSpec((HG_CHUNK, w), lambda i: (i, 0))

    def body(q_ref, k_ref, v_ref, lf_ref, o_ref, st_ref, st_sc):
        @pl.when(pl.program_id(0) == 0)
        def _():
            st_sc[...] = jnp.zeros(st_sc.shape, F32)

        for hh in range(HG_HEADS):
            cs = slice(hh * HG_DK, (hh + 1) * HG_DK)
            st = st_sc[hh]
            st_ref[0, hh] = st
            o, st_new = _hg_chunk(q_ref[:, cs], k_ref[:, cs], v_ref[:, cs], lf_ref[:, cs], st)
            o_ref[:, cs] = o
            st_sc[hh] = st_new

    return _pcall(
        body,
        name=name,
        grid=(nc,),
        in_specs=[blk, blk, blk, blk],
        out_specs=[blk, pl.BlockSpec((1, HG_HEADS, HG_DV, HG_DK), lambda i: (i, 0, 0, 0))],
        out_shape=[jax.ShapeDtypeStruct((s, w), F32), jax.ShapeDtypeStruct((nc, HG_HEADS, HG_DV, HG_DK), F32)],
        scratch_shapes=[pltpu.VMEM((HG_HEADS, HG_DV, HG_DK), F32)],
        compiler_params=_params(("arbitrary",)),
    )(q, k, v, lf)


def _hg_bwd_call(q, k, v, lf, sts, do, name):
    s, w = q.shape
    nc = s // HG_CHUNK
    blk = pl.BlockSpec((HG_CHUNK, w), lambda i: (nc - 1 - i, 0))

    def body(q_ref, k_ref, v_ref, lf_ref, st_ref, do_ref, dq_ref, dk_ref, dv_ref, dlf_ref, dst_sc):
        @pl.when(pl.program_id(0) == 0)
        def _():
            dst_sc[...] = jnp.zeros(dst_sc.shape, F32)

        for hh in range(HG_HEADS):
            cs = slice(hh * HG_DK, (hh + 1) * HG_DK)
            _, vjp = jax.vjp(_hg_chunk, q_ref[:, cs], k_ref[:, cs], v_ref[:, cs], lf_ref[:, cs], st_ref[0, hh])
            dq, dk, dv, dlf, dst = vjp((do_ref[:, cs], dst_sc[hh]))
            dq_ref[:, cs] = dq
            dk_ref[:, cs] = dk
            dv_ref[:, cs] = dv
            dlf_ref[:, cs] = dlf
            dst_sc[hh] = dst

    out = jax.ShapeDtypeStruct((s, w), F32)
    return _pcall(
        body,
        name=name,
        grid=(nc,),
        in_specs=[blk, blk, blk, blk, pl.BlockSpec((1, HG_HEADS, HG_DV, HG_DK), lambda i: (nc - 1 - i, 0, 0, 0)), blk],
        out_specs=[blk, blk, blk, blk],
        out_shape=[out, out, out, out],
        scratch_shapes=[pltpu.VMEM((HG_HEADS, HG_DV, HG_DK), F32)],
        compiler_params=_params(("arbitrary",)),
    )(q, k, v, lf, sts, do)


@functools.partial(jax.custom_vjp, nondiff_argnums=(4,))
def hgrn_core(q, k, v, lf, name):
    return _hg_fwd_call(q, k, v, lf, name + "_f")[0]


def _hgrn_core_fwd(q, k, v, lf, name):
    o, sts = _hg_fwd_call(q, k, v, lf, name + "_f")
    return o, (q, k, v, lf, sts)


def _hgrn_core_bwd(name, res, do):
    q, k, v, lf, sts = res
    return tuple(_hg_bwd_call(q, k, v, lf, sts, do, name + "_b"))


hgrn_core.defvjp(_hgrn_core_fwd, _hgrn_core_bwd)


def _adamw_call(w, g, m, v, name):
    rows, cols = w.shape
    tr = _tile(rows, 256, 8)
    c1 = 1.0 / (1.0 - ADAM_B1 ** ADAM_STEP)
    c2 = 1.0 / (1.0 - ADAM_B2 ** ADAM_STEP)

    def body(w_ref, g_ref, m_ref, v_ref, d_ref, mo_ref, vo_ref):
        gv = g_ref[...]
        mn = ADAM_B1 * m_ref[...] + (1.0 - ADAM_B1) * gv
        vn = ADAM_B2 * v_ref[...] + (1.0 - ADAM_B2) * (gv * gv)
        d_ref[...] = -ADAM_LR * ((mn * c1) / (jnp.sqrt(vn * c2) + ADAM_EPS) + ADAM_WD * w_ref[...])
        mo_ref[...] = mn
        vo_ref[...] = vn

    blk = pl.BlockSpec((tr, cols), lambda i: (i, 0))
    out = jax.ShapeDtypeStruct((rows, cols), F32)
    return _pcall(
        body,
        name=name,
        grid=(rows // tr,),
        in_specs=[blk, blk, blk, blk],
        out_specs=[blk, blk, blk],
        out_shape=[out, out, out],
        compiler_params=_params(("parallel",)),
    )(w, g, m, v)


ANY = pl.BlockSpec(memory_space=pl.ANY)


def _place():
    return lax.axis_index("x"), lax.axis_index("y"), lax.axis_index("c")


def _other_chips(x, y):
    return [(1 - x, y), (x, 1 - y), (1 - x, 1 - y)]


def _gather_weights(shards):
    nw = len(shards)

    def body(*refs):
        w_refs, g_refs, (send_sems, recv_sems) = refs[:nw], refs[nw:2 * nw], refs[2 * nw:]
        x, y, c = _place()
        myq = 2 * x + y
        sibling = (x, y, 1 - c)
        chips = _other_chips(x, y)

        def copy(k, src, dst, to):
            return pltpu.make_async_remote_copy(src_ref=src, dst_ref=dst, send_sem=send_sems.at[k],
                                                recv_sem=recv_sems.at[k], device_id=to, device_id_type=MESH)

        def halves(i):
            hh = shards[i].shape[0] // 2
            return pl.ds(c * hh, hh), pl.ds((1 - c) * hh, hh)

        first = []
        for i in range(nw):
            mine, _ = halves(i)
            for j, (px, py) in enumerate(chips):
                first.append(copy(6 * i + j, w_refs[i].at[mine], g_refs[i].at[myq, mine], (px, py, c)))
                first[-1].start()
        passed = []
        for i in range(nw):
            mine, _ = halves(i)
            for j, (px, py) in enumerate(chips):
                q = 2 * px + py
                copy(6 * i + j, w_refs[i].at[mine], g_refs[i].at[q, mine], (px, py, c)).wait_recv()
                passed.append(copy(6 * i + 3 + j, g_refs[i].at[q, mine], g_refs[i].at[q, mine], sibling))
                passed[-1].start()
        for i in range(nw):
            _, theirs = halves(i)
            for j, (px, py) in enumerate(chips):
                q = 2 * px + py
                copy(6 * i + 3 + j, g_refs[i].at[q, theirs], g_refs[i].at[q, theirs], sibling).wait_recv()
        for cp in first + passed:
            cp.wait_send()

    got = _pcall(
        body,
        name="gather_weights",
        in_specs=[ANY] * nw,
        out_specs=[ANY] * nw,
        out_shape=[jax.ShapeDtypeStruct((4,) + s.shape, s.dtype) for s in shards],
        scratch_shapes=[pltpu.SemaphoreType.DMA((6 * nw,)), pltpu.SemaphoreType.DMA((6 * nw,))],
    )(*shards)
    my_chip = 2 * lax.axis_index("x") + lax.axis_index("y")
    return [lax.dynamic_update_slice(g, s[None], (my_chip, 0, 0)) for g, s in zip(got, shards)]


def _swap_halves_to_sibling(grs):
    nw = len(grs)

    def body(*refs):
        g_refs, t_refs, (send_sems, recv_sems) = refs[:nw], refs[nw:2 * nw], refs[2 * nw:]
        x, y, c = _place()
        cps = []
        for i in range(nw):
            hh = grs[i].shape[1] // 2
            cps.append(pltpu.make_async_remote_copy(
                src_ref=g_refs[i].at[:, pl.ds((1 - c) * hh, hh)], dst_ref=t_refs[i], send_sem=send_sems.at[i],
                recv_sem=recv_sems.at[i], device_id=(x, y, 1 - c), device_id_type=MESH))
            cps[-1].start()
        for cp in cps:
            cp.wait()

    return _pcall(
        body,
        name="grad_d2d",
        in_specs=[ANY] * nw,
        out_specs=[ANY] * nw,
        out_shape=[jax.ShapeDtypeStruct((g.shape[0], g.shape[1] // 2, g.shape[2]), g.dtype) for g in grs],
        scratch_shapes=[pltpu.SemaphoreType.DMA((nw,)), pltpu.SemaphoreType.DMA((nw,))],
    )(*grs)


def _add_half(gr, t1, place, name):
    nq, r, wd = gr.shape
    hh = r // 2
    tr = _tile(hh, 512, 16)
    nb = hh // tr

    def body(p_ref, g_ref, t_ref, o_ref):
        o_ref[...] = (g_ref[...] + t_ref[...]).astype(BF16)

    return _pcall(
        body,
        name=name,
        grid_spec=pltpu.PrefetchScalarGridSpec(
            num_scalar_prefetch=1,
            grid=(nq, nb),
            in_specs=[
                pl.BlockSpec((1, tr, wd), lambda q, i, p_ref: (q, p_ref[0] * nb + i, 0)),
                pl.BlockSpec((1, tr, wd), lambda q, i, p_ref: (q, i, 0)),
            ],
            out_specs=pl.BlockSpec((1, tr, wd), lambda q, i, p_ref: (q, i, 0)),
        ),
        out_shape=jax.ShapeDtypeStruct((nq, hh, wd), BF16),
        compiler_params=_params(("parallel", "parallel")),
    )(place, gr, t1)


def _scatter_partials(ps):
    nw = len(ps)

    def body(*refs):
        p_refs, t_refs, (send_sems, recv_sems) = refs[:nw], refs[nw:2 * nw], refs[2 * nw:]
        x, y, c = _place()
        cps = []
        for i in range(nw):
            for j, (px, py) in enumerate(_other_chips(x, y)):
                cps.append(pltpu.make_async_remote_copy(
                    src_ref=p_refs[i].at[2 * px + py], dst_ref=t_refs[i].at[j], send_sem=send_sems.at[3 * i + j],
                    recv_sem=recv_sems.at[3 * i + j], device_id=(px, py, c), device_id_type=MESH))
                cps[-1].start()
        for cp in cps:
            cp.wait()

    return _pcall(
        body,
        name="grad_ici",
        in_specs=[ANY] * nw,
        out_specs=[ANY] * nw,
        out_shape=[jax.ShapeDtypeStruct((3,) + p.shape[1:], p.dtype) for p in ps],
        scratch_shapes=[pltpu.SemaphoreType.DMA((3 * nw,)), pltpu.SemaphoreType.DMA((3 * nw,))],
    )(*ps)


def _add_partials(gr, t1, t2, place, name):
    nq, r, wd = gr.shape
    hh = r // 2
    tr = _tile(hh, 512, 16)
    nb = hh // tr

    def body(p_ref, g_ref, s_ref, t_ref, o_ref):
        o_ref[...] = (((g_ref[0] + s_ref[0]) + t_ref[0].astype(F32)) + t_ref[1].astype(F32)) + t_ref[2].astype(F32)

    return _pcall(
        body,
        name=name,
        grid_spec=pltpu.PrefetchScalarGridSpec(
            num_scalar_prefetch=1,
            grid=(nb,),
            in_specs=[
                pl.BlockSpec((1, tr, wd), lambda i, p_ref: (p_ref[1], p_ref[0] * nb + i, 0)),
                pl.BlockSpec((1, tr, wd), lambda i, p_ref: (p_ref[1], i, 0)),
                pl.BlockSpec((3, tr, wd), lambda i, p_ref: (0, i, 0)),
            ],
            out_specs=pl.BlockSpec((tr, wd), lambda i, p_ref: (i, 0)),
        ),
        out_shape=jax.ShapeDtypeStruct((hh, wd), F32),
        compiler_params=_params(("parallel",)),
    )(place, gr, t1, t2)


def _join_halves(halves):
    nw = len(halves)

    def body(*refs):
        h_refs, f_refs, (send_sems, recv_sems) = refs[:nw], refs[nw:2 * nw], refs[2 * nw:]
        x, y, c = _place()

        def copy(i, rows):
            hh = halves[i].shape[0]
            return pltpu.make_async_remote_copy(src_ref=h_refs[i], dst_ref=f_refs[i].at[pl.ds(rows * hh, hh)],
                                                send_sem=send_sems.at[i], recv_sem=recv_sems.at[i],
                                                device_id=(x, y, 1 - c), device_id_type=MESH)

        for i in range(nw):
            copy(i, c).start()
        for i in range(nw):
            copy(i, c).wait_send()
            copy(i, 1 - c).wait_recv()

    got = _pcall(
        body,
        name="grad_join",
        in_specs=[ANY] * nw,
        out_specs=[ANY] * nw,
        out_shape=[jax.ShapeDtypeStruct((2 * h.shape[0], h.shape[1]), h.dtype) for h in halves],
        scratch_shapes=[pltpu.SemaphoreType.DMA((nw,)), pltpu.SemaphoreType.DMA((nw,))],
    )(*halves)
    c = lax.axis_index("c")
    return [lax.dynamic_update_slice(g, h, (c * h.shape[0], 0)) for g, h in zip(got, halves)]


N_DEV = 8


def _allreduce_small(buf):
    rows, wd = buf.shape
    n = rows // N_DEV

    def body(x_ref, o_ref, stage_ref, send_sems, recv_sems):
        x, y, c = _place()
        me = 4 * x + 2 * y + c

        def peer(k):
            return (x ^ (k >> 2), y ^ ((k >> 1) & 1), c ^ (k & 1))

        def chunk(ref, d):
            return ref.at[pl.ds(pl.multiple_of(d * n, 8), n)]

        first = []
        for k in range(1, N_DEV):
            px, py, pc = peer(k)
            cp = pltpu.make_async_remote_copy(src_ref=chunk(x_ref, 4 * px + 2 * py + pc), dst_ref=stage_ref.at[k],
                                              send_sem=send_sems.at[k - 1], recv_sem=recv_sems.at[k - 1],
                                              device_id=(px, py, pc), device_id_type=MESH)
            cp.start()
            first.append(cp)
        acc = chunk(x_ref, me)[...]
        for k in range(1, N_DEV):
            first[k - 1].wait_recv()
            acc = acc + stage_ref[k]
        chunk(o_ref, me)[...] = acc
        second = []
        for k in range(1, N_DEV):
            cp = pltpu.make_async_remote_copy(src_ref=chunk(o_ref, me), dst_ref=chunk(o_ref, me),
                                              send_sem=send_sems.at[6 + k], recv_sem=recv_sems.at[6 + k],
                                              device_id=peer(k), device_id_type=MESH)
            cp.start()
            second.append(cp)
        for k in range(1, N_DEV):
            px, py, pc = peer(k)
            them = 4 * px + 2 * py + pc
            pltpu.make_async_remote_copy(src_ref=chunk(o_ref, them), dst_ref=chunk(o_ref, them),
                                         send_sem=send_sems.at[6 + k], recv_sem=recv_sems.at[6 + k],
                                         device_id=(px, py, pc), device_id_type=MESH).wait_recv()
        for cp in first + second:
            cp.wait_send()

    vmem = pl.BlockSpec(memory_space=pltpu.VMEM)
    return _pcall(
        body,
        name="allreduce_small",
        in_specs=[vmem],
        out_specs=vmem,
        out_shape=jax.ShapeDtypeStruct((rows, wd), F32),
        scratch_shapes=[pltpu.VMEM((N_DEV, n, wd), F32), pltpu.SemaphoreType.DMA((14,)),
                        pltpu.SemaphoreType.DMA((14,))],
    )(buf)


@functools.partial(jax.custom_vjp, nondiff_argnums=(1,))
def split_cols(z, spec):
    outs, off = [], 0
    for n in spec[0]:
        outs.append(z[:, off:off + n])
        off += n
    return tuple(outs)


def _split_cols_fwd(z, spec):
    return split_cols(z, spec), None


def _split_cols_bwd(spec, _, gs):
    parts = list(gs)
    if spec[1]:
        parts.append(jnp.zeros((parts[0].shape[0], spec[1]), parts[0].dtype))
    return (jnp.concatenate(parts, axis=1),)


split_cols.defvjp(_split_cols_fwd, _split_cols_bwd)


def _rope(t, cos, sin):
    half = t.shape[-1] // 2
    t1, t2 = t[..., :half], t[..., half:]
    return jnp.concatenate([t1 * cos - t2 * sin, t2 * cos + t1 * sin], axis=-1)


def _s5_mixer(u, sp, l, w16, toks):
    g, p_, h = SSM_GROUPS, SSM_STATE, SSM_GROUP_CH
    lam_re, lam_im = sp["ssm_lam_re", l], sp["ssm_lam_im", l]
    step = jnp.exp(sp["ssm_log_step", l])[:, None]
    mag, ang = jnp.exp(lam_re * step), lam_im * step
    lbr, lbi = mag * jnp.cos(ang), mag * jnp.sin(ang)
    den = lam_re * lam_re + lam_im * lam_im
    nr, ni = lbr - 1.0, lbi
    qr, qi = (nr * lam_re + ni * lam_im) / den, (ni * lam_re - nr * lam_im) / den
    b_re, b_im = sp["ssm_b_re", l], sp["ssm_b_im", l]
    bbr = qr[..., None] * b_re - qi[..., None] * b_im
    bbi = qr[..., None] * b_im + qi[..., None] * b_re
    nb = g // SSM_PACK
    eye = jnp.eye(SSM_PACK, dtype=F32)

    def in_blocks(t):
        return jnp.einsum("jgph,gk->jghkp", t.reshape(nb, SSM_PACK, p_, h), eye).reshape(nb, SSM_PACK * h, SSM_PACK * p_)

    def out_blocks(t):
        return jnp.einsum("jghp,gk->jgpkh", t.reshape(nb, SSM_PACK, h, p_), eye).reshape(nb, SSM_PACK * p_, SSM_PACK * h)

    nm = "ssm%d" % l
    ur = bd_matmul(u, in_blocks(bbr), nm + "_bur")
    ui = bd_matmul(u, in_blocks(bbi), nm + "_bui")
    xr, xi = s5_scan(ur, ui, lbr.reshape(1, g * p_), lbi.reshape(1, g * p_), nm + "_scan")
    y = (bd_matmul(xr, out_blocks(sp["ssm_c_re", l]), nm + "_cr") + bd_matmul(xi, out_blocks(-sp["ssm_c_im", l]), nm + "_ci")
         + sp["ssm_d", l].reshape(1, g * h) * u)
    y = jax.nn.gelu(y)
    zz = matmul_w(y, w16["ssm_w_glu", l], toks["ssm_w_glu", l], nm + "_glu")
    z_out, z_gate = split_cols(zz, ((D_MODEL, D_MODEL), 0))
    return z_out * jax.nn.sigmoid(z_gate)


def _mla_mixer(q_lat, kv_lat, k_rope, cos, sin, sp, l, w16, toks):
    s = q_lat.shape[0]
    nm = "mla%d" % l
    dqk = MLA_NOPE + MLA_ROPE
    qt = matmul_w_t(rmsnorm(q_lat, sp["mla_q_norm", l], nm + "_qn"), w16["mla_w_uq", l], toks["mla_w_uq", l],
                    nm + "_uq").reshape(MLA_HEADS, dqk, s)
    half = MLA_ROPE // 2
    q1, q2 = qt[:, MLA_NOPE:MLA_NOPE + half], qt[:, MLA_NOPE + half:]
    ct, st = cos.T[None], sin.T[None]
    qt = jnp.concatenate([qt[:, :MLA_NOPE], q1 * ct - q2 * st, q2 * ct + q1 * st], axis=1).reshape(MLA_HEADS * dqk, s)
    kvt = matmul_w_t(rmsnorm(kv_lat, sp["mla_kv_norm", l], nm + "_kvn"), w16["mla_w_ukv", l], toks["mla_w_ukv", l],
                     nm + "_ukv").reshape(MLA_HEADS, MLA_NOPE + MLA_V, s)
    k_pe = _rope(k_rope, cos, sin)
    k = jnp.concatenate([kvt[:, :MLA_NOPE].transpose(0, 2, 1),
                         jnp.broadcast_to(k_pe[None], (MLA_HEADS, s, MLA_ROPE))], axis=-1)
    vt = kvt[:, MLA_NOPE:].reshape(MLA_HEADS * MLA_V, s)
    ot = attention(qt, k, vt, 1.0 / math.sqrt(dqk), True, nm + "_att")
    return matmul_w_at(ot, w16["mla_w_o", l], toks["mla_w_o", l], nm + "_o")


def _hgrn_mixer(q, f_logit, i_in, g, lb, sp, l, w16, toks):
    s = q.shape[0]
    nm = "hg%d" % l
    f = lb + (1.0 - lb) * jax.nn.sigmoid(f_logit)
    o = hgrn_core(jax.nn.silu(q), 1.0 - f, i_in, jnp.log(f), nm + "_core")
    o = rmsnorm(o.reshape(s * HG_HEADS, HG_DV), sp["hg_g_norm", l], nm + "_gn").reshape(s, HG_HEADS * HG_DV)
    return matmul_w(o * jax.nn.silu(g), w16["hg_w_o", l], toks["hg_w_o", l], nm + "_o")


def _cross_attention(h, mem_n, l, w16, toks):
    s, m = h.shape[0], mem_n.shape[0]
    nm = "xat%d" % l
    xw = X_HEADS * X_HEAD_DIM
    qt = matmul_w_t(h, w16["x_w_q", l], toks["x_w_q", l], nm + "_q")
    kv = matmul_w(mem_n, w16["x_w_kv", l], toks["x_w_kv", l], nm + "_kv")
    k, v = split_cols(kv, ((xw, xw), 0))
    k = k.reshape(m, X_HEADS, X_HEAD_DIM).transpose(1, 0, 2)
    ot = attention(qt, k, v.T, 1.0 / math.sqrt(X_HEAD_DIM), False, nm + "_att")
    return matmul_w_at(ot, w16["x_w_o", l], toks["x_w_o", l], nm + "_o")


_IN_HG = SSM_WIDTH + MLA_Q_RANK + MLA_KV_RANK
_IN_GATE = _IN_HG + 4 * HG_HEADS * HG_DK
_IN_ROPE = _IN_GATE + 3 * D_MODEL


def _perm_in(w):
    pad = jnp.zeros(w.shape[:-1] + (D_IN_PAD - D_IN,), w.dtype)
    return jnp.concatenate([w[..., :_IN_HG], w[..., _IN_HG + MLA_ROPE:], w[..., _IN_HG:_IN_HG + MLA_ROPE], pad], axis=-1)


def _unperm_in(g):
    return jnp.concatenate([g[..., :_IN_HG], g[..., _IN_ROPE:_IN_ROPE + MLA_ROPE], g[..., _IN_HG:_IN_ROPE]], axis=-1)


def _layer(x, mem, cos, sin, lb, l, sp, w16, toks):
    s = x.shape[0]
    nm = "l%d" % l
    h = rmsnorm(x, sp["norm_mix", l], nm + "_nmix")
    z = matmul_w(h, w16["w_in", l], toks["w_in", l], nm + "_in")
    hw = HG_HEADS * HG_DK
    sizes = (SSM_WIDTH, MLA_Q_RANK, MLA_KV_RANK, hw, hw, hw, hw, 3 * D_MODEL, MLA_ROPE)
    u, q_lat, kv_lat, hg_q, hg_f, hg_i, hg_g, gate_logits, k_rope = split_cols(z, (sizes, D_IN_PAD - D_IN))
    g_ssm, g_mla, g_hg = split_cols(jax.nn.sigmoid(gate_logits), ((D_MODEL,) * 3, 0))
    y_ssm = _s5_mixer(u, sp, l, w16, toks)
    y_mla = _mla_mixer(q_lat, kv_lat, k_rope, cos, sin, sp, l, w16, toks)
    y_hg = _hgrn_mixer(hg_q, hg_f, hg_i, hg_g, lb, sp, l, w16, toks)
    merged = g_ssm * y_ssm + g_mla * y_mla + g_hg * y_hg
    x = x + matmul_w(merged, w16["w_out", l], toks["w_out", l], nm + "_out")
    hc = rmsnorm(x, sp["norm_cross", l], nm + "_ncross")
    mem_n = rmsnorm(mem, sp["norm_mem", l], nm + "_nmem")
    x = x + _cross_attention(hc, mem_n, l, w16, toks)
    hf = rmsnorm(x, sp["norm_ffn", l], nm + "_nffn")
    gu = matmul_w(hf, w16["ffn_w_gate_up", l], toks["ffn_w_gate_up", l], nm + "_gu")
    gt, up = split_cols(gu, ((D_FF, D_FF), 0))
    return x + matmul_w(jax.nn.silu(gt) * up, w16["ffn_w_down", l], toks["ffn_w_down", l], nm + "_down")


def _loss_fn(sp, toks, x, w16, mem, pos, target):
    half = MLA_ROPE // 2
    inv_freq = ROPE_THETA ** (-jnp.arange(half, dtype=F32) / half)
    ang = pos.astype(F32)[:, None] * inv_freq
    cos, sin = jnp.cos(ang), jnp.sin(ang)
    lb_p = jax.nn.softmax(sp["hg_lb", ALL], axis=0)
    lower = jnp.cumsum(lb_p, axis=0) - lb_p[0:1]
    for l in range(DEPTH):
        x = _layer(x, mem, cos, sin, lower[l], l, sp, w16, toks)
    y = rmsnorm(x, sp["norm_final", ALL], "nfinal")
    return 0.5 * jnp.sum(jnp.mean(jnp.square(y - target), axis=-1))


WEIGHTS = ["norm_mix", "w_in", "ssm_lam_re", "ssm_lam_im", "ssm_b_re", "ssm_b_im", "ssm_c_re", "ssm_c_im", "ssm_d",
           "ssm_log_step", "ssm_w_glu", "mla_q_norm", "mla_kv_norm", "mla_w_uq", "mla_w_ukv", "mla_w_o", "hg_lb",
           "hg_g_norm", "hg_w_o", "w_out", "norm_cross", "norm_mem", "x_w_q", "x_w_kv", "x_w_o", "norm_ffn",
           "ffn_w_gate_up", "ffn_w_down", "norm_final"]
INPUTS = ["x", "mem", "positions"] + WEIGHTS + ["loss_target"] + ["m_" + n for n in WEIGHTS] + ["v_" + n for n in WEIGHTS]
BIG = [("w_in", 2), ("ssm_w_glu", 2), ("mla_w_uq", 2), ("mla_w_ukv", 2), ("mla_w_o", 2), ("hg_w_o", 2), ("w_out", 1),
       ("x_w_q", 1), ("x_w_kv", 1), ("x_w_o", 2), ("ffn_w_gate_up", 2), ("ffn_w_down", 1)]
SMALL = [n for n in WEIGHTS if n not in dict(BIG)]
WHOLE = ("hg_lb", "norm_final")
ALL = -1
SMALL_W = 128
N_CHIPS = 4


def _pack(arrs, rows, width, dtype):
    flat = jnp.concatenate([a.astype(dtype).reshape(-1) for a in arrs])
    return jnp.pad(flat, (0, rows * width - flat.shape[0])).reshape(rows, width)


def _unpack(buf, like):
    flat = buf.reshape(-1)
    out, off = [], 0
    for a in like:
        n = math.prod(a.shape)
        out.append(flat[off:off + n].reshape(a.shape))
        off += n
    return out


def kernel(
    x, mem, positions, norm_mix, w_in, ssm_lam_re, ssm_lam_im, ssm_b_re, ssm_b_im, ssm_c_re, ssm_c_im, ssm_d,
    ssm_log_step, ssm_w_glu, mla_q_norm, mla_kv_norm, mla_w_uq, mla_w_ukv, mla_w_o, hg_lb, hg_g_norm, hg_w_o,
    w_out, norm_cross, norm_mem, x_w_q, x_w_kv, x_w_o, norm_ffn, ffn_w_gate_up, ffn_w_down, norm_final,
    loss_target, m_norm_mix, m_w_in, m_ssm_lam_re, m_ssm_lam_im, m_ssm_b_re, m_ssm_b_im, m_ssm_c_re, m_ssm_c_im,
    m_ssm_d, m_ssm_log_step, m_ssm_w_glu, m_mla_q_norm, m_mla_kv_norm, m_mla_w_uq, m_mla_w_ukv, m_mla_w_o,
    m_hg_lb, m_hg_g_norm, m_hg_w_o, m_w_out, m_norm_cross, m_norm_mem, m_x_w_q, m_x_w_kv, m_x_w_o, m_norm_ffn,
    m_ffn_w_gate_up, m_ffn_w_down, m_norm_final, v_norm_mix, v_w_in, v_ssm_lam_re, v_ssm_lam_im, v_ssm_b_re,
    v_ssm_b_im, v_ssm_c_re, v_ssm_c_im, v_ssm_d, v_ssm_log_step, v_ssm_w_glu, v_mla_q_norm, v_mla_kv_norm,
    v_mla_w_uq, v_mla_w_ukv, v_mla_w_o, v_hg_lb, v_hg_g_norm, v_hg_w_o, v_w_out, v_norm_cross, v_norm_mem,
    v_x_w_q, v_x_w_kv, v_x_w_o, v_norm_ffn, v_ffn_w_gate_up, v_ffn_w_down, v_norm_final):
    given = locals()
    a = {n: given[n] for n in INPUTS}
    x, mem, pos, target = a["x"][0], a["mem"][0], a["positions"][0], a["loss_target"][0]
    place = jnp.stack([lax.axis_index("c"), 2 * lax.axis_index("x") + lax.axis_index("y")]).astype(jnp.int32)

    two = lambda t: t.reshape(t.shape[0] * t.shape[1], t.shape[2])
    gathered = _gather_weights([two(a[n]).astype(BF16) for n, _ in BIG])
    w16, toks = {}, {}
    for (n, ax), got in zip(BIG, gathered):
        per = a[n].shape[1]
        for l in range(DEPTH):
            blk = [got[q, l * per:(l + 1) * per] for q in range(N_CHIPS)]
            if n == "w_in":
                cut = _IN_HG + MLA_ROPE
                pad = jnp.zeros((per, D_IN_PAD - D_IN), BF16)
                full = jnp.concatenate([blk[0][:, :_IN_HG], blk[0][:, cut:], blk[1], blk[2], blk[3],
                                        blk[0][:, _IN_HG:cut], pad], axis=1)
            else:
                full = jnp.concatenate(blk, axis=ax - 1)
            w16[n, l] = full
            toks[n, l] = jnp.zeros(full.shape, F32)
    sp = {}
    for n in SMALL:
        if n in WHOLE:
            sp[n, ALL] = a[n]
        else:
            for l in range(DEPTH):
                sp[n, l] = a[n][l]

    loss, (g_sp, g_tok, g_x) = jax.value_and_grad(_loss_fn, argnums=(0, 1, 2))(sp, toks, x, w16, mem, pos, target)
    g_small = {n: g_sp[n, ALL] if n in WHOLE else jnp.stack([g_sp[n, l] for l in range(DEPTH)]) for n in SMALL}

    grads = []
    for n, ax in BIG:
        layers = []
        for l in range(DEPTH):
            g = _unperm_in(g_tok[n, l]) if n == "w_in" else g_tok[n, l]
            if ax == 2:
                layers.append(g.reshape(g.shape[0], N_CHIPS, g.shape[1] // N_CHIPS).transpose(1, 0, 2))
            else:
                layers.append(g.reshape(N_CHIPS, g.shape[0] // N_CHIPS, g.shape[1]))
        grads.append(jnp.concatenate(layers, axis=1))
    names = [n for n, _ in BIG]
    from_sibling = _swap_halves_to_sibling(grads)
    chip_sums = [_add_half(g, t, place, "grad_add_half_" + n) for n, g, t in zip(names, grads, from_sibling)]
    from_chips = _scatter_partials(chip_sums)
    totals = [_add_partials(g, t1, t2, place, "grad_add_partials_" + n)
              for n, g, t1, t2 in zip(names, grads, from_sibling, from_chips)]
    g_big = {n: g.reshape(a[n].shape) for n, g in zip(names, _join_halves(totals))}

    small_like = [a[n] for n in SMALL] + [jnp.zeros((1,), F32)]
    n_small = sum(math.prod(t.shape) for t in small_like)
    srows = -(-n_small // (SMALL_W * 8 * N_DEV)) * 8 * N_DEV
    reduced = _allreduce_small(_pack([g_small[n] for n in SMALL] + [loss.reshape(1)], srows, SMALL_W, F32))
    red = _unpack(reduced, small_like)
    g_sm = dict(zip(SMALL, red[:-1]))
    loss = red[-1].reshape(())

    out_g, out_d, out_m, out_v = {}, {}, {}, {}
    for n, _ in BIG:
        shp = a[n].shape
        two = lambda t: t.reshape(shp[0] * shp[1], shp[2])
        d, m, v = _adamw_call(two(a[n]), two(g_big[n]), two(a["m_" + n]), two(a["v_" + n]), "adamw_" + n)
        out_g[n], out_d[n], out_m[n], out_v[n] = g_big[n], d.reshape(shp), m.reshape(shp), v.reshape(shp)
    small_arrs = [a[n] for n in SMALL]
    d, m, v = _adamw_call(
        _pack(small_arrs, srows, SMALL_W, F32), reduced,
        _pack([a["m_" + n] for n in SMALL], srows, SMALL_W, F32),
        _pack([a["v_" + n] for n in SMALL], srows, SMALL_W, F32), "adamw_small")
    for n, gg, dd, mm_, vv in zip(SMALL, red[:-1], _unpack(d, small_arrs), _unpack(m, small_arrs), _unpack(v, small_arrs)):
        out_g[n], out_d[n], out_m[n], out_v[n] = gg, dd, mm_, vv

    return (loss, g_x[None], *[out_g[n] for n in WEIGHTS], *[out_d[n] for n in WEIGHTS],
            *[out_m[n] for n in WEIGHTS], *[out_v[n] for n in WEIGHTS])
```

```python
import functools
import math

import jax
import jax.numpy as jnp
from jax import lax
from jax.experimental import pallas as pl
from jax.experimental.pallas import tpu as pltpu

F32 = jnp.float32
BF16 = jnp.bfloat16
MESH = pl.DeviceIdType.MESH

D_MODEL = 1024
DEPTH = 2
RMS_EPS = 1e-6
SSM_GROUPS, SSM_GROUP_CH, SSM_STATE = 32, 16, 64
SSM_WIDTH = SSM_GROUPS * SSM_GROUP_CH
SSM_LANES = SSM_GROUPS * SSM_STATE
SSM_PACK = 8
MLA_HEADS, MLA_Q_RANK, MLA_KV_RANK, MLA_NOPE, MLA_ROPE, MLA_V = 8, 512, 256, 64, 32, 64
ROPE_THETA = 10000.0
HG_HEADS, HG_DK, HG_DV, HG_CHUNK = 4, 128, 128, 64
HG_SUB_FWD, HG_SUB_BWD = 32, 16
X_HEADS, X_HEAD_DIM = 4, 128
D_FF = 2816
D_IN = 6432
N_CHIPS = 4
D_IN_SHARD = D_IN // N_CHIPS
D_IN_BLOCK = 1664
MASK_VALUE = -1e30

ADAM_LR, ADAM_B1, ADAM_B2, ADAM_EPS, ADAM_WD, ADAM_STEP = 0.001, 0.9, 0.999, 1e-08, 0.01, 10

VMEM_LIMIT = 48 * 1024 * 1024


def _pcall(body, **kw):
    return pl.pallas_call(body, **kw)


def _params(sem):
    return pltpu.CompilerParams(dimension_semantics=sem, vmem_limit_bytes=VMEM_LIMIT)


def _tile(n, cap, align=128):
    if n <= cap:
        return n
    t = (cap // align) * align
    while t >= align:
        if n % t == 0:
            return t
        t -= align
    return n


MM_VMEM_BUDGET = 36 * 1024 * 1024
MM_TILE_CAP = 2048


def _divisors(n, cap, align=128):
    out = [n] if n <= cap else []
    out += [t for t in range(align, min(n, cap + 1), align) if n % t == 0]
    return sorted(set(out), reverse=True)


@functools.lru_cache(maxsize=None)
def _mm_tiles(m, n, k, a_bytes, b_bytes, o_bytes, n_unit, k_unit):
    best = None
    for tm in _divisors(m, MM_TILE_CAP):
        for tn in _divisors(n_unit, MM_TILE_CAP):
            for tk in _divisors(k_unit, MM_TILE_CAP):
                nk = k // tk
                need = 2 * (tm * tk * a_bytes + tk * tn * b_bytes + tm * tn * o_bytes)
                need += tm * tn * 4 * (2 if nk > 1 else 1)
                need += (tm * tk * 2 if a_bytes > 2 else 0) + (tk * tn * 2 if b_bytes > 2 else 0)
                if need > MM_VMEM_BUDGET:
                    continue
                key = ((m // tm) * (n // tn) * nk, nk, -tn)
                if best is None or key < best[0]:
                    best = (key, (tm, tn, tk))
    assert best is not None, (m, n, k)
    return best[1]


def _mm(a, b, *, ta=False, tb=False, out_dtype=F32, name="mm", b_blocks=False, out_blocks=0):
    if ta:
        K, M = a.shape
    else:
        M, K = a.shape
    n_unit = k_unit = None
    if b_blocks:
        nb, rows, cols = b.shape
        if tb:
            N, Kb, k_unit = rows, nb * cols, cols
        else:
            Kb, N, n_unit = rows, nb * cols, cols
    elif tb:
        N, Kb = b.shape
    else:
        Kb, N = b.shape
    if out_blocks:
        assert not b_blocks and N % out_blocks == 0
        n_unit = N // out_blocks
    assert K == Kb, (a.shape, b.shape, ta, tb)
    tm, tn, tk = _mm_tiles(M, N, K, a.dtype.itemsize, b.dtype.itemsize, jnp.dtype(out_dtype).itemsize,
                           n_unit or N, k_unit or K)
    nk = K // tk
    dims = (((0,) if ta else (1,), (1,) if tb else (0,)), ((), ()))

    def body(a_ref, b_ref, o_ref, *acc):
        av = a_ref[...].astype(BF16)
        bv = b_ref[...].astype(BF16)
        part = lax.dot_general(av, bv, dims, preferred_element_type=F32)
        if nk == 1:
            o_ref[...] = part.astype(o_ref.dtype)
        else:
            acc_ref = acc[0]
            k = pl.program_id(2)

            @pl.when(k == 0)
            def _():
                acc_ref[...] = part

            @pl.when(k > 0)
            def _():
                acc_ref[...] += part

            @pl.when(k == nk - 1)
            def _():
                o_ref[...] = acc_ref[...].astype(o_ref.dtype)

    a_spec = pl.BlockSpec((tk, tm), lambda i, j, k: (k, i)) if ta else pl.BlockSpec((tm, tk), lambda i, j, k: (i, k))
    if b_blocks and tb:
        r = k_unit // tk
        b_spec = pl.BlockSpec((None, tn, tk), lambda i, j, k: (k // r, j, k % r))
    elif b_blocks:
        r = n_unit // tn
        b_spec = pl.BlockSpec((None, tk, tn), lambda i, j, k: (j // r, k, j % r))
    elif tb:
        b_spec = pl.BlockSpec((tn, tk), lambda i, j, k: (j, k))
    else:
        b_spec = pl.BlockSpec((tk, tn), lambda i, j, k: (k, j))
    if out_blocks:
        r = n_unit // tn
        out_spec = pl.BlockSpec((None, tm, tn), lambda i, j, k: (j // r, i, j % r))
        out_shape = jax.ShapeDtypeStruct((out_blocks, M, n_unit), out_dtype)
    else:
        out_spec = pl.BlockSpec((tm, tn), lambda i, j, k: (i, j))
        out_shape = jax.ShapeDtypeStruct((M, N), out_dtype)
    return _pcall(
        body,
        name=name,
        grid=(M // tm, N // tn, nk),
        in_specs=[a_spec, b_spec],
        out_specs=out_spec,
        out_shape=out_shape,
        scratch_shapes=[pltpu.VMEM((tm, tn), F32)] if nk > 1 else [],
        compiler_params=_params(("parallel", "parallel", "arbitrary")),
    )(a, b)


def _bd_call(a, bc, transpose, name):
    m = a.shape[0]
    nb, ka, kb = bc.shape
    win, wout = (kb, ka) if transpose else (ka, kb)
    assert a.shape[1] == nb * win
    tm = _tile(m, 1024)
    dims = _NT if transpose else (((1,), (0,)), ((), ()))

    def body(a_ref, b_ref, o_ref):
        o_ref[...] = lax.dot_general(a_ref[...].astype(BF16), b_ref[0].astype(BF16), dims, preferred_element_type=F32)

    return _pcall(
        body,
        name=name,
        grid=(m // tm, nb),
        in_specs=[pl.BlockSpec((tm, win), lambda i, j: (i, j)), pl.BlockSpec((1, ka, kb), lambda i, j: (j, 0, 0))],
        out_specs=pl.BlockSpec((tm, wout), lambda i, j: (i, j)),
        out_shape=jax.ShapeDtypeStruct((m, nb * wout), F32),
        compiler_params=_params(("parallel", "parallel")),
    )(a, bc)


def _bd_dw_call(a, g, nb, name):
    m = a.shape[0]
    ka, kb = a.shape[1] // nb, g.shape[1] // nb
    tk = _tile(m, 1024)
    nk = m // tk

    def body(a_ref, g_ref, o_ref):
        part = lax.dot_general(a_ref[...].astype(BF16), g_ref[...].astype(BF16), _TN, preferred_element_type=F32)
        k = pl.program_id(1)

        @pl.when(k == 0)
        def _():
            o_ref[0] = part

        @pl.when(k > 0)
        def _():
            o_ref[0] += part

    return _pcall(
        body,
        name=name,
        grid=(nb, nk),
        in_specs=[pl.BlockSpec((tk, ka), lambda j, k: (k, j)), pl.BlockSpec((tk, kb), lambda j, k: (k, j))],
        out_specs=pl.BlockSpec((1, ka, kb), lambda j, k: (j, 0, 0)),
        out_shape=jax.ShapeDtypeStruct((nb, ka, kb), F32),
        compiler_params=_params(("parallel", "arbitrary")),
    )(a, g)


@functools.partial(jax.custom_vjp, nondiff_argnums=(2,))
def bd_matmul(a, bc, name):
    return _bd_call(a, bc, False, name + "_f")


def _bd_matmul_fwd(a, bc, name):
    return _bd_call(a, bc, False, name + "_f"), (a, bc)


def _bd_matmul_bwd(name, res, g):
    a, bc = res
    return _bd_call(g, bc, True, name + "_da"), _bd_dw_call(a, g, bc.shape[0], name + "_dw")


bd_matmul.defvjp(_bd_matmul_fwd, _bd_matmul_bwd)


@functools.partial(jax.custom_vjp, nondiff_argnums=(3,))
def matmul_w(a, w16, tok, name):
    return _mm(a, w16, b_blocks=w16.ndim == 3, name=name + "_f")


def _matmul_w_fwd(a, w16, tok, name):
    return _mm(a, w16, b_blocks=w16.ndim == 3, name=name + "_f"), (a, w16)


def _matmul_w_bwd(name, res, g):
    a, w16 = res
    blocks = w16.shape[0] if w16.ndim == 3 else 0
    return (_mm(g, w16, tb=True, b_blocks=bool(blocks), name=name + "_da"), jnp.zeros_like(w16),
            _mm(a, g, ta=True, out_blocks=blocks, name=name + "_dw"))


matmul_w.defvjp(_matmul_w_fwd, _matmul_w_bwd)


@functools.partial(jax.custom_vjp, nondiff_argnums=(3,))
def matmul_w_t(a, w16, tok, name):
    return _mm(w16, a, ta=True, tb=True, name=name + "_f")


def _matmul_w_t_fwd(a, w16, tok, name):
    return _mm(w16, a, ta=True, tb=True, name=name + "_f"), (a, w16)


def _matmul_w_t_bwd(name, res, g):
    a, w16 = res
    return (_mm(g, w16, ta=True, tb=True, name=name + "_da"), jnp.zeros_like(w16),
            _mm(a, g, ta=True, tb=True, name=name + "_dw"))


matmul_w_t.defvjp(_matmul_w_t_fwd, _matmul_w_t_bwd)


@functools.partial(jax.custom_vjp, nondiff_argnums=(3,))
def matmul_w_at(at, w16, tok, name):
    return _mm(at, w16, ta=True, name=name + "_f")


def _matmul_w_at_fwd(at, w16, tok, name):
    return _mm(at, w16, ta=True, name=name + "_f"), (at, w16)


def _matmul_w_at_bwd(name, res, g):
    at, w16 = res
    return _mm(w16, g, tb=True, name=name + "_da"), jnp.zeros_like(w16), _mm(at, g, name=name + "_dw")


matmul_w_at.defvjp(_matmul_w_at_fwd, _matmul_w_at_bwd)


def _rms_fwd_call(x, g, name):
    rows, d = x.shape
    tr = _tile(rows, 512, 8)

    def body(x_ref, g_ref, y_ref):
        xv = x_ref[...]
        rstd = lax.rsqrt(jnp.mean(xv * xv, axis=-1, keepdims=True) + RMS_EPS)
        y_ref[...] = xv * rstd * g_ref[...]

    return _pcall(
        body,
        name=name,
        grid=(rows // tr,),
        in_specs=[pl.BlockSpec((tr, d), lambda i: (i, 0)), pl.BlockSpec((1, d), lambda i: (0, 0))],
        out_specs=pl.BlockSpec((tr, d), lambda i: (i, 0)),
        out_shape=jax.ShapeDtypeStruct((rows, d), F32),
        compiler_params=_params(("parallel",)),
    )(x, g.reshape(1, d))


def _rms_bwd_call(x, g, dy, name):
    rows, d = x.shape
    tr = _tile(rows, 512, 8)
    nb = rows // tr

    def body(x_ref, g_ref, dy_ref, dx_ref, dg_ref):
        xv = x_ref[...]
        rstd = lax.rsqrt(jnp.mean(xv * xv, axis=-1, keepdims=True) + RMS_EPS)
        xh = xv * rstd
        dyv = dy_ref[...]
        dyg = dyv * g_ref[...]
        dx_ref[...] = rstd * (dyg - xh * jnp.mean(dyg * xh, axis=-1, keepdims=True))
        dg_ref[...] = jnp.sum((dyv * xh).reshape(tr // 8, 8, d), axis=0)

    dx, dgp = _pcall(
        body,
        name=name,
        grid=(nb,),
        in_specs=[
            pl.BlockSpec((tr, d), lambda i: (i, 0)),
            pl.BlockSpec((1, d), lambda i: (0, 0)),
            pl.BlockSpec((tr, d), lambda i: (i, 0)),
        ],
        out_specs=[pl.BlockSpec((tr, d), lambda i: (i, 0)), pl.BlockSpec((8, d), lambda i: (i, 0))],
        out_shape=[jax.ShapeDtypeStruct((rows, d), F32), jax.ShapeDtypeStruct((nb * 8, d), F32)],
        compiler_params=_params(("parallel",)),
    )(x, g.reshape(1, d), dy)
    return dx, jnp.sum(dgp, axis=0)


@functools.partial(jax.custom_vjp, nondiff_argnums=(2,))
def rmsnorm(x, g, name):
    return _rms_fwd_call(x, g, name + "_f")


def _rmsnorm_fwd(x, g, name):
    return _rms_fwd_call(x, g, name + "_f"), (x, g)


def _rmsnorm_bwd(name, res, dy):
    x, g = res
    return _rms_bwd_call(x, g, dy, name + "_b")


rmsnorm.defvjp(_rmsnorm_fwd, _rmsnorm_bwd)


def _attn_blocks(sq, sk):
    return _tile(sq, 1024), _tile(sk, 512)


def _attn_pt(qs, k_ref, qi, ki, bq, bk, masked, shift):
    st = jnp.dot(k_ref[0].astype(BF16), qs, preferred_element_type=F32)
    if masked:
        kpos = ki * bk + lax.broadcasted_iota(jnp.int32, (bk, bq), 0)
        qpos = qi * bq + lax.broadcasted_iota(jnp.int32, (bk, bq), 1)
        st = jnp.where(kpos <= qpos, st, MASK_VALUE)
    return st if shift is None else jnp.exp(st - shift)


def _attn_pairs(nq, nk, bq, bk, causal, k_major):
    ok = lambda i, j: (not causal) or j * bk <= i * bq + bq - 1
    if k_major:
        pairs = [(i, j) for j in range(nk) for i in range(nq) if ok(i, j)]
    else:
        pairs = [(i, j) for i in range(nq) for j in range(nk) if ok(i, j)]
    qtab = jnp.asarray([p[0] for p in pairs], jnp.int32)
    ktab = jnp.asarray([p[1] for p in pairs], jnp.int32)
    first_q = (lambda j: (j * bk) // bq) if causal else (lambda j: 0)
    last_k = (lambda i: jnp.minimum((i * bq + bq - 1) // bk, nk - 1)) if causal else (lambda i: nk - 1)
    return qtab, ktab, first_q, last_k


def _attn_cases(causal, qi, ki, bq, bk, step):
    if not causal:
        step(False)
        return
    pl.when(ki * bk + bk - 1 <= qi * bq)(functools.partial(step, False))
    pl.when(ki * bk + bk - 1 > qi * bq)(functools.partial(step, True))


def _attn_fwd_call(qt, k, vt, scale, causal, name):
    h, sk, dq = k.shape
    sq = qt.shape[1]
    dv = vt.shape[0] // h
    bq, bk = _attn_blocks(sq, sk)
    nq, nk = sq // bq, sk // bk

    qtab, ktab, _, last_k = _attn_pairs(nq, nk, bq, bk, causal, False)

    def body(qi_ref, ki_ref, q_ref, k_ref, v_ref, o_ref, lse_ref, m_sc, l_sc, acc_sc):
        qi, ki = qi_ref[pl.program_id(1)], ki_ref[pl.program_id(1)]

        @pl.when(ki == 0)
        def _():
            m_sc[...] = jnp.full(m_sc.shape, -jnp.inf, F32)
            l_sc[...] = jnp.zeros(l_sc.shape, F32)
            acc_sc[...] = jnp.zeros(acc_sc.shape, F32)

        def step(masked):
            qs = (q_ref[...] * scale).astype(BF16)
            st = _attn_pt(qs, k_ref, qi, ki, bq, bk, masked, None)
            m_prev = m_sc[...]
            m_new = jnp.maximum(m_prev, jnp.max(st, axis=0, keepdims=True))
            alpha = jnp.exp(m_prev - m_new)
            pt = jnp.exp(st - m_new)
            l_sc[...] = alpha * l_sc[...] + jnp.sum(pt, axis=0, keepdims=True)
            acc_sc[...] = alpha * acc_sc[...] + jnp.dot(v_ref[...].astype(BF16), pt.astype(BF16),
                                                        preferred_element_type=F32)
            m_sc[...] = m_new

        _attn_cases(causal, qi, ki, bq, bk, step)

        @pl.when(ki == last_k(qi))
        def _():
            o_ref[...] = acc_sc[...] / l_sc[...]
            lse_ref[0] = m_sc[...] + jnp.log(l_sc[...])

    qmap = lambda hh, p, qi_ref, ki_ref: (hh, qi_ref[p])
    return _pcall(
        body,
        name=name,
        grid_spec=pltpu.PrefetchScalarGridSpec(
            num_scalar_prefetch=2,
            grid=(h, qtab.shape[0]),
            in_specs=[
                pl.BlockSpec((dq, bq), qmap),
                pl.BlockSpec((1, bk, dq), lambda hh, p, qi_ref, ki_ref: (hh, ki_ref[p], 0)),
                pl.BlockSpec((dv, bk), lambda hh, p, qi_ref, ki_ref: (hh, ki_ref[p])),
            ],
            out_specs=[pl.BlockSpec((dv, bq), qmap),
                       pl.BlockSpec((1, 1, bq), lambda hh, p, qi_ref, ki_ref: (hh, 0, qi_ref[p]))],
            scratch_shapes=[pltpu.VMEM((1, bq), F32), pltpu.VMEM((1, bq), F32), pltpu.VMEM((dv, bq), F32)],
        ),
        out_shape=[jax.ShapeDtypeStruct((h * dv, sq), F32), jax.ShapeDtypeStruct((h, 1, sq), F32)],
        compiler_params=_params(("parallel", "arbitrary")),
    )(qtab, ktab, qt, k, vt)


_TN = (((0,), (0,)), ((), ()))
_NT = (((1,), (1,)), ((), ()))


def _attn_dst(qs, k_ref, v_ref, do_ref, lse_ref, dl_ref, qi, ki, bq, bk, masked):
    pt = _attn_pt(qs, k_ref, qi, ki, bq, bk, masked, lse_ref[0])
    dpt = lax.dot_general(v_ref[...].astype(BF16), do_ref[...].astype(BF16), _TN, preferred_element_type=F32)
    return pt, pt * (dpt - dl_ref[0])


def _attn_dq_call(qt, k, vt, dot, lse, delta, scale, causal, name):
    h, sk, dq = k.shape
    sq = qt.shape[1]
    dv = vt.shape[0] // h
    bq, bk = _attn_blocks(sq, sk)
    nq, nk = sq // bq, sk // bk

    qtab, ktab, _, last_k = _attn_pairs(nq, nk, bq, bk, causal, False)

    def body(qi_ref, ki_ref, q_ref, k_ref, v_ref, do_ref, lse_ref, dl_ref, dq_ref, acc_sc):
        qi, ki = qi_ref[pl.program_id(1)], ki_ref[pl.program_id(1)]

        @pl.when(ki == 0)
        def _():
            acc_sc[...] = jnp.zeros(acc_sc.shape, F32)

        def step(masked):
            qs = (q_ref[...] * scale).astype(BF16)
            _, dst = _attn_dst(qs, k_ref, v_ref, do_ref, lse_ref, dl_ref, qi, ki, bq, bk, masked)
            acc_sc[...] += lax.dot_general(k_ref[0].astype(BF16), dst.astype(BF16), _TN, preferred_element_type=F32)

        _attn_cases(causal, qi, ki, bq, bk, step)

        @pl.when(ki == last_k(qi))
        def _():
            dq_ref[...] = acc_sc[...] * scale

    qmap = lambda hh, p, qi_ref, ki_ref: (hh, qi_ref[p])
    rowmap = lambda hh, p, qi_ref, ki_ref: (hh, 0, qi_ref[p])
    return _pcall(
        body,
        name=name,
        grid_spec=pltpu.PrefetchScalarGridSpec(
            num_scalar_prefetch=2,
            grid=(h, qtab.shape[0]),
            in_specs=[
                pl.BlockSpec((dq, bq), qmap),
                pl.BlockSpec((1, bk, dq), lambda hh, p, qi_ref, ki_ref: (hh, ki_ref[p], 0)),
                pl.BlockSpec((dv, bk), lambda hh, p, qi_ref, ki_ref: (hh, ki_ref[p])),
                pl.BlockSpec((dv, bq), qmap),
                pl.BlockSpec((1, 1, bq), rowmap),
                pl.BlockSpec((1, 1, bq), rowmap),
            ],
            out_specs=pl.BlockSpec((dq, bq), qmap),
            scratch_shapes=[pltpu.VMEM((dq, bq), F32)],
        ),
        out_shape=jax.ShapeDtypeStruct((h * dq, sq), F32),
        compiler_params=_params(("parallel", "arbitrary")),
    )(qtab, ktab, qt, k, vt, dot, lse, delta)


def _attn_dkv_call(qt, k, vt, dot, lse, delta, scale, causal, name):
    h, sk, dq = k.shape
    sq = qt.shape[1]
    dv = vt.shape[0] // h
    bq, bk = _attn_blocks(sq, sk)
    nq, nk = sq // bq, sk // bk

    qtab, ktab, first_q, _ = _attn_pairs(nq, nk, bq, bk, causal, True)

    def body(qi_ref, ki_ref, q_ref, k_ref, v_ref, do_ref, lse_ref, dl_ref, dk_ref, dv_ref, dk_sc, dv_sc):
        qi, ki = qi_ref[pl.program_id(1)], ki_ref[pl.program_id(1)]

        @pl.when(qi == first_q(ki))
        def _():
            dk_sc[...] = jnp.zeros(dk_sc.shape, F32)
            dv_sc[...] = jnp.zeros(dv_sc.shape, F32)

        def step(masked):
            qs = (q_ref[...] * scale).astype(BF16)
            pt, dst = _attn_dst(qs, k_ref, v_ref, do_ref, lse_ref, dl_ref, qi, ki, bq, bk, masked)
            dv_sc[...] += lax.dot_general(do_ref[...].astype(BF16), pt.astype(BF16), _NT, preferred_element_type=F32)
            dk_sc[...] += lax.dot_general(dst.astype(BF16), qs, _NT, preferred_element_type=F32)

        _attn_cases(causal, qi, ki, bq, bk, step)

        @pl.when(qi == nq - 1)
        def _():
            dk_ref[0] = dk_sc[...]
            dv_ref[...] = dv_sc[...]

    qmap = lambda hh, p, qi_ref, ki_ref: (hh, qi_ref[p])
    rowmap = lambda hh, p, qi_ref, ki_ref: (hh, 0, qi_ref[p])
    kmap = lambda hh, p, qi_ref, ki_ref: (hh, ki_ref[p], 0)
    vmap = lambda hh, p, qi_ref, ki_ref: (hh, ki_ref[p])
    return _pcall(
        body,
        name=name,
        grid_spec=pltpu.PrefetchScalarGridSpec(
            num_scalar_prefetch=2,
            grid=(h, qtab.shape[0]),
            in_specs=[
                pl.BlockSpec((dq, bq), qmap),
                pl.BlockSpec((1, bk, dq), kmap),
                pl.BlockSpec((dv, bk), vmap),
                pl.BlockSpec((dv, bq), qmap),
                pl.BlockSpec((1, 1, bq), rowmap),
                pl.BlockSpec((1, 1, bq), rowmap),
            ],
            out_specs=[pl.BlockSpec((1, bk, dq), kmap), pl.BlockSpec((dv, bk), vmap)],
            scratch_shapes=[pltpu.VMEM((bk, dq), F32), pltpu.VMEM((dv, bk), F32)],
        ),
        out_shape=[jax.ShapeDtypeStruct((h, sk, dq), F32), jax.ShapeDtypeStruct((h * dv, sk), F32)],
        compiler_params=_params(("parallel", "arbitrary")),
    )(qtab, ktab, qt, k, vt, dot, lse, delta)


@functools.partial(jax.custom_vjp, nondiff_argnums=(3, 4, 5))
def attention(qt, k, vt, scale, causal, name):
    return _attn_fwd_call(qt, k, vt, scale, causal, name + "_f")[0]


def _attention_fwd(qt, k, vt, scale, causal, name):
    ot, lse = _attn_fwd_call(qt, k, vt, scale, causal, name + "_f")
    return ot, (qt, k, vt, ot, lse)


def _attention_bwd(scale, causal, name, res, dot):
    qt, k, vt, ot, lse = res
    h = k.shape[0]
    delta = jnp.sum((dot * ot).reshape(h, ot.shape[0] // h, ot.shape[1]), axis=1, keepdims=True)
    dqt = _attn_dq_call(qt, k, vt, dot, lse, delta, scale, causal, name + "_dq")
    dk, dvt = _attn_dkv_call(qt, k, vt, dot, lse, delta, scale, causal, name + "_dkv")
    return dqt, dk, dvt


attention.defvjp(_attention_fwd, _attention_bwd)


SCAN_ROWS = 8
SCAN_LANES = 256
SCAN_TBLOCK = 512
SCAN_UNROLL = 4


def _scan_tables(lr, li):
    pr, pi = [lr], [li]
    for _ in range(SCAN_ROWS - 1):
        pr, pi = pr + [pr[-1] * lr - pi[-1] * li], pi + [pr[-1] * li + pi[-1] * lr]
    return jnp.concatenate(pr, axis=0), jnp.concatenate(pi, axis=0)


def _scan_call(ur, ui, lr, li, reverse, name, states=None):
    s, n = ur.shape
    tb = _tile(s, SCAN_TBLOCK, 8)
    nt, nl = s // tb, n // SCAN_LANES
    ntile = tb // SCAN_ROWS
    pr, pi = _scan_tables(lr, li)
    if reverse:
        pr, pi = pr[::-1], pi[::-1]
    shape = (SCAN_ROWS, SCAN_LANES)
    with_states = states is not None
    assert reverse or not with_states

    def body(*refs):
        if with_states:
            ur_ref, ui_ref, pr_ref, pi_ref, sr_ref, si_ref, xr_ref, xi_ref, dr_ref, di_ref, cr_sc, ci_sc, ar_sc, ai_sc = refs
        else:
            ur_ref, ui_ref, pr_ref, pi_ref, xr_ref, xi_ref, cr_sc, ci_sc, ar_sc, ai_sc = refs

        @pl.when(pl.program_id(1) == 0)
        def _():
            for sc in (cr_sc, ci_sc, ar_sc, ai_sc):
                sc[...] = jnp.zeros(shape, F32)

        prv, piv = pr_ref[...], pi_ref[...]
        rows = lax.broadcasted_iota(jnp.int32, shape, 0)

        def powers(k):
            r = (SCAN_ROWS - k) if reverse else (k - 1)
            return jnp.broadcast_to(prv[r:r + 1], shape), jnp.broadcast_to(piv[r:r + 1], shape)

        pw = [powers(k) for k in (1, 2, 4)]

        def tile(i, carry):
            cr, ci, acr, aci = carry
            j = (ntile - 1 - i) if reverse else i
            sl = pl.ds(pl.multiple_of(j * SCAN_ROWS, SCAN_ROWS), SCAN_ROWS)
            xr, xi = ur_ref[sl, :], ui_ref[sl, :]
            for (ar, ai), k in zip(pw, (1, 2, 4)):
                if reverse:
                    keep = rows < SCAN_ROWS - k
                    sr = jnp.where(keep, pltpu.roll(xr, SCAN_ROWS - k, 0), 0.0)
                    si = jnp.where(keep, pltpu.roll(xi, SCAN_ROWS - k, 0), 0.0)
                else:
                    keep = rows >= k
                    sr = jnp.where(keep, pltpu.roll(xr, k, 0), 0.0)
                    si = jnp.where(keep, pltpu.roll(xi, k, 0), 0.0)
                xr, xi = xr + (ar * sr - ai * si), xi + (ar * si + ai * sr)
            xr, xi = xr + (prv * cr - piv * ci), xi + (prv * ci + piv * cr)
            xr_ref[sl, :] = xr
            xi_ref[sl, :] = xi
            if with_states:
                last = rows == SCAN_ROWS - 1
                nr = jnp.where(last, cr, pltpu.roll(xr, SCAN_ROWS - 1, 0))
                ni = jnp.where(last, ci, pltpu.roll(xi, SCAN_ROWS - 1, 0))
                sr, si = sr_ref[sl, :], si_ref[sl, :]
                acr, aci = acr + (sr * nr + si * ni), aci + (sr * ni - si * nr)
            e = 0 if reverse else SCAN_ROWS - 1
            return jnp.broadcast_to(xr[e:e + 1], shape), jnp.broadcast_to(xi[e:e + 1], shape), acr, aci

        def tiles(g, carry):
            for u in range(SCAN_UNROLL):
                carry = tile(g * SCAN_UNROLL + u, carry)
            return carry

        assert ntile % SCAN_UNROLL == 0
        cr, ci, acr, aci = lax.fori_loop(0, ntile // SCAN_UNROLL, tiles,
                                         (cr_sc[...], ci_sc[...], ar_sc[...], ai_sc[...]))
        cr_sc[...] = cr
        ci_sc[...] = ci
        if with_states:
            ar_sc[...] = acr
            ai_sc[...] = aci

            @pl.when(pl.program_id(1) == nt - 1)
            def _():
                dr_ref[...] = acr
                di_ref[...] = aci

    tmap = (lambda l, t: (nt - 1 - t, l)) if reverse else (lambda l, t: (t, l))
    blk = pl.BlockSpec((tb, SCAN_LANES), tmap)
    tab = pl.BlockSpec((SCAN_ROWS, SCAN_LANES), lambda l, t: (0, l))
    full = jax.ShapeDtypeStruct((s, n), F32)
    return _pcall(
        body,
        name=name,
        grid=(nl, nt),
        in_specs=[blk, blk, tab, tab] + ([blk, blk] if with_states else []),
        out_specs=[blk, blk] + ([tab, tab] if with_states else []),
        out_shape=[full, full] + ([jax.ShapeDtypeStruct((SCAN_ROWS, n), F32)] * 2 if with_states else []),
        scratch_shapes=[pltpu.VMEM(shape, F32)] * 4,
        compiler_params=_params(("parallel", "arbitrary")),
    )(ur, ui, pr, pi, *(states if with_states else ()))


@functools.partial(jax.custom_vjp, nondiff_argnums=(4,))
def s5_scan(ur, ui, lr, li, name):
    return tuple(_scan_call(ur, ui, lr, li, False, name + "_f"))


def _s5_scan_fwd(ur, ui, lr, li, name):
    xr, xi = _scan_call(ur, ui, lr, li, False, name + "_f")
    return (xr, xi), (xr, xi, lr, li)


def _s5_scan_bwd(name, res, g):
    xr, xi, lr, li = res
    gr, gi, dlr, dli = _scan_call(g[0], g[1], lr, -li, True, name + "_b", states=(xr, xi))
    return gr, gi, jnp.sum(dlr, axis=0, keepdims=True), jnp.sum(dli, axis=0, keepdims=True)


s5_scan.defvjp(_s5_scan_fwd, _s5_scan_bwd)


def _hg_chunk(HG_SUB, q, k, v, lf, st):
    c = HG_CHUNK
    row = lax.broadcasted_iota(jnp.int32, (c, c), 0)
    col = lax.broadcasted_iota(jnp.int32, (c, c), 1)
    tri = (col <= row).astype(F32)
    b = jnp.dot(tri, lf, precision=lax.Precision.HIGHEST, preferred_element_type=F32)
    qb = q * jnp.exp(b)
    o = lax.dot_general(qb.astype(BF16), st.astype(BF16), (((1,), (1,)), ((), ())), preferred_element_type=F32)
    nsub = c // HG_SUB
    srow = lax.broadcasted_iota(jnp.int32, (HG_SUB, HG_SUB, 1), 0)
    scol = lax.broadcasted_iota(jnp.int32, (HG_SUB, HG_SUB, 1), 1)
    smask = scol <= srow
    outs = []
    for i in range(nsub):
        lo = i * HG_SUB
        bi, qi, ki = b[lo:lo + HG_SUB], q[lo:lo + HG_SUB], k[lo:lo + HG_SUB]
        diff = jnp.where(smask, bi[:, None, :] - bi[None, :, :], 0.0)
        e = jnp.where(smask, jnp.exp(diff), 0.0)
        a_ii = jnp.sum(qi[:, None, :] * ki[None, :, :] * e, axis=-1)
        oi = jnp.dot(a_ii.astype(BF16), v[lo:lo + HG_SUB].astype(BF16), preferred_element_type=F32)
        if i > 0:
            r = b[lo - 1:lo]
            qt = qi * jnp.exp(bi - r)
            kt = k[:lo] * jnp.exp(r - b[:lo])
            a_ij = lax.dot_general(qt.astype(BF16), kt.astype(BF16), (((1,), (1,)), ((), ())),
                                   preferred_element_type=F32)
            oi = oi + jnp.dot(a_ij.astype(BF16), v[:lo].astype(BF16), preferred_element_type=F32)
        outs.append(oi)
    o = o + jnp.concatenate(outs, axis=0)
    bl = b[c - 1:c]
    kd = k * jnp.exp(bl - b)
    st_new = st * jnp.exp(bl) + lax.dot_general(v.astype(BF16), kd.astype(BF16), (((0,), (0,)), ((), ())),
                                                preferred_element_type=F32)
    return o, st_new


def _hg_fwd_call(q, k, v, lf, name):
    s, w = q.shape
    nc = s // HG_CHUNK
    blk = pl.BlockSpec((HG_CHUNK, w), lambda i: (i, 0))

    def body(q_ref, k_ref, v_ref, lf_ref, o_ref, st_ref, st_sc):
        @pl.when(pl.program_id(0) == 0)
        def _():
            st_sc[...] = jnp.zeros(st_sc.shape, F32)

        for hh in range(HG_HEADS):
            cs = slice(hh * HG_DK, (hh + 1) * HG_DK)
            st = st_sc[hh]
            st_ref[0, hh] = st
            o, st_new = _hg_chunk(HG_SUB_FWD, q_ref[:, cs], k_ref[:, cs], v_ref[:, cs], lf_ref[:, cs], st)
            o_ref[:, cs] = o
            st_sc[hh] = st_new

    return _pcall(
        body,
        name=name,
        grid=(nc,),
        in_specs=[blk, blk, blk, blk],
        out_specs=[blk, pl.BlockSpec((1, HG_HEADS, HG_DV, HG_DK), lambda i: (i, 0, 0, 0))],
        out_shape=[jax.ShapeDtypeStruct((s, w), F32), jax.ShapeDtypeStruct((nc, HG_HEADS, HG_DV, HG_DK), F32)],
        scratch_shapes=[pltpu.VMEM((HG_HEADS, HG_DV, HG_DK), F32)],
        compiler_params=_params(("arbitrary",)),
    )(q, k, v, lf)


def _hg_bwd_call(q, k, v, lf, sts, do, name):
    s, w = q.shape
    nc = s // HG_CHUNK
    blk = pl.BlockSpec((HG_CHUNK, w), lambda i: (nc - 1 - i, 0))

    def body(q_ref, k_ref, v_ref, lf_ref, st_ref, do_ref, dq_ref, dk_ref, dv_ref, dlf_ref, dst_sc):
        @pl.when(pl.program_id(0) == 0)
        def _():
            dst_sc[...] = jnp.zeros(dst_sc.shape, F32)

        for hh in range(HG_HEADS):
            cs = slice(hh * HG_DK, (hh + 1) * HG_DK)
            _, vjp = jax.vjp(functools.partial(_hg_chunk, HG_SUB_BWD), q_ref[:, cs], k_ref[:, cs], v_ref[:, cs], lf_ref[:, cs], st_ref[0, hh])
            dq, dk, dv, dlf, dst = vjp((do_ref[:, cs], dst_sc[hh]))
            dq_ref[:, cs] = dq
            dk_ref[:, cs] = dk
            dv_ref[:, cs] = dv
            dlf_ref[:, cs] = dlf
            dst_sc[hh] = dst

    out = jax.ShapeDtypeStruct((s, w), F32)
    return _pcall(
        body,
        name=name,
        grid=(nc,),
        in_specs=[blk, blk, blk, blk, pl.BlockSpec((1, HG_HEADS, HG_DV, HG_DK), lambda i: (nc - 1 - i, 0, 0, 0)), blk],
        out_specs=[blk, blk, blk, blk],
        out_shape=[out, out, out, out],
        scratch_shapes=[pltpu.VMEM((HG_HEADS, HG_DV, HG_DK), F32)],
        compiler_params=_params(("arbitrary",)),
    )(q, k, v, lf, sts, do)


@functools.partial(jax.custom_vjp, nondiff_argnums=(4,))
def hgrn_core(q, k, v, lf, name):
    return _hg_fwd_call(q, k, v, lf, name + "_f")[0]


def _hgrn_core_fwd(q, k, v, lf, name):
    o, sts = _hg_fwd_call(q, k, v, lf, name + "_f")
    return o, (q, k, v, lf, sts)


def _hgrn_core_bwd(name, res, do):
    q, k, v, lf, sts = res
    return tuple(_hg_bwd_call(q, k, v, lf, sts, do, name + "_b"))


hgrn_core.defvjp(_hgrn_core_fwd, _hgrn_core_bwd)


def _adamw_call(w, g, m, v, name):
    rows, cols = w.shape
    tr = _tile(rows, 256, 8)
    c1 = 1.0 / (1.0 - ADAM_B1 ** ADAM_STEP)
    c2 = 1.0 / (1.0 - ADAM_B2 ** ADAM_STEP)

    def body(w_ref, g_ref, m_ref, v_ref, d_ref, mo_ref, vo_ref):
        gv = g_ref[...]
        mn = ADAM_B1 * m_ref[...] + (1.0 - ADAM_B1) * gv
        vn = ADAM_B2 * v_ref[...] + (1.0 - ADAM_B2) * (gv * gv)
        d_ref[...] = -ADAM_LR * ((mn * c1) / (jnp.sqrt(vn * c2) + ADAM_EPS) + ADAM_WD * w_ref[...])
        mo_ref[...] = mn
        vo_ref[...] = vn

    blk = pl.BlockSpec((tr, cols), lambda i: (i, 0))
    out = jax.ShapeDtypeStruct((rows, cols), F32)
    return _pcall(
        body,
        name=name,
        grid=(rows // tr,),
        in_specs=[blk, blk, blk, blk],
        out_specs=[blk, blk, blk],
        out_shape=[out, out, out],
        compiler_params=_params(("parallel",)),
    )(w, g, m, v)


ANY = pl.BlockSpec(memory_space=pl.ANY)


def _place():
    return lax.axis_index("x"), lax.axis_index("y"), lax.axis_index("c")


def _other_chips(x, y):
    return [(1 - x, y), (x, 1 - y), (1 - x, 1 - y)]


def _gather_weights(shards):
    nw = len(shards)

    def body(*refs):
        w_refs, g_refs, (send_sems, recv_sems) = refs[:nw], refs[nw:2 * nw], refs[2 * nw:]
        x, y, c = _place()
        myq = 2 * x + y
        sibling = (x, y, 1 - c)
        chips = _other_chips(x, y)

        def copy(k, src, dst, to):
            return pltpu.make_async_remote_copy(src_ref=src, dst_ref=dst, send_sem=send_sems.at[k],
                                                recv_sem=recv_sems.at[k], device_id=to, device_id_type=MESH)

        first = []
        for i in range(nw):
            for j, (px, py) in enumerate(chips):
                first.append(copy(6 * i + j, w_refs[i].at[c], g_refs[i].at[c, myq], (px, py, c)))
                first[-1].start()
        passed = []
        for i in range(nw):
            for j, (px, py) in enumerate(chips):
                q = 2 * px + py
                copy(6 * i + j, w_refs[i].at[c], g_refs[i].at[c, q], (px, py, c)).wait_recv()
                passed.append(copy(6 * i + 3 + j, g_refs[i].at[c, q], g_refs[i].at[c, q], sibling))
                passed[-1].start()
        for i in range(nw):
            for j, (px, py) in enumerate(chips):
                q = 2 * px + py
                copy(6 * i + 3 + j, g_refs[i].at[1 - c, q], g_refs[i].at[1 - c, q], sibling).wait_recv()
        for cp in first + passed:
            cp.wait_send()

    got = _pcall(
        body,
        name="gather_weights",
        in_specs=[ANY] * nw,
        out_specs=[ANY] * nw,
        out_shape=[jax.ShapeDtypeStruct((s.shape[0], N_CHIPS) + s.shape[1:], s.dtype) for s in shards],
        scratch_shapes=[pltpu.SemaphoreType.DMA((6 * nw,)), pltpu.SemaphoreType.DMA((6 * nw,))],
    )(*shards)
    my_chip = 2 * lax.axis_index("x") + lax.axis_index("y")
    return [lax.dynamic_update_slice(g, s[:, None], (0, my_chip, 0, 0)) for g, s in zip(got, shards)]


def _swap_halves_to_sibling(grs):
    nw = len(grs)

    def body(*refs):
        g_refs, t_refs, (send_sems, recv_sems) = refs[:nw], refs[nw:2 * nw], refs[2 * nw:]
        x, y, c = _place()
        cps = []
        for i in range(nw):
            hh = grs[i].shape[1] // 2
            cps.append(pltpu.make_async_remote_copy(
                src_ref=g_refs[i].at[:, pl.ds((1 - c) * hh, hh)], dst_ref=t_refs[i], send_sem=send_sems.at[i],
                recv_sem=recv_sems.at[i], device_id=(x, y, 1 - c), device_id_type=MESH))
            cps[-1].start()
        for cp in cps:
            cp.wait()

    return _pcall(
        body,
        name="grad_d2d",
        in_specs=[ANY] * nw,
        out_specs=[ANY] * nw,
        out_shape=[jax.ShapeDtypeStruct((g.shape[0], g.shape[1] // 2, g.shape[2]), g.dtype) for g in grs],
        scratch_shapes=[pltpu.SemaphoreType.DMA((nw,)), pltpu.SemaphoreType.DMA((nw,))],
    )(*grs)


def _add_half(gr, t1, place, name):
    nq, r, wd = gr.shape
    hh = r // 2
    tr = _tile(hh, 512, 16)
    nb = hh // tr

    def body(p_ref, g_ref, t_ref, o_ref):
        o_ref[...] = (g_ref[...] + t_ref[...]).astype(BF16)

    return _pcall(
        body,
        name=name,
        grid_spec=pltpu.PrefetchScalarGridSpec(
            num_scalar_prefetch=1,
            grid=(nq, nb),
            in_specs=[
                pl.BlockSpec((1, tr, wd), lambda q, i, p_ref: (q, p_ref[0] * nb + i, 0)),
                pl.BlockSpec((1, tr, wd), lambda q, i, p_ref: (q, i, 0)),
            ],
            out_specs=pl.BlockSpec((1, tr, wd), lambda q, i, p_ref: (q, i, 0)),
        ),
        out_shape=jax.ShapeDtypeStruct((nq, hh, wd), BF16),
        compiler_params=_params(("parallel", "parallel")),
    )(place, gr, t1)


def _scatter_partials(ps):
    nw = len(ps)

    def body(*refs):
        p_refs, t_refs, (send_sems, recv_sems) = refs[:nw], refs[nw:2 * nw], refs[2 * nw:]
        x, y, c = _place()
        cps = []
        for i in range(nw):
            for j, (px, py) in enumerate(_other_chips(x, y)):
                cps.append(pltpu.make_async_remote_copy(
                    src_ref=p_refs[i].at[2 * px + py], dst_ref=t_refs[i].at[j], send_sem=send_sems.at[3 * i + j],
                    recv_sem=recv_sems.at[3 * i + j], device_id=(px, py, c), device_id_type=MESH))
                cps[-1].start()
        for cp in cps:
            cp.wait()

    return _pcall(
        body,
        name="grad_ici",
        in_specs=[ANY] * nw,
        out_specs=[ANY] * nw,
        out_shape=[jax.ShapeDtypeStruct((3,) + p.shape[1:], p.dtype) for p in ps],
        scratch_shapes=[pltpu.SemaphoreType.DMA((3 * nw,)), pltpu.SemaphoreType.DMA((3 * nw,))],
    )(*ps)


def _add_partials(gr, t1, t2, place, name):
    nq, r, wd = gr.shape
    hh = r // 2
    tr = _tile(hh, 512, 16)
    nb = hh // tr

    def body(p_ref, g_ref, s_ref, t_ref, o_ref):
        o_ref[...] = (((g_ref[0] + s_ref[0]) + t_ref[0].astype(F32)) + t_ref[1].astype(F32)) + t_ref[2].astype(F32)

    return _pcall(
        body,
        name=name,
        grid_spec=pltpu.PrefetchScalarGridSpec(
            num_scalar_prefetch=1,
            grid=(nb,),
            in_specs=[
                pl.BlockSpec((1, tr, wd), lambda i, p_ref: (p_ref[1], p_ref[0] * nb + i, 0)),
                pl.BlockSpec((1, tr, wd), lambda i, p_ref: (p_ref[1], i, 0)),
                pl.BlockSpec((3, tr, wd), lambda i, p_ref: (0, i, 0)),
            ],
            out_specs=pl.BlockSpec((tr, wd), lambda i, p_ref: (i, 0)),
        ),
        out_shape=jax.ShapeDtypeStruct((hh, wd), F32),
        compiler_params=_params(("parallel",)),
    )(place, gr, t1, t2)


def _join_halves(halves):
    nw = len(halves)

    def body(*refs):
        h_refs, f_refs, (send_sems, recv_sems) = refs[:nw], refs[nw:2 * nw], refs[2 * nw:]
        x, y, c = _place()

        def copy(i, rows):
            hh = halves[i].shape[0]
            return pltpu.make_async_remote_copy(src_ref=h_refs[i], dst_ref=f_refs[i].at[pl.ds(rows * hh, hh)],
                                                send_sem=send_sems.at[i], recv_sem=recv_sems.at[i],
                                                device_id=(x, y, 1 - c), device_id_type=MESH)

        for i in range(nw):
            copy(i, c).start()
        for i in range(nw):
            copy(i, c).wait_send()
            copy(i, 1 - c).wait_recv()

    got = _pcall(
        body,
        name="grad_join",
        in_specs=[ANY] * nw,
        out_specs=[ANY] * nw,
        out_shape=[jax.ShapeDtypeStruct((2 * h.shape[0], h.shape[1]), h.dtype) for h in halves],
        scratch_shapes=[pltpu.SemaphoreType.DMA((nw,)), pltpu.SemaphoreType.DMA((nw,))],
    )(*halves)
    c = lax.axis_index("c")
    return [lax.dynamic_update_slice(g, h, (c * h.shape[0], 0)) for g, h in zip(got, halves)]


N_DEV = 8


def _allreduce_small(buf):
    rows, wd = buf.shape
    n = rows // N_DEV

    def body(x_ref, o_ref, stage_ref, send_sems, recv_sems):
        x, y, c = _place()
        me = 4 * x + 2 * y + c

        def peer(k):
            return (x ^ (k >> 2), y ^ ((k >> 1) & 1), c ^ (k & 1))

        def chunk(ref, d):
            return ref.at[pl.ds(pl.multiple_of(d * n, 8), n)]

        first = []
        for k in range(1, N_DEV):
            px, py, pc = peer(k)
            cp = pltpu.make_async_remote_copy(src_ref=chunk(x_ref, 4 * px + 2 * py + pc), dst_ref=stage_ref.at[k],
                                              send_sem=send_sems.at[k - 1], recv_sem=recv_sems.at[k - 1],
                                              device_id=(px, py, pc), device_id_type=MESH)
            cp.start()
            first.append(cp)
        acc = chunk(x_ref, me)[...]
        for k in range(1, N_DEV):
            first[k - 1].wait_recv()
            acc = acc + stage_ref[k]
        chunk(o_ref, me)[...] = acc
        second = []
        for k in range(1, N_DEV):
            cp = pltpu.make_async_remote_copy(src_ref=chunk(o_ref, me), dst_ref=chunk(o_ref, me),
                                              send_sem=send_sems.at[6 + k], recv_sem=recv_sems.at[6 + k],
                                              device_id=peer(k), device_id_type=MESH)
            cp.start()
            second.append(cp)
        for k in range(1, N_DEV):
            px, py, pc = peer(k)
            them = 4 * px + 2 * py + pc
            pltpu.make_async_remote_copy(src_ref=chunk(o_ref, them), dst_ref=chunk(o_ref, them),
                                         send_sem=send_sems.at[6 + k], recv_sem=recv_sems.at[6 + k],
                                         device_id=(px, py, pc), device_id_type=MESH).wait_recv()
        for cp in first + second:
            cp.wait_send()

    vmem = pl.BlockSpec(memory_space=pltpu.VMEM)
    return _pcall(
        body,
        name="allreduce_small",
        in_specs=[vmem],
        out_specs=vmem,
        out_shape=jax.ShapeDtypeStruct((rows, wd), F32),
        scratch_shapes=[pltpu.VMEM((N_DEV, n, wd), F32), pltpu.SemaphoreType.DMA((14,)),
                        pltpu.SemaphoreType.DMA((14,))],
    )(buf)


@functools.partial(jax.custom_vjp, nondiff_argnums=(1,))
def split_cols(z, spec):
    outs = []
    for segs in spec[0]:
        parts = [z[:, s:s + n] for s, n in segs]
        outs.append(parts[0] if len(parts) == 1 else jnp.concatenate(parts, axis=1))
    return tuple(outs)


def _split_cols_fwd(z, spec):
    return split_cols(z, spec), None


def _split_cols_bwd(spec, _, gs):
    pieces, width = spec
    segs = []
    for p, piece in enumerate(pieces):
        off = 0
        for s, n in piece:
            segs.append((s, n, p, off))
            off += n
    rows, parts, pos = gs[0].shape[0], [], 0
    for s, n, p, off in sorted(segs):
        if s > pos:
            parts.append(jnp.zeros((rows, s - pos), gs[0].dtype))
        parts.append(gs[p] if (off == 0 and n == gs[p].shape[1]) else gs[p][:, off:off + n])
        pos = s + n
    if pos < width:
        parts.append(jnp.zeros((rows, width - pos), gs[0].dtype))
    return (jnp.concatenate(parts, axis=1),)


split_cols.defvjp(_split_cols_fwd, _split_cols_bwd)


def _contiguous(sizes):
    pieces, off = [], 0
    for n in sizes:
        pieces.append(((off, n),))
        off += n
    return tuple(pieces), off


def _rope(t, cos, sin):
    half = t.shape[-1] // 2
    t1, t2 = t[..., :half], t[..., half:]
    return jnp.concatenate([t1 * cos - t2 * sin, t2 * cos + t1 * sin], axis=-1)


def _s5_mixer(u, sp, l, w16, toks):
    g, p_, h = SSM_GROUPS, SSM_STATE, SSM_GROUP_CH
    lam_re, lam_im = sp["ssm_lam_re", l], sp["ssm_lam_im", l]
    step = jnp.exp(sp["ssm_log_step", l])[:, None]
    mag, ang = jnp.exp(lam_re * step), lam_im * step
    lbr, lbi = mag * jnp.cos(ang), mag * jnp.sin(ang)
    den = lam_re * lam_re + lam_im * lam_im
    nr, ni = lbr - 1.0, lbi
    qr, qi = (nr * lam_re + ni * lam_im) / den, (ni * lam_re - nr * lam_im) / den
    b_re, b_im = sp["ssm_b_re", l], sp["ssm_b_im", l]
    bbr = qr[..., None] * b_re - qi[..., None] * b_im
    bbi = qr[..., None] * b_im + qi[..., None] * b_re
    nb = g // SSM_PACK
    eye = jnp.eye(SSM_PACK, dtype=F32)

    def in_blocks(t):
        return jnp.einsum("jgph,gk->jghkp", t.reshape(nb, SSM_PACK, p_, h), eye).reshape(nb, SSM_PACK * h, SSM_PACK * p_)

    def out_blocks(t):
        return jnp.einsum("jghp,gk->jgpkh", t.reshape(nb, SSM_PACK, h, p_), eye).reshape(nb, SSM_PACK * p_, SSM_PACK * h)

    nm = "ssm%d" % l
    ur = bd_matmul(u, in_blocks(bbr), nm + "_bur")
    ui = bd_matmul(u, in_blocks(bbi), nm + "_bui")
    xr, xi = s5_scan(ur, ui, lbr.reshape(1, g * p_), lbi.reshape(1, g * p_), nm + "_scan")
    y = (bd_matmul(xr, out_blocks(sp["ssm_c_re", l]), nm + "_cr") + bd_matmul(xi, out_blocks(-sp["ssm_c_im", l]), nm + "_ci")
         + sp["ssm_d", l].reshape(1, g * h) * u)
    y = jax.nn.gelu(y)
    zz = matmul_w(y, w16["ssm_w_glu", l], toks["ssm_w_glu", l], nm + "_glu")
    z_out, z_gate = split_cols(zz, _contiguous((D_MODEL, D_MODEL)))
    return z_out * jax.nn.sigmoid(z_gate)


def _mla_mixer(q_lat, kv_lat, k_rope, cos, sin, sp, l, w16, toks):
    s = q_lat.shape[0]
    nm = "mla%d" % l
    dqk = MLA_NOPE + MLA_ROPE
    qt = matmul_w_t(rmsnorm(q_lat, sp["mla_q_norm", l], nm + "_qn"), w16["mla_w_uq", l], toks["mla_w_uq", l],
                    nm + "_uq").reshape(MLA_HEADS, dqk, s)
    half = MLA_ROPE // 2
    q1, q2 = qt[:, MLA_NOPE:MLA_NOPE + half], qt[:, MLA_NOPE + half:]
    ct, st = cos.T[None], sin.T[None]
    qt = jnp.concatenate([qt[:, :MLA_NOPE], q1 * ct - q2 * st, q2 * ct + q1 * st], axis=1).reshape(MLA_HEADS * dqk, s)
    kvt = matmul_w_t(rmsnorm(kv_lat, sp["mla_kv_norm", l], nm + "_kvn"), w16["mla_w_ukv", l], toks["mla_w_ukv", l],
                     nm + "_ukv").reshape(MLA_HEADS, MLA_NOPE + MLA_V, s)
    k_pe = _rope(k_rope, cos, sin)
    k = jnp.concatenate([kvt[:, :MLA_NOPE].transpose(0, 2, 1),
                         jnp.broadcast_to(k_pe[None], (MLA_HEADS, s, MLA_ROPE))], axis=-1)
    vt = kvt[:, MLA_NOPE:].reshape(MLA_HEADS * MLA_V, s)
    ot = attention(qt, k, vt, 1.0 / math.sqrt(dqk), True, nm + "_att")
    return matmul_w_at(ot, w16["mla_w_o", l], toks["mla_w_o", l], nm + "_o")


def _hgrn_mixer(q, f_logit, i_in, g, lb, sp, l, w16, toks):
    s = q.shape[0]
    nm = "hg%d" % l
    f = lb + (1.0 - lb) * jax.nn.sigmoid(f_logit)
    o = hgrn_core(jax.nn.silu(q), 1.0 - f, i_in, jnp.log(f), nm + "_core")
    o = rmsnorm(o.reshape(s * HG_HEADS, HG_DV), sp["hg_g_norm", l], nm + "_gn").reshape(s, HG_HEADS * HG_DV)
    return matmul_w(o * jax.nn.silu(g), w16["hg_w_o", l], toks["hg_w_o", l], nm + "_o")


def _cross_attention(h, mem_n, l, w16, toks):
    s, m = h.shape[0], mem_n.shape[0]
    nm = "xat%d" % l
    xw = X_HEADS * X_HEAD_DIM
    qt = matmul_w_t(h, w16["x_w_q", l], toks["x_w_q", l], nm + "_q")
    kv = matmul_w(mem_n, w16["x_w_kv", l], toks["x_w_kv", l], nm + "_kv")
    k, v = split_cols(kv, _contiguous((xw, xw)))
    k = k.reshape(m, X_HEADS, X_HEAD_DIM).transpose(1, 0, 2)
    ot = attention(qt, k, v.T, 1.0 / math.sqrt(X_HEAD_DIM), False, nm + "_att")
    return matmul_w_at(ot, w16["x_w_o", l], toks["x_w_o", l], nm + "_o")


def _in_spec():
    hw = HG_HEADS * HG_DK
    sizes = (SSM_WIDTH, MLA_Q_RANK, MLA_KV_RANK, MLA_ROPE, hw, hw, hw, hw, 3 * D_MODEL)
    pieces, lo = [], 0
    for n in sizes:
        segs, c = [], lo
        while c < lo + n:
            end = min(lo + n, (c // D_IN_SHARD + 1) * D_IN_SHARD)
            segs.append((c + (D_IN_BLOCK - D_IN_SHARD) * (c // D_IN_SHARD), end - c))
            c = end
        pieces.append(tuple(segs))
        lo += n
    assert lo == D_IN
    return tuple(pieces), N_CHIPS * D_IN_BLOCK


def _layer(x, mem, cos, sin, lb, l, sp, w16, toks):
    s = x.shape[0]
    nm = "l%d" % l
    h = rmsnorm(x, sp["norm_mix", l], nm + "_nmix")
    z = matmul_w(h, w16["w_in", l], toks["w_in", l], nm + "_in")
    u, q_lat, kv_lat, k_rope, hg_q, hg_f, hg_i, hg_g, gate_logits = split_cols(z, _in_spec())
    g_ssm, g_mla, g_hg = split_cols(jax.nn.sigmoid(gate_logits), _contiguous((D_MODEL,) * 3))
    y_ssm = _s5_mixer(u, sp, l, w16, toks)
    y_mla = _mla_mixer(q_lat, kv_lat, k_rope, cos, sin, sp, l, w16, toks)
    y_hg = _hgrn_mixer(hg_q, hg_f, hg_i, hg_g, lb, sp, l, w16, toks)
    merged = g_ssm * y_ssm + g_mla * y_mla + g_hg * y_hg
    x = x + matmul_w(merged, w16["w_out", l], toks["w_out", l], nm + "_out")
    hc = rmsnorm(x, sp["norm_cross", l], nm + "_ncross")
    mem_n = rmsnorm(mem, sp["norm_mem", l], nm + "_nmem")
    x = x + _cross_attention(hc, mem_n, l, w16, toks)
    hf = rmsnorm(x, sp["norm_ffn", l], nm + "_nffn")
    gu = matmul_w(hf, w16["ffn_w_gate_up", l], toks["ffn_w_gate_up", l], nm + "_gu")
    gt, up = split_cols(gu, _contiguous((D_FF, D_FF)))
    return x + matmul_w(jax.nn.silu(gt) * up, w16["ffn_w_down", l], toks["ffn_w_down", l], nm + "_down")


def _loss_fn(sp, toks, x, w16, mem, pos, target):
    half = MLA_ROPE // 2
    inv_freq = ROPE_THETA ** (-jnp.arange(half, dtype=F32) / half)
    ang = pos.astype(F32)[:, None] * inv_freq
    cos, sin = jnp.cos(ang), jnp.sin(ang)
    lb_p = jax.nn.softmax(sp["hg_lb", ALL], axis=0)
    lower = jnp.cumsum(lb_p, axis=0) - lb_p[0:1]
    for l in range(DEPTH):
        x = _layer(x, mem, cos, sin, lower[l], l, sp, w16, toks)
    y = rmsnorm(x, sp["norm_final", ALL], "nfinal")
    return 0.5 * jnp.sum(jnp.mean(jnp.square(y - target), axis=-1))


WEIGHTS = ["norm_mix", "w_in", "ssm_lam_re", "ssm_lam_im", "ssm_b_re", "ssm_b_im", "ssm_c_re", "ssm_c_im", "ssm_d",
           "ssm_log_step", "ssm_w_glu", "mla_q_norm", "mla_kv_norm", "mla_w_uq", "mla_w_ukv", "mla_w_o", "hg_lb",
           "hg_g_norm", "hg_w_o", "w_out", "norm_cross", "norm_mem", "x_w_q", "x_w_kv", "x_w_o", "norm_ffn",
           "ffn_w_gate_up", "ffn_w_down", "norm_final"]
INPUTS = ["x", "mem", "positions"] + WEIGHTS + ["loss_target"] + ["m_" + n for n in WEIGHTS] + ["v_" + n for n in WEIGHTS]
BIG = [("w_in", 2), ("ssm_w_glu", 2), ("mla_w_uq", 2), ("mla_w_ukv", 2), ("mla_w_o", 2), ("hg_w_o", 2), ("w_out", 1),
       ("x_w_q", 1), ("x_w_kv", 1), ("x_w_o", 2), ("ffn_w_gate_up", 2), ("ffn_w_down", 1)]
SMALL = [n for n in WEIGHTS if n not in dict(BIG)]
WHOLE = ("hg_lb", "norm_final")
ALL = -1
SMALL_W = 128
BLOCKED = ("w_in", "ffn_w_gate_up")


def _pack(arrs, rows, width, dtype):
    flat = jnp.concatenate([a.astype(dtype).reshape(-1) for a in arrs])
    return jnp.pad(flat, (0, rows * width - flat.shape[0])).reshape(rows, width)


def _unpack(buf, like):
    flat = buf.reshape(-1)
    out, off = [], 0
    for a in like:
        n = math.prod(a.shape)
        out.append(flat[off:off + n].reshape(a.shape))
        off += n
    return out


def kernel(
    x, mem, positions, norm_mix, w_in, ssm_lam_re, ssm_lam_im, ssm_b_re, ssm_b_im, ssm_c_re, ssm_c_im, ssm_d,
    ssm_log_step, ssm_w_glu, mla_q_norm, mla_kv_norm, mla_w_uq, mla_w_ukv, mla_w_o, hg_lb, hg_g_norm, hg_w_o,
    w_out, norm_cross, norm_mem, x_w_q, x_w_kv, x_w_o, norm_ffn, ffn_w_gate_up, ffn_w_down, norm_final,
    loss_target, m_norm_mix, m_w_in, m_ssm_lam_re, m_ssm_lam_im, m_ssm_b_re, m_ssm_b_im, m_ssm_c_re, m_ssm_c_im,
    m_ssm_d, m_ssm_log_step, m_ssm_w_glu, m_mla_q_norm, m_mla_kv_norm, m_mla_w_uq, m_mla_w_ukv, m_mla_w_o,
    m_hg_lb, m_hg_g_norm, m_hg_w_o, m_w_out, m_norm_cross, m_norm_mem, m_x_w_q, m_x_w_kv, m_x_w_o, m_norm_ffn,
    m_ffn_w_gate_up, m_ffn_w_down, m_norm_final, v_norm_mix, v_w_in, v_ssm_lam_re, v_ssm_lam_im, v_ssm_b_re,
    v_ssm_b_im, v_ssm_c_re, v_ssm_c_im, v_ssm_d, v_ssm_log_step, v_ssm_w_glu, v_mla_q_norm, v_mla_kv_norm,
    v_mla_w_uq, v_mla_w_ukv, v_mla_w_o, v_hg_lb, v_hg_g_norm, v_hg_w_o, v_w_out, v_norm_cross, v_norm_mem,
    v_x_w_q, v_x_w_kv, v_x_w_o, v_norm_ffn, v_ffn_w_gate_up, v_ffn_w_down, v_norm_final):
    given = locals()
    a = {n: given[n] for n in INPUTS}
    x, mem, pos, target = a["x"][0], a["mem"][0], a["positions"][0], a["loss_target"][0]
    place = jnp.stack([lax.axis_index("c"), 2 * lax.axis_index("x") + lax.axis_index("y")]).astype(jnp.int32)

    def shard16(n):
        w = a[n].astype(BF16)
        if n == "w_in":
            w = jnp.pad(w, ((0, 0), (0, 0), (0, D_IN_BLOCK - D_IN_SHARD)))
        return w

    gathered = _gather_weights([shard16(n) for n, _ in BIG])
    w16, toks = {}, {}
    for (n, ax), got in zip(BIG, gathered):
        for l in range(DEPTH):
            if n in BLOCKED:
                full = got[l]
            elif ax == 2:
                full = jnp.concatenate([got[l, q] for q in range(N_CHIPS)], axis=1)
            else:
                full = got[l].reshape(N_CHIPS * got.shape[2], got.shape[3])
            w16[n, l] = full
            toks[n, l] = jnp.zeros(full.shape, F32)
    sp = {}
    for n in SMALL:
        if n in WHOLE:
            sp[n, ALL] = a[n]
        else:
            for l in range(DEPTH):
                sp[n, l] = a[n][l]

    loss, (g_sp, g_tok, g_x) = jax.value_and_grad(_loss_fn, argnums=(0, 1, 2))(sp, toks, x, w16, mem, pos, target)
    g_small = {n: g_sp[n, ALL] if n in WHOLE else jnp.stack([g_sp[n, l] for l in range(DEPTH)]) for n in SMALL}

    grads = []
    for n, ax in BIG:
        layers = []
        for l in range(DEPTH):
            g = g_tok[n, l]
            if n in BLOCKED:
                layers.append(g)
            elif ax == 2:
                layers.append(g.reshape(g.shape[0], N_CHIPS, g.shape[1] // N_CHIPS).transpose(1, 0, 2))
            else:
                layers.append(g.reshape(N_CHIPS, g.shape[0] // N_CHIPS, g.shape[1]))
        grads.append(jnp.concatenate(layers, axis=1))
    names = [n for n, _ in BIG]
    from_sibling = _swap_halves_to_sibling(grads)
    chip_sums = [_add_half(g, t, place, "grad_add_half_" + n) for n, g, t in zip(names, grads, from_sibling)]
    from_chips = _scatter_partials(chip_sums)
    totals = [_add_partials(g, t1, t2, place, "grad_add_partials_" + n)
              for n, g, t1, t2 in zip(names, grads, from_sibling, from_chips)]
    g_big = {n: g[:, :a[n].shape[2]].reshape(a[n].shape) for n, g in zip(names, _join_halves(totals))}

    small_like = [a[n] for n in SMALL] + [jnp.zeros((1,), F32)]
    n_small = sum(math.prod(t.shape) for t in small_like)
    srows = -(-n_small // (SMALL_W * 8 * N_DEV)) * 8 * N_DEV
    reduced = _allreduce_small(_pack([g_small[n] for n in SMALL] + [loss.reshape(1)], srows, SMALL_W, F32))
    red = _unpack(reduced, small_like)
    g_sm = dict(zip(SMALL, red[:-1]))
    loss = red[-1].reshape(())

    out_g, out_d, out_m, out_v = {}, {}, {}, {}
    for n, _ in BIG:
        shp = a[n].shape
        two = lambda t: t.reshape(shp[0] * shp[1], shp[2])
        d, m, v = _adamw_call(two(a[n]), two(g_big[n]), two(a["m_" + n]), two(a["v_" + n]), "adamw_" + n)
        out_g[n], out_d[n], out_m[n], out_v[n] = g_big[n], d.reshape(shp), m.reshape(shp), v.reshape(shp)
    small_arrs = [a[n] for n in SMALL]
    d, m, v = _adamw_call(
        _pack(small_arrs, srows, SMALL_W, F32), reduced,
        _pack([a["m_" + n] for n in SMALL], srows, SMALL_W, F32),
        _pack([a["v_" + n] for n in SMALL], srows, SMALL_W, F32), "adamw_small")
    for n, gg, dd, mm_, vv in zip(SMALL, red[:-1], _unpack(d, small_arrs), _unpack(m, small_arrs), _unpack(v, small_arrs)):
        out_g[n], out_d[n], out_m[n], out_v[n] = gg, dd, mm_, vv

    return (loss, g_x[None], *[out_g[n] for n in WEIGHTS], *[out_d[n] for n in WEIGHTS],
            *[out_m[n] for n in WEIGHTS], *[out_v[n] for n in WEIGHTS])
```

```python
import functools
import math

import jax
import jax.numpy as jnp
from jax import lax
from jax.experimental import pallas as pl
from jax.experimental.pallas import tpu as pltpu

F32 = jnp.float32
BF16 = jnp.bfloat16
MESH = pl.DeviceIdType.MESH

D_MODEL = 1024
DEPTH = 2
RMS_EPS = 1e-6
SSM_GROUPS, SSM_GROUP_CH, SSM_STATE = 32, 16, 64
SSM_WIDTH = SSM_GROUPS * SSM_GROUP_CH
SSM_LANES = SSM_GROUPS * SSM_STATE
SSM_PACK = 8
MLA_HEADS, MLA_Q_RANK, MLA_KV_RANK, MLA_NOPE, MLA_ROPE, MLA_V = 8, 512, 256, 64, 32, 64
ROPE_THETA = 10000.0
HG_HEADS, HG_DK, HG_DV, HG_CHUNK = 4, 128, 128, 64
HG_SUB_FWD, HG_SUB_BWD = 32, 16
X_HEADS, X_HEAD_DIM = 4, 128
D_FF = 2816
D_IN = 6432
N_CHIPS = 4
D_IN_SHARD = D_IN // N_CHIPS
D_IN_BLOCK = 1664
MASK_VALUE = -1e30

ADAM_LR, ADAM_B1, ADAM_B2, ADAM_EPS, ADAM_WD, ADAM_STEP = 0.001, 0.9, 0.999, 1e-08, 0.01, 10

VMEM_LIMIT = 48 * 1024 * 1024


def _pcall(body, **kw):
    return pl.pallas_call(body, **kw)


def _params(sem):
    return pltpu.CompilerParams(dimension_semantics=sem, vmem_limit_bytes=VMEM_LIMIT)


def _tile(n, cap, align=128):
    if n <= cap:
        return n
    t = (cap // align) * align
    while t >= align:
        if n % t == 0:
            return t
        t -= align
    return n


MM_VMEM_BUDGET = 36 * 1024 * 1024
MM_TILE_CAP = 2048


def _divisors(n, cap, align=128):
    out = [n] if n <= cap else []
    out += [t for t in range(align, min(n, cap + 1), align) if n % t == 0]
    return sorted(set(out), reverse=True)


@functools.lru_cache(maxsize=None)
def _mm_tiles(m, n, k, a_bytes, b_bytes, o_bytes, n_unit, k_unit):
    best = None
    for tm in _divisors(m, MM_TILE_CAP):
        for tn in _divisors(n_unit, MM_TILE_CAP):
            for tk in _divisors(k_unit, MM_TILE_CAP):
                nk = k // tk
                need = 2 * (tm * tk * a_bytes + tk * tn * b_bytes + tm * tn * o_bytes)
                need += tm * tn * 4 * (2 if nk > 1 else 1)
                need += (tm * tk * 2 if a_bytes > 2 else 0) + (tk * tn * 2 if b_bytes > 2 else 0)
                if need > MM_VMEM_BUDGET:
                    continue
                key = ((m // tm) * (n // tn) * nk, nk, -tn)
                if best is None or key < best[0]:
                    best = (key, (tm, tn, tk))
    assert best is not None, (m, n, k)
    return best[1]


def _mm(a, b, *, ta=False, tb=False, out_dtype=F32, name="mm", b_blocks=False, out_blocks=0):
    if ta:
        K, M = a.shape
    else:
        M, K = a.shape
    n_unit = k_unit = None
    if b_blocks:
        nb, rows, cols = b.shape
        if tb:
            N, Kb, k_unit = rows, nb * cols, cols
        else:
            Kb, N, n_unit = rows, nb * cols, cols
    elif tb:
        N, Kb = b.shape
    else:
        Kb, N = b.shape
    if out_blocks:
        assert not b_blocks and N % out_blocks == 0
        n_unit = N // out_blocks
    assert K == Kb, (a.shape, b.shape, ta, tb)
    tm, tn, tk = _mm_tiles(M, N, K, a.dtype.itemsize, b.dtype.itemsize, jnp.dtype(out_dtype).itemsize,
                           n_unit or N, k_unit or K)
    nk = K // tk
    dims = (((0,) if ta else (1,), (1,) if tb else (0,)), ((), ()))

    def body(a_ref, b_ref, o_ref, *acc):
        av = a_ref[...].astype(BF16)
        bv = b_ref[...].astype(BF16)
        part = lax.dot_general(av, bv, dims, preferred_element_type=F32)
        if nk == 1:
            o_ref[...] = part.astype(o_ref.dtype)
        else:
            acc_ref = acc[0]
            k = pl.program_id(2)

            @pl.when(k == 0)
            def _():
                acc_ref[...] = part

            @pl.when(k > 0)
            def _():
                acc_ref[...] += part

            @pl.when(k == nk - 1)
            def _():
                o_ref[...] = acc_ref[...].astype(o_ref.dtype)

    a_spec = pl.BlockSpec((tk, tm), lambda i, j, k: (k, i)) if ta else pl.BlockSpec((tm, tk), lambda i, j, k: (i, k))
    if b_blocks and tb:
        r = k_unit // tk
        b_spec = pl.BlockSpec((None, tn, tk), lambda i, j, k: (k // r, j, k % r))
    elif b_blocks:
        r = n_unit // tn
        b_spec = pl.BlockSpec((None, tk, tn), lambda i, j, k: (j // r, k, j % r))
    elif tb:
        b_spec = pl.BlockSpec((tn, tk), lambda i, j, k: (j, k))
    else:
        b_spec = pl.BlockSpec((tk, tn), lambda i, j, k: (k, j))
    if out_blocks:
        r = n_unit // tn
        out_spec = pl.BlockSpec((None, tm, tn), lambda i, j, k: (j // r, i, j % r))
        out_shape = jax.ShapeDtypeStruct((out_blocks, M, n_unit), out_dtype)
    else:
        out_spec = pl.BlockSpec((tm, tn), lambda i, j, k: (i, j))
        out_shape = jax.ShapeDtypeStruct((M, N), out_dtype)
    return _pcall(
        body,
        name=name,
        grid=(M // tm, N // tn, nk),
        in_specs=[a_spec, b_spec],
        out_specs=out_spec,
        out_shape=out_shape,
        scratch_shapes=[pltpu.VMEM((tm, tn), F32)] if nk > 1 else [],
        compiler_params=_params(("parallel", "parallel", "arbitrary")),
    )(a, b)


def _bd_call(a, bc, transpose, name):
    m = a.shape[0]
    nb, ka, kb = bc.shape
    win, wout = (kb, ka) if transpose else (ka, kb)
    assert a.shape[1] == nb * win
    tm = _tile(m, 1024)
    dims = _NT if transpose else (((1,), (0,)), ((), ()))

    def body(a_ref, b_ref, o_ref):
        o_ref[...] = lax.dot_general(a_ref[...].astype(BF16), b_ref[0].astype(BF16), dims, preferred_element_type=F32)

    return _pcall(
        body,
        name=name,
        grid=(m // tm, nb),
        in_specs=[pl.BlockSpec((tm, win), lambda i, j: (i, j)), pl.BlockSpec((1, ka, kb), lambda i, j: (j, 0, 0))],
        out_specs=pl.BlockSpec((tm, wout), lambda i, j: (i, j)),
        out_shape=jax.ShapeDtypeStruct((m, nb * wout), F32),
        compiler_params=_params(("parallel", "parallel")),
    )(a, bc)


def _bd_dw_call(a, g, nb, name):
    m = a.shape[0]
    ka, kb = a.shape[1] // nb, g.shape[1] // nb
    tk = _tile(m, 1024)
    nk = m // tk

    def body(a_ref, g_ref, o_ref):
        part = lax.dot_general(a_ref[...].astype(BF16), g_ref[...].astype(BF16), _TN, preferred_element_type=F32)
        k = pl.program_id(1)

        @pl.when(k == 0)
        def _():
            o_ref[0] = part

        @pl.when(k > 0)
        def _():
            o_ref[0] += part

    return _pcall(
        body,
        name=name,
        grid=(nb, nk),
        in_specs=[pl.BlockSpec((tk, ka), lambda j, k: (k, j)), pl.BlockSpec((tk, kb), lambda j, k: (k, j))],
        out_specs=pl.BlockSpec((1, ka, kb), lambda j, k: (j, 0, 0)),
        out_shape=jax.ShapeDtypeStruct((nb, ka, kb), F32),
        compiler_params=_params(("parallel", "arbitrary")),
    )(a, g)


@functools.partial(jax.custom_vjp, nondiff_argnums=(2,))
def bd_matmul(a, bc, name):
    return _bd_call(a, bc, False, name + "_f")


def _bd_matmul_fwd(a, bc, name):
    return _bd_call(a, bc, False, name + "_f"), (a, bc)


def _bd_matmul_bwd(name, res, g):
    a, bc = res
    return _bd_call(g, bc, True, name + "_da"), _bd_dw_call(a, g, bc.shape[0], name + "_dw")


bd_matmul.defvjp(_bd_matmul_fwd, _bd_matmul_bwd)


@functools.partial(jax.custom_vjp, nondiff_argnums=(3,))
def matmul_w(a, w16, tok, name):
    return _mm(a, w16, b_blocks=w16.ndim == 3, name=name + "_f")


def _matmul_w_fwd(a, w16, tok, name):
    return _mm(a, w16, b_blocks=w16.ndim == 3, name=name + "_f"), (a, w16)


def _matmul_w_bwd(name, res, g):
    a, w16 = res
    blocks = w16.shape[0] if w16.ndim == 3 else 0
    return (_mm(g, w16, tb=True, b_blocks=bool(blocks), name=name + "_da"), jnp.zeros_like(w16),
            _mm(a, g, ta=True, out_blocks=blocks, name=name + "_dw"))


matmul_w.defvjp(_matmul_w_fwd, _matmul_w_bwd)


@functools.partial(jax.custom_vjp, nondiff_argnums=(3,))
def matmul_w_t(a, w16, tok, name):
    return _mm(w16, a, ta=True, tb=True, name=name + "_f")


def _matmul_w_t_fwd(a, w16, tok, name):
    return _mm(w16, a, ta=True, tb=True, name=name + "_f"), (a, w16)


def _matmul_w_t_bwd(name, res, g):
    a, w16 = res
    return (_mm(g, w16, ta=True, tb=True, name=name + "_da"), jnp.zeros_like(w16),
            _mm(a, g, ta=True, tb=True, name=name + "_dw"))


matmul_w_t.defvjp(_matmul_w_t_fwd, _matmul_w_t_bwd)


@functools.partial(jax.custom_vjp, nondiff_argnums=(3,))
def matmul_w_at(at, w16, tok, name):
    return _mm(at, w16, ta=True, name=name + "_f")


def _matmul_w_at_fwd(at, w16, tok, name):
    return _mm(at, w16, ta=True, name=name + "_f"), (at, w16)


def _matmul_w_at_bwd(name, res, g):
    at, w16 = res
    return _mm(w16, g, tb=True, name=name + "_da"), jnp.zeros_like(w16), _mm(at, g, name=name + "_dw")


matmul_w_at.defvjp(_matmul_w_at_fwd, _matmul_w_at_bwd)


def _rms_fwd_call(x, g, name):
    rows, d = x.shape
    tr = _tile(rows, 512, 8)

    def body(x_ref, g_ref, y_ref):
        xv = x_ref[...]
        rstd = lax.rsqrt(jnp.mean(xv * xv, axis=-1, keepdims=True) + RMS_EPS)
        y_ref[...] = xv * rstd * g_ref[...]

    return _pcall(
        body,
        name=name,
        grid=(rows // tr,),
        in_specs=[pl.BlockSpec((tr, d), lambda i: (i, 0)), pl.BlockSpec((1, d), lambda i: (0, 0))],
        out_specs=pl.BlockSpec((tr, d), lambda i: (i, 0)),
        out_shape=jax.ShapeDtypeStruct((rows, d), F32),
        compiler_params=_params(("parallel",)),
    )(x, g.reshape(1, d))


def _rms_bwd_call(x, g, dy, name):
    rows, d = x.shape
    tr = _tile(rows, 512, 8)
    nb = rows // tr

    def body(x_ref, g_ref, dy_ref, dx_ref, dg_ref):
        xv = x_ref[...]
        rstd = lax.rsqrt(jnp.mean(xv * xv, axis=-1, keepdims=True) + RMS_EPS)
        xh = xv * rstd
        dyv = dy_ref[...]
        dyg = dyv * g_ref[...]
        dx_ref[...] = rstd * (dyg - xh * jnp.mean(dyg * xh, axis=-1, keepdims=True))
        dg_ref[...] = jnp.sum((dyv * xh).reshape(tr // 8, 8, d), axis=0)

    dx, dgp = _pcall(
        body,
        name=name,
        grid=(nb,),
        in_specs=[
            pl.BlockSpec((tr, d), lambda i: (i, 0)),
            pl.BlockSpec((1, d), lambda i: (0, 0)),
            pl.BlockSpec((tr, d), lambda i: (i, 0)),
        ],
        out_specs=[pl.BlockSpec((tr, d), lambda i: (i, 0)), pl.BlockSpec((8, d), lambda i: (i, 0))],
        out_shape=[jax.ShapeDtypeStruct((rows, d), F32), jax.ShapeDtypeStruct((nb * 8, d), F32)],
        compiler_params=_params(("parallel",)),
    )(x, g.reshape(1, d), dy)
    return dx, jnp.sum(dgp, axis=0)


@functools.partial(jax.custom_vjp, nondiff_argnums=(2,))
def rmsnorm(x, g, name):
    return _rms_fwd_call(x, g, name + "_f")


def _rmsnorm_fwd(x, g, name):
    return _rms_fwd_call(x, g, name + "_f"), (x, g)


def _rmsnorm_bwd(name, res, dy):
    x, g = res
    return _rms_bwd_call(x, g, dy, name + "_b")


rmsnorm.defvjp(_rmsnorm_fwd, _rmsnorm_bwd)


def _attn_blocks(sq, sk):
    return _tile(sq, 1024), _tile(sk, 512)


def _attn_pt(qs, k_ref, qi, ki, bq, bk, masked, shift):
    st = jnp.dot(k_ref[0].astype(BF16), qs, preferred_element_type=F32)
    if masked:
        kpos = ki * bk + lax.broadcasted_iota(jnp.int32, (bk, bq), 0)
        qpos = qi * bq + lax.broadcasted_iota(jnp.int32, (bk, bq), 1)
        st = jnp.where(kpos <= qpos, st, MASK_VALUE)
    return st if shift is None else jnp.exp(st - shift)


def _attn_pairs(nq, nk, bq, bk, causal, k_major):
    ok = lambda i, j: (not causal) or j * bk <= i * bq + bq - 1
    if k_major:
        pairs = [(i, j) for j in range(nk) for i in range(nq) if ok(i, j)]
    else:
        pairs = [(i, j) for i in range(nq) for j in range(nk) if ok(i, j)]
    qtab = jnp.asarray([p[0] for p in pairs], jnp.int32)
    ktab = jnp.asarray([p[1] for p in pairs], jnp.int32)
    first_q = (lambda j: (j * bk) // bq) if causal else (lambda j: 0)
    last_k = (lambda i: jnp.minimum((i * bq + bq - 1) // bk, nk - 1)) if causal else (lambda i: nk - 1)
    return qtab, ktab, first_q, last_k


def _attn_cases(causal, qi, ki, bq, bk, step):
    if not causal:
        step(False)
        return
    pl.when(ki * bk + bk - 1 <= qi * bq)(functools.partial(step, False))
    pl.when(ki * bk + bk - 1 > qi * bq)(functools.partial(step, True))


def _attn_fwd_call(qt, k, vt, scale, causal, name):
    h, sk, dq = k.shape
    sq = qt.shape[1]
    dv = vt.shape[0] // h
    bq, bk = _attn_blocks(sq, sk)
    nq, nk = sq // bq, sk // bk

    qtab, ktab, _, last_k = _attn_pairs(nq, nk, bq, bk, causal, False)

    def body(qi_ref, ki_ref, q_ref, k_ref, v_ref, o_ref, lse_ref, m_sc, l_sc, acc_sc):
        qi, ki = qi_ref[pl.program_id(1)], ki_ref[pl.program_id(1)]

        @pl.when(ki == 0)
        def _():
            m_sc[...] = jnp.full(m_sc.shape, -jnp.inf, F32)
            l_sc[...] = jnp.zeros(l_sc.shape, F32)
            acc_sc[...] = jnp.zeros(acc_sc.shape, F32)

        def step(masked):
            qs = (q_ref[...] * scale).astype(BF16)
            st = _attn_pt(qs, k_ref, qi, ki, bq, bk, masked, None)
            m_prev = m_sc[...]
            m_new = jnp.maximum(m_prev, jnp.max(st, axis=0, keepdims=True))
            alpha = jnp.exp(m_prev - m_new)
            pt = jnp.exp(st - m_new)
            l_sc[...] = alpha * l_sc[...] + jnp.sum(pt, axis=0, keepdims=True)
            acc_sc[...] = alpha * acc_sc[...] + jnp.dot(v_ref[...].astype(BF16), pt.astype(BF16),
                                                        preferred_element_type=F32)
            m_sc[...] = m_new

        _attn_cases(causal, qi, ki, bq, bk, step)

        @pl.when(ki == last_k(qi))
        def _():
            o_ref[...] = acc_sc[...] / l_sc[...]
            lse_ref[0] = m_sc[...] + jnp.log(l_sc[...])

    qmap = lambda hh, p, qi_ref, ki_ref: (hh, qi_ref[p])
    return _pcall(
        body,
        name=name,
        grid_spec=pltpu.PrefetchScalarGridSpec(
            num_scalar_prefetch=2,
            grid=(h, qtab.shape[0]),
            in_specs=[
                pl.BlockSpec((dq, bq), qmap),
                pl.BlockSpec((1, bk, dq), lambda hh, p, qi_ref, ki_ref: (hh, ki_ref[p], 0)),
                pl.BlockSpec((dv, bk), lambda hh, p, qi_ref, ki_ref: (hh, ki_ref[p])),
            ],
            out_specs=[pl.BlockSpec((dv, bq), qmap),
                       pl.BlockSpec((1, 1, bq), lambda hh, p, qi_ref, ki_ref: (hh, 0, qi_ref[p]))],
            scratch_shapes=[pltpu.VMEM((1, bq), F32), pltpu.VMEM((1, bq), F32), pltpu.VMEM((dv, bq), F32)],
        ),
        out_shape=[jax.ShapeDtypeStruct((h * dv, sq), F32), jax.ShapeDtypeStruct((h, 1, sq), F32)],
        compiler_params=_params(("parallel", "arbitrary")),
    )(qtab, ktab, qt, k, vt)


_TN = (((0,), (0,)), ((), ()))
_NT = (((1,), (1,)), ((), ()))


def _attn_dst(qs, k_ref, v_ref, do_ref, lse_ref, dl_ref, qi, ki, bq, bk, masked):
    pt = _attn_pt(qs, k_ref, qi, ki, bq, bk, masked, lse_ref[0])
    dpt = lax.dot_general(v_ref[...].astype(BF16), do_ref[...].astype(BF16), _TN, preferred_element_type=F32)
    return pt, pt * (dpt - dl_ref[0])


def _attn_dq_call(qt, k, vt, dot, lse, delta, scale, causal, name):
    h, sk, dq = k.shape
    sq = qt.shape[1]
    dv = vt.shape[0] // h
    bq, bk = _attn_blocks(sq, sk)
    nq, nk = sq // bq, sk // bk

    qtab, ktab, _, last_k = _attn_pairs(nq, nk, bq, bk, causal, False)

    def body(qi_ref, ki_ref, q_ref, k_ref, v_ref, do_ref, lse_ref, dl_ref, dq_ref, acc_sc):
        qi, ki = qi_ref[pl.program_id(1)], ki_ref[pl.program_id(1)]

        @pl.when(ki == 0)
        def _():
            acc_sc[...] = jnp.zeros(acc_sc.shape, F32)

        def step(masked):
            qs = (q_ref[...] * scale).astype(BF16)
            _, dst = _attn_dst(qs, k_ref, v_ref, do_ref, lse_ref, dl_ref, qi, ki, bq, bk, masked)
            acc_sc[...] += lax.dot_general(k_ref[0].astype(BF16), dst.astype(BF16), _TN, preferred_element_type=F32)

        _attn_cases(causal, qi, ki, bq, bk, step)

        @pl.when(ki == last_k(qi))
        def _():
            dq_ref[...] = acc_sc[...] * scale

    qmap = lambda hh, p, qi_ref, ki_ref: (hh, qi_ref[p])
    rowmap = lambda hh, p, qi_ref, ki_ref: (hh, 0, qi_ref[p])
    return _pcall(
        body,
        name=name,
        grid_spec=pltpu.PrefetchScalarGridSpec(
            num_scalar_prefetch=2,
            grid=(h, qtab.shape[0]),
            in_specs=[
                pl.BlockSpec((dq, bq), qmap),
                pl.BlockSpec((1, bk, dq), lambda hh, p, qi_ref, ki_ref: (hh, ki_ref[p], 0)),
                pl.BlockSpec((dv, bk), lambda hh, p, qi_ref, ki_ref: (hh, ki_ref[p])),
                pl.BlockSpec((dv, bq), qmap),
                pl.BlockSpec((1, 1, bq), rowmap),
                pl.BlockSpec((1, 1, bq), rowmap),
            ],
            out_specs=pl.BlockSpec((dq, bq), qmap),
            scratch_shapes=[pltpu.VMEM((dq, bq), F32)],
        ),
        out_shape=jax.ShapeDtypeStruct((h * dq, sq), F32),
        compiler_params=_params(("parallel", "arbitrary")),
    )(qtab, ktab, qt, k, vt, dot, lse, delta)


def _attn_dkv_call(qt, k, vt, dot, lse, delta, scale, causal, name):
    h, sk, dq = k.shape
    sq = qt.shape[1]
    dv = vt.shape[0] // h
    bq, bk = _attn_blocks(sq, sk)
    nq, nk = sq // bq, sk // bk

    qtab, ktab, first_q, _ = _attn_pairs(nq, nk, bq, bk, causal, True)

    def body(qi_ref, ki_ref, q_ref, k_ref, v_ref, do_ref, lse_ref, dl_ref, dk_ref, dv_ref, dk_sc, dv_sc):
        qi, ki = qi_ref[pl.program_id(1)], ki_ref[pl.program_id(1)]

        @pl.when(qi == first_q(ki))
        def _():
            dk_sc[...] = jnp.zeros(dk_sc.shape, F32)
            dv_sc[...] = jnp.zeros(dv_sc.shape, F32)

        def step(masked):
            qs = (q_ref[...] * scale).astype(BF16)
            pt, dst = _attn_dst(qs, k_ref, v_ref, do_ref, lse_ref, dl_ref, qi, ki, bq, bk, masked)
            dv_sc[...] += lax.dot_general(do_ref[...].astype(BF16), pt.astype(BF16), _NT, preferred_element_type=F32)
            dk_sc[...] += lax.dot_general(dst.astype(BF16), qs, _NT, preferred_element_type=F32)

        _attn_cases(causal, qi, ki, bq, bk, step)

        @pl.when(qi == nq - 1)
        def _():
            dk_ref[0] = dk_sc[...]
            dv_ref[...] = dv_sc[...]

    qmap = lambda hh, p, qi_ref, ki_ref: (hh, qi_ref[p])
    rowmap = lambda hh, p, qi_ref, ki_ref: (hh, 0, qi_ref[p])
    kmap = lambda hh, p, qi_ref, ki_ref: (hh, ki_ref[p], 0)
    vmap = lambda hh, p, qi_ref, ki_ref: (hh, ki_ref[p])
    return _pcall(
        body,
        name=name,
        grid_spec=pltpu.PrefetchScalarGridSpec(
            num_scalar_prefetch=2,
            grid=(h, qtab.shape[0]),
            in_specs=[
                pl.BlockSpec((dq, bq), qmap),
                pl.BlockSpec((1, bk, dq), kmap),
                pl.BlockSpec((dv, bk), vmap),
                pl.BlockSpec((dv, bq), qmap),
                pl.BlockSpec((1, 1, bq), rowmap),
                pl.BlockSpec((1, 1, bq), rowmap),
            ],
            out_specs=[pl.BlockSpec((1, bk, dq), kmap), pl.BlockSpec((dv, bk), vmap)],
            scratch_shapes=[pltpu.VMEM((bk, dq), F32), pltpu.VMEM((dv, bk), F32)],
        ),
        out_shape=[jax.ShapeDtypeStruct((h, sk, dq), F32), jax.ShapeDtypeStruct((h * dv, sk), F32)],
        compiler_params=_params(("parallel", "arbitrary")),
    )(qtab, ktab, qt, k, vt, dot, lse, delta)


@functools.partial(jax.custom_vjp, nondiff_argnums=(3, 4, 5))
def attention(qt, k, vt, scale, causal, name):
    return _attn_fwd_call(qt, k, vt, scale, causal, name + "_f")[0]


def _attention_fwd(qt, k, vt, scale, causal, name):
    ot, lse = _attn_fwd_call(qt, k, vt, scale, causal, name + "_f")
    return ot, (qt, k, vt, ot, lse)


def _attention_bwd(scale, causal, name, res, dot):
    qt, k, vt, ot, lse = res
    h = k.shape[0]
    delta = jnp.sum((dot * ot).reshape(h, ot.shape[0] // h, ot.shape[1]), axis=1, keepdims=True)
    dqt = _attn_dq_call(qt, k, vt, dot, lse, delta, scale, causal, name + "_dq")
    dk, dvt = _attn_dkv_call(qt, k, vt, dot, lse, delta, scale, causal, name + "_dkv")
    return dqt, dk, dvt


attention.defvjp(_attention_fwd, _attention_bwd)


SCAN_ROWS = 8
SCAN_LANES = 256
SCAN_TBLOCK = 512
SCAN_UNROLL = 4


def _scan_tables(lr, li):
    pr, pi = [lr], [li]
    for _ in range(SCAN_ROWS - 1):
        pr, pi = pr + [pr[-1] * lr - pi[-1] * li], pi + [pr[-1] * li + pi[-1] * lr]
    return jnp.concatenate(pr, axis=0), jnp.concatenate(pi, axis=0)


def _scan_call(ur, ui, lr, li, reverse, name, states=None):
    s, n = ur.shape
    tb = _tile(s, SCAN_TBLOCK, 8)
    nt, nl = s // tb, n // SCAN_LANES
    ntile = tb // SCAN_ROWS
    pr, pi = _scan_tables(lr, li)
    if reverse:
        pr, pi = pr[::-1], pi[::-1]
    shape = (SCAN_ROWS, SCAN_LANES)
    with_states = states is not None
    assert reverse or not with_states

    def body(*refs):
        if with_states:
            ur_ref, ui_ref, pr_ref, pi_ref, sr_ref, si_ref, xr_ref, xi_ref, dr_ref, di_ref, cr_sc, ci_sc, ar_sc, ai_sc = refs
        else:
            ur_ref, ui_ref, pr_ref, pi_ref, xr_ref, xi_ref, cr_sc, ci_sc, ar_sc, ai_sc = refs

        @pl.when(pl.program_id(1) == 0)
        def _():
            for sc in (cr_sc, ci_sc, ar_sc, ai_sc):
                sc[...] = jnp.zeros(shape, F32)

        prv, piv = pr_ref[...], pi_ref[...]
        rows = lax.broadcasted_iota(jnp.int32, shape, 0)

        def powers(k):
            r = (SCAN_ROWS - k) if reverse else (k - 1)
            return jnp.broadcast_to(prv[r:r + 1], shape), jnp.broadcast_to(piv[r:r + 1], shape)

        pw = [powers(k) for k in (1, 2, 4)]

        def tile(i, carry):
            cr, ci, acr, aci = carry
            j = (ntile - 1 - i) if reverse else i
            sl = pl.ds(pl.multiple_of(j * SCAN_ROWS, SCAN_ROWS), SCAN_ROWS)
            xr, xi = ur_ref[sl, :], ui_ref[sl, :]
            for (ar, ai), k in zip(pw, (1, 2, 4)):
                if reverse:
                    keep = rows < SCAN_ROWS - k
                    sr = jnp.where(keep, pltpu.roll(xr, SCAN_ROWS - k, 0), 0.0)
                    si = jnp.where(keep, pltpu.roll(xi, SCAN_ROWS - k, 0), 0.0)
                else:
                    keep = rows >= k
                    sr = jnp.where(keep, pltpu.roll(xr, k, 0), 0.0)
                    si = jnp.where(keep, pltpu.roll(xi, k, 0), 0.0)
                xr, xi = xr + (ar * sr - ai * si), xi + (ar * si + ai * sr)
            xr, xi = xr + (prv * cr - piv * ci), xi + (prv * ci + piv * cr)
            xr_ref[sl, :] = xr
            xi_ref[sl, :] = xi
            if with_states:
                last = rows == SCAN_ROWS - 1
                nr = jnp.where(last, cr, pltpu.roll(xr, SCAN_ROWS - 1, 0))
                ni = jnp.where(last, ci, pltpu.roll(xi, SCAN_ROWS - 1, 0))
                sr, si = sr_ref[sl, :], si_ref[sl, :]
                acr, aci = acr + (sr * nr + si * ni), aci + (sr * ni - si * nr)
            e = 0 if reverse else SCAN_ROWS - 1
            return jnp.broadcast_to(xr[e:e + 1], shape), jnp.broadcast_to(xi[e:e + 1], shape), acr, aci

        def tiles(g, carry):
            for u in range(SCAN_UNROLL):
                carry = tile(g * SCAN_UNROLL + u, carry)
            return carry

        assert ntile % SCAN_UNROLL == 0
        cr, ci, acr, aci = lax.fori_loop(0, ntile // SCAN_UNROLL, tiles,
                                         (cr_sc[...], ci_sc[...], ar_sc[...], ai_sc[...]))
        cr_sc[...] = cr
        ci_sc[...] = ci
        if with_states:
            ar_sc[...] = acr
            ai_sc[...] = aci

            @pl.when(pl.program_id(1) == nt - 1)
            def _():
                dr_ref[...] = acr
                di_ref[...] = aci

    tmap = (lambda l, t: (nt - 1 - t, l)) if reverse else (lambda l, t: (t, l))
    blk = pl.BlockSpec((tb, SCAN_LANES), tmap)
    tab = pl.BlockSpec((SCAN_ROWS, SCAN_LANES), lambda l, t: (0, l))
    full = jax.ShapeDtypeStruct((s, n), F32)
    return _pcall(
        body,
        name=name,
        grid=(nl, nt),
        in_specs=[blk, blk, tab, tab] + ([blk, blk] if with_states else []),
        out_specs=[blk, blk] + ([tab, tab] if with_states else []),
        out_shape=[full, full] + ([jax.ShapeDtypeStruct((SCAN_ROWS, n), F32)] * 2 if with_states else []),
        scratch_shapes=[pltpu.VMEM(shape, F32)] * 4,
        compiler_params=_params(("parallel", "arbitrary")),
    )(ur, ui, pr, pi, *(states if with_states else ()))


@functools.partial(jax.custom_vjp, nondiff_argnums=(4,))
def s5_scan(ur, ui, lr, li, name):
    return tuple(_scan_call(ur, ui, lr, li, False, name + "_f"))


def _s5_scan_fwd(ur, ui, lr, li, name):
    xr, xi = _scan_call(ur, ui, lr, li, False, name + "_f")
    return (xr, xi), (xr, xi, lr, li)


def _s5_scan_bwd(name, res, g):
    xr, xi, lr, li = res
    gr, gi, dlr, dli = _scan_call(g[0], g[1], lr, -li, True, name + "_b", states=(xr, xi))
    return gr, gi, jnp.sum(dlr, axis=0, keepdims=True), jnp.sum(dli, axis=0, keepdims=True)


s5_scan.defvjp(_s5_scan_fwd, _s5_scan_bwd)


def _running_sum(x, reverse):
    n = x.shape[0]
    rows = lax.broadcasted_iota(jnp.int32, x.shape, 0)
    k = 1
    while k < n:
        if reverse:
            x = x + jnp.where(rows < n - k, pltpu.roll(x, n - k, 0), 0.0)
        else:
            x = x + jnp.where(rows >= k, pltpu.roll(x, k, 0), 0.0)
        k *= 2
    return x


@jax.custom_vjp
def cumsum_rows(x):
    return _running_sum(x, False)


cumsum_rows.defvjp(lambda x: (_running_sum(x, False), None), lambda _, g: (_running_sum(g, True),))


def _hg_chunk(HG_SUB, q, k, v, lf, st):
    c = HG_CHUNK
    b = cumsum_rows(lf)
    qb = q * jnp.exp(b)
    o = lax.dot_general(qb.astype(BF16), st.astype(BF16), (((1,), (1,)), ((), ())), preferred_element_type=F32)
    nsub = c // HG_SUB
    srow = lax.broadcasted_iota(jnp.int32, (HG_SUB, HG_SUB, 1), 0)
    scol = lax.broadcasted_iota(jnp.int32, (HG_SUB, HG_SUB, 1), 1)
    smask = scol <= srow
    outs = []
    for i in range(nsub):
        lo = i * HG_SUB
        bi, qi, ki = b[lo:lo + HG_SUB], q[lo:lo + HG_SUB], k[lo:lo + HG_SUB]
        diff = jnp.where(smask, bi[:, None, :] - bi[None, :, :], 0.0)
        e = jnp.where(smask, jnp.exp(diff), 0.0)
        a_ii = jnp.sum(qi[:, None, :] * ki[None, :, :] * e, axis=-1)
        oi = jnp.dot(a_ii.astype(BF16), v[lo:lo + HG_SUB].astype(BF16), preferred_element_type=F32)
        if i > 0:
            r = b[lo - 1:lo]
            qt = qi * jnp.exp(bi - r)
            kt = k[:lo] * jnp.exp(r - b[:lo])
            a_ij = lax.dot_general(qt.astype(BF16), kt.astype(BF16), (((1,), (1,)), ((), ())),
                                   preferred_element_type=F32)
            oi = oi + jnp.dot(a_ij.astype(BF16), v[:lo].astype(BF16), preferred_element_type=F32)
        outs.append(oi)
    o = o + jnp.concatenate(outs, axis=0)
    bl = b[c - 1:c]
    kd = k * jnp.exp(bl - b)
    st_new = st * jnp.exp(bl) + lax.dot_general(v.astype(BF16), kd.astype(BF16), (((0,), (0,)), ((), ())),
                                                preferred_element_type=F32)
    return o, st_new


def _hg_fwd_call(q, k, v, lf, name):
    s, w = q.shape
    nc = s // HG_CHUNK
    blk = pl.BlockSpec((HG_CHUNK, w), lambda i: (i, 0))

    def body(q_ref, k_ref, v_ref, lf_ref, o_ref, st_ref, st_sc):
        @pl.when(pl.program_id(0) == 0)
        def _():
            st_sc[...] = jnp.zeros(st_sc.shape, F32)

        for hh in range(HG_HEADS):
            cs = slice(hh * HG_DK, (hh + 1) * HG_DK)
            st = st_sc[hh]
            st_ref[0, hh] = st
            o, st_new = _hg_chunk(HG_SUB_FWD, q_ref[:, cs], k_ref[:, cs], v_ref[:, cs], lf_ref[:, cs], st)
            o_ref[:, cs] = o
            st_sc[hh] = st_new

    return _pcall(
        body,
        name=name,
        grid=(nc,),
        in_specs=[blk, blk, blk, blk],
        out_specs=[blk, pl.BlockSpec((1, HG_HEADS, HG_DV, HG_DK), lambda i: (i, 0, 0, 0))],
        out_shape=[jax.ShapeDtypeStruct((s, w), F32), jax.ShapeDtypeStruct((nc, HG_HEADS, HG_DV, HG_DK), F32)],
        scratch_shapes=[pltpu.VMEM((HG_HEADS, HG_DV, HG_DK), F32)],
        compiler_params=_params(("arbitrary",)),
    )(q, k, v, lf)


def _hg_bwd_call(q, k, v, lf, sts, do, name):
    s, w = q.shape
    nc = s // HG_CHUNK
    blk = pl.BlockSpec((HG_CHUNK, w), lambda i: (nc - 1 - i, 0))

    def body(q_ref, k_ref, v_ref, lf_ref, st_ref, do_ref, dq_ref, dk_ref, dv_ref, dlf_ref, dst_sc):
        @pl.when(pl.program_id(0) == 0)
        def _():
            dst_sc[...] = jnp.zeros(dst_sc.shape, F32)

        def head(hh, carry):
            cs = pl.ds(pl.multiple_of(hh * HG_DK, HG_DK), HG_DK)
            _, vjp = jax.vjp(functools.partial(_hg_chunk, HG_SUB_BWD), q_ref[:, cs], k_ref[:, cs], v_ref[:, cs],
                             lf_ref[:, cs], st_ref[0, hh])
            dq, dk, dv, dlf, dst = vjp((do_ref[:, cs], dst_sc[hh]))
            dq_ref[:, cs] = dq
            dk_ref[:, cs] = dk
            dv_ref[:, cs] = dv
            dlf_ref[:, cs] = dlf
            dst_sc[hh] = dst
            return carry

        lax.fori_loop(0, HG_HEADS, head, 0)

    out = jax.ShapeDtypeStruct((s, w), F32)
    return _pcall(
        body,
        name=name,
        grid=(nc,),
        in_specs=[blk, blk, blk, blk, pl.BlockSpec((1, HG_HEADS, HG_DV, HG_DK), lambda i: (nc - 1 - i, 0, 0, 0)), blk],
        out_specs=[blk, blk, blk, blk],
        out_shape=[out, out, out, out],
        scratch_shapes=[pltpu.VMEM((HG_HEADS, HG_DV, HG_DK), F32)],
        compiler_params=_params(("arbitrary",)),
    )(q, k, v, lf, sts, do)


@functools.partial(jax.custom_vjp, nondiff_argnums=(4,))
def hgrn_core(q, k, v, lf, name):
    return _hg_fwd_call(q, k, v, lf, name + "_f")[0]


def _hgrn_core_fwd(q, k, v, lf, name):
    o, sts = _hg_fwd_call(q, k, v, lf, name + "_f")
    return o, (q, k, v, lf, sts)


def _hgrn_core_bwd(name, res, do):
    q, k, v, lf, sts = res
    return tuple(_hg_bwd_call(q, k, v, lf, sts, do, name + "_b"))


hgrn_core.defvjp(_hgrn_core_fwd, _hgrn_core_bwd)


def _adamw_call(w, g, m, v, name):
    rows, cols = w.shape
    tr = _tile(rows, 256, 8)
    c1 = 1.0 / (1.0 - ADAM_B1 ** ADAM_STEP)
    c2 = 1.0 / (1.0 - ADAM_B2 ** ADAM_STEP)

    def body(w_ref, g_ref, m_ref, v_ref, d_ref, mo_ref, vo_ref):
        gv = g_ref[...]
        mn = ADAM_B1 * m_ref[...] + (1.0 - ADAM_B1) * gv
        vn = ADAM_B2 * v_ref[...] + (1.0 - ADAM_B2) * (gv * gv)
        d_ref[...] = -ADAM_LR * ((mn * c1) / (jnp.sqrt(vn * c2) + ADAM_EPS) + ADAM_WD * w_ref[...])
        mo_ref[...] = mn
        vo_ref[...] = vn

    blk = pl.BlockSpec((tr, cols), lambda i: (i, 0))
    out = jax.ShapeDtypeStruct((rows, cols), F32)
    return _pcall(
        body,
        name=name,
        grid=(rows // tr,),
        in_specs=[blk, blk, blk, blk],
        out_specs=[blk, blk, blk],
        out_shape=[out, out, out],
        compiler_params=_params(("parallel",)),
    )(w, g, m, v)


ANY = pl.BlockSpec(memory_space=pl.ANY)


def _place():
    return lax.axis_index("x"), lax.axis_index("y"), lax.axis_index("c")


def _other_chips(x, y):
    return [(1 - x, y), (x, 1 - y), (1 - x, 1 - y)]


def _gather_weights(shards):
    nw = len(shards)

    def body(*refs):
        w_refs, g_refs, (send_sems, recv_sems) = refs[:nw], refs[nw:2 * nw], refs[2 * nw:]
        x, y, c = _place()
        myq = 2 * x + y
        sibling = (x, y, 1 - c)
        chips = _other_chips(x, y)

        def copy(k, src, dst, to):
            return pltpu.make_async_remote_copy(src_ref=src, dst_ref=dst, send_sem=send_sems.at[k],
                                                recv_sem=recv_sems.at[k], device_id=to, device_id_type=MESH)

        first = []
        for i in range(nw):
            for j, (px, py) in enumerate(chips):
                first.append(copy(6 * i + j, w_refs[i].at[c], g_refs[i].at[c, myq], (px, py, c)))
                first[-1].start()
        passed = []
        for i in range(nw):
            for j, (px, py) in enumerate(chips):
                q = 2 * px + py
                copy(6 * i + j, w_refs[i].at[c], g_refs[i].at[c, q], (px, py, c)).wait_recv()
                passed.append(copy(6 * i + 3 + j, g_refs[i].at[c, q], g_refs[i].at[c, q], sibling))
                passed[-1].start()
        for i in range(nw):
            for j, (px, py) in enumerate(chips):
                q = 2 * px + py
                copy(6 * i + 3 + j, g_refs[i].at[1 - c, q], g_refs[i].at[1 - c, q], sibling).wait_recv()
        for cp in first + passed:
            cp.wait_send()

    got = _pcall(
        body,
        name="gather_weights",
        in_specs=[ANY] * nw,
        out_specs=[ANY] * nw,
        out_shape=[jax.ShapeDtypeStruct((s.shape[0], N_CHIPS) + s.shape[1:], s.dtype) for s in shards],
        scratch_shapes=[pltpu.SemaphoreType.DMA((6 * nw,)), pltpu.SemaphoreType.DMA((6 * nw,))],
    )(*shards)
    my_chip = 2 * lax.axis_index("x") + lax.axis_index("y")
    return [lax.dynamic_update_slice(g, s[:, None], (0, my_chip, 0, 0)) for g, s in zip(got, shards)]


def _swap_halves_to_sibling(grs):
    nw = len(grs)

    def body(*refs):
        g_refs, t_refs, (send_sems, recv_sems) = refs[:nw], refs[nw:2 * nw], refs[2 * nw:]
        x, y, c = _place()
        cps = []
        for i in range(nw):
            hh = grs[i].shape[1] // 2
            cps.append(pltpu.make_async_remote_copy(
                src_ref=g_refs[i].at[:, pl.ds((1 - c) * hh, hh)], dst_ref=t_refs[i], send_sem=send_sems.at[i],
                recv_sem=recv_sems.at[i], device_id=(x, y, 1 - c), device_id_type=MESH))
            cps[-1].start()
        for cp in cps:
            cp.wait()

    return _pcall(
        body,
        name="grad_d2d",
        in_specs=[ANY] * nw,
        out_specs=[ANY] * nw,
        out_shape=[jax.ShapeDtypeStruct((g.shape[0], g.shape[1] // 2, g.shape[2]), g.dtype) for g in grs],
        scratch_shapes=[pltpu.SemaphoreType.DMA((nw,)), pltpu.SemaphoreType.DMA((nw,))],
    )(*grs)


def _add_half(gr, t1, place, name):
    nq, r, wd = gr.shape
    hh = r // 2
    tr = _tile(hh, 512, 16)
    nb = hh // tr

    def body(p_ref, g_ref, t_ref, o_ref):
        o_ref[...] = (g_ref[...] + t_ref[...]).astype(BF16)

    return _pcall(
        body,
        name=name,
        grid_spec=pltpu.PrefetchScalarGridSpec(
            num_scalar_prefetch=1,
            grid=(nq, nb),
            in_specs=[
                pl.BlockSpec((1, tr, wd), lambda q, i, p_ref: (q, p_ref[0] * nb + i, 0)),
                pl.BlockSpec((1, tr, wd), lambda q, i, p_ref: (q, i, 0)),
            ],
            out_specs=pl.BlockSpec((1, tr, wd), lambda q, i, p_ref: (q, i, 0)),
        ),
        out_shape=jax.ShapeDtypeStruct((nq, hh, wd), BF16),
        compiler_params=_params(("parallel", "parallel")),
    )(place, gr, t1)


def _scatter_partials(ps):
    nw = len(ps)

    def body(*refs):
        p_refs, t_refs, (send_sems, recv_sems) = refs[:nw], refs[nw:2 * nw], refs[2 * nw:]
        x, y, c = _place()
        cps = []
        for i in range(nw):
            for j, (px, py) in enumerate(_other_chips(x, y)):
                cps.append(pltpu.make_async_remote_copy(
                    src_ref=p_refs[i].at[2 * px + py], dst_ref=t_refs[i].at[j], send_sem=send_sems.at[3 * i + j],
                    recv_sem=recv_sems.at[3 * i + j], device_id=(px, py, c), device_id_type=MESH))
                cps[-1].start()
        for cp in cps:
            cp.wait()

    return _pcall(
        body,
        name="grad_ici",
        in_specs=[ANY] * nw,
        out_specs=[ANY] * nw,
        out_shape=[jax.ShapeDtypeStruct((3,) + p.shape[1:], p.dtype) for p in ps],
        scratch_shapes=[pltpu.SemaphoreType.DMA((3 * nw,)), pltpu.SemaphoreType.DMA((3 * nw,))],
    )(*ps)


def _add_partials(gr, t1, t2, place, name):
    nq, r, wd = gr.shape
    hh = r // 2
    tr = _tile(hh, 512, 16)
    nb = hh // tr

    def body(p_ref, g_ref, s_ref, t_ref, o_ref):
        o_ref[...] = (((g_ref[0] + s_ref[0]) + t_ref[0].astype(F32)) + t_ref[1].astype(F32)) + t_ref[2].astype(F32)

    return _pcall(
        body,
        name=name,
        grid_spec=pltpu.PrefetchScalarGridSpec(
            num_scalar_prefetch=1,
            grid=(nb,),
            in_specs=[
                pl.BlockSpec((1, tr, wd), lambda i, p_ref: (p_ref[1], p_ref[0] * nb + i, 0)),
                pl.BlockSpec((1, tr, wd), lambda i, p_ref: (p_ref[1], i, 0)),
                pl.BlockSpec((3, tr, wd), lambda i, p_ref: (0, i, 0)),
            ],
            out_specs=pl.BlockSpec((tr, wd), lambda i, p_ref: (i, 0)),
        ),
        out_shape=jax.ShapeDtypeStruct((hh, wd), F32),
        compiler_params=_params(("parallel",)),
    )(place, gr, t1, t2)


def _join_halves(halves):
    nw = len(halves)

    def body(*refs):
        h_refs, f_refs, (send_sems, recv_sems) = refs[:nw], refs[nw:2 * nw], refs[2 * nw:]
        x, y, c = _place()

        def copy(i, rows):
            hh = halves[i].shape[0]
            return pltpu.make_async_remote_copy(src_ref=h_refs[i], dst_ref=f_refs[i].at[pl.ds(rows * hh, hh)],
                                                send_sem=send_sems.at[i], recv_sem=recv_sems.at[i],
                                                device_id=(x, y, 1 - c), device_id_type=MESH)

        for i in range(nw):
            copy(i, c).start()
        for i in range(nw):
            copy(i, c).wait_send()
            copy(i, 1 - c).wait_recv()

    got = _pcall(
        body,
        name="grad_join",
        in_specs=[ANY] * nw,
        out_specs=[ANY] * nw,
        out_shape=[jax.ShapeDtypeStruct((2 * h.shape[0], h.shape[1]), h.dtype) for h in halves],
        scratch_shapes=[pltpu.SemaphoreType.DMA((nw,)), pltpu.SemaphoreType.DMA((nw,))],
    )(*halves)
    c = lax.axis_index("c")
    return [lax.dynamic_update_slice(g, h, (c * h.shape[0], 0)) for g, h in zip(got, halves)]


N_DEV = 8


def _allreduce_small(buf):
    rows, wd = buf.shape
    n = rows // N_DEV

    def body(x_ref, o_ref, stage_ref, send_sems, recv_sems):
        x, y, c = _place()
        me = 4 * x + 2 * y + c

        def peer(k):
            return (x ^ (k >> 2), y ^ ((k >> 1) & 1), c ^ (k & 1))

        def chunk(ref, d):
            return ref.at[pl.ds(pl.multiple_of(d * n, 8), n)]

        first = []
        for k in range(1, N_DEV):
            px, py, pc = peer(k)
            cp = pltpu.make_async_remote_copy(src_ref=chunk(x_ref, 4 * px + 2 * py + pc), dst_ref=stage_ref.at[k],
                                              send_sem=send_sems.at[k - 1], recv_sem=recv_sems.at[k - 1],
                                              device_id=(px, py, pc), device_id_type=MESH)
            cp.start()
            first.append(cp)
        acc = chunk(x_ref, me)[...]
        for k in range(1, N_DEV):
            first[k - 1].wait_recv()
            acc = acc + stage_ref[k]
        chunk(o_ref, me)[...] = acc
        second = []
        for k in range(1, N_DEV):
            cp = pltpu.make_async_remote_copy(src_ref=chunk(o_ref, me), dst_ref=chunk(o_ref, me),
                                              send_sem=send_sems.at[6 + k], recv_sem=recv_sems.at[6 + k],
                                              device_id=peer(k), device_id_type=MESH)
            cp.start()
            second.append(cp)
        for k in range(1, N_DEV):
            px, py, pc = peer(k)
            them = 4 * px + 2 * py + pc
            pltpu.make_async_remote_copy(src_ref=chunk(o_ref, them), dst_ref=chunk(o_ref, them),
                                         send_sem=send_sems.at[6 + k], recv_sem=recv_sems.at[6 + k],
                                         device_id=(px, py, pc), device_id_type=MESH).wait_recv()
        for cp in first + second:
            cp.wait_send()

    vmem = pl.BlockSpec(memory_space=pltpu.VMEM)
    return _pcall(
        body,
        name="allreduce_small",
        in_specs=[vmem],
        out_specs=vmem,
        out_shape=jax.ShapeDtypeStruct((rows, wd), F32),
        scratch_shapes=[pltpu.VMEM((N_DEV, n, wd), F32), pltpu.SemaphoreType.DMA((14,)),
                        pltpu.SemaphoreType.DMA((14,))],
    )(buf)


@functools.partial(jax.custom_vjp, nondiff_argnums=(1,))
def split_cols(z, spec):
    outs = []
    for segs in spec[0]:
        parts = [z[:, s:s + n] for s, n in segs]
        outs.append(parts[0] if len(parts) == 1 else jnp.concatenate(parts, axis=1))
    return tuple(outs)


def _split_cols_fwd(z, spec):
    return split_cols(z, spec), None


def _split_cols_bwd(spec, _, gs):
    pieces, width = spec
    segs = []
    for p, piece in enumerate(pieces):
        off = 0
        for s, n in piece:
            segs.append((s, n, p, off))
            off += n
    rows, parts, pos = gs[0].shape[0], [], 0
    for s, n, p, off in sorted(segs):
        if s > pos:
            parts.append(jnp.zeros((rows, s - pos), gs[0].dtype))
        parts.append(gs[p] if (off == 0 and n == gs[p].shape[1]) else gs[p][:, off:off + n])
        pos = s + n
    if pos < width:
        parts.append(jnp.zeros((rows, width - pos), gs[0].dtype))
    return (jnp.concatenate(parts, axis=1),)


split_cols.defvjp(_split_cols_fwd, _split_cols_bwd)


def _contiguous(sizes):
    pieces, off = [], 0
    for n in sizes:
        pieces.append(((off, n),))
        off += n
    return tuple(pieces), off


def _rope(t, cos, sin):
    half = t.shape[-1] // 2
    t1, t2 = t[..., :half], t[..., half:]
    return jnp.concatenate([t1 * cos - t2 * sin, t2 * cos + t1 * sin], axis=-1)


def _s5_mixer(u, sp, l, w16, toks):
    g, p_, h = SSM_GROUPS, SSM_STATE, SSM_GROUP_CH
    lam_re, lam_im = sp["ssm_lam_re", l], sp["ssm_lam_im", l]
    step = jnp.exp(sp["ssm_log_step", l])[:, None]
    mag, ang = jnp.exp(lam_re * step), lam_im * step
    lbr, lbi = mag * jnp.cos(ang), mag * jnp.sin(ang)
    den = lam_re * lam_re + lam_im * lam_im
    nr, ni = lbr - 1.0, lbi
    qr, qi = (nr * lam_re + ni * lam_im) / den, (ni * lam_re - nr * lam_im) / den
    b_re, b_im = sp["ssm_b_re", l], sp["ssm_b_im", l]
    bbr = qr[..., None] * b_re - qi[..., None] * b_im
    bbi = qr[..., None] * b_im + qi[..., None] * b_re
    nb = g // SSM_PACK
    eye = jnp.eye(SSM_PACK, dtype=F32)

    def in_blocks(t):
        return jnp.einsum("jgph,gk->jghkp", t.reshape(nb, SSM_PACK, p_, h), eye).reshape(nb, SSM_PACK * h, SSM_PACK * p_)

    def out_blocks(t):
        return jnp.einsum("jghp,gk->jgpkh", t.reshape(nb, SSM_PACK, h, p_), eye).reshape(nb, SSM_PACK * p_, SSM_PACK * h)

    nm = "ssm%d" % l
    ur = bd_matmul(u, in_blocks(bbr), nm + "_bur")
    ui = bd_matmul(u, in_blocks(bbi), nm + "_bui")
    xr, xi = s5_scan(ur, ui, lbr.reshape(1, g * p_), lbi.reshape(1, g * p_), nm + "_scan")
    y = (bd_matmul(xr, out_blocks(sp["ssm_c_re", l]), nm + "_cr") + bd_matmul(xi, out_blocks(-sp["ssm_c_im", l]), nm + "_ci")
         + sp["ssm_d", l].reshape(1, g * h) * u)
    y = jax.nn.gelu(y)
    zz = matmul_w(y, w16["ssm_w_glu", l], toks["ssm_w_glu", l], nm + "_glu")
    z_out, z_gate = split_cols(zz, _contiguous((D_MODEL, D_MODEL)))
    return z_out * jax.nn.sigmoid(z_gate)


def _mla_mixer(q_lat, kv_lat, k_rope, cos, sin, sp, l, w16, toks):
    s = q_lat.shape[0]
    nm = "mla%d" % l
    dqk = MLA_NOPE + MLA_ROPE
    qt = matmul_w_t(rmsnorm(q_lat, sp["mla_q_norm", l], nm + "_qn"), w16["mla_w_uq", l], toks["mla_w_uq", l],
                    nm + "_uq").reshape(MLA_HEADS, dqk, s)
    half = MLA_ROPE // 2
    q1, q2 = qt[:, MLA_NOPE:MLA_NOPE + half], qt[:, MLA_NOPE + half:]
    ct, st = cos.T[None], sin.T[None]
    qt = jnp.concatenate([qt[:, :MLA_NOPE], q1 * ct - q2 * st, q2 * ct + q1 * st], axis=1).reshape(MLA_HEADS * dqk, s)
    kvt = matmul_w_t(rmsnorm(kv_lat, sp["mla_kv_norm", l], nm + "_kvn"), w16["mla_w_ukv", l], toks["mla_w_ukv", l],
                     nm + "_ukv").reshape(MLA_HEADS, MLA_NOPE + MLA_V, s)
    k_pe = _rope(k_rope, cos, sin)
    k = jnp.concatenate([kvt[:, :MLA_NOPE].transpose(0, 2, 1),
                         jnp.broadcast_to(k_pe[None], (MLA_HEADS, s, MLA_ROPE))], axis=-1)
    vt = kvt[:, MLA_NOPE:].reshape(MLA_HEADS * MLA_V, s)
    ot = attention(qt, k, vt, 1.0 / math.sqrt(dqk), True, nm + "_att")
    return matmul_w_at(ot, w16["mla_w_o", l], toks["mla_w_o", l], nm + "_o")


def _hgrn_mixer(q, f_logit, i_in, g, lb, sp, l, w16, toks):
    s = q.shape[0]
    nm = "hg%d" % l
    f = lb + (1.0 - lb) * jax.nn.sigmoid(f_logit)
    o = hgrn_core(jax.nn.silu(q), 1.0 - f, i_in, jnp.log(f), nm + "_core")
    o = rmsnorm(o.reshape(s * HG_HEADS, HG_DV), sp["hg_g_norm", l], nm + "_gn").reshape(s, HG_HEADS * HG_DV)
    return matmul_w(o * jax.nn.silu(g), w16["hg_w_o", l], toks["hg_w_o", l], nm + "_o")


def _cross_attention(h, mem_n, l, w16, toks):
    s, m = h.shape[0], mem_n.shape[0]
    nm = "xat%d" % l
    xw = X_HEADS * X_HEAD_DIM
    qt = matmul_w_t(h, w16["x_w_q", l], toks["x_w_q", l], nm + "_q")
    kv = matmul_w(mem_n, w16["x_w_kv", l], toks["x_w_kv", l], nm + "_kv")
    k, v = split_cols(kv, _contiguous((xw, xw)))
    k = k.reshape(m, X_HEADS, X_HEAD_DIM).transpose(1, 0, 2)
    ot = attention(qt, k, v.T, 1.0 / math.sqrt(X_HEAD_DIM), False, nm + "_att")
    return matmul_w_at(ot, w16["x_w_o", l], toks["x_w_o", l], nm + "_o")


def _in_spec():
    hw = HG_HEADS * HG_DK
    sizes = (SSM_WIDTH, MLA_Q_RANK, MLA_KV_RANK, MLA_ROPE, hw, hw, hw, hw, 3 * D_MODEL)
    pieces, lo = [], 0
    for n in sizes:
        segs, c = [], lo
        while c < lo + n:
            end = min(lo + n, (c // D_IN_SHARD + 1) * D_IN_SHARD)
            segs.append((c + (D_IN_BLOCK - D_IN_SHARD) * (c // D_IN_SHARD), end - c))
            c = end
        pieces.append(tuple(segs))
        lo += n
    assert lo == D_IN
    return tuple(pieces), N_CHIPS * D_IN_BLOCK


def _layer(x, mem, cos, sin, lb, l, sp, w16, toks):
    s = x.shape[0]
    nm = "l%d" % l
    h = rmsnorm(x, sp["norm_mix", l], nm + "_nmix")
    z = matmul_w(h, w16["w_in", l], toks["w_in", l], nm + "_in")
    u, q_lat, kv_lat, k_rope, hg_q, hg_f, hg_i, hg_g, gate_logits = split_cols(z, _in_spec())
    g_ssm, g_mla, g_hg = split_cols(jax.nn.sigmoid(gate_logits), _contiguous((D_MODEL,) * 3))
    y_ssm = _s5_mixer(u, sp, l, w16, toks)
    y_mla = _mla_mixer(q_lat, kv_lat, k_rope, cos, sin, sp, l, w16, toks)
    y_hg = _hgrn_mixer(hg_q, hg_f, hg_i, hg_g, lb, sp, l, w16, toks)
    merged = g_ssm * y_ssm + g_mla * y_mla + g_hg * y_hg
    x = x + matmul_w(merged, w16["w_out", l], toks["w_out", l], nm + "_out")
    hc = rmsnorm(x, sp["norm_cross", l], nm + "_ncross")
    mem_n = rmsnorm(mem, sp["norm_mem", l], nm + "_nmem")
    x = x + _cross_attention(hc, mem_n, l, w16, toks)
    hf = rmsnorm(x, sp["norm_ffn", l], nm + "_nffn")
    gu = matmul_w(hf, w16["ffn_w_gate_up", l], toks["ffn_w_gate_up", l], nm + "_gu")
    gt, up = split_cols(gu, _contiguous((D_FF, D_FF)))
    return x + matmul_w(jax.nn.silu(gt) * up, w16["ffn_w_down", l], toks["ffn_w_down", l], nm + "_down")


def _loss_fn(sp, toks, x, w16, mem, pos, target):
    half = MLA_ROPE // 2
    inv_freq = ROPE_THETA ** (-jnp.arange(half, dtype=F32) / half)
    ang = pos.astype(F32)[:, None] * inv_freq
    cos, sin = jnp.cos(ang), jnp.sin(ang)
    lb_p = jax.nn.softmax(sp["hg_lb", ALL], axis=0)
    lower = jnp.cumsum(lb_p, axis=0) - lb_p[0:1]
    for l in range(DEPTH):
        x = _layer(x, mem, cos, sin, lower[l], l, sp, w16, toks)
    y = rmsnorm(x, sp["norm_final", ALL], "nfinal")
    return 0.5 * jnp.sum(jnp.mean(jnp.square(y - target), axis=-1))


WEIGHTS = ["norm_mix", "w_in", "ssm_lam_re", "ssm_lam_im", "ssm_b_re", "ssm_b_im", "ssm_c_re", "ssm_c_im", "ssm_d",
           "ssm_log_step", "ssm_w_glu", "mla_q_norm", "mla_kv_norm", "mla_w_uq", "mla_w_ukv", "mla_w_o", "hg_lb",
           "hg_g_norm", "hg_w_o", "w_out", "norm_cross", "norm_mem", "x_w_q", "x_w_kv", "x_w_o", "norm_ffn",
           "ffn_w_gate_up", "ffn_w_down", "norm_final"]
INPUTS = ["x", "mem", "positions"] + WEIGHTS + ["loss_target"] + ["m_" + n for n in WEIGHTS] + ["v_" + n for n in WEIGHTS]
BIG = [("w_in", 2), ("ssm_w_glu", 2), ("mla_w_uq", 2), ("mla_w_ukv", 2), ("mla_w_o", 2), ("hg_w_o", 2), ("w_out", 1),
       ("x_w_q", 1), ("x_w_kv", 1), ("x_w_o", 2), ("ffn_w_gate_up", 2), ("ffn_w_down", 1)]
SMALL = [n for n in WEIGHTS if n not in dict(BIG)]
WHOLE = ("hg_lb", "norm_final")
ALL = -1
SMALL_W = 128
BLOCKED = ("w_in", "ffn_w_gate_up")


def _pack(arrs, rows, width, dtype):
    flat = jnp.concatenate([a.astype(dtype).reshape(-1) for a in arrs])
    return jnp.pad(flat, (0, rows * width - flat.shape[0])).reshape(rows, width)


def _unpack(buf, like):
    flat = buf.reshape(-1)
    out, off = [], 0
    for a in like:
        n = math.prod(a.shape)
        out.append(flat[off:off + n].reshape(a.shape))
        off += n
    return out


def kernel(
    x, mem, positions, norm_mix, w_in, ssm_lam_re, ssm_lam_im, ssm_b_re, ssm_b_im, ssm_c_re, ssm_c_im, ssm_d,
    ssm_log_step, ssm_w_glu, mla_q_norm, mla_kv_norm, mla_w_uq, mla_w_ukv, mla_w_o, hg_lb, hg_g_norm, hg_w_o,
    w_out, norm_cross, norm_mem, x_w_q, x_w_kv, x_w_o, norm_ffn, ffn_w_gate_up, ffn_w_down, norm_final,
    loss_target, m_norm_mix, m_w_in, m_ssm_lam_re, m_ssm_lam_im, m_ssm_b_re, m_ssm_b_im, m_ssm_c_re, m_ssm_c_im,
    m_ssm_d, m_ssm_log_step, m_ssm_w_glu, m_mla_q_norm, m_mla_kv_norm, m_mla_w_uq, m_mla_w_ukv, m_mla_w_o,
    m_hg_lb, m_hg_g_norm, m_hg_w_o, m_w_out, m_norm_cross, m_norm_mem, m_x_w_q, m_x_w_kv, m_x_w_o, m_norm_ffn,
    m_ffn_w_gate_up, m_ffn_w_down, m_norm_final, v_norm_mix, v_w_in, v_ssm_lam_re, v_ssm_lam_im, v_ssm_b_re,
    v_ssm_b_im, v_ssm_c_re, v_ssm_c_im, v_ssm_d, v_ssm_log_step, v_ssm_w_glu, v_mla_q_norm, v_mla_kv_norm,
    v_mla_w_uq, v_mla_w_ukv, v_mla_w_o, v_hg_lb, v_hg_g_norm, v_hg_w_o, v_w_out, v_norm_cross, v_norm_mem,
    v_x_w_q, v_x_w_kv, v_x_w_o, v_norm_ffn, v_ffn_w_gate_up, v_ffn_w_down, v_norm_final):
    given = locals()
    a = {n: given[n] for n in INPUTS}
    x, mem, pos, target = a["x"][0], a["mem"][0], a["positions"][0], a["loss_target"][0]
    place = jnp.stack([lax.axis_index("c"), 2 * lax.axis_index("x") + lax.axis_index("y")]).astype(jnp.int32)

    def shard16(n):
        w = a[n].astype(BF16)
        if n == "w_in":
            w = jnp.pad(w, ((0, 0), (0, 0), (0, D_IN_BLOCK - D_IN_SHARD)))
        return w

    gathered = _gather_weights([shard16(n) for n, _ in BIG])
    w16, toks = {}, {}
    for (n, ax), got in zip(BIG, gathered):
        for l in range(DEPTH):
            if n in BLOCKED:
                full = got[l]
            elif ax == 2:
                full = jnp.concatenate([got[l, q] for q in range(N_CHIPS)], axis=1)
            else:
                full = got[l].reshape(N_CHIPS * got.shape[2], got.shape[3])
            w16[n, l] = full
            toks[n, l] = jnp.zeros(full.shape, F32)
    sp = {}
    for n in SMALL:
        if n in WHOLE:
            sp[n, ALL] = a[n]
        else:
            for l in range(DEPTH):
                sp[n, l] = a[n][l]

    loss, (g_sp, g_tok, g_x) = jax.value_and_grad(_loss_fn, argnums=(0, 1, 2))(sp, toks, x, w16, mem, pos, target)
    g_small = {n: g_sp[n, ALL] if n in WHOLE else jnp.stack([g_sp[n, l] for l in range(DEPTH)]) for n in SMALL}

    grads, names = [], []
    for n, ax in BIG:
        for l in range(DEPTH):
            g = g_tok[n, l]
            if n in BLOCKED:
                grads.append(g)
            elif ax == 2:
                grads.append(g.reshape(g.shape[0], N_CHIPS, g.shape[1] // N_CHIPS).transpose(1, 0, 2))
            else:
                grads.append(g.reshape(N_CHIPS, g.shape[0] // N_CHIPS, g.shape[1]))
            names.append("%s%d" % (n, l))
    from_sibling = _swap_halves_to_sibling(grads)
    chip_sums = [_add_half(g, t, place, "grad_add_half_" + n) for n, g, t in zip(names, grads, from_sibling)]
    from_chips = _scatter_partials(chip_sums)
    totals = [_add_partials(g, t1, t2, place, "grad_add_partials_" + n)
              for n, g, t1, t2 in zip(names, grads, from_sibling, from_chips)]
    joined = _join_halves(totals)
    g_big = {n: jnp.stack([joined[DEPTH * i + l][:, :a[n].shape[2]] for l in range(DEPTH)])
             for i, (n, _) in enumerate(BIG)}

    small_like = [a[n] for n in SMALL] + [jnp.zeros((1,), F32)]
    n_small = sum(math.prod(t.shape) for t in small_like)
    srows = -(-n_small // (SMALL_W * 8 * N_DEV)) * 8 * N_DEV
    reduced = _allreduce_small(_pack([g_small[n] for n in SMALL] + [loss.reshape(1)], srows, SMALL_W, F32))
    red = _unpack(reduced, small_like)
    g_sm = dict(zip(SMALL, red[:-1]))
    loss = red[-1].reshape(())

    out_g, out_d, out_m, out_v = {}, {}, {}, {}
    for n, _ in BIG:
        shp = a[n].shape
        two = lambda t: t.reshape(shp[0] * shp[1], shp[2])
        d, m, v = _adamw_call(two(a[n]), two(g_big[n]), two(a["m_" + n]), two(a["v_" + n]), "adamw_" + n)
        out_g[n], out_d[n], out_m[n], out_v[n] = g_big[n], d.reshape(shp), m.reshape(shp), v.reshape(shp)
    small_arrs = [a[n] for n in SMALL]
    d, m, v = _adamw_call(
        _pack(small_arrs, srows, SMALL_W, F32), reduced,
        _pack([a["m_" + n] for n in SMALL], srows, SMALL_W, F32),
        _pack([a["v_" + n] for n in SMALL], srows, SMALL_W, F32), "adamw_small")
    for n, gg, dd, mm_, vv in zip(SMALL, red[:-1], _unpack(d, small_arrs), _unpack(m, small_arrs), _unpack(v, small_arrs)):
        out_g[n], out_d[n], out_m[n], out_v[n] = gg, dd, mm_, vv

    return (loss, g_x[None], *[out_g[n] for n in WEIGHTS], *[out_d[n] for n in WEIGHTS],
            *[out_m[n] for n in WEIGHTS], *[out_v[n] for n in WEIGHTS])
```

```python
import functools
import math

import jax
import jax.numpy as jnp
from jax import lax
from jax.experimental import pallas as pl
from jax.experimental.pallas import tpu as pltpu

F32 = jnp.float32
BF16 = jnp.bfloat16
MESH = pl.DeviceIdType.MESH

D_MODEL = 1024
DEPTH = 2
RMS_EPS = 1e-6
SSM_GROUPS, SSM_GROUP_CH, SSM_STATE = 32, 16, 64
SSM_WIDTH = SSM_GROUPS * SSM_GROUP_CH
SSM_LANES = SSM_GROUPS * SSM_STATE
SSM_PACK = 8
MLA_HEADS, MLA_Q_RANK, MLA_KV_RANK, MLA_NOPE, MLA_ROPE, MLA_V = 8, 512, 256, 64, 32, 64
ROPE_THETA = 10000.0
HG_HEADS, HG_DK, HG_DV, HG_CHUNK = 4, 128, 128, 64
HG_SUB_FWD, HG_SUB_BWD = 32, 16
X_HEADS, X_HEAD_DIM = 4, 128
D_FF = 2816
D_IN = 6432
N_CHIPS = 4
D_IN_SHARD = D_IN // N_CHIPS
D_IN_BLOCK = 1664
MASK_VALUE = -1e30

ADAM_LR, ADAM_B1, ADAM_B2, ADAM_EPS, ADAM_WD, ADAM_STEP = 0.001, 0.9, 0.999, 1e-08, 0.01, 10

VMEM_LIMIT = 48 * 1024 * 1024


def _pcall(body, **kw):
    return pl.pallas_call(body, **kw)


def _params(sem):
    return pltpu.CompilerParams(dimension_semantics=sem, vmem_limit_bytes=VMEM_LIMIT)


def _tile(n, cap, align=128):
    if n <= cap:
        return n
    t = (cap // align) * align
    while t >= align:
        if n % t == 0:
            return t
        t -= align
    return n


MM_VMEM_BUDGET = 36 * 1024 * 1024
MM_TILE_CAP = 2048


def _divisors(n, cap, align=128):
    out = [n] if n <= cap else []
    out += [t for t in range(align, min(n, cap + 1), align) if n % t == 0]
    return sorted(set(out), reverse=True)


@functools.lru_cache(maxsize=None)
def _mm_tiles(m, n, k, a_bytes, b_bytes, o_bytes, n_unit, k_unit):
    best = None
    for tm in _divisors(m, MM_TILE_CAP):
        for tn in _divisors(n_unit, MM_TILE_CAP):
            for tk in _divisors(k_unit, MM_TILE_CAP):
                nk = k // tk
                need = 2 * (tm * tk * a_bytes + tk * tn * b_bytes + tm * tn * o_bytes)
                need += tm * tn * 4 * (2 if nk > 1 else 1)
                need += (tm * tk * 2 if a_bytes > 2 else 0) + (tk * tn * 2 if b_bytes > 2 else 0)
                if need > MM_VMEM_BUDGET:
                    continue
                key = ((m // tm) * (n // tn) * nk, nk, -tn)
                if best is None or key < best[0]:
                    best = (key, (tm, tn, tk))
    assert best is not None, (m, n, k)
    return best[1]


def _mm(a, b, *, ta=False, tb=False, out_dtype=F32, name="mm", b_blocks=False, out_blocks=0, pro=None, epi=None):
    if ta:
        K, M = a.shape
    else:
        M, K = a.shape
    n_unit = k_unit = None
    if b_blocks:
        nb, rows, cols = b.shape
        if tb:
            N, Kb, k_unit = rows, nb * cols, cols
        else:
            Kb, N, n_unit = rows, nb * cols, cols
    elif tb:
        N, Kb = b.shape
    else:
        Kb, N = b.shape
    if out_blocks:
        assert not b_blocks and N % out_blocks == 0
        n_unit = N // out_blocks
    assert K == Kb, (a.shape, b.shape, ta, tb)
    n_pro = len(pro) - 1 if pro else 0
    n_epi = len(epi) - 1 if epi else 0
    tm, tn, tk = _mm_tiles(M, N, K, a.dtype.itemsize * (1 + n_pro), b.dtype.itemsize,
                           jnp.dtype(out_dtype).itemsize * max(1, 2 * n_epi), n_unit or N, k_unit or K)
    nk = K // tk
    dims = (((0,) if ta else (1,), (1,) if tb else (0,)), ((), ()))
    n_in = 2 + n_pro + n_epi
    n_out = max(1, n_epi)

    def body(*refs):
        ins, outs, acc = refs[:n_in], refs[n_in:n_in + n_out], refs[n_in + n_out:]
        b_ref = ins[1 + n_pro]
        av = (pro[0](*[r[...] for r in ins[:1 + n_pro]]) if pro else ins[0][...]).astype(BF16)
        bv = b_ref[...].astype(BF16)
        part = lax.dot_general(av, bv, dims, preferred_element_type=F32)

        def finish(total):
            if epi:
                for o_ref, r in zip(outs, epi[0](total, *[e[...] for e in ins[2 + n_pro:]])):
                    o_ref[...] = r.astype(o_ref.dtype)
            else:
                outs[0][...] = total.astype(outs[0].dtype)

        if nk == 1:
            finish(part)
        else:
            acc_ref = acc[0]
            k = pl.program_id(2)

            @pl.when(k == 0)
            def _():
                acc_ref[...] = part

            @pl.when(k > 0)
            def _():
                acc_ref[...] += part

            @pl.when(k == nk - 1)
            def _():
                finish(acc_ref[...])

    a_spec = pl.BlockSpec((tk, tm), lambda i, j, k: (k, i)) if ta else pl.BlockSpec((tm, tk), lambda i, j, k: (i, k))
    if b_blocks and tb:
        r = k_unit // tk
        b_spec = pl.BlockSpec((None, tn, tk), lambda i, j, k: (k // r, j, k % r))
    elif b_blocks:
        r = n_unit // tn
        b_spec = pl.BlockSpec((None, tk, tn), lambda i, j, k: (j // r, k, j % r))
    elif tb:
        b_spec = pl.BlockSpec((tn, tk), lambda i, j, k: (j, k))
    else:
        b_spec = pl.BlockSpec((tk, tn), lambda i, j, k: (k, j))
    if out_blocks:
        r = n_unit // tn
        out_spec = pl.BlockSpec((None, tm, tn), lambda i, j, k: (j // r, i, j % r))
        out_shape = jax.ShapeDtypeStruct((out_blocks, M, n_unit), out_dtype)
    else:
        out_spec = pl.BlockSpec((tm, tn), lambda i, j, k: (i, j))
        out_shape = jax.ShapeDtypeStruct((M, N), out_dtype)
    in_specs = [a_spec] * (1 + n_pro) + [b_spec] + [out_spec] * n_epi
    args = [a] + (list(pro[1:]) if pro else []) + [b] + (list(epi[1:]) if epi else [])
    return _pcall(
        body,
        name=name,
        grid=(M // tm, N // tn, nk),
        in_specs=in_specs,
        out_specs=[out_spec] * n_epi if epi else out_spec,
        out_shape=[out_shape] * n_epi if epi else out_shape,
        scratch_shapes=[pltpu.VMEM((tm, tn), F32)] if nk > 1 else [],
        compiler_params=_params(("parallel", "parallel", "arbitrary")),
    )(*args)


def _bd_call(a, bc, transpose, name):
    m = a.shape[0]
    nb, ka, kb = bc.shape
    win, wout = (kb, ka) if transpose else (ka, kb)
    assert a.shape[1] == nb * win
    tm = _tile(m, 1024)
    dims = _NT if transpose else (((1,), (0,)), ((), ()))

    def body(a_ref, b_ref, o_ref):
        o_ref[...] = lax.dot_general(a_ref[...].astype(BF16), b_ref[0].astype(BF16), dims, preferred_element_type=F32)

    return _pcall(
        body,
        name=name,
        grid=(m // tm, nb),
        in_specs=[pl.BlockSpec((tm, win), lambda i, j: (i, j)), pl.BlockSpec((1, ka, kb), lambda i, j: (j, 0, 0))],
        out_specs=pl.BlockSpec((tm, wout), lambda i, j: (i, j)),
        out_shape=jax.ShapeDtypeStruct((m, nb * wout), F32),
        compiler_params=_params(("parallel", "parallel")),
    )(a, bc)


def _bd_dw_call(a, g, nb, name):
    m = a.shape[0]
    ka, kb = a.shape[1] // nb, g.shape[1] // nb
    tk = _tile(m, 1024)
    nk = m // tk

    def body(a_ref, g_ref, o_ref):
        part = lax.dot_general(a_ref[...].astype(BF16), g_ref[...].astype(BF16), _TN, preferred_element_type=F32)
        k = pl.program_id(1)

        @pl.when(k == 0)
        def _():
            o_ref[0] = part

        @pl.when(k > 0)
        def _():
            o_ref[0] += part

    return _pcall(
        body,
        name=name,
        grid=(nb, nk),
        in_specs=[pl.BlockSpec((tk, ka), lambda j, k: (k, j)), pl.BlockSpec((tk, kb), lambda j, k: (k, j))],
        out_specs=pl.BlockSpec((1, ka, kb), lambda j, k: (j, 0, 0)),
        out_shape=jax.ShapeDtypeStruct((nb, ka, kb), F32),
        compiler_params=_params(("parallel", "arbitrary")),
    )(a, g)


@functools.partial(jax.custom_vjp, nondiff_argnums=(2,))
def bd_matmul(a, bc, name):
    return _bd_call(a, bc, False, name + "_f")


def _bd_matmul_fwd(a, bc, name):
    return _bd_call(a, bc, False, name + "_f"), (a, bc)


def _bd_matmul_bwd(name, res, g):
    a, bc = res
    return _bd_call(g, bc, True, name + "_da"), _bd_dw_call(a, g, bc.shape[0], name + "_dw")


bd_matmul.defvjp(_bd_matmul_fwd, _bd_matmul_bwd)


@functools.partial(jax.custom_vjp, nondiff_argnums=(3,))
def matmul_w(a, w16, tok, name):
    return _mm(a, w16, b_blocks=w16.ndim == 3, name=name + "_f")


def _matmul_w_fwd(a, w16, tok, name):
    return _mm(a, w16, b_blocks=w16.ndim == 3, name=name + "_f"), (a, w16)


def _matmul_w_bwd(name, res, g):
    a, w16 = res
    blocks = w16.shape[0] if w16.ndim == 3 else 0
    return (_mm(g, w16, tb=True, b_blocks=bool(blocks), name=name + "_da"), jnp.zeros_like(w16),
            _mm(a, g, ta=True, out_blocks=blocks, name=name + "_dw"))


matmul_w.defvjp(_matmul_w_fwd, _matmul_w_bwd)


def _swiglu(gt, up):
    return gt * jax.nn.sigmoid(gt) * up


def _swiglu_bwd(dact, gt, up):
    sg = jax.nn.sigmoid(gt)
    return dact * up * sg * (1.0 + gt * (1.0 - sg)), dact * gt * sg


@functools.partial(jax.custom_vjp, nondiff_argnums=(4,))
def swiglu_down(gt, up, w16, tok, name):
    return _mm(gt, w16, pro=(_swiglu, up), name=name + "_f")


def _swiglu_down_fwd(gt, up, w16, tok, name):
    return _mm(gt, w16, pro=(_swiglu, up), name=name + "_f"), (gt, up, w16)


def _swiglu_down_bwd(name, res, g):
    gt, up, w16 = res
    d_gt, d_up = _mm(g, w16, tb=True, epi=(_swiglu_bwd, gt, up), name=name + "_da")
    return d_gt, d_up, jnp.zeros_like(w16), _mm(gt, g, ta=True, pro=(_swiglu, up), name=name + "_dw")


swiglu_down.defvjp(_swiglu_down_fwd, _swiglu_down_bwd)


def _merge(y1, y2, y3, l1, l2, l3):
    return jax.nn.sigmoid(l1) * y1 + jax.nn.sigmoid(l2) * y2 + jax.nn.sigmoid(l3) * y3


def _merge_bwd(dm, y1, y2, y3, l1, l2, l3):
    dys, dls = [], []
    for y, logit in ((y1, l1), (y2, l2), (y3, l3)):
        s = jax.nn.sigmoid(logit)
        dys.append(dm * s)
        dls.append(dm * y * s * (1.0 - s))
    return (*dys, *dls)


@functools.partial(jax.custom_vjp, nondiff_argnums=(4,))
def gated_merge_out(ys, logits, w16, tok, name):
    return _mm(ys[0], w16, pro=(_merge, *ys[1:], *logits), name=name + "_f")


def _gated_merge_out_fwd(ys, logits, w16, tok, name):
    return _mm(ys[0], w16, pro=(_merge, *ys[1:], *logits), name=name + "_f"), (ys, logits, w16)


def _gated_merge_out_bwd(name, res, g):
    ys, logits, w16 = res
    d = _mm(g, w16, tb=True, epi=(_merge_bwd, *ys, *logits), name=name + "_da")
    dw = _mm(ys[0], g, ta=True, pro=(_merge, *ys[1:], *logits), name=name + "_dw")
    return tuple(d[:3]), tuple(d[3:]), jnp.zeros_like(w16), dw


gated_merge_out.defvjp(_gated_merge_out_fwd, _gated_merge_out_bwd)


@functools.partial(jax.custom_vjp, nondiff_argnums=(3,))
def matmul_w_t(a, w16, tok, name):
    return _mm(w16, a, ta=True, tb=True, name=name + "_f")


def _matmul_w_t_fwd(a, w16, tok, name):
    return _mm(w16, a, ta=True, tb=True, name=name + "_f"), (a, w16)


def _matmul_w_t_bwd(name, res, g):
    a, w16 = res
    return (_mm(g, w16, ta=True, tb=True, name=name + "_da"), jnp.zeros_like(w16),
            _mm(a, g, ta=True, tb=True, name=name + "_dw"))


matmul_w_t.defvjp(_matmul_w_t_fwd, _matmul_w_t_bwd)


@functools.partial(jax.custom_vjp, nondiff_argnums=(3,))
def matmul_w_at(at, w16, tok, name):
    return _mm(at, w16, ta=True, name=name + "_f")


def _matmul_w_at_fwd(at, w16, tok, name):
    return _mm(at, w16, ta=True, name=name + "_f"), (at, w16)


def _matmul_w_at_bwd(name, res, g):
    at, w16 = res
    return _mm(w16, g, tb=True, name=name + "_da"), jnp.zeros_like(w16), _mm(at, g, name=name + "_dw")


matmul_w_at.defvjp(_matmul_w_at_fwd, _matmul_w_at_bwd)


def _rms_fwd_call(x, g, name):
    rows, d = x.shape
    tr = _tile(rows, 512, 8)

    def body(x_ref, g_ref, y_ref):
        xv = x_ref[...]
        rstd = lax.rsqrt(jnp.mean(xv * xv, axis=-1, keepdims=True) + RMS_EPS)
        y_ref[...] = xv * rstd * g_ref[...]

    return _pcall(
        body,
        name=name,
        grid=(rows // tr,),
        in_specs=[pl.BlockSpec((tr, d), lambda i: (i, 0)), pl.BlockSpec((1, d), lambda i: (0, 0))],
        out_specs=pl.BlockSpec((tr, d), lambda i: (i, 0)),
        out_shape=jax.ShapeDtypeStruct((rows, d), F32),
        compiler_params=_params(("parallel",)),
    )(x, g.reshape(1, d))


def _rms_bwd_call(x, g, dy, name):
    rows, d = x.shape
    tr = _tile(rows, 512, 8)
    nb = rows // tr

    def body(x_ref, g_ref, dy_ref, dx_ref, dg_ref):
        xv = x_ref[...]
        rstd = lax.rsqrt(jnp.mean(xv * xv, axis=-1, keepdims=True) + RMS_EPS)
        xh = xv * rstd
        dyv = dy_ref[...]
        dyg = dyv * g_ref[...]
        dx_ref[...] = rstd * (dyg - xh * jnp.mean(dyg * xh, axis=-1, keepdims=True))
        dg_ref[...] = jnp.sum((dyv * xh).reshape(tr // 8, 8, d), axis=0)

    dx, dgp = _pcall(
        body,
        name=name,
        grid=(nb,),
        in_specs=[
            pl.BlockSpec((tr, d), lambda i: (i, 0)),
            pl.BlockSpec((1, d), lambda i: (0, 0)),
            pl.BlockSpec((tr, d), lambda i: (i, 0)),
        ],
        out_specs=[pl.BlockSpec((tr, d), lambda i: (i, 0)), pl.BlockSpec((8, d), lambda i: (i, 0))],
        out_shape=[jax.ShapeDtypeStruct((rows, d), F32), jax.ShapeDtypeStruct((nb * 8, d), F32)],
        compiler_params=_params(("parallel",)),
    )(x, g.reshape(1, d), dy)
    return dx, jnp.sum(dgp, axis=0)


@functools.partial(jax.custom_vjp, nondiff_argnums=(2,))
def rmsnorm(x, g, name):
    return _rms_fwd_call(x, g, name + "_f")


def _rmsnorm_fwd(x, g, name):
    return _rms_fwd_call(x, g, name + "_f"), (x, g)


def _rmsnorm_bwd(name, res, dy):
    x, g = res
    return _rms_bwd_call(x, g, dy, name + "_b")


rmsnorm.defvjp(_rmsnorm_fwd, _rmsnorm_bwd)


def _attn_blocks(sq, sk):
    return _tile(sq, 1024), _tile(sk, 512)


def _attn_pt(qs, k_ref, qi, ki, bq, bk, masked, shift):
    st = jnp.dot(k_ref[0].astype(BF16), qs, preferred_element_type=F32)
    if masked:
        kpos = ki * bk + lax.broadcasted_iota(jnp.int32, (bk, bq), 0)
        qpos = qi * bq + lax.broadcasted_iota(jnp.int32, (bk, bq), 1)
        st = jnp.where(kpos <= qpos, st, MASK_VALUE)
    return st if shift is None else jnp.exp(st - shift)


def _attn_pairs(nq, nk, bq, bk, causal, k_major):
    ok = lambda i, j: (not causal) or j * bk <= i * bq + bq - 1
    if k_major:
        pairs = [(i, j) for j in range(nk) for i in range(nq) if ok(i, j)]
    else:
        pairs = [(i, j) for i in range(nq) for j in range(nk) if ok(i, j)]
    qtab = jnp.asarray([p[0] for p in pairs], jnp.int32)
    ktab = jnp.asarray([p[1] for p in pairs], jnp.int32)
    first_q = (lambda j: (j * bk) // bq) if causal else (lambda j: 0)
    last_k = (lambda i: jnp.minimum((i * bq + bq - 1) // bk, nk - 1)) if causal else (lambda i: nk - 1)
    return qtab, ktab, first_q, last_k


def _attn_cases(causal, qi, ki, bq, bk, step):
    if not causal:
        step(False)
        return
    pl.when(ki * bk + bk - 1 <= qi * bq)(functools.partial(step, False))
    pl.when(ki * bk + bk - 1 > qi * bq)(functools.partial(step, True))


def _attn_fwd_call(qt, k, vt, scale, causal, name):
    h, sk, dq = k.shape
    sq = qt.shape[1]
    dv = vt.shape[0] // h
    bq, bk = _attn_blocks(sq, sk)
    nq, nk = sq // bq, sk // bk

    qtab, ktab, _, last_k = _attn_pairs(nq, nk, bq, bk, causal, False)

    def body(qi_ref, ki_ref, q_ref, k_ref, v_ref, o_ref, lse_ref, m_sc, l_sc, acc_sc):
        qi, ki = qi_ref[pl.program_id(1)], ki_ref[pl.program_id(1)]

        @pl.when(ki == 0)
        def _():
            m_sc[...] = jnp.full(m_sc.shape, -jnp.inf, F32)
            l_sc[...] = jnp.zeros(l_sc.shape, F32)
            acc_sc[...] = jnp.zeros(acc_sc.shape, F32)

        def step(masked):
            qs = (q_ref[...] * scale).astype(BF16)
            st = _attn_pt(qs, k_ref, qi, ki, bq, bk, masked, None)
            m_prev = m_sc[...]
            m_new = jnp.maximum(m_prev, jnp.max(st, axis=0, keepdims=True))
            alpha = jnp.exp(m_prev - m_new)
            pt = jnp.exp(st - m_new)
            l_sc[...] = alpha * l_sc[...] + jnp.sum(pt, axis=0, keepdims=True)
            acc_sc[...] = alpha * acc_sc[...] + jnp.dot(v_ref[...].astype(BF16), pt.astype(BF16),
                                                        preferred_element_type=F32)
            m_sc[...] = m_new

        _attn_cases(causal, qi, ki, bq, bk, step)

        @pl.when(ki == last_k(qi))
        def _():
            o_ref[...] = acc_sc[...] / l_sc[...]
            lse_ref[0] = m_sc[...] + jnp.log(l_sc[...])

    qmap = lambda hh, p, qi_ref, ki_ref: (hh, qi_ref[p])
    return _pcall(
        body,
        name=name,
        grid_spec=pltpu.PrefetchScalarGridSpec(
            num_scalar_prefetch=2,
            grid=(h, qtab.shape[0]),
            in_specs=[
                pl.BlockSpec((dq, bq), qmap),
                pl.BlockSpec((1, bk, dq), lambda hh, p, qi_ref, ki_ref: (hh, ki_ref[p], 0)),
                pl.BlockSpec((dv, bk), lambda hh, p, qi_ref, ki_ref: (hh, ki_ref[p])),
            ],
            out_specs=[pl.BlockSpec((dv, bq), qmap),
                       pl.BlockSpec((1, 1, bq), lambda hh, p, qi_ref, ki_ref: (hh, 0, qi_ref[p]))],
            scratch_shapes=[pltpu.VMEM((1, bq), F32), pltpu.VMEM((1, bq), F32), pltpu.VMEM((dv, bq), F32)],
        ),
        out_shape=[jax.ShapeDtypeStruct((h * dv, sq), F32), jax.ShapeDtypeStruct((h, 1, sq), F32)],
        compiler_params=_params(("parallel", "arbitrary")),
    )(qtab, ktab, qt, k, vt)


_TN = (((0,), (0,)), ((), ()))
_NT = (((1,), (1,)), ((), ()))


def _attn_dst(qs, k_ref, v_ref, do_ref, lse_ref, dl_ref, qi, ki, bq, bk, masked):
    pt = _attn_pt(qs, k_ref, qi, ki, bq, bk, masked, lse_ref[0])
    dpt = lax.dot_general(v_ref[...].astype(BF16), do_ref[...].astype(BF16), _TN, preferred_element_type=F32)
    return pt, pt * (dpt - dl_ref[0])


def _attn_dq_call(qt, k, vt, dot, lse, delta, scale, causal, name):
    h, sk, dq = k.shape
    sq = qt.shape[1]
    dv = vt.shape[0] // h
    bq, bk = _attn_blocks(sq, sk)
    nq, nk = sq // bq, sk // bk

    qtab, ktab, _, last_k = _attn_pairs(nq, nk, bq, bk, causal, False)

    def body(qi_ref, ki_ref, q_ref, k_ref, v_ref, do_ref, lse_ref, dl_ref, dq_ref, acc_sc):
        qi, ki = qi_ref[pl.program_id(1)], ki_ref[pl.program_id(1)]

        @pl.when(ki == 0)
        def _():
            acc_sc[...] = jnp.zeros(acc_sc.shape, F32)

        def step(masked):
            qs = (q_ref[...] * scale).astype(BF16)
            _, dst = _attn_dst(qs, k_ref, v_ref, do_ref, lse_ref, dl_ref, qi, ki, bq, bk, masked)
            acc_sc[...] += lax.dot_general(k_ref[0].astype(BF16), dst.astype(BF16), _TN, preferred_element_type=F32)

        _attn_cases(causal, qi, ki, bq, bk, step)

        @pl.when(ki == last_k(qi))
        def _():
            dq_ref[...] = acc_sc[...] * scale

    qmap = lambda hh, p, qi_ref, ki_ref: (hh, qi_ref[p])
    rowmap = lambda hh, p, qi_ref, ki_ref: (hh, 0, qi_ref[p])
    return _pcall(
        body,
        name=name,
        grid_spec=pltpu.PrefetchScalarGridSpec(
            num_scalar_prefetch=2,
            grid=(h, qtab.shape[0]),
            in_specs=[
                pl.BlockSpec((dq, bq), qmap),
                pl.BlockSpec((1, bk, dq), lambda hh, p, qi_ref, ki_ref: (hh, ki_ref[p], 0)),
                pl.BlockSpec((dv, bk), lambda hh, p, qi_ref, ki_ref: (hh, ki_ref[p])),
                pl.BlockSpec((dv, bq), qmap),
                pl.BlockSpec((1, 1, bq), rowmap),
                pl.BlockSpec((1, 1, bq), rowmap),
            ],
            out_specs=pl.BlockSpec((dq, bq), qmap),
            scratch_shapes=[pltpu.VMEM((dq, bq), F32)],
        ),
        out_shape=jax.ShapeDtypeStruct((h * dq, sq), F32),
        compiler_params=_params(("parallel", "arbitrary")),
    )(qtab, ktab, qt, k, vt, dot, lse, delta)


def _attn_dkv_call(qt, k, vt, dot, lse, delta, scale, causal, name):
    h, sk, dq = k.shape
    sq = qt.shape[1]
    dv = vt.shape[0] // h
    bq, bk = _attn_blocks(sq, sk)
    nq, nk = sq // bq, sk // bk

    qtab, ktab, first_q, _ = _attn_pairs(nq, nk, bq, bk, causal, True)

    def body(qi_ref, ki_ref, q_ref, k_ref, v_ref, do_ref, lse_ref, dl_ref, dk_ref, dv_ref, dk_sc, dv_sc):
        qi, ki = qi_ref[pl.program_id(1)], ki_ref[pl.program_id(1)]

        @pl.when(qi == first_q(ki))
        def _():
            dk_sc[...] = jnp.zeros(dk_sc.shape, F32)
            dv_sc[...] = jnp.zeros(dv_sc.shape, F32)

        def step(masked):
            qs = (q_ref[...] * scale).astype(BF16)
            pt, dst = _attn_dst(qs, k_ref, v_ref, do_ref, lse_ref, dl_ref, qi, ki, bq, bk, masked)
            dv_sc[...] += lax.dot_general(do_ref[...].astype(BF16), pt.astype(BF16), _NT, preferred_element_type=F32)
            dk_sc[...] += lax.dot_general(dst.astype(BF16), qs, _NT, preferred_element_type=F32)

        _attn_cases(causal, qi, ki, bq, bk, step)

        @pl.when(qi == nq - 1)
        def _():
            dk_ref[0] = dk_sc[...]
            dv_ref[...] = dv_sc[...]

    qmap = lambda hh, p, qi_ref, ki_ref: (hh, qi_ref[p])
    rowmap = lambda hh, p, qi_ref, ki_ref: (hh, 0, qi_ref[p])
    kmap = lambda hh, p, qi_ref, ki_ref: (hh, ki_ref[p], 0)
    vmap = lambda hh, p, qi_ref, ki_ref: (hh, ki_ref[p])
    return _pcall(
        body,
        name=name,
        grid_spec=pltpu.PrefetchScalarGridSpec(
            num_scalar_prefetch=2,
            grid=(h, qtab.shape[0]),
            in_specs=[
                pl.BlockSpec((dq, bq), qmap),
                pl.BlockSpec((1, bk, dq), kmap),
                pl.BlockSpec((dv, bk), vmap),
                pl.BlockSpec((dv, bq), qmap),
                pl.BlockSpec((1, 1, bq), rowmap),
                pl.BlockSpec((1, 1, bq), rowmap),
            ],
            out_specs=[pl.BlockSpec((1, bk, dq), kmap), pl.BlockSpec((dv, bk), vmap)],
            scratch_shapes=[pltpu.VMEM((bk, dq), F32), pltpu.VMEM((dv, bk), F32)],
        ),
        out_shape=[jax.ShapeDtypeStruct((h, sk, dq), F32), jax.ShapeDtypeStruct((h * dv, sk), F32)],
        compiler_params=_params(("parallel", "arbitrary")),
    )(qtab, ktab, qt, k, vt, dot, lse, delta)


@functools.partial(jax.custom_vjp, nondiff_argnums=(3, 4, 5))
def attention(qt, k, vt, scale, causal, name):
    return _attn_fwd_call(qt, k, vt, scale, causal, name + "_f")[0]


def _attention_fwd(qt, k, vt, scale, causal, name):
    ot, lse = _attn_fwd_call(qt, k, vt, scale, causal, name + "_f")
    return ot, (qt, k, vt, ot, lse)


def _attention_bwd(scale, causal, name, res, dot):
    qt, k, vt, ot, lse = res
    h = k.shape[0]
    delta = jnp.sum((dot * ot).reshape(h, ot.shape[0] // h, ot.shape[1]), axis=1, keepdims=True)
    dqt = _attn_dq_call(qt, k, vt, dot, lse, delta, scale, causal, name + "_dq")
    dk, dvt = _attn_dkv_call(qt, k, vt, dot, lse, delta, scale, causal, name + "_dkv")
    return dqt, dk, dvt


attention.defvjp(_attention_fwd, _attention_bwd)


SCAN_ROWS = 8
SCAN_LANES = 256
SCAN_TBLOCK = 512
SCAN_UNROLL = 4


def _scan_tables(lr, li):
    pr, pi = [lr], [li]
    for _ in range(SCAN_ROWS - 1):
        pr, pi = pr + [pr[-1] * lr - pi[-1] * li], pi + [pr[-1] * li + pi[-1] * lr]
    return jnp.concatenate(pr, axis=0), jnp.concatenate(pi, axis=0)


def _scan_call(ur, ui, lr, li, reverse, name, states=None):
    s, n = ur.shape
    tb = _tile(s, SCAN_TBLOCK, 8)
    nt, nl = s // tb, n // SCAN_LANES
    ntile = tb // SCAN_ROWS
    pr, pi = _scan_tables(lr, li)
    if reverse:
        pr, pi = pr[::-1], pi[::-1]
    shape = (SCAN_ROWS, SCAN_LANES)
    with_states = states is not None
    assert reverse or not with_states

    def body(*refs):
        if with_states:
            ur_ref, ui_ref, pr_ref, pi_ref, sr_ref, si_ref, xr_ref, xi_ref, dr_ref, di_ref, cr_sc, ci_sc, ar_sc, ai_sc = refs
        else:
            ur_ref, ui_ref, pr_ref, pi_ref, xr_ref, xi_ref, cr_sc, ci_sc, ar_sc, ai_sc = refs

        @pl.when(pl.program_id(1) == 0)
        def _():
            for sc in (cr_sc, ci_sc, ar_sc, ai_sc):
                sc[...] = jnp.zeros(shape, F32)

        prv, piv = pr_ref[...], pi_ref[...]
        rows = lax.broadcasted_iota(jnp.int32, shape, 0)

        def powers(k):
            r = (SCAN_ROWS - k) if reverse else (k - 1)
            return jnp.broadcast_to(prv[r:r + 1], shape), jnp.broadcast_to(piv[r:r + 1], shape)

        pw = [powers(k) for k in (1, 2, 4)]

        def tile(i, carry):
            cr, ci, acr, aci = carry
            j = (ntile - 1 - i) if reverse else i
            sl = pl.ds(pl.multiple_of(j * SCAN_ROWS, SCAN_ROWS), SCAN_ROWS)
            xr, xi = ur_ref[sl, :], ui_ref[sl, :]
            for (ar, ai), k in zip(pw, (1, 2, 4)):
                if reverse:
                    keep = rows < SCAN_ROWS - k
                    sr = jnp.where(keep, pltpu.roll(xr, SCAN_ROWS - k, 0), 0.0)
                    si = jnp.where(keep, pltpu.roll(xi, SCAN_ROWS - k, 0), 0.0)
                else:
                    keep = rows >= k
                    sr = jnp.where(keep, pltpu.roll(xr, k, 0), 0.0)
                    si = jnp.where(keep, pltpu.roll(xi, k, 0), 0.0)
                xr, xi = xr + (ar * sr - ai * si), xi + (ar * si + ai * sr)
            xr, xi = xr + (prv * cr - piv * ci), xi + (prv * ci + piv * cr)
            xr_ref[sl, :] = xr
            xi_ref[sl, :] = xi
            if with_states:
                last = rows == SCAN_ROWS - 1
                nr = jnp.where(last, cr, pltpu.roll(xr, SCAN_ROWS - 1, 0))
                ni = jnp.where(last, ci, pltpu.roll(xi, SCAN_ROWS - 1, 0))
                sr, si = sr_ref[sl, :], si_ref[sl, :]
                acr, aci = acr + (sr * nr + si * ni), aci + (sr * ni - si * nr)
            e = 0 if reverse else SCAN_ROWS - 1
            return jnp.broadcast_to(xr[e:e + 1], shape), jnp.broadcast_to(xi[e:e + 1], shape), acr, aci

        def tiles(g, carry):
            for u in range(SCAN_UNROLL):
                carry = tile(g * SCAN_UNROLL + u, carry)
            return carry

        assert ntile % SCAN_UNROLL == 0
        cr, ci, acr, aci = lax.fori_loop(0, ntile // SCAN_UNROLL, tiles,
                                         (cr_sc[...], ci_sc[...], ar_sc[...], ai_sc[...]))
        cr_sc[...] = cr
        ci_sc[...] = ci
        if with_states:
            ar_sc[...] = acr
            ai_sc[...] = aci

            @pl.when(pl.program_id(1) == nt - 1)
            def _():
                dr_ref[...] = acr
                di_ref[...] = aci

    tmap = (lambda l, t: (nt - 1 - t, l)) if reverse else (lambda l, t: (t, l))
    blk = pl.BlockSpec((tb, SCAN_LANES), tmap)
    tab = pl.BlockSpec((SCAN_ROWS, SCAN_LANES), lambda l, t: (0, l))
    full = jax.ShapeDtypeStruct((s, n), F32)
    return _pcall(
        body,
        name=name,
        grid=(nl, nt),
        in_specs=[blk, blk, tab, tab] + ([blk, blk] if with_states else []),
        out_specs=[blk, blk] + ([tab, tab] if with_states else []),
        out_shape=[full, full] + ([jax.ShapeDtypeStruct((SCAN_ROWS, n), F32)] * 2 if with_states else []),
        scratch_shapes=[pltpu.VMEM(shape, F32)] * 4,
        compiler_params=_params(("parallel", "arbitrary")),
    )(ur, ui, pr, pi, *(states if with_states else ()))


@functools.partial(jax.custom_vjp, nondiff_argnums=(4,))
def s5_scan(ur, ui, lr, li, name):
    return tuple(_scan_call(ur, ui, lr, li, False, name + "_f"))


def _s5_scan_fwd(ur, ui, lr, li, name):
    xr, xi = _scan_call(ur, ui, lr, li, False, name + "_f")
    return (xr, xi), (xr, xi, lr, li)


def _s5_scan_bwd(name, res, g):
    xr, xi, lr, li = res
    gr, gi, dlr, dli = _scan_call(g[0], g[1], lr, -li, True, name + "_b", states=(xr, xi))
    return gr, gi, jnp.sum(dlr, axis=0, keepdims=True), jnp.sum(dli, axis=0, keepdims=True)


s5_scan.defvjp(_s5_scan_fwd, _s5_scan_bwd)


def _running_sum(x, reverse):
    n = x.shape[0]
    rows = lax.broadcasted_iota(jnp.int32, x.shape, 0)
    k = 1
    while k < n:
        if reverse:
            x = x + jnp.where(rows < n - k, pltpu.roll(x, n - k, 0), 0.0)
        else:
            x = x + jnp.where(rows >= k, pltpu.roll(x, k, 0), 0.0)
        k *= 2
    return x


@jax.custom_vjp
def cumsum_rows(x):
    return _running_sum(x, False)


cumsum_rows.defvjp(lambda x: (_running_sum(x, False), None), lambda _, g: (_running_sum(g, True),))


def _hg_chunk(HG_SUB, q, k, v, lf, st):
    c = HG_CHUNK
    b = cumsum_rows(lf)
    qb = q * jnp.exp(b)
    o = lax.dot_general(qb.astype(BF16), st.astype(BF16), (((1,), (1,)), ((), ())), preferred_element_type=F32)
    nsub = c // HG_SUB
    srow = lax.broadcasted_iota(jnp.int32, (HG_SUB, HG_SUB, 1), 0)
    scol = lax.broadcasted_iota(jnp.int32, (HG_SUB, HG_SUB, 1), 1)
    smask = scol <= srow
    outs = []
    for i in range(nsub):
        lo = i * HG_SUB
        bi, qi, ki = b[lo:lo + HG_SUB], q[lo:lo + HG_SUB], k[lo:lo + HG_SUB]
        diff = jnp.where(smask, bi[:, None, :] - bi[None, :, :], 0.0)
        e = jnp.where(smask, jnp.exp(diff), 0.0)
        a_ii = jnp.sum(qi[:, None, :] * ki[None, :, :] * e, axis=-1)
        oi = jnp.dot(a_ii.astype(BF16), v[lo:lo + HG_SUB].astype(BF16), preferred_element_type=F32)
        if i > 0:
            r = b[lo - 1:lo]
            qt = qi * jnp.exp(bi - r)
            kt = k[:lo] * jnp.exp(r - b[:lo])
            a_ij = lax.dot_general(qt.astype(BF16), kt.astype(BF16), (((1,), (1,)), ((), ())),
                                   preferred_element_type=F32)
            oi = oi + jnp.dot(a_ij.astype(BF16), v[:lo].astype(BF16), preferred_element_type=F32)
        outs.append(oi)
    o = o + jnp.concatenate(outs, axis=0)
    bl = b[c - 1:c]
    kd = k * jnp.exp(bl - b)
    st_new = st * jnp.exp(bl) + lax.dot_general(v.astype(BF16), kd.astype(BF16), (((0,), (0,)), ((), ())),
                                                preferred_element_type=F32)
    return o, st_new


def _hg_fwd_call(q, k, v, lf, name):
    s, w = q.shape
    nc = s // HG_CHUNK
    blk = pl.BlockSpec((HG_CHUNK, w), lambda i: (i, 0))

    def body(q_ref, k_ref, v_ref, lf_ref, o_ref, st_ref, st_sc):
        @pl.when(pl.program_id(0) == 0)
        def _():
            st_sc[...] = jnp.zeros(st_sc.shape, F32)

        for hh in range(HG_HEADS):
            cs = slice(hh * HG_DK, (hh + 1) * HG_DK)
            st = st_sc[hh]
            st_ref[0, hh] = st
            o, st_new = _hg_chunk(HG_SUB_FWD, q_ref[:, cs], k_ref[:, cs], v_ref[:, cs], lf_ref[:, cs], st)
            o_ref[:, cs] = o
            st_sc[hh] = st_new

    return _pcall(
        body,
        name=name,
        grid=(nc,),
        in_specs=[blk, blk, blk, blk],
        out_specs=[blk, pl.BlockSpec((1, HG_HEADS, HG_DV, HG_DK), lambda i: (i, 0, 0, 0))],
        out_shape=[jax.ShapeDtypeStruct((s, w), F32), jax.ShapeDtypeStruct((nc, HG_HEADS, HG_DV, HG_DK), F32)],
        scratch_shapes=[pltpu.VMEM((HG_HEADS, HG_DV, HG_DK), F32)],
        compiler_params=_params(("arbitrary",)),
    )(q, k, v, lf)


def _hg_bwd_call(q, k, v, lf, sts, do, name):
    s, w = q.shape
    nc = s // HG_CHUNK
    blk = pl.BlockSpec((HG_CHUNK, w), lambda i: (nc - 1 - i, 0))

    def body(q_ref, k_ref, v_ref, lf_ref, st_ref, do_ref, dq_ref, dk_ref, dv_ref, dlf_ref, dst_sc):
        @pl.when(pl.program_id(0) == 0)
        def _():
            dst_sc[...] = jnp.zeros(dst_sc.shape, F32)

        def head(hh, carry):
            cs = pl.ds(pl.multiple_of(hh * HG_DK, HG_DK), HG_DK)
            _, vjp = jax.vjp(functools.partial(_hg_chunk, HG_SUB_BWD), q_ref[:, cs], k_ref[:, cs], v_ref[:, cs],
                             lf_ref[:, cs], st_ref[0, hh])
            dq, dk, dv, dlf, dst = vjp((do_ref[:, cs], dst_sc[hh]))
            dq_ref[:, cs] = dq
            dk_ref[:, cs] = dk
            dv_ref[:, cs] = dv
            dlf_ref[:, cs] = dlf
            dst_sc[hh] = dst
            return carry

        lax.fori_loop(0, HG_HEADS, head, 0)

    out = jax.ShapeDtypeStruct((s, w), F32)
    return _pcall(
        body,
        name=name,
        grid=(nc,),
        in_specs=[blk, blk, blk, blk, pl.BlockSpec((1, HG_HEADS, HG_DV, HG_DK), lambda i: (nc - 1 - i, 0, 0, 0)), blk],
        out_specs=[blk, blk, blk, blk],
        out_shape=[out, out, out, out],
        scratch_shapes=[pltpu.VMEM((HG_HEADS, HG_DV, HG_DK), F32)],
        compiler_params=_params(("arbitrary",)),
    )(q, k, v, lf, sts, do)


@functools.partial(jax.custom_vjp, nondiff_argnums=(4,))
def hgrn_core(q, k, v, lf, name):
    return _hg_fwd_call(q, k, v, lf, name + "_f")[0]


def _hgrn_core_fwd(q, k, v, lf, name):
    o, sts = _hg_fwd_call(q, k, v, lf, name + "_f")
    return o, (q, k, v, lf, sts)


def _hgrn_core_bwd(name, res, do):
    q, k, v, lf, sts = res
    return tuple(_hg_bwd_call(q, k, v, lf, sts, do, name + "_b"))


hgrn_core.defvjp(_hgrn_core_fwd, _hgrn_core_bwd)


def _adamw_call(w, g, m, v, name):
    rows, cols = w.shape
    tr = _tile(rows, 256, 8)
    c1 = 1.0 / (1.0 - ADAM_B1 ** ADAM_STEP)
    c2 = 1.0 / (1.0 - ADAM_B2 ** ADAM_STEP)

    def body(w_ref, g_ref, m_ref, v_ref, d_ref, mo_ref, vo_ref):
        gv = g_ref[...]
        mn = ADAM_B1 * m_ref[...] + (1.0 - ADAM_B1) * gv
        vn = ADAM_B2 * v_ref[...] + (1.0 - ADAM_B2) * (gv * gv)
        d_ref[...] = -ADAM_LR * ((mn * c1) / (jnp.sqrt(vn * c2) + ADAM_EPS) + ADAM_WD * w_ref[...])
        mo_ref[...] = mn
        vo_ref[...] = vn

    blk = pl.BlockSpec((tr, cols), lambda i: (i, 0))
    out = jax.ShapeDtypeStruct((rows, cols), F32)
    return _pcall(
        body,
        name=name,
        grid=(rows // tr,),
        in_specs=[blk, blk, blk, blk],
        out_specs=[blk, blk, blk],
        out_shape=[out, out, out],
        compiler_params=_params(("parallel",)),
    )(w, g, m, v)


ANY = pl.BlockSpec(memory_space=pl.ANY)


def _place():
    return lax.axis_index("x"), lax.axis_index("y"), lax.axis_index("c")


def _other_chips(x, y):
    return [(1 - x, y), (x, 1 - y), (1 - x, 1 - y)]


def _gather_weights(shards):
    nw = len(shards)

    def body(*refs):
        w_refs, g_refs, (send_sems, recv_sems) = refs[:nw], refs[nw:2 * nw], refs[2 * nw:]
        x, y, c = _place()
        myq = 2 * x + y
        sibling = (x, y, 1 - c)
        chips = _other_chips(x, y)

        def copy(k, src, dst, to):
            return pltpu.make_async_remote_copy(src_ref=src, dst_ref=dst, send_sem=send_sems.at[k],
                                                recv_sem=recv_sems.at[k], device_id=to, device_id_type=MESH)

        first = []
        for i in range(nw):
            for j, (px, py) in enumerate(chips):
                first.append(copy(6 * i + j, w_refs[i].at[c], g_refs[i].at[c, myq], (px, py, c)))
                first[-1].start()
        passed = []
        for i in range(nw):
            for j, (px, py) in enumerate(chips):
                q = 2 * px + py
                copy(6 * i + j, w_refs[i].at[c], g_refs[i].at[c, q], (px, py, c)).wait_recv()
                passed.append(copy(6 * i + 3 + j, g_refs[i].at[c, q], g_refs[i].at[c, q], sibling))
                passed[-1].start()
        for i in range(nw):
            for j, (px, py) in enumerate(chips):
                q = 2 * px + py
                copy(6 * i + 3 + j, g_refs[i].at[1 - c, q], g_refs[i].at[1 - c, q], sibling).wait_recv()
        for cp in first + passed:
            cp.wait_send()

    got = _pcall(
        body,
        name="gather_weights",
        in_specs=[ANY] * nw,
        out_specs=[ANY] * nw,
        out_shape=[jax.ShapeDtypeStruct((s.shape[0], N_CHIPS) + s.shape[1:], s.dtype) for s in shards],
        scratch_shapes=[pltpu.SemaphoreType.DMA((6 * nw,)), pltpu.SemaphoreType.DMA((6 * nw,))],
    )(*shards)
    my_chip = 2 * lax.axis_index("x") + lax.axis_index("y")
    return [lax.dynamic_update_slice(g, s[:, None], (0, my_chip, 0, 0)) for g, s in zip(got, shards)]


def _swap_halves_to_sibling(grs):
    nw = len(grs)

    def body(*refs):
        g_refs, t_refs, (send_sems, recv_sems) = refs[:nw], refs[nw:2 * nw], refs[2 * nw:]
        x, y, c = _place()
        cps = []
        for i in range(nw):
            hh = grs[i].shape[1] // 2
            cps.append(pltpu.make_async_remote_copy(
                src_ref=g_refs[i].at[:, pl.ds((1 - c) * hh, hh)], dst_ref=t_refs[i], send_sem=send_sems.at[i],
                recv_sem=recv_sems.at[i], device_id=(x, y, 1 - c), device_id_type=MESH))
            cps[-1].start()
        for cp in cps:
            cp.wait()

    return _pcall(
        body,
        name="grad_d2d",
        in_specs=[ANY] * nw,
        out_specs=[ANY] * nw,
        out_shape=[jax.ShapeDtypeStruct((g.shape[0], g.shape[1] // 2, g.shape[2]), g.dtype) for g in grs],
        scratch_shapes=[pltpu.SemaphoreType.DMA((nw,)), pltpu.SemaphoreType.DMA((nw,))],
    )(*grs)


def _add_half(gr, t1, place, name):
    nq, r, wd = gr.shape
    hh = r // 2
    tr = _tile(hh, 512, 16)
    nb = hh // tr

    def body(p_ref, g_ref, t_ref, o_ref):
        o_ref[...] = (g_ref[...] + t_ref[...]).astype(BF16)

    return _pcall(
        body,
        name=name,
        grid_spec=pltpu.PrefetchScalarGridSpec(
            num_scalar_prefetch=1,
            grid=(nq, nb),
            in_specs=[
                pl.BlockSpec((1, tr, wd), lambda q, i, p_ref: (q, p_ref[0] * nb + i, 0)),
                pl.BlockSpec((1, tr, wd), lambda q, i, p_ref: (q, i, 0)),
            ],
            out_specs=pl.BlockSpec((1, tr, wd), lambda q, i, p_ref: (q, i, 0)),
        ),
        out_shape=jax.ShapeDtypeStruct((nq, hh, wd), BF16),
        compiler_params=_params(("parallel", "parallel")),
    )(place, gr, t1)


def _scatter_partials(ps):
    nw = len(ps)

    def body(*refs):
        p_refs, t_refs, (send_sems, recv_sems) = refs[:nw], refs[nw:2 * nw], refs[2 * nw:]
        x, y, c = _place()
        cps = []
        for i in range(nw):
            for j, (px, py) in enumerate(_other_chips(x, y)):
                cps.append(pltpu.make_async_remote_copy(
                    src_ref=p_refs[i].at[2 * px + py], dst_ref=t_refs[i].at[j], send_sem=send_sems.at[3 * i + j],
                    recv_sem=recv_sems.at[3 * i + j], device_id=(px, py, c), device_id_type=MESH))
                cps[-1].start()
        for cp in cps:
            cp.wait()

    return _pcall(
        body,
        name="grad_ici",
        in_specs=[ANY] * nw,
        out_specs=[ANY] * nw,
        out_shape=[jax.ShapeDtypeStruct((3,) + p.shape[1:], p.dtype) for p in ps],
        scratch_shapes=[pltpu.SemaphoreType.DMA((3 * nw,)), pltpu.SemaphoreType.DMA((3 * nw,))],
    )(*ps)


def _add_partials(gr, t1, t2, place, name):
    nq, r, wd = gr.shape
    hh = r // 2
    tr = _tile(hh, 512, 16)
    nb = hh // tr

    def body(p_ref, g_ref, s_ref, t_ref, o_ref):
        o_ref[...] = (((g_ref[0] + s_ref[0]) + t_ref[0].astype(F32)) + t_ref[1].astype(F32)) + t_ref[2].astype(F32)

    return _pcall(
        body,
        name=name,
        grid_spec=pltpu.PrefetchScalarGridSpec(
            num_scalar_prefetch=1,
            grid=(nb,),
            in_specs=[
                pl.BlockSpec((1, tr, wd), lambda i, p_ref: (p_ref[1], p_ref[0] * nb + i, 0)),
                pl.BlockSpec((1, tr, wd), lambda i, p_ref: (p_ref[1], i, 0)),
                pl.BlockSpec((3, tr, wd), lambda i, p_ref: (0, i, 0)),
            ],
            out_specs=pl.BlockSpec((tr, wd), lambda i, p_ref: (i, 0)),
        ),
        out_shape=jax.ShapeDtypeStruct((hh, wd), F32),
        compiler_params=_params(("parallel",)),
    )(place, gr, t1, t2)


def _join_halves(halves):
    nw = len(halves)

    def body(*refs):
        h_refs, f_refs, (send_sems, recv_sems) = refs[:nw], refs[nw:2 * nw], refs[2 * nw:]
        x, y, c = _place()

        def copy(i, rows):
            hh = halves[i].shape[0]
            return pltpu.make_async_remote_copy(src_ref=h_refs[i], dst_ref=f_refs[i].at[pl.ds(rows * hh, hh)],
                                                send_sem=send_sems.at[i], recv_sem=recv_sems.at[i],
                                                device_id=(x, y, 1 - c), device_id_type=MESH)

        for i in range(nw):
            copy(i, c).start()
        for i in range(nw):
            copy(i, c).wait_send()
            copy(i, 1 - c).wait_recv()

    got = _pcall(
        body,
        name="grad_join",
        in_specs=[ANY] * nw,
        out_specs=[ANY] * nw,
        out_shape=[jax.ShapeDtypeStruct((2 * h.shape[0], h.shape[1]), h.dtype) for h in halves],
        scratch_shapes=[pltpu.SemaphoreType.DMA((nw,)), pltpu.SemaphoreType.DMA((nw,))],
    )(*halves)
    c = lax.axis_index("c")
    return [lax.dynamic_update_slice(g, h, (c * h.shape[0], 0)) for g, h in zip(got, halves)]


N_DEV = 8


def _allreduce_small(buf):
    rows, wd = buf.shape
    n = rows // N_DEV

    def body(x_ref, o_ref, stage_ref, send_sems, recv_sems):
        x, y, c = _place()
        me = 4 * x + 2 * y + c

        def peer(k):
            return (x ^ (k >> 2), y ^ ((k >> 1) & 1), c ^ (k & 1))

        def chunk(ref, d):
            return ref.at[pl.ds(pl.multiple_of(d * n, 8), n)]

        first = []
        for k in range(1, N_DEV):
            px, py, pc = peer(k)
            cp = pltpu.make_async_remote_copy(src_ref=chunk(x_ref, 4 * px + 2 * py + pc), dst_ref=stage_ref.at[k],
                                              send_sem=send_sems.at[k - 1], recv_sem=recv_sems.at[k - 1],
                                              device_id=(px, py, pc), device_id_type=MESH)
            cp.start()
            first.append(cp)
        acc = chunk(x_ref, me)[...]
        for k in range(1, N_DEV):
            first[k - 1].wait_recv()
            acc = acc + stage_ref[k]
        chunk(o_ref, me)[...] = acc
        second = []
        for k in range(1, N_DEV):
            cp = pltpu.make_async_remote_copy(src_ref=chunk(o_ref, me), dst_ref=chunk(o_ref, me),
                                              send_sem=send_sems.at[6 + k], recv_sem=recv_sems.at[6 + k],
                                              device_id=peer(k), device_id_type=MESH)
            cp.start()
            second.append(cp)
        for k in range(1, N_DEV):
            px, py, pc = peer(k)
            them = 4 * px + 2 * py + pc
            pltpu.make_async_remote_copy(src_ref=chunk(o_ref, them), dst_ref=chunk(o_ref, them),
                                         send_sem=send_sems.at[6 + k], recv_sem=recv_sems.at[6 + k],
                                         device_id=(px, py, pc), device_id_type=MESH).wait_recv()
        for cp in first + second:
            cp.wait_send()

    vmem = pl.BlockSpec(memory_space=pltpu.VMEM)
    return _pcall(
        body,
        name="allreduce_small",
        in_specs=[vmem],
        out_specs=vmem,
        out_shape=jax.ShapeDtypeStruct((rows, wd), F32),
        scratch_shapes=[pltpu.VMEM((N_DEV, n, wd), F32), pltpu.SemaphoreType.DMA((14,)),
                        pltpu.SemaphoreType.DMA((14,))],
    )(buf)


@functools.partial(jax.custom_vjp, nondiff_argnums=(1,))
def split_cols(z, spec):
    outs = []
    for segs in spec[0]:
        parts = [z[:, s:s + n] for s, n in segs]
        outs.append(parts[0] if len(parts) == 1 else jnp.concatenate(parts, axis=1))
    return tuple(outs)


def _split_cols_fwd(z, spec):
    return split_cols(z, spec), None


def _split_cols_bwd(spec, _, gs):
    pieces, width = spec
    segs = []
    for p, piece in enumerate(pieces):
        off = 0
        for s, n in piece:
            segs.append((s, n, p, off))
            off += n
    rows, parts, pos = gs[0].shape[0], [], 0
    for s, n, p, off in sorted(segs):
        if s > pos:
            parts.append(jnp.zeros((rows, s - pos), gs[0].dtype))
        parts.append(gs[p] if (off == 0 and n == gs[p].shape[1]) else gs[p][:, off:off + n])
        pos = s + n
    if pos < width:
        parts.append(jnp.zeros((rows, width - pos), gs[0].dtype))
    return (jnp.concatenate(parts, axis=1),)


split_cols.defvjp(_split_cols_fwd, _split_cols_bwd)


def _contiguous(sizes):
    pieces, off = [], 0
    for n in sizes:
        pieces.append(((off, n),))
        off += n
    return tuple(pieces), off


def _rope(t, cos, sin):
    half = t.shape[-1] // 2
    t1, t2 = t[..., :half], t[..., half:]
    return jnp.concatenate([t1 * cos - t2 * sin, t2 * cos + t1 * sin], axis=-1)


def _s5_mixer(u, sp, l, w16, toks):
    g, p_, h = SSM_GROUPS, SSM_STATE, SSM_GROUP_CH
    lam_re, lam_im = sp["ssm_lam_re", l], sp["ssm_lam_im", l]
    step = jnp.exp(sp["ssm_log_step", l])[:, None]
    mag, ang = jnp.exp(lam_re * step), lam_im * step
    lbr, lbi = mag * jnp.cos(ang), mag * jnp.sin(ang)
    den = lam_re * lam_re + lam_im * lam_im
    nr, ni = lbr - 1.0, lbi
    qr, qi = (nr * lam_re + ni * lam_im) / den, (ni * lam_re - nr * lam_im) / den
    b_re, b_im = sp["ssm_b_re", l], sp["ssm_b_im", l]
    bbr = qr[..., None] * b_re - qi[..., None] * b_im
    bbi = qr[..., None] * b_im + qi[..., None] * b_re
    nb = g // SSM_PACK
    eye = jnp.eye(SSM_PACK, dtype=F32)

    def in_blocks(t):
        return jnp.einsum("jgph,gk->jghkp", t.reshape(nb, SSM_PACK, p_, h), eye).reshape(nb, SSM_PACK * h, SSM_PACK * p_)

    def out_blocks(t):
        return jnp.einsum("jghp,gk->jgpkh", t.reshape(nb, SSM_PACK, h, p_), eye).reshape(nb, SSM_PACK * p_, SSM_PACK * h)

    nm = "ssm%d" % l
    ur = bd_matmul(u, in_blocks(bbr), nm + "_bur")
    ui = bd_matmul(u, in_blocks(bbi), nm + "_bui")
    xr, xi = s5_scan(ur, ui, lbr.reshape(1, g * p_), lbi.reshape(1, g * p_), nm + "_scan")
    y = (bd_matmul(xr, out_blocks(sp["ssm_c_re", l]), nm + "_cr") + bd_matmul(xi, out_blocks(-sp["ssm_c_im", l]), nm + "_ci")
         + sp["ssm_d", l].reshape(1, g * h) * u)
    y = jax.nn.gelu(y)
    zz = matmul_w(y, w16["ssm_w_glu", l], toks["ssm_w_glu", l], nm + "_glu")
    z_out, z_gate = split_cols(zz, _contiguous((D_MODEL, D_MODEL)))
    return z_out * jax.nn.sigmoid(z_gate)


def _mla_mixer(q_lat, kv_lat, k_rope, cos, sin, sp, l, w16, toks):
    s = q_lat.shape[0]
    nm = "mla%d" % l
    dqk = MLA_NOPE + MLA_ROPE
    qt = matmul_w_t(rmsnorm(q_lat, sp["mla_q_norm", l], nm + "_qn"), w16["mla_w_uq", l], toks["mla_w_uq", l],
                    nm + "_uq").reshape(MLA_HEADS, dqk, s)
    half = MLA_ROPE // 2
    q1, q2 = qt[:, MLA_NOPE:MLA_NOPE + half], qt[:, MLA_NOPE + half:]
    ct, st = cos.T[None], sin.T[None]
    qt = jnp.concatenate([qt[:, :MLA_NOPE], q1 * ct - q2 * st, q2 * ct + q1 * st], axis=1).reshape(MLA_HEADS * dqk, s)
    kvt = matmul_w_t(rmsnorm(kv_lat, sp["mla_kv_norm", l], nm + "_kvn"), w16["mla_w_ukv", l], toks["mla_w_ukv", l],
                     nm + "_ukv").reshape(MLA_HEADS, MLA_NOPE + MLA_V, s)
    k_pe = _rope(k_rope, cos, sin)
    k = jnp.concatenate([kvt[:, :MLA_NOPE].transpose(0, 2, 1),
                         jnp.broadcast_to(k_pe[None], (MLA_HEADS, s, MLA_ROPE))], axis=-1)
    vt = kvt[:, MLA_NOPE:].reshape(MLA_HEADS * MLA_V, s)
    ot = attention(qt, k, vt, 1.0 / math.sqrt(dqk), True, nm + "_att")
    return matmul_w_at(ot, w16["mla_w_o", l], toks["mla_w_o", l], nm + "_o")


def _hgrn_mixer(q, f_logit, i_in, g, lb, sp, l, w16, toks):
    s = q.shape[0]
    nm = "hg%d" % l
    f = lb + (1.0 - lb) * jax.nn.sigmoid(f_logit)
    o = hgrn_core(jax.nn.silu(q), 1.0 - f, i_in, jnp.log(f), nm + "_core")
    o = rmsnorm(o.reshape(s * HG_HEADS, HG_DV), sp["hg_g_norm", l], nm + "_gn").reshape(s, HG_HEADS * HG_DV)
    return matmul_w(o * jax.nn.silu(g), w16["hg_w_o", l], toks["hg_w_o", l], nm + "_o")


def _cross_attention(h, mem_n, l, w16, toks):
    s, m = h.shape[0], mem_n.shape[0]
    nm = "xat%d" % l
    xw = X_HEADS * X_HEAD_DIM
    qt = matmul_w_t(h, w16["x_w_q", l], toks["x_w_q", l], nm + "_q")
    kv = matmul_w(mem_n, w16["x_w_kv", l], toks["x_w_kv", l], nm + "_kv")
    k, v = split_cols(kv, _contiguous((xw, xw)))
    k = k.reshape(m, X_HEADS, X_HEAD_DIM).transpose(1, 0, 2)
    ot = attention(qt, k, v.T, 1.0 / math.sqrt(X_HEAD_DIM), False, nm + "_att")
    return matmul_w_at(ot, w16["x_w_o", l], toks["x_w_o", l], nm + "_o")


def _in_spec():
    hw = HG_HEADS * HG_DK
    sizes = (SSM_WIDTH, MLA_Q_RANK, MLA_KV_RANK, MLA_ROPE, hw, hw, hw, hw, D_MODEL, D_MODEL, D_MODEL)
    pieces, lo = [], 0
    for n in sizes:
        segs, c = [], lo
        while c < lo + n:
            end = min(lo + n, (c // D_IN_SHARD + 1) * D_IN_SHARD)
            segs.append((c + (D_IN_BLOCK - D_IN_SHARD) * (c // D_IN_SHARD), end - c))
            c = end
        pieces.append(tuple(segs))
        lo += n
    assert lo == D_IN
    return tuple(pieces), N_CHIPS * D_IN_BLOCK


def _layer(x, mem, cos, sin, lb, l, sp, w16, toks):
    s = x.shape[0]
    nm = "l%d" % l
    h = rmsnorm(x, sp["norm_mix", l], nm + "_nmix")
    z = matmul_w(h, w16["w_in", l], toks["w_in", l], nm + "_in")
    u, q_lat, kv_lat, k_rope, hg_q, hg_f, hg_i, hg_g, l_ssm, l_mla, l_hg = split_cols(z, _in_spec())
    y_ssm = _s5_mixer(u, sp, l, w16, toks)
    y_mla = _mla_mixer(q_lat, kv_lat, k_rope, cos, sin, sp, l, w16, toks)
    y_hg = _hgrn_mixer(hg_q, hg_f, hg_i, hg_g, lb, sp, l, w16, toks)
    x = x + gated_merge_out((y_ssm, y_mla, y_hg), (l_ssm, l_mla, l_hg), w16["w_out", l], toks["w_out", l], nm + "_out")
    hc = rmsnorm(x, sp["norm_cross", l], nm + "_ncross")
    mem_n = rmsnorm(mem, sp["norm_mem", l], nm + "_nmem")
    x = x + _cross_attention(hc, mem_n, l, w16, toks)
    hf = rmsnorm(x, sp["norm_ffn", l], nm + "_nffn")
    gt = matmul_w(hf, w16["ffn_w_gate", l], toks["ffn_w_gate", l], nm + "_gate")
    up = matmul_w(hf, w16["ffn_w_up", l], toks["ffn_w_up", l], nm + "_up")
    return x + swiglu_down(gt, up, w16["ffn_w_down", l], toks["ffn_w_down", l], nm + "_down")


def _loss_fn(sp, toks, x, w16, mem, pos, target):
    half = MLA_ROPE // 2
    inv_freq = ROPE_THETA ** (-jnp.arange(half, dtype=F32) / half)
    ang = pos.astype(F32)[:, None] * inv_freq
    cos, sin = jnp.cos(ang), jnp.sin(ang)
    lb_p = jax.nn.softmax(sp["hg_lb", ALL], axis=0)
    lower = jnp.cumsum(lb_p, axis=0) - lb_p[0:1]
    for l in range(DEPTH):
        x = _layer(x, mem, cos, sin, lower[l], l, sp, w16, toks)
    y = rmsnorm(x, sp["norm_final", ALL], "nfinal")
    return 0.5 * jnp.sum(jnp.mean(jnp.square(y - target), axis=-1))


WEIGHTS = ["norm_mix", "w_in", "ssm_lam_re", "ssm_lam_im", "ssm_b_re", "ssm_b_im", "ssm_c_re", "ssm_c_im", "ssm_d",
           "ssm_log_step", "ssm_w_glu", "mla_q_norm", "mla_kv_norm", "mla_w_uq", "mla_w_ukv", "mla_w_o", "hg_lb",
           "hg_g_norm", "hg_w_o", "w_out", "norm_cross", "norm_mem", "x_w_q", "x_w_kv", "x_w_o", "norm_ffn",
           "ffn_w_gate_up", "ffn_w_down", "norm_final"]
INPUTS = ["x", "mem", "positions"] + WEIGHTS + ["loss_target"] + ["m_" + n for n in WEIGHTS] + ["v_" + n for n in WEIGHTS]
BIG = [("w_in", 2), ("ssm_w_glu", 2), ("mla_w_uq", 2), ("mla_w_ukv", 2), ("mla_w_o", 2), ("hg_w_o", 2), ("w_out", 1),
       ("x_w_q", 1), ("x_w_kv", 1), ("x_w_o", 2), ("ffn_w_gate_up", 2), ("ffn_w_down", 1)]
SMALL = [n for n in WEIGHTS if n not in dict(BIG)]
WHOLE = ("hg_lb", "norm_final")
ALL = -1
SMALL_W = 128
BLOCKED = ("w_in", "ffn_w_gate_up")


def _pack(arrs, rows, width, dtype):
    flat = jnp.concatenate([a.astype(dtype).reshape(-1) for a in arrs])
    return jnp.pad(flat, (0, rows * width - flat.shape[0])).reshape(rows, width)


def _unpack(buf, like):
    flat = buf.reshape(-1)
    out, off = [], 0
    for a in like:
        n = math.prod(a.shape)
        out.append(flat[off:off + n].reshape(a.shape))
        off += n
    return out


def kernel(
    x, mem, positions, norm_mix, w_in, ssm_lam_re, ssm_lam_im, ssm_b_re, ssm_b_im, ssm_c_re, ssm_c_im, ssm_d,
    ssm_log_step, ssm_w_glu, mla_q_norm, mla_kv_norm, mla_w_uq, mla_w_ukv, mla_w_o, hg_lb, hg_g_norm, hg_w_o,
    w_out, norm_cross, norm_mem, x_w_q, x_w_kv, x_w_o, norm_ffn, ffn_w_gate_up, ffn_w_down, norm_final,
    loss_target, m_norm_mix, m_w_in, m_ssm_lam_re, m_ssm_lam_im, m_ssm_b_re, m_ssm_b_im, m_ssm_c_re, m_ssm_c_im,
    m_ssm_d, m_ssm_log_step, m_ssm_w_glu, m_mla_q_norm, m_mla_kv_norm, m_mla_w_uq, m_mla_w_ukv, m_mla_w_o,
    m_hg_lb, m_hg_g_norm, m_hg_w_o, m_w_out, m_norm_cross, m_norm_mem, m_x_w_q, m_x_w_kv, m_x_w_o, m_norm_ffn,
    m_ffn_w_gate_up, m_ffn_w_down, m_norm_final, v_norm_mix, v_w_in, v_ssm_lam_re, v_ssm_lam_im, v_ssm_b_re,
    v_ssm_b_im, v_ssm_c_re, v_ssm_c_im, v_ssm_d, v_ssm_log_step, v_ssm_w_glu, v_mla_q_norm, v_mla_kv_norm,
    v_mla_w_uq, v_mla_w_ukv, v_mla_w_o, v_hg_lb, v_hg_g_norm, v_hg_w_o, v_w_out, v_norm_cross, v_norm_mem,
    v_x_w_q, v_x_w_kv, v_x_w_o, v_norm_ffn, v_ffn_w_gate_up, v_ffn_w_down, v_norm_final):
    given = locals()
    a = {n: given[n] for n in INPUTS}
    x, mem, pos, target = a["x"][0], a["mem"][0], a["positions"][0], a["loss_target"][0]
    place = jnp.stack([lax.axis_index("c"), 2 * lax.axis_index("x") + lax.axis_index("y")]).astype(jnp.int32)

    def shard16(n):
        w = a[n].astype(BF16)
        if n == "w_in":
            w = jnp.pad(w, ((0, 0), (0, 0), (0, D_IN_BLOCK - D_IN_SHARD)))
        return w

    gathered = _gather_weights([shard16(n) for n, _ in BIG])
    w16, toks = {}, {}
    for (n, ax), got in zip(BIG, gathered):
        for l in range(DEPTH):
            if n in BLOCKED:
                full = got[l]
            elif ax == 2:
                full = jnp.concatenate([got[l, q] for q in range(N_CHIPS)], axis=1)
            else:
                full = got[l].reshape(N_CHIPS * got.shape[2], got.shape[3])
            parts = {n: full}
            if n == "ffn_w_gate_up":
                half = N_CHIPS // 2
                parts = {"ffn_w_gate": full[:half], "ffn_w_up": full[half:]}
            for name, w in parts.items():
                w16[name, l] = w
                toks[name, l] = jnp.zeros(w.shape, F32)
    sp = {}
    for n in SMALL:
        if n in WHOLE:
            sp[n, ALL] = a[n]
        else:
            for l in range(DEPTH):
                sp[n, l] = a[n][l]

    loss, (g_sp, g_tok, g_x) = jax.value_and_grad(_loss_fn, argnums=(0, 1, 2))(sp, toks, x, w16, mem, pos, target)
    g_small = {n: g_sp[n, ALL] if n in WHOLE else jnp.stack([g_sp[n, l] for l in range(DEPTH)]) for n in SMALL}

    grads, names = [], []
    for n, ax in BIG:
        for l in range(DEPTH):
            if n == "ffn_w_gate_up":
                g = jnp.concatenate([g_tok["ffn_w_gate", l], g_tok["ffn_w_up", l]], axis=0)
            else:
                g = g_tok[n, l]
            if n in BLOCKED:
                grads.append(g)
            elif ax == 2:
                grads.append(g.reshape(g.shape[0], N_CHIPS, g.shape[1] // N_CHIPS).transpose(1, 0, 2))
            else:
                grads.append(g.reshape(N_CHIPS, g.shape[0] // N_CHIPS, g.shape[1]))
            names.append("%s%d" % (n, l))
    from_sibling = _swap_halves_to_sibling(grads)
    chip_sums = [_add_half(g, t, place, "grad_add_half_" + n) for n, g, t in zip(names, grads, from_sibling)]
    from_chips = _scatter_partials(chip_sums)
    totals = [_add_partials(g, t1, t2, place, "grad_add_partials_" + n)
              for n, g, t1, t2 in zip(names, grads, from_sibling, from_chips)]
    joined = _join_halves(totals)
    g_big = {n: jnp.stack([joined[DEPTH * i + l][:, :a[n].shape[2]] for l in range(DEPTH)])
             for i, (n, _) in enumerate(BIG)}

    small_like = [a[n] for n in SMALL] + [jnp.zeros((1,), F32)]
    n_small = sum(math.prod(t.shape) for t in small_like)
    srows = -(-n_small // (SMALL_W * 8 * N_DEV)) * 8 * N_DEV
    reduced = _allreduce_small(_pack([g_small[n] for n in SMALL] + [loss.reshape(1)], srows, SMALL_W, F32))
    red = _unpack(reduced, small_like)
    g_sm = dict(zip(SMALL, red[:-1]))
    loss = red[-1].reshape(())

    out_g, out_d, out_m, out_v = {}, {}, {}, {}
    for n, _ in BIG:
        shp = a[n].shape
        two = lambda t: t.reshape(shp[0] * shp[1], shp[2])
        d, m, v = _adamw_call(two(a[n]), two(g_big[n]), two(a["m_" + n]), two(a["v_" + n]), "adamw_" + n)
        out_g[n], out_d[n], out_m[n], out_v[n] = g_big[n], d.reshape(shp), m.reshape(shp), v.reshape(shp)
    small_arrs = [a[n] for n in SMALL]
    d, m, v = _adamw_call(
        _pack(small_arrs, srows, SMALL_W, F32), reduced,
        _pack([a["m_" + n] for n in SMALL], srows, SMALL_W, F32),
        _pack([a["v_" + n] for n in SMALL], srows, SMALL_W, F32), "adamw_small")
    for n, gg, dd, mm_, vv in zip(SMALL, red[:-1], _unpack(d, small_arrs), _unpack(m, small_arrs), _unpack(v, small_arrs)):
        out_g[n], out_d[n], out_m[n], out_v[n] = gg, dd, mm_, vv

    return (loss, g_x[None], *[out_g[n] for n in WEIGHTS], *[out_d[n] for n in WEIGHTS],
            *[out_m[n] for n in WEIGHTS], *[out_v[n] for n in WEIGHTS])
```

```python
import functools
import math

import jax
import jax.numpy as jnp
from jax import lax
from jax.experimental import pallas as pl
from jax.experimental.pallas import tpu as pltpu

F32 = jnp.float32
BF16 = jnp.bfloat16
MESH = pl.DeviceIdType.MESH

D_MODEL = 1024
DEPTH = 2
RMS_EPS = 1e-6
SSM_GROUPS, SSM_GROUP_CH, SSM_STATE = 32, 16, 64
SSM_WIDTH = SSM_GROUPS * SSM_GROUP_CH
SSM_LANES = SSM_GROUPS * SSM_STATE
SSM_PACK = 8
MLA_HEADS, MLA_Q_RANK, MLA_KV_RANK, MLA_NOPE, MLA_ROPE, MLA_V = 8, 512, 256, 64, 32, 64
ROPE_THETA = 10000.0
HG_HEADS, HG_DK, HG_DV, HG_CHUNK = 4, 128, 128, 64
HG_SUB_FWD, HG_SUB_BWD = 32, 16
X_HEADS, X_HEAD_DIM = 4, 128
D_FF = 2816
D_IN = 6432
N_CHIPS = 4
D_IN_SHARD = D_IN // N_CHIPS
D_IN_BLOCK = 1664
MASK_VALUE = -1e30

ADAM_LR, ADAM_B1, ADAM_B2, ADAM_EPS, ADAM_WD, ADAM_STEP = 0.001, 0.9, 0.999, 1e-08, 0.01, 10

VMEM_LIMIT = 48 * 1024 * 1024


def _pcall(body, **kw):
    return pl.pallas_call(body, **kw)


def _params(sem):
    return pltpu.CompilerParams(dimension_semantics=sem, vmem_limit_bytes=VMEM_LIMIT)


def _tile(n, cap, align=128):
    if n <= cap:
        return n
    t = (cap // align) * align
    while t >= align:
        if n % t == 0:
            return t
        t -= align
    return n


MM_VMEM_BUDGET = 36 * 1024 * 1024
MM_TILE_CAP = 2048


def _divisors(n, cap, align=128):
    out = [n] if n <= cap else []
    out += [t for t in range(align, min(n, cap + 1), align) if n % t == 0]
    return sorted(set(out), reverse=True)


@functools.lru_cache(maxsize=None)
def _mm_tiles(m, n, k, a_bytes, b_bytes, o_bytes, n_unit, k_unit):
    best = None
    for tm in _divisors(m, MM_TILE_CAP):
        for tn in _divisors(n_unit, MM_TILE_CAP):
            for tk in _divisors(k_unit, MM_TILE_CAP):
                nk = k // tk
                need = 2 * (tm * tk * a_bytes + tk * tn * b_bytes + tm * tn * o_bytes)
                need += tm * tn * 4 * (2 if nk > 1 else 1)
                need += (tm * tk * 2 if a_bytes > 2 else 0) + (tk * tn * 2 if b_bytes > 2 else 0)
                if need > MM_VMEM_BUDGET:
                    continue
                key = ((m // tm) * (n // tn) * nk, nk, -tn)
                if best is None or key < best[0]:
                    best = (key, (tm, tn, tk))
    assert best is not None, (m, n, k)
    return best[1]


def _mm(a, b, *, ta=False, tb=False, out_dtype=F32, name="mm", b_blocks=False, out_blocks=0, pro=None, epi=None,
        save_left=False):
    if ta:
        K, M = a.shape
    else:
        M, K = a.shape
    n_unit = k_unit = None
    if b_blocks:
        nb, rows, cols = b.shape
        if tb:
            N, Kb, k_unit = rows, nb * cols, cols
        else:
            Kb, N, n_unit = rows, nb * cols, cols
    elif tb:
        N, Kb = b.shape
    else:
        Kb, N = b.shape
    if out_blocks:
        assert not b_blocks and N % out_blocks == 0
        n_unit = N // out_blocks
    assert K == Kb, (a.shape, b.shape, ta, tb)
    n_pro = len(pro) - 1 if pro else 0
    n_epi = len(epi) - 1 if epi else 0
    tm, tn, tk = _mm_tiles(M, N, K, a.dtype.itemsize * (1 + n_pro), b.dtype.itemsize,
                           jnp.dtype(out_dtype).itemsize * max(1, 2 * n_epi), n_unit or N, k_unit or K)
    nk = K // tk
    dims = (((0,) if ta else (1,), (1,) if tb else (0,)), ((), ()))
    n_in = 2 + n_pro + n_epi
    n_out = max(1, n_epi) + (1 if save_left else 0)
    assert not save_left or (pro and not epi and tn == N)

    def body(*refs):
        ins, outs, acc = refs[:n_in], refs[n_in:n_in + n_out], refs[n_in + n_out:]
        b_ref = ins[1 + n_pro]
        av = (pro[0](*[r[...] for r in ins[:1 + n_pro]]) if pro else ins[0][...]).astype(BF16)
        if save_left:
            outs[1][...] = av
        bv = b_ref[...].astype(BF16)
        part = lax.dot_general(av, bv, dims, preferred_element_type=F32)

        def finish(total):
            if epi:
                for o_ref, r in zip(outs, epi[0](total, *[e[...] for e in ins[2 + n_pro:]])):
                    o_ref[...] = r.astype(o_ref.dtype)
            else:
                outs[0][...] = total.astype(outs[0].dtype)

        if nk == 1:
            finish(part)
        else:
            acc_ref = acc[0]
            k = pl.program_id(2)

            @pl.when(k == 0)
            def _():
                acc_ref[...] = part

            @pl.when(k > 0)
            def _():
                acc_ref[...] += part

            @pl.when(k == nk - 1)
            def _():
                finish(acc_ref[...])

    a_spec = pl.BlockSpec((tk, tm), lambda i, j, k: (k, i)) if ta else pl.BlockSpec((tm, tk), lambda i, j, k: (i, k))
    if b_blocks and tb:
        r = k_unit // tk
        b_spec = pl.BlockSpec((None, tn, tk), lambda i, j, k: (k // r, j, k % r))
    elif b_blocks:
        r = n_unit // tn
        b_spec = pl.BlockSpec((None, tk, tn), lambda i, j, k: (j // r, k, j % r))
    elif tb:
        b_spec = pl.BlockSpec((tn, tk), lambda i, j, k: (j, k))
    else:
        b_spec = pl.BlockSpec((tk, tn), lambda i, j, k: (k, j))
    if out_blocks:
        r = n_unit // tn
        out_spec = pl.BlockSpec((None, tm, tn), lambda i, j, k: (j // r, i, j % r))
        out_shape = jax.ShapeDtypeStruct((out_blocks, M, n_unit), out_dtype)
    else:
        out_spec = pl.BlockSpec((tm, tn), lambda i, j, k: (i, j))
        out_shape = jax.ShapeDtypeStruct((M, N), out_dtype)
    in_specs = [a_spec] * (1 + n_pro) + [b_spec] + [out_spec] * n_epi
    args = [a] + (list(pro[1:]) if pro else []) + [b] + (list(epi[1:]) if epi else [])
    out_specs, out_shapes = ([out_spec] * n_epi, [out_shape] * n_epi) if epi else (out_spec, out_shape)
    if save_left:
        out_specs, out_shapes = [out_spec, a_spec], [out_shape, jax.ShapeDtypeStruct(a.shape, BF16)]
    return _pcall(
        body,
        name=name,
        grid=(M // tm, N // tn, nk),
        in_specs=in_specs,
        out_specs=out_specs,
        out_shape=out_shapes,
        scratch_shapes=[pltpu.VMEM((tm, tn), F32)] if nk > 1 else [],
        compiler_params=_params(("parallel", "parallel", "arbitrary")),
    )(*args)


def _bd_call(a, bc, transpose, name):
    m = a.shape[0]
    nb, ka, kb = bc.shape
    win, wout = (kb, ka) if transpose else (ka, kb)
    assert a.shape[1] == nb * win
    tm = _tile(m, 1024)
    dims = _NT if transpose else (((1,), (0,)), ((), ()))

    def body(a_ref, b_ref, o_ref):
        o_ref[...] = lax.dot_general(a_ref[...].astype(BF16), b_ref[0].astype(BF16), dims, preferred_element_type=F32)

    return _pcall(
        body,
        name=name,
        grid=(m // tm, nb),
        in_specs=[pl.BlockSpec((tm, win), lambda i, j: (i, j)), pl.BlockSpec((1, ka, kb), lambda i, j: (j, 0, 0))],
        out_specs=pl.BlockSpec((tm, wout), lambda i, j: (i, j)),
        out_shape=jax.ShapeDtypeStruct((m, nb * wout), F32),
        compiler_params=_params(("parallel", "parallel")),
    )(a, bc)


def _bd_dw_call(a, g, nb, name):
    m = a.shape[0]
    ka, kb = a.shape[1] // nb, g.shape[1] // nb
    tk = _tile(m, 1024)
    nk = m // tk

    def body(a_ref, g_ref, o_ref):
        part = lax.dot_general(a_ref[...].astype(BF16), g_ref[...].astype(BF16), _TN, preferred_element_type=F32)
        k = pl.program_id(1)

        @pl.when(k == 0)
        def _():
            o_ref[0] = part

        @pl.when(k > 0)
        def _():
            o_ref[0] += part

    return _pcall(
        body,
        name=name,
        grid=(nb, nk),
        in_specs=[pl.BlockSpec((tk, ka), lambda j, k: (k, j)), pl.BlockSpec((tk, kb), lambda j, k: (k, j))],
        out_specs=pl.BlockSpec((1, ka, kb), lambda j, k: (j, 0, 0)),
        out_shape=jax.ShapeDtypeStruct((nb, ka, kb), F32),
        compiler_params=_params(("parallel", "arbitrary")),
    )(a, g)


@functools.partial(jax.custom_vjp, nondiff_argnums=(2,))
def bd_matmul(a, bc, name):
    return _bd_call(a, bc, False, name + "_f")


def _bd_matmul_fwd(a, bc, name):
    return _bd_call(a, bc, False, name + "_f"), (a, bc)


def _bd_matmul_bwd(name, res, g):
    a, bc = res
    return _bd_call(g, bc, True, name + "_da"), _bd_dw_call(a, g, bc.shape[0], name + "_dw")


bd_matmul.defvjp(_bd_matmul_fwd, _bd_matmul_bwd)


@functools.partial(jax.custom_vjp, nondiff_argnums=(3,))
def matmul_w(a, w16, tok, name):
    return _mm(a, w16, b_blocks=w16.ndim == 3, name=name + "_f")


def _matmul_w_fwd(a, w16, tok, name):
    return _mm(a, w16, b_blocks=w16.ndim == 3, name=name + "_f"), (a, w16)


def _matmul_w_bwd(name, res, g):
    a, w16 = res
    blocks = w16.shape[0] if w16.ndim == 3 else 0
    return (_mm(g, w16, tb=True, b_blocks=bool(blocks), name=name + "_da"), jnp.zeros_like(w16),
            _mm(a, g, ta=True, out_blocks=blocks, name=name + "_dw"))


matmul_w.defvjp(_matmul_w_fwd, _matmul_w_bwd)


def _swiglu(gt, up):
    return gt * jax.nn.sigmoid(gt) * up


def _swiglu_bwd(dact, gt, up):
    sg = jax.nn.sigmoid(gt)
    return dact * up * sg * (1.0 + gt * (1.0 - sg)), dact * gt * sg


@functools.partial(jax.custom_vjp, nondiff_argnums=(4,))
def swiglu_down(gt, up, w16, tok, name):
    return _mm(gt, w16, pro=(_swiglu, up), name=name + "_f")


def _swiglu_down_fwd(gt, up, w16, tok, name):
    out, act16 = _mm(gt, w16, pro=(_swiglu, up), save_left=True, name=name + "_f")
    return out, (gt, up, w16, act16)


def _swiglu_down_bwd(name, res, g):
    gt, up, w16, act16 = res
    d_gt, d_up = _mm(g, w16, tb=True, epi=(_swiglu_bwd, gt, up), name=name + "_da")
    return d_gt, d_up, jnp.zeros_like(w16), _mm(act16, g, ta=True, name=name + "_dw")


swiglu_down.defvjp(_swiglu_down_fwd, _swiglu_down_bwd)


def _merge(y1, y2, y3, l1, l2, l3):
    return jax.nn.sigmoid(l1) * y1 + jax.nn.sigmoid(l2) * y2 + jax.nn.sigmoid(l3) * y3


def _merge_bwd(dm, y1, y2, y3, l1, l2, l3):
    dys, dls = [], []
    for y, logit in ((y1, l1), (y2, l2), (y3, l3)):
        s = jax.nn.sigmoid(logit)
        dys.append(dm * s)
        dls.append(dm * y * s * (1.0 - s))
    return (*dys, *dls)


@functools.partial(jax.custom_vjp, nondiff_argnums=(4,))
def gated_merge_out(ys, logits, w16, tok, name):
    return _mm(ys[0], w16, pro=(_merge, *ys[1:], *logits), name=name + "_f")


def _gated_merge_out_fwd(ys, logits, w16, tok, name):
    out, merged16 = _mm(ys[0], w16, pro=(_merge, *ys[1:], *logits), save_left=True, name=name + "_f")
    return out, (ys, logits, w16, merged16)


def _gated_merge_out_bwd(name, res, g):
    ys, logits, w16, merged16 = res
    d = _mm(g, w16, tb=True, epi=(_merge_bwd, *ys, *logits), name=name + "_da")
    return tuple(d[:3]), tuple(d[3:]), jnp.zeros_like(w16), _mm(merged16, g, ta=True, name=name + "_dw")


gated_merge_out.defvjp(_gated_merge_out_fwd, _gated_merge_out_bwd)


@functools.partial(jax.custom_vjp, nondiff_argnums=(3,))
def matmul_w_t(a, w16, tok, name):
    return _mm(w16, a, ta=True, tb=True, name=name + "_f")


def _matmul_w_t_fwd(a, w16, tok, name):
    return _mm(w16, a, ta=True, tb=True, name=name + "_f"), (a, w16)


def _matmul_w_t_bwd(name, res, g):
    a, w16 = res
    return (_mm(g, w16, ta=True, tb=True, name=name + "_da"), jnp.zeros_like(w16),
            _mm(a, g, ta=True, tb=True, name=name + "_dw"))


matmul_w_t.defvjp(_matmul_w_t_fwd, _matmul_w_t_bwd)


@functools.partial(jax.custom_vjp, nondiff_argnums=(3,))
def matmul_w_at(at, w16, tok, name):
    return _mm(at, w16, ta=True, name=name + "_f")


def _matmul_w_at_fwd(at, w16, tok, name):
    return _mm(at, w16, ta=True, name=name + "_f"), (at, w16)


def _matmul_w_at_bwd(name, res, g):
    at, w16 = res
    return _mm(w16, g, tb=True, name=name + "_da"), jnp.zeros_like(w16), _mm(at, g, name=name + "_dw")


matmul_w_at.defvjp(_matmul_w_at_fwd, _matmul_w_at_bwd)


def _rms_fwd_call(x, g, name):
    rows, d = x.shape
    tr = _tile(rows, 512, 8)

    def body(x_ref, g_ref, y_ref):
        xv = x_ref[...]
        rstd = lax.rsqrt(jnp.mean(xv * xv, axis=-1, keepdims=True) + RMS_EPS)
        y_ref[...] = xv * rstd * g_ref[...]

    return _pcall(
        body,
        name=name,
        grid=(rows // tr,),
        in_specs=[pl.BlockSpec((tr, d), lambda i: (i, 0)), pl.BlockSpec((1, d), lambda i: (0, 0))],
        out_specs=pl.BlockSpec((tr, d), lambda i: (i, 0)),
        out_shape=jax.ShapeDtypeStruct((rows, d), F32),
        compiler_params=_params(("parallel",)),
    )(x, g.reshape(1, d))


def _rms_bwd_call(x, g, dy, name):
    rows, d = x.shape
    tr = _tile(rows, 512, 8)
    nb = rows // tr

    def body(x_ref, g_ref, dy_ref, dx_ref, dg_ref):
        xv = x_ref[...]
        rstd = lax.rsqrt(jnp.mean(xv * xv, axis=-1, keepdims=True) + RMS_EPS)
        xh = xv * rstd
        dyv = dy_ref[...]
        dyg = dyv * g_ref[...]
        dx_ref[...] = rstd * (dyg - xh * jnp.mean(dyg * xh, axis=-1, keepdims=True))
        dg_ref[...] = jnp.sum((dyv * xh).reshape(tr // 8, 8, d), axis=0)

    dx, dgp = _pcall(
        body,
        name=name,
        grid=(nb,),
        in_specs=[
            pl.BlockSpec((tr, d), lambda i: (i, 0)),
            pl.BlockSpec((1, d), lambda i: (0, 0)),
            pl.BlockSpec((tr, d), lambda i: (i, 0)),
        ],
        out_specs=[pl.BlockSpec((tr, d), lambda i: (i, 0)), pl.BlockSpec((8, d), lambda i: (i, 0))],
        out_shape=[jax.ShapeDtypeStruct((rows, d), F32), jax.ShapeDtypeStruct((nb * 8, d), F32)],
        compiler_params=_params(("parallel",)),
    )(x, g.reshape(1, d), dy)
    return dx, jnp.sum(dgp, axis=0)


@functools.partial(jax.custom_vjp, nondiff_argnums=(2,))
def rmsnorm(x, g, name):
    return _rms_fwd_call(x, g, name + "_f")


def _rmsnorm_fwd(x, g, name):
    return _rms_fwd_call(x, g, name + "_f"), (x, g)


def _rmsnorm_bwd(name, res, dy):
    x, g = res
    return _rms_bwd_call(x, g, dy, name + "_b")


rmsnorm.defvjp(_rmsnorm_fwd, _rmsnorm_bwd)


def _attn_blocks(sq, sk):
    return _tile(sq, 1024), _tile(sk, 512)


def _attn_pt(qs, k_ref, qi, ki, bq, bk, masked, shift):
    st = jnp.dot(k_ref[0].astype(BF16), qs, preferred_element_type=F32)
    if masked:
        kpos = ki * bk + lax.broadcasted_iota(jnp.int32, (bk, bq), 0)
        qpos = qi * bq + lax.broadcasted_iota(jnp.int32, (bk, bq), 1)
        st = jnp.where(kpos <= qpos, st, MASK_VALUE)
    return st if shift is None else jnp.exp(st - shift)


def _attn_pairs(nq, nk, bq, bk, causal, k_major):
    ok = lambda i, j: (not causal) or j * bk <= i * bq + bq - 1
    if k_major:
        pairs = [(i, j) for j in range(nk) for i in range(nq) if ok(i, j)]
    else:
        pairs = [(i, j) for i in range(nq) for j in range(nk) if ok(i, j)]
    qtab = jnp.asarray([p[0] for p in pairs], jnp.int32)
    ktab = jnp.asarray([p[1] for p in pairs], jnp.int32)
    first_q = (lambda j: (j * bk) // bq) if causal else (lambda j: 0)
    last_k = (lambda i: jnp.minimum((i * bq + bq - 1) // bk, nk - 1)) if causal else (lambda i: nk - 1)
    return qtab, ktab, first_q, last_k


def _attn_cases(causal, qi, ki, bq, bk, step):
    if not causal:
        step(False)
        return
    pl.when(ki * bk + bk - 1 <= qi * bq)(functools.partial(step, False))
    pl.when(ki * bk + bk - 1 > qi * bq)(functools.partial(step, True))


def _attn_fwd_call(qt, k, vt, scale, causal, name):
    h, sk, dq = k.shape
    sq = qt.shape[1]
    dv = vt.shape[0] // h
    bq, bk = _attn_blocks(sq, sk)
    nq, nk = sq // bq, sk // bk

    qtab, ktab, _, last_k = _attn_pairs(nq, nk, bq, bk, causal, False)

    def body(qi_ref, ki_ref, q_ref, k_ref, v_ref, o_ref, lse_ref, m_sc, l_sc, acc_sc):
        qi, ki = qi_ref[pl.program_id(1)], ki_ref[pl.program_id(1)]

        @pl.when(ki == 0)
        def _():
            m_sc[...] = jnp.full(m_sc.shape, -jnp.inf, F32)
            l_sc[...] = jnp.zeros(l_sc.shape, F32)
            acc_sc[...] = jnp.zeros(acc_sc.shape, F32)

        def step(masked):
            qs = (q_ref[...] * scale).astype(BF16)
            st = _attn_pt(qs, k_ref, qi, ki, bq, bk, masked, None)
            m_prev = m_sc[...]
            m_new = jnp.maximum(m_prev, jnp.max(st, axis=0, keepdims=True))
            alpha = jnp.exp(m_prev - m_new)
            pt = jnp.exp(st - m_new)
            l_sc[...] = alpha * l_sc[...] + jnp.sum(pt, axis=0, keepdims=True)
            acc_sc[...] = alpha * acc_sc[...] + jnp.dot(v_ref[...].astype(BF16), pt.astype(BF16),
                                                        preferred_element_type=F32)
            m_sc[...] = m_new

        _attn_cases(causal, qi, ki, bq, bk, step)

        @pl.when(ki == last_k(qi))
        def _():
            o_ref[...] = acc_sc[...] / l_sc[...]
            lse_ref[0] = m_sc[...] + jnp.log(l_sc[...])

    qmap = lambda hh, p, qi_ref, ki_ref: (hh, qi_ref[p])
    return _pcall(
        body,
        name=name,
        grid_spec=pltpu.PrefetchScalarGridSpec(
            num_scalar_prefetch=2,
            grid=(h, qtab.shape[0]),
            in_specs=[
                pl.BlockSpec((dq, bq), qmap),
                pl.BlockSpec((1, bk, dq), lambda hh, p, qi_ref, ki_ref: (hh, ki_ref[p], 0)),
                pl.BlockSpec((dv, bk), lambda hh, p, qi_ref, ki_ref: (hh, ki_ref[p])),
            ],
            out_specs=[pl.BlockSpec((dv, bq), qmap),
                       pl.BlockSpec((1, 1, bq), lambda hh, p, qi_ref, ki_ref: (hh, 0, qi_ref[p]))],
            scratch_shapes=[pltpu.VMEM((1, bq), F32), pltpu.VMEM((1, bq), F32), pltpu.VMEM((dv, bq), F32)],
        ),
        out_shape=[jax.ShapeDtypeStruct((h * dv, sq), F32), jax.ShapeDtypeStruct((h, 1, sq), F32)],
        compiler_params=_params(("parallel", "arbitrary")),
    )(qtab, ktab, qt, k, vt)


_TN = (((0,), (0,)), ((), ()))
_NT = (((1,), (1,)), ((), ()))


def _attn_dst(qs, k_ref, v_ref, do_ref, lse_ref, dl_ref, qi, ki, bq, bk, masked):
    pt = _attn_pt(qs, k_ref, qi, ki, bq, bk, masked, lse_ref[0])
    dpt = lax.dot_general(v_ref[...].astype(BF16), do_ref[...].astype(BF16), _TN, preferred_element_type=F32)
    return pt, pt * (dpt - dl_ref[0])


def _attn_dq_call(qt, k, vt, dot, lse, delta, scale, causal, name):
    h, sk, dq = k.shape
    sq = qt.shape[1]
    dv = vt.shape[0] // h
    bq, bk = _attn_blocks(sq, sk)
    nq, nk = sq // bq, sk // bk

    qtab, ktab, _, last_k = _attn_pairs(nq, nk, bq, bk, causal, False)

    def body(qi_ref, ki_ref, q_ref, k_ref, v_ref, do_ref, lse_ref, dl_ref, dq_ref, acc_sc):
        qi, ki = qi_ref[pl.program_id(1)], ki_ref[pl.program_id(1)]

        @pl.when(ki == 0)
        def _():
            acc_sc[...] = jnp.zeros(acc_sc.shape, F32)

        def step(masked):
            qs = (q_ref[...] * scale).astype(BF16)
            _, dst = _attn_dst(qs, k_ref, v_ref, do_ref, lse_ref, dl_ref, qi, ki, bq, bk, masked)
            acc_sc[...] += lax.dot_general(k_ref[0].astype(BF16), dst.astype(BF16), _TN, preferred_element_type=F32)

        _attn_cases(causal, qi, ki, bq, bk, step)

        @pl.when(ki == last_k(qi))
        def _():
            dq_ref[...] = acc_sc[...] * scale

    qmap = lambda hh, p, qi_ref, ki_ref: (hh, qi_ref[p])
    rowmap = lambda hh, p, qi_ref, ki_ref: (hh, 0, qi_ref[p])
    return _pcall(
        body,
        name=name,
        grid_spec=pltpu.PrefetchScalarGridSpec(
            num_scalar_prefetch=2,
            grid=(h, qtab.shape[0]),
            in_specs=[
                pl.BlockSpec((dq, bq), qmap),
                pl.BlockSpec((1, bk, dq), lambda hh, p, qi_ref, ki_ref: (hh, ki_ref[p], 0)),
                pl.BlockSpec((dv, bk), lambda hh, p, qi_ref, ki_ref: (hh, ki_ref[p])),
                pl.BlockSpec((dv, bq), qmap),
                pl.BlockSpec((1, 1, bq), rowmap),
                pl.BlockSpec((1, 1, bq), rowmap),
            ],
            out_specs=pl.BlockSpec((dq, bq), qmap),
            scratch_shapes=[pltpu.VMEM((dq, bq), F32)],
        ),
        out_shape=jax.ShapeDtypeStruct((h * dq, sq), F32),
        compiler_params=_params(("parallel", "arbitrary")),
    )(qtab, ktab, qt, k, vt, dot, lse, delta)


def _attn_dkv_call(qt, k, vt, dot, lse, delta, scale, causal, name):
    h, sk, dq = k.shape
    sq = qt.shape[1]
    dv = vt.shape[0] // h
    bq, bk = _attn_blocks(sq, sk)
    nq, nk = sq // bq, sk // bk

    qtab, ktab, first_q, _ = _attn_pairs(nq, nk, bq, bk, causal, True)

    def body(qi_ref, ki_ref, q_ref, k_ref, v_ref, do_ref, lse_ref, dl_ref, dk_ref, dv_ref, dk_sc, dv_sc):
        qi, ki = qi_ref[pl.program_id(1)], ki_ref[pl.program_id(1)]

        @pl.when(qi == first_q(ki))
        def _():
            dk_sc[...] = jnp.zeros(dk_sc.shape, F32)
            dv_sc[...] = jnp.zeros(dv_sc.shape, F32)

        def step(masked):
            qs = (q_ref[...] * scale).astype(BF16)
            pt, dst = _attn_dst(qs, k_ref, v_ref, do_ref, lse_ref, dl_ref, qi, ki, bq, bk, masked)
            dv_sc[...] += lax.dot_general(do_ref[...].astype(BF16), pt.astype(BF16), _NT, preferred_element_type=F32)
            dk_sc[...] += lax.dot_general(dst.astype(BF16), qs, _NT, preferred_element_type=F32)

        _attn_cases(causal, qi, ki, bq, bk, step)

        @pl.when(qi == nq - 1)
        def _():
            dk_ref[0] = dk_sc[...]
            dv_ref[...] = dv_sc[...]

    qmap = lambda hh, p, qi_ref, ki_ref: (hh, qi_ref[p])
    rowmap = lambda hh, p, qi_ref, ki_ref: (hh, 0, qi_ref[p])
    kmap = lambda hh, p, qi_ref, ki_ref: (hh, ki_ref[p], 0)
    vmap = lambda hh, p, qi_ref, ki_ref: (hh, ki_ref[p])
    return _pcall(
        body,
        name=name,
        grid_spec=pltpu.PrefetchScalarGridSpec(
            num_scalar_prefetch=2,
            grid=(h, qtab.shape[0]),
            in_specs=[
                pl.BlockSpec((dq, bq), qmap),
                pl.BlockSpec((1, bk, dq), kmap),
                pl.BlockSpec((dv, bk), vmap),
                pl.BlockSpec((dv, bq), qmap),
                pl.BlockSpec((1, 1, bq), rowmap),
                pl.BlockSpec((1, 1, bq), rowmap),
            ],
            out_specs=[pl.BlockSpec((1, bk, dq), kmap), pl.BlockSpec((dv, bk), vmap)],
            scratch_shapes=[pltpu.VMEM((bk, dq), F32), pltpu.VMEM((dv, bk), F32)],
        ),
        out_shape=[jax.ShapeDtypeStruct((h, sk, dq), F32), jax.ShapeDtypeStruct((h * dv, sk), F32)],
        compiler_params=_params(("parallel", "arbitrary")),
    )(qtab, ktab, qt, k, vt, dot, lse, delta)


@functools.partial(jax.custom_vjp, nondiff_argnums=(3, 4, 5))
def attention(qt, k, vt, scale, causal, name):
    return _attn_fwd_call(qt, k, vt, scale, causal, name + "_f")[0]


def _attention_fwd(qt, k, vt, scale, causal, name):
    ot, lse = _attn_fwd_call(qt, k, vt, scale, causal, name + "_f")
    return ot, (qt, k, vt, ot, lse)


def _attention_bwd(scale, causal, name, res, dot):
    qt, k, vt, ot, lse = res
    h = k.shape[0]
    delta = jnp.sum((dot * ot).reshape(h, ot.shape[0] // h, ot.shape[1]), axis=1, keepdims=True)
    dqt = _attn_dq_call(qt, k, vt, dot, lse, delta, scale, causal, name + "_dq")
    dk, dvt = _attn_dkv_call(qt, k, vt, dot, lse, delta, scale, causal, name + "_dkv")
    return dqt, dk, dvt


attention.defvjp(_attention_fwd, _attention_bwd)


SCAN_ROWS = 8
SCAN_LANES = 256
SCAN_TBLOCK = 512
SCAN_UNROLL = 4


def _scan_tables(lr, li):
    pr, pi = [lr], [li]
    for _ in range(SCAN_ROWS - 1):
        pr, pi = pr + [pr[-1] * lr - pi[-1] * li], pi + [pr[-1] * li + pi[-1] * lr]
    return jnp.concatenate(pr, axis=0), jnp.concatenate(pi, axis=0)


def _scan_call(ur, ui, lr, li, reverse, name, states=None):
    s, n = ur.shape
    tb = _tile(s, SCAN_TBLOCK, 8)
    nt, nl = s // tb, n // SCAN_LANES
    ntile = tb // SCAN_ROWS
    pr, pi = _scan_tables(lr, li)
    if reverse:
        pr, pi = pr[::-1], pi[::-1]
    shape = (SCAN_ROWS, SCAN_LANES)
    with_states = states is not None
    assert reverse or not with_states

    def body(*refs):
        if with_states:
            ur_ref, ui_ref, pr_ref, pi_ref, sr_ref, si_ref, xr_ref, xi_ref, dr_ref, di_ref, cr_sc, ci_sc, ar_sc, ai_sc = refs
        else:
            ur_ref, ui_ref, pr_ref, pi_ref, xr_ref, xi_ref, cr_sc, ci_sc, ar_sc, ai_sc = refs

        @pl.when(pl.program_id(1) == 0)
        def _():
            for sc in (cr_sc, ci_sc, ar_sc, ai_sc):
                sc[...] = jnp.zeros(shape, F32)

        prv, piv = pr_ref[...], pi_ref[...]
        rows = lax.broadcasted_iota(jnp.int32, shape, 0)

        def powers(k):
            r = (SCAN_ROWS - k) if reverse else (k - 1)
            return jnp.broadcast_to(prv[r:r + 1], shape), jnp.broadcast_to(piv[r:r + 1], shape)

        pw = [powers(k) for k in (1, 2, 4)]

        def tile(i, carry):
            cr, ci, acr, aci = carry
            j = (ntile - 1 - i) if reverse else i
            sl = pl.ds(pl.multiple_of(j * SCAN_ROWS, SCAN_ROWS), SCAN_ROWS)
            xr, xi = ur_ref[sl, :], ui_ref[sl, :]
            for (ar, ai), k in zip(pw, (1, 2, 4)):
                if reverse:
                    keep = rows < SCAN_ROWS - k
                    sr = jnp.where(keep, pltpu.roll(xr, SCAN_ROWS - k, 0), 0.0)
                    si = jnp.where(keep, pltpu.roll(xi, SCAN_ROWS - k, 0), 0.0)
                else:
                    keep = rows >= k
                    sr = jnp.where(keep, pltpu.roll(xr, k, 0), 0.0)
                    si = jnp.where(keep, pltpu.roll(xi, k, 0), 0.0)
                xr, xi = xr + (ar * sr - ai * si), xi + (ar * si + ai * sr)
            xr, xi = xr + (prv * cr - piv * ci), xi + (prv * ci + piv * cr)
            xr_ref[sl, :] = xr
            xi_ref[sl, :] = xi
            if with_states:
                last = rows == SCAN_ROWS - 1
                nr = jnp.where(last, cr, pltpu.roll(xr, SCAN_ROWS - 1, 0))
                ni = jnp.where(last, ci, pltpu.roll(xi, SCAN_ROWS - 1, 0))
                sr, si = sr_ref[sl, :], si_ref[sl, :]
                acr, aci = acr + (sr * nr + si * ni), aci + (sr * ni - si * nr)
            e = 0 if reverse else SCAN_ROWS - 1
            return jnp.broadcast_to(xr[e:e + 1], shape), jnp.broadcast_to(xi[e:e + 1], shape), acr, aci

        def tiles(g, carry):
            for u in range(SCAN_UNROLL):
                carry = tile(g * SCAN_UNROLL + u, carry)
            return carry

        assert ntile % SCAN_UNROLL == 0
        cr, ci, acr, aci = lax.fori_loop(0, ntile // SCAN_UNROLL, tiles,
                                         (cr_sc[...], ci_sc[...], ar_sc[...], ai_sc[...]))
        cr_sc[...] = cr
        ci_sc[...] = ci
        if with_states:
            ar_sc[...] = acr
            ai_sc[...] = aci

            @pl.when(pl.program_id(1) == nt - 1)
            def _():
                dr_ref[...] = acr
                di_ref[...] = aci

    tmap = (lambda l, t: (nt - 1 - t, l)) if reverse else (lambda l, t: (t, l))
    blk = pl.BlockSpec((tb, SCAN_LANES), tmap)
    tab = pl.BlockSpec((SCAN_ROWS, SCAN_LANES), lambda l, t: (0, l))
    full = jax.ShapeDtypeStruct((s, n), F32)
    return _pcall(
        body,
        name=name,
        grid=(nl, nt),
        in_specs=[blk, blk, tab, tab] + ([blk, blk] if with_states else []),
        out_specs=[blk, blk] + ([tab, tab] if with_states else []),
        out_shape=[full, full] + ([jax.ShapeDtypeStruct((SCAN_ROWS, n), F32)] * 2 if with_states else []),
        scratch_shapes=[pltpu.VMEM(shape, F32)] * 4,
        compiler_params=_params(("parallel", "arbitrary")),
    )(ur, ui, pr, pi, *(states if with_states else ()))


@functools.partial(jax.custom_vjp, nondiff_argnums=(4,))
def s5_scan(ur, ui, lr, li, name):
    return tuple(_scan_call(ur, ui, lr, li, False, name + "_f"))


def _s5_scan_fwd(ur, ui, lr, li, name):
    xr, xi = _scan_call(ur, ui, lr, li, False, name + "_f")
    return (xr, xi), (xr, xi, lr, li)


def _s5_scan_bwd(name, res, g):
    xr, xi, lr, li = res
    gr, gi, dlr, dli = _scan_call(g[0], g[1], lr, -li, True, name + "_b", states=(xr, xi))
    return gr, gi, jnp.sum(dlr, axis=0, keepdims=True), jnp.sum(dli, axis=0, keepdims=True)


s5_scan.defvjp(_s5_scan_fwd, _s5_scan_bwd)


def _running_sum(x, reverse):
    n = x.shape[0]
    rows = lax.broadcasted_iota(jnp.int32, x.shape, 0)
    k = 1
    while k < n:
        if reverse:
            x = x + jnp.where(rows < n - k, pltpu.roll(x, n - k, 0), 0.0)
        else:
            x = x + jnp.where(rows >= k, pltpu.roll(x, k, 0), 0.0)
        k *= 2
    return x


@jax.custom_vjp
def cumsum_rows(x):
    return _running_sum(x, False)


cumsum_rows.defvjp(lambda x: (_running_sum(x, False), None), lambda _, g: (_running_sum(g, True),))


def _hg_chunk(HG_SUB, q, k, v, lf, st):
    c = HG_CHUNK
    b = cumsum_rows(lf)
    qb = q * jnp.exp(b)
    o = lax.dot_general(qb.astype(BF16), st.astype(BF16), (((1,), (1,)), ((), ())), preferred_element_type=F32)
    nsub = c // HG_SUB
    srow = lax.broadcasted_iota(jnp.int32, (HG_SUB, HG_SUB, 1), 0)
    scol = lax.broadcasted_iota(jnp.int32, (HG_SUB, HG_SUB, 1), 1)
    smask = scol <= srow
    outs = []
    for i in range(nsub):
        lo = i * HG_SUB
        bi, qi, ki = b[lo:lo + HG_SUB], q[lo:lo + HG_SUB], k[lo:lo + HG_SUB]
        diff = jnp.where(smask, bi[:, None, :] - bi[None, :, :], 0.0)
        e = jnp.where(smask, jnp.exp(diff), 0.0)
        a_ii = jnp.sum(qi[:, None, :] * ki[None, :, :] * e, axis=-1)
        oi = jnp.dot(a_ii.astype(BF16), v[lo:lo + HG_SUB].astype(BF16), preferred_element_type=F32)
        if i > 0:
            r = b[lo - 1:lo]
            qt = qi * jnp.exp(bi - r)
            kt = k[:lo] * jnp.exp(r - b[:lo])
            a_ij = lax.dot_general(qt.astype(BF16), kt.astype(BF16), (((1,), (1,)), ((), ())),
                                   preferred_element_type=F32)
            oi = oi + jnp.dot(a_ij.astype(BF16), v[:lo].astype(BF16), preferred_element_type=F32)
        outs.append(oi)
    o = o + jnp.concatenate(outs, axis=0)
    bl = b[c - 1:c]
    kd = k * jnp.exp(bl - b)
    st_new = st * jnp.exp(bl) + lax.dot_general(v.astype(BF16), kd.astype(BF16), (((0,), (0,)), ((), ())),
                                                preferred_element_type=F32)
    return o, st_new


def _hg_fwd_call(q, k, v, lf, name):
    s, w = q.shape
    nc = s // HG_CHUNK
    blk = pl.BlockSpec((HG_CHUNK, w), lambda i: (i, 0))

    def body(q_ref, k_ref, v_ref, lf_ref, o_ref, st_ref, st_sc):
        @pl.when(pl.program_id(0) == 0)
        def _():
            st_sc[...] = jnp.zeros(st_sc.shape, F32)

        for hh in range(HG_HEADS):
            cs = slice(hh * HG_DK, (hh + 1) * HG_DK)
            st = st_sc[hh]
            st_ref[0, hh] = st
            o, st_new = _hg_chunk(HG_SUB_FWD, q_ref[:, cs], k_ref[:, cs], v_ref[:, cs], lf_ref[:, cs], st)
            o_ref[:, cs] = o
            st_sc[hh] = st_new

    return _pcall(
        body,
        name=name,
        grid=(nc,),
        in_specs=[blk, blk, blk, blk],
        out_specs=[blk, pl.BlockSpec((1, HG_HEADS, HG_DV, HG_DK), lambda i: (i, 0, 0, 0))],
        out_shape=[jax.ShapeDtypeStruct((s, w), F32), jax.ShapeDtypeStruct((nc, HG_HEADS, HG_DV, HG_DK), F32)],
        scratch_shapes=[pltpu.VMEM((HG_HEADS, HG_DV, HG_DK), F32)],
        compiler_params=_params(("arbitrary",)),
    )(q, k, v, lf)


def _hg_bwd_call(q, k, v, lf, sts, do, name):
    s, w = q.shape
    nc = s // HG_CHUNK
    blk = pl.BlockSpec((HG_CHUNK, w), lambda i: (nc - 1 - i, 0))

    def body(q_ref, k_ref, v_ref, lf_ref, st_ref, do_ref, dq_ref, dk_ref, dv_ref, dlf_ref, dst_sc):
        @pl.when(pl.program_id(0) == 0)
        def _():
            dst_sc[...] = jnp.zeros(dst_sc.shape, F32)

        def head(hh, carry):
            cs = pl.ds(pl.multiple_of(hh * HG_DK, HG_DK), HG_DK)
            _, vjp = jax.vjp(functools.partial(_hg_chunk, HG_SUB_BWD), q_ref[:, cs], k_ref[:, cs], v_ref[:, cs],
                             lf_ref[:, cs], st_ref[0, hh])
            dq, dk, dv, dlf, dst = vjp((do_ref[:, cs], dst_sc[hh]))
            dq_ref[:, cs] = dq
            dk_ref[:, cs] = dk
            dv_ref[:, cs] = dv
            dlf_ref[:, cs] = dlf
            dst_sc[hh] = dst
            return carry

        lax.fori_loop(0, HG_HEADS, head, 0)

    out = jax.ShapeDtypeStruct((s, w), F32)
    return _pcall(
        body,
        name=name,
        grid=(nc,),
        in_specs=[blk, blk, blk, blk, pl.BlockSpec((1, HG_HEADS, HG_DV, HG_DK), lambda i: (nc - 1 - i, 0, 0, 0)), blk],
        out_specs=[blk, blk, blk, blk],
        out_shape=[out, out, out, out],
        scratch_shapes=[pltpu.VMEM((HG_HEADS, HG_DV, HG_DK), F32)],
        compiler_params=_params(("arbitrary",)),
    )(q, k, v, lf, sts, do)


@functools.partial(jax.custom_vjp, nondiff_argnums=(4,))
def hgrn_core(q, k, v, lf, name):
    return _hg_fwd_call(q, k, v, lf, name + "_f")[0]


def _hgrn_core_fwd(q, k, v, lf, name):
    o, sts = _hg_fwd_call(q, k, v, lf, name + "_f")
    return o, (q, k, v, lf, sts)


def _hgrn_core_bwd(name, res, do):
    q, k, v, lf, sts = res
    return tuple(_hg_bwd_call(q, k, v, lf, sts, do, name + "_b"))


hgrn_core.defvjp(_hgrn_core_fwd, _hgrn_core_bwd)


def _adamw_call(w, g, m, v, name):
    rows, cols = w.shape
    tr = _tile(rows, 256, 8)
    c1 = 1.0 / (1.0 - ADAM_B1 ** ADAM_STEP)
    c2 = 1.0 / (1.0 - ADAM_B2 ** ADAM_STEP)

    def body(w_ref, g_ref, m_ref, v_ref, d_ref, mo_ref, vo_ref):
        gv = g_ref[...]
        mn = ADAM_B1 * m_ref[...] + (1.0 - ADAM_B1) * gv
        vn = ADAM_B2 * v_ref[...] + (1.0 - ADAM_B2) * (gv * gv)
        d_ref[...] = -ADAM_LR * ((mn * c1) / (jnp.sqrt(vn * c2) + ADAM_EPS) + ADAM_WD * w_ref[...])
        mo_ref[...] = mn
        vo_ref[...] = vn

    blk = pl.BlockSpec((tr, cols), lambda i: (i, 0))
    out = jax.ShapeDtypeStruct((rows, cols), F32)
    return _pcall(
        body,
        name=name,
        grid=(rows // tr,),
        in_specs=[blk, blk, blk, blk],
        out_specs=[blk, blk, blk],
        out_shape=[out, out, out],
        compiler_params=_params(("parallel",)),
    )(w, g, m, v)


ANY = pl.BlockSpec(memory_space=pl.ANY)


def _place():
    return lax.axis_index("x"), lax.axis_index("y"), lax.axis_index("c")


def _other_chips(x, y):
    return [(1 - x, y), (x, 1 - y), (1 - x, 1 - y)]


def _gather_weights(shards):
    nw = len(shards)

    def body(*refs):
        w_refs, g_refs, (send_sems, recv_sems) = refs[:nw], refs[nw:2 * nw], refs[2 * nw:]
        x, y, c = _place()
        myq = 2 * x + y
        sibling = (x, y, 1 - c)
        chips = _other_chips(x, y)

        def copy(k, src, dst, to):
            return pltpu.make_async_remote_copy(src_ref=src, dst_ref=dst, send_sem=send_sems.at[k],
                                                recv_sem=recv_sems.at[k], device_id=to, device_id_type=MESH)

        first = []
        for i in range(nw):
            for j, (px, py) in enumerate(chips):
                first.append(copy(6 * i + j, w_refs[i].at[c], g_refs[i].at[c, myq], (px, py, c)))
                first[-1].start()
        passed = []
        for i in range(nw):
            for j, (px, py) in enumerate(chips):
                q = 2 * px + py
                copy(6 * i + j, w_refs[i].at[c], g_refs[i].at[c, q], (px, py, c)).wait_recv()
                passed.append(copy(6 * i + 3 + j, g_refs[i].at[c, q], g_refs[i].at[c, q], sibling))
                passed[-1].start()
        for i in range(nw):
            for j, (px, py) in enumerate(chips):
                q = 2 * px + py
                copy(6 * i + 3 + j, g_refs[i].at[1 - c, q], g_refs[i].at[1 - c, q], sibling).wait_recv()
        for cp in first + passed:
            cp.wait_send()

    got = _pcall(
        body,
        name="gather_weights",
        in_specs=[ANY] * nw,
        out_specs=[ANY] * nw,
        out_shape=[jax.ShapeDtypeStruct((s.shape[0], N_CHIPS) + s.shape[1:], s.dtype) for s in shards],
        scratch_shapes=[pltpu.SemaphoreType.DMA((6 * nw,)), pltpu.SemaphoreType.DMA((6 * nw,))],
    )(*shards)
    my_chip = 2 * lax.axis_index("x") + lax.axis_index("y")
    return [lax.dynamic_update_slice(g, s[:, None], (0, my_chip, 0, 0)) for g, s in zip(got, shards)]


def _swap_halves_to_sibling(grs):
    nw = len(grs)

    def body(*refs):
        g_refs, t_refs, (send_sems, recv_sems) = refs[:nw], refs[nw:2 * nw], refs[2 * nw:]
        x, y, c = _place()
        cps = []
        for i in range(nw):
            hh = grs[i].shape[1] // 2
            cps.append(pltpu.make_async_remote_copy(
                src_ref=g_refs[i].at[:, pl.ds((1 - c) * hh, hh)], dst_ref=t_refs[i], send_sem=send_sems.at[i],
                recv_sem=recv_sems.at[i], device_id=(x, y, 1 - c), device_id_type=MESH))
            cps[-1].start()
        for cp in cps:
            cp.wait()

    return _pcall(
        body,
        name="grad_d2d",
        in_specs=[ANY] * nw,
        out_specs=[ANY] * nw,
        out_shape=[jax.ShapeDtypeStruct((g.shape[0], g.shape[1] // 2, g.shape[2]), g.dtype) for g in grs],
        scratch_shapes=[pltpu.SemaphoreType.DMA((nw,)), pltpu.SemaphoreType.DMA((nw,))],
    )(*grs)


def _add_half(gr, t1, place, name):
    nq, r, wd = gr.shape
    hh = r // 2
    tr = _tile(hh, 512, 16)
    nb = hh // tr

    def body(p_ref, g_ref, t_ref, o_ref):
        o_ref[...] = (g_ref[...] + t_ref[...]).astype(BF16)

    return _pcall(
        body,
        name=name,
        grid_spec=pltpu.PrefetchScalarGridSpec(
            num_scalar_prefetch=1,
            grid=(nq, nb),
            in_specs=[
                pl.BlockSpec((1, tr, wd), lambda q, i, p_ref: (q, p_ref[0] * nb + i, 0)),
                pl.BlockSpec((1, tr, wd), lambda q, i, p_ref: (q, i, 0)),
            ],
            out_specs=pl.BlockSpec((1, tr, wd), lambda q, i, p_ref: (q, i, 0)),
        ),
        out_shape=jax.ShapeDtypeStruct((nq, hh, wd), BF16),
        compiler_params=_params(("parallel", "parallel")),
    )(place, gr, t1)


def _scatter_partials(ps):
    nw = len(ps)

    def body(*refs):
        p_refs, t_refs, (send_sems, recv_sems) = refs[:nw], refs[nw:2 * nw], refs[2 * nw:]
        x, y, c = _place()
        cps = []
        for i in range(nw):
            for j, (px, py) in enumerate(_other_chips(x, y)):
                cps.append(pltpu.make_async_remote_copy(
                    src_ref=p_refs[i].at[2 * px + py], dst_ref=t_refs[i].at[j], send_sem=send_sems.at[3 * i + j],
                    recv_sem=recv_sems.at[3 * i + j], device_id=(px, py, c), device_id_type=MESH))
                cps[-1].start()
        for cp in cps:
            cp.wait()

    return _pcall(
        body,
        name="grad_ici",
        in_specs=[ANY] * nw,
        out_specs=[ANY] * nw,
        out_shape=[jax.ShapeDtypeStruct((3,) + p.shape[1:], p.dtype) for p in ps],
        scratch_shapes=[pltpu.SemaphoreType.DMA((3 * nw,)), pltpu.SemaphoreType.DMA((3 * nw,))],
    )(*ps)


def _add_partials(gr, t1, t2, place, name):
    nq, r, wd = gr.shape
    hh = r // 2
    tr = _tile(hh, 512, 16)
    nb = hh // tr

    def body(p_ref, g_ref, s_ref, t_ref, o_ref):
        o_ref[...] = (((g_ref[0] + s_ref[0]) + t_ref[0].astype(F32)) + t_ref[1].astype(F32)) + t_ref[2].astype(F32)

    return _pcall(
        body,
        name=name,
        grid_spec=pltpu.PrefetchScalarGridSpec(
            num_scalar_prefetch=1,
            grid=(nb,),
            in_specs=[
                pl.BlockSpec((1, tr, wd), lambda i, p_ref: (p_ref[1], p_ref[0] * nb + i, 0)),
                pl.BlockSpec((1, tr, wd), lambda i, p_ref: (p_ref[1], i, 0)),
                pl.BlockSpec((3, tr, wd), lambda i, p_ref: (0, i, 0)),
            ],
            out_specs=pl.BlockSpec((tr, wd), lambda i, p_ref: (i, 0)),
        ),
        out_shape=jax.ShapeDtypeStruct((hh, wd), F32),
        compiler_params=_params(("parallel",)),
    )(place, gr, t1, t2)


def _join_halves(halves):
    nw = len(halves)

    def body(*refs):
        h_refs, f_refs, (send_sems, recv_sems) = refs[:nw], refs[nw:2 * nw], refs[2 * nw:]
        x, y, c = _place()

        def copy(i, rows):
            hh = halves[i].shape[0]
            return pltpu.make_async_remote_copy(src_ref=h_refs[i], dst_ref=f_refs[i].at[pl.ds(rows * hh, hh)],
                                                send_sem=send_sems.at[i], recv_sem=recv_sems.at[i],
                                                device_id=(x, y, 1 - c), device_id_type=MESH)

        for i in range(nw):
            copy(i, c).start()
        for i in range(nw):
            copy(i, c).wait_send()
            copy(i, 1 - c).wait_recv()

    got = _pcall(
        body,
        name="grad_join",
        in_specs=[ANY] * nw,
        out_specs=[ANY] * nw,
        out_shape=[jax.ShapeDtypeStruct((2 * h.shape[0], h.shape[1]), h.dtype) for h in halves],
        scratch_shapes=[pltpu.SemaphoreType.DMA((nw,)), pltpu.SemaphoreType.DMA((nw,))],
    )(*halves)
    c = lax.axis_index("c")
    return [lax.dynamic_update_slice(g, h, (c * h.shape[0], 0)) for g, h in zip(got, halves)]


N_DEV = 8


def _allreduce_small(buf):
    rows, wd = buf.shape
    n = rows // N_DEV

    def body(x_ref, o_ref, stage_ref, send_sems, recv_sems):
        x, y, c = _place()
        me = 4 * x + 2 * y + c

        def peer(k):
            return (x ^ (k >> 2), y ^ ((k >> 1) & 1), c ^ (k & 1))

        def chunk(ref, d):
            return ref.at[pl.ds(pl.multiple_of(d * n, 8), n)]

        first = []
        for k in range(1, N_DEV):
            px, py, pc = peer(k)
            cp = pltpu.make_async_remote_copy(src_ref=chunk(x_ref, 4 * px + 2 * py + pc), dst_ref=stage_ref.at[k],
                                              send_sem=send_sems.at[k - 1], recv_sem=recv_sems.at[k - 1],
                                              device_id=(px, py, pc), device_id_type=MESH)
            cp.start()
            first.append(cp)
        acc = chunk(x_ref, me)[...]
        for k in range(1, N_DEV):
            first[k - 1].wait_recv()
            acc = acc + stage_ref[k]
        chunk(o_ref, me)[...] = acc
        second = []
        for k in range(1, N_DEV):
            cp = pltpu.make_async_remote_copy(src_ref=chunk(o_ref, me), dst_ref=chunk(o_ref, me),
                                              send_sem=send_sems.at[6 + k], recv_sem=recv_sems.at[6 + k],
                                              device_id=peer(k), device_id_type=MESH)
            cp.start()
            second.append(cp)
        for k in range(1, N_DEV):
            px, py, pc = peer(k)
            them = 4 * px + 2 * py + pc
            pltpu.make_async_remote_copy(src_ref=chunk(o_ref, them), dst_ref=chunk(o_ref, them),
                                         send_sem=send_sems.at[6 + k], recv_sem=recv_sems.at[6 + k],
                                         device_id=(px, py, pc), device_id_type=MESH).wait_recv()
        for cp in first + second:
            cp.wait_send()

    vmem = pl.BlockSpec(memory_space=pltpu.VMEM)
    return _pcall(
        body,
        name="allreduce_small",
        in_specs=[vmem],
        out_specs=vmem,
        out_shape=jax.ShapeDtypeStruct((rows, wd), F32),
        scratch_shapes=[pltpu.VMEM((N_DEV, n, wd), F32), pltpu.SemaphoreType.DMA((14,)),
                        pltpu.SemaphoreType.DMA((14,))],
    )(buf)


@functools.partial(jax.custom_vjp, nondiff_argnums=(1,))
def split_cols(z, spec):
    outs = []
    for segs in spec[0]:
        parts = [z[:, s:s + n] for s, n in segs]
        outs.append(parts[0] if len(parts) == 1 else jnp.concatenate(parts, axis=1))
    return tuple(outs)


def _split_cols_fwd(z, spec):
    return split_cols(z, spec), None


def _split_cols_bwd(spec, _, gs):
    pieces, width = spec
    segs = []
    for p, piece in enumerate(pieces):
        off = 0
        for s, n in piece:
            segs.append((s, n, p, off))
            off += n
    rows, parts, pos = gs[0].shape[0], [], 0
    for s, n, p, off in sorted(segs):
        if s > pos:
            parts.append(jnp.zeros((rows, s - pos), gs[0].dtype))
        parts.append(gs[p] if (off == 0 and n == gs[p].shape[1]) else gs[p][:, off:off + n])
        pos = s + n
    if pos < width:
        parts.append(jnp.zeros((rows, width - pos), gs[0].dtype))
    return (jnp.concatenate(parts, axis=1),)


split_cols.defvjp(_split_cols_fwd, _split_cols_bwd)


def _contiguous(sizes):
    pieces, off = [], 0
    for n in sizes:
        pieces.append(((off, n),))
        off += n
    return tuple(pieces), off


def _rope(t, cos, sin):
    half = t.shape[-1] // 2
    t1, t2 = t[..., :half], t[..., half:]
    return jnp.concatenate([t1 * cos - t2 * sin, t2 * cos + t1 * sin], axis=-1)


def _s5_mixer(u, sp, l, w16, toks):
    g, p_, h = SSM_GROUPS, SSM_STATE, SSM_GROUP_CH
    lam_re, lam_im = sp["ssm_lam_re", l], sp["ssm_lam_im", l]
    step = jnp.exp(sp["ssm_log_step", l])[:, None]
    mag, ang = jnp.exp(lam_re * step), lam_im * step
    lbr, lbi = mag * jnp.cos(ang), mag * jnp.sin(ang)
    den = lam_re * lam_re + lam_im * lam_im
    nr, ni = lbr - 1.0, lbi
    qr, qi = (nr * lam_re + ni * lam_im) / den, (ni * lam_re - nr * lam_im) / den
    b_re, b_im = sp["ssm_b_re", l], sp["ssm_b_im", l]
    bbr = qr[..., None] * b_re - qi[..., None] * b_im
    bbi = qr[..., None] * b_im + qi[..., None] * b_re
    nb = g // SSM_PACK
    eye = jnp.eye(SSM_PACK, dtype=F32)

    def in_blocks(t):
        return jnp.einsum("jgph,gk->jghkp", t.reshape(nb, SSM_PACK, p_, h), eye).reshape(nb, SSM_PACK * h, SSM_PACK * p_)

    def out_blocks(t):
        return jnp.einsum("jghp,gk->jgpkh", t.reshape(nb, SSM_PACK, h, p_), eye).reshape(nb, SSM_PACK * p_, SSM_PACK * h)

    nm = "ssm%d" % l
    ur = bd_matmul(u, in_blocks(bbr), nm + "_bur")
    ui = bd_matmul(u, in_blocks(bbi), nm + "_bui")
    xr, xi = s5_scan(ur, ui, lbr.reshape(1, g * p_), lbi.reshape(1, g * p_), nm + "_scan")
    y = (bd_matmul(xr, out_blocks(sp["ssm_c_re", l]), nm + "_cr") + bd_matmul(xi, out_blocks(-sp["ssm_c_im", l]), nm + "_ci")
         + sp["ssm_d", l].reshape(1, g * h) * u)
    y = jax.nn.gelu(y)
    zz = matmul_w(y, w16["ssm_w_glu", l], toks["ssm_w_glu", l], nm + "_glu")
    z_out, z_gate = split_cols(zz, _contiguous((D_MODEL, D_MODEL)))
    return z_out * jax.nn.sigmoid(z_gate)


def _mla_mixer(q_lat, kv_lat, k_rope, cos, sin, sp, l, w16, toks):
    s = q_lat.shape[0]
    nm = "mla%d" % l
    dqk = MLA_NOPE + MLA_ROPE
    qt = matmul_w_t(rmsnorm(q_lat, sp["mla_q_norm", l], nm + "_qn"), w16["mla_w_uq", l], toks["mla_w_uq", l],
                    nm + "_uq").reshape(MLA_HEADS, dqk, s)
    half = MLA_ROPE // 2
    q1, q2 = qt[:, MLA_NOPE:MLA_NOPE + half], qt[:, MLA_NOPE + half:]
    ct, st = cos.T[None], sin.T[None]
    qt = jnp.concatenate([qt[:, :MLA_NOPE], q1 * ct - q2 * st, q2 * ct + q1 * st], axis=1).reshape(MLA_HEADS * dqk, s)
    kvt = matmul_w_t(rmsnorm(kv_lat, sp["mla_kv_norm", l], nm + "_kvn"), w16["mla_w_ukv", l], toks["mla_w_ukv", l],
                     nm + "_ukv").reshape(MLA_HEADS, MLA_NOPE + MLA_V, s)
    k_pe = _rope(k_rope, cos, sin)
    k = jnp.concatenate([kvt[:, :MLA_NOPE].transpose(0, 2, 1),
                         jnp.broadcast_to(k_pe[None], (MLA_HEADS, s, MLA_ROPE))], axis=-1)
    vt = kvt[:, MLA_NOPE:].reshape(MLA_HEADS * MLA_V, s)
    ot = attention(qt, k, vt, 1.0 / math.sqrt(dqk), True, nm + "_att")
    return matmul_w_at(ot, w16["mla_w_o", l], toks["mla_w_o", l], nm + "_o")


def _hgrn_mixer(q, f_logit, i_in, g, lb, sp, l, w16, toks):
    s = q.shape[0]
    nm = "hg%d" % l
    f = lb + (1.0 - lb) * jax.nn.sigmoid(f_logit)
    o = hgrn_core(jax.nn.silu(q), 1.0 - f, i_in, jnp.log(f), nm + "_core")
    o = rmsnorm(o.reshape(s * HG_HEADS, HG_DV), sp["hg_g_norm", l], nm + "_gn").reshape(s, HG_HEADS * HG_DV)
    return matmul_w(o * jax.nn.silu(g), w16["hg_w_o", l], toks["hg_w_o", l], nm + "_o")


def _cross_attention(h, mem_n, l, w16, toks):
    s, m = h.shape[0], mem_n.shape[0]
    nm = "xat%d" % l
    xw = X_HEADS * X_HEAD_DIM
    qt = matmul_w_t(h, w16["x_w_q", l], toks["x_w_q", l], nm + "_q")
    kv = matmul_w(mem_n, w16["x_w_kv", l], toks["x_w_kv", l], nm + "_kv")
    k, v = split_cols(kv, _contiguous((xw, xw)))
    k = k.reshape(m, X_HEADS, X_HEAD_DIM).transpose(1, 0, 2)
    ot = attention(qt, k, v.T, 1.0 / math.sqrt(X_HEAD_DIM), False, nm + "_att")
    return matmul_w_at(ot, w16["x_w_o", l], toks["x_w_o", l], nm + "_o")


def _in_spec():
    hw = HG_HEADS * HG_DK
    sizes = (SSM_WIDTH, MLA_Q_RANK, MLA_KV_RANK, MLA_ROPE, hw, hw, hw, hw, D_MODEL, D_MODEL, D_MODEL)
    pieces, lo = [], 0
    for n in sizes:
        segs, c = [], lo
        while c < lo + n:
            end = min(lo + n, (c // D_IN_SHARD + 1) * D_IN_SHARD)
            segs.append((c + (D_IN_BLOCK - D_IN_SHARD) * (c // D_IN_SHARD), end - c))
            c = end
        pieces.append(tuple(segs))
        lo += n
    assert lo == D_IN
    return tuple(pieces), N_CHIPS * D_IN_BLOCK


def _layer(x, mem, cos, sin, lb, l, sp, w16, toks):
    s = x.shape[0]
    nm = "l%d" % l
    h = rmsnorm(x, sp["norm_mix", l], nm + "_nmix")
    z = matmul_w(h, w16["w_in", l], toks["w_in", l], nm + "_in")
    u, q_lat, kv_lat, k_rope, hg_q, hg_f, hg_i, hg_g, l_ssm, l_mla, l_hg = split_cols(z, _in_spec())
    y_ssm = _s5_mixer(u, sp, l, w16, toks)
    y_mla = _mla_mixer(q_lat, kv_lat, k_rope, cos, sin, sp, l, w16, toks)
    y_hg = _hgrn_mixer(hg_q, hg_f, hg_i, hg_g, lb, sp, l, w16, toks)
    x = x + gated_merge_out((y_ssm, y_mla, y_hg), (l_ssm, l_mla, l_hg), w16["w_out", l], toks["w_out", l], nm + "_out")
    hc = rmsnorm(x, sp["norm_cross", l], nm + "_ncross")
    mem_n = rmsnorm(mem, sp["norm_mem", l], nm + "_nmem")
    x = x + _cross_attention(hc, mem_n, l, w16, toks)
    hf = rmsnorm(x, sp["norm_ffn", l], nm + "_nffn")
    gt = matmul_w(hf, w16["ffn_w_gate", l], toks["ffn_w_gate", l], nm + "_gate")
    up = matmul_w(hf, w16["ffn_w_up", l], toks["ffn_w_up", l], nm + "_up")
    return x + swiglu_down(gt, up, w16["ffn_w_down", l], toks["ffn_w_down", l], nm + "_down")


def _loss_fn(sp, toks, x, w16, mem, pos, target):
    half = MLA_ROPE // 2
    inv_freq = ROPE_THETA ** (-jnp.arange(half, dtype=F32) / half)
    ang = pos.astype(F32)[:, None] * inv_freq
    cos, sin = jnp.cos(ang), jnp.sin(ang)
    lb_p = jax.nn.softmax(sp["hg_lb", ALL], axis=0)
    lower = jnp.cumsum(lb_p, axis=0) - lb_p[0:1]
    for l in range(DEPTH):
        x = _layer(x, mem, cos, sin, lower[l], l, sp, w16, toks)
    y = rmsnorm(x, sp["norm_final", ALL], "nfinal")
    return 0.5 * jnp.sum(jnp.mean(jnp.square(y - target), axis=-1))


WEIGHTS = ["norm_mix", "w_in", "ssm_lam_re", "ssm_lam_im", "ssm_b_re", "ssm_b_im", "ssm_c_re", "ssm_c_im", "ssm_d",
           "ssm_log_step", "ssm_w_glu", "mla_q_norm", "mla_kv_norm", "mla_w_uq", "mla_w_ukv", "mla_w_o", "hg_lb",
           "hg_g_norm", "hg_w_o", "w_out", "norm_cross", "norm_mem", "x_w_q", "x_w_kv", "x_w_o", "norm_ffn",
           "ffn_w_gate_up", "ffn_w_down", "norm_final"]
INPUTS = ["x", "mem", "positions"] + WEIGHTS + ["loss_target"] + ["m_" + n for n in WEIGHTS] + ["v_" + n for n in WEIGHTS]
BIG = [("w_in", 2), ("ssm_w_glu", 2), ("mla_w_uq", 2), ("mla_w_ukv", 2), ("mla_w_o", 2), ("hg_w_o", 2), ("w_out", 1),
       ("x_w_q", 1), ("x_w_kv", 1), ("x_w_o", 2), ("ffn_w_gate_up", 2), ("ffn_w_down", 1)]
SMALL = [n for n in WEIGHTS if n not in dict(BIG)]
WHOLE = ("hg_lb", "norm_final")
ALL = -1
SMALL_W = 128
BLOCKED = ("w_in", "ffn_w_gate_up")


def _pack(arrs, rows, width, dtype):
    flat = jnp.concatenate([a.astype(dtype).reshape(-1) for a in arrs])
    return jnp.pad(flat, (0, rows * width - flat.shape[0])).reshape(rows, width)


def _unpack(buf, like):
    flat = buf.reshape(-1)
    out, off = [], 0
    for a in like:
        n = math.prod(a.shape)
        out.append(flat[off:off + n].reshape(a.shape))
        off += n
    return out


def kernel(
    x, mem, positions, norm_mix, w_in, ssm_lam_re, ssm_lam_im, ssm_b_re, ssm_b_im, ssm_c_re, ssm_c_im, ssm_d,
    ssm_log_step, ssm_w_glu, mla_q_norm, mla_kv_norm, mla_w_uq, mla_w_ukv, mla_w_o, hg_lb, hg_g_norm, hg_w_o,
    w_out, norm_cross, norm_mem, x_w_q, x_w_kv, x_w_o, norm_ffn, ffn_w_gate_up, ffn_w_down, norm_final,
    loss_target, m_norm_mix, m_w_in, m_ssm_lam_re, m_ssm_lam_im, m_ssm_b_re, m_ssm_b_im, m_ssm_c_re, m_ssm_c_im,
    m_ssm_d, m_ssm_log_step, m_ssm_w_glu, m_mla_q_norm, m_mla_kv_norm, m_mla_w_uq, m_mla_w_ukv, m_mla_w_o,
    m_hg_lb, m_hg_g_norm, m_hg_w_o, m_w_out, m_norm_cross, m_norm_mem, m_x_w_q, m_x_w_kv, m_x_w_o, m_norm_ffn,
    m_ffn_w_gate_up, m_ffn_w_down, m_norm_final, v_norm_mix, v_w_in, v_ssm_lam_re, v_ssm_lam_im, v_ssm_b_re,
    v_ssm_b_im, v_ssm_c_re, v_ssm_c_im, v_ssm_d, v_ssm_log_step, v_ssm_w_glu, v_mla_q_norm, v_mla_kv_norm,
    v_mla_w_uq, v_mla_w_ukv, v_mla_w_o, v_hg_lb, v_hg_g_norm, v_hg_w_o, v_w_out, v_norm_cross, v_norm_mem,
    v_x_w_q, v_x_w_kv, v_x_w_o, v_norm_ffn, v_ffn_w_gate_up, v_ffn_w_down, v_norm_final):
    given = locals()
    a = {n: given[n] for n in INPUTS}
    x, mem, pos, target = a["x"][0], a["mem"][0], a["positions"][0], a["loss_target"][0]
    place = jnp.stack([lax.axis_index("c"), 2 * lax.axis_index("x") + lax.axis_index("y")]).astype(jnp.int32)

    def shard16(n):
        w = a[n].astype(BF16)
        if n == "w_in":
            w = jnp.pad(w, ((0, 0), (0, 0), (0, D_IN_BLOCK - D_IN_SHARD)))
        return w

    gathered = _gather_weights([shard16(n) for n, _ in BIG])
    w16, toks = {}, {}
    for (n, ax), got in zip(BIG, gathered):
        for l in range(DEPTH):
            if n in BLOCKED:
                full = got[l]
            elif ax == 2:
                full = jnp.concatenate([got[l, q] for q in range(N_CHIPS)], axis=1)
            else:
                full = got[l].reshape(N_CHIPS * got.shape[2], got.shape[3])
            parts = {n: full}
            if n == "ffn_w_gate_up":
                half = N_CHIPS // 2
                parts = {"ffn_w_gate": full[:half], "ffn_w_up": full[half:]}
            for name, w in parts.items():
                w16[name, l] = w
                toks[name, l] = jnp.zeros(w.shape, F32)
    sp = {}
    for n in SMALL:
        if n in WHOLE:
            sp[n, ALL] = a[n]
        else:
            for l in range(DEPTH):
                sp[n, l] = a[n][l]

    loss, (g_sp, g_tok, g_x) = jax.value_and_grad(_loss_fn, argnums=(0, 1, 2))(sp, toks, x, w16, mem, pos, target)
    g_small = {n: g_sp[n, ALL] if n in WHOLE else jnp.stack([g_sp[n, l] for l in range(DEPTH)]) for n in SMALL}

    grads, names = [], []
    for n, ax in BIG:
        for l in range(DEPTH):
            if n == "ffn_w_gate_up":
                g = jnp.concatenate([g_tok["ffn_w_gate", l], g_tok["ffn_w_up", l]], axis=0)
            else:
                g = g_tok[n, l]
            if n in BLOCKED:
                grads.append(g)
            elif ax == 2:
                grads.append(g.reshape(g.shape[0], N_CHIPS, g.shape[1] // N_CHIPS).transpose(1, 0, 2))
            else:
                grads.append(g.reshape(N_CHIPS, g.shape[0] // N_CHIPS, g.shape[1]))
            names.append("%s%d" % (n, l))
    from_sibling = _swap_halves_to_sibling(grads)
    chip_sums = [_add_half(g, t, place, "grad_add_half_" + n) for n, g, t in zip(names, grads, from_sibling)]
    from_chips = _scatter_partials(chip_sums)
    totals = [_add_partials(g, t1, t2, place, "grad_add_partials_" + n)
              for n, g, t1, t2 in zip(names, grads, from_sibling, from_chips)]
    joined = _join_halves(totals)
    g_big = {n: jnp.stack([joined[DEPTH * i + l][:, :a[n].shape[2]] for l in range(DEPTH)])
             for i, (n, _) in enumerate(BIG)}

    small_like = [a[n] for n in SMALL] + [jnp.zeros((1,), F32)]
    n_small = sum(math.prod(t.shape) for t in small_like)
    srows = -(-n_small // (SMALL_W * 8 * N_DEV)) * 8 * N_DEV
    reduced = _allreduce_small(_pack([g_small[n] for n in SMALL] + [loss.reshape(1)], srows, SMALL_W, F32))
    red = _unpack(reduced, small_like)
    g_sm = dict(zip(SMALL, red[:-1]))
    loss = red[-1].reshape(())

    out_g, out_d, out_m, out_v = {}, {}, {}, {}
    for n, _ in BIG:
        shp = a[n].shape
        two = lambda t: t.reshape(shp[0] * shp[1], shp[2])
        d, m, v = _adamw_call(two(a[n]), two(g_big[n]), two(a["m_" + n]), two(a["v_" + n]), "adamw_" + n)
        out_g[n], out_d[n], out_m[n], out_v[n] = g_big[n], d.reshape(shp), m.reshape(shp), v.reshape(shp)
    small_arrs = [a[n] for n in SMALL]
    d, m, v = _adamw_call(
        _pack(small_arrs, srows, SMALL_W, F32), reduced,
        _pack([a["m_" + n] for n in SMALL], srows, SMALL_W, F32),
        _pack([a["v_" + n] for n in SMALL], srows, SMALL_W, F32), "adamw_small")
    for n, gg, dd, mm_, vv in zip(SMALL, red[:-1], _unpack(d, small_arrs), _unpack(m, small_arrs), _unpack(v, small_arrs)):
        out_g[n], out_d[n], out_m[n], out_v[n] = gg, dd, mm_, vv

    return (loss, g_x[None], *[out_g[n] for n in WEIGHTS], *[out_d[n] for n in WEIGHTS],
            *[out_m[n] for n in WEIGHTS], *[out_v[n] for n in WEIGHTS])
```

```python
import functools
import math

import jax
import jax.numpy as jnp
from jax import lax
from jax.experimental import pallas as pl
from jax.experimental.pallas import tpu as pltpu

F32 = jnp.float32
BF16 = jnp.bfloat16
MESH = pl.DeviceIdType.MESH

D_MODEL = 1024
DEPTH = 2
RMS_EPS = 1e-6
SSM_GROUPS, SSM_GROUP_CH, SSM_STATE = 32, 16, 64
SSM_WIDTH = SSM_GROUPS * SSM_GROUP_CH
SSM_LANES = SSM_GROUPS * SSM_STATE
SSM_PACK = 8
MLA_HEADS, MLA_Q_RANK, MLA_KV_RANK, MLA_NOPE, MLA_ROPE, MLA_V = 8, 512, 256, 64, 32, 64
ROPE_THETA = 10000.0
HG_HEADS, HG_DK, HG_DV, HG_CHUNK = 4, 128, 128, 64
HG_SUB_FWD, HG_SUB_BWD = 32, 16
X_HEADS, X_HEAD_DIM = 4, 128
D_FF = 2816
D_IN = 6432
N_CHIPS = 4
D_IN_SHARD = D_IN // N_CHIPS
D_IN_BLOCK = 1664
MASK_VALUE = -1e30

ADAM_LR, ADAM_B1, ADAM_B2, ADAM_EPS, ADAM_WD, ADAM_STEP = 0.001, 0.9, 0.999, 1e-08, 0.01, 10

VMEM_LIMIT = 48 * 1024 * 1024


def _pcall(body, **kw):
    return pl.pallas_call(body, **kw)


def _params(sem):
    return pltpu.CompilerParams(dimension_semantics=sem, vmem_limit_bytes=VMEM_LIMIT)


def _tile(n, cap, align=128):
    if n <= cap:
        return n
    t = (cap // align) * align
    while t >= align:
        if n % t == 0:
            return t
        t -= align
    return n


MM_VMEM_BUDGET = 36 * 1024 * 1024
MM_TILE_CAP = 2048


def _divisors(n, cap, align=128):
    out = [n] if n <= cap else []
    out += [t for t in range(align, min(n, cap + 1), align) if n % t == 0]
    return sorted(set(out), reverse=True)


@functools.lru_cache(maxsize=None)
def _mm_tiles(m, n, k, a_bytes, b_bytes, o_bytes, n_unit, k_unit):
    best = None
    for tm in _divisors(m, MM_TILE_CAP):
        for tn in _divisors(n_unit, MM_TILE_CAP):
            for tk in _divisors(k_unit, MM_TILE_CAP):
                nk = k // tk
                need = 2 * (tm * tk * a_bytes + tk * tn * b_bytes + tm * tn * o_bytes)
                need += tm * tn * 4 * (2 if nk > 1 else 1)
                need += (tm * tk * 2 if a_bytes > 2 else 0) + (tk * tn * 2 if b_bytes > 2 else 0)
                if need > MM_VMEM_BUDGET:
                    continue
                key = ((m // tm) * (n // tn) * nk, nk, -tn)
                if best is None or key < best[0]:
                    best = (key, (tm, tn, tk))
    assert best is not None, (m, n, k)
    return best[1]


def _mm(a, b, *, ta=False, tb=False, out_dtype=F32, name="mm", b_blocks=False, out_blocks=0, pro=None, epi=None,
        save_left=False):
    if ta:
        K, M = a.shape
    else:
        M, K = a.shape
    n_unit = k_unit = None
    if b_blocks:
        nb, rows, cols = b.shape
        if tb:
            N, Kb, k_unit = rows, nb * cols, cols
        else:
            Kb, N, n_unit = rows, nb * cols, cols
    elif tb:
        N, Kb = b.shape
    else:
        Kb, N = b.shape
    if out_blocks:
        assert not b_blocks and N % out_blocks == 0
        n_unit = N // out_blocks
    assert K == Kb, (a.shape, b.shape, ta, tb)
    n_pro = len(pro) - 1 if pro else 0
    n_epi = len(epi) - 1 if epi else 0
    tm, tn, tk = _mm_tiles(M, N, K, a.dtype.itemsize * (1 + n_pro), b.dtype.itemsize,
                           jnp.dtype(out_dtype).itemsize * max(1, 2 * n_epi), n_unit or N, k_unit or K)
    nk = K // tk
    dims = (((0,) if ta else (1,), (1,) if tb else (0,)), ((), ()))
    n_in = 2 + n_pro + n_epi
    n_out = max(1, n_epi) + (1 if save_left else 0)
    assert not save_left or (pro and tn == N)

    def body(*refs):
        ins, outs, acc = refs[:n_in], refs[n_in:n_in + n_out], refs[n_in + n_out:]
        b_ref = ins[1 + n_pro]
        av = (pro[0](*[r[...] for r in ins[:1 + n_pro]]) if pro else ins[0][...]).astype(BF16)
        if save_left:
            outs[-1][...] = av
        bv = b_ref[...].astype(BF16)
        part = lax.dot_general(av, bv, dims, preferred_element_type=F32)

        def finish(total):
            if epi:
                for o_ref, r in zip(outs, epi[0](total, *[e[...] for e in ins[2 + n_pro:]])):
                    o_ref[...] = r.astype(o_ref.dtype)
            else:
                outs[0][...] = total.astype(outs[0].dtype)

        if nk == 1:
            finish(part)
        else:
            acc_ref = acc[0]
            k = pl.program_id(2)

            @pl.when(k == 0)
            def _():
                acc_ref[...] = part

            @pl.when(k > 0)
            def _():
                acc_ref[...] += part

            @pl.when(k == nk - 1)
            def _():
                finish(acc_ref[...])

    a_spec = pl.BlockSpec((tk, tm), lambda i, j, k: (k, i)) if ta else pl.BlockSpec((tm, tk), lambda i, j, k: (i, k))
    if b_blocks and tb:
        r = k_unit // tk
        b_spec = pl.BlockSpec((None, tn, tk), lambda i, j, k: (k // r, j, k % r))
    elif b_blocks:
        r = n_unit // tn
        b_spec = pl.BlockSpec((None, tk, tn), lambda i, j, k: (j // r, k, j % r))
    elif tb:
        b_spec = pl.BlockSpec((tn, tk), lambda i, j, k: (j, k))
    else:
        b_spec = pl.BlockSpec((tk, tn), lambda i, j, k: (k, j))
    if out_blocks:
        r = n_unit // tn
        out_spec = pl.BlockSpec((None, tm, tn), lambda i, j, k: (j // r, i, j % r))
        out_shape = jax.ShapeDtypeStruct((out_blocks, M, n_unit), out_dtype)
    else:
        out_spec = pl.BlockSpec((tm, tn), lambda i, j, k: (i, j))
        out_shape = jax.ShapeDtypeStruct((M, N), out_dtype)
    in_specs = [a_spec] * (1 + n_pro) + [b_spec] + [out_spec] * n_epi
    args = [a] + (list(pro[1:]) if pro else []) + [b] + (list(epi[1:]) if epi else [])
    out_specs, out_shapes = ([out_spec] * n_epi, [out_shape] * n_epi) if epi else (out_spec, out_shape)
    if save_left:
        out_specs = [out_spec] * max(1, n_epi) + [a_spec]
        out_shapes = [out_shape] * max(1, n_epi) + [jax.ShapeDtypeStruct(a.shape, BF16)]
    return _pcall(
        body,
        name=name,
        grid=(M // tm, N // tn, nk),
        in_specs=in_specs,
        out_specs=out_specs,
        out_shape=out_shapes,
        scratch_shapes=[pltpu.VMEM((tm, tn), F32)] if nk > 1 else [],
        compiler_params=_params(("parallel", "parallel", "arbitrary")),
    )(*args)


def _bd_call(a, bc, transpose, name):
    m = a.shape[0]
    nb, ka, kb = bc.shape
    win, wout = (kb, ka) if transpose else (ka, kb)
    assert a.shape[1] == nb * win
    tm = _tile(m, 1024)
    dims = _NT if transpose else (((1,), (0,)), ((), ()))

    def body(a_ref, b_ref, o_ref):
        o_ref[...] = lax.dot_general(a_ref[...].astype(BF16), b_ref[0].astype(BF16), dims, preferred_element_type=F32)

    return _pcall(
        body,
        name=name,
        grid=(m // tm, nb),
        in_specs=[pl.BlockSpec((tm, win), lambda i, j: (i, j)), pl.BlockSpec((1, ka, kb), lambda i, j: (j, 0, 0))],
        out_specs=pl.BlockSpec((tm, wout), lambda i, j: (i, j)),
        out_shape=jax.ShapeDtypeStruct((m, nb * wout), F32),
        compiler_params=_params(("parallel", "parallel")),
    )(a, bc)


def _bd_dw_call(a, g, nb, name):
    m = a.shape[0]
    ka, kb = a.shape[1] // nb, g.shape[1] // nb
    tk = _tile(m, 1024)
    nk = m // tk

    def body(a_ref, g_ref, o_ref):
        part = lax.dot_general(a_ref[...].astype(BF16), g_ref[...].astype(BF16), _TN, preferred_element_type=F32)
        k = pl.program_id(1)

        @pl.when(k == 0)
        def _():
            o_ref[0] = part

        @pl.when(k > 0)
        def _():
            o_ref[0] += part

    return _pcall(
        body,
        name=name,
        grid=(nb, nk),
        in_specs=[pl.BlockSpec((tk, ka), lambda j, k: (k, j)), pl.BlockSpec((tk, kb), lambda j, k: (k, j))],
        out_specs=pl.BlockSpec((1, ka, kb), lambda j, k: (j, 0, 0)),
        out_shape=jax.ShapeDtypeStruct((nb, ka, kb), F32),
        compiler_params=_params(("parallel", "arbitrary")),
    )(a, g)


@functools.partial(jax.custom_vjp, nondiff_argnums=(2,))
def bd_matmul(a, bc, name):
    return _bd_call(a, bc, False, name + "_f")


def _bd_matmul_fwd(a, bc, name):
    return _bd_call(a, bc, False, name + "_f"), (a, bc)


def _bd_matmul_bwd(name, res, g):
    a, bc = res
    return _bd_call(g, bc, True, name + "_da"), _bd_dw_call(a, g, bc.shape[0], name + "_dw")


bd_matmul.defvjp(_bd_matmul_fwd, _bd_matmul_bwd)


@functools.partial(jax.custom_vjp, nondiff_argnums=(3,))
def matmul_w(a, w16, tok, name):
    return _mm(a, w16, b_blocks=w16.ndim == 3, name=name + "_f")


def _matmul_w_fwd(a, w16, tok, name):
    return _mm(a, w16, b_blocks=w16.ndim == 3, name=name + "_f"), (a, w16)


def _matmul_w_bwd(name, res, g):
    a, w16 = res
    blocks = w16.shape[0] if w16.ndim == 3 else 0
    return (_mm(g, w16, tb=True, b_blocks=bool(blocks), name=name + "_da"), jnp.zeros_like(w16),
            _mm(a, g, ta=True, out_blocks=blocks, name=name + "_dw"))


matmul_w.defvjp(_matmul_w_fwd, _matmul_w_bwd)


def _swiglu(gt, up):
    return gt * jax.nn.sigmoid(gt) * up


def _swiglu_bwd(dact, gt, up):
    sg = jax.nn.sigmoid(gt)
    return dact * up * sg * (1.0 + gt * (1.0 - sg)), dact * gt * sg


def _add_residual(product, x):
    return (product + x,)


def _silu_gate(o, g):
    return o * (g * jax.nn.sigmoid(g))


def _silu_gate_bwd(dm, o, g):
    sg = jax.nn.sigmoid(g)
    return dm * g * sg, dm * o * sg * (1.0 + g * (1.0 - sg))


@functools.partial(jax.custom_vjp, nondiff_argnums=(4,))
def silu_gated_out(o, g, w16, tok, name):
    return _mm(o, w16, pro=(_silu_gate, g), name=name + "_f")


def _silu_gated_out_fwd(o, g, w16, tok, name):
    out, left16 = _mm(o, w16, pro=(_silu_gate, g), save_left=True, name=name + "_f")
    return out, (o, g, w16, left16)


def _silu_gated_out_bwd(name, res, d):
    o, g, w16, left16 = res
    d_o, d_g = _mm(d, w16, tb=True, epi=(_silu_gate_bwd, o, g), name=name + "_da")
    return d_o, d_g, jnp.zeros_like(w16), _mm(left16, d, ta=True, name=name + "_dw")


silu_gated_out.defvjp(_silu_gated_out_fwd, _silu_gated_out_bwd)


@functools.partial(jax.custom_vjp, nondiff_argnums=(5,))
def swiglu_down(x, gt, up, w16, tok, name):
    return _mm(gt, w16, pro=(_swiglu, up), epi=(_add_residual, x), name=name + "_f")[0]


def _swiglu_down_fwd(x, gt, up, w16, tok, name):
    out, act16 = _mm(gt, w16, pro=(_swiglu, up), epi=(_add_residual, x), save_left=True, name=name + "_f")
    return out, (gt, up, w16, act16)


def _swiglu_down_bwd(name, res, g):
    gt, up, w16, act16 = res
    d_gt, d_up = _mm(g, w16, tb=True, epi=(_swiglu_bwd, gt, up), name=name + "_da")
    return g, d_gt, d_up, jnp.zeros_like(w16), _mm(act16, g, ta=True, name=name + "_dw")


swiglu_down.defvjp(_swiglu_down_fwd, _swiglu_down_bwd)


def _merge(y1, y2, y3, l1, l2, l3):
    return jax.nn.sigmoid(l1) * y1 + jax.nn.sigmoid(l2) * y2 + jax.nn.sigmoid(l3) * y3


def _merge_bwd(dm, y1, y2, y3, l1, l2, l3):
    dys, dls = [], []
    for y, logit in ((y1, l1), (y2, l2), (y3, l3)):
        s = jax.nn.sigmoid(logit)
        dys.append(dm * s)
        dls.append(dm * y * s * (1.0 - s))
    return (*dys, *dls)


@functools.partial(jax.custom_vjp, nondiff_argnums=(5,))
def gated_merge_out(x, ys, logits, w16, tok, name):
    return _mm(ys[0], w16, pro=(_merge, *ys[1:], *logits), epi=(_add_residual, x), name=name + "_f")[0]


def _gated_merge_out_fwd(x, ys, logits, w16, tok, name):
    out, merged16 = _mm(ys[0], w16, pro=(_merge, *ys[1:], *logits), epi=(_add_residual, x), save_left=True,
                        name=name + "_f")
    return out, (ys, logits, w16, merged16)


def _gated_merge_out_bwd(name, res, g):
    ys, logits, w16, merged16 = res
    d = _mm(g, w16, tb=True, epi=(_merge_bwd, *ys, *logits), name=name + "_da")
    return g, tuple(d[:3]), tuple(d[3:]), jnp.zeros_like(w16), _mm(merged16, g, ta=True, name=name + "_dw")


gated_merge_out.defvjp(_gated_merge_out_fwd, _gated_merge_out_bwd)


@functools.partial(jax.custom_vjp, nondiff_argnums=(3,))
def matmul_w_t(a, w16, tok, name):
    return _mm(w16, a, ta=True, tb=True, name=name + "_f")


def _matmul_w_t_fwd(a, w16, tok, name):
    return _mm(w16, a, ta=True, tb=True, name=name + "_f"), (a, w16)


def _matmul_w_t_bwd(name, res, g):
    a, w16 = res
    return (_mm(g, w16, ta=True, tb=True, name=name + "_da"), jnp.zeros_like(w16),
            _mm(a, g, ta=True, tb=True, name=name + "_dw"))


matmul_w_t.defvjp(_matmul_w_t_fwd, _matmul_w_t_bwd)


@functools.partial(jax.custom_vjp, nondiff_argnums=(3,))
def matmul_w_at(at, w16, tok, name):
    return _mm(at, w16, ta=True, name=name + "_f")


def _matmul_w_at_fwd(at, w16, tok, name):
    return _mm(at, w16, ta=True, name=name + "_f"), (at, w16)


def _matmul_w_at_bwd(name, res, g):
    at, w16 = res
    return _mm(w16, g, tb=True, name=name + "_da"), jnp.zeros_like(w16), _mm(at, g, name=name + "_dw")


matmul_w_at.defvjp(_matmul_w_at_fwd, _matmul_w_at_bwd)


@functools.partial(jax.custom_vjp, nondiff_argnums=(4,))
def matmul_w_at_res(x, at, w16, tok, name):
    return _mm(at, w16, ta=True, epi=(_add_residual, x), name=name + "_f")[0]


def _matmul_w_at_res_fwd(x, at, w16, tok, name):
    return _mm(at, w16, ta=True, epi=(_add_residual, x), name=name + "_f")[0], (at, w16)


def _matmul_w_at_res_bwd(name, res, g):
    return (g, *_matmul_w_at_bwd(name, res, g))


matmul_w_at_res.defvjp(_matmul_w_at_res_fwd, _matmul_w_at_res_bwd)


def _rms_fwd_call(x, g, name):
    rows, d = x.shape
    tr = _tile(rows, 512, 8)

    def body(x_ref, g_ref, y_ref):
        xv = x_ref[...]
        rstd = lax.rsqrt(jnp.mean(xv * xv, axis=-1, keepdims=True) + RMS_EPS)
        y_ref[...] = xv * rstd * g_ref[...]

    return _pcall(
        body,
        name=name,
        grid=(rows // tr,),
        in_specs=[pl.BlockSpec((tr, d), lambda i: (i, 0)), pl.BlockSpec((1, d), lambda i: (0, 0))],
        out_specs=pl.BlockSpec((tr, d), lambda i: (i, 0)),
        out_shape=jax.ShapeDtypeStruct((rows, d), F32),
        compiler_params=_params(("parallel",)),
    )(x, g.reshape(1, d))


def _rms_bwd_call(x, g, dy, name):
    rows, d = x.shape
    tr = _tile(rows, 512, 8)
    nb = rows // tr

    def body(x_ref, g_ref, dy_ref, dx_ref, dg_ref):
        xv = x_ref[...]
        rstd = lax.rsqrt(jnp.mean(xv * xv, axis=-1, keepdims=True) + RMS_EPS)
        xh = xv * rstd
        dyv = dy_ref[...]
        dyg = dyv * g_ref[...]
        dx_ref[...] = rstd * (dyg - xh * jnp.mean(dyg * xh, axis=-1, keepdims=True))
        dg_ref[...] = jnp.sum((dyv * xh).reshape(tr // 8, 8, d), axis=0)

    dx, dgp = _pcall(
        body,
        name=name,
        grid=(nb,),
        in_specs=[
            pl.BlockSpec((tr, d), lambda i: (i, 0)),
            pl.BlockSpec((1, d), lambda i: (0, 0)),
            pl.BlockSpec((tr, d), lambda i: (i, 0)),
        ],
        out_specs=[pl.BlockSpec((tr, d), lambda i: (i, 0)), pl.BlockSpec((8, d), lambda i: (i, 0))],
        out_shape=[jax.ShapeDtypeStruct((rows, d), F32), jax.ShapeDtypeStruct((nb * 8, d), F32)],
        compiler_params=_params(("parallel",)),
    )(x, g.reshape(1, d), dy)
    return dx, jnp.sum(dgp, axis=0)


@functools.partial(jax.custom_vjp, nondiff_argnums=(2,))
def rmsnorm(x, g, name):
    return _rms_fwd_call(x, g, name + "_f")


def _rmsnorm_fwd(x, g, name):
    return _rms_fwd_call(x, g, name + "_f"), (x, g)


def _rmsnorm_bwd(name, res, dy):
    x, g = res
    return _rms_bwd_call(x, g, dy, name + "_b")


rmsnorm.defvjp(_rmsnorm_fwd, _rmsnorm_bwd)


def _attn_blocks(sq, sk):
    return _tile(sq, 1024), _tile(sk, 512)


def _attn_pt(qs, k_ref, qi, ki, bq, bk, masked, shift):
    st = jnp.dot(k_ref[0].astype(BF16), qs, preferred_element_type=F32)
    if masked:
        kpos = ki * bk + lax.broadcasted_iota(jnp.int32, (bk, bq), 0)
        qpos = qi * bq + lax.broadcasted_iota(jnp.int32, (bk, bq), 1)
        st = jnp.where(kpos <= qpos, st, MASK_VALUE)
    return st if shift is None else jnp.exp(st - shift)


def _attn_pairs(nq, nk, bq, bk, causal, k_major):
    ok = lambda i, j: (not causal) or j * bk <= i * bq + bq - 1
    if k_major:
        pairs = [(i, j) for j in range(nk) for i in range(nq) if ok(i, j)]
    else:
        pairs = [(i, j) for i in range(nq) for j in range(nk) if ok(i, j)]
    qtab = jnp.asarray([p[0] for p in pairs], jnp.int32)
    ktab = jnp.asarray([p[1] for p in pairs], jnp.int32)
    first_q = (lambda j: (j * bk) // bq) if causal else (lambda j: 0)
    last_k = (lambda i: jnp.minimum((i * bq + bq - 1) // bk, nk - 1)) if causal else (lambda i: nk - 1)
    return qtab, ktab, first_q, last_k


def _attn_cases(causal, qi, ki, bq, bk, step):
    if not causal:
        step(False)
        return
    pl.when(ki * bk + bk - 1 <= qi * bq)(functools.partial(step, False))
    pl.when(ki * bk + bk - 1 > qi * bq)(functools.partial(step, True))


def _attn_fwd_call(qt, k, vt, scale, causal, name):
    h, sk, dq = k.shape
    sq = qt.shape[1]
    dv = vt.shape[0] // h
    bq, bk = _attn_blocks(sq, sk)
    nq, nk = sq // bq, sk // bk

    qtab, ktab, _, last_k = _attn_pairs(nq, nk, bq, bk, causal, False)

    def body(qi_ref, ki_ref, q_ref, k_ref, v_ref, o_ref, lse_ref, m_sc, l_sc, acc_sc):
        qi, ki = qi_ref[pl.program_id(1)], ki_ref[pl.program_id(1)]

        @pl.when(ki == 0)
        def _():
            m_sc[...] = jnp.full(m_sc.shape, -jnp.inf, F32)
            l_sc[...] = jnp.zeros(l_sc.shape, F32)
            acc_sc[...] = jnp.zeros(acc_sc.shape, F32)

        def step(masked):
            qs = (q_ref[...] * scale).astype(BF16)
            st = _attn_pt(qs, k_ref, qi, ki, bq, bk, masked, None)
            m_prev = m_sc[...]
            m_new = jnp.maximum(m_prev, jnp.max(st, axis=0, keepdims=True))
            alpha = jnp.exp(m_prev - m_new)
            pt = jnp.exp(st - m_new)
            l_sc[...] = alpha * l_sc[...] + jnp.sum(pt, axis=0, keepdims=True)
            acc_sc[...] = alpha * acc_sc[...] + jnp.dot(v_ref[...].astype(BF16), pt.astype(BF16),
                                                        preferred_element_type=F32)
            m_sc[...] = m_new

        _attn_cases(causal, qi, ki, bq, bk, step)

        @pl.when(ki == last_k(qi))
        def _():
            o_ref[...] = acc_sc[...] / l_sc[...]
            lse_ref[0] = m_sc[...] + jnp.log(l_sc[...])

    qmap = lambda hh, p, qi_ref, ki_ref: (hh, qi_ref[p])
    return _pcall(
        body,
        name=name,
        grid_spec=pltpu.PrefetchScalarGridSpec(
            num_scalar_prefetch=2,
            grid=(h, qtab.shape[0]),
            in_specs=[
                pl.BlockSpec((dq, bq), qmap),
                pl.BlockSpec((1, bk, dq), lambda hh, p, qi_ref, ki_ref: (hh, ki_ref[p], 0)),
                pl.BlockSpec((dv, bk), lambda hh, p, qi_ref, ki_ref: (hh, ki_ref[p])),
            ],
            out_specs=[pl.BlockSpec((dv, bq), qmap),
                       pl.BlockSpec((1, 1, bq), lambda hh, p, qi_ref, ki_ref: (hh, 0, qi_ref[p]))],
            scratch_shapes=[pltpu.VMEM((1, bq), F32), pltpu.VMEM((1, bq), F32), pltpu.VMEM((dv, bq), F32)],
        ),
        out_shape=[jax.ShapeDtypeStruct((h * dv, sq), F32), jax.ShapeDtypeStruct((h, 1, sq), F32)],
        compiler_params=_params(("parallel", "arbitrary")),
    )(qtab, ktab, qt, k, vt)


_TN = (((0,), (0,)), ((), ()))
_NT = (((1,), (1,)), ((), ()))


def _attn_dst(qs, k_ref, v_ref, do_ref, lse_ref, dl_ref, qi, ki, bq, bk, masked):
    pt = _attn_pt(qs, k_ref, qi, ki, bq, bk, masked, lse_ref[0])
    dpt = lax.dot_general(v_ref[...].astype(BF16), do_ref[...].astype(BF16), _TN, preferred_element_type=F32)
    return pt, pt * (dpt - dl_ref[0])


def _attn_dq_call(qt, k, vt, dot, lse, delta, scale, causal, name):
    h, sk, dq = k.shape
    sq = qt.shape[1]
    dv = vt.shape[0] // h
    bq, bk = _attn_blocks(sq, sk)
    nq, nk = sq // bq, sk // bk

    qtab, ktab, _, last_k = _attn_pairs(nq, nk, bq, bk, causal, False)

    def body(qi_ref, ki_ref, q_ref, k_ref, v_ref, do_ref, lse_ref, dl_ref, dq_ref, acc_sc):
        qi, ki = qi_ref[pl.program_id(1)], ki_ref[pl.program_id(1)]

        @pl.when(ki == 0)
        def _():
            acc_sc[...] = jnp.zeros(acc_sc.shape, F32)

        def step(masked):
            qs = (q_ref[...] * scale).astype(BF16)
            _, dst = _attn_dst(qs, k_ref, v_ref, do_ref, lse_ref, dl_ref, qi, ki, bq, bk, masked)
            acc_sc[...] += lax.dot_general(k_ref[0].astype(BF16), dst.astype(BF16), _TN, preferred_element_type=F32)

        _attn_cases(causal, qi, ki, bq, bk, step)

        @pl.when(ki == last_k(qi))
        def _():
            dq_ref[...] = acc_sc[...] * scale

    qmap = lambda hh, p, qi_ref, ki_ref: (hh, qi_ref[p])
    rowmap = lambda hh, p, qi_ref, ki_ref: (hh, 0, qi_ref[p])
    return _pcall(
        body,
        name=name,
        grid_spec=pltpu.PrefetchScalarGridSpec(
            num_scalar_prefetch=2,
            grid=(h, qtab.shape[0]),
            in_specs=[
                pl.BlockSpec((dq, bq), qmap),
                pl.BlockSpec((1, bk, dq), lambda hh, p, qi_ref, ki_ref: (hh, ki_ref[p], 0)),
                pl.BlockSpec((dv, bk), lambda hh, p, qi_ref, ki_ref: (hh, ki_ref[p])),
                pl.BlockSpec((dv, bq), qmap),
                pl.BlockSpec((1, 1, bq), rowmap),
                pl.BlockSpec((1, 1, bq), rowmap),
            ],
            out_specs=pl.BlockSpec((dq, bq), qmap),
            scratch_shapes=[pltpu.VMEM((dq, bq), F32)],
        ),
        out_shape=jax.ShapeDtypeStruct((h * dq, sq), F32),
        compiler_params=_params(("parallel", "arbitrary")),
    )(qtab, ktab, qt, k, vt, dot, lse, delta)


def _attn_dkv_call(qt, k, vt, dot, lse, delta, scale, causal, name):
    h, sk, dq = k.shape
    sq = qt.shape[1]
    dv = vt.shape[0] // h
    bq, bk = _attn_blocks(sq, sk)
    nq, nk = sq // bq, sk // bk

    qtab, ktab, first_q, _ = _attn_pairs(nq, nk, bq, bk, causal, True)

    def body(qi_ref, ki_ref, q_ref, k_ref, v_ref, do_ref, lse_ref, dl_ref, dk_ref, dv_ref, dk_sc, dv_sc):
        qi, ki = qi_ref[pl.program_id(1)], ki_ref[pl.program_id(1)]

        @pl.when(qi == first_q(ki))
        def _():
            dk_sc[...] = jnp.zeros(dk_sc.shape, F32)
            dv_sc[...] = jnp.zeros(dv_sc.shape, F32)

        def step(masked):
            qs = (q_ref[...] * scale).astype(BF16)
            pt, dst = _attn_dst(qs, k_ref, v_ref, do_ref, lse_ref, dl_ref, qi, ki, bq, bk, masked)
            dv_sc[...] += lax.dot_general(do_ref[...].astype(BF16), pt.astype(BF16), _NT, preferred_element_type=F32)
            dk_sc[...] += lax.dot_general(dst.astype(BF16), qs, _NT, preferred_element_type=F32)

        _attn_cases(causal, qi, ki, bq, bk, step)

        @pl.when(qi == nq - 1)
        def _():
            dk_ref[0] = dk_sc[...]
            dv_ref[...] = dv_sc[...]

    qmap = lambda hh, p, qi_ref, ki_ref: (hh, qi_ref[p])
    rowmap = lambda hh, p, qi_ref, ki_ref: (hh, 0, qi_ref[p])
    kmap = lambda hh, p, qi_ref, ki_ref: (hh, ki_ref[p], 0)
    vmap = lambda hh, p, qi_ref, ki_ref: (hh, ki_ref[p])
    return _pcall(
        body,
        name=name,
        grid_spec=pltpu.PrefetchScalarGridSpec(
            num_scalar_prefetch=2,
            grid=(h, qtab.shape[0]),
            in_specs=[
                pl.BlockSpec((dq, bq), qmap),
                pl.BlockSpec((1, bk, dq), kmap),
                pl.BlockSpec((dv, bk), vmap),
                pl.BlockSpec((dv, bq), qmap),
                pl.BlockSpec((1, 1, bq), rowmap),
                pl.BlockSpec((1, 1, bq), rowmap),
            ],
            out_specs=[pl.BlockSpec((1, bk, dq), kmap), pl.BlockSpec((dv, bk), vmap)],
            scratch_shapes=[pltpu.VMEM((bk, dq), F32), pltpu.VMEM((dv, bk), F32)],
        ),
        out_shape=[jax.ShapeDtypeStruct((h, sk, dq), F32), jax.ShapeDtypeStruct((h * dv, sk), F32)],
        compiler_params=_params(("parallel", "arbitrary")),
    )(qtab, ktab, qt, k, vt, dot, lse, delta)


@functools.partial(jax.custom_vjp, nondiff_argnums=(3, 4, 5))
def attention(qt, k, vt, scale, causal, name):
    return _attn_fwd_call(qt, k, vt, scale, causal, name + "_f")[0]


def _attention_fwd(qt, k, vt, scale, causal, name):
    ot, lse = _attn_fwd_call(qt, k, vt, scale, causal, name + "_f")
    return ot, (qt, k, vt, ot, lse)


def _attention_bwd(scale, causal, name, res, dot):
    qt, k, vt, ot, lse = res
    h = k.shape[0]
    delta = jnp.sum((dot * ot).reshape(h, ot.shape[0] // h, ot.shape[1]), axis=1, keepdims=True)
    dqt = _attn_dq_call(qt, k, vt, dot, lse, delta, scale, causal, name + "_dq")
    dk, dvt = _attn_dkv_call(qt, k, vt, dot, lse, delta, scale, causal, name + "_dkv")
    return dqt, dk, dvt


attention.defvjp(_attention_fwd, _attention_bwd)


SCAN_ROWS = 8
SCAN_LANES = 256
SCAN_TBLOCK = 512
SCAN_UNROLL = 4


def _scan_tables(lr, li):
    pr, pi = [lr], [li]
    for _ in range(SCAN_ROWS - 1):
        pr, pi = pr + [pr[-1] * lr - pi[-1] * li], pi + [pr[-1] * li + pi[-1] * lr]
    return jnp.concatenate(pr, axis=0), jnp.concatenate(pi, axis=0)


def _scan_call(ur, ui, lr, li, reverse, name, states=None):
    s, n = ur.shape
    tb = _tile(s, SCAN_TBLOCK, 8)
    nt, nl = s // tb, n // SCAN_LANES
    ntile = tb // SCAN_ROWS
    pr, pi = _scan_tables(lr, li)
    if reverse:
        pr, pi = pr[::-1], pi[::-1]
    shape = (SCAN_ROWS, SCAN_LANES)
    with_states = states is not None
    assert reverse or not with_states

    def body(*refs):
        if with_states:
            ur_ref, ui_ref, pr_ref, pi_ref, sr_ref, si_ref, xr_ref, xi_ref, dr_ref, di_ref, cr_sc, ci_sc, ar_sc, ai_sc = refs
        else:
            ur_ref, ui_ref, pr_ref, pi_ref, xr_ref, xi_ref, cr_sc, ci_sc, ar_sc, ai_sc = refs

        @pl.when(pl.program_id(1) == 0)
        def _():
            for sc in (cr_sc, ci_sc, ar_sc, ai_sc):
                sc[...] = jnp.zeros(shape, F32)

        prv, piv = pr_ref[...], pi_ref[...]
        rows = lax.broadcasted_iota(jnp.int32, shape, 0)

        def powers(k):
            r = (SCAN_ROWS - k) if reverse else (k - 1)
            return jnp.broadcast_to(prv[r:r + 1], shape), jnp.broadcast_to(piv[r:r + 1], shape)

        pw = [powers(k) for k in (1, 2, 4)]

        def tile(i, carry):
            cr, ci, acr, aci = carry
            j = (ntile - 1 - i) if reverse else i
            sl = pl.ds(pl.multiple_of(j * SCAN_ROWS, SCAN_ROWS), SCAN_ROWS)
            xr, xi = ur_ref[sl, :], ui_ref[sl, :]
            for (ar, ai), k in zip(pw, (1, 2, 4)):
                if reverse:
                    keep = rows < SCAN_ROWS - k
                    sr = jnp.where(keep, pltpu.roll(xr, SCAN_ROWS - k, 0), 0.0)
                    si = jnp.where(keep, pltpu.roll(xi, SCAN_ROWS - k, 0), 0.0)
                else:
                    keep = rows >= k
                    sr = jnp.where(keep, pltpu.roll(xr, k, 0), 0.0)
                    si = jnp.where(keep, pltpu.roll(xi, k, 0), 0.0)
                xr, xi = xr + (ar * sr - ai * si), xi + (ar * si + ai * sr)
            xr, xi = xr + (prv * cr - piv * ci), xi + (prv * ci + piv * cr)
            xr_ref[sl, :] = xr
            xi_ref[sl, :] = xi
            if with_states:
                last = rows == SCAN_ROWS - 1
                nr = jnp.where(last, cr, pltpu.roll(xr, SCAN_ROWS - 1, 0))
                ni = jnp.where(last, ci, pltpu.roll(xi, SCAN_ROWS - 1, 0))
                sr, si = sr_ref[sl, :], si_ref[sl, :]
                acr, aci = acr + (sr * nr + si * ni), aci + (sr * ni - si * nr)
            e = 0 if reverse else SCAN_ROWS - 1
            return jnp.broadcast_to(xr[e:e + 1], shape), jnp.broadcast_to(xi[e:e + 1], shape), acr, aci

        def tiles(g, carry):
            for u in range(SCAN_UNROLL):
                carry = tile(g * SCAN_UNROLL + u, carry)
            return carry

        assert ntile % SCAN_UNROLL == 0
        cr, ci, acr, aci = lax.fori_loop(0, ntile // SCAN_UNROLL, tiles,
                                         (cr_sc[...], ci_sc[...], ar_sc[...], ai_sc[...]))
        cr_sc[...] = cr
        ci_sc[...] = ci
        if with_states:
            ar_sc[...] = acr
            ai_sc[...] = aci

            @pl.when(pl.program_id(1) == nt - 1)
            def _():
                dr_ref[...] = acr
                di_ref[...] = aci

    tmap = (lambda l, t: (nt - 1 - t, l)) if reverse else (lambda l, t: (t, l))
    blk = pl.BlockSpec((tb, SCAN_LANES), tmap)
    tab = pl.BlockSpec((SCAN_ROWS, SCAN_LANES), lambda l, t: (0, l))
    full = jax.ShapeDtypeStruct((s, n), F32)
    return _pcall(
        body,
        name=name,
        grid=(nl, nt),
        in_specs=[blk, blk, tab, tab] + ([blk, blk] if with_states else []),
        out_specs=[blk, blk] + ([tab, tab] if with_states else []),
        out_shape=[full, full] + ([jax.ShapeDtypeStruct((SCAN_ROWS, n), F32)] * 2 if with_states else []),
        scratch_shapes=[pltpu.VMEM(shape, F32)] * 4,
        compiler_params=_params(("parallel", "arbitrary")),
    )(ur, ui, pr, pi, *(states if with_states else ()))


@functools.partial(jax.custom_vjp, nondiff_argnums=(4,))
def s5_scan(ur, ui, lr, li, name):
    return tuple(_scan_call(ur, ui, lr, li, False, name + "_f"))


def _s5_scan_fwd(ur, ui, lr, li, name):
    xr, xi = _scan_call(ur, ui, lr, li, False, name + "_f")
    return (xr, xi), (xr, xi, lr, li)


def _s5_scan_bwd(name, res, g):
    xr, xi, lr, li = res
    gr, gi, dlr, dli = _scan_call(g[0], g[1], lr, -li, True, name + "_b", states=(xr, xi))
    return gr, gi, jnp.sum(dlr, axis=0, keepdims=True), jnp.sum(dli, axis=0, keepdims=True)


s5_scan.defvjp(_s5_scan_fwd, _s5_scan_bwd)


def _running_sum(x, reverse):
    n = x.shape[0]
    rows = lax.broadcasted_iota(jnp.int32, x.shape, 0)
    k = 1
    while k < n:
        if reverse:
            x = x + jnp.where(rows < n - k, pltpu.roll(x, n - k, 0), 0.0)
        else:
            x = x + jnp.where(rows >= k, pltpu.roll(x, k, 0), 0.0)
        k *= 2
    return x


@jax.custom_vjp
def cumsum_rows(x):
    return _running_sum(x, False)


cumsum_rows.defvjp(lambda x: (_running_sum(x, False), None), lambda _, g: (_running_sum(g, True),))


def _hg_chunk(HG_SUB, q, k, v, lf, st):
    c = HG_CHUNK
    b = cumsum_rows(lf)
    qb = q * jnp.exp(b)
    o = lax.dot_general(qb.astype(BF16), st.astype(BF16), (((1,), (1,)), ((), ())), preferred_element_type=F32)
    nsub = c // HG_SUB
    srow = lax.broadcasted_iota(jnp.int32, (HG_SUB, HG_SUB, 1), 0)
    scol = lax.broadcasted_iota(jnp.int32, (HG_SUB, HG_SUB, 1), 1)
    smask = scol <= srow
    outs = []
    for i in range(nsub):
        lo = i * HG_SUB
        bi, qi, ki = b[lo:lo + HG_SUB], q[lo:lo + HG_SUB], k[lo:lo + HG_SUB]
        diff = jnp.where(smask, bi[:, None, :] - bi[None, :, :], 0.0)
        e = jnp.where(smask, jnp.exp(diff), 0.0)
        a_ii = jnp.sum(qi[:, None, :] * ki[None, :, :] * e, axis=-1)
        oi = jnp.dot(a_ii.astype(BF16), v[lo:lo + HG_SUB].astype(BF16), preferred_element_type=F32)
        if i > 0:
            r = b[lo - 1:lo]
            qt = qi * jnp.exp(bi - r)
            kt = k[:lo] * jnp.exp(r - b[:lo])
            a_ij = lax.dot_general(qt.astype(BF16), kt.astype(BF16), (((1,), (1,)), ((), ())),
                                   preferred_element_type=F32)
            oi = oi + jnp.dot(a_ij.astype(BF16), v[:lo].astype(BF16), preferred_element_type=F32)
        outs.append(oi)
    o = o + jnp.concatenate(outs, axis=0)
    bl = b[c - 1:c]
    kd = k * jnp.exp(bl - b)
    st_new = st * jnp.exp(bl) + lax.dot_general(v.astype(BF16), kd.astype(BF16), (((0,), (0,)), ((), ())),
                                                preferred_element_type=F32)
    return o, st_new


def _hg_fwd_call(q, k, v, lf, name):
    s, w = q.shape
    nc = s // HG_CHUNK
    blk = pl.BlockSpec((HG_CHUNK, w), lambda i: (i, 0))

    def body(q_ref, k_ref, v_ref, lf_ref, o_ref, st_ref, st_sc):
        @pl.when(pl.program_id(0) == 0)
        def _():
            st_sc[...] = jnp.zeros(st_sc.shape, F32)

        for hh in range(HG_HEADS):
            cs = slice(hh * HG_DK, (hh + 1) * HG_DK)
            st = st_sc[hh]
            st_ref[0, hh] = st
            o, st_new = _hg_chunk(HG_SUB_FWD, q_ref[:, cs], k_ref[:, cs], v_ref[:, cs], lf_ref[:, cs], st)
            o_ref[:, cs] = o
            st_sc[hh] = st_new

    return _pcall(
        body,
        name=name,
        grid=(nc,),
        in_specs=[blk, blk, blk, blk],
        out_specs=[blk, pl.BlockSpec((1, HG_HEADS, HG_DV, HG_DK), lambda i: (i, 0, 0, 0))],
        out_shape=[jax.ShapeDtypeStruct((s, w), F32), jax.ShapeDtypeStruct((nc, HG_HEADS, HG_DV, HG_DK), F32)],
        scratch_shapes=[pltpu.VMEM((HG_HEADS, HG_DV, HG_DK), F32)],
        compiler_params=_params(("arbitrary",)),
    )(q, k, v, lf)


def _hg_bwd_call(q, k, v, lf, sts, do, name):
    s, w = q.shape
    nc = s // HG_CHUNK
    blk = pl.BlockSpec((HG_CHUNK, w), lambda i: (nc - 1 - i, 0))

    def body(q_ref, k_ref, v_ref, lf_ref, st_ref, do_ref, dq_ref, dk_ref, dv_ref, dlf_ref, dst_sc):
        @pl.when(pl.program_id(0) == 0)
        def _():
            dst_sc[...] = jnp.zeros(dst_sc.shape, F32)

        def head(hh, carry):
            cs = pl.ds(pl.multiple_of(hh * HG_DK, HG_DK), HG_DK)
            _, vjp = jax.vjp(functools.partial(_hg_chunk, HG_SUB_BWD), q_ref[:, cs], k_ref[:, cs], v_ref[:, cs],
                             lf_ref[:, cs], st_ref[0, hh])
            dq, dk, dv, dlf, dst = vjp((do_ref[:, cs], dst_sc[hh]))
            dq_ref[:, cs] = dq
            dk_ref[:, cs] = dk
            dv_ref[:, cs] = dv
            dlf_ref[:, cs] = dlf
            dst_sc[hh] = dst
            return carry

        lax.fori_loop(0, HG_HEADS, head, 0)

    out = jax.ShapeDtypeStruct((s, w), F32)
    return _pcall(
        body,
        name=name,
        grid=(nc,),
        in_specs=[blk, blk, blk, blk, pl.BlockSpec((1, HG_HEADS, HG_DV, HG_DK), lambda i: (nc - 1 - i, 0, 0, 0)), blk],
        out_specs=[blk, blk, blk, blk],
        out_shape=[out, out, out, out],
        scratch_shapes=[pltpu.VMEM((HG_HEADS, HG_DV, HG_DK), F32)],
        compiler_params=_params(("arbitrary",)),
    )(q, k, v, lf, sts, do)


@functools.partial(jax.custom_vjp, nondiff_argnums=(4,))
def hgrn_core(q, k, v, lf, name):
    return _hg_fwd_call(q, k, v, lf, name + "_f")[0]


def _hgrn_core_fwd(q, k, v, lf, name):
    o, sts = _hg_fwd_call(q, k, v, lf, name + "_f")
    return o, (q, k, v, lf, sts)


def _hgrn_core_bwd(name, res, do):
    q, k, v, lf, sts = res
    return tuple(_hg_bwd_call(q, k, v, lf, sts, do, name + "_b"))


hgrn_core.defvjp(_hgrn_core_fwd, _hgrn_core_bwd)


def _adamw_call(w, g, m, v, name):
    rows, cols = w.shape
    tr = _tile(rows, 256, 8)
    c1 = 1.0 / (1.0 - ADAM_B1 ** ADAM_STEP)
    c2 = 1.0 / (1.0 - ADAM_B2 ** ADAM_STEP)

    def body(w_ref, g_ref, m_ref, v_ref, d_ref, mo_ref, vo_ref):
        gv = g_ref[...]
        mn = ADAM_B1 * m_ref[...] + (1.0 - ADAM_B1) * gv
        vn = ADAM_B2 * v_ref[...] + (1.0 - ADAM_B2) * (gv * gv)
        d_ref[...] = -ADAM_LR * ((mn * c1) / (jnp.sqrt(vn * c2) + ADAM_EPS) + ADAM_WD * w_ref[...])
        mo_ref[...] = mn
        vo_ref[...] = vn

    blk = pl.BlockSpec((tr, cols), lambda i: (i, 0))
    out = jax.ShapeDtypeStruct((rows, cols), F32)
    return _pcall(
        body,
        name=name,
        grid=(rows // tr,),
        in_specs=[blk, blk, blk, blk],
        out_specs=[blk, blk, blk],
        out_shape=[out, out, out],
        compiler_params=_params(("parallel",)),
    )(w, g, m, v)


ANY = pl.BlockSpec(memory_space=pl.ANY)


def _place():
    return lax.axis_index("x"), lax.axis_index("y"), lax.axis_index("c")


def _other_chips(x, y):
    return [(1 - x, y), (x, 1 - y), (1 - x, 1 - y)]


def _gather_weights(shards):
    nw = len(shards)

    def body(*refs):
        w_refs, g_refs, (send_sems, recv_sems) = refs[:nw], refs[nw:2 * nw], refs[2 * nw:]
        x, y, c = _place()
        myq = 2 * x + y
        sibling = (x, y, 1 - c)
        chips = _other_chips(x, y)

        def copy(k, src, dst, to):
            return pltpu.make_async_remote_copy(src_ref=src, dst_ref=dst, send_sem=send_sems.at[k],
                                                recv_sem=recv_sems.at[k], device_id=to, device_id_type=MESH)

        first = []
        for i in range(nw):
            for j, (px, py) in enumerate(chips):
                first.append(copy(6 * i + j, w_refs[i].at[c], g_refs[i].at[c, myq], (px, py, c)))
                first[-1].start()
        passed = []
        for i in range(nw):
            for j, (px, py) in enumerate(chips):
                q = 2 * px + py
                copy(6 * i + j, w_refs[i].at[c], g_refs[i].at[c, q], (px, py, c)).wait_recv()
                passed.append(copy(6 * i + 3 + j, g_refs[i].at[c, q], g_refs[i].at[c, q], sibling))
                passed[-1].start()
        for i in range(nw):
            for j, (px, py) in enumerate(chips):
                q = 2 * px + py
                copy(6 * i + 3 + j, g_refs[i].at[1 - c, q], g_refs[i].at[1 - c, q], sibling).wait_recv()
        for cp in first + passed:
            cp.wait_send()

    got = _pcall(
        body,
        name="gather_weights",
        in_specs=[ANY] * nw,
        out_specs=[ANY] * nw,
        out_shape=[jax.ShapeDtypeStruct((s.shape[0], N_CHIPS) + s.shape[1:], s.dtype) for s in shards],
        scratch_shapes=[pltpu.SemaphoreType.DMA((6 * nw,)), pltpu.SemaphoreType.DMA((6 * nw,))],
    )(*shards)
    my_chip = 2 * lax.axis_index("x") + lax.axis_index("y")
    return [lax.dynamic_update_slice(g, s[:, None], (0, my_chip, 0, 0)) for g, s in zip(got, shards)]


def _swap_halves_to_sibling(grs):
    nw = len(grs)

    def body(*refs):
        g_refs, t_refs, (send_sems, recv_sems) = refs[:nw], refs[nw:2 * nw], refs[2 * nw:]
        x, y, c = _place()
        cps = []
        for i in range(nw):
            hh = grs[i].shape[1] // 2
            cps.append(pltpu.make_async_remote_copy(
                src_ref=g_refs[i].at[:, pl.ds((1 - c) * hh, hh)], dst_ref=t_refs[i], send_sem=send_sems.at[i],
                recv_sem=recv_sems.at[i], device_id=(x, y, 1 - c), device_id_type=MESH))
            cps[-1].start()
        for cp in cps:
            cp.wait()

    return _pcall(
        body,
        name="grad_d2d",
        in_specs=[ANY] * nw,
        out_specs=[ANY] * nw,
        out_shape=[jax.ShapeDtypeStruct((g.shape[0], g.shape[1] // 2, g.shape[2]), g.dtype) for g in grs],
        scratch_shapes=[pltpu.SemaphoreType.DMA((nw,)), pltpu.SemaphoreType.DMA((nw,))],
    )(*grs)


def _add_half(gr, t1, place, name):
    nq, r, wd = gr.shape
    hh = r // 2
    tr = _tile(hh, 512, 16)
    nb = hh // tr

    def body(p_ref, g_ref, t_ref, o_ref):
        o_ref[...] = (g_ref[...] + t_ref[...]).astype(BF16)

    return _pcall(
        body,
        name=name,
        grid_spec=pltpu.PrefetchScalarGridSpec(
            num_scalar_prefetch=1,
            grid=(nq, nb),
            in_specs=[
                pl.BlockSpec((1, tr, wd), lambda q, i, p_ref: (q, p_ref[0] * nb + i, 0)),
                pl.BlockSpec((1, tr, wd), lambda q, i, p_ref: (q, i, 0)),
            ],
            out_specs=pl.BlockSpec((1, tr, wd), lambda q, i, p_ref: (q, i, 0)),
        ),
        out_shape=jax.ShapeDtypeStruct((nq, hh, wd), BF16),
        compiler_params=_params(("parallel", "parallel")),
    )(place, gr, t1)


def _scatter_partials(ps):
    nw = len(ps)

    def body(*refs):
        p_refs, t_refs, (send_sems, recv_sems) = refs[:nw], refs[nw:2 * nw], refs[2 * nw:]
        x, y, c = _place()
        cps = []
        for i in range(nw):
            for j, (px, py) in enumerate(_other_chips(x, y)):
                cps.append(pltpu.make_async_remote_copy(
                    src_ref=p_refs[i].at[2 * px + py], dst_ref=t_refs[i].at[j], send_sem=send_sems.at[3 * i + j],
                    recv_sem=recv_sems.at[3 * i + j], device_id=(px, py, c), device_id_type=MESH))
                cps[-1].start()
        for cp in cps:
            cp.wait()

    return _pcall(
        body,
        name="grad_ici",
        in_specs=[ANY] * nw,
        out_specs=[ANY] * nw,
        out_shape=[jax.ShapeDtypeStruct((3,) + p.shape[1:], p.dtype) for p in ps],
        scratch_shapes=[pltpu.SemaphoreType.DMA((3 * nw,)), pltpu.SemaphoreType.DMA((3 * nw,))],
    )(*ps)


def _add_partials(gr, t1, t2, place, name):
    nq, r, wd = gr.shape
    hh = r // 2
    tr = _tile(hh, 512, 16)
    nb = hh // tr

    def body(p_ref, g_ref, s_ref, t_ref, o_ref):
        o_ref[...] = (((g_ref[0] + s_ref[0]) + t_ref[0].astype(F32)) + t_ref[1].astype(F32)) + t_ref[2].astype(F32)

    return _pcall(
        body,
        name=name,
        grid_spec=pltpu.PrefetchScalarGridSpec(
            num_scalar_prefetch=1,
            grid=(nb,),
            in_specs=[
                pl.BlockSpec((1, tr, wd), lambda i, p_ref: (p_ref[1], p_ref[0] * nb + i, 0)),
                pl.BlockSpec((1, tr, wd), lambda i, p_ref: (p_ref[1], i, 0)),
                pl.BlockSpec((3, tr, wd), lambda i, p_ref: (0, i, 0)),
            ],
            out_specs=pl.BlockSpec((tr, wd), lambda i, p_ref: (i, 0)),
        ),
        out_shape=jax.ShapeDtypeStruct((hh, wd), F32),
        compiler_params=_params(("parallel",)),
    )(place, gr, t1, t2)


def _join_halves(halves):
    nw = len(halves)

    def body(*refs):
        h_refs, f_refs, (send_sems, recv_sems) = refs[:nw], refs[nw:2 * nw], refs[2 * nw:]
        x, y, c = _place()

        def copy(i, rows):
            hh = halves[i].shape[0]
            return pltpu.make_async_remote_copy(src_ref=h_refs[i], dst_ref=f_refs[i].at[pl.ds(rows * hh, hh)],
                                                send_sem=send_sems.at[i], recv_sem=recv_sems.at[i],
                                                device_id=(x, y, 1 - c), device_id_type=MESH)

        for i in range(nw):
            copy(i, c).start()
        for i in range(nw):
            copy(i, c).wait_send()
            copy(i, 1 - c).wait_recv()

    got = _pcall(
        body,
        name="grad_join",
        in_specs=[ANY] * nw,
        out_specs=[ANY] * nw,
        out_shape=[jax.ShapeDtypeStruct((2 * h.shape[0], h.shape[1]), h.dtype) for h in halves],
        scratch_shapes=[pltpu.SemaphoreType.DMA((nw,)), pltpu.SemaphoreType.DMA((nw,))],
    )(*halves)
    c = lax.axis_index("c")
    return [lax.dynamic_update_slice(g, h, (c * h.shape[0], 0)) for g, h in zip(got, halves)]


N_DEV = 8


def _allreduce_small(buf):
    rows, wd = buf.shape
    n = rows // N_DEV

    def body(x_ref, o_ref, stage_ref, send_sems, recv_sems):
        x, y, c = _place()
        me = 4 * x + 2 * y + c

        def peer(k):
            return (x ^ (k >> 2), y ^ ((k >> 1) & 1), c ^ (k & 1))

        def chunk(ref, d):
            return ref.at[pl.ds(pl.multiple_of(d * n, 8), n)]

        first = []
        for k in range(1, N_DEV):
            px, py, pc = peer(k)
            cp = pltpu.make_async_remote_copy(src_ref=chunk(x_ref, 4 * px + 2 * py + pc), dst_ref=stage_ref.at[k],
                                              send_sem=send_sems.at[k - 1], recv_sem=recv_sems.at[k - 1],
                                              device_id=(px, py, pc), device_id_type=MESH)
            cp.start()
            first.append(cp)
        acc = chunk(x_ref, me)[...]
        for k in range(1, N_DEV):
            first[k - 1].wait_recv()
            acc = acc + stage_ref[k]
        chunk(o_ref, me)[...] = acc
        second = []
        for k in range(1, N_DEV):
            cp = pltpu.make_async_remote_copy(src_ref=chunk(o_ref, me), dst_ref=chunk(o_ref, me),
                                              send_sem=send_sems.at[6 + k], recv_sem=recv_sems.at[6 + k],
                                              device_id=peer(k), device_id_type=MESH)
            cp.start()
            second.append(cp)
        for k in range(1, N_DEV):
            px, py, pc = peer(k)
            them = 4 * px + 2 * py + pc
            pltpu.make_async_remote_copy(src_ref=chunk(o_ref, them), dst_ref=chunk(o_ref, them),
                                         send_sem=send_sems.at[6 + k], recv_sem=recv_sems.at[6 + k],
                                         device_id=(px, py, pc), device_id_type=MESH).wait_recv()
        for cp in first + second:
            cp.wait_send()

    vmem = pl.BlockSpec(memory_space=pltpu.VMEM)
    return _pcall(
        body,
        name="allreduce_small",
        in_specs=[vmem],
        out_specs=vmem,
        out_shape=jax.ShapeDtypeStruct((rows, wd), F32),
        scratch_shapes=[pltpu.VMEM((N_DEV, n, wd), F32), pltpu.SemaphoreType.DMA((14,)),
                        pltpu.SemaphoreType.DMA((14,))],
    )(buf)


@functools.partial(jax.custom_vjp, nondiff_argnums=(1,))
def split_cols(z, spec):
    outs = []
    for segs in spec[0]:
        parts = [z[:, s:s + n] for s, n in segs]
        outs.append(parts[0] if len(parts) == 1 else jnp.concatenate(parts, axis=1))
    return tuple(outs)


def _split_cols_fwd(z, spec):
    return split_cols(z, spec), None


def _split_cols_bwd(spec, _, gs):
    pieces, width = spec
    segs = []
    for p, piece in enumerate(pieces):
        off = 0
        for s, n in piece:
            segs.append((s, n, p, off))
            off += n
    rows, parts, pos = gs[0].shape[0], [], 0
    for s, n, p, off in sorted(segs):
        if s > pos:
            parts.append(jnp.zeros((rows, s - pos), gs[0].dtype))
        parts.append(gs[p] if (off == 0 and n == gs[p].shape[1]) else gs[p][:, off:off + n])
        pos = s + n
    if pos < width:
        parts.append(jnp.zeros((rows, width - pos), gs[0].dtype))
    return (jnp.concatenate(parts, axis=1),)


split_cols.defvjp(_split_cols_fwd, _split_cols_bwd)


def _contiguous(sizes):
    pieces, off = [], 0
    for n in sizes:
        pieces.append(((off, n),))
        off += n
    return tuple(pieces), off


def _rope(t, cos, sin):
    half = t.shape[-1] // 2
    t1, t2 = t[..., :half], t[..., half:]
    return jnp.concatenate([t1 * cos - t2 * sin, t2 * cos + t1 * sin], axis=-1)


def _s5_mixer(u, sp, l, w16, toks):
    g, p_, h = SSM_GROUPS, SSM_STATE, SSM_GROUP_CH
    lam_re, lam_im = sp["ssm_lam_re", l], sp["ssm_lam_im", l]
    step = jnp.exp(sp["ssm_log_step", l])[:, None]
    mag, ang = jnp.exp(lam_re * step), lam_im * step
    lbr, lbi = mag * jnp.cos(ang), mag * jnp.sin(ang)
    den = lam_re * lam_re + lam_im * lam_im
    nr, ni = lbr - 1.0, lbi
    qr, qi = (nr * lam_re + ni * lam_im) / den, (ni * lam_re - nr * lam_im) / den
    b_re, b_im = sp["ssm_b_re", l], sp["ssm_b_im", l]
    bbr = qr[..., None] * b_re - qi[..., None] * b_im
    bbi = qr[..., None] * b_im + qi[..., None] * b_re
    nb = g // SSM_PACK
    eye = jnp.eye(SSM_PACK, dtype=F32)

    def in_blocks(t):
        return jnp.einsum("jgph,gk->jghkp", t.reshape(nb, SSM_PACK, p_, h), eye).reshape(nb, SSM_PACK * h, SSM_PACK * p_)

    def out_blocks(t):
        return jnp.einsum("jghp,gk->jgpkh", t.reshape(nb, SSM_PACK, h, p_), eye).reshape(nb, SSM_PACK * p_, SSM_PACK * h)

    nm = "ssm%d" % l
    ur = bd_matmul(u, in_blocks(bbr), nm + "_bur")
    ui = bd_matmul(u, in_blocks(bbi), nm + "_bui")
    xr, xi = s5_scan(ur, ui, lbr.reshape(1, g * p_), lbi.reshape(1, g * p_), nm + "_scan")
    y = (bd_matmul(xr, out_blocks(sp["ssm_c_re", l]), nm + "_cr") + bd_matmul(xi, out_blocks(-sp["ssm_c_im", l]), nm + "_ci")
         + sp["ssm_d", l].reshape(1, g * h) * u)
    y = jax.nn.gelu(y)
    zz = matmul_w(y, w16["ssm_w_glu", l], toks["ssm_w_glu", l], nm + "_glu")
    z_out, z_gate = split_cols(zz, _contiguous((D_MODEL, D_MODEL)))
    return z_out * jax.nn.sigmoid(z_gate)


def _mla_mixer(q_lat, kv_lat, k_rope, cos, sin, sp, l, w16, toks):
    s = q_lat.shape[0]
    nm = "mla%d" % l
    dqk = MLA_NOPE + MLA_ROPE
    qt = matmul_w_t(rmsnorm(q_lat, sp["mla_q_norm", l], nm + "_qn"), w16["mla_w_uq", l], toks["mla_w_uq", l],
                    nm + "_uq").reshape(MLA_HEADS, dqk, s)
    half = MLA_ROPE // 2
    q1, q2 = qt[:, MLA_NOPE:MLA_NOPE + half], qt[:, MLA_NOPE + half:]
    ct, st = cos.T[None], sin.T[None]
    qt = jnp.concatenate([qt[:, :MLA_NOPE], q1 * ct - q2 * st, q2 * ct + q1 * st], axis=1).reshape(MLA_HEADS * dqk, s)
    kvt = matmul_w_t(rmsnorm(kv_lat, sp["mla_kv_norm", l], nm + "_kvn"), w16["mla_w_ukv", l], toks["mla_w_ukv", l],
                     nm + "_ukv").reshape(MLA_HEADS, MLA_NOPE + MLA_V, s)
    k_pe = _rope(k_rope, cos, sin)
    k = jnp.concatenate([kvt[:, :MLA_NOPE].transpose(0, 2, 1),
                         jnp.broadcast_to(k_pe[None], (MLA_HEADS, s, MLA_ROPE))], axis=-1)
    vt = kvt[:, MLA_NOPE:].reshape(MLA_HEADS * MLA_V, s)
    ot = attention(qt, k, vt, 1.0 / math.sqrt(dqk), True, nm + "_att")
    return matmul_w_at(ot, w16["mla_w_o", l], toks["mla_w_o", l], nm + "_o")


def _hgrn_mixer(q, f_logit, i_in, g, lb, sp, l, w16, toks):
    s = q.shape[0]
    nm = "hg%d" % l
    f = lb + (1.0 - lb) * jax.nn.sigmoid(f_logit)
    o = hgrn_core(jax.nn.silu(q), 1.0 - f, i_in, jnp.log(f), nm + "_core")
    o = rmsnorm(o.reshape(s * HG_HEADS, HG_DV), sp["hg_g_norm", l], nm + "_gn").reshape(s, HG_HEADS * HG_DV)
    return silu_gated_out(o, g, w16["hg_w_o", l], toks["hg_w_o", l], nm + "_o")


def _cross_attention(x, h, mem_n, l, w16, toks):
    s, m = h.shape[0], mem_n.shape[0]
    nm = "xat%d" % l
    xw = X_HEADS * X_HEAD_DIM
    qt = matmul_w_t(h, w16["x_w_q", l], toks["x_w_q", l], nm + "_q")
    kv = matmul_w(mem_n, w16["x_w_kv", l], toks["x_w_kv", l], nm + "_kv")
    k, v = split_cols(kv, _contiguous((xw, xw)))
    k = k.reshape(m, X_HEADS, X_HEAD_DIM).transpose(1, 0, 2)
    ot = attention(qt, k, v.T, 1.0 / math.sqrt(X_HEAD_DIM), False, nm + "_att")
    return matmul_w_at_res(x, ot, w16["x_w_o", l], toks["x_w_o", l], nm + "_o")


def _in_spec():
    hw = HG_HEADS * HG_DK
    sizes = (SSM_WIDTH, MLA_Q_RANK, MLA_KV_RANK, MLA_ROPE, hw, hw, hw, hw, D_MODEL, D_MODEL, D_MODEL)
    pieces, lo = [], 0
    for n in sizes:
        segs, c = [], lo
        while c < lo + n:
            end = min(lo + n, (c // D_IN_SHARD + 1) * D_IN_SHARD)
            segs.append((c + (D_IN_BLOCK - D_IN_SHARD) * (c // D_IN_SHARD), end - c))
            c = end
        pieces.append(tuple(segs))
        lo += n
    assert lo == D_IN
    return tuple(pieces), N_CHIPS * D_IN_BLOCK


def _layer(x, mem, cos, sin, lb, l, sp, w16, toks):
    s = x.shape[0]
    nm = "l%d" % l
    h = rmsnorm(x, sp["norm_mix", l], nm + "_nmix")
    z = matmul_w(h, w16["w_in", l], toks["w_in", l], nm + "_in")
    u, q_lat, kv_lat, k_rope, hg_q, hg_f, hg_i, hg_g, l_ssm, l_mla, l_hg = split_cols(z, _in_spec())
    y_ssm = _s5_mixer(u, sp, l, w16, toks)
    y_mla = _mla_mixer(q_lat, kv_lat, k_rope, cos, sin, sp, l, w16, toks)
    y_hg = _hgrn_mixer(hg_q, hg_f, hg_i, hg_g, lb, sp, l, w16, toks)
    x = gated_merge_out(x, (y_ssm, y_mla, y_hg), (l_ssm, l_mla, l_hg), w16["w_out", l], toks["w_out", l], nm + "_out")
    hc = rmsnorm(x, sp["norm_cross", l], nm + "_ncross")
    mem_n = rmsnorm(mem, sp["norm_mem", l], nm + "_nmem")
    x = _cross_attention(x, hc, mem_n, l, w16, toks)
    hf = rmsnorm(x, sp["norm_ffn", l], nm + "_nffn")
    gt = matmul_w(hf, w16["ffn_w_gate", l], toks["ffn_w_gate", l], nm + "_gate")
    up = matmul_w(hf, w16["ffn_w_up", l], toks["ffn_w_up", l], nm + "_up")
    return swiglu_down(x, gt, up, w16["ffn_w_down", l], toks["ffn_w_down", l], nm + "_down")


def _loss_fn(sp, toks, x, w16, mem, pos, target):
    half = MLA_ROPE // 2
    inv_freq = ROPE_THETA ** (-jnp.arange(half, dtype=F32) / half)
    ang = pos.astype(F32)[:, None] * inv_freq
    cos, sin = jnp.cos(ang), jnp.sin(ang)
    lb_p = jax.nn.softmax(sp["hg_lb", ALL], axis=0)
    lower = jnp.cumsum(lb_p, axis=0) - lb_p[0:1]
    for l in range(DEPTH):
        x = _layer(x, mem, cos, sin, lower[l], l, sp, w16, toks)
    y = rmsnorm(x, sp["norm_final", ALL], "nfinal")
    return 0.5 * jnp.sum(jnp.mean(jnp.square(y - target), axis=-1))


WEIGHTS = ["norm_mix", "w_in", "ssm_lam_re", "ssm_lam_im", "ssm_b_re", "ssm_b_im", "ssm_c_re", "ssm_c_im", "ssm_d",
           "ssm_log_step", "ssm_w_glu", "mla_q_norm", "mla_kv_norm", "mla_w_uq", "mla_w_ukv", "mla_w_o", "hg_lb",
           "hg_g_norm", "hg_w_o", "w_out", "norm_cross", "norm_mem", "x_w_q", "x_w_kv", "x_w_o", "norm_ffn",
           "ffn_w_gate_up", "ffn_w_down", "norm_final"]
INPUTS = ["x", "mem", "positions"] + WEIGHTS + ["loss_target"] + ["m_" + n for n in WEIGHTS] + ["v_" + n for n in WEIGHTS]
BIG = [("w_in", 2), ("ssm_w_glu", 2), ("mla_w_uq", 2), ("mla_w_ukv", 2), ("mla_w_o", 2), ("hg_w_o", 2), ("w_out", 1),
       ("x_w_q", 1), ("x_w_kv", 1), ("x_w_o", 2), ("ffn_w_gate_up", 2), ("ffn_w_down", 1)]
SMALL = [n for n in WEIGHTS if n not in dict(BIG)]
WHOLE = ("hg_lb", "norm_final")
ALL = -1
SMALL_W = 128
BLOCKED = ("w_in", "ffn_w_gate_up")


def _pack(arrs, rows, width, dtype):
    flat = jnp.concatenate([a.astype(dtype).reshape(-1) for a in arrs])
    return jnp.pad(flat, (0, rows * width - flat.shape[0])).reshape(rows, width)


def _unpack(buf, like):
    flat = buf.reshape(-1)
    out, off = [], 0
    for a in like:
        n = math.prod(a.shape)
        out.append(flat[off:off + n].reshape(a.shape))
        off += n
    return out


def kernel(
    x, mem, positions, norm_mix, w_in, ssm_lam_re, ssm_lam_im, ssm_b_re, ssm_b_im, ssm_c_re, ssm_c_im, ssm_d,
    ssm_log_step, ssm_w_glu, mla_q_norm, mla_kv_norm, mla_w_uq, mla_w_ukv, mla_w_o, hg_lb, hg_g_norm, hg_w_o,
    w_out, norm_cross, norm_mem, x_w_q, x_w_kv, x_w_o, norm_ffn, ffn_w_gate_up, ffn_w_down, norm_final,
    loss_target, m_norm_mix, m_w_in, m_ssm_lam_re, m_ssm_lam_im, m_ssm_b_re, m_ssm_b_im, m_ssm_c_re, m_ssm_c_im,
    m_ssm_d, m_ssm_log_step, m_ssm_w_glu, m_mla_q_norm, m_mla_kv_norm, m_mla_w_uq, m_mla_w_ukv, m_mla_w_o,
    m_hg_lb, m_hg_g_norm, m_hg_w_o, m_w_out, m_norm_cross, m_norm_mem, m_x_w_q, m_x_w_kv, m_x_w_o, m_norm_ffn,
    m_ffn_w_gate_up, m_ffn_w_down, m_norm_final, v_norm_mix, v_w_in, v_ssm_lam_re, v_ssm_lam_im, v_ssm_b_re,
    v_ssm_b_im, v_ssm_c_re, v_ssm_c_im, v_ssm_d, v_ssm_log_step, v_ssm_w_glu, v_mla_q_norm, v_mla_kv_norm,
    v_mla_w_uq, v_mla_w_ukv, v_mla_w_o, v_hg_lb, v_hg_g_norm, v_hg_w_o, v_w_out, v_norm_cross, v_norm_mem,
    v_x_w_q, v_x_w_kv, v_x_w_o, v_norm_ffn, v_ffn_w_gate_up, v_ffn_w_down, v_norm_final):
    given = locals()
    a = {n: given[n] for n in INPUTS}
    x, mem, pos, target = a["x"][0], a["mem"][0], a["positions"][0], a["loss_target"][0]
    place = jnp.stack([lax.axis_index("c"), 2 * lax.axis_index("x") + lax.axis_index("y")]).astype(jnp.int32)

    def shard16(n):
        w = a[n].astype(BF16)
        if n == "w_in":
            w = jnp.pad(w, ((0, 0), (0, 0), (0, D_IN_BLOCK - D_IN_SHARD)))
        return w

    gathered = _gather_weights([shard16(n) for n, _ in BIG])
    w16, toks = {}, {}
    for (n, ax), got in zip(BIG, gathered):
        for l in range(DEPTH):
            if n in BLOCKED:
                full = got[l]
            elif ax == 2:
                full = jnp.concatenate([got[l, q] for q in range(N_CHIPS)], axis=1)
            else:
                full = got[l].reshape(N_CHIPS * got.shape[2], got.shape[3])
            parts = {n: full}
            if n == "ffn_w_gate_up":
                half = N_CHIPS // 2
                parts = {"ffn_w_gate": full[:half], "ffn_w_up": full[half:]}
            for name, w in parts.items():
                w16[name, l] = w
                toks[name, l] = jnp.zeros(w.shape, F32)
    sp = {}
    for n in SMALL:
        if n in WHOLE:
            sp[n, ALL] = a[n]
        else:
            for l in range(DEPTH):
                sp[n, l] = a[n][l]

    loss, (g_sp, g_tok, g_x) = jax.value_and_grad(_loss_fn, argnums=(0, 1, 2))(sp, toks, x, w16, mem, pos, target)
    g_small = {n: g_sp[n, ALL] if n in WHOLE else jnp.stack([g_sp[n, l] for l in range(DEPTH)]) for n in SMALL}

    grads, names = [], []
    for n, ax in BIG:
        for l in range(DEPTH):
            if n == "ffn_w_gate_up":
                g = jnp.concatenate([g_tok["ffn_w_gate", l], g_tok["ffn_w_up", l]], axis=0)
            else:
                g = g_tok[n, l]
            if n in BLOCKED:
                grads.append(g)
            elif ax == 2:
                grads.append(g.reshape(g.shape[0], N_CHIPS, g.shape[1] // N_CHIPS).transpose(1, 0, 2))
            else:
                grads.append(g.reshape(N_CHIPS, g.shape[0] // N_CHIPS, g.shape[1]))
            names.append("%s%d" % (n, l))
    from_sibling = _swap_halves_to_sibling(grads)
    chip_sums = [_add_half(g, t, place, "grad_add_half_" + n) for n, g, t in zip(names, grads, from_sibling)]
    from_chips = _scatter_partials(chip_sums)
    totals = [_add_partials(g, t1, t2, place, "grad_add_partials_" + n)
              for n, g, t1, t2 in zip(names, grads, from_sibling, from_chips)]
    joined = _join_halves(totals)
    g_big = {n: jnp.stack([joined[DEPTH * i + l][:, :a[n].shape[2]] for l in range(DEPTH)])
             for i, (n, _) in enumerate(BIG)}

    small_like = [a[n] for n in SMALL] + [jnp.zeros((1,), F32)]
    n_small = sum(math.prod(t.shape) for t in small_like)
    srows = -(-n_small // (SMALL_W * 8 * N_DEV)) * 8 * N_DEV
    reduced = _allreduce_small(_pack([g_small[n] for n in SMALL] + [loss.reshape(1)], srows, SMALL_W, F32))
    red = _unpack(reduced, small_like)
    g_sm = dict(zip(SMALL, red[:-1]))
    loss = red[-1].reshape(())

    out_g, out_d, out_m, out_v = {}, {}, {}, {}
    for n, _ in BIG:
        shp = a[n].shape
        two = lambda t: t.reshape(shp[0] * shp[1], shp[2])
        d, m, v = _adamw_call(two(a[n]), two(g_big[n]), two(a["m_" + n]), two(a["v_" + n]), "adamw_" + n)
        out_g[n], out_d[n], out_m[n], out_v[n] = g_big[n], d.reshape(shp), m.reshape(shp), v.reshape(shp)
    small_arrs = [a[n] for n in SMALL]
    d, m, v = _adamw_call(
        _pack(small_arrs, srows, SMALL_W, F32), reduced,
        _pack([a["m_" + n] for n in SMALL], srows, SMALL_W, F32),
        _pack([a["v_" + n] for n in SMALL], srows, SMALL_W, F32), "adamw_small")
    for n, gg, dd, mm_, vv in zip(SMALL, red[:-1], _unpack(d, small_arrs), _unpack(m, small_arrs), _unpack(v, small_arrs)):
        out_g[n], out_d[n], out_m[n], out_v[n] = gg, dd, mm_, vv

    return (loss, g_x[None], *[out_g[n] for n in WEIGHTS], *[out_d[n] for n in WEIGHTS],
            *[out_m[n] for n in WEIGHTS], *[out_v[n] for n in WEIGHTS])
```

```python
import functools
import math

import jax
import jax.numpy as jnp
from jax import lax
from jax.experimental import pallas as pl
from jax.experimental.pallas import tpu as pltpu

F32 = jnp.float32
BF16 = jnp.bfloat16
MESH = pl.DeviceIdType.MESH

D_MODEL = 1024
DEPTH = 2
RMS_EPS = 1e-6
SSM_GROUPS, SSM_GROUP_CH, SSM_STATE = 32, 16, 64
SSM_WIDTH = SSM_GROUPS * SSM_GROUP_CH
SSM_LANES = SSM_GROUPS * SSM_STATE
SSM_PACK = 8
MLA_HEADS, MLA_Q_RANK, MLA_KV_RANK, MLA_NOPE, MLA_ROPE, MLA_V = 8, 512, 256, 64, 32, 64
ROPE_THETA = 10000.0
HG_HEADS, HG_DK, HG_DV, HG_CHUNK = 4, 128, 128, 64
HG_SUB_FWD, HG_SUB_BWD = 32, 16
X_HEADS, X_HEAD_DIM = 4, 128
D_FF = 2816
D_IN = 6432
N_CHIPS = 4
D_IN_SHARD = D_IN // N_CHIPS
D_IN_BLOCK = 1664
MASK_VALUE = -1e30

ADAM_LR, ADAM_B1, ADAM_B2, ADAM_EPS, ADAM_WD, ADAM_STEP = 0.001, 0.9, 0.999, 1e-08, 0.01, 10

VMEM_LIMIT = 48 * 1024 * 1024


def _pcall(body, **kw):
    return pl.pallas_call(body, **kw)


def _params(sem):
    return pltpu.CompilerParams(dimension_semantics=sem, vmem_limit_bytes=VMEM_LIMIT)


def _tile(n, cap, align=128):
    if n <= cap:
        return n
    t = (cap // align) * align
    while t >= align:
        if n % t == 0:
            return t
        t -= align
    return n


MM_VMEM_BUDGET = 36 * 1024 * 1024
MM_TILE_CAP = 2048


def _divisors(n, cap, align=128):
    out = [n] if n <= cap else []
    out += [t for t in range(align, min(n, cap + 1), align) if n % t == 0]
    return sorted(set(out), reverse=True)


@functools.lru_cache(maxsize=None)
def _mm_tiles(m, n, k, a_bytes, b_bytes, o_bytes, n_unit, k_unit):
    best = None
    for tm in _divisors(m, MM_TILE_CAP):
        for tn in _divisors(n_unit, MM_TILE_CAP):
            for tk in _divisors(k_unit, MM_TILE_CAP):
                nk = k // tk
                need = 2 * (tm * tk * a_bytes + tk * tn * b_bytes + tm * tn * o_bytes)
                need += tm * tn * 4 * (2 if nk > 1 else 1)
                need += (tm * tk * 2 if a_bytes > 2 else 0) + (tk * tn * 2 if b_bytes > 2 else 0)
                if need > MM_VMEM_BUDGET:
                    continue
                key = ((m // tm) * (n // tn) * nk, nk, -tn)
                if best is None or key < best[0]:
                    best = (key, (tm, tn, tk))
    assert best is not None, (m, n, k)
    return best[1]


def _mm(a, b, *, ta=False, tb=False, out_dtype=F32, name="mm", b_blocks=False, out_blocks=0, pro=None, epi=None,
        save_left=False):
    if ta:
        K, M = a.shape
    else:
        M, K = a.shape
    n_unit = k_unit = None
    if b_blocks:
        nb, rows, cols = b.shape
        if tb:
            N, Kb, k_unit = rows, nb * cols, cols
        else:
            Kb, N, n_unit = rows, nb * cols, cols
    elif tb:
        N, Kb = b.shape
    else:
        Kb, N = b.shape
    if out_blocks:
        assert not b_blocks and N % out_blocks == 0
        n_unit = N // out_blocks
    assert K == Kb, (a.shape, b.shape, ta, tb)
    n_pro = len(pro) - 1 if pro else 0
    n_epi = len(epi) - 1 if epi else 0
    tm, tn, tk = _mm_tiles(M, N, K, a.dtype.itemsize * (1 + n_pro), b.dtype.itemsize,
                           jnp.dtype(out_dtype).itemsize * max(1, 2 * n_epi), n_unit or N, k_unit or K)
    nk = K // tk
    dims = (((0,) if ta else (1,), (1,) if tb else (0,)), ((), ()))
    n_in = 2 + n_pro + n_epi
    n_out = max(1, n_epi) + (1 if save_left else 0)
    assert not save_left or (pro and tn == N)

    def body(*refs):
        ins, outs, acc = refs[:n_in], refs[n_in:n_in + n_out], refs[n_in + n_out:]
        b_ref = ins[1 + n_pro]
        av = (pro[0](*[r[...] for r in ins[:1 + n_pro]]) if pro else ins[0][...]).astype(BF16)
        if save_left:
            outs[-1][...] = av
        bv = b_ref[...].astype(BF16)
        part = lax.dot_general(av, bv, dims, preferred_element_type=F32)

        def finish(total):
            if epi:
                for o_ref, r in zip(outs, epi[0](total, *[e[...] for e in ins[2 + n_pro:]])):
                    o_ref[...] = r.astype(o_ref.dtype)
            else:
                outs[0][...] = total.astype(outs[0].dtype)

        if nk == 1:
            finish(part)
        else:
            acc_ref = acc[0]
            k = pl.program_id(2)

            @pl.when(k == 0)
            def _():
                acc_ref[...] = part

            @pl.when(k > 0)
            def _():
                acc_ref[...] += part

            @pl.when(k == nk - 1)
            def _():
                finish(acc_ref[...])

    a_spec = pl.BlockSpec((tk, tm), lambda i, j, k: (k, i)) if ta else pl.BlockSpec((tm, tk), lambda i, j, k: (i, k))
    if b_blocks and tb:
        r = k_unit // tk
        b_spec = pl.BlockSpec((None, tn, tk), lambda i, j, k: (k // r, j, k % r))
    elif b_blocks:
        r = n_unit // tn
        b_spec = pl.BlockSpec((None, tk, tn), lambda i, j, k: (j // r, k, j % r))
    elif tb:
        b_spec = pl.BlockSpec((tn, tk), lambda i, j, k: (j, k))
    else:
        b_spec = pl.BlockSpec((tk, tn), lambda i, j, k: (k, j))
    if out_blocks:
        r = n_unit // tn
        out_spec = pl.BlockSpec((None, tm, tn), lambda i, j, k: (j // r, i, j % r))
        out_shape = jax.ShapeDtypeStruct((out_blocks, M, n_unit), out_dtype)
    else:
        out_spec = pl.BlockSpec((tm, tn), lambda i, j, k: (i, j))
        out_shape = jax.ShapeDtypeStruct((M, N), out_dtype)
    in_specs = [a_spec] * (1 + n_pro) + [b_spec] + [out_spec] * n_epi
    args = [a] + (list(pro[1:]) if pro else []) + [b] + (list(epi[1:]) if epi else [])
    out_specs, out_shapes = ([out_spec] * n_epi, [out_shape] * n_epi) if epi else (out_spec, out_shape)
    if save_left:
        out_specs = [out_spec] * max(1, n_epi) + [a_spec]
        out_shapes = [out_shape] * max(1, n_epi) + [jax.ShapeDtypeStruct(a.shape, BF16)]
    return _pcall(
        body,
        name=name,
        grid=(M // tm, N // tn, nk),
        in_specs=in_specs,
        out_specs=out_specs,
        out_shape=out_shapes,
        scratch_shapes=[pltpu.VMEM((tm, tn), F32)] if nk > 1 else [],
        compiler_params=_params(("parallel", "parallel", "arbitrary")),
    )(*args)


def _bd_call(a, bc, transpose, name):
    m = a.shape[0]
    nb, ka, kb = bc.shape
    win, wout = (kb, ka) if transpose else (ka, kb)
    assert a.shape[1] == nb * win
    tm = _tile(m, 1024)
    dims = _NT if transpose else (((1,), (0,)), ((), ()))

    def body(a_ref, b_ref, o_ref):
        o_ref[...] = lax.dot_general(a_ref[...].astype(BF16), b_ref[0].astype(BF16), dims, preferred_element_type=F32)

    return _pcall(
        body,
        name=name,
        grid=(m // tm, nb),
        in_specs=[pl.BlockSpec((tm, win), lambda i, j: (i, j)), pl.BlockSpec((1, ka, kb), lambda i, j: (j, 0, 0))],
        out_specs=pl.BlockSpec((tm, wout), lambda i, j: (i, j)),
        out_shape=jax.ShapeDtypeStruct((m, nb * wout), F32),
        compiler_params=_params(("parallel", "parallel")),
    )(a, bc)


def _bd_dw_call(a, g, nb, name):
    m = a.shape[0]
    ka, kb = a.shape[1] // nb, g.shape[1] // nb
    tk = _tile(m, 1024)
    nk = m // tk

    def body(a_ref, g_ref, o_ref):
        part = lax.dot_general(a_ref[...].astype(BF16), g_ref[...].astype(BF16), _TN, preferred_element_type=F32)
        k = pl.program_id(1)

        @pl.when(k == 0)
        def _():
            o_ref[0] = part

        @pl.when(k > 0)
        def _():
            o_ref[0] += part

    return _pcall(
        body,
        name=name,
        grid=(nb, nk),
        in_specs=[pl.BlockSpec((tk, ka), lambda j, k: (k, j)), pl.BlockSpec((tk, kb), lambda j, k: (k, j))],
        out_specs=pl.BlockSpec((1, ka, kb), lambda j, k: (j, 0, 0)),
        out_shape=jax.ShapeDtypeStruct((nb, ka, kb), F32),
        compiler_params=_params(("parallel", "arbitrary")),
    )(a, g)


@functools.partial(jax.custom_vjp, nondiff_argnums=(2,))
def bd_matmul(a, bc, name):
    return _bd_call(a, bc, False, name + "_f")


def _bd_matmul_fwd(a, bc, name):
    return _bd_call(a, bc, False, name + "_f"), (a, bc)


def _bd_matmul_bwd(name, res, g):
    a, bc = res
    return _bd_call(g, bc, True, name + "_da"), _bd_dw_call(a, g, bc.shape[0], name + "_dw")


bd_matmul.defvjp(_bd_matmul_fwd, _bd_matmul_bwd)


@functools.partial(jax.custom_vjp, nondiff_argnums=(3,))
def matmul_w(a, w16, tok, name):
    return _mm(a, w16, b_blocks=w16.ndim == 3, name=name + "_f")


def _matmul_w_fwd(a, w16, tok, name):
    return _mm(a, w16, b_blocks=w16.ndim == 3, name=name + "_f"), (a, w16)


def _matmul_w_bwd(name, res, g):
    a, w16 = res
    blocks = w16.shape[0] if w16.ndim == 3 else 0
    return (_mm(g, w16, tb=True, b_blocks=bool(blocks), name=name + "_da"), jnp.zeros_like(w16),
            _mm(a, g, ta=True, out_blocks=blocks, name=name + "_dw"))


matmul_w.defvjp(_matmul_w_fwd, _matmul_w_bwd)


def _swiglu(gt, up):
    return gt * jax.nn.sigmoid(gt) * up


def _swiglu_bwd(dact, gt, up):
    sg = jax.nn.sigmoid(gt)
    return dact * up * sg * (1.0 + gt * (1.0 - sg)), dact * gt * sg


def _add_residual(product, x):
    return (product + x,)


def _silu_gate(o, g):
    return o * (g * jax.nn.sigmoid(g))


def _silu_gate_bwd(dm, o, g):
    sg = jax.nn.sigmoid(g)
    return dm * g * sg, dm * o * sg * (1.0 + g * (1.0 - sg))


@functools.partial(jax.custom_vjp, nondiff_argnums=(4,))
def silu_gated_out(o, g, w16, tok, name):
    return _mm(o, w16, pro=(_silu_gate, g), name=name + "_f")


def _silu_gated_out_fwd(o, g, w16, tok, name):
    out, left16 = _mm(o, w16, pro=(_silu_gate, g), save_left=True, name=name + "_f")
    return out, (o, g, w16, left16)


def _silu_gated_out_bwd(name, res, d):
    o, g, w16, left16 = res
    d_o, d_g = _mm(d, w16, tb=True, epi=(_silu_gate_bwd, o, g), name=name + "_da")
    return d_o, d_g, jnp.zeros_like(w16), _mm(left16, d, ta=True, name=name + "_dw")


silu_gated_out.defvjp(_silu_gated_out_fwd, _silu_gated_out_bwd)


@functools.partial(jax.custom_vjp, nondiff_argnums=(5,))
def swiglu_down(x, gt, up, w16, tok, name):
    return _mm(gt, w16, pro=(_swiglu, up), epi=(_add_residual, x), name=name + "_f")[0]


def _swiglu_down_fwd(x, gt, up, w16, tok, name):
    out, act16 = _mm(gt, w16, pro=(_swiglu, up), epi=(_add_residual, x), save_left=True, name=name + "_f")
    return out, (gt, up, w16, act16)


def _swiglu_down_bwd(name, res, g):
    gt, up, w16, act16 = res
    d_gt, d_up = _mm(g, w16, tb=True, epi=(_swiglu_bwd, gt, up), name=name + "_da")
    return g, d_gt, d_up, jnp.zeros_like(w16), _mm(act16, g, ta=True, name=name + "_dw")


swiglu_down.defvjp(_swiglu_down_fwd, _swiglu_down_bwd)


def _merge(y1, y2, y3, l1, l2, l3):
    return jax.nn.sigmoid(l1) * y1 + jax.nn.sigmoid(l2) * y2 + jax.nn.sigmoid(l3) * y3


def _merge_bwd(dm, y1, y2, y3, l1, l2, l3):
    dys, dls = [], []
    for y, logit in ((y1, l1), (y2, l2), (y3, l3)):
        s = jax.nn.sigmoid(logit)
        dys.append(dm * s)
        dls.append(dm * y * s * (1.0 - s))
    return (*dys, *dls)


@functools.partial(jax.custom_vjp, nondiff_argnums=(5,))
def gated_merge_out(x, ys, logits, w16, tok, name):
    return _mm(ys[0], w16, pro=(_merge, *ys[1:], *logits), epi=(_add_residual, x), name=name + "_f")[0]


def _gated_merge_out_fwd(x, ys, logits, w16, tok, name):
    out, merged16 = _mm(ys[0], w16, pro=(_merge, *ys[1:], *logits), epi=(_add_residual, x), save_left=True,
                        name=name + "_f")
    return out, (ys, logits, w16, merged16)


def _gated_merge_out_bwd(name, res, g):
    ys, logits, w16, merged16 = res
    d = _mm(g, w16, tb=True, epi=(_merge_bwd, *ys, *logits), name=name + "_da")
    return g, tuple(d[:3]), tuple(d[3:]), jnp.zeros_like(w16), _mm(merged16, g, ta=True, name=name + "_dw")


gated_merge_out.defvjp(_gated_merge_out_fwd, _gated_merge_out_bwd)


@functools.partial(jax.custom_vjp, nondiff_argnums=(3,))
def matmul_w_t(a, w16, tok, name):
    return _mm(w16, a, ta=True, tb=True, name=name + "_f")


def _matmul_w_t_fwd(a, w16, tok, name):
    return _mm(w16, a, ta=True, tb=True, name=name + "_f"), (a, w16)


def _matmul_w_t_bwd(name, res, g):
    a, w16 = res
    return (_mm(g, w16, ta=True, tb=True, name=name + "_da"), jnp.zeros_like(w16),
            _mm(a, g, ta=True, tb=True, name=name + "_dw"))


matmul_w_t.defvjp(_matmul_w_t_fwd, _matmul_w_t_bwd)


@functools.partial(jax.custom_vjp, nondiff_argnums=(3,))
def matmul_w_at(at, w16, tok, name):
    return _mm(at, w16, ta=True, name=name + "_f")


def _matmul_w_at_fwd(at, w16, tok, name):
    return _mm(at, w16, ta=True, name=name + "_f"), (at, w16)


def _matmul_w_at_bwd(name, res, g):
    at, w16 = res
    return _mm(w16, g, tb=True, name=name + "_da"), jnp.zeros_like(w16), _mm(at, g, name=name + "_dw")


matmul_w_at.defvjp(_matmul_w_at_fwd, _matmul_w_at_bwd)


@functools.partial(jax.custom_vjp, nondiff_argnums=(4,))
def matmul_w_at_res(x, at, w16, tok, name):
    return _mm(at, w16, ta=True, epi=(_add_residual, x), name=name + "_f")[0]


def _matmul_w_at_res_fwd(x, at, w16, tok, name):
    return _mm(at, w16, ta=True, epi=(_add_residual, x), name=name + "_f")[0], (at, w16)


def _matmul_w_at_res_bwd(name, res, g):
    return (g, *_matmul_w_at_bwd(name, res, g))


matmul_w_at_res.defvjp(_matmul_w_at_res_fwd, _matmul_w_at_res_bwd)


def _rms_fwd_call(x, g, name):
    rows, d = x.shape
    tr = _tile(rows, 512, 8)

    def body(x_ref, g_ref, y_ref):
        xv = x_ref[...]
        rstd = lax.rsqrt(jnp.mean(xv * xv, axis=-1, keepdims=True) + RMS_EPS)
        y_ref[...] = xv * rstd * g_ref[...]

    return _pcall(
        body,
        name=name,
        grid=(rows // tr,),
        in_specs=[pl.BlockSpec((tr, d), lambda i: (i, 0)), pl.BlockSpec((1, d), lambda i: (0, 0))],
        out_specs=pl.BlockSpec((tr, d), lambda i: (i, 0)),
        out_shape=jax.ShapeDtypeStruct((rows, d), F32),
        compiler_params=_params(("parallel",)),
    )(x, g.reshape(1, d))


def _rms_bwd_call(x, g, dy, name):
    rows, d = x.shape
    tr = _tile(rows, 512, 8)
    nb = rows // tr

    def body(x_ref, g_ref, dy_ref, dx_ref, dg_ref):
        xv = x_ref[...]
        rstd = lax.rsqrt(jnp.mean(xv * xv, axis=-1, keepdims=True) + RMS_EPS)
        xh = xv * rstd
        dyv = dy_ref[...]
        dyg = dyv * g_ref[...]
        dx_ref[...] = rstd * (dyg - xh * jnp.mean(dyg * xh, axis=-1, keepdims=True))
        dg_ref[...] = jnp.sum((dyv * xh).reshape(tr // 8, 8, d), axis=0)

    dx, dgp = _pcall(
        body,
        name=name,
        grid=(nb,),
        in_specs=[
            pl.BlockSpec((tr, d), lambda i: (i, 0)),
            pl.BlockSpec((1, d), lambda i: (0, 0)),
            pl.BlockSpec((tr, d), lambda i: (i, 0)),
        ],
        out_specs=[pl.BlockSpec((tr, d), lambda i: (i, 0)), pl.BlockSpec((8, d), lambda i: (i, 0))],
        out_shape=[jax.ShapeDtypeStruct((rows, d), F32), jax.ShapeDtypeStruct((nb * 8, d), F32)],
        compiler_params=_params(("parallel",)),
    )(x, g.reshape(1, d), dy)
    return dx, jnp.sum(dgp, axis=0)


@functools.partial(jax.custom_vjp, nondiff_argnums=(2,))
def rmsnorm(x, g, name):
    return _rms_fwd_call(x, g, name + "_f")


def _rmsnorm_fwd(x, g, name):
    return _rms_fwd_call(x, g, name + "_f"), (x, g)


def _rmsnorm_bwd(name, res, dy):
    x, g = res
    return _rms_bwd_call(x, g, dy, name + "_b")


rmsnorm.defvjp(_rmsnorm_fwd, _rmsnorm_bwd)


def _attn_blocks(sq, sk):
    return _tile(sq, 1024), _tile(sk, 512)


def _attn_pt(qs, k_ref, qi, ki, bq, bk, masked, shift):
    st = jnp.dot(k_ref[0].astype(BF16), qs, preferred_element_type=F32)
    if masked:
        kpos = ki * bk + lax.broadcasted_iota(jnp.int32, (bk, bq), 0)
        qpos = qi * bq + lax.broadcasted_iota(jnp.int32, (bk, bq), 1)
        st = jnp.where(kpos <= qpos, st, MASK_VALUE)
    return st if shift is None else jnp.exp(st - shift)


def _attn_pairs(nq, nk, bq, bk, causal, k_major):
    ok = lambda i, j: (not causal) or j * bk <= i * bq + bq - 1
    if k_major:
        pairs = [(i, j) for j in range(nk) for i in range(nq) if ok(i, j)]
    else:
        pairs = [(i, j) for i in range(nq) for j in range(nk) if ok(i, j)]
    qtab = jnp.asarray([p[0] for p in pairs], jnp.int32)
    ktab = jnp.asarray([p[1] for p in pairs], jnp.int32)
    first_q = (lambda j: (j * bk) // bq) if causal else (lambda j: 0)
    last_k = (lambda i: jnp.minimum((i * bq + bq - 1) // bk, nk - 1)) if causal else (lambda i: nk - 1)
    return qtab, ktab, first_q, last_k


def _attn_cases(causal, qi, ki, bq, bk, step):
    if not causal:
        step(False)
        return
    pl.when(ki * bk + bk - 1 <= qi * bq)(functools.partial(step, False))
    pl.when(ki * bk + bk - 1 > qi * bq)(functools.partial(step, True))


def _attn_fwd_call(qt, k, vt, scale, causal, name):
    h, sk, dq = k.shape
    sq = qt.shape[1]
    dv = vt.shape[0] // h
    bq, bk = _attn_blocks(sq, sk)
    nq, nk = sq // bq, sk // bk

    qtab, ktab, _, last_k = _attn_pairs(nq, nk, bq, bk, causal, False)

    def body(qi_ref, ki_ref, q_ref, k_ref, v_ref, o_ref, lse_ref, m_sc, l_sc, acc_sc):
        qi, ki = qi_ref[pl.program_id(1)], ki_ref[pl.program_id(1)]

        @pl.when(ki == 0)
        def _():
            m_sc[...] = jnp.full(m_sc.shape, -jnp.inf, F32)
            l_sc[...] = jnp.zeros(l_sc.shape, F32)
            acc_sc[...] = jnp.zeros(acc_sc.shape, F32)

        def step(masked):
            qs = (q_ref[...] * scale).astype(BF16)
            st = _attn_pt(qs, k_ref, qi, ki, bq, bk, masked, None)
            m_prev = m_sc[...]
            m_new = jnp.maximum(m_prev, jnp.max(st, axis=0, keepdims=True))
            alpha = jnp.exp(m_prev - m_new)
            pt = jnp.exp(st - m_new)
            l_sc[...] = alpha * l_sc[...] + jnp.sum(pt, axis=0, keepdims=True)
            acc_sc[...] = alpha * acc_sc[...] + jnp.dot(v_ref[...].astype(BF16), pt.astype(BF16),
                                                        preferred_element_type=F32)
            m_sc[...] = m_new

        _attn_cases(causal, qi, ki, bq, bk, step)

        @pl.when(ki == last_k(qi))
        def _():
            o_ref[...] = acc_sc[...] / l_sc[...]
            lse_ref[0] = m_sc[...] + jnp.log(l_sc[...])

    qmap = lambda hh, p, qi_ref, ki_ref: (hh, qi_ref[p])
    return _pcall(
        body,
        name=name,
        grid_spec=pltpu.PrefetchScalarGridSpec(
            num_scalar_prefetch=2,
            grid=(h, qtab.shape[0]),
            in_specs=[
                pl.BlockSpec((dq, bq), qmap),
                pl.BlockSpec((1, bk, dq), lambda hh, p, qi_ref, ki_ref: (hh, ki_ref[p], 0)),
                pl.BlockSpec((dv, bk), lambda hh, p, qi_ref, ki_ref: (hh, ki_ref[p])),
            ],
            out_specs=[pl.BlockSpec((dv, bq), qmap),
                       pl.BlockSpec((1, 1, bq), lambda hh, p, qi_ref, ki_ref: (hh, 0, qi_ref[p]))],
            scratch_shapes=[pltpu.VMEM((1, bq), F32), pltpu.VMEM((1, bq), F32), pltpu.VMEM((dv, bq), F32)],
        ),
        out_shape=[jax.ShapeDtypeStruct((h * dv, sq), F32), jax.ShapeDtypeStruct((h, 1, sq), F32)],
        compiler_params=_params(("parallel", "arbitrary")),
    )(qtab, ktab, qt, k, vt)


_TN = (((0,), (0,)), ((), ()))
_NT = (((1,), (1,)), ((), ()))


def _attn_dst(qs, k_ref, v_ref, do_ref, lse_ref, dl_ref, qi, ki, bq, bk, masked):
    pt = _attn_pt(qs, k_ref, qi, ki, bq, bk, masked, lse_ref[0])
    dpt = lax.dot_general(v_ref[...].astype(BF16), do_ref[...].astype(BF16), _TN, preferred_element_type=F32)
    return pt, pt * (dpt - dl_ref[0])


def _attn_dq_call(qt, k, vt, dot, lse, delta, scale, causal, name):
    h, sk, dq = k.shape
    sq = qt.shape[1]
    dv = vt.shape[0] // h
    bq, bk = _attn_blocks(sq, sk)
    nq, nk = sq // bq, sk // bk

    qtab, ktab, _, last_k = _attn_pairs(nq, nk, bq, bk, causal, False)

    def body(qi_ref, ki_ref, q_ref, k_ref, v_ref, do_ref, lse_ref, dl_ref, dq_ref, acc_sc):
        qi, ki = qi_ref[pl.program_id(1)], ki_ref[pl.program_id(1)]

        @pl.when(ki == 0)
        def _():
            acc_sc[...] = jnp.zeros(acc_sc.shape, F32)

        def step(masked):
            qs = (q_ref[...] * scale).astype(BF16)
            _, dst = _attn_dst(qs, k_ref, v_ref, do_ref, lse_ref, dl_ref, qi, ki, bq, bk, masked)
            acc_sc[...] += lax.dot_general(k_ref[0].astype(BF16), dst.astype(BF16), _TN, preferred_element_type=F32)

        _attn_cases(causal, qi, ki, bq, bk, step)

        @pl.when(ki == last_k(qi))
        def _():
            dq_ref[...] = acc_sc[...] * scale

    qmap = lambda hh, p, qi_ref, ki_ref: (hh, qi_ref[p])
    rowmap = lambda hh, p, qi_ref, ki_ref: (hh, 0, qi_ref[p])
    return _pcall(
        body,
        name=name,
        grid_spec=pltpu.PrefetchScalarGridSpec(
            num_scalar_prefetch=2,
            grid=(h, qtab.shape[0]),
            in_specs=[
                pl.BlockSpec((dq, bq), qmap),
                pl.BlockSpec((1, bk, dq), lambda hh, p, qi_ref, ki_ref: (hh, ki_ref[p], 0)),
                pl.BlockSpec((dv, bk), lambda hh, p, qi_ref, ki_ref: (hh, ki_ref[p])),
                pl.BlockSpec((dv, bq), qmap),
                pl.BlockSpec((1, 1, bq), rowmap),
                pl.BlockSpec((1, 1, bq), rowmap),
            ],
            out_specs=pl.BlockSpec((dq, bq), qmap),
            scratch_shapes=[pltpu.VMEM((dq, bq), F32)],
        ),
        out_shape=jax.ShapeDtypeStruct((h * dq, sq), F32),
        compiler_params=_params(("parallel", "arbitrary")),
    )(qtab, ktab, qt, k, vt, dot, lse, delta)


def _attn_dkv_call(qt, k, vt, dot, lse, delta, scale, causal, name):
    h, sk, dq = k.shape
    sq = qt.shape[1]
    dv = vt.shape[0] // h
    bq, bk = _attn_blocks(sq, sk)
    nq, nk = sq // bq, sk // bk

    qtab, ktab, first_q, _ = _attn_pairs(nq, nk, bq, bk, causal, True)

    def body(qi_ref, ki_ref, q_ref, k_ref, v_ref, do_ref, lse_ref, dl_ref, dk_ref, dv_ref, dk_sc, dv_sc):
        qi, ki = qi_ref[pl.program_id(1)], ki_ref[pl.program_id(1)]

        @pl.when(qi == first_q(ki))
        def _():
            dk_sc[...] = jnp.zeros(dk_sc.shape, F32)
            dv_sc[...] = jnp.zeros(dv_sc.shape, F32)

        def step(masked):
            qs = (q_ref[...] * scale).astype(BF16)
            pt, dst = _attn_dst(qs, k_ref, v_ref, do_ref, lse_ref, dl_ref, qi, ki, bq, bk, masked)
            dv_sc[...] += lax.dot_general(do_ref[...].astype(BF16), pt.astype(BF16), _NT, preferred_element_type=F32)
            dk_sc[...] += lax.dot_general(dst.astype(BF16), qs, _NT, preferred_element_type=F32)

        _attn_cases(causal, qi, ki, bq, bk, step)

        @pl.when(qi == nq - 1)
        def _():
            dk_ref[0] = dk_sc[...]
            dv_ref[...] = dv_sc[...]

    qmap = lambda hh, p, qi_ref, ki_ref: (hh, qi_ref[p])
    rowmap = lambda hh, p, qi_ref, ki_ref: (hh, 0, qi_ref[p])
    kmap = lambda hh, p, qi_ref, ki_ref: (hh, ki_ref[p], 0)
    vmap = lambda hh, p, qi_ref, ki_ref: (hh, ki_ref[p])
    return _pcall(
        body,
        name=name,
        grid_spec=pltpu.PrefetchScalarGridSpec(
            num_scalar_prefetch=2,
            grid=(h, qtab.shape[0]),
            in_specs=[
                pl.BlockSpec((dq, bq), qmap),
                pl.BlockSpec((1, bk, dq), kmap),
                pl.BlockSpec((dv, bk), vmap),
                pl.BlockSpec((dv, bq), qmap),
                pl.BlockSpec((1, 1, bq), rowmap),
                pl.BlockSpec((1, 1, bq), rowmap),
            ],
            out_specs=[pl.BlockSpec((1, bk, dq), kmap), pl.BlockSpec((dv, bk), vmap)],
            scratch_shapes=[pltpu.VMEM((bk, dq), F32), pltpu.VMEM((dv, bk), F32)],
        ),
        out_shape=[jax.ShapeDtypeStruct((h, sk, dq), F32), jax.ShapeDtypeStruct((h * dv, sk), F32)],
        compiler_params=_params(("parallel", "arbitrary")),
    )(qtab, ktab, qt, k, vt, dot, lse, delta)


@functools.partial(jax.custom_vjp, nondiff_argnums=(3, 4, 5))
def attention(qt, k, vt, scale, causal, name):
    return _attn_fwd_call(qt, k, vt, scale, causal, name + "_f")[0]


def _attention_fwd(qt, k, vt, scale, causal, name):
    ot, lse = _attn_fwd_call(qt, k, vt, scale, causal, name + "_f")
    return ot, (qt, k, vt, ot, lse)


def _attention_bwd(scale, causal, name, res, dot):
    qt, k, vt, ot, lse = res
    h = k.shape[0]
    delta = jnp.sum((dot * ot).reshape(h, ot.shape[0] // h, ot.shape[1]), axis=1, keepdims=True)
    dqt = _attn_dq_call(qt, k, vt, dot, lse, delta, scale, causal, name + "_dq")
    dk, dvt = _attn_dkv_call(qt, k, vt, dot, lse, delta, scale, causal, name + "_dkv")
    return dqt, dk, dvt


attention.defvjp(_attention_fwd, _attention_bwd)


SCAN_ROWS = 8
SCAN_LANES = 256
SCAN_TBLOCK = 512
SCAN_UNROLL = 4


def _scan_tables(lr, li):
    pr, pi = [lr], [li]
    for _ in range(SCAN_ROWS - 1):
        pr, pi = pr + [pr[-1] * lr - pi[-1] * li], pi + [pr[-1] * li + pi[-1] * lr]
    return jnp.concatenate(pr, axis=0), jnp.concatenate(pi, axis=0)


def _scan_call(ur, ui, lr, li, reverse, name, states=None):
    s, n = ur.shape
    tb = _tile(s, SCAN_TBLOCK, 8)
    nt, nl = s // tb, n // SCAN_LANES
    ntile = tb // SCAN_ROWS
    pr, pi = _scan_tables(lr, li)
    if reverse:
        pr, pi = pr[::-1], pi[::-1]
    shape = (SCAN_ROWS, SCAN_LANES)
    with_states = states is not None
    assert reverse or not with_states

    def body(*refs):
        if with_states:
            ur_ref, ui_ref, pr_ref, pi_ref, sr_ref, si_ref, xr_ref, xi_ref, dr_ref, di_ref, cr_sc, ci_sc, ar_sc, ai_sc = refs
        else:
            ur_ref, ui_ref, pr_ref, pi_ref, xr_ref, xi_ref, cr_sc, ci_sc, ar_sc, ai_sc = refs

        @pl.when(pl.program_id(1) == 0)
        def _():
            for sc in (cr_sc, ci_sc, ar_sc, ai_sc):
                sc[...] = jnp.zeros(shape, F32)

        prv, piv = pr_ref[...], pi_ref[...]
        rows = lax.broadcasted_iota(jnp.int32, shape, 0)

        def powers(k):
            r = (SCAN_ROWS - k) if reverse else (k - 1)
            return jnp.broadcast_to(prv[r:r + 1], shape), jnp.broadcast_to(piv[r:r + 1], shape)

        pw = [powers(k) for k in (1, 2, 4)]

        def tile(i, carry):
            cr, ci, acr, aci = carry
            j = (ntile - 1 - i) if reverse else i
            sl = pl.ds(pl.multiple_of(j * SCAN_ROWS, SCAN_ROWS), SCAN_ROWS)
            xr, xi = ur_ref[sl, :], ui_ref[sl, :]
            for (ar, ai), k in zip(pw, (1, 2, 4)):
                if reverse:
                    keep = rows < SCAN_ROWS - k
                    sr = jnp.where(keep, pltpu.roll(xr, SCAN_ROWS - k, 0), 0.0)
                    si = jnp.where(keep, pltpu.roll(xi, SCAN_ROWS - k, 0), 0.0)
                else:
                    keep = rows >= k
                    sr = jnp.where(keep, pltpu.roll(xr, k, 0), 0.0)
                    si = jnp.where(keep, pltpu.roll(xi, k, 0), 0.0)
                xr, xi = xr + (ar * sr - ai * si), xi + (ar * si + ai * sr)
            xr, xi = xr + (prv * cr - piv * ci), xi + (prv * ci + piv * cr)
            xr_ref[sl, :] = xr
            xi_ref[sl, :] = xi
            if with_states:
                last = rows == SCAN_ROWS - 1
                nr = jnp.where(last, cr, pltpu.roll(xr, SCAN_ROWS - 1, 0))
                ni = jnp.where(last, ci, pltpu.roll(xi, SCAN_ROWS - 1, 0))
                sr, si = sr_ref[sl, :], si_ref[sl, :]
                acr, aci = acr + (sr * nr + si * ni), aci + (sr * ni - si * nr)
            e = 0 if reverse else SCAN_ROWS - 1
            return jnp.broadcast_to(xr[e:e + 1], shape), jnp.broadcast_to(xi[e:e + 1], shape), acr, aci

        def tiles(g, carry):
            for u in range(SCAN_UNROLL):
                carry = tile(g * SCAN_UNROLL + u, carry)
            return carry

        assert ntile % SCAN_UNROLL == 0
        cr, ci, acr, aci = lax.fori_loop(0, ntile // SCAN_UNROLL, tiles,
                                         (cr_sc[...], ci_sc[...], ar_sc[...], ai_sc[...]))
        cr_sc[...] = cr
        ci_sc[...] = ci
        if with_states:
            ar_sc[...] = acr
            ai_sc[...] = aci

            @pl.when(pl.program_id(1) == nt - 1)
            def _():
                dr_ref[...] = acr
                di_ref[...] = aci

    tmap = (lambda l, t: (nt - 1 - t, l)) if reverse else (lambda l, t: (t, l))
    blk = pl.BlockSpec((tb, SCAN_LANES), tmap)
    tab = pl.BlockSpec((SCAN_ROWS, SCAN_LANES), lambda l, t: (0, l))
    full = jax.ShapeDtypeStruct((s, n), F32)
    return _pcall(
        body,
        name=name,
        grid=(nl, nt),
        in_specs=[blk, blk, tab, tab] + ([blk, blk] if with_states else []),
        out_specs=[blk, blk] + ([tab, tab] if with_states else []),
        out_shape=[full, full] + ([jax.ShapeDtypeStruct((SCAN_ROWS, n), F32)] * 2 if with_states else []),
        scratch_shapes=[pltpu.VMEM(shape, F32)] * 4,
        compiler_params=_params(("parallel", "arbitrary")),
    )(ur, ui, pr, pi, *(states if with_states else ()))


@functools.partial(jax.custom_vjp, nondiff_argnums=(4,))
def s5_scan(ur, ui, lr, li, name):
    return tuple(_scan_call(ur, ui, lr, li, False, name + "_f"))


def _s5_scan_fwd(ur, ui, lr, li, name):
    xr, xi = _scan_call(ur, ui, lr, li, False, name + "_f")
    return (xr, xi), (xr, xi, lr, li)


def _s5_scan_bwd(name, res, g):
    xr, xi, lr, li = res
    gr, gi, dlr, dli = _scan_call(g[0], g[1], lr, -li, True, name + "_b", states=(xr, xi))
    return gr, gi, jnp.sum(dlr, axis=0, keepdims=True), jnp.sum(dli, axis=0, keepdims=True)


s5_scan.defvjp(_s5_scan_fwd, _s5_scan_bwd)


def _running_sum(x, reverse):
    n = x.shape[0]
    rows = lax.broadcasted_iota(jnp.int32, x.shape, 0)
    k = 1
    while k < n:
        if reverse:
            x = x + jnp.where(rows < n - k, pltpu.roll(x, n - k, 0), 0.0)
        else:
            x = x + jnp.where(rows >= k, pltpu.roll(x, k, 0), 0.0)
        k *= 2
    return x


@jax.custom_vjp
def cumsum_rows(x):
    return _running_sum(x, False)


cumsum_rows.defvjp(lambda x: (_running_sum(x, False), None), lambda _, g: (_running_sum(g, True),))


def _hg_chunk(HG_SUB, q, k, v, lf, st):
    c = HG_CHUNK
    b = cumsum_rows(lf)
    qb = q * jnp.exp(b)
    o = lax.dot_general(qb.astype(BF16), st.astype(BF16), (((1,), (1,)), ((), ())), preferred_element_type=F32)
    nsub = c // HG_SUB
    srow = lax.broadcasted_iota(jnp.int32, (HG_SUB, HG_SUB, 1), 0)
    scol = lax.broadcasted_iota(jnp.int32, (HG_SUB, HG_SUB, 1), 1)
    smask = scol <= srow
    outs = []
    for i in range(nsub):
        lo = i * HG_SUB
        bi, qi, ki = b[lo:lo + HG_SUB], q[lo:lo + HG_SUB], k[lo:lo + HG_SUB]
        diff = jnp.where(smask, bi[:, None, :] - bi[None, :, :], 0.0)
        e = jnp.where(smask, jnp.exp(diff), 0.0)
        a_ii = jnp.sum(qi[:, None, :] * ki[None, :, :] * e, axis=-1)
        oi = jnp.dot(a_ii.astype(BF16), v[lo:lo + HG_SUB].astype(BF16), preferred_element_type=F32)
        if i > 0:
            r = b[lo - 1:lo]
            qt = qi * jnp.exp(bi - r)
            kt = k[:lo] * jnp.exp(r - b[:lo])
            a_ij = lax.dot_general(qt.astype(BF16), kt.astype(BF16), (((1,), (1,)), ((), ())),
                                   preferred_element_type=F32)
            oi = oi + jnp.dot(a_ij.astype(BF16), v[:lo].astype(BF16), preferred_element_type=F32)
        outs.append(oi)
    o = o + jnp.concatenate(outs, axis=0)
    bl = b[c - 1:c]
    kd = k * jnp.exp(bl - b)
    st_new = st * jnp.exp(bl) + lax.dot_general(v.astype(BF16), kd.astype(BF16), (((0,), (0,)), ((), ())),
                                                preferred_element_type=F32)
    return o, st_new


def _hg_fwd_call(q, k, v, lf, name):
    s, w = q.shape
    nc = s // HG_CHUNK
    blk = pl.BlockSpec((HG_CHUNK, w), lambda i: (i, 0))

    def body(q_ref, k_ref, v_ref, lf_ref, o_ref, st_ref, st_sc):
        @pl.when(pl.program_id(0) == 0)
        def _():
            st_sc[...] = jnp.zeros(st_sc.shape, F32)

        for hh in range(HG_HEADS):
            cs = slice(hh * HG_DK, (hh + 1) * HG_DK)
            st = st_sc[hh]
            st_ref[0, hh] = st
            o, st_new = _hg_chunk(HG_SUB_FWD, q_ref[:, cs], k_ref[:, cs], v_ref[:, cs], lf_ref[:, cs], st)
            o_ref[:, cs] = o
            st_sc[hh] = st_new

    return _pcall(
        body,
        name=name,
        grid=(nc,),
        in_specs=[blk, blk, blk, blk],
        out_specs=[blk, pl.BlockSpec((1, HG_HEADS, HG_DV, HG_DK), lambda i: (i, 0, 0, 0))],
        out_shape=[jax.ShapeDtypeStruct((s, w), F32), jax.ShapeDtypeStruct((nc, HG_HEADS, HG_DV, HG_DK), F32)],
        scratch_shapes=[pltpu.VMEM((HG_HEADS, HG_DV, HG_DK), F32)],
        compiler_params=_params(("arbitrary",)),
    )(q, k, v, lf)


def _hg_bwd_call(q, k, v, lf, sts, do, name):
    s, w = q.shape
    nc = s // HG_CHUNK
    blk = pl.BlockSpec((HG_CHUNK, w), lambda i: (nc - 1 - i, 0))

    def body(q_ref, k_ref, v_ref, lf_ref, st_ref, do_ref, dq_ref, dk_ref, dv_ref, dlf_ref, dst_sc):
        @pl.when(pl.program_id(0) == 0)
        def _():
            dst_sc[...] = jnp.zeros(dst_sc.shape, F32)

        def head(hh, carry):
            cs = pl.ds(pl.multiple_of(hh * HG_DK, HG_DK), HG_DK)
            _, vjp = jax.vjp(functools.partial(_hg_chunk, HG_SUB_BWD), q_ref[:, cs], k_ref[:, cs], v_ref[:, cs],
                             lf_ref[:, cs], st_ref[0, hh])
            dq, dk, dv, dlf, dst = vjp((do_ref[:, cs], dst_sc[hh]))
            dq_ref[:, cs] = dq
            dk_ref[:, cs] = dk
            dv_ref[:, cs] = dv
            dlf_ref[:, cs] = dlf
            dst_sc[hh] = dst
            return carry

        lax.fori_loop(0, HG_HEADS, head, 0)

    out = jax.ShapeDtypeStruct((s, w), F32)
    return _pcall(
        body,
        name=name,
        grid=(nc,),
        in_specs=[blk, blk, blk, blk, pl.BlockSpec((1, HG_HEADS, HG_DV, HG_DK), lambda i: (nc - 1 - i, 0, 0, 0)), blk],
        out_specs=[blk, blk, blk, blk],
        out_shape=[out, out, out, out],
        scratch_shapes=[pltpu.VMEM((HG_HEADS, HG_DV, HG_DK), F32)],
        compiler_params=_params(("arbitrary",)),
    )(q, k, v, lf, sts, do)


@functools.partial(jax.custom_vjp, nondiff_argnums=(4,))
def hgrn_core(q, k, v, lf, name):
    return _hg_fwd_call(q, k, v, lf, name + "_f")[0]


def _hgrn_core_fwd(q, k, v, lf, name):
    o, sts = _hg_fwd_call(q, k, v, lf, name + "_f")
    return o, (q, k, v, lf, sts)


def _hgrn_core_bwd(name, res, do):
    q, k, v, lf, sts = res
    return tuple(_hg_bwd_call(q, k, v, lf, sts, do, name + "_b"))


hgrn_core.defvjp(_hgrn_core_fwd, _hgrn_core_bwd)


def _adamw_call(w, g, m, v, name):
    rows, cols = w.shape
    tr = _tile(rows, 256, 8)
    c1 = 1.0 / (1.0 - ADAM_B1 ** ADAM_STEP)
    c2 = 1.0 / (1.0 - ADAM_B2 ** ADAM_STEP)

    def body(w_ref, g_ref, m_ref, v_ref, d_ref, mo_ref, vo_ref):
        gv = g_ref[...]
        mn = ADAM_B1 * m_ref[...] + (1.0 - ADAM_B1) * gv
        vn = ADAM_B2 * v_ref[...] + (1.0 - ADAM_B2) * (gv * gv)
        d_ref[...] = -ADAM_LR * ((mn * c1) / (jnp.sqrt(vn * c2) + ADAM_EPS) + ADAM_WD * w_ref[...])
        mo_ref[...] = mn
        vo_ref[...] = vn

    blk = pl.BlockSpec((tr, cols), lambda i: (i, 0))
    out = jax.ShapeDtypeStruct((rows, cols), F32)
    return _pcall(
        body,
        name=name,
        grid=(rows // tr,),
        in_specs=[blk, blk, blk, blk],
        out_specs=[blk, blk, blk],
        out_shape=[out, out, out],
        compiler_params=_params(("parallel",)),
    )(w, g, m, v)


ANY = pl.BlockSpec(memory_space=pl.ANY)


def _place():
    return lax.axis_index("x"), lax.axis_index("y"), lax.axis_index("c")


def _other_chips(x, y):
    return [(1 - x, y), (x, 1 - y), (1 - x, 1 - y)]


def _gather_weights(shards):
    nw = len(shards)

    def body(*refs):
        w_refs, g_refs, (send_sems, recv_sems) = refs[:nw], refs[nw:2 * nw], refs[2 * nw:]
        x, y, c = _place()
        myq = 2 * x + y
        sibling = (x, y, 1 - c)
        chips = _other_chips(x, y)

        def copy(k, src, dst, to):
            return pltpu.make_async_remote_copy(src_ref=src, dst_ref=dst, send_sem=send_sems.at[k],
                                                recv_sem=recv_sems.at[k], device_id=to, device_id_type=MESH)

        first = []
        for i in range(nw):
            for j, (px, py) in enumerate(chips):
                first.append(copy(6 * i + j, w_refs[i].at[c], g_refs[i].at[myq, c], (px, py, c)))
                first[-1].start()
        passed = []
        for i in range(nw):
            for j, (px, py) in enumerate(chips):
                q = 2 * px + py
                copy(6 * i + j, w_refs[i].at[c], g_refs[i].at[q, c], (px, py, c)).wait_recv()
                passed.append(copy(6 * i + 3 + j, g_refs[i].at[q, c], g_refs[i].at[q, c], sibling))
                passed[-1].start()
        for i in range(nw):
            for j, (px, py) in enumerate(chips):
                q = 2 * px + py
                copy(6 * i + 3 + j, g_refs[i].at[q, 1 - c], g_refs[i].at[q, 1 - c], sibling).wait_recv()
        for cp in first + passed:
            cp.wait_send()

    got = _pcall(
        body,
        name="gather_weights",
        in_specs=[ANY] * nw,
        out_specs=[ANY] * nw,
        out_shape=[jax.ShapeDtypeStruct((N_CHIPS,) + s.shape, s.dtype) for s in shards],
        scratch_shapes=[pltpu.SemaphoreType.DMA((6 * nw,)), pltpu.SemaphoreType.DMA((6 * nw,))],
    )(*shards)
    my_chip = 2 * lax.axis_index("x") + lax.axis_index("y")
    return [lax.dynamic_update_slice(g, s[None], (my_chip, 0, 0, 0)) for g, s in zip(got, shards)]


HBM = pl.BlockSpec(memory_space=pltpu.HBM)
SEM = pl.BlockSpec(memory_space=pltpu.SEMAPHORE)


def _gather_start(shards):
    nw = len(shards)

    def body(*refs):
        w_refs, land_refs = refs[:nw], refs[nw:2 * nw]
        send_sems, recv_sems = refs[2 * nw:2 * nw + 2]
        token = refs[-1]
        x, y, c = _place()
        myq = 2 * x + y
        for i in range(nw):
            hh = shards[i].shape[0] // 2
            rows = pl.ds(c * hh, hh)
            for j, (px, py) in enumerate(_other_chips(x, y)):
                for tc in range(2):
                    pltpu.make_async_remote_copy(
                        src_ref=w_refs[i].at[rows], dst_ref=land_refs[i].at[myq, rows],
                        send_sem=send_sems.at[6 * i + 2 * j + tc], recv_sem=recv_sems.at[6 * i + 2 * j + c],
                        device_id=(px, py, tc), device_id_type=MESH).start()
        token[...] = jnp.zeros_like(token)

    lands = [lax.empty((N_CHIPS,) + s.shape, s.dtype) for s in shards]
    out = _pcall(
        body,
        name="gather_start",
        in_specs=[HBM] * (2 * nw),
        out_specs=[SEM, SEM] + [HBM] * (2 * nw) + [pl.BlockSpec(memory_space=pltpu.VMEM)],
        out_shape=[pltpu.SemaphoreType.DMA((6 * nw,)), pltpu.SemaphoreType.DMA((6 * nw,))]
        + [pltpu.HBM(s.shape, s.dtype) for s in shards] + [pltpu.HBM(t.shape, t.dtype) for t in lands]
        + [jax.ShapeDtypeStruct((8, 128), F32)],
        input_output_aliases={i: 2 + i for i in range(2 * nw)},
        compiler_params=pltpu.CompilerParams(has_side_effects=pltpu.SideEffectType.DATAFLOW_SIDE_EFFECTING),
    )(*[pltpu.with_memory_space_constraint(t, pltpu.HBM) for t in list(shards) + lands])
    return out[0], out[1], out[2:2 + nw], out[2 + nw:2 + 2 * nw], out[-1]


def _gather_wait(send_sems, recv_sems, shards, lands, after):
    nw = len(shards)

    def body(*refs):
        w_refs, land_refs = refs[:nw], refs[nw:2 * nw]
        send_sems, recv_sems = refs[2 * nw:2 * nw + 2]
        x, y, c = _place()
        myq = 2 * x + y
        for i in range(nw):
            hh = shards[i].shape[0] // 2
            for j, (px, py) in enumerate(_other_chips(x, y)):
                q = 2 * px + py
                for other in range(2):
                    mine = pl.ds(c * hh, hh)
                    theirs = pl.ds(other * hh, hh)
                    pltpu.make_async_remote_copy(
                        src_ref=w_refs[i].at[mine], dst_ref=land_refs[i].at[myq, mine],
                        send_sem=send_sems.at[6 * i + 2 * j + other], recv_sem=recv_sems.at[6 * i + 2 * j + other],
                        device_id=(px, py, other), device_id_type=MESH).wait_send()
                    pltpu.make_async_remote_copy(
                        src_ref=w_refs[i].at[theirs], dst_ref=land_refs[i].at[q, theirs],
                        send_sem=send_sems.at[6 * i + 2 * j + other], recv_sem=recv_sems.at[6 * i + 2 * j + other],
                        device_id=(px, py, other), device_id_type=MESH).wait_recv()

    out = _pcall(
        body,
        name="gather_wait",
        in_specs=[HBM] * (2 * nw) + [SEM, SEM, pl.BlockSpec(memory_space=pl.ANY)],
        out_specs=[HBM] * (2 * nw),
        out_shape=[pltpu.HBM(s.shape, s.dtype) for s in shards] + [pltpu.HBM(t.shape, t.dtype) for t in lands],
        input_output_aliases={i: i for i in range(2 * nw)},
        compiler_params=pltpu.CompilerParams(has_side_effects=pltpu.SideEffectType.DATAFLOW_SIDE_EFFECTING),
    )(*shards, *lands, send_sems, recv_sems, after)
    my_chip = 2 * lax.axis_index("x") + lax.axis_index("y")
    return [lax.dynamic_update_slice(g, s[None], (my_chip, 0, 0)) for g, s in zip(out[nw:], out[:nw])]


def _swap_halves_to_sibling(grs):
    nw = len(grs)

    def body(*refs):
        g_refs, t_refs, (send_sems, recv_sems) = refs[:nw], refs[nw:2 * nw], refs[2 * nw:]
        x, y, c = _place()
        cps = []
        for i in range(nw):
            hh = grs[i].shape[1] // 2
            cps.append(pltpu.make_async_remote_copy(
                src_ref=g_refs[i].at[:, pl.ds((1 - c) * hh, hh)], dst_ref=t_refs[i], send_sem=send_sems.at[i],
                recv_sem=recv_sems.at[i], device_id=(x, y, 1 - c), device_id_type=MESH))
            cps[-1].start()
        for cp in cps:
            cp.wait()

    return _pcall(
        body,
        name="grad_d2d",
        in_specs=[ANY] * nw,
        out_specs=[ANY] * nw,
        out_shape=[jax.ShapeDtypeStruct((g.shape[0], g.shape[1] // 2, g.shape[2]), g.dtype) for g in grs],
        scratch_shapes=[pltpu.SemaphoreType.DMA((nw,)), pltpu.SemaphoreType.DMA((nw,))],
    )(*grs)


def _add_half(gr, t1, place, name):
    nq, r, wd = gr.shape
    hh = r // 2
    tr = _tile(hh, 512, 16)
    nb = hh // tr

    def body(p_ref, g_ref, t_ref, o_ref):
        o_ref[...] = (g_ref[...] + t_ref[...]).astype(BF16)

    return _pcall(
        body,
        name=name,
        grid_spec=pltpu.PrefetchScalarGridSpec(
            num_scalar_prefetch=1,
            grid=(nq, nb),
            in_specs=[
                pl.BlockSpec((1, tr, wd), lambda q, i, p_ref: (q, p_ref[0] * nb + i, 0)),
                pl.BlockSpec((1, tr, wd), lambda q, i, p_ref: (q, i, 0)),
            ],
            out_specs=pl.BlockSpec((1, tr, wd), lambda q, i, p_ref: (q, i, 0)),
        ),
        out_shape=jax.ShapeDtypeStruct((nq, hh, wd), BF16),
        compiler_params=_params(("parallel", "parallel")),
    )(place, gr, t1)


def _scatter_partials(ps):
    nw = len(ps)

    def body(*refs):
        p_refs, t_refs, (send_sems, recv_sems) = refs[:nw], refs[nw:2 * nw], refs[2 * nw:]
        x, y, c = _place()
        cps = []
        for i in range(nw):
            for j, (px, py) in enumerate(_other_chips(x, y)):
                cps.append(pltpu.make_async_remote_copy(
                    src_ref=p_refs[i].at[2 * px + py], dst_ref=t_refs[i].at[j], send_sem=send_sems.at[3 * i + j],
                    recv_sem=recv_sems.at[3 * i + j], device_id=(px, py, c), device_id_type=MESH))
                cps[-1].start()
        for cp in cps:
            cp.wait()

    return _pcall(
        body,
        name="grad_ici",
        in_specs=[ANY] * nw,
        out_specs=[ANY] * nw,
        out_shape=[jax.ShapeDtypeStruct((3,) + p.shape[1:], p.dtype) for p in ps],
        scratch_shapes=[pltpu.SemaphoreType.DMA((3 * nw,)), pltpu.SemaphoreType.DMA((3 * nw,))],
    )(*ps)


def _add_partials(gr, t1, t2, place, name):
    nq, r, wd = gr.shape
    hh = r // 2
    tr = _tile(hh, 512, 16)
    nb = hh // tr

    def body(p_ref, g_ref, s_ref, t_ref, o_ref):
        o_ref[...] = (((g_ref[0] + s_ref[0]) + t_ref[0].astype(F32)) + t_ref[1].astype(F32)) + t_ref[2].astype(F32)

    return _pcall(
        body,
        name=name,
        grid_spec=pltpu.PrefetchScalarGridSpec(
            num_scalar_prefetch=1,
            grid=(nb,),
            in_specs=[
                pl.BlockSpec((1, tr, wd), lambda i, p_ref: (p_ref[1], p_ref[0] * nb + i, 0)),
                pl.BlockSpec((1, tr, wd), lambda i, p_ref: (p_ref[1], i, 0)),
                pl.BlockSpec((3, tr, wd), lambda i, p_ref: (0, i, 0)),
            ],
            out_specs=pl.BlockSpec((tr, wd), lambda i, p_ref: (i, 0)),
        ),
        out_shape=jax.ShapeDtypeStruct((hh, wd), F32),
        compiler_params=_params(("parallel",)),
    )(place, gr, t1, t2)


def _join_halves(halves):
    nw = len(halves)

    def body(*refs):
        h_refs, f_refs, (send_sems, recv_sems) = refs[:nw], refs[nw:2 * nw], refs[2 * nw:]
        x, y, c = _place()

        def copy(i, rows):
            hh = halves[i].shape[0]
            return pltpu.make_async_remote_copy(src_ref=h_refs[i], dst_ref=f_refs[i].at[pl.ds(rows * hh, hh)],
                                                send_sem=send_sems.at[i], recv_sem=recv_sems.at[i],
                                                device_id=(x, y, 1 - c), device_id_type=MESH)

        for i in range(nw):
            copy(i, c).start()
        for i in range(nw):
            copy(i, c).wait_send()
            copy(i, 1 - c).wait_recv()

    got = _pcall(
        body,
        name="grad_join",
        in_specs=[ANY] * nw,
        out_specs=[ANY] * nw,
        out_shape=[jax.ShapeDtypeStruct((2 * h.shape[0], h.shape[1]), h.dtype) for h in halves],
        scratch_shapes=[pltpu.SemaphoreType.DMA((nw,)), pltpu.SemaphoreType.DMA((nw,))],
    )(*halves)
    c = lax.axis_index("c")
    return [lax.dynamic_update_slice(g, h, (c * h.shape[0], 0)) for g, h in zip(got, halves)]


N_DEV = 8


def _allreduce_small(buf):
    rows, wd = buf.shape
    n = rows // N_DEV

    def body(x_ref, o_ref, stage_ref, send_sems, recv_sems):
        x, y, c = _place()
        me = 4 * x + 2 * y + c

        def peer(k):
            return (x ^ (k >> 2), y ^ ((k >> 1) & 1), c ^ (k & 1))

        def chunk(ref, d):
            return ref.at[pl.ds(pl.multiple_of(d * n, 8), n)]

        first = []
        for k in range(1, N_DEV):
            px, py, pc = peer(k)
            cp = pltpu.make_async_remote_copy(src_ref=chunk(x_ref, 4 * px + 2 * py + pc), dst_ref=stage_ref.at[k],
                                              send_sem=send_sems.at[k - 1], recv_sem=recv_sems.at[k - 1],
                                              device_id=(px, py, pc), device_id_type=MESH)
            cp.start()
            first.append(cp)
        acc = chunk(x_ref, me)[...]
        for k in range(1, N_DEV):
            first[k - 1].wait_recv()
            acc = acc + stage_ref[k]
        chunk(o_ref, me)[...] = acc
        second = []
        for k in range(1, N_DEV):
            cp = pltpu.make_async_remote_copy(src_ref=chunk(o_ref, me), dst_ref=chunk(o_ref, me),
                                              send_sem=send_sems.at[6 + k], recv_sem=recv_sems.at[6 + k],
                                              device_id=peer(k), device_id_type=MESH)
            cp.start()
            second.append(cp)
        for k in range(1, N_DEV):
            px, py, pc = peer(k)
            them = 4 * px + 2 * py + pc
            pltpu.make_async_remote_copy(src_ref=chunk(o_ref, them), dst_ref=chunk(o_ref, them),
                                         send_sem=send_sems.at[6 + k], recv_sem=recv_sems.at[6 + k],
                                         device_id=(px, py, pc), device_id_type=MESH).wait_recv()
        for cp in first + second:
            cp.wait_send()

    vmem = pl.BlockSpec(memory_space=pltpu.VMEM)
    return _pcall(
        body,
        name="allreduce_small",
        in_specs=[vmem],
        out_specs=vmem,
        out_shape=jax.ShapeDtypeStruct((rows, wd), F32),
        scratch_shapes=[pltpu.VMEM((N_DEV, n, wd), F32), pltpu.SemaphoreType.DMA((14,)),
                        pltpu.SemaphoreType.DMA((14,))],
    )(buf)


@functools.partial(jax.custom_vjp, nondiff_argnums=(1,))
def split_cols(z, spec):
    outs = []
    for segs in spec[0]:
        parts = [z[:, s:s + n] for s, n in segs]
        outs.append(parts[0] if len(parts) == 1 else jnp.concatenate(parts, axis=1))
    return tuple(outs)


def _split_cols_fwd(z, spec):
    return split_cols(z, spec), None


def _split_cols_bwd(spec, _, gs):
    pieces, width = spec
    segs = []
    for p, piece in enumerate(pieces):
        off = 0
        for s, n in piece:
            segs.append((s, n, p, off))
            off += n
    rows, parts, pos = gs[0].shape[0], [], 0
    for s, n, p, off in sorted(segs):
        if s > pos:
            parts.append(jnp.zeros((rows, s - pos), gs[0].dtype))
        parts.append(gs[p] if (off == 0 and n == gs[p].shape[1]) else gs[p][:, off:off + n])
        pos = s + n
    if pos < width:
        parts.append(jnp.zeros((rows, width - pos), gs[0].dtype))
    return (jnp.concatenate(parts, axis=1),)


split_cols.defvjp(_split_cols_fwd, _split_cols_bwd)


def _contiguous(sizes):
    pieces, off = [], 0
    for n in sizes:
        pieces.append(((off, n),))
        off += n
    return tuple(pieces), off


def _rope(t, cos, sin):
    half = t.shape[-1] // 2
    t1, t2 = t[..., :half], t[..., half:]
    return jnp.concatenate([t1 * cos - t2 * sin, t2 * cos + t1 * sin], axis=-1)


def _s5_mixer(u, sp, l, w16, toks):
    g, p_, h = SSM_GROUPS, SSM_STATE, SSM_GROUP_CH
    lam_re, lam_im = sp["ssm_lam_re", l], sp["ssm_lam_im", l]
    step = jnp.exp(sp["ssm_log_step", l])[:, None]
    mag, ang = jnp.exp(lam_re * step), lam_im * step
    lbr, lbi = mag * jnp.cos(ang), mag * jnp.sin(ang)
    den = lam_re * lam_re + lam_im * lam_im
    nr, ni = lbr - 1.0, lbi
    qr, qi = (nr * lam_re + ni * lam_im) / den, (ni * lam_re - nr * lam_im) / den
    b_re, b_im = sp["ssm_b_re", l], sp["ssm_b_im", l]
    bbr = qr[..., None] * b_re - qi[..., None] * b_im
    bbi = qr[..., None] * b_im + qi[..., None] * b_re
    nb = g // SSM_PACK
    eye = jnp.eye(SSM_PACK, dtype=F32)

    def in_blocks(t):
        return jnp.einsum("jgph,gk->jghkp", t.reshape(nb, SSM_PACK, p_, h), eye).reshape(nb, SSM_PACK * h, SSM_PACK * p_)

    def out_blocks(t):
        return jnp.einsum("jghp,gk->jgpkh", t.reshape(nb, SSM_PACK, h, p_), eye).reshape(nb, SSM_PACK * p_, SSM_PACK * h)

    nm = "ssm%d" % l
    ur = bd_matmul(u, in_blocks(bbr), nm + "_bur")
    ui = bd_matmul(u, in_blocks(bbi), nm + "_bui")
    xr, xi = s5_scan(ur, ui, lbr.reshape(1, g * p_), lbi.reshape(1, g * p_), nm + "_scan")
    y = (bd_matmul(xr, out_blocks(sp["ssm_c_re", l]), nm + "_cr") + bd_matmul(xi, out_blocks(-sp["ssm_c_im", l]), nm + "_ci")
         + sp["ssm_d", l].reshape(1, g * h) * u)
    y = jax.nn.gelu(y)
    zz = matmul_w(y, w16["ssm_w_glu", l], toks["ssm_w_glu", l], nm + "_glu")
    z_out, z_gate = split_cols(zz, _contiguous((D_MODEL, D_MODEL)))
    return z_out * jax.nn.sigmoid(z_gate)


def _mla_mixer(q_lat, kv_lat, k_rope, cos, sin, sp, l, w16, toks):
    s = q_lat.shape[0]
    nm = "mla%d" % l
    dqk = MLA_NOPE + MLA_ROPE
    qt = matmul_w_t(rmsnorm(q_lat, sp["mla_q_norm", l], nm + "_qn"), w16["mla_w_uq", l], toks["mla_w_uq", l],
                    nm + "_uq").reshape(MLA_HEADS, dqk, s)
    half = MLA_ROPE // 2
    q1, q2 = qt[:, MLA_NOPE:MLA_NOPE + half], qt[:, MLA_NOPE + half:]
    ct, st = cos.T[None], sin.T[None]
    qt = jnp.concatenate([qt[:, :MLA_NOPE], q1 * ct - q2 * st, q2 * ct + q1 * st], axis=1).reshape(MLA_HEADS * dqk, s)
    kvt = matmul_w_t(rmsnorm(kv_lat, sp["mla_kv_norm", l], nm + "_kvn"), w16["mla_w_ukv", l], toks["mla_w_ukv", l],
                     nm + "_ukv").reshape(MLA_HEADS, MLA_NOPE + MLA_V, s)
    k_pe = _rope(k_rope, cos, sin)
    k = jnp.concatenate([kvt[:, :MLA_NOPE].transpose(0, 2, 1),
                         jnp.broadcast_to(k_pe[None], (MLA_HEADS, s, MLA_ROPE))], axis=-1)
    vt = kvt[:, MLA_NOPE:].reshape(MLA_HEADS * MLA_V, s)
    ot = attention(qt, k, vt, 1.0 / math.sqrt(dqk), True, nm + "_att")
    return matmul_w_at(ot, w16["mla_w_o", l], toks["mla_w_o", l], nm + "_o")


def _hgrn_mixer(q, f_logit, i_in, g, lb, sp, l, w16, toks):
    s = q.shape[0]
    nm = "hg%d" % l
    f = lb + (1.0 - lb) * jax.nn.sigmoid(f_logit)
    o = hgrn_core(jax.nn.silu(q), 1.0 - f, i_in, jnp.log(f), nm + "_core")
    o = rmsnorm(o.reshape(s * HG_HEADS, HG_DV), sp["hg_g_norm", l], nm + "_gn").reshape(s, HG_HEADS * HG_DV)
    return silu_gated_out(o, g, w16["hg_w_o", l], toks["hg_w_o", l], nm + "_o")


def _cross_attention(x, h, mem_n, l, w16, toks):
    s, m = h.shape[0], mem_n.shape[0]
    nm = "xat%d" % l
    xw = X_HEADS * X_HEAD_DIM
    qt = matmul_w_t(h, w16["x_w_q", l], toks["x_w_q", l], nm + "_q")
    kv = matmul_w(mem_n, w16["x_w_kv", l], toks["x_w_kv", l], nm + "_kv")
    k, v = split_cols(kv, _contiguous((xw, xw)))
    k = k.reshape(m, X_HEADS, X_HEAD_DIM).transpose(1, 0, 2)
    ot = attention(qt, k, v.T, 1.0 / math.sqrt(X_HEAD_DIM), False, nm + "_att")
    return matmul_w_at_res(x, ot, w16["x_w_o", l], toks["x_w_o", l], nm + "_o")


def _in_spec():
    hw = HG_HEADS * HG_DK
    sizes = (SSM_WIDTH, MLA_Q_RANK, MLA_KV_RANK, MLA_ROPE, hw, hw, hw, hw, D_MODEL, D_MODEL, D_MODEL)
    pieces, lo = [], 0
    for n in sizes:
        segs, c = [], lo
        while c < lo + n:
            end = min(lo + n, (c // D_IN_SHARD + 1) * D_IN_SHARD)
            segs.append((c + (D_IN_BLOCK - D_IN_SHARD) * (c // D_IN_SHARD), end - c))
            c = end
        pieces.append(tuple(segs))
        lo += n
    assert lo == D_IN
    return tuple(pieces), N_CHIPS * D_IN_BLOCK


def _layer(x, mem, cos, sin, lb, l, sp, w16, toks):
    s = x.shape[0]
    nm = "l%d" % l
    h = rmsnorm(x, sp["norm_mix", l], nm + "_nmix")
    z = matmul_w(h, w16["w_in", l], toks["w_in", l], nm + "_in")
    u, q_lat, kv_lat, k_rope, hg_q, hg_f, hg_i, hg_g, l_ssm, l_mla, l_hg = split_cols(z, _in_spec())
    y_ssm = _s5_mixer(u, sp, l, w16, toks)
    y_mla = _mla_mixer(q_lat, kv_lat, k_rope, cos, sin, sp, l, w16, toks)
    y_hg = _hgrn_mixer(hg_q, hg_f, hg_i, hg_g, lb, sp, l, w16, toks)
    x = gated_merge_out(x, (y_ssm, y_mla, y_hg), (l_ssm, l_mla, l_hg), w16["w_out", l], toks["w_out", l], nm + "_out")
    hc = rmsnorm(x, sp["norm_cross", l], nm + "_ncross")
    mem_n = rmsnorm(mem, sp["norm_mem", l], nm + "_nmem")
    x = _cross_attention(x, hc, mem_n, l, w16, toks)
    hf = rmsnorm(x, sp["norm_ffn", l], nm + "_nffn")
    gt = matmul_w(hf, w16["ffn_w_gate", l], toks["ffn_w_gate", l], nm + "_gate")
    up = matmul_w(hf, w16["ffn_w_up", l], toks["ffn_w_up", l], nm + "_up")
    return swiglu_down(x, gt, up, w16["ffn_w_down", l], toks["ffn_w_down", l], nm + "_down")


def _lower_bounds(hg_lb):
    lb_p = jax.nn.softmax(hg_lb, axis=0)
    return jnp.cumsum(lb_p, axis=0) - lb_p[0:1]


def _loss_head(norm_final, x, target):
    y = rmsnorm(x, norm_final, "nfinal")
    return 0.5 * jnp.sum(jnp.mean(jnp.square(y - target), axis=-1))


def _forward_backward(sp, toks, x, w16, later, mem, pos, target):
    half = MLA_ROPE // 2
    inv_freq = ROPE_THETA ** (-jnp.arange(half, dtype=F32) / half)
    ang = pos.astype(F32)[:, None] * inv_freq
    cos, sin = jnp.cos(ang), jnp.sin(ang)
    lower, lower_vjp = jax.vjp(_lower_bounds, sp["hg_lb", ALL])
    token, wait_for_weights = later
    x = x + token[0, 0]
    vjps = []
    for l in range(DEPTH):
        if l == 1:
            w16 = wait_for_weights(w16, x)
        sp_l = {k: v for k, v in sp.items() if k[1] == l}
        toks_l = {k: v for k, v in toks.items() if k[1] == l}
        weights = w16
        x, layer_vjp = jax.vjp(lambda s, t, lb, xx, l=l, weights=weights: _layer(xx, mem, cos, sin, lb, l, s, weights, t),
                               sp_l, toks_l, lower[l], x)
        vjps.append(layer_vjp)
    loss, head_vjp = jax.vjp(lambda g, xx: _loss_head(g, xx, target), sp["norm_final", ALL], x)
    g_sp, g_tok, d_lower = {}, {}, [None] * DEPTH
    g_sp["norm_final", ALL], dx = head_vjp(jnp.ones((), F32))
    for l in reversed(range(DEPTH)):
        g_s, g_t, d_lower[l], dx = vjps[l](dx)
        g_sp.update(g_s)
        g_tok.update(g_t)
    (g_sp["hg_lb", ALL],) = lower_vjp(jnp.stack(d_lower))
    return loss, g_sp, g_tok, dx


WEIGHTS = ["norm_mix", "w_in", "ssm_lam_re", "ssm_lam_im", "ssm_b_re", "ssm_b_im", "ssm_c_re", "ssm_c_im", "ssm_d",
           "ssm_log_step", "ssm_w_glu", "mla_q_norm", "mla_kv_norm", "mla_w_uq", "mla_w_ukv", "mla_w_o", "hg_lb",
           "hg_g_norm", "hg_w_o", "w_out", "norm_cross", "norm_mem", "x_w_q", "x_w_kv", "x_w_o", "norm_ffn",
           "ffn_w_gate_up", "ffn_w_down", "norm_final"]
INPUTS = ["x", "mem", "positions"] + WEIGHTS + ["loss_target"] + ["m_" + n for n in WEIGHTS] + ["v_" + n for n in WEIGHTS]
BIG = [("w_in", 2), ("ssm_w_glu", 2), ("mla_w_uq", 2), ("mla_w_ukv", 2), ("mla_w_o", 2), ("hg_w_o", 2), ("w_out", 1),
       ("x_w_q", 1), ("x_w_kv", 1), ("x_w_o", 2), ("ffn_w_gate_up", 2), ("ffn_w_down", 1)]
SMALL = [n for n in WEIGHTS if n not in dict(BIG)]
WHOLE = ("hg_lb", "norm_final")
ALL = -1
SMALL_W = 128
BLOCKED = ("w_in", "ffn_w_gate_up")


def _pack(arrs, rows, width, dtype):
    flat = jnp.concatenate([a.astype(dtype).reshape(-1) for a in arrs])
    return jnp.pad(flat, (0, rows * width - flat.shape[0])).reshape(rows, width)


def _unpack(buf, like):
    flat = buf.reshape(-1)
    out, off = [], 0
    for a in like:
        n = math.prod(a.shape)
        out.append(flat[off:off + n].reshape(a.shape))
        off += n
    return out


def kernel(
    x, mem, positions, norm_mix, w_in, ssm_lam_re, ssm_lam_im, ssm_b_re, ssm_b_im, ssm_c_re, ssm_c_im, ssm_d,
    ssm_log_step, ssm_w_glu, mla_q_norm, mla_kv_norm, mla_w_uq, mla_w_ukv, mla_w_o, hg_lb, hg_g_norm, hg_w_o,
    w_out, norm_cross, norm_mem, x_w_q, x_w_kv, x_w_o, norm_ffn, ffn_w_gate_up, ffn_w_down, norm_final,
    loss_target, m_norm_mix, m_w_in, m_ssm_lam_re, m_ssm_lam_im, m_ssm_b_re, m_ssm_b_im, m_ssm_c_re, m_ssm_c_im,
    m_ssm_d, m_ssm_log_step, m_ssm_w_glu, m_mla_q_norm, m_mla_kv_norm, m_mla_w_uq, m_mla_w_ukv, m_mla_w_o,
    m_hg_lb, m_hg_g_norm, m_hg_w_o, m_w_out, m_norm_cross, m_norm_mem, m_x_w_q, m_x_w_kv, m_x_w_o, m_norm_ffn,
    m_ffn_w_gate_up, m_ffn_w_down, m_norm_final, v_norm_mix, v_w_in, v_ssm_lam_re, v_ssm_lam_im, v_ssm_b_re,
    v_ssm_b_im, v_ssm_c_re, v_ssm_c_im, v_ssm_d, v_ssm_log_step, v_ssm_w_glu, v_mla_q_norm, v_mla_kv_norm,
    v_mla_w_uq, v_mla_w_ukv, v_mla_w_o, v_hg_lb, v_hg_g_norm, v_hg_w_o, v_w_out, v_norm_cross, v_norm_mem,
    v_x_w_q, v_x_w_kv, v_x_w_o, v_norm_ffn, v_ffn_w_gate_up, v_ffn_w_down, v_norm_final):
    given = locals()
    a = {n: given[n] for n in INPUTS}
    x, mem, pos, target = a["x"][0], a["mem"][0], a["positions"][0], a["loss_target"][0]
    place = jnp.stack([lax.axis_index("c"), 2 * lax.axis_index("x") + lax.axis_index("y")]).astype(jnp.int32)

    def shard16(n):
        w = a[n].astype(BF16)
        if n == "w_in":
            w = jnp.pad(w, ((0, 0), (0, 0), (0, D_IN_BLOCK - D_IN_SHARD)))
        return w

    def use(blocks, l, w16):
        for (n, ax), got in zip(BIG, blocks):
            if n in BLOCKED:
                full = got
            elif ax == 2:
                full = jnp.concatenate([got[q] for q in range(N_CHIPS)], axis=1)
            else:
                full = got.reshape(N_CHIPS * got.shape[1], got.shape[2])
            if n == "ffn_w_gate_up":
                half = N_CHIPS // 2
                w16["ffn_w_gate", l], w16["ffn_w_up", l] = full[:half], full[half:]
            else:
                w16[n, l] = full
        return w16

    shards = [shard16(n) for n, _ in BIG]
    halves = [s[0].reshape(2, s.shape[1] // 2, s.shape[2]) for s in shards]
    w16 = use([g.reshape(N_CHIPS, s.shape[1], s.shape[2]) for g, s in zip(_gather_weights(halves), shards)], 0, {})
    pending = _gather_start([s[1] for s in shards])
    toks = {}
    for n, ax in BIG:
        rows, cols = a[n].shape[1], shards[BIG.index((n, ax))].shape[2]
        shape = (N_CHIPS, rows, cols) if n in BLOCKED else (rows, N_CHIPS * cols) if ax == 2 else (N_CHIPS * rows, cols)
        for l in range(DEPTH):
            if n == "ffn_w_gate_up":
                toks["ffn_w_gate", l] = jnp.zeros((N_CHIPS // 2,) + shape[1:], F32)
                toks["ffn_w_up", l] = jnp.zeros((N_CHIPS // 2,) + shape[1:], F32)
            else:
                toks[n, l] = jnp.zeros(shape, F32)

    def layer1_weights(w16, after):
        return use(_gather_wait(*pending[:4], after), 1, dict(w16))
    sp = {}
    for n in SMALL:
        if n in WHOLE:
            sp[n, ALL] = a[n]
        else:
            for l in range(DEPTH):
                sp[n, l] = a[n][l]

    loss, g_sp, g_tok, g_x = _forward_backward(sp, toks, x, w16, (pending[4], layer1_weights), mem, pos, target)
    g_small = {n: g_sp[n, ALL] if n in WHOLE else jnp.stack([g_sp[n, l] for l in range(DEPTH)]) for n in SMALL}

    grads, names = [], []
    for n, ax in BIG:
        for l in range(DEPTH):
            if n == "ffn_w_gate_up":
                g = jnp.concatenate([g_tok["ffn_w_gate", l], g_tok["ffn_w_up", l]], axis=0)
            else:
                g = g_tok[n, l]
            if n in BLOCKED:
                grads.append(g)
            elif ax == 2:
                grads.append(g.reshape(g.shape[0], N_CHIPS, g.shape[1] // N_CHIPS).transpose(1, 0, 2))
            else:
                grads.append(g.reshape(N_CHIPS, g.shape[0] // N_CHIPS, g.shape[1]))
            names.append("%s%d" % (n, l))
    from_sibling = _swap_halves_to_sibling(grads)
    chip_sums = [_add_half(g, t, place, "grad_add_half_" + n) for n, g, t in zip(names, grads, from_sibling)]
    from_chips = _scatter_partials(chip_sums)
    totals = [_add_partials(g, t1, t2, place, "grad_add_partials_" + n)
              for n, g, t1, t2 in zip(names, grads, from_sibling, from_chips)]
    joined = _join_halves(totals)
    g_big = {n: jnp.stack([joined[DEPTH * i + l][:, :a[n].shape[2]] for l in range(DEPTH)])
             for i, (n, _) in enumerate(BIG)}

    small_like = [a[n] for n in SMALL] + [jnp.zeros((1,), F32)]
    n_small = sum(math.prod(t.shape) for t in small_like)
    srows = -(-n_small // (SMALL_W * 8 * N_DEV)) * 8 * N_DEV
    reduced = _allreduce_small(_pack([g_small[n] for n in SMALL] + [loss.reshape(1)], srows, SMALL_W, F32))
    red = _unpack(reduced, small_like)
    g_sm = dict(zip(SMALL, red[:-1]))
    loss = red[-1].reshape(())

    out_g, out_d, out_m, out_v = {}, {}, {}, {}
    for n, _ in BIG:
        shp = a[n].shape
        two = lambda t: t.reshape(shp[0] * shp[1], shp[2])
        d, m, v = _adamw_call(two(a[n]), two(g_big[n]), two(a["m_" + n]), two(a["v_" + n]), "adamw_" + n)
        out_g[n], out_d[n], out_m[n], out_v[n] = g_big[n], d.reshape(shp), m.reshape(shp), v.reshape(shp)
    small_arrs = [a[n] for n in SMALL]
    d, m, v = _adamw_call(
        _pack(small_arrs, srows, SMALL_W, F32), reduced,
        _pack([a["m_" + n] for n in SMALL], srows, SMALL_W, F32),
        _pack([a["v_" + n] for n in SMALL], srows, SMALL_W, F32), "adamw_small")
    for n, gg, dd, mm_, vv in zip(SMALL, red[:-1], _unpack(d, small_arrs), _unpack(m, small_arrs), _unpack(v, small_arrs)):
        out_g[n], out_d[n], out_m[n], out_v[n] = gg, dd, mm_, vv

    return (loss, g_x[None], *[out_g[n] for n in WEIGHTS], *[out_d[n] for n in WEIGHTS],
            *[out_m[n] for n in WEIGHTS], *[out_v[n] for n in WEIGHTS])
```

```python
import functools
import math

import jax
import jax.numpy as jnp
from jax import lax
from jax.experimental import pallas as pl
from jax.experimental.pallas import tpu as pltpu

F32 = jnp.float32
BF16 = jnp.bfloat16
MESH = pl.DeviceIdType.MESH

D_MODEL = 1024
DEPTH = 2
RMS_EPS = 1e-6
SSM_GROUPS, SSM_GROUP_CH, SSM_STATE = 32, 16, 64
SSM_WIDTH = SSM_GROUPS * SSM_GROUP_CH
SSM_LANES = SSM_GROUPS * SSM_STATE
SSM_PACK = 8
MLA_HEADS, MLA_Q_RANK, MLA_KV_RANK, MLA_NOPE, MLA_ROPE, MLA_V = 8, 512, 256, 64, 32, 64
ROPE_THETA = 10000.0
HG_HEADS, HG_DK, HG_DV, HG_CHUNK = 4, 128, 128, 64
HG_SUB_FWD, HG_SUB_BWD = 32, 16
X_HEADS, X_HEAD_DIM = 4, 128
D_FF = 2816
D_IN = 6432
N_CHIPS = 4
D_IN_SHARD = D_IN // N_CHIPS
D_IN_BLOCK = 1664
MASK_VALUE = -1e30

ADAM_LR, ADAM_B1, ADAM_B2, ADAM_EPS, ADAM_WD, ADAM_STEP = 0.001, 0.9, 0.999, 1e-08, 0.01, 10

VMEM_LIMIT = 48 * 1024 * 1024


def _pcall(body, **kw):
    return pl.pallas_call(body, **kw)


def _params(sem):
    return pltpu.CompilerParams(dimension_semantics=sem, vmem_limit_bytes=VMEM_LIMIT)


def _tile(n, cap, align=128):
    if n <= cap:
        return n
    t = (cap // align) * align
    while t >= align:
        if n % t == 0:
            return t
        t -= align
    return n


MM_VMEM_BUDGET = 36 * 1024 * 1024
MM_TILE_CAP = 2048


def _divisors(n, cap, align=128):
    out = [n] if n <= cap else []
    out += [t for t in range(align, min(n, cap + 1), align) if n % t == 0]
    return sorted(set(out), reverse=True)


@functools.lru_cache(maxsize=None)
def _mm_tiles(m, n, k, a_bytes, b_bytes, o_bytes, n_unit, k_unit):
    best = None
    for tm in _divisors(m, MM_TILE_CAP):
        for tn in _divisors(n_unit, MM_TILE_CAP):
            for tk in _divisors(k_unit, MM_TILE_CAP):
                nk = k // tk
                need = 2 * (tm * tk * a_bytes + tk * tn * b_bytes + tm * tn * o_bytes)
                need += tm * tn * 4 * (2 if nk > 1 else 1)
                need += (tm * tk * 2 if a_bytes > 2 else 0) + (tk * tn * 2 if b_bytes > 2 else 0)
                if need > MM_VMEM_BUDGET:
                    continue
                key = ((m // tm) * (n // tn) * nk, nk, -tn)
                if best is None or key < best[0]:
                    best = (key, (tm, tn, tk))
    assert best is not None, (m, n, k)
    return best[1]


def _mm(a, b, *, ta=False, tb=False, out_dtype=F32, name="mm", b_blocks=False, out_blocks=0, pro=None, epi=None,
        save_left=False):
    if ta:
        K, M = a.shape
    else:
        M, K = a.shape
    n_unit = k_unit = None
    if b_blocks:
        nb, rows, cols = b.shape
        if tb:
            N, Kb, k_unit = rows, nb * cols, cols
        else:
            Kb, N, n_unit = rows, nb * cols, cols
    elif tb:
        N, Kb = b.shape
    else:
        Kb, N = b.shape
    if out_blocks:
        assert not b_blocks and N % out_blocks == 0
        n_unit = N // out_blocks
    assert K == Kb, (a.shape, b.shape, ta, tb)
    n_pro = len(pro) - 1 if pro else 0
    n_epi = len(epi) - 1 if epi else 0
    tm, tn, tk = _mm_tiles(M, N, K, a.dtype.itemsize * (1 + n_pro), b.dtype.itemsize,
                           jnp.dtype(out_dtype).itemsize * max(1, 2 * n_epi), n_unit or N, k_unit or K)
    nk = K // tk
    dims = (((0,) if ta else (1,), (1,) if tb else (0,)), ((), ()))
    n_in = 2 + n_pro + n_epi
    n_out = max(1, n_epi) + (1 if save_left else 0)
    assert not save_left or (pro and tn == N)

    def body(*refs):
        ins, outs, acc = refs[:n_in], refs[n_in:n_in + n_out], refs[n_in + n_out:]
        b_ref = ins[1 + n_pro]
        av = (pro[0](*[r[...] for r in ins[:1 + n_pro]]) if pro else ins[0][...]).astype(BF16)
        if save_left:
            outs[-1][...] = av
        bv = b_ref[...].astype(BF16)
        part = lax.dot_general(av, bv, dims, preferred_element_type=F32)

        def finish(total):
            if epi:
                for o_ref, r in zip(outs, epi[0](total, *[e[...] for e in ins[2 + n_pro:]])):
                    o_ref[...] = r.astype(o_ref.dtype)
            else:
                outs[0][...] = total.astype(outs[0].dtype)

        if nk == 1:
            finish(part)
        else:
            acc_ref = acc[0]
            k = pl.program_id(2)

            @pl.when(k == 0)
            def _():
                acc_ref[...] = part

            @pl.when(k > 0)
            def _():
                acc_ref[...] += part

            @pl.when(k == nk - 1)
            def _():
                finish(acc_ref[...])

    a_spec = pl.BlockSpec((tk, tm), lambda i, j, k: (k, i)) if ta else pl.BlockSpec((tm, tk), lambda i, j, k: (i, k))
    if b_blocks and tb:
        r = k_unit // tk
        b_spec = pl.BlockSpec((None, tn, tk), lambda i, j, k: (k // r, j, k % r))
    elif b_blocks:
        r = n_unit // tn
        b_spec = pl.BlockSpec((None, tk, tn), lambda i, j, k: (j // r, k, j % r))
    elif tb:
        b_spec = pl.BlockSpec((tn, tk), lambda i, j, k: (j, k))
    else:
        b_spec = pl.BlockSpec((tk, tn), lambda i, j, k: (k, j))
    if out_blocks:
        r = n_unit // tn
        out_spec = pl.BlockSpec((None, tm, tn), lambda i, j, k: (j // r, i, j % r))
        out_shape = jax.ShapeDtypeStruct((out_blocks, M, n_unit), out_dtype)
    else:
        out_spec = pl.BlockSpec((tm, tn), lambda i, j, k: (i, j))
        out_shape = jax.ShapeDtypeStruct((M, N), out_dtype)
    in_specs = [a_spec] * (1 + n_pro) + [b_spec] + [out_spec] * n_epi
    args = [a] + (list(pro[1:]) if pro else []) + [b] + (list(epi[1:]) if epi else [])
    out_specs, out_shapes = ([out_spec] * n_epi, [out_shape] * n_epi) if epi else (out_spec, out_shape)
    if save_left:
        out_specs = [out_spec] * max(1, n_epi) + [a_spec]
        out_shapes = [out_shape] * max(1, n_epi) + [jax.ShapeDtypeStruct(a.shape, BF16)]
    return _pcall(
        body,
        name=name,
        grid=(M // tm, N // tn, nk),
        in_specs=in_specs,
        out_specs=out_specs,
        out_shape=out_shapes,
        scratch_shapes=[pltpu.VMEM((tm, tn), F32)] if nk > 1 else [],
        compiler_params=_params(("parallel", "parallel", "arbitrary")),
    )(*args)


def _bd_call(a, bc, transpose, name):
    m = a.shape[0]
    nb, ka, kb = bc.shape
    win, wout = (kb, ka) if transpose else (ka, kb)
    assert a.shape[1] == nb * win
    tm = _tile(m, 1024)
    dims = _NT if transpose else (((1,), (0,)), ((), ()))

    def body(a_ref, b_ref, o_ref):
        o_ref[...] = lax.dot_general(a_ref[...].astype(BF16), b_ref[0].astype(BF16), dims, preferred_element_type=F32)

    return _pcall(
        body,
        name=name,
        grid=(m // tm, nb),
        in_specs=[pl.BlockSpec((tm, win), lambda i, j: (i, j)), pl.BlockSpec((1, ka, kb), lambda i, j: (j, 0, 0))],
        out_specs=pl.BlockSpec((tm, wout), lambda i, j: (i, j)),
        out_shape=jax.ShapeDtypeStruct((m, nb * wout), F32),
        compiler_params=_params(("parallel", "parallel")),
    )(a, bc)


def _bd_dw_call(a, g, nb, name):
    m = a.shape[0]
    ka, kb = a.shape[1] // nb, g.shape[1] // nb
    tk = _tile(m, 1024)
    nk = m // tk

    def body(a_ref, g_ref, o_ref):
        part = lax.dot_general(a_ref[...].astype(BF16), g_ref[...].astype(BF16), _TN, preferred_element_type=F32)
        k = pl.program_id(1)

        @pl.when(k == 0)
        def _():
            o_ref[0] = part

        @pl.when(k > 0)
        def _():
            o_ref[0] += part

    return _pcall(
        body,
        name=name,
        grid=(nb, nk),
        in_specs=[pl.BlockSpec((tk, ka), lambda j, k: (k, j)), pl.BlockSpec((tk, kb), lambda j, k: (k, j))],
        out_specs=pl.BlockSpec((1, ka, kb), lambda j, k: (j, 0, 0)),
        out_shape=jax.ShapeDtypeStruct((nb, ka, kb), F32),
        compiler_params=_params(("parallel", "arbitrary")),
    )(a, g)


@functools.partial(jax.custom_vjp, nondiff_argnums=(2,))
def bd_matmul(a, bc, name):
    return _bd_call(a, bc, False, name + "_f")


def _bd_matmul_fwd(a, bc, name):
    return _bd_call(a, bc, False, name + "_f"), (a, bc)


def _bd_matmul_bwd(name, res, g):
    a, bc = res
    return _bd_call(g, bc, True, name + "_da"), _bd_dw_call(a, g, bc.shape[0], name + "_dw")


bd_matmul.defvjp(_bd_matmul_fwd, _bd_matmul_bwd)


@functools.partial(jax.custom_vjp, nondiff_argnums=(3,))
def matmul_w(a, w16, tok, name):
    return _mm(a, w16, b_blocks=w16.ndim == 3, name=name + "_f")


def _matmul_w_fwd(a, w16, tok, name):
    return _mm(a, w16, b_blocks=w16.ndim == 3, name=name + "_f"), (a, w16)


def _matmul_w_bwd(name, res, g):
    a, w16 = res
    blocks = w16.shape[0] if w16.ndim == 3 else 0
    return (_mm(g, w16, tb=True, b_blocks=bool(blocks), name=name + "_da"), jnp.zeros_like(w16),
            _mm(a, g, ta=True, out_blocks=blocks, name=name + "_dw"))


matmul_w.defvjp(_matmul_w_fwd, _matmul_w_bwd)


def _swiglu(gt, up):
    return gt * jax.nn.sigmoid(gt) * up


def _swiglu_bwd(dact, gt, up):
    sg = jax.nn.sigmoid(gt)
    return dact * up * sg * (1.0 + gt * (1.0 - sg)), dact * gt * sg


def _add_residual(product, x):
    return (product + x,)


def _silu_gate(o, g):
    return o * (g * jax.nn.sigmoid(g))


def _silu_gate_bwd(dm, o, g):
    sg = jax.nn.sigmoid(g)
    return dm * g * sg, dm * o * sg * (1.0 + g * (1.0 - sg))


@functools.partial(jax.custom_vjp, nondiff_argnums=(4,))
def silu_gated_out(o, g, w16, tok, name):
    return _mm(o, w16, pro=(_silu_gate, g), name=name + "_f")


def _silu_gated_out_fwd(o, g, w16, tok, name):
    out, left16 = _mm(o, w16, pro=(_silu_gate, g), save_left=True, name=name + "_f")
    return out, (o, g, w16, left16)


def _silu_gated_out_bwd(name, res, d):
    o, g, w16, left16 = res
    d_o, d_g = _mm(d, w16, tb=True, epi=(_silu_gate_bwd, o, g), name=name + "_da")
    return d_o, d_g, jnp.zeros_like(w16), _mm(left16, d, ta=True, name=name + "_dw")


silu_gated_out.defvjp(_silu_gated_out_fwd, _silu_gated_out_bwd)


@functools.partial(jax.custom_vjp, nondiff_argnums=(5,))
def swiglu_down(x, gt, up, w16, tok, name):
    return _mm(gt, w16, pro=(_swiglu, up), epi=(_add_residual, x), name=name + "_f")[0]


def _swiglu_down_fwd(x, gt, up, w16, tok, name):
    out, act16 = _mm(gt, w16, pro=(_swiglu, up), epi=(_add_residual, x), save_left=True, name=name + "_f")
    return out, (gt, up, w16, act16)


def _swiglu_down_bwd(name, res, g):
    gt, up, w16, act16 = res
    d_gt, d_up = _mm(g, w16, tb=True, epi=(_swiglu_bwd, gt, up), name=name + "_da")
    return g, d_gt, d_up, jnp.zeros_like(w16), _mm(act16, g, ta=True, name=name + "_dw")


swiglu_down.defvjp(_swiglu_down_fwd, _swiglu_down_bwd)


def _merge(y1, y2, y3, l1, l2, l3):
    return jax.nn.sigmoid(l1) * y1 + jax.nn.sigmoid(l2) * y2 + jax.nn.sigmoid(l3) * y3


def _merge_bwd(dm, y1, y2, y3, l1, l2, l3):
    dys, dls = [], []
    for y, logit in ((y1, l1), (y2, l2), (y3, l3)):
        s = jax.nn.sigmoid(logit)
        dys.append(dm * s)
        dls.append(dm * y * s * (1.0 - s))
    return (*dys, *dls)


@functools.partial(jax.custom_vjp, nondiff_argnums=(5,))
def gated_merge_out(x, ys, logits, w16, tok, name):
    return _mm(ys[0], w16, pro=(_merge, *ys[1:], *logits), epi=(_add_residual, x), name=name + "_f")[0]


def _gated_merge_out_fwd(x, ys, logits, w16, tok, name):
    out, merged16 = _mm(ys[0], w16, pro=(_merge, *ys[1:], *logits), epi=(_add_residual, x), save_left=True,
                        name=name + "_f")
    return out, (ys, logits, w16, merged16)


def _gated_merge_out_bwd(name, res, g):
    ys, logits, w16, merged16 = res
    d = _mm(g, w16, tb=True, epi=(_merge_bwd, *ys, *logits), name=name + "_da")
    return g, tuple(d[:3]), tuple(d[3:]), jnp.zeros_like(w16), _mm(merged16, g, ta=True, name=name + "_dw")


gated_merge_out.defvjp(_gated_merge_out_fwd, _gated_merge_out_bwd)


@functools.partial(jax.custom_vjp, nondiff_argnums=(3,))
def matmul_w_t(a, w16, tok, name):
    return _mm(w16, a, ta=True, tb=True, name=name + "_f")


def _matmul_w_t_fwd(a, w16, tok, name):
    return _mm(w16, a, ta=True, tb=True, name=name + "_f"), (a, w16)


def _matmul_w_t_bwd(name, res, g):
    a, w16 = res
    return (_mm(g, w16, ta=True, tb=True, name=name + "_da"), jnp.zeros_like(w16),
            _mm(a, g, ta=True, tb=True, name=name + "_dw"))


matmul_w_t.defvjp(_matmul_w_t_fwd, _matmul_w_t_bwd)


@functools.partial(jax.custom_vjp, nondiff_argnums=(3,))
def matmul_w_at(at, w16, tok, name):
    return _mm(at, w16, ta=True, name=name + "_f")


def _matmul_w_at_fwd(at, w16, tok, name):
    return _mm(at, w16, ta=True, name=name + "_f"), (at, w16)


def _matmul_w_at_bwd(name, res, g):
    at, w16 = res
    return _mm(w16, g, tb=True, name=name + "_da"), jnp.zeros_like(w16), _mm(at, g, name=name + "_dw")


matmul_w_at.defvjp(_matmul_w_at_fwd, _matmul_w_at_bwd)


@functools.partial(jax.custom_vjp, nondiff_argnums=(4,))
def matmul_w_at_res(x, at, w16, tok, name):
    return _mm(at, w16, ta=True, epi=(_add_residual, x), name=name + "_f")[0]


def _matmul_w_at_res_fwd(x, at, w16, tok, name):
    return _mm(at, w16, ta=True, epi=(_add_residual, x), name=name + "_f")[0], (at, w16)


def _matmul_w_at_res_bwd(name, res, g):
    return (g, *_matmul_w_at_bwd(name, res, g))


matmul_w_at_res.defvjp(_matmul_w_at_res_fwd, _matmul_w_at_res_bwd)


def _rms_fwd_call(x, g, name):
    rows, d = x.shape
    tr = _tile(rows, 512, 8)

    def body(x_ref, g_ref, y_ref):
        xv = x_ref[...]
        rstd = lax.rsqrt(jnp.mean(xv * xv, axis=-1, keepdims=True) + RMS_EPS)
        y_ref[...] = xv * rstd * g_ref[...]

    return _pcall(
        body,
        name=name,
        grid=(rows // tr,),
        in_specs=[pl.BlockSpec((tr, d), lambda i: (i, 0)), pl.BlockSpec((1, d), lambda i: (0, 0))],
        out_specs=pl.BlockSpec((tr, d), lambda i: (i, 0)),
        out_shape=jax.ShapeDtypeStruct((rows, d), F32),
        compiler_params=_params(("parallel",)),
    )(x, g.reshape(1, d))


def _rms_bwd_call(x, g, dy, name):
    rows, d = x.shape
    tr = _tile(rows, 512, 8)
    nb = rows // tr

    def body(x_ref, g_ref, dy_ref, dx_ref, dg_ref):
        xv = x_ref[...]
        rstd = lax.rsqrt(jnp.mean(xv * xv, axis=-1, keepdims=True) + RMS_EPS)
        xh = xv * rstd
        dyv = dy_ref[...]
        dyg = dyv * g_ref[...]
        dx_ref[...] = rstd * (dyg - xh * jnp.mean(dyg * xh, axis=-1, keepdims=True))
        dg_ref[...] = jnp.sum((dyv * xh).reshape(tr // 8, 8, d), axis=0)

    dx, dgp = _pcall(
        body,
        name=name,
        grid=(nb,),
        in_specs=[
            pl.BlockSpec((tr, d), lambda i: (i, 0)),
            pl.BlockSpec((1, d), lambda i: (0, 0)),
            pl.BlockSpec((tr, d), lambda i: (i, 0)),
        ],
        out_specs=[pl.BlockSpec((tr, d), lambda i: (i, 0)), pl.BlockSpec((8, d), lambda i: (i, 0))],
        out_shape=[jax.ShapeDtypeStruct((rows, d), F32), jax.ShapeDtypeStruct((nb * 8, d), F32)],
        compiler_params=_params(("parallel",)),
    )(x, g.reshape(1, d), dy)
    return dx, jnp.sum(dgp, axis=0)


@functools.partial(jax.custom_vjp, nondiff_argnums=(2,))
def rmsnorm(x, g, name):
    return _rms_fwd_call(x, g, name + "_f")


def _rmsnorm_fwd(x, g, name):
    return _rms_fwd_call(x, g, name + "_f"), (x, g)


def _rmsnorm_bwd(name, res, dy):
    x, g = res
    return _rms_bwd_call(x, g, dy, name + "_b")


rmsnorm.defvjp(_rmsnorm_fwd, _rmsnorm_bwd)


def _attn_blocks(sq, sk):
    return _tile(sq, 1024), _tile(sk, 512)


def _attn_pt(qs, k_ref, qi, ki, bq, bk, masked, shift):
    st = jnp.dot(k_ref[0].astype(BF16), qs, preferred_element_type=F32)
    if masked:
        kpos = ki * bk + lax.broadcasted_iota(jnp.int32, (bk, bq), 0)
        qpos = qi * bq + lax.broadcasted_iota(jnp.int32, (bk, bq), 1)
        st = jnp.where(kpos <= qpos, st, MASK_VALUE)
    return st if shift is None else jnp.exp(st - shift)


def _attn_pairs(nq, nk, bq, bk, causal, k_major):
    ok = lambda i, j: (not causal) or j * bk <= i * bq + bq - 1
    if k_major:
        pairs = [(i, j) for j in range(nk) for i in range(nq) if ok(i, j)]
    else:
        pairs = [(i, j) for i in range(nq) for j in range(nk) if ok(i, j)]
    qtab = jnp.asarray([p[0] for p in pairs], jnp.int32)
    ktab = jnp.asarray([p[1] for p in pairs], jnp.int32)
    first_q = (lambda j: (j * bk) // bq) if causal else (lambda j: 0)
    last_k = (lambda i: jnp.minimum((i * bq + bq - 1) // bk, nk - 1)) if causal else (lambda i: nk - 1)
    return qtab, ktab, first_q, last_k


def _attn_cases(causal, qi, ki, bq, bk, step):
    if not causal:
        step(False)
        return
    pl.when(ki * bk + bk - 1 <= qi * bq)(functools.partial(step, False))
    pl.when(ki * bk + bk - 1 > qi * bq)(functools.partial(step, True))


def _attn_fwd_call(qt, k, vt, scale, causal, name):
    h, sk, dq = k.shape
    sq = qt.shape[1]
    dv = vt.shape[0] // h
    bq, bk = _attn_blocks(sq, sk)
    nq, nk = sq // bq, sk // bk

    qtab, ktab, _, last_k = _attn_pairs(nq, nk, bq, bk, causal, False)

    def body(qi_ref, ki_ref, q_ref, k_ref, v_ref, o_ref, lse_ref, m_sc, l_sc, acc_sc):
        qi, ki = qi_ref[pl.program_id(1)], ki_ref[pl.program_id(1)]

        @pl.when(ki == 0)
        def _():
            m_sc[...] = jnp.full(m_sc.shape, -jnp.inf, F32)
            l_sc[...] = jnp.zeros(l_sc.shape, F32)
            acc_sc[...] = jnp.zeros(acc_sc.shape, F32)

        def step(masked):
            qs = (q_ref[...] * scale).astype(BF16)
            st = _attn_pt(qs, k_ref, qi, ki, bq, bk, masked, None)
            m_prev = m_sc[...]
            m_new = jnp.maximum(m_prev, jnp.max(st, axis=0, keepdims=True))
            alpha = jnp.exp(m_prev - m_new)
            pt = jnp.exp(st - m_new)
            l_sc[...] = alpha * l_sc[...] + jnp.sum(pt, axis=0, keepdims=True)
            acc_sc[...] = alpha * acc_sc[...] + jnp.dot(v_ref[...].astype(BF16), pt.astype(BF16),
                                                        preferred_element_type=F32)
            m_sc[...] = m_new

        _attn_cases(causal, qi, ki, bq, bk, step)

        @pl.when(ki == last_k(qi))
        def _():
            o_ref[...] = acc_sc[...] / l_sc[...]
            lse_ref[0] = m_sc[...] + jnp.log(l_sc[...])

    qmap = lambda hh, p, qi_ref, ki_ref: (hh, qi_ref[p])
    return _pcall(
        body,
        name=name,
        grid_spec=pltpu.PrefetchScalarGridSpec(
            num_scalar_prefetch=2,
            grid=(h, qtab.shape[0]),
            in_specs=[
                pl.BlockSpec((dq, bq), qmap),
                pl.BlockSpec((1, bk, dq), lambda hh, p, qi_ref, ki_ref: (hh, ki_ref[p], 0)),
                pl.BlockSpec((dv, bk), lambda hh, p, qi_ref, ki_ref: (hh, ki_ref[p])),
            ],
            out_specs=[pl.BlockSpec((dv, bq), qmap),
                       pl.BlockSpec((1, 1, bq), lambda hh, p, qi_ref, ki_ref: (hh, 0, qi_ref[p]))],
            scratch_shapes=[pltpu.VMEM((1, bq), F32), pltpu.VMEM((1, bq), F32), pltpu.VMEM((dv, bq), F32)],
        ),
        out_shape=[jax.ShapeDtypeStruct((h * dv, sq), F32), jax.ShapeDtypeStruct((h, 1, sq), F32)],
        compiler_params=_params(("parallel", "arbitrary")),
    )(qtab, ktab, qt, k, vt)


_TN = (((0,), (0,)), ((), ()))
_NT = (((1,), (1,)), ((), ()))


def _attn_dst(qs, k_ref, v_ref, do_ref, lse_ref, dl_ref, qi, ki, bq, bk, masked):
    pt = _attn_pt(qs, k_ref, qi, ki, bq, bk, masked, lse_ref[0])
    dpt = lax.dot_general(v_ref[...].astype(BF16), do_ref[...].astype(BF16), _TN, preferred_element_type=F32)
    return pt, pt * (dpt - dl_ref[0])


def _attn_dq_call(qt, k, vt, dot, lse, delta, scale, causal, name):
    h, sk, dq = k.shape
    sq = qt.shape[1]
    dv = vt.shape[0] // h
    bq, bk = _attn_blocks(sq, sk)
    nq, nk = sq // bq, sk // bk

    qtab, ktab, _, last_k = _attn_pairs(nq, nk, bq, bk, causal, False)

    def body(qi_ref, ki_ref, q_ref, k_ref, v_ref, do_ref, lse_ref, dl_ref, dq_ref, acc_sc):
        qi, ki = qi_ref[pl.program_id(1)], ki_ref[pl.program_id(1)]

        @pl.when(ki == 0)
        def _():
            acc_sc[...] = jnp.zeros(acc_sc.shape, F32)

        def step(masked):
            qs = (q_ref[...] * scale).astype(BF16)
            _, dst = _attn_dst(qs, k_ref, v_ref, do_ref, lse_ref, dl_ref, qi, ki, bq, bk, masked)
            acc_sc[...] += lax.dot_general(k_ref[0].astype(BF16), dst.astype(BF16), _TN, preferred_element_type=F32)

        _attn_cases(causal, qi, ki, bq, bk, step)

        @pl.when(ki == last_k(qi))
        def _():
            dq_ref[...] = acc_sc[...] * scale

    qmap = lambda hh, p, qi_ref, ki_ref: (hh, qi_ref[p])
    rowmap = lambda hh, p, qi_ref, ki_ref: (hh, 0, qi_ref[p])
    return _pcall(
        body,
        name=name,
        grid_spec=pltpu.PrefetchScalarGridSpec(
            num_scalar_prefetch=2,
            grid=(h, qtab.shape[0]),
            in_specs=[
                pl.BlockSpec((dq, bq), qmap),
                pl.BlockSpec((1, bk, dq), lambda hh, p, qi_ref, ki_ref: (hh, ki_ref[p], 0)),
                pl.BlockSpec((dv, bk), lambda hh, p, qi_ref, ki_ref: (hh, ki_ref[p])),
                pl.BlockSpec((dv, bq), qmap),
                pl.BlockSpec((1, 1, bq), rowmap),
                pl.BlockSpec((1, 1, bq), rowmap),
            ],
            out_specs=pl.BlockSpec((dq, bq), qmap),
            scratch_shapes=[pltpu.VMEM((dq, bq), F32)],
        ),
        out_shape=jax.ShapeDtypeStruct((h * dq, sq), F32),
        compiler_params=_params(("parallel", "arbitrary")),
    )(qtab, ktab, qt, k, vt, dot, lse, delta)


def _attn_dkv_call(qt, k, vt, dot, lse, delta, scale, causal, name):
    h, sk, dq = k.shape
    sq = qt.shape[1]
    dv = vt.shape[0] // h
    bq, bk = _attn_blocks(sq, sk)
    nq, nk = sq // bq, sk // bk

    qtab, ktab, first_q, _ = _attn_pairs(nq, nk, bq, bk, causal, True)

    def body(qi_ref, ki_ref, q_ref, k_ref, v_ref, do_ref, lse_ref, dl_ref, dk_ref, dv_ref, dk_sc, dv_sc):
        qi, ki = qi_ref[pl.program_id(1)], ki_ref[pl.program_id(1)]

        @pl.when(qi == first_q(ki))
        def _():
            dk_sc[...] = jnp.zeros(dk_sc.shape, F32)
            dv_sc[...] = jnp.zeros(dv_sc.shape, F32)

        def step(masked):
            qs = (q_ref[...] * scale).astype(BF16)
            pt, dst = _attn_dst(qs, k_ref, v_ref, do_ref, lse_ref, dl_ref, qi, ki, bq, bk, masked)
            dv_sc[...] += lax.dot_general(do_ref[...].astype(BF16), pt.astype(BF16), _NT, preferred_element_type=F32)
            dk_sc[...] += lax.dot_general(dst.astype(BF16), qs, _NT, preferred_element_type=F32)

        _attn_cases(causal, qi, ki, bq, bk, step)

        @pl.when(qi == nq - 1)
        def _():
            dk_ref[0] = dk_sc[...]
            dv_ref[...] = dv_sc[...]

    qmap = lambda hh, p, qi_ref, ki_ref: (hh, qi_ref[p])
    rowmap = lambda hh, p, qi_ref, ki_ref: (hh, 0, qi_ref[p])
    kmap = lambda hh, p, qi_ref, ki_ref: (hh, ki_ref[p], 0)
    vmap = lambda hh, p, qi_ref, ki_ref: (hh, ki_ref[p])
    return _pcall(
        body,
        name=name,
        grid_spec=pltpu.PrefetchScalarGridSpec(
            num_scalar_prefetch=2,
            grid=(h, qtab.shape[0]),
            in_specs=[
                pl.BlockSpec((dq, bq), qmap),
                pl.BlockSpec((1, bk, dq), kmap),
                pl.BlockSpec((dv, bk), vmap),
                pl.BlockSpec((dv, bq), qmap),
                pl.BlockSpec((1, 1, bq), rowmap),
                pl.BlockSpec((1, 1, bq), rowmap),
            ],
            out_specs=[pl.BlockSpec((1, bk, dq), kmap), pl.BlockSpec((dv, bk), vmap)],
            scratch_shapes=[pltpu.VMEM((bk, dq), F32), pltpu.VMEM((dv, bk), F32)],
        ),
        out_shape=[jax.ShapeDtypeStruct((h, sk, dq), F32), jax.ShapeDtypeStruct((h * dv, sk), F32)],
        compiler_params=_params(("parallel", "arbitrary")),
    )(qtab, ktab, qt, k, vt, dot, lse, delta)


@functools.partial(jax.custom_vjp, nondiff_argnums=(3, 4, 5))
def attention(qt, k, vt, scale, causal, name):
    return _attn_fwd_call(qt, k, vt, scale, causal, name + "_f")[0]


def _attention_fwd(qt, k, vt, scale, causal, name):
    ot, lse = _attn_fwd_call(qt, k, vt, scale, causal, name + "_f")
    return ot, (qt, k, vt, ot, lse)


def _attention_bwd(scale, causal, name, res, dot):
    qt, k, vt, ot, lse = res
    h = k.shape[0]
    delta = jnp.sum((dot * ot).reshape(h, ot.shape[0] // h, ot.shape[1]), axis=1, keepdims=True)
    dqt = _attn_dq_call(qt, k, vt, dot, lse, delta, scale, causal, name + "_dq")
    dk, dvt = _attn_dkv_call(qt, k, vt, dot, lse, delta, scale, causal, name + "_dkv")
    return dqt, dk, dvt


attention.defvjp(_attention_fwd, _attention_bwd)


SCAN_ROWS = 8
SCAN_LANES = 256
SCAN_TBLOCK = 512
SCAN_UNROLL = 4


def _scan_tables(lr, li):
    pr, pi = [lr], [li]
    for _ in range(SCAN_ROWS - 1):
        pr, pi = pr + [pr[-1] * lr - pi[-1] * li], pi + [pr[-1] * li + pi[-1] * lr]
    return jnp.concatenate(pr, axis=0), jnp.concatenate(pi, axis=0)


def _scan_call(ur, ui, lr, li, reverse, name, states=None):
    s, n = ur.shape
    tb = _tile(s, SCAN_TBLOCK, 8)
    nt, nl = s // tb, n // SCAN_LANES
    ntile = tb // SCAN_ROWS
    pr, pi = _scan_tables(lr, li)
    if reverse:
        pr, pi = pr[::-1], pi[::-1]
    shape = (SCAN_ROWS, SCAN_LANES)
    with_states = states is not None
    assert reverse or not with_states

    def body(*refs):
        if with_states:
            ur_ref, ui_ref, pr_ref, pi_ref, sr_ref, si_ref, xr_ref, xi_ref, dr_ref, di_ref, cr_sc, ci_sc, ar_sc, ai_sc = refs
        else:
            ur_ref, ui_ref, pr_ref, pi_ref, xr_ref, xi_ref, cr_sc, ci_sc, ar_sc, ai_sc = refs

        @pl.when(pl.program_id(1) == 0)
        def _():
            for sc in (cr_sc, ci_sc, ar_sc, ai_sc):
                sc[...] = jnp.zeros(shape, F32)

        prv, piv = pr_ref[...], pi_ref[...]
        rows = lax.broadcasted_iota(jnp.int32, shape, 0)

        def powers(k):
            r = (SCAN_ROWS - k) if reverse else (k - 1)
            return jnp.broadcast_to(prv[r:r + 1], shape), jnp.broadcast_to(piv[r:r + 1], shape)

        pw = [powers(k) for k in (1, 2, 4)]

        def tile(i, carry):
            cr, ci, acr, aci = carry
            j = (ntile - 1 - i) if reverse else i
            sl = pl.ds(pl.multiple_of(j * SCAN_ROWS, SCAN_ROWS), SCAN_ROWS)
            xr, xi = ur_ref[sl, :], ui_ref[sl, :]
            for (ar, ai), k in zip(pw, (1, 2, 4)):
                if reverse:
                    keep = rows < SCAN_ROWS - k
                    sr = jnp.where(keep, pltpu.roll(xr, SCAN_ROWS - k, 0), 0.0)
                    si = jnp.where(keep, pltpu.roll(xi, SCAN_ROWS - k, 0), 0.0)
                else:
                    keep = rows >= k
                    sr = jnp.where(keep, pltpu.roll(xr, k, 0), 0.0)
                    si = jnp.where(keep, pltpu.roll(xi, k, 0), 0.0)
                xr, xi = xr + (ar * sr - ai * si), xi + (ar * si + ai * sr)
            xr, xi = xr + (prv * cr - piv * ci), xi + (prv * ci + piv * cr)
            xr_ref[sl, :] = xr
            xi_ref[sl, :] = xi
            if with_states:
                last = rows == SCAN_ROWS - 1
                nr = jnp.where(last, cr, pltpu.roll(xr, SCAN_ROWS - 1, 0))
                ni = jnp.where(last, ci, pltpu.roll(xi, SCAN_ROWS - 1, 0))
                sr, si = sr_ref[sl, :], si_ref[sl, :]
                acr, aci = acr + (sr * nr + si * ni), aci + (sr * ni - si * nr)
            e = 0 if reverse else SCAN_ROWS - 1
            return jnp.broadcast_to(xr[e:e + 1], shape), jnp.broadcast_to(xi[e:e + 1], shape), acr, aci

        def tiles(g, carry):
            for u in range(SCAN_UNROLL):
                carry = tile(g * SCAN_UNROLL + u, carry)
            return carry

        assert ntile % SCAN_UNROLL == 0
        cr, ci, acr, aci = lax.fori_loop(0, ntile // SCAN_UNROLL, tiles,
                                         (cr_sc[...], ci_sc[...], ar_sc[...], ai_sc[...]))
        cr_sc[...] = cr
        ci_sc[...] = ci
        if with_states:
            ar_sc[...] = acr
            ai_sc[...] = aci

            @pl.when(pl.program_id(1) == nt - 1)
            def _():
                dr_ref[...] = acr
                di_ref[...] = aci

    tmap = (lambda l, t: (nt - 1 - t, l)) if reverse else (lambda l, t: (t, l))
    blk = pl.BlockSpec((tb, SCAN_LANES), tmap)
    tab = pl.BlockSpec((SCAN_ROWS, SCAN_LANES), lambda l, t: (0, l))
    full = jax.ShapeDtypeStruct((s, n), F32)
    return _pcall(
        body,
        name=name,
        grid=(nl, nt),
        in_specs=[blk, blk, tab, tab] + ([blk, blk] if with_states else []),
        out_specs=[blk, blk] + ([tab, tab] if with_states else []),
        out_shape=[full, full] + ([jax.ShapeDtypeStruct((SCAN_ROWS, n), F32)] * 2 if with_states else []),
        scratch_shapes=[pltpu.VMEM(shape, F32)] * 4,
        compiler_params=_params(("parallel", "arbitrary")),
    )(ur, ui, pr, pi, *(states if with_states else ()))


@functools.partial(jax.custom_vjp, nondiff_argnums=(4,))
def s5_scan(ur, ui, lr, li, name):
    return tuple(_scan_call(ur, ui, lr, li, False, name + "_f"))


def _s5_scan_fwd(ur, ui, lr, li, name):
    xr, xi = _scan_call(ur, ui, lr, li, False, name + "_f")
    return (xr, xi), (xr, xi, lr, li)


def _s5_scan_bwd(name, res, g):
    xr, xi, lr, li = res
    gr, gi, dlr, dli = _scan_call(g[0], g[1], lr, -li, True, name + "_b", states=(xr, xi))
    return gr, gi, jnp.sum(dlr, axis=0, keepdims=True), jnp.sum(dli, axis=0, keepdims=True)


s5_scan.defvjp(_s5_scan_fwd, _s5_scan_bwd)


def _running_sum(x, reverse):
    n = x.shape[0]
    rows = lax.broadcasted_iota(jnp.int32, x.shape, 0)
    k = 1
    while k < n:
        if reverse:
            x = x + jnp.where(rows < n - k, pltpu.roll(x, n - k, 0), 0.0)
        else:
            x = x + jnp.where(rows >= k, pltpu.roll(x, k, 0), 0.0)
        k *= 2
    return x


@jax.custom_vjp
def cumsum_rows(x):
    return _running_sum(x, False)


cumsum_rows.defvjp(lambda x: (_running_sum(x, False), None), lambda _, g: (_running_sum(g, True),))


def _hg_chunk(HG_SUB, q, k, v, lf, st):
    c = HG_CHUNK
    b = cumsum_rows(lf)
    qb = q * jnp.exp(b)
    o = lax.dot_general(qb.astype(BF16), st.astype(BF16), (((1,), (1,)), ((), ())), preferred_element_type=F32)
    nsub = c // HG_SUB
    srow = lax.broadcasted_iota(jnp.int32, (HG_SUB, HG_SUB, 1), 0)
    scol = lax.broadcasted_iota(jnp.int32, (HG_SUB, HG_SUB, 1), 1)
    smask = scol <= srow
    outs = []
    for i in range(nsub):
        lo = i * HG_SUB
        bi, qi, ki = b[lo:lo + HG_SUB], q[lo:lo + HG_SUB], k[lo:lo + HG_SUB]
        diff = jnp.where(smask, bi[:, None, :] - bi[None, :, :], 0.0)
        e = jnp.where(smask, jnp.exp(diff), 0.0)
        a_ii = jnp.sum(qi[:, None, :] * ki[None, :, :] * e, axis=-1)
        oi = jnp.dot(a_ii.astype(BF16), v[lo:lo + HG_SUB].astype(BF16), preferred_element_type=F32)
        if i > 0:
            r = b[lo - 1:lo]
            qt = qi * jnp.exp(bi - r)
            kt = k[:lo] * jnp.exp(r - b[:lo])
            a_ij = lax.dot_general(qt.astype(BF16), kt.astype(BF16), (((1,), (1,)), ((), ())),
                                   preferred_element_type=F32)
            oi = oi + jnp.dot(a_ij.astype(BF16), v[:lo].astype(BF16), preferred_element_type=F32)
        outs.append(oi)
    o = o + jnp.concatenate(outs, axis=0)
    bl = b[c - 1:c]
    kd = k * jnp.exp(bl - b)
    st_new = st * jnp.exp(bl) + lax.dot_general(v.astype(BF16), kd.astype(BF16), (((0,), (0,)), ((), ())),
                                                preferred_element_type=F32)
    return o, st_new


def _hg_fwd_call(q, k, v, lf, name):
    s, w = q.shape
    nc = s // HG_CHUNK
    blk = pl.BlockSpec((HG_CHUNK, w), lambda i: (i, 0))

    def body(q_ref, k_ref, v_ref, lf_ref, o_ref, st_ref, st_sc):
        @pl.when(pl.program_id(0) == 0)
        def _():
            st_sc[...] = jnp.zeros(st_sc.shape, F32)

        for hh in range(HG_HEADS):
            cs = slice(hh * HG_DK, (hh + 1) * HG_DK)
            st = st_sc[hh]
            st_ref[0, hh] = st
            o, st_new = _hg_chunk(HG_SUB_FWD, q_ref[:, cs], k_ref[:, cs], v_ref[:, cs], lf_ref[:, cs], st)
            o_ref[:, cs] = o
            st_sc[hh] = st_new

    return _pcall(
        body,
        name=name,
        grid=(nc,),
        in_specs=[blk, blk, blk, blk],
        out_specs=[blk, pl.BlockSpec((1, HG_HEADS, HG_DV, HG_DK), lambda i: (i, 0, 0, 0))],
        out_shape=[jax.ShapeDtypeStruct((s, w), F32), jax.ShapeDtypeStruct((nc, HG_HEADS, HG_DV, HG_DK), F32)],
        scratch_shapes=[pltpu.VMEM((HG_HEADS, HG_DV, HG_DK), F32)],
        compiler_params=_params(("arbitrary",)),
    )(q, k, v, lf)


def _hg_bwd_call(q, k, v, lf, sts, do, name):
    s, w = q.shape
    nc = s // HG_CHUNK
    blk = pl.BlockSpec((HG_CHUNK, w), lambda i: (nc - 1 - i, 0))

    def body(q_ref, k_ref, v_ref, lf_ref, st_ref, do_ref, dq_ref, dk_ref, dv_ref, dlf_ref, dst_sc):
        @pl.when(pl.program_id(0) == 0)
        def _():
            dst_sc[...] = jnp.zeros(dst_sc.shape, F32)

        def head(hh, carry):
            cs = pl.ds(pl.multiple_of(hh * HG_DK, HG_DK), HG_DK)
            _, vjp = jax.vjp(functools.partial(_hg_chunk, HG_SUB_BWD), q_ref[:, cs], k_ref[:, cs], v_ref[:, cs],
                             lf_ref[:, cs], st_ref[0, hh])
            dq, dk, dv, dlf, dst = vjp((do_ref[:, cs], dst_sc[hh]))
            dq_ref[:, cs] = dq
            dk_ref[:, cs] = dk
            dv_ref[:, cs] = dv
            dlf_ref[:, cs] = dlf
            dst_sc[hh] = dst
            return carry

        lax.fori_loop(0, HG_HEADS, head, 0)

    out = jax.ShapeDtypeStruct((s, w), F32)
    return _pcall(
        body,
        name=name,
        grid=(nc,),
        in_specs=[blk, blk, blk, blk, pl.BlockSpec((1, HG_HEADS, HG_DV, HG_DK), lambda i: (nc - 1 - i, 0, 0, 0)), blk],
        out_specs=[blk, blk, blk, blk],
        out_shape=[out, out, out, out],
        scratch_shapes=[pltpu.VMEM((HG_HEADS, HG_DV, HG_DK), F32)],
        compiler_params=_params(("arbitrary",)),
    )(q, k, v, lf, sts, do)


@functools.partial(jax.custom_vjp, nondiff_argnums=(4,))
def hgrn_core(q, k, v, lf, name):
    return _hg_fwd_call(q, k, v, lf, name + "_f")[0]


def _hgrn_core_fwd(q, k, v, lf, name):
    o, sts = _hg_fwd_call(q, k, v, lf, name + "_f")
    return o, (q, k, v, lf, sts)


def _hgrn_core_bwd(name, res, do):
    q, k, v, lf, sts = res
    return tuple(_hg_bwd_call(q, k, v, lf, sts, do, name + "_b"))


hgrn_core.defvjp(_hgrn_core_fwd, _hgrn_core_bwd)


def _adamw_call(w, g, m, v, name):
    rows, cols = w.shape
    tr = _tile(rows, 256, 8)
    c1 = 1.0 / (1.0 - ADAM_B1 ** ADAM_STEP)
    c2 = 1.0 / (1.0 - ADAM_B2 ** ADAM_STEP)

    def body(w_ref, g_ref, m_ref, v_ref, d_ref, mo_ref, vo_ref):
        gv = g_ref[...]
        mn = ADAM_B1 * m_ref[...] + (1.0 - ADAM_B1) * gv
        vn = ADAM_B2 * v_ref[...] + (1.0 - ADAM_B2) * (gv * gv)
        d_ref[...] = -ADAM_LR * ((mn * c1) / (jnp.sqrt(vn * c2) + ADAM_EPS) + ADAM_WD * w_ref[...])
        mo_ref[...] = mn
        vo_ref[...] = vn

    blk = pl.BlockSpec((tr, cols), lambda i: (i, 0))
    out = jax.ShapeDtypeStruct((rows, cols), F32)
    return _pcall(
        body,
        name=name,
        grid=(rows // tr,),
        in_specs=[blk, blk, blk, blk],
        out_specs=[blk, blk, blk],
        out_shape=[out, out, out],
        compiler_params=_params(("parallel",)),
    )(w, g, m, v)


ANY = pl.BlockSpec(memory_space=pl.ANY)


def _place():
    return lax.axis_index("x"), lax.axis_index("y"), lax.axis_index("c")


def _other_chips(x, y):
    return [(1 - x, y), (x, 1 - y), (1 - x, 1 - y)]


def _gather_weights(shards):
    nw = len(shards)

    def body(*refs):
        w_refs, g_refs, (send_sems, recv_sems) = refs[:nw], refs[nw:2 * nw], refs[2 * nw:]
        x, y, c = _place()
        myq = 2 * x + y
        sibling = (x, y, 1 - c)
        chips = _other_chips(x, y)

        def copy(k, src, dst, to):
            return pltpu.make_async_remote_copy(src_ref=src, dst_ref=dst, send_sem=send_sems.at[k],
                                                recv_sem=recv_sems.at[k], device_id=to, device_id_type=MESH)

        first = []
        for i in range(nw):
            for j, (px, py) in enumerate(chips):
                first.append(copy(6 * i + j, w_refs[i].at[c], g_refs[i].at[myq, c], (px, py, c)))
                first[-1].start()
        passed = []
        for i in range(nw):
            for j, (px, py) in enumerate(chips):
                q = 2 * px + py
                copy(6 * i + j, w_refs[i].at[c], g_refs[i].at[q, c], (px, py, c)).wait_recv()
                passed.append(copy(6 * i + 3 + j, g_refs[i].at[q, c], g_refs[i].at[q, c], sibling))
                passed[-1].start()
        for i in range(nw):
            for j, (px, py) in enumerate(chips):
                q = 2 * px + py
                copy(6 * i + 3 + j, g_refs[i].at[q, 1 - c], g_refs[i].at[q, 1 - c], sibling).wait_recv()
        for cp in first + passed:
            cp.wait_send()

    got = _pcall(
        body,
        name="gather_weights",
        in_specs=[ANY] * nw,
        out_specs=[ANY] * nw,
        out_shape=[jax.ShapeDtypeStruct((N_CHIPS,) + s.shape, s.dtype) for s in shards],
        scratch_shapes=[pltpu.SemaphoreType.DMA((6 * nw,)), pltpu.SemaphoreType.DMA((6 * nw,))],
    )(*shards)
    my_chip = 2 * lax.axis_index("x") + lax.axis_index("y")
    return [lax.dynamic_update_slice(g, s[None], (my_chip, 0, 0, 0)) for g, s in zip(got, shards)]


HBM = pl.BlockSpec(memory_space=pltpu.HBM)
SEM = pl.BlockSpec(memory_space=pltpu.SEMAPHORE)


def _gather_start(shards, after):
    nw = len(shards)

    def body(*refs):
        w_refs, land_refs = refs[:nw], refs[nw:2 * nw]
        send_sems, recv_sems = refs[2 * nw + 1:2 * nw + 3]
        token = refs[-1]
        x, y, c = _place()
        myq = 2 * x + y
        for i in range(nw):
            hh = shards[i].shape[0] // 2
            rows = pl.ds(c * hh, hh)
            for j, (px, py) in enumerate(_other_chips(x, y)):
                for tc in range(2):
                    pltpu.make_async_remote_copy(
                        src_ref=w_refs[i].at[rows], dst_ref=land_refs[i].at[myq, rows],
                        send_sem=send_sems.at[6 * i + 2 * j + tc], recv_sem=recv_sems.at[6 * i + 2 * j + c],
                        device_id=(px, py, tc), device_id_type=MESH).start()
        token[...] = jnp.zeros_like(token)

    lands = [lax.empty((N_CHIPS,) + s.shape, s.dtype) for s in shards]
    out = _pcall(
        body,
        name="gather_start",
        in_specs=[HBM] * (2 * nw) + [pl.BlockSpec(memory_space=pl.ANY)],
        out_specs=[SEM, SEM] + [HBM] * (2 * nw) + [pl.BlockSpec(memory_space=pltpu.VMEM)],
        out_shape=[pltpu.SemaphoreType.DMA((6 * nw,)), pltpu.SemaphoreType.DMA((6 * nw,))]
        + [pltpu.HBM(s.shape, s.dtype) for s in shards] + [pltpu.HBM(t.shape, t.dtype) for t in lands]
        + [jax.ShapeDtypeStruct((8, 128), F32)],
        input_output_aliases={i: 2 + i for i in range(2 * nw)},
        compiler_params=pltpu.CompilerParams(has_side_effects=pltpu.SideEffectType.DATAFLOW_SIDE_EFFECTING),
    )(*[pltpu.with_memory_space_constraint(t, pltpu.HBM) for t in list(shards) + lands], after)
    return out[0], out[1], out[2:2 + nw], out[2 + nw:2 + 2 * nw], out[-1]


def _gather_wait(send_sems, recv_sems, shards, lands, after):
    nw = len(shards)

    def body(*refs):
        w_refs, land_refs = refs[:nw], refs[nw:2 * nw]
        send_sems, recv_sems = refs[2 * nw:2 * nw + 2]
        x, y, c = _place()
        myq = 2 * x + y
        for i in range(nw):
            hh = shards[i].shape[0] // 2
            for j, (px, py) in enumerate(_other_chips(x, y)):
                q = 2 * px + py
                for other in range(2):
                    mine = pl.ds(c * hh, hh)
                    theirs = pl.ds(other * hh, hh)
                    pltpu.make_async_remote_copy(
                        src_ref=w_refs[i].at[mine], dst_ref=land_refs[i].at[myq, mine],
                        send_sem=send_sems.at[6 * i + 2 * j + other], recv_sem=recv_sems.at[6 * i + 2 * j + other],
                        device_id=(px, py, other), device_id_type=MESH).wait_send()
                    pltpu.make_async_remote_copy(
                        src_ref=w_refs[i].at[theirs], dst_ref=land_refs[i].at[q, theirs],
                        send_sem=send_sems.at[6 * i + 2 * j + other], recv_sem=recv_sems.at[6 * i + 2 * j + other],
                        device_id=(px, py, other), device_id_type=MESH).wait_recv()

    out = _pcall(
        body,
        name="gather_wait",
        in_specs=[HBM] * (2 * nw) + [SEM, SEM, pl.BlockSpec(memory_space=pl.ANY)],
        out_specs=[HBM] * (2 * nw),
        out_shape=[pltpu.HBM(s.shape, s.dtype) for s in shards] + [pltpu.HBM(t.shape, t.dtype) for t in lands],
        input_output_aliases={i: i for i in range(2 * nw)},
        compiler_params=pltpu.CompilerParams(has_side_effects=pltpu.SideEffectType.DATAFLOW_SIDE_EFFECTING),
    )(*shards, *lands, send_sems, recv_sems, after)
    my_chip = 2 * lax.axis_index("x") + lax.axis_index("y")
    return [lax.dynamic_update_slice(g, s[None], (my_chip, 0, 0)) for g, s in zip(out[nw:], out[:nw])]


def _swap_halves_to_sibling(grs):
    nw = len(grs)

    def body(*refs):
        g_refs, t_refs, (send_sems, recv_sems) = refs[:nw], refs[nw:2 * nw], refs[2 * nw:]
        x, y, c = _place()
        cps = []
        for i in range(nw):
            hh = grs[i].shape[1] // 2
            cps.append(pltpu.make_async_remote_copy(
                src_ref=g_refs[i].at[:, pl.ds((1 - c) * hh, hh)], dst_ref=t_refs[i], send_sem=send_sems.at[i],
                recv_sem=recv_sems.at[i], device_id=(x, y, 1 - c), device_id_type=MESH))
            cps[-1].start()
        for cp in cps:
            cp.wait()

    return _pcall(
        body,
        name="grad_d2d",
        in_specs=[ANY] * nw,
        out_specs=[ANY] * nw,
        out_shape=[jax.ShapeDtypeStruct((g.shape[0], g.shape[1] // 2, g.shape[2]), g.dtype) for g in grs],
        scratch_shapes=[pltpu.SemaphoreType.DMA((nw,)), pltpu.SemaphoreType.DMA((nw,))],
    )(*grs)


def _add_half(gr, t1, place, name):
    nq, r, wd = gr.shape
    hh = r // 2
    tr = _tile(hh, 512, 16)
    nb = hh // tr

    def body(p_ref, g_ref, t_ref, o_ref):
        o_ref[...] = (g_ref[...] + t_ref[...]).astype(BF16)

    return _pcall(
        body,
        name=name,
        grid_spec=pltpu.PrefetchScalarGridSpec(
            num_scalar_prefetch=1,
            grid=(nq, nb),
            in_specs=[
                pl.BlockSpec((1, tr, wd), lambda q, i, p_ref: (q, p_ref[0] * nb + i, 0)),
                pl.BlockSpec((1, tr, wd), lambda q, i, p_ref: (q, i, 0)),
            ],
            out_specs=pl.BlockSpec((1, tr, wd), lambda q, i, p_ref: (q, i, 0)),
        ),
        out_shape=jax.ShapeDtypeStruct((nq, hh, wd), BF16),
        compiler_params=_params(("parallel", "parallel")),
    )(place, gr, t1)


def _scatter_partials(ps):
    nw = len(ps)

    def body(*refs):
        p_refs, t_refs, (send_sems, recv_sems) = refs[:nw], refs[nw:2 * nw], refs[2 * nw:]
        x, y, c = _place()
        cps = []
        for i in range(nw):
            for j, (px, py) in enumerate(_other_chips(x, y)):
                cps.append(pltpu.make_async_remote_copy(
                    src_ref=p_refs[i].at[2 * px + py], dst_ref=t_refs[i].at[j], send_sem=send_sems.at[3 * i + j],
                    recv_sem=recv_sems.at[3 * i + j], device_id=(px, py, c), device_id_type=MESH))
                cps[-1].start()
        for cp in cps:
            cp.wait()

    return _pcall(
        body,
        name="grad_ici",
        in_specs=[ANY] * nw,
        out_specs=[ANY] * nw,
        out_shape=[jax.ShapeDtypeStruct((3,) + p.shape[1:], p.dtype) for p in ps],
        scratch_shapes=[pltpu.SemaphoreType.DMA((3 * nw,)), pltpu.SemaphoreType.DMA((3 * nw,))],
    )(*ps)


def _add_partials(gr, t1, t2, place, name):
    nq, r, wd = gr.shape
    hh = r // 2
    tr = _tile(hh, 512, 16)
    nb = hh // tr

    def body(p_ref, g_ref, s_ref, t_ref, o_ref):
        o_ref[...] = (((g_ref[0] + s_ref[0]) + t_ref[0].astype(F32)) + t_ref[1].astype(F32)) + t_ref[2].astype(F32)

    return _pcall(
        body,
        name=name,
        grid_spec=pltpu.PrefetchScalarGridSpec(
            num_scalar_prefetch=1,
            grid=(nb,),
            in_specs=[
                pl.BlockSpec((1, tr, wd), lambda i, p_ref: (p_ref[1], p_ref[0] * nb + i, 0)),
                pl.BlockSpec((1, tr, wd), lambda i, p_ref: (p_ref[1], i, 0)),
                pl.BlockSpec((3, tr, wd), lambda i, p_ref: (0, i, 0)),
            ],
            out_specs=pl.BlockSpec((tr, wd), lambda i, p_ref: (i, 0)),
        ),
        out_shape=jax.ShapeDtypeStruct((hh, wd), F32),
        compiler_params=_params(("parallel",)),
    )(place, gr, t1, t2)


def _join_halves(halves):
    nw = len(halves)

    def body(*refs):
        h_refs, f_refs, (send_sems, recv_sems) = refs[:nw], refs[nw:2 * nw], refs[2 * nw:]
        x, y, c = _place()

        def copy(i, rows):
            hh = halves[i].shape[0]
            return pltpu.make_async_remote_copy(src_ref=h_refs[i], dst_ref=f_refs[i].at[pl.ds(rows * hh, hh)],
                                                send_sem=send_sems.at[i], recv_sem=recv_sems.at[i],
                                                device_id=(x, y, 1 - c), device_id_type=MESH)

        for i in range(nw):
            copy(i, c).start()
        for i in range(nw):
            copy(i, c).wait_send()
            copy(i, 1 - c).wait_recv()

    got = _pcall(
        body,
        name="grad_join",
        in_specs=[ANY] * nw,
        out_specs=[ANY] * nw,
        out_shape=[jax.ShapeDtypeStruct((2 * h.shape[0], h.shape[1]), h.dtype) for h in halves],
        scratch_shapes=[pltpu.SemaphoreType.DMA((nw,)), pltpu.SemaphoreType.DMA((nw,))],
    )(*halves)
    c = lax.axis_index("c")
    return [lax.dynamic_update_slice(g, h, (c * h.shape[0], 0)) for g, h in zip(got, halves)]


N_DEV = 8


def _allreduce_small(buf):
    rows, wd = buf.shape
    n = rows // N_DEV

    def body(x_ref, o_ref, stage_ref, send_sems, recv_sems):
        x, y, c = _place()
        me = 4 * x + 2 * y + c

        def peer(k):
            return (x ^ (k >> 2), y ^ ((k >> 1) & 1), c ^ (k & 1))

        def chunk(ref, d):
            return ref.at[pl.ds(pl.multiple_of(d * n, 8), n)]

        first = []
        for k in range(1, N_DEV):
            px, py, pc = peer(k)
            cp = pltpu.make_async_remote_copy(src_ref=chunk(x_ref, 4 * px + 2 * py + pc), dst_ref=stage_ref.at[k],
                                              send_sem=send_sems.at[k - 1], recv_sem=recv_sems.at[k - 1],
                                              device_id=(px, py, pc), device_id_type=MESH)
            cp.start()
            first.append(cp)
        acc = chunk(x_ref, me)[...]
        for k in range(1, N_DEV):
            first[k - 1].wait_recv()
            acc = acc + stage_ref[k]
        chunk(o_ref, me)[...] = acc
        second = []
        for k in range(1, N_DEV):
            cp = pltpu.make_async_remote_copy(src_ref=chunk(o_ref, me), dst_ref=chunk(o_ref, me),
                                              send_sem=send_sems.at[6 + k], recv_sem=recv_sems.at[6 + k],
                                              device_id=peer(k), device_id_type=MESH)
            cp.start()
            second.append(cp)
        for k in range(1, N_DEV):
            px, py, pc = peer(k)
            them = 4 * px + 2 * py + pc
            pltpu.make_async_remote_copy(src_ref=chunk(o_ref, them), dst_ref=chunk(o_ref, them),
                                         send_sem=send_sems.at[6 + k], recv_sem=recv_sems.at[6 + k],
                                         device_id=(px, py, pc), device_id_type=MESH).wait_recv()
        for cp in first + second:
            cp.wait_send()

    vmem = pl.BlockSpec(memory_space=pltpu.VMEM)
    return _pcall(
        body,
        name="allreduce_small",
        in_specs=[vmem],
        out_specs=vmem,
        out_shape=jax.ShapeDtypeStruct((rows, wd), F32),
        scratch_shapes=[pltpu.VMEM((N_DEV, n, wd), F32), pltpu.SemaphoreType.DMA((14,)),
                        pltpu.SemaphoreType.DMA((14,))],
    )(buf)


@functools.partial(jax.custom_vjp, nondiff_argnums=(1,))
def split_cols(z, spec):
    outs = []
    for segs in spec[0]:
        parts = [z[:, s:s + n] for s, n in segs]
        outs.append(parts[0] if len(parts) == 1 else jnp.concatenate(parts, axis=1))
    return tuple(outs)


def _split_cols_fwd(z, spec):
    return split_cols(z, spec), None


def _split_cols_bwd(spec, _, gs):
    pieces, width = spec
    segs = []
    for p, piece in enumerate(pieces):
        off = 0
        for s, n in piece:
            segs.append((s, n, p, off))
            off += n
    rows, parts, pos = gs[0].shape[0], [], 0
    for s, n, p, off in sorted(segs):
        if s > pos:
            parts.append(jnp.zeros((rows, s - pos), gs[0].dtype))
        parts.append(gs[p] if (off == 0 and n == gs[p].shape[1]) else gs[p][:, off:off + n])
        pos = s + n
    if pos < width:
        parts.append(jnp.zeros((rows, width - pos), gs[0].dtype))
    return (jnp.concatenate(parts, axis=1),)


split_cols.defvjp(_split_cols_fwd, _split_cols_bwd)


def _contiguous(sizes):
    pieces, off = [], 0
    for n in sizes:
        pieces.append(((off, n),))
        off += n
    return tuple(pieces), off


def _rope(t, cos, sin):
    half = t.shape[-1] // 2
    t1, t2 = t[..., :half], t[..., half:]
    return jnp.concatenate([t1 * cos - t2 * sin, t2 * cos + t1 * sin], axis=-1)


def _s5_mixer(u, sp, l, w16, toks):
    g, p_, h = SSM_GROUPS, SSM_STATE, SSM_GROUP_CH
    lam_re, lam_im = sp["ssm_lam_re", l], sp["ssm_lam_im", l]
    step = jnp.exp(sp["ssm_log_step", l])[:, None]
    mag, ang = jnp.exp(lam_re * step), lam_im * step
    lbr, lbi = mag * jnp.cos(ang), mag * jnp.sin(ang)
    den = lam_re * lam_re + lam_im * lam_im
    nr, ni = lbr - 1.0, lbi
    qr, qi = (nr * lam_re + ni * lam_im) / den, (ni * lam_re - nr * lam_im) / den
    b_re, b_im = sp["ssm_b_re", l], sp["ssm_b_im", l]
    bbr = qr[..., None] * b_re - qi[..., None] * b_im
    bbi = qr[..., None] * b_im + qi[..., None] * b_re
    nb = g // SSM_PACK
    eye = jnp.eye(SSM_PACK, dtype=F32)

    def in_blocks(t):
        return jnp.einsum("jgph,gk->jghkp", t.reshape(nb, SSM_PACK, p_, h), eye).reshape(nb, SSM_PACK * h, SSM_PACK * p_)

    def out_blocks(t):
        return jnp.einsum("jghp,gk->jgpkh", t.reshape(nb, SSM_PACK, h, p_), eye).reshape(nb, SSM_PACK * p_, SSM_PACK * h)

    nm = "ssm%d" % l
    ur = bd_matmul(u, in_blocks(bbr), nm + "_bur")
    ui = bd_matmul(u, in_blocks(bbi), nm + "_bui")
    xr, xi = s5_scan(ur, ui, lbr.reshape(1, g * p_), lbi.reshape(1, g * p_), nm + "_scan")
    y = (bd_matmul(xr, out_blocks(sp["ssm_c_re", l]), nm + "_cr") + bd_matmul(xi, out_blocks(-sp["ssm_c_im", l]), nm + "_ci")
         + sp["ssm_d", l].reshape(1, g * h) * u)
    y = jax.nn.gelu(y)
    zz = matmul_w(y, w16["ssm_w_glu", l], toks["ssm_w_glu", l], nm + "_glu")
    z_out, z_gate = split_cols(zz, _contiguous((D_MODEL, D_MODEL)))
    return z_out * jax.nn.sigmoid(z_gate)


def _mla_mixer(q_lat, kv_lat, k_rope, cos, sin, sp, l, w16, toks):
    s = q_lat.shape[0]
    nm = "mla%d" % l
    dqk = MLA_NOPE + MLA_ROPE
    qt = matmul_w_t(rmsnorm(q_lat, sp["mla_q_norm", l], nm + "_qn"), w16["mla_w_uq", l], toks["mla_w_uq", l],
                    nm + "_uq").reshape(MLA_HEADS, dqk, s)
    half = MLA_ROPE // 2
    q1, q2 = qt[:, MLA_NOPE:MLA_NOPE + half], qt[:, MLA_NOPE + half:]
    ct, st = cos.T[None], sin.T[None]
    qt = jnp.concatenate([qt[:, :MLA_NOPE], q1 * ct - q2 * st, q2 * ct + q1 * st], axis=1).reshape(MLA_HEADS * dqk, s)
    kvt = matmul_w_t(rmsnorm(kv_lat, sp["mla_kv_norm", l], nm + "_kvn"), w16["mla_w_ukv", l], toks["mla_w_ukv", l],
                     nm + "_ukv").reshape(MLA_HEADS, MLA_NOPE + MLA_V, s)
    k_pe = _rope(k_rope, cos, sin)
    k = jnp.concatenate([kvt[:, :MLA_NOPE].transpose(0, 2, 1),
                         jnp.broadcast_to(k_pe[None], (MLA_HEADS, s, MLA_ROPE))], axis=-1)
    vt = kvt[:, MLA_NOPE:].reshape(MLA_HEADS * MLA_V, s)
    ot = attention(qt, k, vt, 1.0 / math.sqrt(dqk), True, nm + "_att")
    return matmul_w_at(ot, w16["mla_w_o", l], toks["mla_w_o", l], nm + "_o")


def _hgrn_mixer(q, f_logit, i_in, g, lb, sp, l, w16, toks):
    s = q.shape[0]
    nm = "hg%d" % l
    f = lb + (1.0 - lb) * jax.nn.sigmoid(f_logit)
    o = hgrn_core(jax.nn.silu(q), 1.0 - f, i_in, jnp.log(f), nm + "_core")
    o = rmsnorm(o.reshape(s * HG_HEADS, HG_DV), sp["hg_g_norm", l], nm + "_gn").reshape(s, HG_HEADS * HG_DV)
    return silu_gated_out(o, g, w16["hg_w_o", l], toks["hg_w_o", l], nm + "_o")


def _cross_attention(x, h, mem_n, l, w16, toks):
    s, m = h.shape[0], mem_n.shape[0]
    nm = "xat%d" % l
    xw = X_HEADS * X_HEAD_DIM
    qt = matmul_w_t(h, w16["x_w_q", l], toks["x_w_q", l], nm + "_q")
    kv = matmul_w(mem_n, w16["x_w_kv", l], toks["x_w_kv", l], nm + "_kv")
    k, v = split_cols(kv, _contiguous((xw, xw)))
    k = k.reshape(m, X_HEADS, X_HEAD_DIM).transpose(1, 0, 2)
    ot = attention(qt, k, v.T, 1.0 / math.sqrt(X_HEAD_DIM), False, nm + "_att")
    return matmul_w_at_res(x, ot, w16["x_w_o", l], toks["x_w_o", l], nm + "_o")


def _in_spec():
    hw = HG_HEADS * HG_DK
    sizes = (SSM_WIDTH, MLA_Q_RANK, MLA_KV_RANK, MLA_ROPE, hw, hw, hw, hw, D_MODEL, D_MODEL, D_MODEL)
    pieces, lo = [], 0
    for n in sizes:
        segs, c = [], lo
        while c < lo + n:
            end = min(lo + n, (c // D_IN_SHARD + 1) * D_IN_SHARD)
            segs.append((c + (D_IN_BLOCK - D_IN_SHARD) * (c // D_IN_SHARD), end - c))
            c = end
        pieces.append(tuple(segs))
        lo += n
    assert lo == D_IN
    return tuple(pieces), N_CHIPS * D_IN_BLOCK


def _layer(x, mem, cos, sin, lb, l, sp, w16, toks):
    s = x.shape[0]
    nm = "l%d" % l
    h = rmsnorm(x, sp["norm_mix", l], nm + "_nmix")
    z = matmul_w(h, w16["w_in", l], toks["w_in", l], nm + "_in")
    u, q_lat, kv_lat, k_rope, hg_q, hg_f, hg_i, hg_g, l_ssm, l_mla, l_hg = split_cols(z, _in_spec())
    y_ssm = _s5_mixer(u, sp, l, w16, toks)
    y_mla = _mla_mixer(q_lat, kv_lat, k_rope, cos, sin, sp, l, w16, toks)
    y_hg = _hgrn_mixer(hg_q, hg_f, hg_i, hg_g, lb, sp, l, w16, toks)
    x = gated_merge_out(x, (y_ssm, y_mla, y_hg), (l_ssm, l_mla, l_hg), w16["w_out", l], toks["w_out", l], nm + "_out")
    hc = rmsnorm(x, sp["norm_cross", l], nm + "_ncross")
    mem_n = rmsnorm(mem, sp["norm_mem", l], nm + "_nmem")
    x = _cross_attention(x, hc, mem_n, l, w16, toks)
    hf = rmsnorm(x, sp["norm_ffn", l], nm + "_nffn")
    gt = matmul_w(hf, w16["ffn_w_gate", l], toks["ffn_w_gate", l], nm + "_gate")
    up = matmul_w(hf, w16["ffn_w_up", l], toks["ffn_w_up", l], nm + "_up")
    return swiglu_down(x, gt, up, w16["ffn_w_down", l], toks["ffn_w_down", l], nm + "_down")


def _lower_bounds(hg_lb):
    lb_p = jax.nn.softmax(hg_lb, axis=0)
    return jnp.cumsum(lb_p, axis=0) - lb_p[0:1]


def _loss_head(norm_final, x, target):
    y = rmsnorm(x, norm_final, "nfinal")
    return 0.5 * jnp.sum(jnp.mean(jnp.square(y - target), axis=-1))


def _forward_backward(sp, toks, x, w16, later, mem, pos, target):
    half = MLA_ROPE // 2
    inv_freq = ROPE_THETA ** (-jnp.arange(half, dtype=F32) / half)
    ang = pos.astype(F32)[:, None] * inv_freq
    cos, sin = jnp.cos(ang), jnp.sin(ang)
    lower, lower_vjp = jax.vjp(_lower_bounds, sp["hg_lb", ALL])
    token, wait_for_weights = later
    x = x + token[0, 0]
    vjps = []
    for l in range(DEPTH):
        if l == 1:
            w16 = wait_for_weights(w16, x)
        sp_l = {k: v for k, v in sp.items() if k[1] == l}
        toks_l = {k: v for k, v in toks.items() if k[1] == l}
        weights = w16
        x, layer_vjp = jax.vjp(lambda s, t, lb, xx, l=l, weights=weights: _layer(xx, mem, cos, sin, lb, l, s, weights, t),
                               sp_l, toks_l, lower[l], x)
        vjps.append(layer_vjp)
    loss, head_vjp = jax.vjp(lambda g, xx: _loss_head(g, xx, target), sp["norm_final", ALL], x)
    g_sp, g_tok, d_lower = {}, {}, [None] * DEPTH
    g_sp["norm_final", ALL], dx = head_vjp(jnp.ones((), F32))
    for l in reversed(range(DEPTH)):
        g_s, g_t, d_lower[l], dx = vjps[l](dx)
        g_sp.update(g_s)
        g_tok.update(g_t)
    (g_sp["hg_lb", ALL],) = lower_vjp(jnp.stack(d_lower))
    return loss, g_sp, g_tok, dx


WEIGHTS = ["norm_mix", "w_in", "ssm_lam_re", "ssm_lam_im", "ssm_b_re", "ssm_b_im", "ssm_c_re", "ssm_c_im", "ssm_d",
           "ssm_log_step", "ssm_w_glu", "mla_q_norm", "mla_kv_norm", "mla_w_uq", "mla_w_ukv", "mla_w_o", "hg_lb",
           "hg_g_norm", "hg_w_o", "w_out", "norm_cross", "norm_mem", "x_w_q", "x_w_kv", "x_w_o", "norm_ffn",
           "ffn_w_gate_up", "ffn_w_down", "norm_final"]
INPUTS = ["x", "mem", "positions"] + WEIGHTS + ["loss_target"] + ["m_" + n for n in WEIGHTS] + ["v_" + n for n in WEIGHTS]
BIG = [("w_in", 2), ("ssm_w_glu", 2), ("mla_w_uq", 2), ("mla_w_ukv", 2), ("mla_w_o", 2), ("hg_w_o", 2), ("w_out", 1),
       ("x_w_q", 1), ("x_w_kv", 1), ("x_w_o", 2), ("ffn_w_gate_up", 2), ("ffn_w_down", 1)]
SMALL = [n for n in WEIGHTS if n not in dict(BIG)]
WHOLE = ("hg_lb", "norm_final")
ALL = -1
SMALL_W = 128
BLOCKED = ("w_in", "ffn_w_gate_up")


def _pack(arrs, rows, width, dtype):
    flat = jnp.concatenate([a.astype(dtype).reshape(-1) for a in arrs])
    return jnp.pad(flat, (0, rows * width - flat.shape[0])).reshape(rows, width)


def _unpack(buf, like):
    flat = buf.reshape(-1)
    out, off = [], 0
    for a in like:
        n = math.prod(a.shape)
        out.append(flat[off:off + n].reshape(a.shape))
        off += n
    return out


def kernel(
    x, mem, positions, norm_mix, w_in, ssm_lam_re, ssm_lam_im, ssm_b_re, ssm_b_im, ssm_c_re, ssm_c_im, ssm_d,
    ssm_log_step, ssm_w_glu, mla_q_norm, mla_kv_norm, mla_w_uq, mla_w_ukv, mla_w_o, hg_lb, hg_g_norm, hg_w_o,
    w_out, norm_cross, norm_mem, x_w_q, x_w_kv, x_w_o, norm_ffn, ffn_w_gate_up, ffn_w_down, norm_final,
    loss_target, m_norm_mix, m_w_in, m_ssm_lam_re, m_ssm_lam_im, m_ssm_b_re, m_ssm_b_im, m_ssm_c_re, m_ssm_c_im,
    m_ssm_d, m_ssm_log_step, m_ssm_w_glu, m_mla_q_norm, m_mla_kv_norm, m_mla_w_uq, m_mla_w_ukv, m_mla_w_o,
    m_hg_lb, m_hg_g_norm, m_hg_w_o, m_w_out, m_norm_cross, m_norm_mem, m_x_w_q, m_x_w_kv, m_x_w_o, m_norm_ffn,
    m_ffn_w_gate_up, m_ffn_w_down, m_norm_final, v_norm_mix, v_w_in, v_ssm_lam_re, v_ssm_lam_im, v_ssm_b_re,
    v_ssm_b_im, v_ssm_c_re, v_ssm_c_im, v_ssm_d, v_ssm_log_step, v_ssm_w_glu, v_mla_q_norm, v_mla_kv_norm,
    v_mla_w_uq, v_mla_w_ukv, v_mla_w_o, v_hg_lb, v_hg_g_norm, v_hg_w_o, v_w_out, v_norm_cross, v_norm_mem,
    v_x_w_q, v_x_w_kv, v_x_w_o, v_norm_ffn, v_ffn_w_gate_up, v_ffn_w_down, v_norm_final):
    given = locals()
    a = {n: given[n] for n in INPUTS}
    x, mem, pos, target = a["x"][0], a["mem"][0], a["positions"][0], a["loss_target"][0]
    place = jnp.stack([lax.axis_index("c"), 2 * lax.axis_index("x") + lax.axis_index("y")]).astype(jnp.int32)

    def shard16(n):
        w = a[n].astype(BF16)
        if n == "w_in":
            w = jnp.pad(w, ((0, 0), (0, 0), (0, D_IN_BLOCK - D_IN_SHARD)))
        return w

    def use(blocks, l, w16):
        for (n, ax), got in zip(BIG, blocks):
            if n in BLOCKED:
                full = got
            elif ax == 2:
                full = jnp.concatenate([got[q] for q in range(N_CHIPS)], axis=1)
            else:
                full = got.reshape(N_CHIPS * got.shape[1], got.shape[2])
            if n == "ffn_w_gate_up":
                half = N_CHIPS // 2
                w16["ffn_w_gate", l], w16["ffn_w_up", l] = full[:half], full[half:]
            else:
                w16[n, l] = full
        return w16

    shards = [shard16(n) for n, _ in BIG]
    halves = [s[0].reshape(2, s.shape[1] // 2, s.shape[2]) for s in shards]
    w16 = use([g.reshape(N_CHIPS, s.shape[1], s.shape[2]) for g, s in zip(_gather_weights(halves), shards)], 0, {})
    pending = _gather_start([s[1] for s in shards], after=w16["w_in", 0])
    toks = {}
    for n, ax in BIG:
        rows, cols = a[n].shape[1], shards[BIG.index((n, ax))].shape[2]
        shape = (N_CHIPS, rows, cols) if n in BLOCKED else (rows, N_CHIPS * cols) if ax == 2 else (N_CHIPS * rows, cols)
        for l in range(DEPTH):
            if n == "ffn_w_gate_up":
                toks["ffn_w_gate", l] = jnp.zeros((N_CHIPS // 2,) + shape[1:], F32)
                toks["ffn_w_up", l] = jnp.zeros((N_CHIPS // 2,) + shape[1:], F32)
            else:
                toks[n, l] = jnp.zeros(shape, F32)

    def layer1_weights(w16, after):
        return use(_gather_wait(*pending[:4], after), 1, dict(w16))
    sp = {}
    for n in SMALL:
        if n in WHOLE:
            sp[n, ALL] = a[n]
        else:
            for l in range(DEPTH):
                sp[n, l] = a[n][l]

    loss, g_sp, g_tok, g_x = _forward_backward(sp, toks, x, w16, (pending[4], layer1_weights), mem, pos, target)
    g_small = {n: g_sp[n, ALL] if n in WHOLE else jnp.stack([g_sp[n, l] for l in range(DEPTH)]) for n in SMALL}

    grads, names = [], []
    for n, ax in BIG:
        for l in range(DEPTH):
            if n == "ffn_w_gate_up":
                g = jnp.concatenate([g_tok["ffn_w_gate", l], g_tok["ffn_w_up", l]], axis=0)
            else:
                g = g_tok[n, l]
            if n in BLOCKED:
                grads.append(g)
            elif ax == 2:
                grads.append(g.reshape(g.shape[0], N_CHIPS, g.shape[1] // N_CHIPS).transpose(1, 0, 2))
            else:
                grads.append(g.reshape(N_CHIPS, g.shape[0] // N_CHIPS, g.shape[1]))
            names.append("%s%d" % (n, l))
    from_sibling = _swap_halves_to_sibling(grads)
    chip_sums = [_add_half(g, t, place, "grad_add_half_" + n) for n, g, t in zip(names, grads, from_sibling)]
    from_chips = _scatter_partials(chip_sums)
    totals = [_add_partials(g, t1, t2, place, "grad_add_partials_" + n)
              for n, g, t1, t2 in zip(names, grads, from_sibling, from_chips)]
    joined = _join_halves(totals)
    g_big = {n: jnp.stack([joined[DEPTH * i + l][:, :a[n].shape[2]] for l in range(DEPTH)])
             for i, (n, _) in enumerate(BIG)}

    small_like = [a[n] for n in SMALL] + [jnp.zeros((1,), F32)]
    n_small = sum(math.prod(t.shape) for t in small_like)
    srows = -(-n_small // (SMALL_W * 8 * N_DEV)) * 8 * N_DEV
    reduced = _allreduce_small(_pack([g_small[n] for n in SMALL] + [loss.reshape(1)], srows, SMALL_W, F32))
    red = _unpack(reduced, small_like)
    g_sm = dict(zip(SMALL, red[:-1]))
    loss = red[-1].reshape(())

    out_g, out_d, out_m, out_v = {}, {}, {}, {}
    for n, _ in BIG:
        shp = a[n].shape
        two = lambda t: t.reshape(shp[0] * shp[1], shp[2])
        d, m, v = _adamw_call(two(a[n]), two(g_big[n]), two(a["m_" + n]), two(a["v_" + n]), "adamw_" + n)
        out_g[n], out_d[n], out_m[n], out_v[n] = g_big[n], d.reshape(shp), m.reshape(shp), v.reshape(shp)
    small_arrs = [a[n] for n in SMALL]
    d, m, v = _adamw_call(
        _pack(small_arrs, srows, SMALL_W, F32), reduced,
        _pack([a["m_" + n] for n in SMALL], srows, SMALL_W, F32),
        _pack([a["v_" + n] for n in SMALL], srows, SMALL_W, F32), "adamw_small")
    for n, gg, dd, mm_, vv in zip(SMALL, red[:-1], _unpack(d, small_arrs), _unpack(m, small_arrs), _unpack(v, small_arrs)):
        out_g[n], out_d[n], out_m[n], out_v[n] = gg, dd, mm_, vv

    return (loss, g_x[None], *[out_g[n] for n in WEIGHTS], *[out_d[n] for n in WEIGHTS],
            *[out_m[n] for n in WEIGHTS], *[out_v[n] for n in WEIGHTS])
```

```python
import functools
import math

import jax
import jax.numpy as jnp
from jax import lax
from jax.experimental import pallas as pl
from jax.experimental.pallas import tpu as pltpu

F32 = jnp.float32
BF16 = jnp.bfloat16
MESH = pl.DeviceIdType.MESH

D_MODEL = 1024
DEPTH = 2
RMS_EPS = 1e-6
SSM_GROUPS, SSM_GROUP_CH, SSM_STATE = 32, 16, 64
SSM_WIDTH = SSM_GROUPS * SSM_GROUP_CH
SSM_LANES = SSM_GROUPS * SSM_STATE
SSM_PACK = 8
MLA_HEADS, MLA_Q_RANK, MLA_KV_RANK, MLA_NOPE, MLA_ROPE, MLA_V = 8, 512, 256, 64, 32, 64
ROPE_THETA = 10000.0
HG_HEADS, HG_DK, HG_DV, HG_CHUNK = 4, 128, 128, 64
HG_SUB_FWD, HG_SUB_BWD = 32, 16
X_HEADS, X_HEAD_DIM = 4, 128
D_FF = 2816
D_IN = 6432
N_CHIPS = 4
D_IN_SHARD = D_IN // N_CHIPS
D_IN_BLOCK = 1664
MASK_VALUE = -1e30

ADAM_LR, ADAM_B1, ADAM_B2, ADAM_EPS, ADAM_WD, ADAM_STEP = 0.001, 0.9, 0.999, 1e-08, 0.01, 10

VMEM_LIMIT = 48 * 1024 * 1024


def _pcall(body, **kw):
    return pl.pallas_call(body, **kw)


def _params(sem):
    return pltpu.CompilerParams(dimension_semantics=sem, vmem_limit_bytes=VMEM_LIMIT)


def _tile(n, cap, align=128):
    if n <= cap:
        return n
    t = (cap // align) * align
    while t >= align:
        if n % t == 0:
            return t
        t -= align
    return n


MM_VMEM_BUDGET = 36 * 1024 * 1024
MM_TILE_CAP = 2048


def _divisors(n, cap, align=128):
    out = [n] if n <= cap else []
    out += [t for t in range(align, min(n, cap + 1), align) if n % t == 0]
    return sorted(set(out), reverse=True)


@functools.lru_cache(maxsize=None)
def _mm_tiles(m, n, k, a_bytes, b_bytes, o_bytes, n_unit, k_unit):
    best = None
    for tm in _divisors(m, MM_TILE_CAP):
        for tn in _divisors(n_unit, MM_TILE_CAP):
            for tk in _divisors(k_unit, MM_TILE_CAP):
                nk = k // tk
                need = 2 * (tm * tk * a_bytes + tk * tn * b_bytes + tm * tn * o_bytes)
                need += tm * tn * 4 * (2 if nk > 1 else 1)
                need += (tm * tk * 2 if a_bytes > 2 else 0) + (tk * tn * 2 if b_bytes > 2 else 0)
                if need > MM_VMEM_BUDGET:
                    continue
                key = ((m // tm) * (n // tn) * nk, nk, -tn)
                if best is None or key < best[0]:
                    best = (key, (tm, tn, tk))
    assert best is not None, (m, n, k)
    return best[1]


def _mm(a, b, *, ta=False, tb=False, out_dtype=F32, name="mm", b_blocks=False, out_blocks=0, pro=None, epi=None,
        save_left=False):
    if ta:
        K, M = a.shape
    else:
        M, K = a.shape
    n_unit = k_unit = None
    if b_blocks:
        nb, rows, cols = b.shape
        if tb:
            N, Kb, k_unit = rows, nb * cols, cols
        else:
            Kb, N, n_unit = rows, nb * cols, cols
    elif tb:
        N, Kb = b.shape
    else:
        Kb, N = b.shape
    if out_blocks:
        assert not b_blocks and N % out_blocks == 0
        n_unit = N // out_blocks
    assert K == Kb, (a.shape, b.shape, ta, tb)
    n_pro = len(pro) - 1 if pro else 0
    n_epi = len(epi) - 1 if epi else 0
    tm, tn, tk = _mm_tiles(M, N, K, a.dtype.itemsize * (1 + n_pro), b.dtype.itemsize,
                           jnp.dtype(out_dtype).itemsize * max(1, 2 * n_epi), n_unit or N, k_unit or K)
    nk = K // tk
    dims = (((0,) if ta else (1,), (1,) if tb else (0,)), ((), ()))
    n_in = 2 + n_pro + n_epi
    n_out = max(1, n_epi) + (1 if save_left else 0)
    assert not save_left or (pro and tn == N)

    def body(*refs):
        ins, outs, acc = refs[:n_in], refs[n_in:n_in + n_out], refs[n_in + n_out:]
        b_ref = ins[1 + n_pro]
        av = (pro[0](*[r[...] for r in ins[:1 + n_pro]]) if pro else ins[0][...]).astype(BF16)
        if save_left:
            outs[-1][...] = av
        bv = b_ref[...].astype(BF16)
        part = lax.dot_general(av, bv, dims, preferred_element_type=F32)

        def finish(total):
            if epi:
                for o_ref, r in zip(outs, epi[0](total, *[e[...] for e in ins[2 + n_pro:]])):
                    o_ref[...] = r.astype(o_ref.dtype)
            else:
                outs[0][...] = total.astype(outs[0].dtype)

        if nk == 1:
            finish(part)
        else:
            acc_ref = acc[0]
            k = pl.program_id(2)

            @pl.when(k == 0)
            def _():
                acc_ref[...] = part

            @pl.when(k > 0)
            def _():
                acc_ref[...] += part

            @pl.when(k == nk - 1)
            def _():
                finish(acc_ref[...])

    a_spec = pl.BlockSpec((tk, tm), lambda i, j, k: (k, i)) if ta else pl.BlockSpec((tm, tk), lambda i, j, k: (i, k))
    if b_blocks and tb:
        r = k_unit // tk
        b_spec = pl.BlockSpec((None, tn, tk), lambda i, j, k: (k // r, j, k % r))
    elif b_blocks:
        r = n_unit // tn
        b_spec = pl.BlockSpec((None, tk, tn), lambda i, j, k: (j // r, k, j % r))
    elif tb:
        b_spec = pl.BlockSpec((tn, tk), lambda i, j, k: (j, k))
    else:
        b_spec = pl.BlockSpec((tk, tn), lambda i, j, k: (k, j))
    if out_blocks:
        r = n_unit // tn
        out_spec = pl.BlockSpec((None, tm, tn), lambda i, j, k: (j // r, i, j % r))
        out_shape = jax.ShapeDtypeStruct((out_blocks, M, n_unit), out_dtype)
    else:
        out_spec = pl.BlockSpec((tm, tn), lambda i, j, k: (i, j))
        out_shape = jax.ShapeDtypeStruct((M, N), out_dtype)
    in_specs = [a_spec] * (1 + n_pro) + [b_spec] + [out_spec] * n_epi
    args = [a] + (list(pro[1:]) if pro else []) + [b] + (list(epi[1:]) if epi else [])
    out_specs, out_shapes = ([out_spec] * n_epi, [out_shape] * n_epi) if epi else (out_spec, out_shape)
    if save_left:
        out_specs = [out_spec] * max(1, n_epi) + [a_spec]
        out_shapes = [out_shape] * max(1, n_epi) + [jax.ShapeDtypeStruct(a.shape, BF16)]
    return _pcall(
        body,
        name=name,
        grid=(M // tm, N // tn, nk),
        in_specs=in_specs,
        out_specs=out_specs,
        out_shape=out_shapes,
        scratch_shapes=[pltpu.VMEM((tm, tn), F32)] if nk > 1 else [],
        compiler_params=_params(("parallel", "parallel", "arbitrary")),
    )(*args)


def _bd_call(a, bc, transpose, name):
    m = a.shape[0]
    nb, ka, kb = bc.shape
    win, wout = (kb, ka) if transpose else (ka, kb)
    assert a.shape[1] == nb * win
    tm = _tile(m, 1024)
    dims = _NT if transpose else (((1,), (0,)), ((), ()))

    def body(a_ref, b_ref, o_ref):
        o_ref[...] = lax.dot_general(a_ref[...].astype(BF16), b_ref[0].astype(BF16), dims, preferred_element_type=F32)

    return _pcall(
        body,
        name=name,
        grid=(m // tm, nb),
        in_specs=[pl.BlockSpec((tm, win), lambda i, j: (i, j)), pl.BlockSpec((1, ka, kb), lambda i, j: (j, 0, 0))],
        out_specs=pl.BlockSpec((tm, wout), lambda i, j: (i, j)),
        out_shape=jax.ShapeDtypeStruct((m, nb * wout), F32),
        compiler_params=_params(("parallel", "parallel")),
    )(a, bc)


def _bd_dw_call(a, g, nb, name):
    m = a.shape[0]
    ka, kb = a.shape[1] // nb, g.shape[1] // nb
    tk = _tile(m, 1024)
    nk = m // tk

    def body(a_ref, g_ref, o_ref):
        part = lax.dot_general(a_ref[...].astype(BF16), g_ref[...].astype(BF16), _TN, preferred_element_type=F32)
        k = pl.program_id(1)

        @pl.when(k == 0)
        def _():
            o_ref[0] = part

        @pl.when(k > 0)
        def _():
            o_ref[0] += part

    return _pcall(
        body,
        name=name,
        grid=(nb, nk),
        in_specs=[pl.BlockSpec((tk, ka), lambda j, k: (k, j)), pl.BlockSpec((tk, kb), lambda j, k: (k, j))],
        out_specs=pl.BlockSpec((1, ka, kb), lambda j, k: (j, 0, 0)),
        out_shape=jax.ShapeDtypeStruct((nb, ka, kb), F32),
        compiler_params=_params(("parallel", "arbitrary")),
    )(a, g)


@functools.partial(jax.custom_vjp, nondiff_argnums=(2,))
def bd_matmul(a, bc, name):
    return _bd_call(a, bc, False, name + "_f")


def _bd_matmul_fwd(a, bc, name):
    return _bd_call(a, bc, False, name + "_f"), (a, bc)


def _bd_matmul_bwd(name, res, g):
    a, bc = res
    return _bd_call(g, bc, True, name + "_da"), _bd_dw_call(a, g, bc.shape[0], name + "_dw")


bd_matmul.defvjp(_bd_matmul_fwd, _bd_matmul_bwd)


@functools.partial(jax.custom_vjp, nondiff_argnums=(3,))
def matmul_w(a, w16, tok, name):
    return _mm(a, w16, b_blocks=w16.ndim == 3, name=name + "_f")


def _matmul_w_fwd(a, w16, tok, name):
    return _mm(a, w16, b_blocks=w16.ndim == 3, name=name + "_f"), (a, w16)


def _matmul_w_bwd(name, res, g):
    a, w16 = res
    blocks = w16.shape[0] if w16.ndim == 3 else 0
    return (_mm(g, w16, tb=True, b_blocks=bool(blocks), name=name + "_da"), jnp.zeros_like(w16),
            _mm(a, g, ta=True, out_blocks=blocks, name=name + "_dw"))


matmul_w.defvjp(_matmul_w_fwd, _matmul_w_bwd)


def _swiglu(gt, up):
    return gt * jax.nn.sigmoid(gt) * up


def _swiglu_bwd(dact, gt, up):
    sg = jax.nn.sigmoid(gt)
    return dact * up * sg * (1.0 + gt * (1.0 - sg)), dact * gt * sg


def _add_residual(product, x):
    return (product + x,)


def _silu_gate(o, g):
    return o * (g * jax.nn.sigmoid(g))


def _silu_gate_bwd(dm, o, g):
    sg = jax.nn.sigmoid(g)
    return dm * g * sg, dm * o * sg * (1.0 + g * (1.0 - sg))


@functools.partial(jax.custom_vjp, nondiff_argnums=(4,))
def silu_gated_out(o, g, w16, tok, name):
    return _mm(o, w16, pro=(_silu_gate, g), name=name + "_f")


def _silu_gated_out_fwd(o, g, w16, tok, name):
    out, left16 = _mm(o, w16, pro=(_silu_gate, g), save_left=True, name=name + "_f")
    return out, (o, g, w16, left16)


def _silu_gated_out_bwd(name, res, d):
    o, g, w16, left16 = res
    d_o, d_g = _mm(d, w16, tb=True, epi=(_silu_gate_bwd, o, g), name=name + "_da")
    return d_o, d_g, jnp.zeros_like(w16), _mm(left16, d, ta=True, name=name + "_dw")


silu_gated_out.defvjp(_silu_gated_out_fwd, _silu_gated_out_bwd)


@functools.partial(jax.custom_vjp, nondiff_argnums=(5,))
def swiglu_down(x, gt, up, w16, tok, name):
    return _mm(gt, w16, pro=(_swiglu, up), epi=(_add_residual, x), name=name + "_f")[0]


def _swiglu_down_fwd(x, gt, up, w16, tok, name):
    out, act16 = _mm(gt, w16, pro=(_swiglu, up), epi=(_add_residual, x), save_left=True, name=name + "_f")
    return out, (gt, up, w16, act16)


def _swiglu_down_bwd(name, res, g):
    gt, up, w16, act16 = res
    d_gt, d_up = _mm(g, w16, tb=True, epi=(_swiglu_bwd, gt, up), name=name + "_da")
    return g, d_gt, d_up, jnp.zeros_like(w16), _mm(act16, g, ta=True, name=name + "_dw")


swiglu_down.defvjp(_swiglu_down_fwd, _swiglu_down_bwd)


def _merge(y1, y2, y3, l1, l2, l3):
    return jax.nn.sigmoid(l1) * y1 + jax.nn.sigmoid(l2) * y2 + jax.nn.sigmoid(l3) * y3


def _merge_bwd(dm, y1, y2, y3, l1, l2, l3):
    dys, dls = [], []
    for y, logit in ((y1, l1), (y2, l2), (y3, l3)):
        s = jax.nn.sigmoid(logit)
        dys.append(dm * s)
        dls.append(dm * y * s * (1.0 - s))
    return (*dys, *dls)


@functools.partial(jax.custom_vjp, nondiff_argnums=(5,))
def gated_merge_out(x, ys, logits, w16, tok, name):
    return _mm(ys[0], w16, pro=(_merge, *ys[1:], *logits), epi=(_add_residual, x), name=name + "_f")[0]


def _gated_merge_out_fwd(x, ys, logits, w16, tok, name):
    out, merged16 = _mm(ys[0], w16, pro=(_merge, *ys[1:], *logits), epi=(_add_residual, x), save_left=True,
                        name=name + "_f")
    return out, (ys, logits, w16, merged16)


def _gated_merge_out_bwd(name, res, g):
    ys, logits, w16, merged16 = res
    d = _mm(g, w16, tb=True, epi=(_merge_bwd, *ys, *logits), name=name + "_da")
    return g, tuple(d[:3]), tuple(d[3:]), jnp.zeros_like(w16), _mm(merged16, g, ta=True, name=name + "_dw")


gated_merge_out.defvjp(_gated_merge_out_fwd, _gated_merge_out_bwd)


@functools.partial(jax.custom_vjp, nondiff_argnums=(3,))
def matmul_w_t(a, w16, tok, name):
    return _mm(w16, a, ta=True, tb=True, name=name + "_f")


def _matmul_w_t_fwd(a, w16, tok, name):
    return _mm(w16, a, ta=True, tb=True, name=name + "_f"), (a, w16)


def _matmul_w_t_bwd(name, res, g):
    a, w16 = res
    return (_mm(g, w16, ta=True, tb=True, name=name + "_da"), jnp.zeros_like(w16),
            _mm(a, g, ta=True, tb=True, name=name + "_dw"))


matmul_w_t.defvjp(_matmul_w_t_fwd, _matmul_w_t_bwd)


@functools.partial(jax.custom_vjp, nondiff_argnums=(3,))
def matmul_w_at(at, w16, tok, name):
    return _mm(at, w16, ta=True, name=name + "_f")


def _matmul_w_at_fwd(at, w16, tok, name):
    return _mm(at, w16, ta=True, name=name + "_f"), (at, w16)


def _matmul_w_at_bwd(name, res, g):
    at, w16 = res
    return _mm(w16, g, tb=True, name=name + "_da"), jnp.zeros_like(w16), _mm(at, g, name=name + "_dw")


matmul_w_at.defvjp(_matmul_w_at_fwd, _matmul_w_at_bwd)


@functools.partial(jax.custom_vjp, nondiff_argnums=(4,))
def matmul_w_at_res(x, at, w16, tok, name):
    return _mm(at, w16, ta=True, epi=(_add_residual, x), name=name + "_f")[0]


def _matmul_w_at_res_fwd(x, at, w16, tok, name):
    return _mm(at, w16, ta=True, epi=(_add_residual, x), name=name + "_f")[0], (at, w16)


def _matmul_w_at_res_bwd(name, res, g):
    return (g, *_matmul_w_at_bwd(name, res, g))


matmul_w_at_res.defvjp(_matmul_w_at_res_fwd, _matmul_w_at_res_bwd)


def _rms_fwd_call(x, g, name):
    rows, d = x.shape
    tr = _tile(rows, 512, 8)

    def body(x_ref, g_ref, y_ref):
        xv = x_ref[...]
        rstd = lax.rsqrt(jnp.mean(xv * xv, axis=-1, keepdims=True) + RMS_EPS)
        y_ref[...] = xv * rstd * g_ref[...]

    return _pcall(
        body,
        name=name,
        grid=(rows // tr,),
        in_specs=[pl.BlockSpec((tr, d), lambda i: (i, 0)), pl.BlockSpec((1, d), lambda i: (0, 0))],
        out_specs=pl.BlockSpec((tr, d), lambda i: (i, 0)),
        out_shape=jax.ShapeDtypeStruct((rows, d), F32),
        compiler_params=_params(("parallel",)),
    )(x, g.reshape(1, d))


def _rms_bwd_call(x, g, dy, name):
    rows, d = x.shape
    tr = _tile(rows, 512, 8)
    nb = rows // tr

    def body(x_ref, g_ref, dy_ref, dx_ref, dg_ref):
        xv = x_ref[...]
        rstd = lax.rsqrt(jnp.mean(xv * xv, axis=-1, keepdims=True) + RMS_EPS)
        xh = xv * rstd
        dyv = dy_ref[...]
        dyg = dyv * g_ref[...]
        dx_ref[...] = rstd * (dyg - xh * jnp.mean(dyg * xh, axis=-1, keepdims=True))
        dg_ref[...] = jnp.sum((dyv * xh).reshape(tr // 8, 8, d), axis=0)

    dx, dgp = _pcall(
        body,
        name=name,
        grid=(nb,),
        in_specs=[
            pl.BlockSpec((tr, d), lambda i: (i, 0)),
            pl.BlockSpec((1, d), lambda i: (0, 0)),
            pl.BlockSpec((tr, d), lambda i: (i, 0)),
        ],
        out_specs=[pl.BlockSpec((tr, d), lambda i: (i, 0)), pl.BlockSpec((8, d), lambda i: (i, 0))],
        out_shape=[jax.ShapeDtypeStruct((rows, d), F32), jax.ShapeDtypeStruct((nb * 8, d), F32)],
        compiler_params=_params(("parallel",)),
    )(x, g.reshape(1, d), dy)
    return dx, jnp.sum(dgp, axis=0)


@functools.partial(jax.custom_vjp, nondiff_argnums=(2,))
def rmsnorm(x, g, name):
    return _rms_fwd_call(x, g, name + "_f")


def _rmsnorm_fwd(x, g, name):
    return _rms_fwd_call(x, g, name + "_f"), (x, g)


def _rmsnorm_bwd(name, res, dy):
    x, g = res
    return _rms_bwd_call(x, g, dy, name + "_b")


rmsnorm.defvjp(_rmsnorm_fwd, _rmsnorm_bwd)


def _attn_blocks(sq, sk):
    return _tile(sq, 1024), _tile(sk, 512)


def _attn_pt(qs, k_ref, qi, ki, bq, bk, masked, shift):
    st = jnp.dot(k_ref[0].astype(BF16), qs, preferred_element_type=F32)
    if masked:
        kpos = ki * bk + lax.broadcasted_iota(jnp.int32, (bk, bq), 0)
        qpos = qi * bq + lax.broadcasted_iota(jnp.int32, (bk, bq), 1)
        st = jnp.where(kpos <= qpos, st, MASK_VALUE)
    return st if shift is None else jnp.exp(st - shift)


def _attn_pairs(nq, nk, bq, bk, causal, k_major):
    ok = lambda i, j: (not causal) or j * bk <= i * bq + bq - 1
    if k_major:
        pairs = [(i, j) for j in range(nk) for i in range(nq) if ok(i, j)]
    else:
        pairs = [(i, j) for i in range(nq) for j in range(nk) if ok(i, j)]
    qtab = jnp.asarray([p[0] for p in pairs], jnp.int32)
    ktab = jnp.asarray([p[1] for p in pairs], jnp.int32)
    first_q = (lambda j: (j * bk) // bq) if causal else (lambda j: 0)
    last_k = (lambda i: jnp.minimum((i * bq + bq - 1) // bk, nk - 1)) if causal else (lambda i: nk - 1)
    return qtab, ktab, first_q, last_k


def _attn_cases(causal, qi, ki, bq, bk, step):
    if not causal:
        step(False)
        return
    pl.when(ki * bk + bk - 1 <= qi * bq)(functools.partial(step, False))
    pl.when(ki * bk + bk - 1 > qi * bq)(functools.partial(step, True))


def _attn_fwd_call(qt, k, vt, scale, causal, name):
    h, sk, dq = k.shape
    sq = qt.shape[1]
    dv = vt.shape[0] // h
    bq, bk = _attn_blocks(sq, sk)
    nq, nk = sq // bq, sk // bk

    qtab, ktab, _, last_k = _attn_pairs(nq, nk, bq, bk, causal, False)

    def body(qi_ref, ki_ref, q_ref, k_ref, v_ref, o_ref, lse_ref, m_sc, l_sc, acc_sc):
        qi, ki = qi_ref[pl.program_id(1)], ki_ref[pl.program_id(1)]

        @pl.when(ki == 0)
        def _():
            m_sc[...] = jnp.full(m_sc.shape, -jnp.inf, F32)
            l_sc[...] = jnp.zeros(l_sc.shape, F32)
            acc_sc[...] = jnp.zeros(acc_sc.shape, F32)

        def step(masked):
            qs = (q_ref[...] * scale).astype(BF16)
            st = _attn_pt(qs, k_ref, qi, ki, bq, bk, masked, None)
            m_prev = m_sc[...]
            m_new = jnp.maximum(m_prev, jnp.max(st, axis=0, keepdims=True))
            alpha = jnp.exp(m_prev - m_new)
            pt = jnp.exp(st - m_new)
            l_sc[...] = alpha * l_sc[...] + jnp.sum(pt, axis=0, keepdims=True)
            acc_sc[...] = alpha * acc_sc[...] + jnp.dot(v_ref[...].astype(BF16), pt.astype(BF16),
                                                        preferred_element_type=F32)
            m_sc[...] = m_new

        _attn_cases(causal, qi, ki, bq, bk, step)

        @pl.when(ki == last_k(qi))
        def _():
            o_ref[...] = acc_sc[...] / l_sc[...]
            lse_ref[0] = m_sc[...] + jnp.log(l_sc[...])

    qmap = lambda hh, p, qi_ref, ki_ref: (hh, qi_ref[p])
    return _pcall(
        body,
        name=name,
        grid_spec=pltpu.PrefetchScalarGridSpec(
            num_scalar_prefetch=2,
            grid=(h, qtab.shape[0]),
            in_specs=[
                pl.BlockSpec((dq, bq), qmap),
                pl.BlockSpec((1, bk, dq), lambda hh, p, qi_ref, ki_ref: (hh, ki_ref[p], 0)),
                pl.BlockSpec((dv, bk), lambda hh, p, qi_ref, ki_ref: (hh, ki_ref[p])),
            ],
            out_specs=[pl.BlockSpec((dv, bq), qmap),
                       pl.BlockSpec((1, 1, bq), lambda hh, p, qi_ref, ki_ref: (hh, 0, qi_ref[p]))],
            scratch_shapes=[pltpu.VMEM((1, bq), F32), pltpu.VMEM((1, bq), F32), pltpu.VMEM((dv, bq), F32)],
        ),
        out_shape=[jax.ShapeDtypeStruct((h * dv, sq), F32), jax.ShapeDtypeStruct((h, 1, sq), F32)],
        compiler_params=_params(("parallel", "arbitrary")),
    )(qtab, ktab, qt, k, vt)


_TN = (((0,), (0,)), ((), ()))
_NT = (((1,), (1,)), ((), ()))


def _attn_dst(qs, k_ref, v_ref, do_ref, lse_ref, dl_ref, qi, ki, bq, bk, masked):
    pt = _attn_pt(qs, k_ref, qi, ki, bq, bk, masked, lse_ref[0])
    dpt = lax.dot_general(v_ref[...].astype(BF16), do_ref[...].astype(BF16), _TN, preferred_element_type=F32)
    return pt, pt * (dpt - dl_ref[0])


def _attn_dq_call(qt, k, vt, dot, lse, delta, scale, causal, name):
    h, sk, dq = k.shape
    sq = qt.shape[1]
    dv = vt.shape[0] // h
    bq, bk = _attn_blocks(sq, sk)
    nq, nk = sq // bq, sk // bk

    qtab, ktab, _, last_k = _attn_pairs(nq, nk, bq, bk, causal, False)

    def body(qi_ref, ki_ref, q_ref, k_ref, v_ref, do_ref, lse_ref, dl_ref, dq_ref, acc_sc):
        qi, ki = qi_ref[pl.program_id(1)], ki_ref[pl.program_id(1)]

        @pl.when(ki == 0)
        def _():
            acc_sc[...] = jnp.zeros(acc_sc.shape, F32)

        def step(masked):
            qs = (q_ref[...] * scale).astype(BF16)
            _, dst = _attn_dst(qs, k_ref, v_ref, do_ref, lse_ref, dl_ref, qi, ki, bq, bk, masked)
            acc_sc[...] += lax.dot_general(k_ref[0].astype(BF16), dst.astype(BF16), _TN, preferred_element_type=F32)

        _attn_cases(causal, qi, ki, bq, bk, step)

        @pl.when(ki == last_k(qi))
        def _():
            dq_ref[...] = acc_sc[...] * scale

    qmap = lambda hh, p, qi_ref, ki_ref: (hh, qi_ref[p])
    rowmap = lambda hh, p, qi_ref, ki_ref: (hh, 0, qi_ref[p])
    return _pcall(
        body,
        name=name,
        grid_spec=pltpu.PrefetchScalarGridSpec(
            num_scalar_prefetch=2,
            grid=(h, qtab.shape[0]),
            in_specs=[
                pl.BlockSpec((dq, bq), qmap),
                pl.BlockSpec((1, bk, dq), lambda hh, p, qi_ref, ki_ref: (hh, ki_ref[p], 0)),
                pl.BlockSpec((dv, bk), lambda hh, p, qi_ref, ki_ref: (hh, ki_ref[p])),
                pl.BlockSpec((dv, bq), qmap),
                pl.BlockSpec((1, 1, bq), rowmap),
                pl.BlockSpec((1, 1, bq), rowmap),
            ],
            out_specs=pl.BlockSpec((dq, bq), qmap),
            scratch_shapes=[pltpu.VMEM((dq, bq), F32)],
        ),
        out_shape=jax.ShapeDtypeStruct((h * dq, sq), F32),
        compiler_params=_params(("parallel", "arbitrary")),
    )(qtab, ktab, qt, k, vt, dot, lse, delta)


def _attn_dkv_call(qt, k, vt, dot, lse, delta, scale, causal, name):
    h, sk, dq = k.shape
    sq = qt.shape[1]
    dv = vt.shape[0] // h
    bq, bk = _attn_blocks(sq, sk)
    nq, nk = sq // bq, sk // bk

    qtab, ktab, first_q, _ = _attn_pairs(nq, nk, bq, bk, causal, True)

    def body(qi_ref, ki_ref, q_ref, k_ref, v_ref, do_ref, lse_ref, dl_ref, dk_ref, dv_ref, dk_sc, dv_sc):
        qi, ki = qi_ref[pl.program_id(1)], ki_ref[pl.program_id(1)]

        @pl.when(qi == first_q(ki))
        def _():
            dk_sc[...] = jnp.zeros(dk_sc.shape, F32)
            dv_sc[...] = jnp.zeros(dv_sc.shape, F32)

        def step(masked):
            qs = (q_ref[...] * scale).astype(BF16)
            pt, dst = _attn_dst(qs, k_ref, v_ref, do_ref, lse_ref, dl_ref, qi, ki, bq, bk, masked)
            dv_sc[...] += lax.dot_general(do_ref[...].astype(BF16), pt.astype(BF16), _NT, preferred_element_type=F32)
            dk_sc[...] += lax.dot_general(dst.astype(BF16), qs, _NT, preferred_element_type=F32)

        _attn_cases(causal, qi, ki, bq, bk, step)

        @pl.when(qi == nq - 1)
        def _():
            dk_ref[0] = dk_sc[...]
            dv_ref[...] = dv_sc[...]

    qmap = lambda hh, p, qi_ref, ki_ref: (hh, qi_ref[p])
    rowmap = lambda hh, p, qi_ref, ki_ref: (hh, 0, qi_ref[p])
    kmap = lambda hh, p, qi_ref, ki_ref: (hh, ki_ref[p], 0)
    vmap = lambda hh, p, qi_ref, ki_ref: (hh, ki_ref[p])
    return _pcall(
        body,
        name=name,
        grid_spec=pltpu.PrefetchScalarGridSpec(
            num_scalar_prefetch=2,
            grid=(h, qtab.shape[0]),
            in_specs=[
                pl.BlockSpec((dq, bq), qmap),
                pl.BlockSpec((1, bk, dq), kmap),
                pl.BlockSpec((dv, bk), vmap),
                pl.BlockSpec((dv, bq), qmap),
                pl.BlockSpec((1, 1, bq), rowmap),
                pl.BlockSpec((1, 1, bq), rowmap),
            ],
            out_specs=[pl.BlockSpec((1, bk, dq), kmap), pl.BlockSpec((dv, bk), vmap)],
            scratch_shapes=[pltpu.VMEM((bk, dq), F32), pltpu.VMEM((dv, bk), F32)],
        ),
        out_shape=[jax.ShapeDtypeStruct((h, sk, dq), F32), jax.ShapeDtypeStruct((h * dv, sk), F32)],
        compiler_params=_params(("parallel", "arbitrary")),
    )(qtab, ktab, qt, k, vt, dot, lse, delta)


@functools.partial(jax.custom_vjp, nondiff_argnums=(3, 4, 5))
def attention(qt, k, vt, scale, causal, name):
    return _attn_fwd_call(qt, k, vt, scale, causal, name + "_f")[0]


def _attention_fwd(qt, k, vt, scale, causal, name):
    ot, lse = _attn_fwd_call(qt, k, vt, scale, causal, name + "_f")
    return ot, (qt, k, vt, ot, lse)


def _attention_bwd(scale, causal, name, res, dot):
    qt, k, vt, ot, lse = res
    h = k.shape[0]
    delta = jnp.sum((dot * ot).reshape(h, ot.shape[0] // h, ot.shape[1]), axis=1, keepdims=True)
    dqt = _attn_dq_call(qt, k, vt, dot, lse, delta, scale, causal, name + "_dq")
    dk, dvt = _attn_dkv_call(qt, k, vt, dot, lse, delta, scale, causal, name + "_dkv")
    return dqt, dk, dvt


attention.defvjp(_attention_fwd, _attention_bwd)


SCAN_ROWS = 8
SCAN_LANES = 256
SCAN_TBLOCK = 512
SCAN_UNROLL = 4


def _scan_tables(lr, li):
    pr, pi = [lr], [li]
    for _ in range(SCAN_ROWS - 1):
        pr, pi = pr + [pr[-1] * lr - pi[-1] * li], pi + [pr[-1] * li + pi[-1] * lr]
    return jnp.concatenate(pr, axis=0), jnp.concatenate(pi, axis=0)


def _scan_call(ur, ui, lr, li, reverse, name, states=None):
    s, n = ur.shape
    tb = _tile(s, SCAN_TBLOCK, 8)
    nt, nl = s // tb, n // SCAN_LANES
    ntile = tb // SCAN_ROWS
    pr, pi = _scan_tables(lr, li)
    if reverse:
        pr, pi = pr[::-1], pi[::-1]
    shape = (SCAN_ROWS, SCAN_LANES)
    with_states = states is not None
    assert reverse or not with_states

    def body(*refs):
        if with_states:
            ur_ref, ui_ref, pr_ref, pi_ref, sr_ref, si_ref, xr_ref, xi_ref, dr_ref, di_ref, cr_sc, ci_sc, ar_sc, ai_sc = refs
        else:
            ur_ref, ui_ref, pr_ref, pi_ref, xr_ref, xi_ref, cr_sc, ci_sc, ar_sc, ai_sc = refs

        @pl.when(pl.program_id(1) == 0)
        def _():
            for sc in (cr_sc, ci_sc, ar_sc, ai_sc):
                sc[...] = jnp.zeros(shape, F32)

        prv, piv = pr_ref[...], pi_ref[...]
        rows = lax.broadcasted_iota(jnp.int32, shape, 0)

        def powers(k):
            r = (SCAN_ROWS - k) if reverse else (k - 1)
            return jnp.broadcast_to(prv[r:r + 1], shape), jnp.broadcast_to(piv[r:r + 1], shape)

        pw = [powers(k) for k in (1, 2, 4)]

        def tile(i, carry):
            cr, ci, acr, aci = carry
            j = (ntile - 1 - i) if reverse else i
            sl = pl.ds(pl.multiple_of(j * SCAN_ROWS, SCAN_ROWS), SCAN_ROWS)
            xr, xi = ur_ref[sl, :], ui_ref[sl, :]
            for (ar, ai), k in zip(pw, (1, 2, 4)):
                if reverse:
                    keep = rows < SCAN_ROWS - k
                    sr = jnp.where(keep, pltpu.roll(xr, SCAN_ROWS - k, 0), 0.0)
                    si = jnp.where(keep, pltpu.roll(xi, SCAN_ROWS - k, 0), 0.0)
                else:
                    keep = rows >= k
                    sr = jnp.where(keep, pltpu.roll(xr, k, 0), 0.0)
                    si = jnp.where(keep, pltpu.roll(xi, k, 0), 0.0)
                xr, xi = xr + (ar * sr - ai * si), xi + (ar * si + ai * sr)
            xr, xi = xr + (prv * cr - piv * ci), xi + (prv * ci + piv * cr)
            xr_ref[sl, :] = xr
            xi_ref[sl, :] = xi
            if with_states:
                last = rows == SCAN_ROWS - 1
                nr = jnp.where(last, cr, pltpu.roll(xr, SCAN_ROWS - 1, 0))
                ni = jnp.where(last, ci, pltpu.roll(xi, SCAN_ROWS - 1, 0))
                sr, si = sr_ref[sl, :], si_ref[sl, :]
                acr, aci = acr + (sr * nr + si * ni), aci + (sr * ni - si * nr)
            e = 0 if reverse else SCAN_ROWS - 1
            return jnp.broadcast_to(xr[e:e + 1], shape), jnp.broadcast_to(xi[e:e + 1], shape), acr, aci

        def tiles(g, carry):
            for u in range(SCAN_UNROLL):
                carry = tile(g * SCAN_UNROLL + u, carry)
            return carry

        assert ntile % SCAN_UNROLL == 0
        cr, ci, acr, aci = lax.fori_loop(0, ntile // SCAN_UNROLL, tiles,
                                         (cr_sc[...], ci_sc[...], ar_sc[...], ai_sc[...]))
        cr_sc[...] = cr
        ci_sc[...] = ci
        if with_states:
            ar_sc[...] = acr
            ai_sc[...] = aci

            @pl.when(pl.program_id(1) == nt - 1)
            def _():
                dr_ref[...] = acr
                di_ref[...] = aci

    tmap = (lambda l, t: (nt - 1 - t, l)) if reverse else (lambda l, t: (t, l))
    blk = pl.BlockSpec((tb, SCAN_LANES), tmap)
    tab = pl.BlockSpec((SCAN_ROWS, SCAN_LANES), lambda l, t: (0, l))
    full = jax.ShapeDtypeStruct((s, n), F32)
    return _pcall(
        body,
        name=name,
        grid=(nl, nt),
        in_specs=[blk, blk, tab, tab] + ([blk, blk] if with_states else []),
        out_specs=[blk, blk] + ([tab, tab] if with_states else []),
        out_shape=[full, full] + ([jax.ShapeDtypeStruct((SCAN_ROWS, n), F32)] * 2 if with_states else []),
        scratch_shapes=[pltpu.VMEM(shape, F32)] * 4,
        compiler_params=_params(("parallel", "arbitrary")),
    )(ur, ui, pr, pi, *(states if with_states else ()))


@functools.partial(jax.custom_vjp, nondiff_argnums=(4,))
def s5_scan(ur, ui, lr, li, name):
    return tuple(_scan_call(ur, ui, lr, li, False, name + "_f"))


def _s5_scan_fwd(ur, ui, lr, li, name):
    xr, xi = _scan_call(ur, ui, lr, li, False, name + "_f")
    return (xr, xi), (xr, xi, lr, li)


def _s5_scan_bwd(name, res, g):
    xr, xi, lr, li = res
    gr, gi, dlr, dli = _scan_call(g[0], g[1], lr, -li, True, name + "_b", states=(xr, xi))
    return gr, gi, jnp.sum(dlr, axis=0, keepdims=True), jnp.sum(dli, axis=0, keepdims=True)


s5_scan.defvjp(_s5_scan_fwd, _s5_scan_bwd)


def _running_sum(x, reverse):
    n = x.shape[0]
    rows = lax.broadcasted_iota(jnp.int32, x.shape, 0)
    k = 1
    while k < n:
        if reverse:
            x = x + jnp.where(rows < n - k, pltpu.roll(x, n - k, 0), 0.0)
        else:
            x = x + jnp.where(rows >= k, pltpu.roll(x, k, 0), 0.0)
        k *= 2
    return x


@jax.custom_vjp
def cumsum_rows(x):
    return _running_sum(x, False)


cumsum_rows.defvjp(lambda x: (_running_sum(x, False), None), lambda _, g: (_running_sum(g, True),))


def _hg_chunk(HG_SUB, q, k, v, lf, st):
    c = HG_CHUNK
    b = cumsum_rows(lf)
    qb = q * jnp.exp(b)
    o = lax.dot_general(qb.astype(BF16), st.astype(BF16), (((1,), (1,)), ((), ())), preferred_element_type=F32)
    nsub = c // HG_SUB
    srow = lax.broadcasted_iota(jnp.int32, (HG_SUB, HG_SUB, 1), 0)
    scol = lax.broadcasted_iota(jnp.int32, (HG_SUB, HG_SUB, 1), 1)
    smask = scol <= srow
    outs = []
    for i in range(nsub):
        lo = i * HG_SUB
        bi, qi, ki = b[lo:lo + HG_SUB], q[lo:lo + HG_SUB], k[lo:lo + HG_SUB]
        diff = jnp.where(smask, bi[:, None, :] - bi[None, :, :], 0.0)
        e = jnp.where(smask, jnp.exp(diff), 0.0)
        a_ii = jnp.sum(qi[:, None, :] * ki[None, :, :] * e, axis=-1)
        oi = jnp.dot(a_ii.astype(BF16), v[lo:lo + HG_SUB].astype(BF16), preferred_element_type=F32)
        if i > 0:
            r = b[lo - 1:lo]
            qt = qi * jnp.exp(bi - r)
            kt = k[:lo] * jnp.exp(r - b[:lo])
            a_ij = lax.dot_general(qt.astype(BF16), kt.astype(BF16), (((1,), (1,)), ((), ())),
                                   preferred_element_type=F32)
            oi = oi + jnp.dot(a_ij.astype(BF16), v[:lo].astype(BF16), preferred_element_type=F32)
        outs.append(oi)
    o = o + jnp.concatenate(outs, axis=0)
    bl = b[c - 1:c]
    kd = k * jnp.exp(bl - b)
    st_new = st * jnp.exp(bl) + lax.dot_general(v.astype(BF16), kd.astype(BF16), (((0,), (0,)), ((), ())),
                                                preferred_element_type=F32)
    return o, st_new


def _hg_fwd_call(q, k, v, lf, name):
    s, w = q.shape
    nc = s // HG_CHUNK
    blk = pl.BlockSpec((HG_CHUNK, w), lambda i: (i, 0))

    def body(q_ref, k_ref, v_ref, lf_ref, o_ref, st_ref, st_sc):
        @pl.when(pl.program_id(0) == 0)
        def _():
            st_sc[...] = jnp.zeros(st_sc.shape, F32)

        for hh in range(HG_HEADS):
            cs = slice(hh * HG_DK, (hh + 1) * HG_DK)
            st = st_sc[hh]
            st_ref[0, hh] = st
            o, st_new = _hg_chunk(HG_SUB_FWD, q_ref[:, cs], k_ref[:, cs], v_ref[:, cs], lf_ref[:, cs], st)
            o_ref[:, cs] = o
            st_sc[hh] = st_new

    return _pcall(
        body,
        name=name,
        grid=(nc,),
        in_specs=[blk, blk, blk, blk],
        out_specs=[blk, pl.BlockSpec((1, HG_HEADS, HG_DV, HG_DK), lambda i: (i, 0, 0, 0))],
        out_shape=[jax.ShapeDtypeStruct((s, w), F32), jax.ShapeDtypeStruct((nc, HG_HEADS, HG_DV, HG_DK), F32)],
        scratch_shapes=[pltpu.VMEM((HG_HEADS, HG_DV, HG_DK), F32)],
        compiler_params=_params(("arbitrary",)),
    )(q, k, v, lf)


def _hg_bwd_call(q, k, v, lf, sts, do, name):
    s, w = q.shape
    nc = s // HG_CHUNK
    blk = pl.BlockSpec((HG_CHUNK, w), lambda i: (nc - 1 - i, 0))

    def body(q_ref, k_ref, v_ref, lf_ref, st_ref, do_ref, dq_ref, dk_ref, dv_ref, dlf_ref, dst_sc):
        @pl.when(pl.program_id(0) == 0)
        def _():
            dst_sc[...] = jnp.zeros(dst_sc.shape, F32)

        def head(hh, carry):
            cs = pl.ds(pl.multiple_of(hh * HG_DK, HG_DK), HG_DK)
            _, vjp = jax.vjp(functools.partial(_hg_chunk, HG_SUB_BWD), q_ref[:, cs], k_ref[:, cs], v_ref[:, cs],
                             lf_ref[:, cs], st_ref[0, hh])
            dq, dk, dv, dlf, dst = vjp((do_ref[:, cs], dst_sc[hh]))
            dq_ref[:, cs] = dq
            dk_ref[:, cs] = dk
            dv_ref[:, cs] = dv
            dlf_ref[:, cs] = dlf
            dst_sc[hh] = dst
            return carry

        lax.fori_loop(0, HG_HEADS, head, 0)

    out = jax.ShapeDtypeStruct((s, w), F32)
    return _pcall(
        body,
        name=name,
        grid=(nc,),
        in_specs=[blk, blk, blk, blk, pl.BlockSpec((1, HG_HEADS, HG_DV, HG_DK), lambda i: (nc - 1 - i, 0, 0, 0)), blk],
        out_specs=[blk, blk, blk, blk],
        out_shape=[out, out, out, out],
        scratch_shapes=[pltpu.VMEM((HG_HEADS, HG_DV, HG_DK), F32)],
        compiler_params=_params(("arbitrary",)),
    )(q, k, v, lf, sts, do)


@functools.partial(jax.custom_vjp, nondiff_argnums=(4,))
def hgrn_core(q, k, v, lf, name):
    return _hg_fwd_call(q, k, v, lf, name + "_f")[0]


def _hgrn_core_fwd(q, k, v, lf, name):
    o, sts = _hg_fwd_call(q, k, v, lf, name + "_f")
    return o, (q, k, v, lf, sts)


def _hgrn_core_bwd(name, res, do):
    q, k, v, lf, sts = res
    return tuple(_hg_bwd_call(q, k, v, lf, sts, do, name + "_b"))


hgrn_core.defvjp(_hgrn_core_fwd, _hgrn_core_bwd)


def _adamw_call(w, g, m, v, name):
    rows, cols = w.shape
    tr = _tile(rows, 256, 8)
    c1 = 1.0 / (1.0 - ADAM_B1 ** ADAM_STEP)
    c2 = 1.0 / (1.0 - ADAM_B2 ** ADAM_STEP)

    def body(w_ref, g_ref, m_ref, v_ref, d_ref, mo_ref, vo_ref):
        gv = g_ref[...]
        mn = ADAM_B1 * m_ref[...] + (1.0 - ADAM_B1) * gv
        vn = ADAM_B2 * v_ref[...] + (1.0 - ADAM_B2) * (gv * gv)
        d_ref[...] = -ADAM_LR * ((mn * c1) / (jnp.sqrt(vn * c2) + ADAM_EPS) + ADAM_WD * w_ref[...])
        mo_ref[...] = mn
        vo_ref[...] = vn

    blk = pl.BlockSpec((tr, cols), lambda i: (i, 0))
    out = jax.ShapeDtypeStruct((rows, cols), F32)
    return _pcall(
        body,
        name=name,
        grid=(rows // tr,),
        in_specs=[blk, blk, blk, blk],
        out_specs=[blk, blk, blk],
        out_shape=[out, out, out],
        compiler_params=_params(("parallel",)),
    )(w, g, m, v)


ANY = pl.BlockSpec(memory_space=pl.ANY)


def _place():
    return lax.axis_index("x"), lax.axis_index("y"), lax.axis_index("c")


def _other_chips(x, y):
    return [(1 - x, y), (x, 1 - y), (1 - x, 1 - y)]


def _gather_weights(shards):
    nw = len(shards)

    def body(*refs):
        w_refs, g_refs, (send_sems, recv_sems) = refs[:nw], refs[nw:2 * nw], refs[2 * nw:]
        x, y, c = _place()
        myq = 2 * x + y
        sibling = (x, y, 1 - c)
        chips = _other_chips(x, y)

        def copy(k, src, dst, to):
            return pltpu.make_async_remote_copy(src_ref=src, dst_ref=dst, send_sem=send_sems.at[k],
                                                recv_sem=recv_sems.at[k], device_id=to, device_id_type=MESH)

        first = []
        for i in range(nw):
            for j, (px, py) in enumerate(chips):
                first.append(copy(6 * i + j, w_refs[i].at[c], g_refs[i].at[myq, c], (px, py, c)))
                first[-1].start()
        passed = []
        for i in range(nw):
            for j, (px, py) in enumerate(chips):
                q = 2 * px + py
                copy(6 * i + j, w_refs[i].at[c], g_refs[i].at[q, c], (px, py, c)).wait_recv()
                passed.append(copy(6 * i + 3 + j, g_refs[i].at[q, c], g_refs[i].at[q, c], sibling))
                passed[-1].start()
        for i in range(nw):
            for j, (px, py) in enumerate(chips):
                q = 2 * px + py
                copy(6 * i + 3 + j, g_refs[i].at[q, 1 - c], g_refs[i].at[q, 1 - c], sibling).wait_recv()
        for cp in first + passed:
            cp.wait_send()

    got = _pcall(
        body,
        name="gather_weights",
        in_specs=[ANY] * nw,
        out_specs=[ANY] * nw,
        out_shape=[jax.ShapeDtypeStruct((N_CHIPS,) + s.shape, s.dtype) for s in shards],
        scratch_shapes=[pltpu.SemaphoreType.DMA((6 * nw,)), pltpu.SemaphoreType.DMA((6 * nw,))],
    )(*shards)
    my_chip = 2 * lax.axis_index("x") + lax.axis_index("y")
    return [lax.dynamic_update_slice(g, s[None], (my_chip, 0, 0, 0)) for g, s in zip(got, shards)]


HBM = pl.BlockSpec(memory_space=pltpu.HBM)
SEM = pl.BlockSpec(memory_space=pltpu.SEMAPHORE)


def _gather_start(shards, after):
    nw = len(shards)

    def body(*refs):
        w_refs, land_refs = refs[:nw], refs[nw:2 * nw]
        send_sems, recv_sems = refs[2 * nw + 1:2 * nw + 3]
        token = refs[-1]
        x, y, c = _place()
        myq = 2 * x + y
        for i in range(nw):
            hh = shards[i].shape[0] // 2
            rows = pl.ds(c * hh, hh)
            for j, (px, py) in enumerate(_other_chips(x, y)):
                for tc in range(2):
                    pltpu.make_async_remote_copy(
                        src_ref=w_refs[i].at[rows], dst_ref=land_refs[i].at[myq, rows],
                        send_sem=send_sems.at[6 * i + 2 * j + tc], recv_sem=recv_sems.at[6 * i + 2 * j + c],
                        device_id=(px, py, tc), device_id_type=MESH).start()
        token[...] = jnp.zeros_like(token)

    lands = [lax.empty((N_CHIPS,) + s.shape, s.dtype) for s in shards]
    out = _pcall(
        body,
        name="gather_start",
        in_specs=[HBM] * (2 * nw) + [pl.BlockSpec(memory_space=pl.ANY)],
        out_specs=[SEM, SEM] + [HBM] * (2 * nw) + [pl.BlockSpec(memory_space=pltpu.VMEM)],
        out_shape=[pltpu.SemaphoreType.DMA((6 * nw,)), pltpu.SemaphoreType.DMA((6 * nw,))]
        + [pltpu.HBM(s.shape, s.dtype) for s in shards] + [pltpu.HBM(t.shape, t.dtype) for t in lands]
        + [jax.ShapeDtypeStruct((8, 128), F32)],
        input_output_aliases={i: 2 + i for i in range(2 * nw)},
        compiler_params=pltpu.CompilerParams(has_side_effects=pltpu.SideEffectType.DATAFLOW_SIDE_EFFECTING),
    )(*[pltpu.with_memory_space_constraint(t, pltpu.HBM) for t in list(shards) + lands], after)
    return out[0], out[1], out[2:2 + nw], out[2 + nw:2 + 2 * nw], out[-1]


def _gather_wait(send_sems, recv_sems, shards, lands, after):
    nw = len(shards)

    def body(*refs):
        w_refs, land_refs = refs[:nw], refs[nw:2 * nw]
        send_sems, recv_sems = refs[2 * nw:2 * nw + 2]
        x, y, c = _place()
        myq = 2 * x + y
        for i in range(nw):
            hh = shards[i].shape[0] // 2
            for j, (px, py) in enumerate(_other_chips(x, y)):
                q = 2 * px + py
                for other in range(2):
                    mine = pl.ds(c * hh, hh)
                    theirs = pl.ds(other * hh, hh)
                    pltpu.make_async_remote_copy(
                        src_ref=w_refs[i].at[mine], dst_ref=land_refs[i].at[myq, mine],
                        send_sem=send_sems.at[6 * i + 2 * j + other], recv_sem=recv_sems.at[6 * i + 2 * j + other],
                        device_id=(px, py, other), device_id_type=MESH).wait_send()
                    pltpu.make_async_remote_copy(
                        src_ref=w_refs[i].at[theirs], dst_ref=land_refs[i].at[q, theirs],
                        send_sem=send_sems.at[6 * i + 2 * j + other], recv_sem=recv_sems.at[6 * i + 2 * j + other],
                        device_id=(px, py, other), device_id_type=MESH).wait_recv()

    out = _pcall(
        body,
        name="gather_wait",
        in_specs=[HBM] * (2 * nw) + [SEM, SEM, pl.BlockSpec(memory_space=pl.ANY)],
        out_specs=[HBM] * (2 * nw),
        out_shape=[pltpu.HBM(s.shape, s.dtype) for s in shards] + [pltpu.HBM(t.shape, t.dtype) for t in lands],
        input_output_aliases={i: i for i in range(2 * nw)},
        compiler_params=pltpu.CompilerParams(has_side_effects=pltpu.SideEffectType.DATAFLOW_SIDE_EFFECTING),
    )(*shards, *lands, send_sems, recv_sems, after)
    my_chip = 2 * lax.axis_index("x") + lax.axis_index("y")
    return [lax.dynamic_update_slice(g, s[None], (my_chip, 0, 0)) for g, s in zip(out[nw:], out[:nw])]


def _swap_halves_to_sibling(grs, tag=""):
    nw = len(grs)

    def body(*refs):
        g_refs, t_refs, (send_sems, recv_sems) = refs[:nw], refs[nw:2 * nw], refs[2 * nw:]
        x, y, c = _place()
        cps = []
        for i in range(nw):
            hh = grs[i].shape[1] // 2
            cps.append(pltpu.make_async_remote_copy(
                src_ref=g_refs[i].at[:, pl.ds((1 - c) * hh, hh)], dst_ref=t_refs[i], send_sem=send_sems.at[i],
                recv_sem=recv_sems.at[i], device_id=(x, y, 1 - c), device_id_type=MESH))
            cps[-1].start()
        for cp in cps:
            cp.wait()

    return _pcall(
        body,
        name="grad_d2d" + tag,
        in_specs=[ANY] * nw,
        out_specs=[ANY] * nw,
        out_shape=[jax.ShapeDtypeStruct((g.shape[0], g.shape[1] // 2, g.shape[2]), g.dtype) for g in grs],
        scratch_shapes=[pltpu.SemaphoreType.DMA((nw,)), pltpu.SemaphoreType.DMA((nw,))],
    )(*grs)


def _add_half(gr, t1, place, name):
    nq, r, wd = gr.shape
    hh = r // 2
    tr = _tile(hh, 512, 16)
    nb = hh // tr

    def body(p_ref, g_ref, t_ref, o_ref):
        o_ref[...] = (g_ref[...] + t_ref[...]).astype(BF16)

    return _pcall(
        body,
        name=name,
        grid_spec=pltpu.PrefetchScalarGridSpec(
            num_scalar_prefetch=1,
            grid=(nq, nb),
            in_specs=[
                pl.BlockSpec((1, tr, wd), lambda q, i, p_ref: (q, p_ref[0] * nb + i, 0)),
                pl.BlockSpec((1, tr, wd), lambda q, i, p_ref: (q, i, 0)),
            ],
            out_specs=pl.BlockSpec((1, tr, wd), lambda q, i, p_ref: (q, i, 0)),
        ),
        out_shape=jax.ShapeDtypeStruct((nq, hh, wd), BF16),
        compiler_params=_params(("parallel", "parallel")),
    )(place, gr, t1)


def _scatter_partials(ps):
    nw = len(ps)

    def body(*refs):
        p_refs, t_refs, (send_sems, recv_sems) = refs[:nw], refs[nw:2 * nw], refs[2 * nw:]
        x, y, c = _place()
        cps = []
        for i in range(nw):
            for j, (px, py) in enumerate(_other_chips(x, y)):
                cps.append(pltpu.make_async_remote_copy(
                    src_ref=p_refs[i].at[2 * px + py], dst_ref=t_refs[i].at[j], send_sem=send_sems.at[3 * i + j],
                    recv_sem=recv_sems.at[3 * i + j], device_id=(px, py, c), device_id_type=MESH))
                cps[-1].start()
        for cp in cps:
            cp.wait()

    return _pcall(
        body,
        name="grad_ici",
        in_specs=[ANY] * nw,
        out_specs=[ANY] * nw,
        out_shape=[jax.ShapeDtypeStruct((3,) + p.shape[1:], p.dtype) for p in ps],
        scratch_shapes=[pltpu.SemaphoreType.DMA((3 * nw,)), pltpu.SemaphoreType.DMA((3 * nw,))],
    )(*ps)


def _scatter_copies(p_refs, t_refs, send_sems, recv_sems):
    x, y, c = _place()
    return [pltpu.make_async_remote_copy(
        src_ref=p_refs[i].at[2 * px + py], dst_ref=t_refs[i].at[j], send_sem=send_sems.at[3 * i + j],
        recv_sem=recv_sems.at[3 * i + j], device_id=(px, py, c), device_id_type=MESH)
        for i in range(len(p_refs)) for j, (px, py) in enumerate(_other_chips(x, y))]


def _scatter_start(ps, after):
    nw = len(ps)

    def body(*refs):
        send_sems, recv_sems = refs[2 * nw + 1:2 * nw + 3]
        for cp in _scatter_copies(refs[:nw], refs[nw:2 * nw], send_sems, recv_sems):
            cp.start()
        refs[-1][...] = jnp.zeros_like(refs[-1])

    lands = [lax.empty((3,) + p.shape[1:], p.dtype) for p in ps]
    out = _pcall(
        body,
        name="scatter_start",
        in_specs=[HBM] * (2 * nw) + [pl.BlockSpec(memory_space=pl.ANY)],
        out_specs=[SEM, SEM] + [HBM] * (2 * nw) + [pl.BlockSpec(memory_space=pltpu.VMEM)],
        out_shape=[pltpu.SemaphoreType.DMA((3 * nw,)), pltpu.SemaphoreType.DMA((3 * nw,))]
        + [pltpu.HBM(t.shape, t.dtype) for t in list(ps) + lands] + [jax.ShapeDtypeStruct((8, 128), F32)],
        input_output_aliases={i: 2 + i for i in range(2 * nw)},
        compiler_params=pltpu.CompilerParams(has_side_effects=pltpu.SideEffectType.DATAFLOW_SIDE_EFFECTING),
    )(*[pltpu.with_memory_space_constraint(t, pltpu.HBM) for t in list(ps) + lands], after)
    return out[0], out[1], out[2:2 + nw], out[2 + nw:2 + 2 * nw], out[-1]


def _scatter_wait(send_sems, recv_sems, ps, lands, after):
    nw = len(ps)

    def body(*refs):
        for cp in _scatter_copies(refs[:nw], refs[nw:2 * nw], refs[2 * nw], refs[2 * nw + 1]):
            cp.wait_send()
            cp.wait_recv()

    out = _pcall(
        body,
        name="scatter_wait",
        in_specs=[HBM] * (2 * nw) + [SEM, SEM, pl.BlockSpec(memory_space=pl.ANY)],
        out_specs=[HBM] * (2 * nw),
        out_shape=[pltpu.HBM(t.shape, t.dtype) for t in list(ps) + list(lands)],
        input_output_aliases={i: i for i in range(2 * nw)},
        compiler_params=pltpu.CompilerParams(has_side_effects=pltpu.SideEffectType.DATAFLOW_SIDE_EFFECTING),
    )(*ps, *lands, send_sems, recv_sems, after)
    return out[nw:]


def _add_partials(gr, t1, t2, place, name):
    nq, r, wd = gr.shape
    hh = r // 2
    tr = _tile(hh, 512, 16)
    nb = hh // tr

    def body(p_ref, g_ref, s_ref, t_ref, o_ref):
        o_ref[...] = (((g_ref[0] + s_ref[0]) + t_ref[0].astype(F32)) + t_ref[1].astype(F32)) + t_ref[2].astype(F32)

    return _pcall(
        body,
        name=name,
        grid_spec=pltpu.PrefetchScalarGridSpec(
            num_scalar_prefetch=1,
            grid=(nb,),
            in_specs=[
                pl.BlockSpec((1, tr, wd), lambda i, p_ref: (p_ref[1], p_ref[0] * nb + i, 0)),
                pl.BlockSpec((1, tr, wd), lambda i, p_ref: (p_ref[1], i, 0)),
                pl.BlockSpec((3, tr, wd), lambda i, p_ref: (0, i, 0)),
            ],
            out_specs=pl.BlockSpec((tr, wd), lambda i, p_ref: (i, 0)),
        ),
        out_shape=jax.ShapeDtypeStruct((hh, wd), F32),
        compiler_params=_params(("parallel",)),
    )(place, gr, t1, t2)


def _join_halves(halves):
    nw = len(halves)

    def body(*refs):
        h_refs, f_refs, (send_sems, recv_sems) = refs[:nw], refs[nw:2 * nw], refs[2 * nw:]
        x, y, c = _place()

        def copy(i, rows):
            hh = halves[i].shape[0]
            return pltpu.make_async_remote_copy(src_ref=h_refs[i], dst_ref=f_refs[i].at[pl.ds(rows * hh, hh)],
                                                send_sem=send_sems.at[i], recv_sem=recv_sems.at[i],
                                                device_id=(x, y, 1 - c), device_id_type=MESH)

        for i in range(nw):
            copy(i, c).start()
        for i in range(nw):
            copy(i, c).wait_send()
            copy(i, 1 - c).wait_recv()

    got = _pcall(
        body,
        name="grad_join",
        in_specs=[ANY] * nw,
        out_specs=[ANY] * nw,
        out_shape=[jax.ShapeDtypeStruct((2 * h.shape[0], h.shape[1]), h.dtype) for h in halves],
        scratch_shapes=[pltpu.SemaphoreType.DMA((nw,)), pltpu.SemaphoreType.DMA((nw,))],
    )(*halves)
    c = lax.axis_index("c")
    return [lax.dynamic_update_slice(g, h, (c * h.shape[0], 0)) for g, h in zip(got, halves)]


N_DEV = 8


def _allreduce_small(buf):
    rows, wd = buf.shape
    n = rows // N_DEV

    def body(x_ref, o_ref, stage_ref, send_sems, recv_sems):
        x, y, c = _place()
        me = 4 * x + 2 * y + c

        def peer(k):
            return (x ^ (k >> 2), y ^ ((k >> 1) & 1), c ^ (k & 1))

        def chunk(ref, d):
            return ref.at[pl.ds(pl.multiple_of(d * n, 8), n)]

        first = []
        for k in range(1, N_DEV):
            px, py, pc = peer(k)
            cp = pltpu.make_async_remote_copy(src_ref=chunk(x_ref, 4 * px + 2 * py + pc), dst_ref=stage_ref.at[k],
                                              send_sem=send_sems.at[k - 1], recv_sem=recv_sems.at[k - 1],
                                              device_id=(px, py, pc), device_id_type=MESH)
            cp.start()
            first.append(cp)
        acc = chunk(x_ref, me)[...]
        for k in range(1, N_DEV):
            first[k - 1].wait_recv()
            acc = acc + stage_ref[k]
        chunk(o_ref, me)[...] = acc
        second = []
        for k in range(1, N_DEV):
            cp = pltpu.make_async_remote_copy(src_ref=chunk(o_ref, me), dst_ref=chunk(o_ref, me),
                                              send_sem=send_sems.at[6 + k], recv_sem=recv_sems.at[6 + k],
                                              device_id=peer(k), device_id_type=MESH)
            cp.start()
            second.append(cp)
        for k in range(1, N_DEV):
            px, py, pc = peer(k)
            them = 4 * px + 2 * py + pc
            pltpu.make_async_remote_copy(src_ref=chunk(o_ref, them), dst_ref=chunk(o_ref, them),
                                         send_sem=send_sems.at[6 + k], recv_sem=recv_sems.at[6 + k],
                                         device_id=(px, py, pc), device_id_type=MESH).wait_recv()
        for cp in first + second:
            cp.wait_send()

    vmem = pl.BlockSpec(memory_space=pltpu.VMEM)
    return _pcall(
        body,
        name="allreduce_small",
        in_specs=[vmem],
        out_specs=vmem,
        out_shape=jax.ShapeDtypeStruct((rows, wd), F32),
        scratch_shapes=[pltpu.VMEM((N_DEV, n, wd), F32), pltpu.SemaphoreType.DMA((14,)),
                        pltpu.SemaphoreType.DMA((14,))],
    )(buf)


@functools.partial(jax.custom_vjp, nondiff_argnums=(1,))
def split_cols(z, spec):
    outs = []
    for segs in spec[0]:
        parts = [z[:, s:s + n] for s, n in segs]
        outs.append(parts[0] if len(parts) == 1 else jnp.concatenate(parts, axis=1))
    return tuple(outs)


def _split_cols_fwd(z, spec):
    return split_cols(z, spec), None


def _split_cols_bwd(spec, _, gs):
    pieces, width = spec
    segs = []
    for p, piece in enumerate(pieces):
        off = 0
        for s, n in piece:
            segs.append((s, n, p, off))
            off += n
    rows, parts, pos = gs[0].shape[0], [], 0
    for s, n, p, off in sorted(segs):
        if s > pos:
            parts.append(jnp.zeros((rows, s - pos), gs[0].dtype))
        parts.append(gs[p] if (off == 0 and n == gs[p].shape[1]) else gs[p][:, off:off + n])
        pos = s + n
    if pos < width:
        parts.append(jnp.zeros((rows, width - pos), gs[0].dtype))
    return (jnp.concatenate(parts, axis=1),)


split_cols.defvjp(_split_cols_fwd, _split_cols_bwd)


def _contiguous(sizes):
    pieces, off = [], 0
    for n in sizes:
        pieces.append(((off, n),))
        off += n
    return tuple(pieces), off


def _rope(t, cos, sin):
    half = t.shape[-1] // 2
    t1, t2 = t[..., :half], t[..., half:]
    return jnp.concatenate([t1 * cos - t2 * sin, t2 * cos + t1 * sin], axis=-1)


def _s5_mixer(u, sp, l, w16, toks):
    g, p_, h = SSM_GROUPS, SSM_STATE, SSM_GROUP_CH
    lam_re, lam_im = sp["ssm_lam_re", l], sp["ssm_lam_im", l]
    step = jnp.exp(sp["ssm_log_step", l])[:, None]
    mag, ang = jnp.exp(lam_re * step), lam_im * step
    lbr, lbi = mag * jnp.cos(ang), mag * jnp.sin(ang)
    den = lam_re * lam_re + lam_im * lam_im
    nr, ni = lbr - 1.0, lbi
    qr, qi = (nr * lam_re + ni * lam_im) / den, (ni * lam_re - nr * lam_im) / den
    b_re, b_im = sp["ssm_b_re", l], sp["ssm_b_im", l]
    bbr = qr[..., None] * b_re - qi[..., None] * b_im
    bbi = qr[..., None] * b_im + qi[..., None] * b_re
    nb = g // SSM_PACK
    eye = jnp.eye(SSM_PACK, dtype=F32)

    def in_blocks(t):
        return jnp.einsum("jgph,gk->jghkp", t.reshape(nb, SSM_PACK, p_, h), eye).reshape(nb, SSM_PACK * h, SSM_PACK * p_)

    def out_blocks(t):
        return jnp.einsum("jghp,gk->jgpkh", t.reshape(nb, SSM_PACK, h, p_), eye).reshape(nb, SSM_PACK * p_, SSM_PACK * h)

    nm = "ssm%d" % l
    ur = bd_matmul(u, in_blocks(bbr), nm + "_bur")
    ui = bd_matmul(u, in_blocks(bbi), nm + "_bui")
    xr, xi = s5_scan(ur, ui, lbr.reshape(1, g * p_), lbi.reshape(1, g * p_), nm + "_scan")
    y = (bd_matmul(xr, out_blocks(sp["ssm_c_re", l]), nm + "_cr") + bd_matmul(xi, out_blocks(-sp["ssm_c_im", l]), nm + "_ci")
         + sp["ssm_d", l].reshape(1, g * h) * u)
    y = jax.nn.gelu(y)
    zz = matmul_w(y, w16["ssm_w_glu", l], toks["ssm_w_glu", l], nm + "_glu")
    z_out, z_gate = split_cols(zz, _contiguous((D_MODEL, D_MODEL)))
    return z_out * jax.nn.sigmoid(z_gate)


def _mla_mixer(q_lat, kv_lat, k_rope, cos, sin, sp, l, w16, toks):
    s = q_lat.shape[0]
    nm = "mla%d" % l
    dqk = MLA_NOPE + MLA_ROPE
    qt = matmul_w_t(rmsnorm(q_lat, sp["mla_q_norm", l], nm + "_qn"), w16["mla_w_uq", l], toks["mla_w_uq", l],
                    nm + "_uq").reshape(MLA_HEADS, dqk, s)
    half = MLA_ROPE // 2
    q1, q2 = qt[:, MLA_NOPE:MLA_NOPE + half], qt[:, MLA_NOPE + half:]
    ct, st = cos.T[None], sin.T[None]
    qt = jnp.concatenate([qt[:, :MLA_NOPE], q1 * ct - q2 * st, q2 * ct + q1 * st], axis=1).reshape(MLA_HEADS * dqk, s)
    kvt = matmul_w_t(rmsnorm(kv_lat, sp["mla_kv_norm", l], nm + "_kvn"), w16["mla_w_ukv", l], toks["mla_w_ukv", l],
                     nm + "_ukv").reshape(MLA_HEADS, MLA_NOPE + MLA_V, s)
    k_pe = _rope(k_rope, cos, sin)
    k = jnp.concatenate([kvt[:, :MLA_NOPE].transpose(0, 2, 1),
                         jnp.broadcast_to(k_pe[None], (MLA_HEADS, s, MLA_ROPE))], axis=-1)
    vt = kvt[:, MLA_NOPE:].reshape(MLA_HEADS * MLA_V, s)
    ot = attention(qt, k, vt, 1.0 / math.sqrt(dqk), True, nm + "_att")
    return matmul_w_at(ot, w16["mla_w_o", l], toks["mla_w_o", l], nm + "_o")


def _hgrn_mixer(q, f_logit, i_in, g, lb, sp, l, w16, toks):
    s = q.shape[0]
    nm = "hg%d" % l
    f = lb + (1.0 - lb) * jax.nn.sigmoid(f_logit)
    o = hgrn_core(jax.nn.silu(q), 1.0 - f, i_in, jnp.log(f), nm + "_core")
    o = rmsnorm(o.reshape(s * HG_HEADS, HG_DV), sp["hg_g_norm", l], nm + "_gn").reshape(s, HG_HEADS * HG_DV)
    return silu_gated_out(o, g, w16["hg_w_o", l], toks["hg_w_o", l], nm + "_o")


def _cross_attention(x, h, mem_n, l, w16, toks):
    s, m = h.shape[0], mem_n.shape[0]
    nm = "xat%d" % l
    xw = X_HEADS * X_HEAD_DIM
    qt = matmul_w_t(h, w16["x_w_q", l], toks["x_w_q", l], nm + "_q")
    kv = matmul_w(mem_n, w16["x_w_kv", l], toks["x_w_kv", l], nm + "_kv")
    k, v = split_cols(kv, _contiguous((xw, xw)))
    k = k.reshape(m, X_HEADS, X_HEAD_DIM).transpose(1, 0, 2)
    ot = attention(qt, k, v.T, 1.0 / math.sqrt(X_HEAD_DIM), False, nm + "_att")
    return matmul_w_at_res(x, ot, w16["x_w_o", l], toks["x_w_o", l], nm + "_o")


def _in_spec():
    hw = HG_HEADS * HG_DK
    sizes = (SSM_WIDTH, MLA_Q_RANK, MLA_KV_RANK, MLA_ROPE, hw, hw, hw, hw, D_MODEL, D_MODEL, D_MODEL)
    pieces, lo = [], 0
    for n in sizes:
        segs, c = [], lo
        while c < lo + n:
            end = min(lo + n, (c // D_IN_SHARD + 1) * D_IN_SHARD)
            segs.append((c + (D_IN_BLOCK - D_IN_SHARD) * (c // D_IN_SHARD), end - c))
            c = end
        pieces.append(tuple(segs))
        lo += n
    assert lo == D_IN
    return tuple(pieces), N_CHIPS * D_IN_BLOCK


def _layer(x, mem, cos, sin, lb, l, sp, w16, toks):
    s = x.shape[0]
    nm = "l%d" % l
    h = rmsnorm(x, sp["norm_mix", l], nm + "_nmix")
    z = matmul_w(h, w16["w_in", l], toks["w_in", l], nm + "_in")
    u, q_lat, kv_lat, k_rope, hg_q, hg_f, hg_i, hg_g, l_ssm, l_mla, l_hg = split_cols(z, _in_spec())
    y_ssm = _s5_mixer(u, sp, l, w16, toks)
    y_mla = _mla_mixer(q_lat, kv_lat, k_rope, cos, sin, sp, l, w16, toks)
    y_hg = _hgrn_mixer(hg_q, hg_f, hg_i, hg_g, lb, sp, l, w16, toks)
    x = gated_merge_out(x, (y_ssm, y_mla, y_hg), (l_ssm, l_mla, l_hg), w16["w_out", l], toks["w_out", l], nm + "_out")
    hc = rmsnorm(x, sp["norm_cross", l], nm + "_ncross")
    mem_n = rmsnorm(mem, sp["norm_mem", l], nm + "_nmem")
    x = _cross_attention(x, hc, mem_n, l, w16, toks)
    hf = rmsnorm(x, sp["norm_ffn", l], nm + "_nffn")
    gt = matmul_w(hf, w16["ffn_w_gate", l], toks["ffn_w_gate", l], nm + "_gate")
    up = matmul_w(hf, w16["ffn_w_up", l], toks["ffn_w_up", l], nm + "_up")
    return swiglu_down(x, gt, up, w16["ffn_w_down", l], toks["ffn_w_down", l], nm + "_down")


def _lower_bounds(hg_lb):
    lb_p = jax.nn.softmax(hg_lb, axis=0)
    return jnp.cumsum(lb_p, axis=0) - lb_p[0:1]


def _loss_head(norm_final, x, target):
    y = rmsnorm(x, norm_final, "nfinal")
    return 0.5 * jnp.sum(jnp.mean(jnp.square(y - target), axis=-1))


def _forward_backward(sp, toks, x, w16, later, between, mem, pos, target):
    half = MLA_ROPE // 2
    inv_freq = ROPE_THETA ** (-jnp.arange(half, dtype=F32) / half)
    ang = pos.astype(F32)[:, None] * inv_freq
    cos, sin = jnp.cos(ang), jnp.sin(ang)
    lower, lower_vjp = jax.vjp(_lower_bounds, sp["hg_lb", ALL])
    token, wait_for_weights = later
    x = x + token[0, 0]
    vjps = []
    for l in range(DEPTH):
        if l == 1:
            w16 = wait_for_weights(w16, x)
        sp_l = {k: v for k, v in sp.items() if k[1] == l}
        toks_l = {k: v for k, v in toks.items() if k[1] == l}
        weights = w16
        x, layer_vjp = jax.vjp(lambda s, t, lb, xx, l=l, weights=weights: _layer(xx, mem, cos, sin, lb, l, s, weights, t),
                               sp_l, toks_l, lower[l], x)
        vjps.append(layer_vjp)
    loss, head_vjp = jax.vjp(lambda g, xx: _loss_head(g, xx, target), sp["norm_final", ALL], x)
    g_sp, g_tok, d_lower = {}, {}, [None] * DEPTH
    g_sp["norm_final", ALL], dx = head_vjp(jnp.ones((), F32))
    for l in reversed(range(DEPTH)):
        g_s, g_t, d_lower[l], dx = vjps[l](dx)
        g_sp.update(g_s)
        g_tok.update(g_t)
        if l > 0:
            dx = between(l, g_t, dx)
    (g_sp["hg_lb", ALL],) = lower_vjp(jnp.stack(d_lower))
    return loss, g_sp, g_tok, dx


WEIGHTS = ["norm_mix", "w_in", "ssm_lam_re", "ssm_lam_im", "ssm_b_re", "ssm_b_im", "ssm_c_re", "ssm_c_im", "ssm_d",
           "ssm_log_step", "ssm_w_glu", "mla_q_norm", "mla_kv_norm", "mla_w_uq", "mla_w_ukv", "mla_w_o", "hg_lb",
           "hg_g_norm", "hg_w_o", "w_out", "norm_cross", "norm_mem", "x_w_q", "x_w_kv", "x_w_o", "norm_ffn",
           "ffn_w_gate_up", "ffn_w_down", "norm_final"]
INPUTS = ["x", "mem", "positions"] + WEIGHTS + ["loss_target"] + ["m_" + n for n in WEIGHTS] + ["v_" + n for n in WEIGHTS]
BIG = [("w_in", 2), ("ssm_w_glu", 2), ("mla_w_uq", 2), ("mla_w_ukv", 2), ("mla_w_o", 2), ("hg_w_o", 2), ("w_out", 1),
       ("x_w_q", 1), ("x_w_kv", 1), ("x_w_o", 2), ("ffn_w_gate_up", 2), ("ffn_w_down", 1)]
SMALL = [n for n in WEIGHTS if n not in dict(BIG)]
WHOLE = ("hg_lb", "norm_final")
ALL = -1
SMALL_W = 128
BLOCKED = ("w_in", "ffn_w_gate_up")


def _pack(arrs, rows, width, dtype):
    flat = jnp.concatenate([a.astype(dtype).reshape(-1) for a in arrs])
    return jnp.pad(flat, (0, rows * width - flat.shape[0])).reshape(rows, width)


def _unpack(buf, like):
    flat = buf.reshape(-1)
    out, off = [], 0
    for a in like:
        n = math.prod(a.shape)
        out.append(flat[off:off + n].reshape(a.shape))
        off += n
    return out


def kernel(
    x, mem, positions, norm_mix, w_in, ssm_lam_re, ssm_lam_im, ssm_b_re, ssm_b_im, ssm_c_re, ssm_c_im, ssm_d,
    ssm_log_step, ssm_w_glu, mla_q_norm, mla_kv_norm, mla_w_uq, mla_w_ukv, mla_w_o, hg_lb, hg_g_norm, hg_w_o,
    w_out, norm_cross, norm_mem, x_w_q, x_w_kv, x_w_o, norm_ffn, ffn_w_gate_up, ffn_w_down, norm_final,
    loss_target, m_norm_mix, m_w_in, m_ssm_lam_re, m_ssm_lam_im, m_ssm_b_re, m_ssm_b_im, m_ssm_c_re, m_ssm_c_im,
    m_ssm_d, m_ssm_log_step, m_ssm_w_glu, m_mla_q_norm, m_mla_kv_norm, m_mla_w_uq, m_mla_w_ukv, m_mla_w_o,
    m_hg_lb, m_hg_g_norm, m_hg_w_o, m_w_out, m_norm_cross, m_norm_mem, m_x_w_q, m_x_w_kv, m_x_w_o, m_norm_ffn,
    m_ffn_w_gate_up, m_ffn_w_down, m_norm_final, v_norm_mix, v_w_in, v_ssm_lam_re, v_ssm_lam_im, v_ssm_b_re,
    v_ssm_b_im, v_ssm_c_re, v_ssm_c_im, v_ssm_d, v_ssm_log_step, v_ssm_w_glu, v_mla_q_norm, v_mla_kv_norm,
    v_mla_w_uq, v_mla_w_ukv, v_mla_w_o, v_hg_lb, v_hg_g_norm, v_hg_w_o, v_w_out, v_norm_cross, v_norm_mem,
    v_x_w_q, v_x_w_kv, v_x_w_o, v_norm_ffn, v_ffn_w_gate_up, v_ffn_w_down, v_norm_final):
    given = locals()
    a = {n: given[n] for n in INPUTS}
    x, mem, pos, target = a["x"][0], a["mem"][0], a["positions"][0], a["loss_target"][0]
    place = jnp.stack([lax.axis_index("c"), 2 * lax.axis_index("x") + lax.axis_index("y")]).astype(jnp.int32)

    def shard16(n):
        w = a[n].astype(BF16)
        if n == "w_in":
            w = jnp.pad(w, ((0, 0), (0, 0), (0, D_IN_BLOCK - D_IN_SHARD)))
        return w

    def use(blocks, l, w16):
        for (n, ax), got in zip(BIG, blocks):
            if n in BLOCKED:
                full = got
            elif ax == 2:
                full = jnp.concatenate([got[q] for q in range(N_CHIPS)], axis=1)
            else:
                full = got.reshape(N_CHIPS * got.shape[1], got.shape[2])
            if n == "ffn_w_gate_up":
                half = N_CHIPS // 2
                w16["ffn_w_gate", l], w16["ffn_w_up", l] = full[:half], full[half:]
            else:
                w16[n, l] = full
        return w16

    shards = [shard16(n) for n, _ in BIG]
    halves = [s[0].reshape(2, s.shape[1] // 2, s.shape[2]) for s in shards]
    w16 = use([g.reshape(N_CHIPS, s.shape[1], s.shape[2]) for g, s in zip(_gather_weights(halves), shards)], 0, {})
    pending = _gather_start([s[1] for s in shards], after=w16["w_in", 0])
    toks = {}
    for n, ax in BIG:
        rows, cols = a[n].shape[1], shards[BIG.index((n, ax))].shape[2]
        shape = (N_CHIPS, rows, cols) if n in BLOCKED else (rows, N_CHIPS * cols) if ax == 2 else (N_CHIPS * rows, cols)
        for l in range(DEPTH):
            if n == "ffn_w_gate_up":
                toks["ffn_w_gate", l] = jnp.zeros((N_CHIPS // 2,) + shape[1:], F32)
                toks["ffn_w_up", l] = jnp.zeros((N_CHIPS // 2,) + shape[1:], F32)
            else:
                toks[n, l] = jnp.zeros(shape, F32)

    def layer1_weights(w16, after):
        return use(_gather_wait(*pending[:4], after), 1, dict(w16))
    sp = {}
    for n in SMALL:
        if n in WHOLE:
            sp[n, ALL] = a[n]
        else:
            for l in range(DEPTH):
                sp[n, l] = a[n][l]

    def chip_sums_of(g_tok, l):
        grads = []
        for n, ax in BIG:
            if n == "ffn_w_gate_up":
                g = jnp.concatenate([g_tok["ffn_w_gate", l], g_tok["ffn_w_up", l]], axis=0)
            else:
                g = g_tok[n, l]
            if n in BLOCKED:
                grads.append(g)
            elif ax == 2:
                grads.append(g.reshape(g.shape[0], N_CHIPS, g.shape[1] // N_CHIPS).transpose(1, 0, 2))
            else:
                grads.append(g.reshape(N_CHIPS, g.shape[0] // N_CHIPS, g.shape[1]))
        from_sibling = _swap_halves_to_sibling(grads, tag=str(l))
        sums = [_add_half(g, t, place, "grad_add_half_%s%d" % (n, l)) for (n, _), g, t in zip(BIG, grads, from_sibling)]
        return grads, from_sibling, sums

    def totals_of(l, grads, from_sibling, from_chips):
        return [_add_partials(g, t1, t2, place, "grad_add_partials_%s%d" % (n, l))
                for (n, _), g, t1, t2 in zip(BIG, grads, from_sibling, from_chips)]

    started = {}

    def between(l, g_tok_l, dx):
        grads, from_sibling, sums = chip_sums_of(g_tok_l, l)
        flight = _scatter_start(sums, after=dx)
        started[l] = (grads, from_sibling, flight)
        return dx + flight[4][0, 0]

    loss, g_sp, g_tok, g_x = _forward_backward(sp, toks, x, w16, (pending[4], layer1_weights), between, mem, pos, target)
    g_small = {n: g_sp[n, ALL] if n in WHOLE else jnp.stack([g_sp[n, l] for l in range(DEPTH)]) for n in SMALL}
    totals = {}
    for l, (grads, from_sibling, flight) in started.items():
        totals[l] = totals_of(l, grads, from_sibling, _scatter_wait(*flight[:4], after=g_x))
    grads, from_sibling, sums = chip_sums_of(g_tok, 0)
    totals[0] = totals_of(0, grads, from_sibling, _scatter_partials(sums))
    joined = _join_halves([t for l in range(DEPTH) for t in totals[l]])
    nbig = len(BIG)
    g_big = {n: jnp.stack([joined[nbig * l + i][:, :a[n].shape[2]] for l in range(DEPTH)])
             for i, (n, _) in enumerate(BIG)}

    small_like = [a[n] for n in SMALL] + [jnp.zeros((1,), F32)]
    n_small = sum(math.prod(t.shape) for t in small_like)
    srows = -(-n_small // (SMALL_W * 8 * N_DEV)) * 8 * N_DEV
    reduced = _allreduce_small(_pack([g_small[n] for n in SMALL] + [loss.reshape(1)], srows, SMALL_W, F32))
    red = _unpack(reduced, small_like)
    g_sm = dict(zip(SMALL, red[:-1]))
    loss = red[-1].reshape(())

    out_g, out_d, out_m, out_v = {}, {}, {}, {}
    for n, _ in BIG:
        shp = a[n].shape
        two = lambda t: t.reshape(shp[0] * shp[1], shp[2])
        d, m, v = _adamw_call(two(a[n]), two(g_big[n]), two(a["m_" + n]), two(a["v_" + n]), "adamw_" + n)
        out_g[n], out_d[n], out_m[n], out_v[n] = g_big[n], d.reshape(shp), m.reshape(shp), v.reshape(shp)
    small_arrs = [a[n] for n in SMALL]
    d, m, v = _adamw_call(
        _pack(small_arrs, srows, SMALL_W, F32), reduced,
        _pack([a["m_" + n] for n in SMALL], srows, SMALL_W, F32),
        _pack([a["v_" + n] for n in SMALL], srows, SMALL_W, F32), "adamw_small")
    for n, gg, dd, mm_, vv in zip(SMALL, red[:-1], _unpack(d, small_arrs), _unpack(m, small_arrs), _unpack(v, small_arrs)):
        out_g[n], out_d[n], out_m[n], out_v[n] = gg, dd, mm_, vv

    return (loss, g_x[None], *[out_g[n] for n in WEIGHTS], *[out_d[n] for n in WEIGHTS],
            *[out_m[n] for n in WEIGHTS], *[out_v[n] for n in WEIGHTS])
```

```python
import functools
import math

import jax
import jax.numpy as jnp
from jax import lax
from jax.experimental import pallas as pl
from jax.experimental.pallas import tpu as pltpu

F32 = jnp.float32
BF16 = jnp.bfloat16
MESH = pl.DeviceIdType.MESH

D_MODEL = 1024
DEPTH = 2
RMS_EPS = 1e-6
SSM_GROUPS, SSM_GROUP_CH, SSM_STATE = 32, 16, 64
SSM_WIDTH = SSM_GROUPS * SSM_GROUP_CH
SSM_LANES = SSM_GROUPS * SSM_STATE
SSM_PACK = 8
MLA_HEADS, MLA_Q_RANK, MLA_KV_RANK, MLA_NOPE, MLA_ROPE, MLA_V = 8, 512, 256, 64, 32, 64
ROPE_THETA = 10000.0
HG_HEADS, HG_DK, HG_DV, HG_CHUNK = 4, 128, 128, 64
HG_SUB_FWD, HG_SUB_BWD = 32, 16
X_HEADS, X_HEAD_DIM = 4, 128
D_FF = 2816
D_IN = 6432
N_CHIPS = 4
D_IN_SHARD = D_IN // N_CHIPS
D_IN_BLOCK = 1664
MASK_VALUE = -1e30

ADAM_LR, ADAM_B1, ADAM_B2, ADAM_EPS, ADAM_WD, ADAM_STEP = 0.001, 0.9, 0.999, 1e-08, 0.01, 10

VMEM_LIMIT = 48 * 1024 * 1024


def _pcall(body, **kw):
    return pl.pallas_call(body, **kw)


def _params(sem):
    return pltpu.CompilerParams(dimension_semantics=sem, vmem_limit_bytes=VMEM_LIMIT)


def _tile(n, cap, align=128):
    if n <= cap:
        return n
    t = (cap // align) * align
    while t >= align:
        if n % t == 0:
            return t
        t -= align
    return n


MM_VMEM_BUDGET = 36 * 1024 * 1024
MM_TILE_CAP = 2048


def _divisors(n, cap, align=128):
    out = [n] if n <= cap else []
    out += [t for t in range(align, min(n, cap + 1), align) if n % t == 0]
    return sorted(set(out), reverse=True)


@functools.lru_cache(maxsize=None)
def _mm_tiles(m, n, k, a_bytes, b_bytes, o_bytes, n_unit, k_unit):
    best = None
    for tm in _divisors(m, MM_TILE_CAP):
        for tn in _divisors(n_unit, MM_TILE_CAP):
            for tk in _divisors(k_unit, MM_TILE_CAP):
                nk = k // tk
                need = 2 * (tm * tk * a_bytes + tk * tn * b_bytes + tm * tn * o_bytes)
                need += tm * tn * 4 * (2 if nk > 1 else 1)
                need += (tm * tk * 2 if a_bytes > 2 else 0) + (tk * tn * 2 if b_bytes > 2 else 0)
                if need > MM_VMEM_BUDGET:
                    continue
                key = ((m // tm) * (n // tn) * nk, nk, -tn)
                if best is None or key < best[0]:
                    best = (key, (tm, tn, tk))
    assert best is not None, (m, n, k)
    return best[1]


def _mm(a, b, *, ta=False, tb=False, out_dtype=F32, name="mm", b_blocks=False, out_blocks=0, pro=None, epi=None,
        save_left=False):
    if ta:
        K, M = a.shape
    else:
        M, K = a.shape
    n_unit = k_unit = None
    if b_blocks:
        nb, rows, cols = b.shape
        if tb:
            N, Kb, k_unit = rows, nb * cols, cols
        else:
            Kb, N, n_unit = rows, nb * cols, cols
    elif tb:
        N, Kb = b.shape
    else:
        Kb, N = b.shape
    if out_blocks:
        assert not b_blocks and N % out_blocks == 0
        n_unit = N // out_blocks
    assert K == Kb, (a.shape, b.shape, ta, tb)
    n_pro = len(pro) - 1 if pro else 0
    n_epi = len(epi) - 1 if epi else 0
    tm, tn, tk = _mm_tiles(M, N, K, a.dtype.itemsize * (1 + n_pro), b.dtype.itemsize,
                           jnp.dtype(out_dtype).itemsize * max(1, 2 * n_epi), n_unit or N, k_unit or K)
    nk = K // tk
    dims = (((0,) if ta else (1,), (1,) if tb else (0,)), ((), ()))
    n_in = 2 + n_pro + n_epi
    n_out = max(1, n_epi) + (1 if save_left else 0)
    assert not save_left or (pro and tn == N)

    def body(*refs):
        ins, outs, acc = refs[:n_in], refs[n_in:n_in + n_out], refs[n_in + n_out:]
        b_ref = ins[1 + n_pro]
        av = (pro[0](*[r[...] for r in ins[:1 + n_pro]]) if pro else ins[0][...]).astype(BF16)
        if save_left:
            outs[-1][...] = av
        bv = b_ref[...].astype(BF16)
        part = lax.dot_general(av, bv, dims, preferred_element_type=F32)

        def finish(total):
            if epi:
                for o_ref, r in zip(outs, epi[0](total, *[e[...] for e in ins[2 + n_pro:]])):
                    o_ref[...] = r.astype(o_ref.dtype)
            else:
                outs[0][...] = total.astype(outs[0].dtype)

        if nk == 1:
            finish(part)
        else:
            acc_ref = acc[0]
            k = pl.program_id(2)

            @pl.when(k == 0)
            def _():
                acc_ref[...] = part

            @pl.when(k > 0)
            def _():
                acc_ref[...] += part

            @pl.when(k == nk - 1)
            def _():
                finish(acc_ref[...])

    a_spec = pl.BlockSpec((tk, tm), lambda i, j, k: (k, i)) if ta else pl.BlockSpec((tm, tk), lambda i, j, k: (i, k))
    if b_blocks and tb:
        r = k_unit // tk
        b_spec = pl.BlockSpec((None, tn, tk), lambda i, j, k: (k // r, j, k % r))
    elif b_blocks:
        r = n_unit // tn
        b_spec = pl.BlockSpec((None, tk, tn), lambda i, j, k: (j // r, k, j % r))
    elif tb:
        b_spec = pl.BlockSpec((tn, tk), lambda i, j, k: (j, k))
    else:
        b_spec = pl.BlockSpec((tk, tn), lambda i, j, k: (k, j))
    if out_blocks:
        r = n_unit // tn
        out_spec = pl.BlockSpec((None, tm, tn), lambda i, j, k: (j // r, i, j % r))
        out_shape = jax.ShapeDtypeStruct((out_blocks, M, n_unit), out_dtype)
    else:
        out_spec = pl.BlockSpec((tm, tn), lambda i, j, k: (i, j))
        out_shape = jax.ShapeDtypeStruct((M, N), out_dtype)
    in_specs = [a_spec] * (1 + n_pro) + [b_spec] + [out_spec] * n_epi
    args = [a] + (list(pro[1:]) if pro else []) + [b] + (list(epi[1:]) if epi else [])
    out_specs, out_shapes = ([out_spec] * n_epi, [out_shape] * n_epi) if epi else (out_spec, out_shape)
    if save_left:
        out_specs = [out_spec] * max(1, n_epi) + [a_spec]
        out_shapes = [out_shape] * max(1, n_epi) + [jax.ShapeDtypeStruct(a.shape, BF16)]
    return _pcall(
        body,
        name=name,
        grid=(M // tm, N // tn, nk),
        in_specs=in_specs,
        out_specs=out_specs,
        out_shape=out_shapes,
        scratch_shapes=[pltpu.VMEM((tm, tn), F32)] if nk > 1 else [],
        compiler_params=_params(("parallel", "parallel", "arbitrary")),
    )(*args)


def _bd_call(a, bc, transpose, name):
    m = a.shape[0]
    nb, ka, kb = bc.shape
    win, wout = (kb, ka) if transpose else (ka, kb)
    assert a.shape[1] == nb * win
    tm = _tile(m, 1024)
    dims = _NT if transpose else (((1,), (0,)), ((), ()))

    def body(a_ref, b_ref, o_ref):
        o_ref[...] = lax.dot_general(a_ref[...].astype(BF16), b_ref[0].astype(BF16), dims, preferred_element_type=F32)

    return _pcall(
        body,
        name=name,
        grid=(m // tm, nb),
        in_specs=[pl.BlockSpec((tm, win), lambda i, j: (i, j)), pl.BlockSpec((1, ka, kb), lambda i, j: (j, 0, 0))],
        out_specs=pl.BlockSpec((tm, wout), lambda i, j: (i, j)),
        out_shape=jax.ShapeDtypeStruct((m, nb * wout), F32),
        compiler_params=_params(("parallel", "parallel")),
    )(a, bc)


def _bd_dw_call(a, g, nb, name):
    m = a.shape[0]
    ka, kb = a.shape[1] // nb, g.shape[1] // nb
    tk = _tile(m, 1024)
    nk = m // tk

    def body(a_ref, g_ref, o_ref):
        part = lax.dot_general(a_ref[...].astype(BF16), g_ref[...].astype(BF16), _TN, preferred_element_type=F32)
        k = pl.program_id(1)

        @pl.when(k == 0)
        def _():
            o_ref[0] = part

        @pl.when(k > 0)
        def _():
            o_ref[0] += part

    return _pcall(
        body,
        name=name,
        grid=(nb, nk),
        in_specs=[pl.BlockSpec((tk, ka), lambda j, k: (k, j)), pl.BlockSpec((tk, kb), lambda j, k: (k, j))],
        out_specs=pl.BlockSpec((1, ka, kb), lambda j, k: (j, 0, 0)),
        out_shape=jax.ShapeDtypeStruct((nb, ka, kb), F32),
        compiler_params=_params(("parallel", "arbitrary")),
    )(a, g)


@functools.partial(jax.custom_vjp, nondiff_argnums=(2,))
def bd_matmul(a, bc, name):
    return _bd_call(a, bc, False, name + "_f")


def _bd_matmul_fwd(a, bc, name):
    return _bd_call(a, bc, False, name + "_f"), (a, bc)


def _bd_matmul_bwd(name, res, g):
    a, bc = res
    return _bd_call(g, bc, True, name + "_da"), _bd_dw_call(a, g, bc.shape[0], name + "_dw")


bd_matmul.defvjp(_bd_matmul_fwd, _bd_matmul_bwd)


@functools.partial(jax.custom_vjp, nondiff_argnums=(3,))
def matmul_w(a, w16, tok, name):
    return _mm(a, w16, b_blocks=w16.ndim == 3, name=name + "_f")


def _matmul_w_fwd(a, w16, tok, name):
    return _mm(a, w16, b_blocks=w16.ndim == 3, name=name + "_f"), (a, w16)


def _matmul_w_bwd(name, res, g):
    a, w16 = res
    blocks = w16.shape[0] if w16.ndim == 3 else 0
    return (_mm(g, w16, tb=True, b_blocks=bool(blocks), name=name + "_da"), jnp.zeros_like(w16),
            _mm(a, g, ta=True, out_blocks=blocks, name=name + "_dw"))


matmul_w.defvjp(_matmul_w_fwd, _matmul_w_bwd)


def _swiglu(gt, up):
    return gt * jax.nn.sigmoid(gt) * up


def _swiglu_bwd(dact, gt, up):
    sg = jax.nn.sigmoid(gt)
    return dact * up * sg * (1.0 + gt * (1.0 - sg)), dact * gt * sg


def _add_residual(product, x):
    return (product + x,)


def _silu_gate(o, g):
    return o * (g * jax.nn.sigmoid(g))


def _silu_gate_bwd(dm, o, g):
    sg = jax.nn.sigmoid(g)
    return dm * g * sg, dm * o * sg * (1.0 + g * (1.0 - sg))


@functools.partial(jax.custom_vjp, nondiff_argnums=(4,))
def silu_gated_out(o, g, w16, tok, name):
    return _mm(o, w16, pro=(_silu_gate, g), name=name + "_f")


def _silu_gated_out_fwd(o, g, w16, tok, name):
    out, left16 = _mm(o, w16, pro=(_silu_gate, g), save_left=True, name=name + "_f")
    return out, (o, g, w16, left16)


def _silu_gated_out_bwd(name, res, d):
    o, g, w16, left16 = res
    d_o, d_g = _mm(d, w16, tb=True, epi=(_silu_gate_bwd, o, g), name=name + "_da")
    return d_o, d_g, jnp.zeros_like(w16), _mm(left16, d, ta=True, name=name + "_dw")


silu_gated_out.defvjp(_silu_gated_out_fwd, _silu_gated_out_bwd)


@functools.partial(jax.custom_vjp, nondiff_argnums=(5,))
def swiglu_down(x, gt, up, w16, tok, name):
    return _mm(gt, w16, pro=(_swiglu, up), epi=(_add_residual, x), name=name + "_f")[0]


def _swiglu_down_fwd(x, gt, up, w16, tok, name):
    out, act16 = _mm(gt, w16, pro=(_swiglu, up), epi=(_add_residual, x), save_left=True, name=name + "_f")
    return out, (gt, up, w16, act16)


def _swiglu_down_bwd(name, res, g):
    gt, up, w16, act16 = res
    d_gt, d_up = _mm(g, w16, tb=True, epi=(_swiglu_bwd, gt, up), name=name + "_da")
    return g, d_gt, d_up, jnp.zeros_like(w16), _mm(act16, g, ta=True, name=name + "_dw")


swiglu_down.defvjp(_swiglu_down_fwd, _swiglu_down_bwd)


def _merge(y1, y2, y3, l1, l2, l3):
    return jax.nn.sigmoid(l1) * y1 + jax.nn.sigmoid(l2) * y2 + jax.nn.sigmoid(l3) * y3


def _merge_bwd(dm, y1, y2, y3, l1, l2, l3):
    dys, dls = [], []
    for y, logit in ((y1, l1), (y2, l2), (y3, l3)):
        s = jax.nn.sigmoid(logit)
        dys.append(dm * s)
        dls.append(dm * y * s * (1.0 - s))
    return (*dys, *dls)


@functools.partial(jax.custom_vjp, nondiff_argnums=(5,))
def gated_merge_out(x, ys, logits, w16, tok, name):
    return _mm(ys[0], w16, pro=(_merge, *ys[1:], *logits), epi=(_add_residual, x), name=name + "_f")[0]


def _gated_merge_out_fwd(x, ys, logits, w16, tok, name):
    out, merged16 = _mm(ys[0], w16, pro=(_merge, *ys[1:], *logits), epi=(_add_residual, x), save_left=True,
                        name=name + "_f")
    return out, (ys, logits, w16, merged16)


def _gated_merge_out_bwd(name, res, g):
    ys, logits, w16, merged16 = res
    d = _mm(g, w16, tb=True, epi=(_merge_bwd, *ys, *logits), name=name + "_da")
    return g, tuple(d[:3]), tuple(d[3:]), jnp.zeros_like(w16), _mm(merged16, g, ta=True, name=name + "_dw")


gated_merge_out.defvjp(_gated_merge_out_fwd, _gated_merge_out_bwd)


@functools.partial(jax.custom_vjp, nondiff_argnums=(3,))
def matmul_w_t(a, w16, tok, name):
    return _mm(w16, a, ta=True, tb=True, name=name + "_f")


def _matmul_w_t_fwd(a, w16, tok, name):
    return _mm(w16, a, ta=True, tb=True, name=name + "_f"), (a, w16)


def _matmul_w_t_bwd(name, res, g):
    a, w16 = res
    return (_mm(g, w16, ta=True, tb=True, name=name + "_da"), jnp.zeros_like(w16),
            _mm(a, g, ta=True, tb=True, name=name + "_dw"))


matmul_w_t.defvjp(_matmul_w_t_fwd, _matmul_w_t_bwd)


@functools.partial(jax.custom_vjp, nondiff_argnums=(3,))
def matmul_w_at(at, w16, tok, name):
    return _mm(at, w16, ta=True, name=name + "_f")


def _matmul_w_at_fwd(at, w16, tok, name):
    return _mm(at, w16, ta=True, name=name + "_f"), (at, w16)


def _matmul_w_at_bwd(name, res, g):
    at, w16 = res
    return _mm(w16, g, tb=True, name=name + "_da"), jnp.zeros_like(w16), _mm(at, g, name=name + "_dw")


matmul_w_at.defvjp(_matmul_w_at_fwd, _matmul_w_at_bwd)


@functools.partial(jax.custom_vjp, nondiff_argnums=(4,))
def matmul_w_at_res(x, at, w16, tok, name):
    return _mm(at, w16, ta=True, epi=(_add_residual, x), name=name + "_f")[0]


def _matmul_w_at_res_fwd(x, at, w16, tok, name):
    return _mm(at, w16, ta=True, epi=(_add_residual, x), name=name + "_f")[0], (at, w16)


def _matmul_w_at_res_bwd(name, res, g):
    return (g, *_matmul_w_at_bwd(name, res, g))


matmul_w_at_res.defvjp(_matmul_w_at_res_fwd, _matmul_w_at_res_bwd)


def _rms_fwd_call(x, g, name):
    rows, d = x.shape
    tr = _tile(rows, 512, 8)

    def body(x_ref, g_ref, y_ref):
        xv = x_ref[...]
        rstd = lax.rsqrt(jnp.mean(xv * xv, axis=-1, keepdims=True) + RMS_EPS)
        y_ref[...] = xv * rstd * g_ref[...]

    return _pcall(
        body,
        name=name,
        grid=(rows // tr,),
        in_specs=[pl.BlockSpec((tr, d), lambda i: (i, 0)), pl.BlockSpec((1, d), lambda i: (0, 0))],
        out_specs=pl.BlockSpec((tr, d), lambda i: (i, 0)),
        out_shape=jax.ShapeDtypeStruct((rows, d), F32),
        compiler_params=_params(("parallel",)),
    )(x, g.reshape(1, d))


def _rms_bwd_call(x, g, dy, name):
    rows, d = x.shape
    tr = _tile(rows, 512, 8)
    nb = rows // tr

    def body(x_ref, g_ref, dy_ref, dx_ref, dg_ref):
        xv = x_ref[...]
        rstd = lax.rsqrt(jnp.mean(xv * xv, axis=-1, keepdims=True) + RMS_EPS)
        xh = xv * rstd
        dyv = dy_ref[...]
        dyg = dyv * g_ref[...]
        dx_ref[...] = rstd * (dyg - xh * jnp.mean(dyg * xh, axis=-1, keepdims=True))
        dg_ref[...] = jnp.sum((dyv * xh).reshape(tr // 8, 8, d), axis=0)

    dx, dgp = _pcall(
        body,
        name=name,
        grid=(nb,),
        in_specs=[
            pl.BlockSpec((tr, d), lambda i: (i, 0)),
            pl.BlockSpec((1, d), lambda i: (0, 0)),
            pl.BlockSpec((tr, d), lambda i: (i, 0)),
        ],
        out_specs=[pl.BlockSpec((tr, d), lambda i: (i, 0)), pl.BlockSpec((8, d), lambda i: (i, 0))],
        out_shape=[jax.ShapeDtypeStruct((rows, d), F32), jax.ShapeDtypeStruct((nb * 8, d), F32)],
        compiler_params=_params(("parallel",)),
    )(x, g.reshape(1, d), dy)
    return dx, jnp.sum(dgp, axis=0)


@functools.partial(jax.custom_vjp, nondiff_argnums=(2,))
def rmsnorm(x, g, name):
    return _rms_fwd_call(x, g, name + "_f")


def _rmsnorm_fwd(x, g, name):
    return _rms_fwd_call(x, g, name + "_f"), (x, g)


def _rmsnorm_bwd(name, res, dy):
    x, g = res
    return _rms_bwd_call(x, g, dy, name + "_b")


rmsnorm.defvjp(_rmsnorm_fwd, _rmsnorm_bwd)


def _attn_blocks(sq, sk):
    return _tile(sq, 1024), _tile(sk, 512)


def _attn_pt(qs, k_ref, qi, ki, bq, bk, masked, shift):
    st = jnp.dot(k_ref[0].astype(BF16), qs, preferred_element_type=F32)
    if masked:
        kpos = ki * bk + lax.broadcasted_iota(jnp.int32, (bk, bq), 0)
        qpos = qi * bq + lax.broadcasted_iota(jnp.int32, (bk, bq), 1)
        st = jnp.where(kpos <= qpos, st, MASK_VALUE)
    return st if shift is None else jnp.exp(st - shift)


def _attn_pairs(nq, nk, bq, bk, causal, k_major):
    ok = lambda i, j: (not causal) or j * bk <= i * bq + bq - 1
    if k_major:
        pairs = [(i, j) for j in range(nk) for i in range(nq) if ok(i, j)]
    else:
        pairs = [(i, j) for i in range(nq) for j in range(nk) if ok(i, j)]
    qtab = jnp.asarray([p[0] for p in pairs], jnp.int32)
    ktab = jnp.asarray([p[1] for p in pairs], jnp.int32)
    first_q = (lambda j: (j * bk) // bq) if causal else (lambda j: 0)
    last_k = (lambda i: jnp.minimum((i * bq + bq - 1) // bk, nk - 1)) if causal else (lambda i: nk - 1)
    return qtab, ktab, first_q, last_k


def _attn_cases(causal, qi, ki, bq, bk, step):
    if not causal:
        step(False)
        return
    pl.when(ki * bk + bk - 1 <= qi * bq)(functools.partial(step, False))
    pl.when(ki * bk + bk - 1 > qi * bq)(functools.partial(step, True))


def _attn_fwd_call(qt, k, vt, scale, causal, name):
    h, sk, dq = k.shape
    sq = qt.shape[1]
    dv = vt.shape[0] // h
    bq, bk = _attn_blocks(sq, sk)
    nq, nk = sq // bq, sk // bk

    qtab, ktab, _, last_k = _attn_pairs(nq, nk, bq, bk, causal, False)

    def body(qi_ref, ki_ref, q_ref, k_ref, v_ref, o_ref, lse_ref, m_sc, l_sc, acc_sc):
        qi, ki = qi_ref[pl.program_id(1)], ki_ref[pl.program_id(1)]

        @pl.when(ki == 0)
        def _():
            m_sc[...] = jnp.full(m_sc.shape, -jnp.inf, F32)
            l_sc[...] = jnp.zeros(l_sc.shape, F32)
            acc_sc[...] = jnp.zeros(acc_sc.shape, F32)

        def step(masked):
            qs = (q_ref[...] * scale).astype(BF16)
            st = _attn_pt(qs, k_ref, qi, ki, bq, bk, masked, None)
            m_prev = m_sc[...]
            m_new = jnp.maximum(m_prev, jnp.max(st, axis=0, keepdims=True))
            alpha = jnp.exp(m_prev - m_new)
            pt = jnp.exp(st - m_new)
            l_sc[...] = alpha * l_sc[...] + jnp.sum(pt, axis=0, keepdims=True)
            acc_sc[...] = alpha * acc_sc[...] + jnp.dot(v_ref[...].astype(BF16), pt.astype(BF16),
                                                        preferred_element_type=F32)
            m_sc[...] = m_new

        _attn_cases(causal, qi, ki, bq, bk, step)

        @pl.when(ki == last_k(qi))
        def _():
            o_ref[...] = acc_sc[...] / l_sc[...]
            lse_ref[0] = m_sc[...] + jnp.log(l_sc[...])

    qmap = lambda hh, p, qi_ref, ki_ref: (hh, qi_ref[p])
    return _pcall(
        body,
        name=name,
        grid_spec=pltpu.PrefetchScalarGridSpec(
            num_scalar_prefetch=2,
            grid=(h, qtab.shape[0]),
            in_specs=[
                pl.BlockSpec((dq, bq), qmap),
                pl.BlockSpec((1, bk, dq), lambda hh, p, qi_ref, ki_ref: (hh, ki_ref[p], 0)),
                pl.BlockSpec((dv, bk), lambda hh, p, qi_ref, ki_ref: (hh, ki_ref[p])),
            ],
            out_specs=[pl.BlockSpec((dv, bq), qmap),
                       pl.BlockSpec((1, 1, bq), lambda hh, p, qi_ref, ki_ref: (hh, 0, qi_ref[p]))],
            scratch_shapes=[pltpu.VMEM((1, bq), F32), pltpu.VMEM((1, bq), F32), pltpu.VMEM((dv, bq), F32)],
        ),
        out_shape=[jax.ShapeDtypeStruct((h * dv, sq), F32), jax.ShapeDtypeStruct((h, 1, sq), F32)],
        compiler_params=_params(("parallel", "arbitrary")),
    )(qtab, ktab, qt, k, vt)


_TN = (((0,), (0,)), ((), ()))
_NT = (((1,), (1,)), ((), ()))


def _attn_dst(qs, k_ref, v_ref, do_ref, lse_ref, dl_ref, qi, ki, bq, bk, masked):
    pt = _attn_pt(qs, k_ref, qi, ki, bq, bk, masked, lse_ref[0])
    dpt = lax.dot_general(v_ref[...].astype(BF16), do_ref[...].astype(BF16), _TN, preferred_element_type=F32)
    return pt, pt * (dpt - dl_ref[0])


def _attn_dq_call(qt, k, vt, dot, lse, delta, scale, causal, name):
    h, sk, dq = k.shape
    sq = qt.shape[1]
    dv = vt.shape[0] // h
    bq, bk = _attn_blocks(sq, sk)
    nq, nk = sq // bq, sk // bk

    qtab, ktab, _, last_k = _attn_pairs(nq, nk, bq, bk, causal, False)

    def body(qi_ref, ki_ref, q_ref, k_ref, v_ref, do_ref, lse_ref, dl_ref, dq_ref, acc_sc):
        qi, ki = qi_ref[pl.program_id(1)], ki_ref[pl.program_id(1)]

        @pl.when(ki == 0)
        def _():
            acc_sc[...] = jnp.zeros(acc_sc.shape, F32)

        def step(masked):
            qs = (q_ref[...] * scale).astype(BF16)
            _, dst = _attn_dst(qs, k_ref, v_ref, do_ref, lse_ref, dl_ref, qi, ki, bq, bk, masked)
            acc_sc[...] += lax.dot_general(k_ref[0].astype(BF16), dst.astype(BF16), _TN, preferred_element_type=F32)

        _attn_cases(causal, qi, ki, bq, bk, step)

        @pl.when(ki == last_k(qi))
        def _():
            dq_ref[...] = acc_sc[...] * scale

    qmap = lambda hh, p, qi_ref, ki_ref: (hh, qi_ref[p])
    rowmap = lambda hh, p, qi_ref, ki_ref: (hh, 0, qi_ref[p])
    return _pcall(
        body,
        name=name,
        grid_spec=pltpu.PrefetchScalarGridSpec(
            num_scalar_prefetch=2,
            grid=(h, qtab.shape[0]),
            in_specs=[
                pl.BlockSpec((dq, bq), qmap),
                pl.BlockSpec((1, bk, dq), lambda hh, p, qi_ref, ki_ref: (hh, ki_ref[p], 0)),
                pl.BlockSpec((dv, bk), lambda hh, p, qi_ref, ki_ref: (hh, ki_ref[p])),
                pl.BlockSpec((dv, bq), qmap),
                pl.BlockSpec((1, 1, bq), rowmap),
                pl.BlockSpec((1, 1, bq), rowmap),
            ],
            out_specs=pl.BlockSpec((dq, bq), qmap),
            scratch_shapes=[pltpu.VMEM((dq, bq), F32)],
        ),
        out_shape=jax.ShapeDtypeStruct((h * dq, sq), F32),
        compiler_params=_params(("parallel", "arbitrary")),
    )(qtab, ktab, qt, k, vt, dot, lse, delta)


def _attn_dkv_call(qt, k, vt, dot, lse, delta, scale, causal, name):
    h, sk, dq = k.shape
    sq = qt.shape[1]
    dv = vt.shape[0] // h
    bq, bk = _attn_blocks(sq, sk)
    nq, nk = sq // bq, sk // bk

    qtab, ktab, first_q, _ = _attn_pairs(nq, nk, bq, bk, causal, True)

    def body(qi_ref, ki_ref, q_ref, k_ref, v_ref, do_ref, lse_ref, dl_ref, dk_ref, dv_ref, dk_sc, dv_sc):
        qi, ki = qi_ref[pl.program_id(1)], ki_ref[pl.program_id(1)]

        @pl.when(qi == first_q(ki))
        def _():
            dk_sc[...] = jnp.zeros(dk_sc.shape, F32)
            dv_sc[...] = jnp.zeros(dv_sc.shape, F32)

        def step(masked):
            qs = (q_ref[...] * scale).astype(BF16)
            pt, dst = _attn_dst(qs, k_ref, v_ref, do_ref, lse_ref, dl_ref, qi, ki, bq, bk, masked)
            dv_sc[...] += lax.dot_general(do_ref[...].astype(BF16), pt.astype(BF16), _NT, preferred_element_type=F32)
            dk_sc[...] += lax.dot_general(dst.astype(BF16), qs, _NT, preferred_element_type=F32)

        _attn_cases(causal, qi, ki, bq, bk, step)

        @pl.when(qi == nq - 1)
        def _():
            dk_ref[0] = dk_sc[...]
            dv_ref[...] = dv_sc[...]

    qmap = lambda hh, p, qi_ref, ki_ref: (hh, qi_ref[p])
    rowmap = lambda hh, p, qi_ref, ki_ref: (hh, 0, qi_ref[p])
    kmap = lambda hh, p, qi_ref, ki_ref: (hh, ki_ref[p], 0)
    vmap = lambda hh, p, qi_ref, ki_ref: (hh, ki_ref[p])
    return _pcall(
        body,
        name=name,
        grid_spec=pltpu.PrefetchScalarGridSpec(
            num_scalar_prefetch=2,
            grid=(h, qtab.shape[0]),
            in_specs=[
                pl.BlockSpec((dq, bq), qmap),
                pl.BlockSpec((1, bk, dq), kmap),
                pl.BlockSpec((dv, bk), vmap),
                pl.BlockSpec((dv, bq), qmap),
                pl.BlockSpec((1, 1, bq), rowmap),
                pl.BlockSpec((1, 1, bq), rowmap),
            ],
            out_specs=[pl.BlockSpec((1, bk, dq), kmap), pl.BlockSpec((dv, bk), vmap)],
            scratch_shapes=[pltpu.VMEM((bk, dq), F32), pltpu.VMEM((dv, bk), F32)],
        ),
        out_shape=[jax.ShapeDtypeStruct((h, sk, dq), F32), jax.ShapeDtypeStruct((h * dv, sk), F32)],
        compiler_params=_params(("parallel", "arbitrary")),
    )(qtab, ktab, qt, k, vt, dot, lse, delta)


@functools.partial(jax.custom_vjp, nondiff_argnums=(3, 4, 5))
def attention(qt, k, vt, scale, causal, name):
    return _attn_fwd_call(qt, k, vt, scale, causal, name + "_f")[0]


def _attention_fwd(qt, k, vt, scale, causal, name):
    ot, lse = _attn_fwd_call(qt, k, vt, scale, causal, name + "_f")
    return ot, (qt, k, vt, ot, lse)


def _attention_bwd(scale, causal, name, res, dot):
    qt, k, vt, ot, lse = res
    h = k.shape[0]
    delta = jnp.sum((dot * ot).reshape(h, ot.shape[0] // h, ot.shape[1]), axis=1, keepdims=True)
    dqt = _attn_dq_call(qt, k, vt, dot, lse, delta, scale, causal, name + "_dq")
    dk, dvt = _attn_dkv_call(qt, k, vt, dot, lse, delta, scale, causal, name + "_dkv")
    return dqt, dk, dvt


attention.defvjp(_attention_fwd, _attention_bwd)


SCAN_ROWS = 8
SCAN_LANES = 256
SCAN_TBLOCK = 512
SCAN_UNROLL = 4


def _scan_tables(lr, li):
    pr, pi = [lr], [li]
    for _ in range(SCAN_ROWS - 1):
        pr, pi = pr + [pr[-1] * lr - pi[-1] * li], pi + [pr[-1] * li + pi[-1] * lr]
    return jnp.concatenate(pr, axis=0), jnp.concatenate(pi, axis=0)


def _scan_call(ur, ui, lr, li, reverse, name, states=None):
    s, n = ur.shape
    tb = _tile(s, SCAN_TBLOCK, 8)
    nt, nl = s // tb, n // SCAN_LANES
    ntile = tb // SCAN_ROWS
    pr, pi = _scan_tables(lr, li)
    if reverse:
        pr, pi = pr[::-1], pi[::-1]
    shape = (SCAN_ROWS, SCAN_LANES)
    with_states = states is not None
    assert reverse or not with_states

    def body(*refs):
        if with_states:
            ur_ref, ui_ref, pr_ref, pi_ref, sr_ref, si_ref, xr_ref, xi_ref, dr_ref, di_ref, cr_sc, ci_sc, ar_sc, ai_sc = refs
        else:
            ur_ref, ui_ref, pr_ref, pi_ref, xr_ref, xi_ref, cr_sc, ci_sc, ar_sc, ai_sc = refs

        @pl.when(pl.program_id(1) == 0)
        def _():
            for sc in (cr_sc, ci_sc, ar_sc, ai_sc):
                sc[...] = jnp.zeros(shape, F32)

        prv, piv = pr_ref[...], pi_ref[...]
        rows = lax.broadcasted_iota(jnp.int32, shape, 0)

        def powers(k):
            r = (SCAN_ROWS - k) if reverse else (k - 1)
            return jnp.broadcast_to(prv[r:r + 1], shape), jnp.broadcast_to(piv[r:r + 1], shape)

        pw = [powers(k) for k in (1, 2, 4)]

        def tile(i, carry):
            cr, ci, acr, aci = carry
            j = (ntile - 1 - i) if reverse else i
            sl = pl.ds(pl.multiple_of(j * SCAN_ROWS, SCAN_ROWS), SCAN_ROWS)
            xr, xi = ur_ref[sl, :], ui_ref[sl, :]
            for (ar, ai), k in zip(pw, (1, 2, 4)):
                if reverse:
                    keep = rows < SCAN_ROWS - k
                    sr = jnp.where(keep, pltpu.roll(xr, SCAN_ROWS - k, 0), 0.0)
                    si = jnp.where(keep, pltpu.roll(xi, SCAN_ROWS - k, 0), 0.0)
                else:
                    keep = rows >= k
                    sr = jnp.where(keep, pltpu.roll(xr, k, 0), 0.0)
                    si = jnp.where(keep, pltpu.roll(xi, k, 0), 0.0)
                xr, xi = xr + (ar * sr - ai * si), xi + (ar * si + ai * sr)
            xr, xi = xr + (prv * cr - piv * ci), xi + (prv * ci + piv * cr)
            xr_ref[sl, :] = xr
            xi_ref[sl, :] = xi
            if with_states:
                last = rows == SCAN_ROWS - 1
                nr = jnp.where(last, cr, pltpu.roll(xr, SCAN_ROWS - 1, 0))
                ni = jnp.where(last, ci, pltpu.roll(xi, SCAN_ROWS - 1, 0))
                sr, si = sr_ref[sl, :], si_ref[sl, :]
                acr, aci = acr + (sr * nr + si * ni), aci + (sr * ni - si * nr)
            e = 0 if reverse else SCAN_ROWS - 1
            return jnp.broadcast_to(xr[e:e + 1], shape), jnp.broadcast_to(xi[e:e + 1], shape), acr, aci

        def tiles(g, carry):
            for u in range(SCAN_UNROLL):
                carry = tile(g * SCAN_UNROLL + u, carry)
            return carry

        assert ntile % SCAN_UNROLL == 0
        cr, ci, acr, aci = lax.fori_loop(0, ntile // SCAN_UNROLL, tiles,
                                         (cr_sc[...], ci_sc[...], ar_sc[...], ai_sc[...]))
        cr_sc[...] = cr
        ci_sc[...] = ci
        if with_states:
            ar_sc[...] = acr
            ai_sc[...] = aci

            @pl.when(pl.program_id(1) == nt - 1)
            def _():
                dr_ref[...] = acr
                di_ref[...] = aci

    tmap = (lambda l, t: (nt - 1 - t, l)) if reverse else (lambda l, t: (t, l))
    blk = pl.BlockSpec((tb, SCAN_LANES), tmap)
    tab = pl.BlockSpec((SCAN_ROWS, SCAN_LANES), lambda l, t: (0, l))
    full = jax.ShapeDtypeStruct((s, n), F32)
    return _pcall(
        body,
        name=name,
        grid=(nl, nt),
        in_specs=[blk, blk, tab, tab] + ([blk, blk] if with_states else []),
        out_specs=[blk, blk] + ([tab, tab] if with_states else []),
        out_shape=[full, full] + ([jax.ShapeDtypeStruct((SCAN_ROWS, n), F32)] * 2 if with_states else []),
        scratch_shapes=[pltpu.VMEM(shape, F32)] * 4,
        compiler_params=_params(("parallel", "arbitrary")),
    )(ur, ui, pr, pi, *(states if with_states else ()))


@functools.partial(jax.custom_vjp, nondiff_argnums=(4,))
def s5_scan(ur, ui, lr, li, name):
    return tuple(_scan_call(ur, ui, lr, li, False, name + "_f"))


def _s5_scan_fwd(ur, ui, lr, li, name):
    xr, xi = _scan_call(ur, ui, lr, li, False, name + "_f")
    return (xr, xi), (xr, xi, lr, li)


def _s5_scan_bwd(name, res, g):
    xr, xi, lr, li = res
    gr, gi, dlr, dli = _scan_call(g[0], g[1], lr, -li, True, name + "_b", states=(xr, xi))
    return gr, gi, jnp.sum(dlr, axis=0, keepdims=True), jnp.sum(dli, axis=0, keepdims=True)


s5_scan.defvjp(_s5_scan_fwd, _s5_scan_bwd)


def _running_sum(x, reverse):
    n = x.shape[0]
    rows = lax.broadcasted_iota(jnp.int32, x.shape, 0)
    k = 1
    while k < n:
        if reverse:
            x = x + jnp.where(rows < n - k, pltpu.roll(x, n - k, 0), 0.0)
        else:
            x = x + jnp.where(rows >= k, pltpu.roll(x, k, 0), 0.0)
        k *= 2
    return x


@jax.custom_vjp
def cumsum_rows(x):
    return _running_sum(x, False)


cumsum_rows.defvjp(lambda x: (_running_sum(x, False), None), lambda _, g: (_running_sum(g, True),))


def _hg_chunk(HG_SUB, q, k, v, lf, st):
    c = HG_CHUNK
    b = cumsum_rows(lf)
    qb = q * jnp.exp(b)
    o = lax.dot_general(qb.astype(BF16), st.astype(BF16), (((1,), (1,)), ((), ())), preferred_element_type=F32)
    nsub = c // HG_SUB
    srow = lax.broadcasted_iota(jnp.int32, (HG_SUB, HG_SUB, 1), 0)
    scol = lax.broadcasted_iota(jnp.int32, (HG_SUB, HG_SUB, 1), 1)
    smask = scol <= srow
    outs = []
    for i in range(nsub):
        lo = i * HG_SUB
        bi, qi, ki = b[lo:lo + HG_SUB], q[lo:lo + HG_SUB], k[lo:lo + HG_SUB]
        diff = jnp.where(smask, bi[:, None, :] - bi[None, :, :], 0.0)
        e = jnp.where(smask, jnp.exp(diff), 0.0)
        a_ii = jnp.sum(qi[:, None, :] * ki[None, :, :] * e, axis=-1)
        oi = jnp.dot(a_ii.astype(BF16), v[lo:lo + HG_SUB].astype(BF16), preferred_element_type=F32)
        if i > 0:
            r = b[lo - 1:lo]
            qt = qi * jnp.exp(bi - r)
            kt = k[:lo] * jnp.exp(r - b[:lo])
            a_ij = lax.dot_general(qt.astype(BF16), kt.astype(BF16), (((1,), (1,)), ((), ())),
                                   preferred_element_type=F32)
            oi = oi + jnp.dot(a_ij.astype(BF16), v[:lo].astype(BF16), preferred_element_type=F32)
        outs.append(oi)
    o = o + jnp.concatenate(outs, axis=0)
    bl = b[c - 1:c]
    kd = k * jnp.exp(bl - b)
    st_new = st * jnp.exp(bl) + lax.dot_general(v.astype(BF16), kd.astype(BF16), (((0,), (0,)), ((), ())),
                                                preferred_element_type=F32)
    return o, st_new


def _hg_fwd_call(q, k, v, lf, name):
    s, w = q.shape
    nc = s // HG_CHUNK
    blk = pl.BlockSpec((HG_CHUNK, w), lambda i: (i, 0))

    def body(q_ref, k_ref, v_ref, lf_ref, o_ref, st_ref, st_sc):
        @pl.when(pl.program_id(0) == 0)
        def _():
            st_sc[...] = jnp.zeros(st_sc.shape, F32)

        for hh in range(HG_HEADS):
            cs = slice(hh * HG_DK, (hh + 1) * HG_DK)
            st = st_sc[hh]
            st_ref[0, hh] = st
            o, st_new = _hg_chunk(HG_SUB_FWD, q_ref[:, cs], k_ref[:, cs], v_ref[:, cs], lf_ref[:, cs], st)
            o_ref[:, cs] = o
            st_sc[hh] = st_new

    return _pcall(
        body,
        name=name,
        grid=(nc,),
        in_specs=[blk, blk, blk, blk],
        out_specs=[blk, pl.BlockSpec((1, HG_HEADS, HG_DV, HG_DK), lambda i: (i, 0, 0, 0))],
        out_shape=[jax.ShapeDtypeStruct((s, w), F32), jax.ShapeDtypeStruct((nc, HG_HEADS, HG_DV, HG_DK), F32)],
        scratch_shapes=[pltpu.VMEM((HG_HEADS, HG_DV, HG_DK), F32)],
        compiler_params=_params(("arbitrary",)),
    )(q, k, v, lf)


def _hg_bwd_call(q, k, v, lf, sts, do, name):
    s, w = q.shape
    nc = s // HG_CHUNK
    blk = pl.BlockSpec((HG_CHUNK, w), lambda i: (nc - 1 - i, 0))

    def body(q_ref, k_ref, v_ref, lf_ref, st_ref, do_ref, dq_ref, dk_ref, dv_ref, dlf_ref, dst_sc):
        @pl.when(pl.program_id(0) == 0)
        def _():
            dst_sc[...] = jnp.zeros(dst_sc.shape, F32)

        def head(hh, carry):
            cs = pl.ds(pl.multiple_of(hh * HG_DK, HG_DK), HG_DK)
            _, vjp = jax.vjp(functools.partial(_hg_chunk, HG_SUB_BWD), q_ref[:, cs], k_ref[:, cs], v_ref[:, cs],
                             lf_ref[:, cs], st_ref[0, hh])
            dq, dk, dv, dlf, dst = vjp((do_ref[:, cs], dst_sc[hh]))
            dq_ref[:, cs] = dq
            dk_ref[:, cs] = dk
            dv_ref[:, cs] = dv
            dlf_ref[:, cs] = dlf
            dst_sc[hh] = dst
            return carry

        lax.fori_loop(0, HG_HEADS, head, 0)

    out = jax.ShapeDtypeStruct((s, w), F32)
    return _pcall(
        body,
        name=name,
        grid=(nc,),
        in_specs=[blk, blk, blk, blk, pl.BlockSpec((1, HG_HEADS, HG_DV, HG_DK), lambda i: (nc - 1 - i, 0, 0, 0)), blk],
        out_specs=[blk, blk, blk, blk],
        out_shape=[out, out, out, out],
        scratch_shapes=[pltpu.VMEM((HG_HEADS, HG_DV, HG_DK), F32)],
        compiler_params=_params(("arbitrary",)),
    )(q, k, v, lf, sts, do)


@functools.partial(jax.custom_vjp, nondiff_argnums=(4,))
def hgrn_core(q, k, v, lf, name):
    return _hg_fwd_call(q, k, v, lf, name + "_f")[0]


def _hgrn_core_fwd(q, k, v, lf, name):
    o, sts = _hg_fwd_call(q, k, v, lf, name + "_f")
    return o, (q, k, v, lf, sts)


def _hgrn_core_bwd(name, res, do):
    q, k, v, lf, sts = res
    return tuple(_hg_bwd_call(q, k, v, lf, sts, do, name + "_b"))


hgrn_core.defvjp(_hgrn_core_fwd, _hgrn_core_bwd)


def _adamw_call(w, g, m, v, name):
    rows, cols = w.shape
    tr = _tile(rows, 256, 8)
    c1 = 1.0 / (1.0 - ADAM_B1 ** ADAM_STEP)
    c2 = 1.0 / (1.0 - ADAM_B2 ** ADAM_STEP)

    def body(w_ref, g_ref, m_ref, v_ref, d_ref, mo_ref, vo_ref):
        gv = g_ref[...]
        mn = ADAM_B1 * m_ref[...] + (1.0 - ADAM_B1) * gv
        vn = ADAM_B2 * v_ref[...] + (1.0 - ADAM_B2) * (gv * gv)
        d_ref[...] = -ADAM_LR * ((mn * c1) / (jnp.sqrt(vn * c2) + ADAM_EPS) + ADAM_WD * w_ref[...])
        mo_ref[...] = mn
        vo_ref[...] = vn

    blk = pl.BlockSpec((tr, cols), lambda i: (i, 0))
    out = jax.ShapeDtypeStruct((rows, cols), F32)
    return _pcall(
        body,
        name=name,
        grid=(rows // tr,),
        in_specs=[blk, blk, blk, blk],
        out_specs=[blk, blk, blk],
        out_shape=[out, out, out],
        compiler_params=_params(("parallel",)),
    )(w, g, m, v)


ANY = pl.BlockSpec(memory_space=pl.ANY)


def _place():
    return lax.axis_index("x"), lax.axis_index("y"), lax.axis_index("c")


def _other_chips(x, y):
    return [(1 - x, y), (x, 1 - y), (1 - x, 1 - y)]


def _gather_weights(shards):
    nw = len(shards)

    def body(*refs):
        w_refs, g_refs, (send_sems, recv_sems) = refs[:nw], refs[nw:2 * nw], refs[2 * nw:]
        x, y, c = _place()
        myq = 2 * x + y
        sibling = (x, y, 1 - c)
        chips = _other_chips(x, y)

        def copy(k, src, dst, to):
            return pltpu.make_async_remote_copy(src_ref=src, dst_ref=dst, send_sem=send_sems.at[k],
                                                recv_sem=recv_sems.at[k], device_id=to, device_id_type=MESH)

        first = []
        for i in range(nw):
            for j, (px, py) in enumerate(chips):
                first.append(copy(6 * i + j, w_refs[i].at[c], g_refs[i].at[myq, c], (px, py, c)))
                first[-1].start()
        passed = []
        for i in range(nw):
            for j, (px, py) in enumerate(chips):
                q = 2 * px + py
                copy(6 * i + j, w_refs[i].at[c], g_refs[i].at[q, c], (px, py, c)).wait_recv()
                passed.append(copy(6 * i + 3 + j, g_refs[i].at[q, c], g_refs[i].at[q, c], sibling))
                passed[-1].start()
        for i in range(nw):
            for j, (px, py) in enumerate(chips):
                q = 2 * px + py
                copy(6 * i + 3 + j, g_refs[i].at[q, 1 - c], g_refs[i].at[q, 1 - c], sibling).wait_recv()
        for cp in first + passed:
            cp.wait_send()

    got = _pcall(
        body,
        name="gather_weights",
        in_specs=[ANY] * nw,
        out_specs=[ANY] * nw,
        out_shape=[jax.ShapeDtypeStruct((N_CHIPS,) + s.shape, s.dtype) for s in shards],
        scratch_shapes=[pltpu.SemaphoreType.DMA((6 * nw,)), pltpu.SemaphoreType.DMA((6 * nw,))],
    )(*shards)
    my_chip = 2 * lax.axis_index("x") + lax.axis_index("y")
    return [lax.dynamic_update_slice(g, s[None], (my_chip, 0, 0, 0)) for g, s in zip(got, shards)]


HBM = pl.BlockSpec(memory_space=pltpu.HBM)
SEM = pl.BlockSpec(memory_space=pltpu.SEMAPHORE)


def _gather_start(shards, after):
    nw = len(shards)

    def body(*refs):
        w_refs, land_refs = refs[:nw], refs[nw:2 * nw]
        send_sems, recv_sems = refs[2 * nw + 1:2 * nw + 3]
        token = refs[-1]
        x, y, c = _place()
        myq = 2 * x + y
        for i in range(nw):
            hh = shards[i].shape[0] // 2
            rows = pl.ds(c * hh, hh)
            for j, (px, py) in enumerate(_other_chips(x, y)):
                for tc in range(2):
                    pltpu.make_async_remote_copy(
                        src_ref=w_refs[i].at[rows], dst_ref=land_refs[i].at[myq, rows],
                        send_sem=send_sems.at[6 * i + 2 * j + tc], recv_sem=recv_sems.at[6 * i + 2 * j + c],
                        device_id=(px, py, tc), device_id_type=MESH).start()
        token[...] = jnp.zeros_like(token)

    lands = [lax.empty((N_CHIPS,) + s.shape, s.dtype) for s in shards]
    out = _pcall(
        body,
        name="gather_start",
        in_specs=[HBM] * (2 * nw) + [pl.BlockSpec(memory_space=pl.ANY)],
        out_specs=[SEM, SEM] + [HBM] * (2 * nw) + [pl.BlockSpec(memory_space=pltpu.VMEM)],
        out_shape=[pltpu.SemaphoreType.DMA((6 * nw,)), pltpu.SemaphoreType.DMA((6 * nw,))]
        + [pltpu.HBM(s.shape, s.dtype) for s in shards] + [pltpu.HBM(t.shape, t.dtype) for t in lands]
        + [jax.ShapeDtypeStruct((8, 128), F32)],
        input_output_aliases={i: 2 + i for i in range(2 * nw)},
        compiler_params=pltpu.CompilerParams(has_side_effects=pltpu.SideEffectType.DATAFLOW_SIDE_EFFECTING),
    )(*[pltpu.with_memory_space_constraint(t, pltpu.HBM) for t in list(shards) + lands], after)
    return out[0], out[1], out[2:2 + nw], out[2 + nw:2 + 2 * nw], out[-1]


def _gather_wait(send_sems, recv_sems, shards, lands, after):
    nw = len(shards)

    def body(*refs):
        w_refs, land_refs = refs[:nw], refs[nw:2 * nw]
        send_sems, recv_sems = refs[2 * nw:2 * nw + 2]
        x, y, c = _place()
        myq = 2 * x + y
        for i in range(nw):
            hh = shards[i].shape[0] // 2
            for j, (px, py) in enumerate(_other_chips(x, y)):
                q = 2 * px + py
                for other in range(2):
                    mine = pl.ds(c * hh, hh)
                    theirs = pl.ds(other * hh, hh)
                    pltpu.make_async_remote_copy(
                        src_ref=w_refs[i].at[mine], dst_ref=land_refs[i].at[myq, mine],
                        send_sem=send_sems.at[6 * i + 2 * j + other], recv_sem=recv_sems.at[6 * i + 2 * j + other],
                        device_id=(px, py, other), device_id_type=MESH).wait_send()
                    pltpu.make_async_remote_copy(
                        src_ref=w_refs[i].at[theirs], dst_ref=land_refs[i].at[q, theirs],
                        send_sem=send_sems.at[6 * i + 2 * j + other], recv_sem=recv_sems.at[6 * i + 2 * j + other],
                        device_id=(px, py, other), device_id_type=MESH).wait_recv()

    out = _pcall(
        body,
        name="gather_wait",
        in_specs=[HBM] * (2 * nw) + [SEM, SEM, pl.BlockSpec(memory_space=pl.ANY)],
        out_specs=[HBM] * (2 * nw),
        out_shape=[pltpu.HBM(s.shape, s.dtype) for s in shards] + [pltpu.HBM(t.shape, t.dtype) for t in lands],
        input_output_aliases={i: i for i in range(2 * nw)},
        compiler_params=pltpu.CompilerParams(has_side_effects=pltpu.SideEffectType.DATAFLOW_SIDE_EFFECTING),
    )(*shards, *lands, send_sems, recv_sems, after)
    my_chip = 2 * lax.axis_index("x") + lax.axis_index("y")
    return [lax.dynamic_update_slice(g, s[None], (my_chip, 0, 0)) for g, s in zip(out[nw:], out[:nw])]


def _swap_halves_to_sibling(grs, tag=""):
    nw = len(grs)

    def body(*refs):
        g_refs, t_refs, (send_sems, recv_sems) = refs[:nw], refs[nw:2 * nw], refs[2 * nw:]
        x, y, c = _place()
        cps = []
        for i in range(nw):
            hh = grs[i].shape[1] // 2
            cps.append(pltpu.make_async_remote_copy(
                src_ref=g_refs[i].at[:, pl.ds((1 - c) * hh, hh)], dst_ref=t_refs[i], send_sem=send_sems.at[i],
                recv_sem=recv_sems.at[i], device_id=(x, y, 1 - c), device_id_type=MESH))
            cps[-1].start()
        for cp in cps:
            cp.wait()

    return _pcall(
        body,
        name="grad_d2d" + tag,
        in_specs=[ANY] * nw,
        out_specs=[ANY] * nw,
        out_shape=[jax.ShapeDtypeStruct((g.shape[0], g.shape[1] // 2, g.shape[2]), g.dtype) for g in grs],
        scratch_shapes=[pltpu.SemaphoreType.DMA((nw,)), pltpu.SemaphoreType.DMA((nw,))],
    )(*grs)


def _add_half(gr, t1, place, name):
    nq, r, wd = gr.shape
    hh = r // 2
    tr = _tile(hh, 512, 16)
    nb = hh // tr

    def body(p_ref, g_ref, t_ref, o_ref):
        o_ref[...] = (g_ref[...] + t_ref[...]).astype(BF16)

    return _pcall(
        body,
        name=name,
        grid_spec=pltpu.PrefetchScalarGridSpec(
            num_scalar_prefetch=1,
            grid=(nq, nb),
            in_specs=[
                pl.BlockSpec((1, tr, wd), lambda q, i, p_ref: (q, p_ref[0] * nb + i, 0)),
                pl.BlockSpec((1, tr, wd), lambda q, i, p_ref: (q, i, 0)),
            ],
            out_specs=pl.BlockSpec((1, tr, wd), lambda q, i, p_ref: (q, i, 0)),
        ),
        out_shape=jax.ShapeDtypeStruct((nq, hh, wd), BF16),
        compiler_params=_params(("parallel", "parallel")),
    )(place, gr, t1)


def _scatter_copies(p_refs, t_refs, send_sems, recv_sems):
    x, y, c = _place()
    return [pltpu.make_async_remote_copy(
        src_ref=p_refs[i].at[2 * px + py], dst_ref=t_refs[i].at[j], send_sem=send_sems.at[3 * i + j],
        recv_sem=recv_sems.at[3 * i + j], device_id=(px, py, c), device_id_type=MESH)
        for i in range(len(p_refs)) for j, (px, py) in enumerate(_other_chips(x, y))]


def _scatter_start(ps, after, tag=""):
    nw = len(ps)

    def body(*refs):
        send_sems, recv_sems = refs[2 * nw + 1:2 * nw + 3]
        for cp in _scatter_copies(refs[:nw], refs[nw:2 * nw], send_sems, recv_sems):
            cp.start()
        refs[-1][...] = jnp.zeros_like(refs[-1])

    lands = [lax.empty((3,) + p.shape[1:], p.dtype) for p in ps]
    out = _pcall(
        body,
        name="scatter_start" + tag,
        in_specs=[HBM] * (2 * nw) + [pl.BlockSpec(memory_space=pl.ANY)],
        out_specs=[SEM, SEM] + [HBM] * (2 * nw) + [pl.BlockSpec(memory_space=pltpu.VMEM)],
        out_shape=[pltpu.SemaphoreType.DMA((3 * nw,)), pltpu.SemaphoreType.DMA((3 * nw,))]
        + [pltpu.HBM(t.shape, t.dtype) for t in list(ps) + lands] + [jax.ShapeDtypeStruct((8, 128), F32)],
        input_output_aliases={i: 2 + i for i in range(2 * nw)},
        compiler_params=pltpu.CompilerParams(has_side_effects=pltpu.SideEffectType.DATAFLOW_SIDE_EFFECTING),
    )(*[pltpu.with_memory_space_constraint(t, pltpu.HBM) for t in list(ps) + lands], after)
    return out[0], out[1], out[2:2 + nw], out[2 + nw:2 + 2 * nw], out[-1]


def _scatter_wait(send_sems, recv_sems, ps, lands, after, tag=""):
    nw = len(ps)

    def body(*refs):
        for cp in _scatter_copies(refs[:nw], refs[nw:2 * nw], refs[2 * nw], refs[2 * nw + 1]):
            cp.wait_send()
            cp.wait_recv()

    out = _pcall(
        body,
        name="scatter_wait" + tag,
        in_specs=[HBM] * (2 * nw) + [SEM, SEM, pl.BlockSpec(memory_space=pl.ANY)],
        out_specs=[HBM] * (2 * nw),
        out_shape=[pltpu.HBM(t.shape, t.dtype) for t in list(ps) + list(lands)],
        input_output_aliases={i: i for i in range(2 * nw)},
        compiler_params=pltpu.CompilerParams(has_side_effects=pltpu.SideEffectType.DATAFLOW_SIDE_EFFECTING),
    )(*ps, *lands, send_sems, recv_sems, after)
    return out[nw:]


def _add_partials(gr, t1, t2, place, name):
    nq, r, wd = gr.shape
    hh = r // 2
    tr = _tile(hh, 512, 16)
    nb = hh // tr

    def body(p_ref, g_ref, s_ref, t_ref, o_ref):
        o_ref[...] = (((g_ref[0] + s_ref[0]) + t_ref[0].astype(F32)) + t_ref[1].astype(F32)) + t_ref[2].astype(F32)

    return _pcall(
        body,
        name=name,
        grid_spec=pltpu.PrefetchScalarGridSpec(
            num_scalar_prefetch=1,
            grid=(nb,),
            in_specs=[
                pl.BlockSpec((1, tr, wd), lambda i, p_ref: (p_ref[1], p_ref[0] * nb + i, 0)),
                pl.BlockSpec((1, tr, wd), lambda i, p_ref: (p_ref[1], i, 0)),
                pl.BlockSpec((3, tr, wd), lambda i, p_ref: (0, i, 0)),
            ],
            out_specs=pl.BlockSpec((tr, wd), lambda i, p_ref: (i, 0)),
        ),
        out_shape=jax.ShapeDtypeStruct((hh, wd), F32),
        compiler_params=_params(("parallel",)),
    )(place, gr, t1, t2)


def _join_halves(halves):
    nw = len(halves)

    def body(*refs):
        h_refs, f_refs, (send_sems, recv_sems) = refs[:nw], refs[nw:2 * nw], refs[2 * nw:]
        x, y, c = _place()

        def copy(i, rows):
            hh = halves[i].shape[0]
            return pltpu.make_async_remote_copy(src_ref=h_refs[i], dst_ref=f_refs[i].at[pl.ds(rows * hh, hh)],
                                                send_sem=send_sems.at[i], recv_sem=recv_sems.at[i],
                                                device_id=(x, y, 1 - c), device_id_type=MESH)

        for i in range(nw):
            copy(i, c).start()
        for i in range(nw):
            copy(i, c).wait_send()
            copy(i, 1 - c).wait_recv()

    got = _pcall(
        body,
        name="grad_join",
        in_specs=[ANY] * nw,
        out_specs=[ANY] * nw,
        out_shape=[jax.ShapeDtypeStruct((2 * h.shape[0], h.shape[1]), h.dtype) for h in halves],
        scratch_shapes=[pltpu.SemaphoreType.DMA((nw,)), pltpu.SemaphoreType.DMA((nw,))],
    )(*halves)
    c = lax.axis_index("c")
    return [lax.dynamic_update_slice(g, h, (c * h.shape[0], 0)) for g, h in zip(got, halves)]


N_DEV = 8


def _allreduce_small(buf):
    rows, wd = buf.shape
    n = rows // N_DEV

    def body(x_ref, o_ref, stage_ref, send_sems, recv_sems):
        x, y, c = _place()
        me = 4 * x + 2 * y + c

        def peer(k):
            return (x ^ (k >> 2), y ^ ((k >> 1) & 1), c ^ (k & 1))

        def chunk(ref, d):
            return ref.at[pl.ds(pl.multiple_of(d * n, 8), n)]

        first = []
        for k in range(1, N_DEV):
            px, py, pc = peer(k)
            cp = pltpu.make_async_remote_copy(src_ref=chunk(x_ref, 4 * px + 2 * py + pc), dst_ref=stage_ref.at[k],
                                              send_sem=send_sems.at[k - 1], recv_sem=recv_sems.at[k - 1],
                                              device_id=(px, py, pc), device_id_type=MESH)
            cp.start()
            first.append(cp)
        acc = chunk(x_ref, me)[...]
        for k in range(1, N_DEV):
            first[k - 1].wait_recv()
            acc = acc + stage_ref[k]
        chunk(o_ref, me)[...] = acc
        second = []
        for k in range(1, N_DEV):
            cp = pltpu.make_async_remote_copy(src_ref=chunk(o_ref, me), dst_ref=chunk(o_ref, me),
                                              send_sem=send_sems.at[6 + k], recv_sem=recv_sems.at[6 + k],
                                              device_id=peer(k), device_id_type=MESH)
            cp.start()
            second.append(cp)
        for k in range(1, N_DEV):
            px, py, pc = peer(k)
            them = 4 * px + 2 * py + pc
            pltpu.make_async_remote_copy(src_ref=chunk(o_ref, them), dst_ref=chunk(o_ref, them),
                                         send_sem=send_sems.at[6 + k], recv_sem=recv_sems.at[6 + k],
                                         device_id=(px, py, pc), device_id_type=MESH).wait_recv()
        for cp in first + second:
            cp.wait_send()

    vmem = pl.BlockSpec(memory_space=pltpu.VMEM)
    return _pcall(
        body,
        name="allreduce_small",
        in_specs=[vmem],
        out_specs=vmem,
        out_shape=jax.ShapeDtypeStruct((rows, wd), F32),
        scratch_shapes=[pltpu.VMEM((N_DEV, n, wd), F32), pltpu.SemaphoreType.DMA((14,)),
                        pltpu.SemaphoreType.DMA((14,))],
    )(buf)


@functools.partial(jax.custom_vjp, nondiff_argnums=(1,))
def split_cols(z, spec):
    outs = []
    for segs in spec[0]:
        parts = [z[:, s:s + n] for s, n in segs]
        outs.append(parts[0] if len(parts) == 1 else jnp.concatenate(parts, axis=1))
    return tuple(outs)


def _split_cols_fwd(z, spec):
    return split_cols(z, spec), None


def _split_cols_bwd(spec, _, gs):
    pieces, width = spec
    segs = []
    for p, piece in enumerate(pieces):
        off = 0
        for s, n in piece:
            segs.append((s, n, p, off))
            off += n
    rows, parts, pos = gs[0].shape[0], [], 0
    for s, n, p, off in sorted(segs):
        if s > pos:
            parts.append(jnp.zeros((rows, s - pos), gs[0].dtype))
        parts.append(gs[p] if (off == 0 and n == gs[p].shape[1]) else gs[p][:, off:off + n])
        pos = s + n
    if pos < width:
        parts.append(jnp.zeros((rows, width - pos), gs[0].dtype))
    return (jnp.concatenate(parts, axis=1),)


split_cols.defvjp(_split_cols_fwd, _split_cols_bwd)


def _contiguous(sizes):
    pieces, off = [], 0
    for n in sizes:
        pieces.append(((off, n),))
        off += n
    return tuple(pieces), off


def _rope(t, cos, sin):
    half = t.shape[-1] // 2
    t1, t2 = t[..., :half], t[..., half:]
    return jnp.concatenate([t1 * cos - t2 * sin, t2 * cos + t1 * sin], axis=-1)


def _s5_mixer(u, sp, l, w16, toks):
    g, p_, h = SSM_GROUPS, SSM_STATE, SSM_GROUP_CH
    lam_re, lam_im = sp["ssm_lam_re", l], sp["ssm_lam_im", l]
    step = jnp.exp(sp["ssm_log_step", l])[:, None]
    mag, ang = jnp.exp(lam_re * step), lam_im * step
    lbr, lbi = mag * jnp.cos(ang), mag * jnp.sin(ang)
    den = lam_re * lam_re + lam_im * lam_im
    nr, ni = lbr - 1.0, lbi
    qr, qi = (nr * lam_re + ni * lam_im) / den, (ni * lam_re - nr * lam_im) / den
    b_re, b_im = sp["ssm_b_re", l], sp["ssm_b_im", l]
    bbr = qr[..., None] * b_re - qi[..., None] * b_im
    bbi = qr[..., None] * b_im + qi[..., None] * b_re
    nb = g // SSM_PACK
    eye = jnp.eye(SSM_PACK, dtype=F32)

    def in_blocks(t):
        return jnp.einsum("jgph,gk->jghkp", t.reshape(nb, SSM_PACK, p_, h), eye).reshape(nb, SSM_PACK * h, SSM_PACK * p_)

    def out_blocks(t):
        return jnp.einsum("jghp,gk->jgpkh", t.reshape(nb, SSM_PACK, h, p_), eye).reshape(nb, SSM_PACK * p_, SSM_PACK * h)

    nm = "ssm%d" % l
    ur = bd_matmul(u, in_blocks(bbr), nm + "_bur")
    ui = bd_matmul(u, in_blocks(bbi), nm + "_bui")
    xr, xi = s5_scan(ur, ui, lbr.reshape(1, g * p_), lbi.reshape(1, g * p_), nm + "_scan")
    y = (bd_matmul(xr, out_blocks(sp["ssm_c_re", l]), nm + "_cr") + bd_matmul(xi, out_blocks(-sp["ssm_c_im", l]), nm + "_ci")
         + sp["ssm_d", l].reshape(1, g * h) * u)
    y = jax.nn.gelu(y)
    zz = matmul_w(y, w16["ssm_w_glu", l], toks["ssm_w_glu", l], nm + "_glu")
    z_out, z_gate = split_cols(zz, _contiguous((D_MODEL, D_MODEL)))
    return z_out * jax.nn.sigmoid(z_gate)


def _mla_mixer(q_lat, kv_lat, k_rope, cos, sin, sp, l, w16, toks):
    s = q_lat.shape[0]
    nm = "mla%d" % l
    dqk = MLA_NOPE + MLA_ROPE
    qt = matmul_w_t(rmsnorm(q_lat, sp["mla_q_norm", l], nm + "_qn"), w16["mla_w_uq", l], toks["mla_w_uq", l],
                    nm + "_uq").reshape(MLA_HEADS, dqk, s)
    half = MLA_ROPE // 2
    q1, q2 = qt[:, MLA_NOPE:MLA_NOPE + half], qt[:, MLA_NOPE + half:]
    ct, st = cos.T[None], sin.T[None]
    qt = jnp.concatenate([qt[:, :MLA_NOPE], q1 * ct - q2 * st, q2 * ct + q1 * st], axis=1).reshape(MLA_HEADS * dqk, s)
    kvt = matmul_w_t(rmsnorm(kv_lat, sp["mla_kv_norm", l], nm + "_kvn"), w16["mla_w_ukv", l], toks["mla_w_ukv", l],
                     nm + "_ukv").reshape(MLA_HEADS, MLA_NOPE + MLA_V, s)
    k_pe = _rope(k_rope, cos, sin)
    k = jnp.concatenate([kvt[:, :MLA_NOPE].transpose(0, 2, 1),
                         jnp.broadcast_to(k_pe[None], (MLA_HEADS, s, MLA_ROPE))], axis=-1)
    vt = kvt[:, MLA_NOPE:].reshape(MLA_HEADS * MLA_V, s)
    ot = attention(qt, k, vt, 1.0 / math.sqrt(dqk), True, nm + "_att")
    return matmul_w_at(ot, w16["mla_w_o", l], toks["mla_w_o", l], nm + "_o")


def _hgrn_mixer(q, f_logit, i_in, g, lb, sp, l, w16, toks):
    s = q.shape[0]
    nm = "hg%d" % l
    f = lb + (1.0 - lb) * jax.nn.sigmoid(f_logit)
    o = hgrn_core(jax.nn.silu(q), 1.0 - f, i_in, jnp.log(f), nm + "_core")
    o = rmsnorm(o.reshape(s * HG_HEADS, HG_DV), sp["hg_g_norm", l], nm + "_gn").reshape(s, HG_HEADS * HG_DV)
    return silu_gated_out(o, g, w16["hg_w_o", l], toks["hg_w_o", l], nm + "_o")


def _cross_attention(x, h, mem_n, l, w16, toks):
    s, m = h.shape[0], mem_n.shape[0]
    nm = "xat%d" % l
    xw = X_HEADS * X_HEAD_DIM
    qt = matmul_w_t(h, w16["x_w_q", l], toks["x_w_q", l], nm + "_q")
    kv = matmul_w(mem_n, w16["x_w_kv", l], toks["x_w_kv", l], nm + "_kv")
    k, v = split_cols(kv, _contiguous((xw, xw)))
    k = k.reshape(m, X_HEADS, X_HEAD_DIM).transpose(1, 0, 2)
    ot = attention(qt, k, v.T, 1.0 / math.sqrt(X_HEAD_DIM), False, nm + "_att")
    return matmul_w_at_res(x, ot, w16["x_w_o", l], toks["x_w_o", l], nm + "_o")


def _in_spec():
    hw = HG_HEADS * HG_DK
    sizes = (SSM_WIDTH, MLA_Q_RANK, MLA_KV_RANK, MLA_ROPE, hw, hw, hw, hw, D_MODEL, D_MODEL, D_MODEL)
    pieces, lo = [], 0
    for n in sizes:
        segs, c = [], lo
        while c < lo + n:
            end = min(lo + n, (c // D_IN_SHARD + 1) * D_IN_SHARD)
            segs.append((c + (D_IN_BLOCK - D_IN_SHARD) * (c // D_IN_SHARD), end - c))
            c = end
        pieces.append(tuple(segs))
        lo += n
    assert lo == D_IN
    return tuple(pieces), N_CHIPS * D_IN_BLOCK


def _layer(x, mem, cos, sin, lb, l, sp, w16, toks):
    s = x.shape[0]
    nm = "l%d" % l
    h = rmsnorm(x, sp["norm_mix", l], nm + "_nmix")
    z = matmul_w(h, w16["w_in", l], toks["w_in", l], nm + "_in")
    u, q_lat, kv_lat, k_rope, hg_q, hg_f, hg_i, hg_g, l_ssm, l_mla, l_hg = split_cols(z, _in_spec())
    y_ssm = _s5_mixer(u, sp, l, w16, toks)
    y_mla = _mla_mixer(q_lat, kv_lat, k_rope, cos, sin, sp, l, w16, toks)
    y_hg = _hgrn_mixer(hg_q, hg_f, hg_i, hg_g, lb, sp, l, w16, toks)
    x = gated_merge_out(x, (y_ssm, y_mla, y_hg), (l_ssm, l_mla, l_hg), w16["w_out", l], toks["w_out", l], nm + "_out")
    hc = rmsnorm(x, sp["norm_cross", l], nm + "_ncross")
    mem_n = rmsnorm(mem, sp["norm_mem", l], nm + "_nmem")
    x = _cross_attention(x, hc, mem_n, l, w16, toks)
    hf = rmsnorm(x, sp["norm_ffn", l], nm + "_nffn")
    gt = matmul_w(hf, w16["ffn_w_gate", l], toks["ffn_w_gate", l], nm + "_gate")
    up = matmul_w(hf, w16["ffn_w_up", l], toks["ffn_w_up", l], nm + "_up")
    return swiglu_down(x, gt, up, w16["ffn_w_down", l], toks["ffn_w_down", l], nm + "_down")


def _lower_bounds(hg_lb):
    lb_p = jax.nn.softmax(hg_lb, axis=0)
    return jnp.cumsum(lb_p, axis=0) - lb_p[0:1]


def _loss_head(norm_final, x, target):
    y = rmsnorm(x, norm_final, "nfinal")
    return 0.5 * jnp.sum(jnp.mean(jnp.square(y - target), axis=-1))


def _forward_backward(sp, toks, x, w16, later, between, mem, pos, target):
    half = MLA_ROPE // 2
    inv_freq = ROPE_THETA ** (-jnp.arange(half, dtype=F32) / half)
    ang = pos.astype(F32)[:, None] * inv_freq
    cos, sin = jnp.cos(ang), jnp.sin(ang)
    lower, lower_vjp = jax.vjp(_lower_bounds, sp["hg_lb", ALL])
    token, wait_for_weights = later
    x = x + token[0, 0]
    vjps = []
    for l in range(DEPTH):
        if l == 1:
            w16 = wait_for_weights(w16, x)
        sp_l = {k: v for k, v in sp.items() if k[1] == l}
        toks_l = {k: v for k, v in toks.items() if k[1] == l}
        weights = w16
        x, layer_vjp = jax.vjp(lambda s, t, lb, xx, l=l, weights=weights: _layer(xx, mem, cos, sin, lb, l, s, weights, t),
                               sp_l, toks_l, lower[l], x)
        vjps.append(layer_vjp)
    loss, head_vjp = jax.vjp(lambda g, xx: _loss_head(g, xx, target), sp["norm_final", ALL], x)
    g_sp, g_tok, d_lower = {}, {}, [None] * DEPTH
    g_sp["norm_final", ALL], dx = head_vjp(jnp.ones((), F32))
    for l in reversed(range(DEPTH)):
        g_s, g_t, d_lower[l], dx = vjps[l](dx)
        g_sp.update(g_s)
        g_tok.update(g_t)
        if l > 0:
            dx = between(l, g_t, dx)
    (g_sp["hg_lb", ALL],) = lower_vjp(jnp.stack(d_lower))
    return loss, g_sp, g_tok, dx


WEIGHTS = ["norm_mix", "w_in", "ssm_lam_re", "ssm_lam_im", "ssm_b_re", "ssm_b_im", "ssm_c_re", "ssm_c_im", "ssm_d",
           "ssm_log_step", "ssm_w_glu", "mla_q_norm", "mla_kv_norm", "mla_w_uq", "mla_w_ukv", "mla_w_o", "hg_lb",
           "hg_g_norm", "hg_w_o", "w_out", "norm_cross", "norm_mem", "x_w_q", "x_w_kv", "x_w_o", "norm_ffn",
           "ffn_w_gate_up", "ffn_w_down", "norm_final"]
INPUTS = ["x", "mem", "positions"] + WEIGHTS + ["loss_target"] + ["m_" + n for n in WEIGHTS] + ["v_" + n for n in WEIGHTS]
BIG = [("w_in", 2), ("ssm_w_glu", 2), ("mla_w_uq", 2), ("mla_w_ukv", 2), ("mla_w_o", 2), ("hg_w_o", 2), ("w_out", 1),
       ("x_w_q", 1), ("x_w_kv", 1), ("x_w_o", 2), ("ffn_w_gate_up", 2), ("ffn_w_down", 1)]
SMALL = [n for n in WEIGHTS if n not in dict(BIG)]
WHOLE = ("hg_lb", "norm_final")
ALL = -1
SMALL_W = 128
BLOCKED = ("w_in", "ffn_w_gate_up")


def _pack(arrs, rows, width, dtype):
    flat = jnp.concatenate([a.astype(dtype).reshape(-1) for a in arrs])
    return jnp.pad(flat, (0, rows * width - flat.shape[0])).reshape(rows, width)


def _unpack(buf, like):
    flat = buf.reshape(-1)
    out, off = [], 0
    for a in like:
        n = math.prod(a.shape)
        out.append(flat[off:off + n].reshape(a.shape))
        off += n
    return out


def kernel(
    x, mem, positions, norm_mix, w_in, ssm_lam_re, ssm_lam_im, ssm_b_re, ssm_b_im, ssm_c_re, ssm_c_im, ssm_d,
    ssm_log_step, ssm_w_glu, mla_q_norm, mla_kv_norm, mla_w_uq, mla_w_ukv, mla_w_o, hg_lb, hg_g_norm, hg_w_o,
    w_out, norm_cross, norm_mem, x_w_q, x_w_kv, x_w_o, norm_ffn, ffn_w_gate_up, ffn_w_down, norm_final,
    loss_target, m_norm_mix, m_w_in, m_ssm_lam_re, m_ssm_lam_im, m_ssm_b_re, m_ssm_b_im, m_ssm_c_re, m_ssm_c_im,
    m_ssm_d, m_ssm_log_step, m_ssm_w_glu, m_mla_q_norm, m_mla_kv_norm, m_mla_w_uq, m_mla_w_ukv, m_mla_w_o,
    m_hg_lb, m_hg_g_norm, m_hg_w_o, m_w_out, m_norm_cross, m_norm_mem, m_x_w_q, m_x_w_kv, m_x_w_o, m_norm_ffn,
    m_ffn_w_gate_up, m_ffn_w_down, m_norm_final, v_norm_mix, v_w_in, v_ssm_lam_re, v_ssm_lam_im, v_ssm_b_re,
    v_ssm_b_im, v_ssm_c_re, v_ssm_c_im, v_ssm_d, v_ssm_log_step, v_ssm_w_glu, v_mla_q_norm, v_mla_kv_norm,
    v_mla_w_uq, v_mla_w_ukv, v_mla_w_o, v_hg_lb, v_hg_g_norm, v_hg_w_o, v_w_out, v_norm_cross, v_norm_mem,
    v_x_w_q, v_x_w_kv, v_x_w_o, v_norm_ffn, v_ffn_w_gate_up, v_ffn_w_down, v_norm_final):
    given = locals()
    a = {n: given[n] for n in INPUTS}
    x, mem, pos, target = a["x"][0], a["mem"][0], a["positions"][0], a["loss_target"][0]
    place = jnp.stack([lax.axis_index("c"), 2 * lax.axis_index("x") + lax.axis_index("y")]).astype(jnp.int32)

    def shard16(n):
        w = a[n].astype(BF16)
        if n == "w_in":
            w = jnp.pad(w, ((0, 0), (0, 0), (0, D_IN_BLOCK - D_IN_SHARD)))
        return w

    def use(blocks, l, w16):
        for (n, ax), got in zip(BIG, blocks):
            if n in BLOCKED:
                full = got
            elif ax == 2:
                full = jnp.concatenate([got[q] for q in range(N_CHIPS)], axis=1)
            else:
                full = got.reshape(N_CHIPS * got.shape[1], got.shape[2])
            if n == "ffn_w_gate_up":
                half = N_CHIPS // 2
                w16["ffn_w_gate", l], w16["ffn_w_up", l] = full[:half], full[half:]
            else:
                w16[n, l] = full
        return w16

    shards = [shard16(n) for n, _ in BIG]
    halves = [s[0].reshape(2, s.shape[1] // 2, s.shape[2]) for s in shards]
    w16 = use([g.reshape(N_CHIPS, s.shape[1], s.shape[2]) for g, s in zip(_gather_weights(halves), shards)], 0, {})
    pending = _gather_start([s[1] for s in shards], after=w16["w_in", 0])
    toks = {}
    for n, ax in BIG:
        rows, cols = a[n].shape[1], shards[BIG.index((n, ax))].shape[2]
        shape = (N_CHIPS, rows, cols) if n in BLOCKED else (rows, N_CHIPS * cols) if ax == 2 else (N_CHIPS * rows, cols)
        for l in range(DEPTH):
            if n == "ffn_w_gate_up":
                toks["ffn_w_gate", l] = jnp.zeros((N_CHIPS // 2,) + shape[1:], F32)
                toks["ffn_w_up", l] = jnp.zeros((N_CHIPS // 2,) + shape[1:], F32)
            else:
                toks[n, l] = jnp.zeros(shape, F32)

    def layer1_weights(w16, after):
        return use(_gather_wait(*pending[:4], after), 1, dict(w16))
    sp = {}
    for n in SMALL:
        if n in WHOLE:
            sp[n, ALL] = a[n]
        else:
            for l in range(DEPTH):
                sp[n, l] = a[n][l]

    def chip_sums_of(g_tok, l):
        grads = []
        for n, ax in BIG:
            if n == "ffn_w_gate_up":
                g = jnp.concatenate([g_tok["ffn_w_gate", l], g_tok["ffn_w_up", l]], axis=0)
            else:
                g = g_tok[n, l]
            if n in BLOCKED:
                grads.append(g)
            elif ax == 2:
                grads.append(g.reshape(g.shape[0], N_CHIPS, g.shape[1] // N_CHIPS).transpose(1, 0, 2))
            else:
                grads.append(g.reshape(N_CHIPS, g.shape[0] // N_CHIPS, g.shape[1]))
        from_sibling = _swap_halves_to_sibling(grads, tag=str(l))
        sums = [_add_half(g, t, place, "grad_add_half_%s%d" % (n, l)) for (n, _), g, t in zip(BIG, grads, from_sibling)]
        return grads, from_sibling, sums

    def totals_of(l, grads, from_sibling, from_chips):
        return [_add_partials(g, t1, t2, place, "grad_add_partials_%s%d" % (n, l))
                for (n, _), g, t1, t2 in zip(BIG, grads, from_sibling, from_chips)]

    started = {}

    def between(l, g_tok_l, dx):
        grads, from_sibling, sums = chip_sums_of(g_tok_l, l)
        flight = _scatter_start(sums, after=dx)
        started[l] = (grads, from_sibling, flight)
        return dx + flight[4][0, 0]

    loss, g_sp, g_tok, g_x = _forward_backward(sp, toks, x, w16, (pending[4], layer1_weights), between, mem, pos, target)
    g_small = {n: g_sp[n, ALL] if n in WHOLE else jnp.stack([g_sp[n, l] for l in range(DEPTH)]) for n in SMALL}
    totals = {}
    for l, (grads, from_sibling, flight) in started.items():
        totals[l] = totals_of(l, grads, from_sibling, _scatter_wait(*flight[:4], after=g_x))
    grads, from_sibling, sums = chip_sums_of(g_tok, 0)
    flight = _scatter_start(sums, after=g_x, tag="0")

    out_g, out_d, out_m, out_v = {}, {}, {}, {}
    small_like = [a[n] for n in SMALL] + [jnp.zeros((1,), F32)]
    n_small = sum(math.prod(t.shape) for t in small_like)
    srows = -(-n_small // (SMALL_W * 8 * N_DEV)) * 8 * N_DEV
    packed = _pack([g_small[n] for n in SMALL] + [loss.reshape(1)], srows, SMALL_W, F32)
    reduced = _allreduce_small(packed + flight[4][0, 0])
    red = _unpack(reduced, small_like)
    loss = red[-1].reshape(())
    small_arrs = [a[n] for n in SMALL]
    d, m, v = _adamw_call(
        _pack(small_arrs, srows, SMALL_W, F32), reduced,
        _pack([a["m_" + n] for n in SMALL], srows, SMALL_W, F32),
        _pack([a["v_" + n] for n in SMALL], srows, SMALL_W, F32), "adamw_small")
    for n, gg, dd, mm_, vv in zip(SMALL, red[:-1], _unpack(d, small_arrs), _unpack(m, small_arrs), _unpack(v, small_arrs)):
        out_g[n], out_d[n], out_m[n], out_v[n] = gg, dd, mm_, vv

    totals[0] = totals_of(0, grads, from_sibling, _scatter_wait(*flight[:4], after=d, tag="0"))
    joined = _join_halves([t for l in range(DEPTH) for t in totals[l]])
    nbig = len(BIG)
    g_big = {n: jnp.stack([joined[nbig * l + i][:, :a[n].shape[2]] for l in range(DEPTH)])
             for i, (n, _) in enumerate(BIG)}
    for n, _ in BIG:
        shp = a[n].shape
        two = lambda t: t.reshape(shp[0] * shp[1], shp[2])
        d, m, v = _adamw_call(two(a[n]), two(g_big[n]), two(a["m_" + n]), two(a["v_" + n]), "adamw_" + n)
        out_g[n], out_d[n], out_m[n], out_v[n] = g_big[n], d.reshape(shp), m.reshape(shp), v.reshape(shp)

    return (loss, g_x[None], *[out_g[n] for n in WEIGHTS], *[out_d[n] for n in WEIGHTS],
            *[out_m[n] for n in WEIGHTS], *[out_v[n] for n in WEIGHTS])
```

```python
import functools
import math

import jax
import jax.numpy as jnp
from jax import lax
from jax.experimental import pallas as pl
from jax.experimental.pallas import tpu as pltpu

F32 = jnp.float32
BF16 = jnp.bfloat16
MESH = pl.DeviceIdType.MESH

D_MODEL = 1024
DEPTH = 2
RMS_EPS = 1e-6
SSM_GROUPS, SSM_GROUP_CH, SSM_STATE = 32, 16, 64
SSM_WIDTH = SSM_GROUPS * SSM_GROUP_CH
SSM_LANES = SSM_GROUPS * SSM_STATE
SSM_PACK = 8
MLA_HEADS, MLA_Q_RANK, MLA_KV_RANK, MLA_NOPE, MLA_ROPE, MLA_V = 8, 512, 256, 64, 32, 64
ROPE_THETA = 10000.0
HG_HEADS, HG_DK, HG_DV, HG_CHUNK = 4, 128, 128, 64
HG_SUB_FWD, HG_SUB_BWD = 32, 16
X_HEADS, X_HEAD_DIM = 4, 128
D_FF = 2816
D_IN = 6432
N_CHIPS = 4
D_IN_SHARD = D_IN // N_CHIPS
D_IN_BLOCK = 1664
MASK_VALUE = -1e30

ADAM_LR, ADAM_B1, ADAM_B2, ADAM_EPS, ADAM_WD, ADAM_STEP = 0.001, 0.9, 0.999, 1e-08, 0.01, 10

VMEM_LIMIT = 48 * 1024 * 1024


def _pcall(body, **kw):
    return pl.pallas_call(body, **kw)


def _params(sem):
    return pltpu.CompilerParams(dimension_semantics=sem, vmem_limit_bytes=VMEM_LIMIT)


def _tile(n, cap, align=128):
    if n <= cap:
        return n
    t = (cap // align) * align
    while t >= align:
        if n % t == 0:
            return t
        t -= align
    return n


MM_VMEM_BUDGET = 36 * 1024 * 1024
MM_TILE_CAP = 2048


def _divisors(n, cap, align=128):
    out = [n] if n <= cap else []
    out += [t for t in range(align, min(n, cap + 1), align) if n % t == 0]
    return sorted(set(out), reverse=True)


@functools.lru_cache(maxsize=None)
def _mm_tiles(m, n, k, a_bytes, b_bytes, o_bytes, n_unit, k_unit):
    best = None
    for tm in _divisors(m, MM_TILE_CAP):
        for tn in _divisors(n_unit, MM_TILE_CAP):
            for tk in _divisors(k_unit, MM_TILE_CAP):
                nk = k // tk
                need = 2 * (tm * tk * a_bytes + tk * tn * b_bytes + tm * tn * o_bytes)
                need += tm * tn * 4 * (2 if nk > 1 else 1)
                need += (tm * tk * 2 if a_bytes > 2 else 0) + (tk * tn * 2 if b_bytes > 2 else 0)
                if need > MM_VMEM_BUDGET:
                    continue
                key = ((m // tm) * (n // tn) * nk, nk, -tn)
                if best is None or key < best[0]:
                    best = (key, (tm, tn, tk))
    assert best is not None, (m, n, k)
    return best[1]


def _mm(a, b, *, ta=False, tb=False, out_dtype=F32, name="mm", b_blocks=False, out_blocks=0, pro=None, epi=None,
        save_left=False):
    if ta:
        K, M = a.shape
    else:
        M, K = a.shape
    n_unit = k_unit = None
    if b_blocks:
        nb, rows, cols = b.shape
        if tb:
            N, Kb, k_unit = rows, nb * cols, cols
        else:
            Kb, N, n_unit = rows, nb * cols, cols
    elif tb:
        N, Kb = b.shape
    else:
        Kb, N = b.shape
    if out_blocks:
        assert not b_blocks and N % out_blocks == 0
        n_unit = N // out_blocks
    assert K == Kb, (a.shape, b.shape, ta, tb)
    n_pro = len(pro) - 1 if pro else 0
    n_epi = len(epi) - 1 if epi else 0
    tm, tn, tk = _mm_tiles(M, N, K, a.dtype.itemsize * (1 + n_pro), b.dtype.itemsize,
                           jnp.dtype(out_dtype).itemsize * max(1, 2 * n_epi), n_unit or N, k_unit or K)
    nk = K // tk
    dims = (((0,) if ta else (1,), (1,) if tb else (0,)), ((), ()))
    n_in = 2 + n_pro + n_epi
    n_out = max(1, n_epi) + (1 if save_left else 0)
    assert not save_left or (pro and tn == N)

    def body(*refs):
        ins, outs, acc = refs[:n_in], refs[n_in:n_in + n_out], refs[n_in + n_out:]
        b_ref = ins[1 + n_pro]
        av = (pro[0](*[r[...] for r in ins[:1 + n_pro]]) if pro else ins[0][...]).astype(BF16)
        if save_left:
            outs[-1][...] = av
        bv = b_ref[...].astype(BF16)
        part = lax.dot_general(av, bv, dims, preferred_element_type=F32)

        def finish(total):
            if epi:
                for o_ref, r in zip(outs, epi[0](total, *[e[...] for e in ins[2 + n_pro:]])):
                    o_ref[...] = r.astype(o_ref.dtype)
            else:
                outs[0][...] = total.astype(outs[0].dtype)

        if nk == 1:
            finish(part)
        else:
            acc_ref = acc[0]
            k = pl.program_id(2)

            @pl.when(k == 0)
            def _():
                acc_ref[...] = part

            @pl.when(k > 0)
            def _():
                acc_ref[...] += part

            @pl.when(k == nk - 1)
            def _():
                finish(acc_ref[...])

    a_spec = pl.BlockSpec((tk, tm), lambda i, j, k: (k, i)) if ta else pl.BlockSpec((tm, tk), lambda i, j, k: (i, k))
    if b_blocks and tb:
        r = k_unit // tk
        b_spec = pl.BlockSpec((None, tn, tk), lambda i, j, k: (k // r, j, k % r))
    elif b_blocks:
        r = n_unit // tn
        b_spec = pl.BlockSpec((None, tk, tn), lambda i, j, k: (j // r, k, j % r))
    elif tb:
        b_spec = pl.BlockSpec((tn, tk), lambda i, j, k: (j, k))
    else:
        b_spec = pl.BlockSpec((tk, tn), lambda i, j, k: (k, j))
    if out_blocks:
        r = n_unit // tn
        out_spec = pl.BlockSpec((None, tm, tn), lambda i, j, k: (j // r, i, j % r))
        out_shape = jax.ShapeDtypeStruct((out_blocks, M, n_unit), out_dtype)
    else:
        out_spec = pl.BlockSpec((tm, tn), lambda i, j, k: (i, j))
        out_shape = jax.ShapeDtypeStruct((M, N), out_dtype)
    in_specs = [a_spec] * (1 + n_pro) + [b_spec] + [out_spec] * n_epi
    args = [a] + (list(pro[1:]) if pro else []) + [b] + (list(epi[1:]) if epi else [])
    out_specs, out_shapes = ([out_spec] * n_epi, [out_shape] * n_epi) if epi else (out_spec, out_shape)
    if save_left:
        out_specs = [out_spec] * max(1, n_epi) + [a_spec]
        out_shapes = [out_shape] * max(1, n_epi) + [jax.ShapeDtypeStruct(a.shape, BF16)]
    return _pcall(
        body,
        name=name,
        grid=(M // tm, N // tn, nk),
        in_specs=in_specs,
        out_specs=out_specs,
        out_shape=out_shapes,
        scratch_shapes=[pltpu.VMEM((tm, tn), F32)] if nk > 1 else [],
        compiler_params=_params(("parallel", "parallel", "arbitrary")),
    )(*args)


def _bd_call(a, bc, transpose, name):
    m = a.shape[0]
    nb, ka, kb = bc.shape
    win, wout = (kb, ka) if transpose else (ka, kb)
    assert a.shape[1] == nb * win
    tm = _tile(m, 1024)
    dims = _NT if transpose else (((1,), (0,)), ((), ()))

    def body(a_ref, b_ref, o_ref):
        o_ref[...] = lax.dot_general(a_ref[...].astype(BF16), b_ref[0].astype(BF16), dims, preferred_element_type=F32)

    return _pcall(
        body,
        name=name,
        grid=(m // tm, nb),
        in_specs=[pl.BlockSpec((tm, win), lambda i, j: (i, j)), pl.BlockSpec((1, ka, kb), lambda i, j: (j, 0, 0))],
        out_specs=pl.BlockSpec((tm, wout), lambda i, j: (i, j)),
        out_shape=jax.ShapeDtypeStruct((m, nb * wout), F32),
        compiler_params=_params(("parallel", "parallel")),
    )(a, bc)


def _bd_dw_call(a, g, nb, name):
    m = a.shape[0]
    ka, kb = a.shape[1] // nb, g.shape[1] // nb
    tk = _tile(m, 1024)
    nk = m // tk

    def body(a_ref, g_ref, o_ref):
        part = lax.dot_general(a_ref[...].astype(BF16), g_ref[...].astype(BF16), _TN, preferred_element_type=F32)
        k = pl.program_id(1)

        @pl.when(k == 0)
        def _():
            o_ref[0] = part

        @pl.when(k > 0)
        def _():
            o_ref[0] += part

    return _pcall(
        body,
        name=name,
        grid=(nb, nk),
        in_specs=[pl.BlockSpec((tk, ka), lambda j, k: (k, j)), pl.BlockSpec((tk, kb), lambda j, k: (k, j))],
        out_specs=pl.BlockSpec((1, ka, kb), lambda j, k: (j, 0, 0)),
        out_shape=jax.ShapeDtypeStruct((nb, ka, kb), F32),
        compiler_params=_params(("parallel", "arbitrary")),
    )(a, g)


@functools.partial(jax.custom_vjp, nondiff_argnums=(2,))
def bd_matmul(a, bc, name):
    return _bd_call(a, bc, False, name + "_f")


def _bd_matmul_fwd(a, bc, name):
    return _bd_call(a, bc, False, name + "_f"), (a, bc)


def _bd_matmul_bwd(name, res, g):
    a, bc = res
    return _bd_call(g, bc, True, name + "_da"), _bd_dw_call(a, g, bc.shape[0], name + "_dw")


bd_matmul.defvjp(_bd_matmul_fwd, _bd_matmul_bwd)


@functools.partial(jax.custom_vjp, nondiff_argnums=(3,))
def matmul_w(a, w16, tok, name):
    return _mm(a, w16, b_blocks=w16.ndim == 3, name=name + "_f")


def _matmul_w_fwd(a, w16, tok, name):
    return _mm(a, w16, b_blocks=w16.ndim == 3, name=name + "_f"), (a, w16)


def _matmul_w_bwd(name, res, g):
    a, w16 = res
    blocks = w16.shape[0] if w16.ndim == 3 else 0
    return (_mm(g, w16, tb=True, b_blocks=bool(blocks), name=name + "_da"), jnp.zeros_like(w16),
            _mm(a, g, ta=True, out_blocks=blocks, name=name + "_dw"))


matmul_w.defvjp(_matmul_w_fwd, _matmul_w_bwd)


def _swiglu(gt, up):
    return gt * jax.nn.sigmoid(gt) * up


def _swiglu_bwd(dact, gt, up):
    sg = jax.nn.sigmoid(gt)
    return dact * up * sg * (1.0 + gt * (1.0 - sg)), dact * gt * sg


def _add_residual(product, x):
    return (product + x,)


def _silu_gate(o, g):
    return o * (g * jax.nn.sigmoid(g))


def _silu_gate_bwd(dm, o, g):
    sg = jax.nn.sigmoid(g)
    return dm * g * sg, dm * o * sg * (1.0 + g * (1.0 - sg))


@functools.partial(jax.custom_vjp, nondiff_argnums=(4,))
def silu_gated_out(o, g, w16, tok, name):
    return _mm(o, w16, pro=(_silu_gate, g), name=name + "_f")


def _silu_gated_out_fwd(o, g, w16, tok, name):
    out, left16 = _mm(o, w16, pro=(_silu_gate, g), save_left=True, name=name + "_f")
    return out, (o, g, w16, left16)


def _silu_gated_out_bwd(name, res, d):
    o, g, w16, left16 = res
    d_o, d_g = _mm(d, w16, tb=True, epi=(_silu_gate_bwd, o, g), name=name + "_da")
    return d_o, d_g, jnp.zeros_like(w16), _mm(left16, d, ta=True, name=name + "_dw")


silu_gated_out.defvjp(_silu_gated_out_fwd, _silu_gated_out_bwd)


@functools.partial(jax.custom_vjp, nondiff_argnums=(5,))
def swiglu_down(x, gt, up, w16, tok, name):
    return _mm(gt, w16, pro=(_swiglu, up), epi=(_add_residual, x), name=name + "_f")[0]


def _swiglu_down_fwd(x, gt, up, w16, tok, name):
    out, act16 = _mm(gt, w16, pro=(_swiglu, up), epi=(_add_residual, x), save_left=True, name=name + "_f")
    return out, (gt, up, w16, act16)


def _swiglu_down_bwd(name, res, g):
    gt, up, w16, act16 = res
    d_gt, d_up = _mm(g, w16, tb=True, epi=(_swiglu_bwd, gt, up), name=name + "_da")
    return g, d_gt, d_up, jnp.zeros_like(w16), _mm(act16, g, ta=True, name=name + "_dw")


swiglu_down.defvjp(_swiglu_down_fwd, _swiglu_down_bwd)


def _merge(y1, y2, y3, l1, l2, l3):
    return jax.nn.sigmoid(l1) * y1 + jax.nn.sigmoid(l2) * y2 + jax.nn.sigmoid(l3) * y3


def _merge_bwd(dm, y1, y2, y3, l1, l2, l3):
    dys, dls = [], []
    for y, logit in ((y1, l1), (y2, l2), (y3, l3)):
        s = jax.nn.sigmoid(logit)
        dys.append(dm * s)
        dls.append(dm * y * s * (1.0 - s))
    return (*dys, *dls)


@functools.partial(jax.custom_vjp, nondiff_argnums=(5,))
def gated_merge_out(x, ys, logits, w16, tok, name):
    return _mm(ys[0], w16, pro=(_merge, *ys[1:], *logits), epi=(_add_residual, x), name=name + "_f")[0]


def _gated_merge_out_fwd(x, ys, logits, w16, tok, name):
    out, merged16 = _mm(ys[0], w16, pro=(_merge, *ys[1:], *logits), epi=(_add_residual, x), save_left=True,
                        name=name + "_f")
    return out, (ys, logits, w16, merged16)


def _gated_merge_out_bwd(name, res, g):
    ys, logits, w16, merged16 = res
    d = _mm(g, w16, tb=True, epi=(_merge_bwd, *ys, *logits), name=name + "_da")
    return g, tuple(d[:3]), tuple(d[3:]), jnp.zeros_like(w16), _mm(merged16, g, ta=True, name=name + "_dw")


gated_merge_out.defvjp(_gated_merge_out_fwd, _gated_merge_out_bwd)


@functools.partial(jax.custom_vjp, nondiff_argnums=(3,))
def matmul_w_t(a, w16, tok, name):
    return _mm(w16, a, ta=True, tb=True, name=name + "_f")


def _matmul_w_t_fwd(a, w16, tok, name):
    return _mm(w16, a, ta=True, tb=True, name=name + "_f"), (a, w16)


def _matmul_w_t_bwd(name, res, g):
    a, w16 = res
    return (_mm(g, w16, ta=True, tb=True, name=name + "_da"), jnp.zeros_like(w16),
            _mm(a, g, ta=True, tb=True, name=name + "_dw"))


matmul_w_t.defvjp(_matmul_w_t_fwd, _matmul_w_t_bwd)


@functools.partial(jax.custom_vjp, nondiff_argnums=(3,))
def matmul_w_at(at, w16, tok, name):
    return _mm(at, w16, ta=True, name=name + "_f")


def _matmul_w_at_fwd(at, w16, tok, name):
    return _mm(at, w16, ta=True, name=name + "_f"), (at, w16)


def _matmul_w_at_bwd(name, res, g):
    at, w16 = res
    return _mm(w16, g, tb=True, name=name + "_da"), jnp.zeros_like(w16), _mm(at, g, name=name + "_dw")


matmul_w_at.defvjp(_matmul_w_at_fwd, _matmul_w_at_bwd)


@functools.partial(jax.custom_vjp, nondiff_argnums=(4,))
def matmul_w_at_res(x, at, w16, tok, name):
    return _mm(at, w16, ta=True, epi=(_add_residual, x), name=name + "_f")[0]


def _matmul_w_at_res_fwd(x, at, w16, tok, name):
    return _mm(at, w16, ta=True, epi=(_add_residual, x), name=name + "_f")[0], (at, w16)


def _matmul_w_at_res_bwd(name, res, g):
    return (g, *_matmul_w_at_bwd(name, res, g))


matmul_w_at_res.defvjp(_matmul_w_at_res_fwd, _matmul_w_at_res_bwd)


def _rms_fwd_call(x, g, name):
    rows, d = x.shape
    tr = _tile(rows, 512, 8)

    def body(x_ref, g_ref, y_ref):
        xv = x_ref[...]
        rstd = lax.rsqrt(jnp.mean(xv * xv, axis=-1, keepdims=True) + RMS_EPS)
        y_ref[...] = xv * rstd * g_ref[...]

    return _pcall(
        body,
        name=name,
        grid=(rows // tr,),
        in_specs=[pl.BlockSpec((tr, d), lambda i: (i, 0)), pl.BlockSpec((1, d), lambda i: (0, 0))],
        out_specs=pl.BlockSpec((tr, d), lambda i: (i, 0)),
        out_shape=jax.ShapeDtypeStruct((rows, d), F32),
        compiler_params=_params(("parallel",)),
    )(x, g.reshape(1, d))


def _rms_bwd_call(x, g, dy, name):
    rows, d = x.shape
    tr = _tile(rows, 512, 8)
    nb = rows // tr

    def body(x_ref, g_ref, dy_ref, dx_ref, dg_ref):
        xv = x_ref[...]
        rstd = lax.rsqrt(jnp.mean(xv * xv, axis=-1, keepdims=True) + RMS_EPS)
        xh = xv * rstd
        dyv = dy_ref[...]
        dyg = dyv * g_ref[...]
        dx_ref[...] = rstd * (dyg - xh * jnp.mean(dyg * xh, axis=-1, keepdims=True))
        dg_ref[...] = jnp.sum((dyv * xh).reshape(tr // 8, 8, d), axis=0)

    dx, dgp = _pcall(
        body,
        name=name,
        grid=(nb,),
        in_specs=[
            pl.BlockSpec((tr, d), lambda i: (i, 0)),
            pl.BlockSpec((1, d), lambda i: (0, 0)),
            pl.BlockSpec((tr, d), lambda i: (i, 0)),
        ],
        out_specs=[pl.BlockSpec((tr, d), lambda i: (i, 0)), pl.BlockSpec((8, d), lambda i: (i, 0))],
        out_shape=[jax.ShapeDtypeStruct((rows, d), F32), jax.ShapeDtypeStruct((nb * 8, d), F32)],
        compiler_params=_params(("parallel",)),
    )(x, g.reshape(1, d), dy)
    return dx, jnp.sum(dgp, axis=0)


@functools.partial(jax.custom_vjp, nondiff_argnums=(2,))
def rmsnorm(x, g, name):
    return _rms_fwd_call(x, g, name + "_f")


def _rmsnorm_fwd(x, g, name):
    return _rms_fwd_call(x, g, name + "_f"), (x, g)


def _rmsnorm_bwd(name, res, dy):
    x, g = res
    return _rms_bwd_call(x, g, dy, name + "_b")


rmsnorm.defvjp(_rmsnorm_fwd, _rmsnorm_bwd)


def _attn_blocks(sq, sk):
    return _tile(sq, 1024), _tile(sk, 512)


def _attn_pt(qs, k_ref, qi, ki, bq, bk, masked, shift):
    st = jnp.dot(k_ref[0].astype(BF16), qs, preferred_element_type=F32)
    if masked:
        kpos = ki * bk + lax.broadcasted_iota(jnp.int32, (bk, bq), 0)
        qpos = qi * bq + lax.broadcasted_iota(jnp.int32, (bk, bq), 1)
        st = jnp.where(kpos <= qpos, st, MASK_VALUE)
    return st if shift is None else jnp.exp(st - shift)


def _attn_pairs(nq, nk, bq, bk, causal, k_major):
    ok = lambda i, j: (not causal) or j * bk <= i * bq + bq - 1
    if k_major:
        pairs = [(i, j) for j in range(nk) for i in range(nq) if ok(i, j)]
    else:
        pairs = [(i, j) for i in range(nq) for j in range(nk) if ok(i, j)]
    qtab = jnp.asarray([p[0] for p in pairs], jnp.int32)
    ktab = jnp.asarray([p[1] for p in pairs], jnp.int32)
    first_q = (lambda j: (j * bk) // bq) if causal else (lambda j: 0)
    last_k = (lambda i: jnp.minimum((i * bq + bq - 1) // bk, nk - 1)) if causal else (lambda i: nk - 1)
    return qtab, ktab, first_q, last_k


def _attn_cases(causal, qi, ki, bq, bk, step):
    if not causal:
        step(False)
        return
    pl.when(ki * bk + bk - 1 <= qi * bq)(functools.partial(step, False))
    pl.when(ki * bk + bk - 1 > qi * bq)(functools.partial(step, True))


def _attn_fwd_call(qt, k, vt, scale, causal, name):
    h, sk, dq = k.shape
    sq = qt.shape[1]
    dv = vt.shape[0] // h
    bq, bk = _attn_blocks(sq, sk)
    nq, nk = sq // bq, sk // bk

    qtab, ktab, _, last_k = _attn_pairs(nq, nk, bq, bk, causal, False)

    def body(qi_ref, ki_ref, q_ref, k_ref, v_ref, o_ref, lse_ref, m_sc, l_sc, acc_sc):
        qi, ki = qi_ref[pl.program_id(1)], ki_ref[pl.program_id(1)]

        @pl.when(ki == 0)
        def _():
            m_sc[...] = jnp.full(m_sc.shape, -jnp.inf, F32)
            l_sc[...] = jnp.zeros(l_sc.shape, F32)
            acc_sc[...] = jnp.zeros(acc_sc.shape, F32)

        def step(masked):
            qs = (q_ref[...] * scale).astype(BF16)
            st = _attn_pt(qs, k_ref, qi, ki, bq, bk, masked, None)
            m_prev = m_sc[...]
            m_new = jnp.maximum(m_prev, jnp.max(st, axis=0, keepdims=True))
            alpha = jnp.exp(m_prev - m_new)
            pt = jnp.exp(st - m_new)
            l_sc[...] = alpha * l_sc[...] + jnp.sum(pt, axis=0, keepdims=True)
            acc_sc[...] = alpha * acc_sc[...] + jnp.dot(v_ref[...].astype(BF16), pt.astype(BF16),
                                                        preferred_element_type=F32)
            m_sc[...] = m_new

        _attn_cases(causal, qi, ki, bq, bk, step)

        @pl.when(ki == last_k(qi))
        def _():
            o_ref[...] = acc_sc[...] / l_sc[...]
            lse_ref[0] = m_sc[...] + jnp.log(l_sc[...])

    qmap = lambda hh, p, qi_ref, ki_ref: (hh, qi_ref[p])
    return _pcall(
        body,
        name=name,
        grid_spec=pltpu.PrefetchScalarGridSpec(
            num_scalar_prefetch=2,
            grid=(h, qtab.shape[0]),
            in_specs=[
                pl.BlockSpec((dq, bq), qmap),
                pl.BlockSpec((1, bk, dq), lambda hh, p, qi_ref, ki_ref: (hh, ki_ref[p], 0)),
                pl.BlockSpec((dv, bk), lambda hh, p, qi_ref, ki_ref: (hh, ki_ref[p])),
            ],
            out_specs=[pl.BlockSpec((dv, bq), qmap),
                       pl.BlockSpec((1, 1, bq), lambda hh, p, qi_ref, ki_ref: (hh, 0, qi_ref[p]))],
            scratch_shapes=[pltpu.VMEM((1, bq), F32), pltpu.VMEM((1, bq), F32), pltpu.VMEM((dv, bq), F32)],
        ),
        out_shape=[jax.ShapeDtypeStruct((h * dv, sq), F32), jax.ShapeDtypeStruct((h, 1, sq), F32)],
        compiler_params=_params(("parallel", "arbitrary")),
    )(qtab, ktab, qt, k, vt)


_TN = (((0,), (0,)), ((), ()))
_NT = (((1,), (1,)), ((), ()))


def _attn_dst(qs, k_ref, v_ref, do_ref, lse_ref, dl_ref, qi, ki, bq, bk, masked):
    pt = _attn_pt(qs, k_ref, qi, ki, bq, bk, masked, lse_ref[0])
    dpt = lax.dot_general(v_ref[...].astype(BF16), do_ref[...].astype(BF16), _TN, preferred_element_type=F32)
    return pt, pt * (dpt - dl_ref[0])


def _attn_dq_call(qt, k, vt, dot, lse, delta, scale, causal, name):
    h, sk, dq = k.shape
    sq = qt.shape[1]
    dv = vt.shape[0] // h
    bq, bk = _attn_blocks(sq, sk)
    nq, nk = sq // bq, sk // bk

    qtab, ktab, _, last_k = _attn_pairs(nq, nk, bq, bk, causal, False)

    def body(qi_ref, ki_ref, q_ref, k_ref, v_ref, do_ref, lse_ref, dl_ref, dq_ref, acc_sc):
        qi, ki = qi_ref[pl.program_id(1)], ki_ref[pl.program_id(1)]

        @pl.when(ki == 0)
        def _():
            acc_sc[...] = jnp.zeros(acc_sc.shape, F32)

        def step(masked):
            qs = (q_ref[...] * scale).astype(BF16)
            _, dst = _attn_dst(qs, k_ref, v_ref, do_ref, lse_ref, dl_ref, qi, ki, bq, bk, masked)
            acc_sc[...] += lax.dot_general(k_ref[0].astype(BF16), dst.astype(BF16), _TN, preferred_element_type=F32)

        _attn_cases(causal, qi, ki, bq, bk, step)

        @pl.when(ki == last_k(qi))
        def _():
            dq_ref[...] = acc_sc[...] * scale

    qmap = lambda hh, p, qi_ref, ki_ref: (hh, qi_ref[p])
    rowmap = lambda hh, p, qi_ref, ki_ref: (hh, 0, qi_ref[p])
    return _pcall(
        body,
        name=name,
        grid_spec=pltpu.PrefetchScalarGridSpec(
            num_scalar_prefetch=2,
            grid=(h, qtab.shape[0]),
            in_specs=[
                pl.BlockSpec((dq, bq), qmap),
                pl.BlockSpec((1, bk, dq), lambda hh, p, qi_ref, ki_ref: (hh, ki_ref[p], 0)),
                pl.BlockSpec((dv, bk), lambda hh, p, qi_ref, ki_ref: (hh, ki_ref[p])),
                pl.BlockSpec((dv, bq), qmap),
                pl.BlockSpec((1, 1, bq), rowmap),
                pl.BlockSpec((1, 1, bq), rowmap),
            ],
            out_specs=pl.BlockSpec((dq, bq), qmap),
            scratch_shapes=[pltpu.VMEM((dq, bq), F32)],
        ),
        out_shape=jax.ShapeDtypeStruct((h * dq, sq), F32),
        compiler_params=_params(("parallel", "arbitrary")),
    )(qtab, ktab, qt, k, vt, dot, lse, delta)


def _attn_dkv_call(qt, k, vt, dot, lse, delta, scale, causal, name):
    h, sk, dq = k.shape
    sq = qt.shape[1]
    dv = vt.shape[0] // h
    bq, bk = _attn_blocks(sq, sk)
    nq, nk = sq // bq, sk // bk

    qtab, ktab, first_q, _ = _attn_pairs(nq, nk, bq, bk, causal, True)

    def body(qi_ref, ki_ref, q_ref, k_ref, v_ref, do_ref, lse_ref, dl_ref, dk_ref, dv_ref, dk_sc, dv_sc):
        qi, ki = qi_ref[pl.program_id(1)], ki_ref[pl.program_id(1)]

        @pl.when(qi == first_q(ki))
        def _():
            dk_sc[...] = jnp.zeros(dk_sc.shape, F32)
            dv_sc[...] = jnp.zeros(dv_sc.shape, F32)

        def step(masked):
            qs = (q_ref[...] * scale).astype(BF16)
            pt, dst = _attn_dst(qs, k_ref, v_ref, do_ref, lse_ref, dl_ref, qi, ki, bq, bk, masked)
            dv_sc[...] += lax.dot_general(do_ref[...].astype(BF16), pt.astype(BF16), _NT, preferred_element_type=F32)
            dk_sc[...] += lax.dot_general(dst.astype(BF16), qs, _NT, preferred_element_type=F32)

        _attn_cases(causal, qi, ki, bq, bk, step)

        @pl.when(qi == nq - 1)
        def _():
            dk_ref[0] = dk_sc[...]
            dv_ref[...] = dv_sc[...]

    qmap = lambda hh, p, qi_ref, ki_ref: (hh, qi_ref[p])
    rowmap = lambda hh, p, qi_ref, ki_ref: (hh, 0, qi_ref[p])
    kmap = lambda hh, p, qi_ref, ki_ref: (hh, ki_ref[p], 0)
    vmap = lambda hh, p, qi_ref, ki_ref: (hh, ki_ref[p])
    return _pcall(
        body,
        name=name,
        grid_spec=pltpu.PrefetchScalarGridSpec(
            num_scalar_prefetch=2,
            grid=(h, qtab.shape[0]),
            in_specs=[
                pl.BlockSpec((dq, bq), qmap),
                pl.BlockSpec((1, bk, dq), kmap),
                pl.BlockSpec((dv, bk), vmap),
                pl.BlockSpec((dv, bq), qmap),
                pl.BlockSpec((1, 1, bq), rowmap),
                pl.BlockSpec((1, 1, bq), rowmap),
            ],
            out_specs=[pl.BlockSpec((1, bk, dq), kmap), pl.BlockSpec((dv, bk), vmap)],
            scratch_shapes=[pltpu.VMEM((bk, dq), F32), pltpu.VMEM((dv, bk), F32)],
        ),
        out_shape=[jax.ShapeDtypeStruct((h, sk, dq), F32), jax.ShapeDtypeStruct((h * dv, sk), F32)],
        compiler_params=_params(("parallel", "arbitrary")),
    )(qtab, ktab, qt, k, vt, dot, lse, delta)


@functools.partial(jax.custom_vjp, nondiff_argnums=(3, 4, 5))
def attention(qt, k, vt, scale, causal, name):
    return _attn_fwd_call(qt, k, vt, scale, causal, name + "_f")[0]


def _attention_fwd(qt, k, vt, scale, causal, name):
    ot, lse = _attn_fwd_call(qt, k, vt, scale, causal, name + "_f")
    return ot, (qt, k, vt, ot, lse)


def _attention_bwd(scale, causal, name, res, dot):
    qt, k, vt, ot, lse = res
    h = k.shape[0]
    delta = jnp.sum((dot * ot).reshape(h, ot.shape[0] // h, ot.shape[1]), axis=1, keepdims=True)
    dqt = _attn_dq_call(qt, k, vt, dot, lse, delta, scale, causal, name + "_dq")
    dk, dvt = _attn_dkv_call(qt, k, vt, dot, lse, delta, scale, causal, name + "_dkv")
    return dqt, dk, dvt


attention.defvjp(_attention_fwd, _attention_bwd)


SCAN_ROWS = 8
SCAN_LANES = 256
SCAN_TBLOCK = 512
SCAN_UNROLL = 4


def _scan_tables(lr, li):
    pr, pi = [lr], [li]
    for _ in range(SCAN_ROWS - 1):
        pr, pi = pr + [pr[-1] * lr - pi[-1] * li], pi + [pr[-1] * li + pi[-1] * lr]
    return jnp.concatenate(pr, axis=0), jnp.concatenate(pi, axis=0)


def _scan_call(ur, ui, lr, li, reverse, name, states=None):
    s, n = ur.shape
    tb = _tile(s, SCAN_TBLOCK, 8)
    nt, nl = s // tb, n // SCAN_LANES
    ntile = tb // SCAN_ROWS
    pr, pi = _scan_tables(lr, li)
    if reverse:
        pr, pi = pr[::-1], pi[::-1]
    shape = (SCAN_ROWS, SCAN_LANES)
    with_states = states is not None
    assert reverse or not with_states

    def body(*refs):
        if with_states:
            ur_ref, ui_ref, pr_ref, pi_ref, sr_ref, si_ref, xr_ref, xi_ref, dr_ref, di_ref, cr_sc, ci_sc, ar_sc, ai_sc = refs
        else:
            ur_ref, ui_ref, pr_ref, pi_ref, xr_ref, xi_ref, cr_sc, ci_sc, ar_sc, ai_sc = refs

        @pl.when(pl.program_id(1) == 0)
        def _():
            for sc in (cr_sc, ci_sc, ar_sc, ai_sc):
                sc[...] = jnp.zeros(shape, F32)

        prv, piv = pr_ref[...], pi_ref[...]
        rows = lax.broadcasted_iota(jnp.int32, shape, 0)

        def powers(k):
            r = (SCAN_ROWS - k) if reverse else (k - 1)
            return jnp.broadcast_to(prv[r:r + 1], shape), jnp.broadcast_to(piv[r:r + 1], shape)

        pw = [powers(k) for k in (1, 2, 4)]

        def tile(i, carry):
            cr, ci, acr, aci = carry
            j = (ntile - 1 - i) if reverse else i
            sl = pl.ds(pl.multiple_of(j * SCAN_ROWS, SCAN_ROWS), SCAN_ROWS)
            xr, xi = ur_ref[sl, :], ui_ref[sl, :]
            for (ar, ai), k in zip(pw, (1, 2, 4)):
                if reverse:
                    keep = rows < SCAN_ROWS - k
                    sr = jnp.where(keep, pltpu.roll(xr, SCAN_ROWS - k, 0), 0.0)
                    si = jnp.where(keep, pltpu.roll(xi, SCAN_ROWS - k, 0), 0.0)
                else:
                    keep = rows >= k
                    sr = jnp.where(keep, pltpu.roll(xr, k, 0), 0.0)
                    si = jnp.where(keep, pltpu.roll(xi, k, 0), 0.0)
                xr, xi = xr + (ar * sr - ai * si), xi + (ar * si + ai * sr)
            xr, xi = xr + (prv * cr - piv * ci), xi + (prv * ci + piv * cr)
            xr_ref[sl, :] = xr
            xi_ref[sl, :] = xi
            if with_states:
                last = rows == SCAN_ROWS - 1
                nr = jnp.where(last, cr, pltpu.roll(xr, SCAN_ROWS - 1, 0))
                ni = jnp.where(last, ci, pltpu.roll(xi, SCAN_ROWS - 1, 0))
                sr, si = sr_ref[sl, :], si_ref[sl, :]
                acr, aci = acr + (sr * nr + si * ni), aci + (sr * ni - si * nr)
            e = 0 if reverse else SCAN_ROWS - 1
            return jnp.broadcast_to(xr[e:e + 1], shape), jnp.broadcast_to(xi[e:e + 1], shape), acr, aci

        def tiles(g, carry):
            for u in range(SCAN_UNROLL):
                carry = tile(g * SCAN_UNROLL + u, carry)
            return carry

        assert ntile % SCAN_UNROLL == 0
        cr, ci, acr, aci = lax.fori_loop(0, ntile // SCAN_UNROLL, tiles,
                                         (cr_sc[...], ci_sc[...], ar_sc[...], ai_sc[...]))
        cr_sc[...] = cr
        ci_sc[...] = ci
        if with_states:
            ar_sc[...] = acr
            ai_sc[...] = aci

            @pl.when(pl.program_id(1) == nt - 1)
            def _():
                dr_ref[...] = acr
                di_ref[...] = aci

    tmap = (lambda l, t: (nt - 1 - t, l)) if reverse else (lambda l, t: (t, l))
    blk = pl.BlockSpec((tb, SCAN_LANES), tmap)
    tab = pl.BlockSpec((SCAN_ROWS, SCAN_LANES), lambda l, t: (0, l))
    full = jax.ShapeDtypeStruct((s, n), F32)
    return _pcall(
        body,
        name=name,
        grid=(nl, nt),
        in_specs=[blk, blk, tab, tab] + ([blk, blk] if with_states else []),
        out_specs=[blk, blk] + ([tab, tab] if with_states else []),
        out_shape=[full, full] + ([jax.ShapeDtypeStruct((SCAN_ROWS, n), F32)] * 2 if with_states else []),
        scratch_shapes=[pltpu.VMEM(shape, F32)] * 4,
        compiler_params=_params(("parallel", "arbitrary")),
    )(ur, ui, pr, pi, *(states if with_states else ()))


@functools.partial(jax.custom_vjp, nondiff_argnums=(4,))
def s5_scan(ur, ui, lr, li, name):
    return tuple(_scan_call(ur, ui, lr, li, False, name + "_f"))


def _s5_scan_fwd(ur, ui, lr, li, name):
    xr, xi = _scan_call(ur, ui, lr, li, False, name + "_f")
    return (xr, xi), (xr, xi, lr, li)


def _s5_scan_bwd(name, res, g):
    xr, xi, lr, li = res
    gr, gi, dlr, dli = _scan_call(g[0], g[1], lr, -li, True, name + "_b", states=(xr, xi))
    return gr, gi, jnp.sum(dlr, axis=0, keepdims=True), jnp.sum(dli, axis=0, keepdims=True)


s5_scan.defvjp(_s5_scan_fwd, _s5_scan_bwd)


def _running_sum(x, reverse):
    n = x.shape[0]
    rows = lax.broadcasted_iota(jnp.int32, x.shape, 0)
    k = 1
    while k < n:
        if reverse:
            x = x + jnp.where(rows < n - k, pltpu.roll(x, n - k, 0), 0.0)
        else:
            x = x + jnp.where(rows >= k, pltpu.roll(x, k, 0), 0.0)
        k *= 2
    return x


@jax.custom_vjp
def cumsum_rows(x):
    return _running_sum(x, False)


cumsum_rows.defvjp(lambda x: (_running_sum(x, False), None), lambda _, g: (_running_sum(g, True),))


def _hg_chunk(HG_SUB, q, k, v, lf, st):
    c = HG_CHUNK
    b = cumsum_rows(lf)
    qb = q * jnp.exp(b)
    o = lax.dot_general(qb.astype(BF16), st.astype(BF16), (((1,), (1,)), ((), ())), preferred_element_type=F32)
    nsub = c // HG_SUB
    srow = lax.broadcasted_iota(jnp.int32, (HG_SUB, HG_SUB, 1), 0)
    scol = lax.broadcasted_iota(jnp.int32, (HG_SUB, HG_SUB, 1), 1)
    smask = scol <= srow
    outs = []
    for i in range(nsub):
        lo = i * HG_SUB
        bi, qi, ki = b[lo:lo + HG_SUB], q[lo:lo + HG_SUB], k[lo:lo + HG_SUB]
        diff = jnp.where(smask, bi[:, None, :] - bi[None, :, :], 0.0)
        e = jnp.where(smask, jnp.exp(diff), 0.0)
        a_ii = jnp.sum(qi[:, None, :] * ki[None, :, :] * e, axis=-1)
        oi = jnp.dot(a_ii.astype(BF16), v[lo:lo + HG_SUB].astype(BF16), preferred_element_type=F32)
        if i > 0:
            r = b[lo - 1:lo]
            qt = qi * jnp.exp(bi - r)
            kt = k[:lo] * jnp.exp(r - b[:lo])
            a_ij = lax.dot_general(qt.astype(BF16), kt.astype(BF16), (((1,), (1,)), ((), ())),
                                   preferred_element_type=F32)
            oi = oi + jnp.dot(a_ij.astype(BF16), v[:lo].astype(BF16), preferred_element_type=F32)
        outs.append(oi)
    o = o + jnp.concatenate(outs, axis=0)
    bl = b[c - 1:c]
    kd = k * jnp.exp(bl - b)
    st_new = st * jnp.exp(bl) + lax.dot_general(v.astype(BF16), kd.astype(BF16), (((0,), (0,)), ((), ())),
                                                preferred_element_type=F32)
    return o, st_new


def _hg_fwd_call(q, k, v, lf, name):
    s, w = q.shape
    nc = s // HG_CHUNK
    blk = pl.BlockSpec((HG_CHUNK, w), lambda i: (i, 0))

    def body(q_ref, k_ref, v_ref, lf_ref, o_ref, st_ref, st_sc):
        @pl.when(pl.program_id(0) == 0)
        def _():
            st_sc[...] = jnp.zeros(st_sc.shape, F32)

        for hh in range(HG_HEADS):
            cs = slice(hh * HG_DK, (hh + 1) * HG_DK)
            st = st_sc[hh]
            st_ref[0, hh] = st
            o, st_new = _hg_chunk(HG_SUB_FWD, q_ref[:, cs], k_ref[:, cs], v_ref[:, cs], lf_ref[:, cs], st)
            o_ref[:, cs] = o
            st_sc[hh] = st_new

    return _pcall(
        body,
        name=name,
        grid=(nc,),
        in_specs=[blk, blk, blk, blk],
        out_specs=[blk, pl.BlockSpec((1, HG_HEADS, HG_DV, HG_DK), lambda i: (i, 0, 0, 0))],
        out_shape=[jax.ShapeDtypeStruct((s, w), F32), jax.ShapeDtypeStruct((nc, HG_HEADS, HG_DV, HG_DK), F32)],
        scratch_shapes=[pltpu.VMEM((HG_HEADS, HG_DV, HG_DK), F32)],
        compiler_params=_params(("arbitrary",)),
    )(q, k, v, lf)


def _hg_bwd_call(q, k, v, lf, sts, do, name):
    s, w = q.shape
    nc = s // HG_CHUNK
    blk = pl.BlockSpec((HG_CHUNK, w), lambda i: (nc - 1 - i, 0))

    def body(q_ref, k_ref, v_ref, lf_ref, st_ref, do_ref, dq_ref, dk_ref, dv_ref, dlf_ref, dst_sc):
        @pl.when(pl.program_id(0) == 0)
        def _():
            dst_sc[...] = jnp.zeros(dst_sc.shape, F32)

        def head(hh, carry):
            cs = pl.ds(pl.multiple_of(hh * HG_DK, HG_DK), HG_DK)
            _, vjp = jax.vjp(functools.partial(_hg_chunk, HG_SUB_BWD), q_ref[:, cs], k_ref[:, cs], v_ref[:, cs],
                             lf_ref[:, cs], st_ref[0, hh])
            dq, dk, dv, dlf, dst = vjp((do_ref[:, cs], dst_sc[hh]))
            dq_ref[:, cs] = dq
            dk_ref[:, cs] = dk
            dv_ref[:, cs] = dv
            dlf_ref[:, cs] = dlf
            dst_sc[hh] = dst
            return carry

        lax.fori_loop(0, HG_HEADS, head, 0)

    out = jax.ShapeDtypeStruct((s, w), F32)
    return _pcall(
        body,
        name=name,
        grid=(nc,),
        in_specs=[blk, blk, blk, blk, pl.BlockSpec((1, HG_HEADS, HG_DV, HG_DK), lambda i: (nc - 1 - i, 0, 0, 0)), blk],
        out_specs=[blk, blk, blk, blk],
        out_shape=[out, out, out, out],
        scratch_shapes=[pltpu.VMEM((HG_HEADS, HG_DV, HG_DK), F32)],
        compiler_params=_params(("arbitrary",)),
    )(q, k, v, lf, sts, do)


@functools.partial(jax.custom_vjp, nondiff_argnums=(4,))
def hgrn_core(q, k, v, lf, name):
    return _hg_fwd_call(q, k, v, lf, name + "_f")[0]


def _hgrn_core_fwd(q, k, v, lf, name):
    o, sts = _hg_fwd_call(q, k, v, lf, name + "_f")
    return o, (q, k, v, lf, sts)


def _hgrn_core_bwd(name, res, do):
    q, k, v, lf, sts = res
    return tuple(_hg_bwd_call(q, k, v, lf, sts, do, name + "_b"))


hgrn_core.defvjp(_hgrn_core_fwd, _hgrn_core_bwd)


def _adamw_call(w, g, m, v, name):
    rows, cols = w.shape
    tr = _tile(rows, 256, 8)
    c1 = 1.0 / (1.0 - ADAM_B1 ** ADAM_STEP)
    c2 = 1.0 / (1.0 - ADAM_B2 ** ADAM_STEP)

    def body(w_ref, g_ref, m_ref, v_ref, d_ref, mo_ref, vo_ref):
        gv = g_ref[...]
        mn = ADAM_B1 * m_ref[...] + (1.0 - ADAM_B1) * gv
        vn = ADAM_B2 * v_ref[...] + (1.0 - ADAM_B2) * (gv * gv)
        d_ref[...] = -ADAM_LR * ((mn * c1) / (jnp.sqrt(vn * c2) + ADAM_EPS) + ADAM_WD * w_ref[...])
        mo_ref[...] = mn
        vo_ref[...] = vn

    blk = pl.BlockSpec((tr, cols), lambda i: (i, 0))
    out = jax.ShapeDtypeStruct((rows, cols), F32)
    return _pcall(
        body,
        name=name,
        grid=(rows // tr,),
        in_specs=[blk, blk, blk, blk],
        out_specs=[blk, blk, blk],
        out_shape=[out, out, out],
        compiler_params=_params(("parallel",)),
    )(w, g, m, v)


ANY = pl.BlockSpec(memory_space=pl.ANY)


def _place():
    return lax.axis_index("x"), lax.axis_index("y"), lax.axis_index("c")


def _other_chips(x, y):
    return [(1 - x, y), (x, 1 - y), (1 - x, 1 - y)]


def _gather_weights(shards):
    nw = len(shards)

    def body(*refs):
        w_refs, g_refs, (send_sems, recv_sems) = refs[:nw], refs[nw:2 * nw], refs[2 * nw:]
        x, y, c = _place()
        myq = 2 * x + y
        sibling = (x, y, 1 - c)
        chips = _other_chips(x, y)

        def copy(k, src, dst, to):
            return pltpu.make_async_remote_copy(src_ref=src, dst_ref=dst, send_sem=send_sems.at[k],
                                                recv_sem=recv_sems.at[k], device_id=to, device_id_type=MESH)

        first = []
        for i in range(nw):
            for j, (px, py) in enumerate(chips):
                first.append(copy(6 * i + j, w_refs[i].at[c], g_refs[i].at[myq, c], (px, py, c)))
                first[-1].start()
        passed = []
        for i in range(nw):
            for j, (px, py) in enumerate(chips):
                q = 2 * px + py
                copy(6 * i + j, w_refs[i].at[c], g_refs[i].at[q, c], (px, py, c)).wait_recv()
                passed.append(copy(6 * i + 3 + j, g_refs[i].at[q, c], g_refs[i].at[q, c], sibling))
                passed[-1].start()
        for i in range(nw):
            for j, (px, py) in enumerate(chips):
                q = 2 * px + py
                copy(6 * i + 3 + j, g_refs[i].at[q, 1 - c], g_refs[i].at[q, 1 - c], sibling).wait_recv()
        for cp in first + passed:
            cp.wait_send()

    got = _pcall(
        body,
        name="gather_weights",
        in_specs=[ANY] * nw,
        out_specs=[ANY] * nw,
        out_shape=[jax.ShapeDtypeStruct((N_CHIPS,) + s.shape, s.dtype) for s in shards],
        scratch_shapes=[pltpu.SemaphoreType.DMA((6 * nw,)), pltpu.SemaphoreType.DMA((6 * nw,))],
    )(*shards)
    my_chip = 2 * lax.axis_index("x") + lax.axis_index("y")
    return [lax.dynamic_update_slice(g, s[None], (my_chip, 0, 0, 0)) for g, s in zip(got, shards)]


HBM = pl.BlockSpec(memory_space=pltpu.HBM)
SEM = pl.BlockSpec(memory_space=pltpu.SEMAPHORE)


def _gather_start(shards, after):
    nw = len(shards)

    def body(*refs):
        w_refs, land_refs = refs[:nw], refs[nw:2 * nw]
        send_sems, recv_sems = refs[2 * nw + 1:2 * nw + 3]
        token = refs[-1]
        x, y, c = _place()
        myq = 2 * x + y
        for i in range(nw):
            hh = shards[i].shape[0] // 2
            rows = pl.ds(c * hh, hh)
            for j, (px, py) in enumerate(_other_chips(x, y)):
                for tc in range(2):
                    pltpu.make_async_remote_copy(
                        src_ref=w_refs[i].at[rows], dst_ref=land_refs[i].at[myq, rows],
                        send_sem=send_sems.at[6 * i + 2 * j + tc], recv_sem=recv_sems.at[6 * i + 2 * j + c],
                        device_id=(px, py, tc), device_id_type=MESH).start()
        token[...] = jnp.zeros_like(token)

    lands = [lax.empty((N_CHIPS,) + s.shape, s.dtype) for s in shards]
    out = _pcall(
        body,
        name="gather_start",
        in_specs=[HBM] * (2 * nw) + [pl.BlockSpec(memory_space=pl.ANY)],
        out_specs=[SEM, SEM] + [HBM] * (2 * nw) + [pl.BlockSpec(memory_space=pltpu.VMEM)],
        out_shape=[pltpu.SemaphoreType.DMA((6 * nw,)), pltpu.SemaphoreType.DMA((6 * nw,))]
        + [pltpu.HBM(s.shape, s.dtype) for s in shards] + [pltpu.HBM(t.shape, t.dtype) for t in lands]
        + [jax.ShapeDtypeStruct((8, 128), F32)],
        input_output_aliases={i: 2 + i for i in range(2 * nw)},
        compiler_params=pltpu.CompilerParams(has_side_effects=pltpu.SideEffectType.DATAFLOW_SIDE_EFFECTING),
    )(*[pltpu.with_memory_space_constraint(t, pltpu.HBM) for t in list(shards) + lands], after)
    return out[0], out[1], out[2:2 + nw], out[2 + nw:2 + 2 * nw], out[-1]


def _gather_wait(send_sems, recv_sems, shards, lands, after):
    nw = len(shards)

    def body(*refs):
        w_refs, land_refs = refs[:nw], refs[nw:2 * nw]
        send_sems, recv_sems = refs[2 * nw:2 * nw + 2]
        x, y, c = _place()
        myq = 2 * x + y
        for i in range(nw):
            hh = shards[i].shape[0] // 2
            for j, (px, py) in enumerate(_other_chips(x, y)):
                q = 2 * px + py
                for other in range(2):
                    mine = pl.ds(c * hh, hh)
                    theirs = pl.ds(other * hh, hh)
                    pltpu.make_async_remote_copy(
                        src_ref=w_refs[i].at[mine], dst_ref=land_refs[i].at[myq, mine],
                        send_sem=send_sems.at[6 * i + 2 * j + other], recv_sem=recv_sems.at[6 * i + 2 * j + other],
                        device_id=(px, py, other), device_id_type=MESH).wait_send()
                    pltpu.make_async_remote_copy(
                        src_ref=w_refs[i].at[theirs], dst_ref=land_refs[i].at[q, theirs],
                        send_sem=send_sems.at[6 * i + 2 * j + other], recv_sem=recv_sems.at[6 * i + 2 * j + other],
                        device_id=(px, py, other), device_id_type=MESH).wait_recv()

    out = _pcall(
        body,
        name="gather_wait",
        in_specs=[HBM] * (2 * nw) + [SEM, SEM, pl.BlockSpec(memory_space=pl.ANY)],
        out_specs=[HBM] * (2 * nw),
        out_shape=[pltpu.HBM(s.shape, s.dtype) for s in shards] + [pltpu.HBM(t.shape, t.dtype) for t in lands],
        input_output_aliases={i: i for i in range(2 * nw)},
        compiler_params=pltpu.CompilerParams(has_side_effects=pltpu.SideEffectType.DATAFLOW_SIDE_EFFECTING),
    )(*shards, *lands, send_sems, recv_sems, after)
    my_chip = 2 * lax.axis_index("x") + lax.axis_index("y")
    return [lax.dynamic_update_slice(g, s[None], (my_chip, 0, 0)) for g, s in zip(out[nw:], out[:nw])]


def _swap_halves_to_sibling(grs, tag=""):
    nw = len(grs)

    def body(*refs):
        g_refs, t_refs, (send_sems, recv_sems) = refs[:nw], refs[nw:2 * nw], refs[2 * nw:]
        x, y, c = _place()
        cps = []
        for i in range(nw):
            hh = grs[i].shape[1] // 2
            cps.append(pltpu.make_async_remote_copy(
                src_ref=g_refs[i].at[:, pl.ds((1 - c) * hh, hh)], dst_ref=t_refs[i], send_sem=send_sems.at[i],
                recv_sem=recv_sems.at[i], device_id=(x, y, 1 - c), device_id_type=MESH))
            cps[-1].start()
        for cp in cps:
            cp.wait()

    return _pcall(
        body,
        name="grad_d2d" + tag,
        in_specs=[ANY] * nw,
        out_specs=[ANY] * nw,
        out_shape=[jax.ShapeDtypeStruct((g.shape[0], g.shape[1] // 2, g.shape[2]), g.dtype) for g in grs],
        scratch_shapes=[pltpu.SemaphoreType.DMA((nw,)), pltpu.SemaphoreType.DMA((nw,))],
    )(*grs)


def _add_half(gr, t1, place, name):
    nq, r, wd = gr.shape
    hh = r // 2
    tr = _tile(hh, 512, 16)
    nb = hh // tr

    def body(p_ref, g_ref, t_ref, o_ref):
        o_ref[...] = (g_ref[...] + t_ref[...]).astype(BF16)

    return _pcall(
        body,
        name=name,
        grid_spec=pltpu.PrefetchScalarGridSpec(
            num_scalar_prefetch=1,
            grid=(nq, nb),
            in_specs=[
                pl.BlockSpec((1, tr, wd), lambda q, i, p_ref: (q, p_ref[0] * nb + i, 0)),
                pl.BlockSpec((1, tr, wd), lambda q, i, p_ref: (q, i, 0)),
            ],
            out_specs=pl.BlockSpec((1, tr, wd), lambda q, i, p_ref: (q, i, 0)),
        ),
        out_shape=jax.ShapeDtypeStruct((nq, hh, wd), BF16),
        compiler_params=_params(("parallel", "parallel")),
    )(place, gr, t1)


def _scatter_copies(p_refs, t_refs, send_sems, recv_sems):
    x, y, c = _place()
    return [pltpu.make_async_remote_copy(
        src_ref=p_refs[i].at[2 * px + py], dst_ref=t_refs[i].at[j], send_sem=send_sems.at[3 * i + j],
        recv_sem=recv_sems.at[3 * i + j], device_id=(px, py, c), device_id_type=MESH)
        for i in range(len(p_refs)) for j, (px, py) in enumerate(_other_chips(x, y))]


def _scatter_start(ps, after, tag=""):
    nw = len(ps)

    def body(*refs):
        send_sems, recv_sems = refs[2 * nw + 1:2 * nw + 3]
        for cp in _scatter_copies(refs[:nw], refs[nw:2 * nw], send_sems, recv_sems):
            cp.start()
        refs[-1][...] = jnp.zeros_like(refs[-1])

    lands = [lax.empty((3,) + p.shape[1:], p.dtype) for p in ps]
    out = _pcall(
        body,
        name="scatter_start" + tag,
        in_specs=[HBM] * (2 * nw) + [pl.BlockSpec(memory_space=pl.ANY)],
        out_specs=[SEM, SEM] + [HBM] * (2 * nw) + [pl.BlockSpec(memory_space=pltpu.VMEM)],
        out_shape=[pltpu.SemaphoreType.DMA((3 * nw,)), pltpu.SemaphoreType.DMA((3 * nw,))]
        + [pltpu.HBM(t.shape, t.dtype) for t in list(ps) + lands] + [jax.ShapeDtypeStruct((8, 128), F32)],
        input_output_aliases={i: 2 + i for i in range(2 * nw)},
        compiler_params=pltpu.CompilerParams(has_side_effects=pltpu.SideEffectType.DATAFLOW_SIDE_EFFECTING),
    )(*[pltpu.with_memory_space_constraint(t, pltpu.HBM) for t in list(ps) + lands], after)
    return out[0], out[1], out[2:2 + nw], out[2 + nw:2 + 2 * nw], out[-1]


def _scatter_wait(send_sems, recv_sems, ps, lands, after, tag=""):
    nw = len(ps)

    def body(*refs):
        for cp in _scatter_copies(refs[:nw], refs[nw:2 * nw], refs[2 * nw], refs[2 * nw + 1]):
            cp.wait_send()
            cp.wait_recv()

    out = _pcall(
        body,
        name="scatter_wait" + tag,
        in_specs=[HBM] * (2 * nw) + [SEM, SEM, pl.BlockSpec(memory_space=pl.ANY)],
        out_specs=[HBM] * (2 * nw),
        out_shape=[pltpu.HBM(t.shape, t.dtype) for t in list(ps) + list(lands)],
        input_output_aliases={i: i for i in range(2 * nw)},
        compiler_params=pltpu.CompilerParams(has_side_effects=pltpu.SideEffectType.DATAFLOW_SIDE_EFFECTING),
    )(*ps, *lands, send_sems, recv_sems, after)
    return out[nw:]


def _add_partials(gr, t1, t2, place, name):
    nq, r, wd = gr.shape
    hh = r // 2
    tr = _tile(hh, 512, 16)
    nb = hh // tr

    def body(p_ref, g_ref, s_ref, t_ref, o_ref):
        o_ref[...] = (((g_ref[0] + s_ref[0]) + t_ref[0].astype(F32)) + t_ref[1].astype(F32)) + t_ref[2].astype(F32)

    return _pcall(
        body,
        name=name,
        grid_spec=pltpu.PrefetchScalarGridSpec(
            num_scalar_prefetch=1,
            grid=(nb,),
            in_specs=[
                pl.BlockSpec((1, tr, wd), lambda i, p_ref: (p_ref[1], p_ref[0] * nb + i, 0)),
                pl.BlockSpec((1, tr, wd), lambda i, p_ref: (p_ref[1], i, 0)),
                pl.BlockSpec((3, tr, wd), lambda i, p_ref: (0, i, 0)),
            ],
            out_specs=pl.BlockSpec((tr, wd), lambda i, p_ref: (i, 0)),
        ),
        out_shape=jax.ShapeDtypeStruct((hh, wd), F32),
        compiler_params=_params(("parallel",)),
    )(place, gr, t1, t2)


def _join_halves(halves):
    nw = len(halves)

    def body(*refs):
        h_refs, f_refs, (send_sems, recv_sems) = refs[:nw], refs[nw:2 * nw], refs[2 * nw:]
        x, y, c = _place()

        def copy(i, rows):
            hh = halves[i].shape[0]
            return pltpu.make_async_remote_copy(src_ref=h_refs[i], dst_ref=f_refs[i].at[pl.ds(rows * hh, hh)],
                                                send_sem=send_sems.at[i], recv_sem=recv_sems.at[i],
                                                device_id=(x, y, 1 - c), device_id_type=MESH)

        for i in range(nw):
            copy(i, c).start()
        for i in range(nw):
            copy(i, c).wait_send()
            copy(i, 1 - c).wait_recv()

    got = _pcall(
        body,
        name="grad_join",
        in_specs=[ANY] * nw,
        out_specs=[ANY] * nw,
        out_shape=[jax.ShapeDtypeStruct((2 * h.shape[0], h.shape[1]), h.dtype) for h in halves],
        scratch_shapes=[pltpu.SemaphoreType.DMA((nw,)), pltpu.SemaphoreType.DMA((nw,))],
    )(*halves)
    c = lax.axis_index("c")
    return [lax.dynamic_update_slice(g, h, (c * h.shape[0], 0)) for g, h in zip(got, halves)]


N_DEV = 8


def _allreduce_small(buf):
    rows, wd = buf.shape
    n = rows // N_DEV

    def body(x_ref, o_ref, stage_ref, send_sems, recv_sems):
        x, y, c = _place()
        me = 4 * x + 2 * y + c

        def peer(k):
            return (x ^ (k >> 2), y ^ ((k >> 1) & 1), c ^ (k & 1))

        def chunk(ref, d):
            return ref.at[pl.ds(pl.multiple_of(d * n, 8), n)]

        first = []
        for k in range(1, N_DEV):
            px, py, pc = peer(k)
            cp = pltpu.make_async_remote_copy(src_ref=chunk(x_ref, 4 * px + 2 * py + pc), dst_ref=stage_ref.at[k],
                                              send_sem=send_sems.at[k - 1], recv_sem=recv_sems.at[k - 1],
                                              device_id=(px, py, pc), device_id_type=MESH)
            cp.start()
            first.append(cp)
        acc = chunk(x_ref, me)[...]
        for k in range(1, N_DEV):
            first[k - 1].wait_recv()
            acc = acc + stage_ref[k]
        chunk(o_ref, me)[...] = acc
        second = []
        for k in range(1, N_DEV):
            cp = pltpu.make_async_remote_copy(src_ref=chunk(o_ref, me), dst_ref=chunk(o_ref, me),
                                              send_sem=send_sems.at[6 + k], recv_sem=recv_sems.at[6 + k],
                                              device_id=peer(k), device_id_type=MESH)
            cp.start()
            second.append(cp)
        for k in range(1, N_DEV):
            px, py, pc = peer(k)
            them = 4 * px + 2 * py + pc
            pltpu.make_async_remote_copy(src_ref=chunk(o_ref, them), dst_ref=chunk(o_ref, them),
                                         send_sem=send_sems.at[6 + k], recv_sem=recv_sems.at[6 + k],
                                         device_id=(px, py, pc), device_id_type=MESH).wait_recv()
        for cp in first + second:
            cp.wait_send()

    vmem = pl.BlockSpec(memory_space=pltpu.VMEM)
    return _pcall(
        body,
        name="allreduce_small",
        in_specs=[vmem],
        out_specs=vmem,
        out_shape=jax.ShapeDtypeStruct((rows, wd), F32),
        scratch_shapes=[pltpu.VMEM((N_DEV, n, wd), F32), pltpu.SemaphoreType.DMA((14,)),
                        pltpu.SemaphoreType.DMA((14,))],
    )(buf)


@functools.partial(jax.custom_vjp, nondiff_argnums=(1,))
def split_cols(z, spec):
    outs = []
    for segs in spec[0]:
        parts = [z[:, s:s + n] for s, n in segs]
        outs.append(parts[0] if len(parts) == 1 else jnp.concatenate(parts, axis=1))
    return tuple(outs)


def _split_cols_fwd(z, spec):
    return split_cols(z, spec), None


def _split_cols_bwd(spec, _, gs):
    pieces, width = spec
    segs = []
    for p, piece in enumerate(pieces):
        off = 0
        for s, n in piece:
            segs.append((s, n, p, off))
            off += n
    rows, parts, pos = gs[0].shape[0], [], 0
    for s, n, p, off in sorted(segs):
        if s > pos:
            parts.append(jnp.zeros((rows, s - pos), gs[0].dtype))
        parts.append(gs[p] if (off == 0 and n == gs[p].shape[1]) else gs[p][:, off:off + n])
        pos = s + n
    if pos < width:
        parts.append(jnp.zeros((rows, width - pos), gs[0].dtype))
    return (jnp.concatenate(parts, axis=1),)


split_cols.defvjp(_split_cols_fwd, _split_cols_bwd)


def _contiguous(sizes):
    pieces, off = [], 0
    for n in sizes:
        pieces.append(((off, n),))
        off += n
    return tuple(pieces), off


def _rope(t, cos, sin):
    half = t.shape[-1] // 2
    t1, t2 = t[..., :half], t[..., half:]
    return jnp.concatenate([t1 * cos - t2 * sin, t2 * cos + t1 * sin], axis=-1)


def _s5_mixer(u, sp, l, w16, toks):
    g, p_, h = SSM_GROUPS, SSM_STATE, SSM_GROUP_CH
    lam_re, lam_im = sp["ssm_lam_re", l], sp["ssm_lam_im", l]
    step = jnp.exp(sp["ssm_log_step", l])[:, None]
    mag, ang = jnp.exp(lam_re * step), lam_im * step
    lbr, lbi = mag * jnp.cos(ang), mag * jnp.sin(ang)
    den = lam_re * lam_re + lam_im * lam_im
    nr, ni = lbr - 1.0, lbi
    qr, qi = (nr * lam_re + ni * lam_im) / den, (ni * lam_re - nr * lam_im) / den
    b_re, b_im = sp["ssm_b_re", l], sp["ssm_b_im", l]
    bbr = qr[..., None] * b_re - qi[..., None] * b_im
    bbi = qr[..., None] * b_im + qi[..., None] * b_re
    nb = g // SSM_PACK
    eye = jnp.eye(SSM_PACK, dtype=F32)

    def in_blocks(t):
        return jnp.einsum("jgph,gk->jghkp", t.reshape(nb, SSM_PACK, p_, h), eye).reshape(nb, SSM_PACK * h, SSM_PACK * p_)

    def out_blocks(t):
        return jnp.einsum("jghp,gk->jgpkh", t.reshape(nb, SSM_PACK, h, p_), eye).reshape(nb, SSM_PACK * p_, SSM_PACK * h)

    nm = "ssm%d" % l
    ur = bd_matmul(u, in_blocks(bbr), nm + "_bur")
    ui = bd_matmul(u, in_blocks(bbi), nm + "_bui")
    xr, xi = s5_scan(ur, ui, lbr.reshape(1, g * p_), lbi.reshape(1, g * p_), nm + "_scan")
    y = (bd_matmul(xr, out_blocks(sp["ssm_c_re", l]), nm + "_cr") + bd_matmul(xi, out_blocks(-sp["ssm_c_im", l]), nm + "_ci")
         + sp["ssm_d", l].reshape(1, g * h) * u)
    y = jax.nn.gelu(y)
    zz = matmul_w(y, w16["ssm_w_glu", l], toks["ssm_w_glu", l], nm + "_glu")
    z_out, z_gate = split_cols(zz, _contiguous((D_MODEL, D_MODEL)))
    return z_out * jax.nn.sigmoid(z_gate)


def _mla_mixer(q_lat, kv_lat, k_rope, cos, sin, sp, l, w16, toks):
    s = q_lat.shape[0]
    nm = "mla%d" % l
    dqk = MLA_NOPE + MLA_ROPE
    qt = matmul_w_t(rmsnorm(q_lat, sp["mla_q_norm", l], nm + "_qn"), w16["mla_w_uq", l], toks["mla_w_uq", l],
                    nm + "_uq").reshape(MLA_HEADS, dqk, s)
    half = MLA_ROPE // 2
    q1, q2 = qt[:, MLA_NOPE:MLA_NOPE + half], qt[:, MLA_NOPE + half:]
    ct, st = cos.T[None], sin.T[None]
    qt = jnp.concatenate([qt[:, :MLA_NOPE], q1 * ct - q2 * st, q2 * ct + q1 * st], axis=1).reshape(MLA_HEADS * dqk, s)
    kvt = matmul_w_t(rmsnorm(kv_lat, sp["mla_kv_norm", l], nm + "_kvn"), w16["mla_w_ukv", l], toks["mla_w_ukv", l],
                     nm + "_ukv").reshape(MLA_HEADS, MLA_NOPE + MLA_V, s)
    k_pe = _rope(k_rope, cos, sin)
    k = jnp.concatenate([kvt[:, :MLA_NOPE].transpose(0, 2, 1),
                         jnp.broadcast_to(k_pe[None], (MLA_HEADS, s, MLA_ROPE))], axis=-1)
    vt = kvt[:, MLA_NOPE:].reshape(MLA_HEADS * MLA_V, s)
    ot = attention(qt, k, vt, 1.0 / math.sqrt(dqk), True, nm + "_att")
    return matmul_w_at(ot, w16["mla_w_o", l], toks["mla_w_o", l], nm + "_o")


def _hgrn_mixer(q, f_logit, i_in, g, lb, sp, l, w16, toks):
    s = q.shape[0]
    nm = "hg%d" % l
    f = lb + (1.0 - lb) * jax.nn.sigmoid(f_logit)
    o = hgrn_core(jax.nn.silu(q), 1.0 - f, i_in, jnp.log(f), nm + "_core")
    o = rmsnorm(o.reshape(s * HG_HEADS, HG_DV), sp["hg_g_norm", l], nm + "_gn").reshape(s, HG_HEADS * HG_DV)
    return silu_gated_out(o, g, w16["hg_w_o", l], toks["hg_w_o", l], nm + "_o")


def _cross_attention(x, h, mem_n, l, w16, toks):
    s, m = h.shape[0], mem_n.shape[0]
    nm = "xat%d" % l
    xw = X_HEADS * X_HEAD_DIM
    qt = matmul_w_t(h, w16["x_w_q", l], toks["x_w_q", l], nm + "_q")
    kv = matmul_w(mem_n, w16["x_w_kv", l], toks["x_w_kv", l], nm + "_kv")
    k, v = split_cols(kv, _contiguous((xw, xw)))
    k = k.reshape(m, X_HEADS, X_HEAD_DIM).transpose(1, 0, 2)
    ot = attention(qt, k, v.T, 1.0 / math.sqrt(X_HEAD_DIM), False, nm + "_att")
    return matmul_w_at_res(x, ot, w16["x_w_o", l], toks["x_w_o", l], nm + "_o")


def _in_spec():
    hw = HG_HEADS * HG_DK
    sizes = (SSM_WIDTH, MLA_Q_RANK, MLA_KV_RANK, MLA_ROPE, hw, hw, hw, hw, D_MODEL, D_MODEL, D_MODEL)
    pieces, lo = [], 0
    for n in sizes:
        segs, c = [], lo
        while c < lo + n:
            end = min(lo + n, (c // D_IN_SHARD + 1) * D_IN_SHARD)
            segs.append((c + (D_IN_BLOCK - D_IN_SHARD) * (c // D_IN_SHARD), end - c))
            c = end
        pieces.append(tuple(segs))
        lo += n
    assert lo == D_IN
    return tuple(pieces), N_CHIPS * D_IN_BLOCK


def _layer(x, mem, cos, sin, lb, l, sp, w16, toks):
    s = x.shape[0]
    nm = "l%d" % l
    h = rmsnorm(x, sp["norm_mix", l], nm + "_nmix")
    z = matmul_w(h, w16["w_in", l], toks["w_in", l], nm + "_in")
    u, q_lat, kv_lat, k_rope, hg_q, hg_f, hg_i, hg_g, l_ssm, l_mla, l_hg = split_cols(z, _in_spec())
    y_ssm = _s5_mixer(u, sp, l, w16, toks)
    y_mla = _mla_mixer(q_lat, kv_lat, k_rope, cos, sin, sp, l, w16, toks)
    y_hg = _hgrn_mixer(hg_q, hg_f, hg_i, hg_g, lb, sp, l, w16, toks)
    x = gated_merge_out(x, (y_ssm, y_mla, y_hg), (l_ssm, l_mla, l_hg), w16["w_out", l], toks["w_out", l], nm + "_out")
    hc = rmsnorm(x, sp["norm_cross", l], nm + "_ncross")
    mem_n = rmsnorm(mem, sp["norm_mem", l], nm + "_nmem")
    x = _cross_attention(x, hc, mem_n, l, w16, toks)
    hf = rmsnorm(x, sp["norm_ffn", l], nm + "_nffn")
    gt = matmul_w(hf, w16["ffn_w_gate", l], toks["ffn_w_gate", l], nm + "_gate")
    up = matmul_w(hf, w16["ffn_w_up", l], toks["ffn_w_up", l], nm + "_up")
    return swiglu_down(x, gt, up, w16["ffn_w_down", l], toks["ffn_w_down", l], nm + "_down")


def _lower_bounds(hg_lb):
    lb_p = jax.nn.softmax(hg_lb, axis=0)
    return jnp.cumsum(lb_p, axis=0) - lb_p[0:1]


def _loss_head(norm_final, x, target):
    y = rmsnorm(x, norm_final, "nfinal")
    return 0.5 * jnp.sum(jnp.mean(jnp.square(y - target), axis=-1))


def _forward_backward(sp, toks, x, w16, later, between, mem, pos, target):
    half = MLA_ROPE // 2
    inv_freq = ROPE_THETA ** (-jnp.arange(half, dtype=F32) / half)
    ang = pos.astype(F32)[:, None] * inv_freq
    cos, sin = jnp.cos(ang), jnp.sin(ang)
    lower, lower_vjp = jax.vjp(_lower_bounds, sp["hg_lb", ALL])
    token, wait_for_weights = later
    x = x + token[0, 0]
    vjps = []
    for l in range(DEPTH):
        if l == 1:
            w16 = wait_for_weights(w16, x)
        sp_l = {k: v for k, v in sp.items() if k[1] == l}
        toks_l = {k: v for k, v in toks.items() if k[1] == l}
        weights = w16
        x, layer_vjp = jax.vjp(lambda s, t, lb, xx, l=l, weights=weights: _layer(xx, mem, cos, sin, lb, l, s, weights, t),
                               sp_l, toks_l, lower[l], x)
        vjps.append(layer_vjp)
    loss, head_vjp = jax.vjp(lambda g, xx: _loss_head(g, xx, target), sp["norm_final", ALL], x)
    g_sp, g_tok, d_lower = {}, {}, [None] * DEPTH
    g_sp["norm_final", ALL], dx = head_vjp(jnp.ones((), F32))
    for l in reversed(range(DEPTH)):
        g_s, g_t, d_lower[l], dx = vjps[l](dx)
        g_sp.update(g_s)
        g_tok.update(g_t)
        if l > 0:
            dx = between(l, g_t, dx)
    (g_sp["hg_lb", ALL],) = lower_vjp(jnp.stack(d_lower))
    return loss, g_sp, g_tok, dx


WEIGHTS = ["norm_mix", "w_in", "ssm_lam_re", "ssm_lam_im", "ssm_b_re", "ssm_b_im", "ssm_c_re", "ssm_c_im", "ssm_d",
           "ssm_log_step", "ssm_w_glu", "mla_q_norm", "mla_kv_norm", "mla_w_uq", "mla_w_ukv", "mla_w_o", "hg_lb",
           "hg_g_norm", "hg_w_o", "w_out", "norm_cross", "norm_mem", "x_w_q", "x_w_kv", "x_w_o", "norm_ffn",
           "ffn_w_gate_up", "ffn_w_down", "norm_final"]
INPUTS = ["x", "mem", "positions"] + WEIGHTS + ["loss_target"] + ["m_" + n for n in WEIGHTS] + ["v_" + n for n in WEIGHTS]
BIG = [("w_in", 2), ("ssm_w_glu", 2), ("mla_w_uq", 2), ("mla_w_ukv", 2), ("mla_w_o", 2), ("hg_w_o", 2), ("w_out", 1),
       ("x_w_q", 1), ("x_w_kv", 1), ("x_w_o", 2), ("ffn_w_gate_up", 2), ("ffn_w_down", 1)]
SMALL = [n for n in WEIGHTS if n not in dict(BIG)]
WHOLE = ("hg_lb", "norm_final")
ALL = -1
SMALL_W = 128
BLOCKED = ("w_in", "ffn_w_gate_up")


def _pack(arrs, rows, width, dtype):
    flat = jnp.concatenate([a.astype(dtype).reshape(-1) for a in arrs])
    return jnp.pad(flat, (0, rows * width - flat.shape[0])).reshape(rows, width)


def _unpack(buf, like):
    flat = buf.reshape(-1)
    out, off = [], 0
    for a in like:
        n = math.prod(a.shape)
        out.append(flat[off:off + n].reshape(a.shape))
        off += n
    return out


def kernel(
    x, mem, positions, norm_mix, w_in, ssm_lam_re, ssm_lam_im, ssm_b_re, ssm_b_im, ssm_c_re, ssm_c_im, ssm_d,
    ssm_log_step, ssm_w_glu, mla_q_norm, mla_kv_norm, mla_w_uq, mla_w_ukv, mla_w_o, hg_lb, hg_g_norm, hg_w_o,
    w_out, norm_cross, norm_mem, x_w_q, x_w_kv, x_w_o, norm_ffn, ffn_w_gate_up, ffn_w_down, norm_final,
    loss_target, m_norm_mix, m_w_in, m_ssm_lam_re, m_ssm_lam_im, m_ssm_b_re, m_ssm_b_im, m_ssm_c_re, m_ssm_c_im,
    m_ssm_d, m_ssm_log_step, m_ssm_w_glu, m_mla_q_norm, m_mla_kv_norm, m_mla_w_uq, m_mla_w_ukv, m_mla_w_o,
    m_hg_lb, m_hg_g_norm, m_hg_w_o, m_w_out, m_norm_cross, m_norm_mem, m_x_w_q, m_x_w_kv, m_x_w_o, m_norm_ffn,
    m_ffn_w_gate_up, m_ffn_w_down, m_norm_final, v_norm_mix, v_w_in, v_ssm_lam_re, v_ssm_lam_im, v_ssm_b_re,
    v_ssm_b_im, v_ssm_c_re, v_ssm_c_im, v_ssm_d, v_ssm_log_step, v_ssm_w_glu, v_mla_q_norm, v_mla_kv_norm,
    v_mla_w_uq, v_mla_w_ukv, v_mla_w_o, v_hg_lb, v_hg_g_norm, v_hg_w_o, v_w_out, v_norm_cross, v_norm_mem,
    v_x_w_q, v_x_w_kv, v_x_w_o, v_norm_ffn, v_ffn_w_gate_up, v_ffn_w_down, v_norm_final):
    given = locals()
    a = {n: given[n] for n in INPUTS}
    x, mem, pos, target = a["x"][0], a["mem"][0], a["positions"][0], a["loss_target"][0]
    place = jnp.stack([lax.axis_index("c"), 2 * lax.axis_index("x") + lax.axis_index("y")]).astype(jnp.int32)

    def shard16(n):
        w = a[n].astype(BF16)
        if n == "w_in":
            w = jnp.pad(w, ((0, 0), (0, 0), (0, D_IN_BLOCK - D_IN_SHARD)))
        return w

    def use(blocks, l, w16):
        for (n, ax), got in zip(BIG, blocks):
            if n in BLOCKED:
                full = got
            elif ax == 2:
                full = jnp.concatenate([got[q] for q in range(N_CHIPS)], axis=1)
            else:
                full = got.reshape(N_CHIPS * got.shape[1], got.shape[2])
            if n == "ffn_w_gate_up":
                half = N_CHIPS // 2
                w16["ffn_w_gate", l], w16["ffn_w_up", l] = full[:half], full[half:]
            else:
                w16[n, l] = full
        return w16

    shards = [shard16(n) for n, _ in BIG]
    halves = [s[0].reshape(2, s.shape[1] // 2, s.shape[2]) for s in shards]
    w16 = use([g.reshape(N_CHIPS, s.shape[1], s.shape[2]) for g, s in zip(_gather_weights(halves), shards)], 0, {})
    pending = _gather_start([s[1] for s in shards], after=w16["w_in", 0])
    toks = {}
    for n, ax in BIG:
        rows, cols = a[n].shape[1], shards[BIG.index((n, ax))].shape[2]
        shape = (N_CHIPS, rows, cols) if n in BLOCKED else (rows, N_CHIPS * cols) if ax == 2 else (N_CHIPS * rows, cols)
        for l in range(DEPTH):
            if n == "ffn_w_gate_up":
                toks["ffn_w_gate", l] = jnp.zeros((N_CHIPS // 2,) + shape[1:], F32)
                toks["ffn_w_up", l] = jnp.zeros((N_CHIPS // 2,) + shape[1:], F32)
            else:
                toks[n, l] = jnp.zeros(shape, F32)

    def layer1_weights(w16, after):
        return use(_gather_wait(*pending[:4], after), 1, dict(w16))
    sp = {}
    for n in SMALL:
        if n in WHOLE:
            sp[n, ALL] = a[n]
        else:
            for l in range(DEPTH):
                sp[n, l] = a[n][l]

    def chip_sums_of(g_tok, l):
        grads = []
        for n, ax in BIG:
            if n == "ffn_w_gate_up":
                g = jnp.concatenate([g_tok["ffn_w_gate", l], g_tok["ffn_w_up", l]], axis=0)
            else:
                g = g_tok[n, l]
            if n in BLOCKED:
                grads.append(g)
            elif ax == 2:
                grads.append(g.reshape(g.shape[0], N_CHIPS, g.shape[1] // N_CHIPS).transpose(1, 0, 2))
            else:
                grads.append(g.reshape(N_CHIPS, g.shape[0] // N_CHIPS, g.shape[1]))
        from_sibling = _swap_halves_to_sibling(grads, tag=str(l))
        sums = [_add_half(g, t, place, "grad_add_half_%s%d" % (n, l)) for (n, _), g, t in zip(BIG, grads, from_sibling)]
        return grads, from_sibling, sums

    def totals_of(l, grads, from_sibling, from_chips):
        return [_add_partials(g, t1, t2, place, "grad_add_partials_%s%d" % (n, l))
                for (n, _), g, t1, t2 in zip(BIG, grads, from_sibling, from_chips)]

    started = {}

    def between(l, g_tok_l, dx):
        grads, from_sibling, sums = chip_sums_of(g_tok_l, l)
        flight = _scatter_start(sums, after=dx)
        started[l] = (grads, from_sibling, flight)
        return dx + flight[4][0, 0]

    loss, g_sp, g_tok, g_x = _forward_backward(sp, toks, x, w16, (pending[4], layer1_weights), between, mem, pos, target)
    g_small = {n: g_sp[n, ALL] if n in WHOLE else jnp.stack([g_sp[n, l] for l in range(DEPTH)]) for n in SMALL}
    totals = {}
    for l, (grads, from_sibling, flight) in started.items():
        totals[l] = totals_of(l, grads, from_sibling, _scatter_wait(*flight[:4], after=g_x))
    grads, from_sibling, sums = chip_sums_of(g_tok, 0)
    out_g, out_d, out_m, out_v = {}, {}, {}, {}
    small_like = [a[n] for n in SMALL] + [jnp.zeros((1,), F32)]
    n_small = sum(math.prod(t.shape) for t in small_like)
    srows = -(-n_small // (SMALL_W * 8 * N_DEV)) * 8 * N_DEV
    reduced = _allreduce_small(_pack([g_small[n] for n in SMALL] + [loss.reshape(1)], srows, SMALL_W, F32))
    flight = _scatter_start(sums, after=reduced, tag="0")
    reduced = reduced + flight[4][0, 0]
    red = _unpack(reduced, small_like)
    loss = red[-1].reshape(())
    small_arrs = [a[n] for n in SMALL]
    d, m, v = _adamw_call(
        _pack(small_arrs, srows, SMALL_W, F32), reduced,
        _pack([a["m_" + n] for n in SMALL], srows, SMALL_W, F32),
        _pack([a["v_" + n] for n in SMALL], srows, SMALL_W, F32), "adamw_small")
    for n, gg, dd, mm_, vv in zip(SMALL, red[:-1], _unpack(d, small_arrs), _unpack(m, small_arrs), _unpack(v, small_arrs)):
        out_g[n], out_d[n], out_m[n], out_v[n] = gg, dd, mm_, vv

    totals[0] = totals_of(0, grads, from_sibling, _scatter_wait(*flight[:4], after=out_v[SMALL[-1]], tag="0"))
    joined = _join_halves([t for l in range(DEPTH) for t in totals[l]])
    nbig = len(BIG)
    g_big = {n: jnp.stack([joined[nbig * l + i][:, :a[n].shape[2]] for l in range(DEPTH)])
             for i, (n, _) in enumerate(BIG)}
    for n, _ in BIG:
        shp = a[n].shape
        two = lambda t: t.reshape(shp[0] * shp[1], shp[2])
        d, m, v = _adamw_call(two(a[n]), two(g_big[n]), two(a["m_" + n]), two(a["v_" + n]), "adamw_" + n)
        out_g[n], out_d[n], out_m[n], out_v[n] = g_big[n], d.reshape(shp), m.reshape(shp), v.reshape(shp)

    return (loss, g_x[None], *[out_g[n] for n in WEIGHTS], *[out_d[n] for n in WEIGHTS],
            *[out_m[n] for n in WEIGHTS], *[out_v[n] for n in WEIGHTS])
```

```python
import functools
import math

import jax
import jax.numpy as jnp
from jax import lax
from jax.experimental import pallas as pl
from jax.experimental.pallas import tpu as pltpu

F32 = jnp.float32
BF16 = jnp.bfloat16
MESH = pl.DeviceIdType.MESH

D_MODEL = 1024
DEPTH = 2
RMS_EPS = 1e-6
SSM_GROUPS, SSM_GROUP_CH, SSM_STATE = 32, 16, 64
SSM_WIDTH = SSM_GROUPS * SSM_GROUP_CH
SSM_LANES = SSM_GROUPS * SSM_STATE
SSM_PACK = 8
MLA_HEADS, MLA_Q_RANK, MLA_KV_RANK, MLA_NOPE, MLA_ROPE, MLA_V = 8, 512, 256, 64, 32, 64
ROPE_THETA = 10000.0
HG_HEADS, HG_DK, HG_DV, HG_CHUNK = 4, 128, 128, 64
HG_SUB_FWD, HG_SUB_BWD = 32, 16
X_HEADS, X_HEAD_DIM = 4, 128
D_FF = 2816
D_IN = 6432
N_CHIPS = 4
D_IN_SHARD = D_IN // N_CHIPS
D_IN_BLOCK = 1664
MASK_VALUE = -1e30

ADAM_LR, ADAM_B1, ADAM_B2, ADAM_EPS, ADAM_WD, ADAM_STEP = 0.001, 0.9, 0.999, 1e-08, 0.01, 10

VMEM_LIMIT = 48 * 1024 * 1024


def _pcall(body, **kw):
    return pl.pallas_call(body, **kw)


def _params(sem):
    return pltpu.CompilerParams(dimension_semantics=sem, vmem_limit_bytes=VMEM_LIMIT)


def _tile(n, cap, align=128):
    if n <= cap:
        return n
    t = (cap // align) * align
    while t >= align:
        if n % t == 0:
            return t
        t -= align
    return n


MM_VMEM_BUDGET = 36 * 1024 * 1024
MM_TILE_CAP = 2048


def _divisors(n, cap, align=128):
    out = [n] if n <= cap else []
    out += [t for t in range(align, min(n, cap + 1), align) if n % t == 0]
    return sorted(set(out), reverse=True)


@functools.lru_cache(maxsize=None)
def _mm_tiles(m, n, k, a_bytes, b_bytes, o_bytes, n_unit, k_unit):
    best = None
    for tm in _divisors(m, MM_TILE_CAP):
        for tn in _divisors(n_unit, MM_TILE_CAP):
            for tk in _divisors(k_unit, MM_TILE_CAP):
                nk = k // tk
                need = 2 * (tm * tk * a_bytes + tk * tn * b_bytes + tm * tn * o_bytes)
                need += tm * tn * 4 * (2 if nk > 1 else 1)
                need += (tm * tk * 2 if a_bytes > 2 else 0) + (tk * tn * 2 if b_bytes > 2 else 0)
                if need > MM_VMEM_BUDGET:
                    continue
                key = ((m // tm) * (n // tn) * nk, nk, -tn)
                if best is None or key < best[0]:
                    best = (key, (tm, tn, tk))
    assert best is not None, (m, n, k)
    return best[1]


def _mm(a, b, *, ta=False, tb=False, out_dtype=F32, name="mm", b_blocks=False, out_blocks=0, pro=None, epi=None,
        save_left=False):
    if ta:
        K, M = a.shape
    else:
        M, K = a.shape
    n_unit = k_unit = None
    if b_blocks:
        nb, rows, cols = b.shape
        if tb:
            N, Kb, k_unit = rows, nb * cols, cols
        else:
            Kb, N, n_unit = rows, nb * cols, cols
    elif tb:
        N, Kb = b.shape
    else:
        Kb, N = b.shape
    if out_blocks:
        assert not b_blocks and N % out_blocks == 0
        n_unit = N // out_blocks
    assert K == Kb, (a.shape, b.shape, ta, tb)
    n_pro = len(pro) - 1 if pro else 0
    n_epi = len(epi) - 1 if epi else 0
    tm, tn, tk = _mm_tiles(M, N, K, a.dtype.itemsize * (1 + n_pro), b.dtype.itemsize,
                           jnp.dtype(out_dtype).itemsize * max(1, 2 * n_epi), n_unit or N, k_unit or K)
    nk = K // tk
    dims = (((0,) if ta else (1,), (1,) if tb else (0,)), ((), ()))
    n_in = 2 + n_pro + n_epi
    n_out = max(1, n_epi) + (1 if save_left else 0)
    assert not save_left or (pro and tn == N)

    def body(*refs):
        ins, outs, acc = refs[:n_in], refs[n_in:n_in + n_out], refs[n_in + n_out:]
        b_ref = ins[1 + n_pro]
        av = (pro[0](*[r[...] for r in ins[:1 + n_pro]]) if pro else ins[0][...]).astype(BF16)
        if save_left:
            outs[-1][...] = av
        bv = b_ref[...].astype(BF16)
        part = lax.dot_general(av, bv, dims, preferred_element_type=F32)

        def finish(total):
            if epi:
                for o_ref, r in zip(outs, epi[0](total, *[e[...] for e in ins[2 + n_pro:]])):
                    o_ref[...] = r.astype(o_ref.dtype)
            else:
                outs[0][...] = total.astype(outs[0].dtype)

        if nk == 1:
            finish(part)
        else:
            acc_ref = acc[0]
            k = pl.program_id(2)

            @pl.when(k == 0)
            def _():
                acc_ref[...] = part

            @pl.when(k > 0)
            def _():
                acc_ref[...] += part

            @pl.when(k == nk - 1)
            def _():
                finish(acc_ref[...])

    a_spec = pl.BlockSpec((tk, tm), lambda i, j, k: (k, i)) if ta else pl.BlockSpec((tm, tk), lambda i, j, k: (i, k))
    if b_blocks and tb:
        r = k_unit // tk
        b_spec = pl.BlockSpec((None, tn, tk), lambda i, j, k: (k // r, j, k % r))
    elif b_blocks:
        r = n_unit // tn
        b_spec = pl.BlockSpec((None, tk, tn), lambda i, j, k: (j // r, k, j % r))
    elif tb:
        b_spec = pl.BlockSpec((tn, tk), lambda i, j, k: (j, k))
    else:
        b_spec = pl.BlockSpec((tk, tn), lambda i, j, k: (k, j))
    if out_blocks:
        r = n_unit // tn
        out_spec = pl.BlockSpec((None, tm, tn), lambda i, j, k: (j // r, i, j % r))
        out_shape = jax.ShapeDtypeStruct((out_blocks, M, n_unit), out_dtype)
    else:
        out_spec = pl.BlockSpec((tm, tn), lambda i, j, k: (i, j))
        out_shape = jax.ShapeDtypeStruct((M, N), out_dtype)
    in_specs = [a_spec] * (1 + n_pro) + [b_spec] + [out_spec] * n_epi
    args = [a] + (list(pro[1:]) if pro else []) + [b] + (list(epi[1:]) if epi else [])
    out_specs, out_shapes = ([out_spec] * n_epi, [out_shape] * n_epi) if epi else (out_spec, out_shape)
    if save_left:
        out_specs = [out_spec] * max(1, n_epi) + [a_spec]
        out_shapes = [out_shape] * max(1, n_epi) + [jax.ShapeDtypeStruct(a.shape, BF16)]
    return _pcall(
        body,
        name=name,
        grid=(M // tm, N // tn, nk),
        in_specs=in_specs,
        out_specs=out_specs,
        out_shape=out_shapes,
        scratch_shapes=[pltpu.VMEM((tm, tn), F32)] if nk > 1 else [],
        compiler_params=_params(("parallel", "parallel", "arbitrary")),
    )(*args)


def _bd_call(a, bc, transpose, name):
    m = a.shape[0]
    nb, ka, kb = bc.shape
    win, wout = (kb, ka) if transpose else (ka, kb)
    assert a.shape[1] == nb * win
    tm = _tile(m, 1024)
    dims = _NT if transpose else (((1,), (0,)), ((), ()))

    def body(a_ref, b_ref, o_ref):
        o_ref[...] = lax.dot_general(a_ref[...].astype(BF16), b_ref[0].astype(BF16), dims, preferred_element_type=F32)

    return _pcall(
        body,
        name=name,
        grid=(m // tm, nb),
        in_specs=[pl.BlockSpec((tm, win), lambda i, j: (i, j)), pl.BlockSpec((1, ka, kb), lambda i, j: (j, 0, 0))],
        out_specs=pl.BlockSpec((tm, wout), lambda i, j: (i, j)),
        out_shape=jax.ShapeDtypeStruct((m, nb * wout), F32),
        compiler_params=_params(("parallel", "parallel")),
    )(a, bc)


def _bd_dw_call(a, g, nb, name):
    m = a.shape[0]
    ka, kb = a.shape[1] // nb, g.shape[1] // nb
    tk = _tile(m, 1024)
    nk = m // tk

    def body(a_ref, g_ref, o_ref):
        part = lax.dot_general(a_ref[...].astype(BF16), g_ref[...].astype(BF16), _TN, preferred_element_type=F32)
        k = pl.program_id(1)

        @pl.when(k == 0)
        def _():
            o_ref[0] = part

        @pl.when(k > 0)
        def _():
            o_ref[0] += part

    return _pcall(
        body,
        name=name,
        grid=(nb, nk),
        in_specs=[pl.BlockSpec((tk, ka), lambda j, k: (k, j)), pl.BlockSpec((tk, kb), lambda j, k: (k, j))],
        out_specs=pl.BlockSpec((1, ka, kb), lambda j, k: (j, 0, 0)),
        out_shape=jax.ShapeDtypeStruct((nb, ka, kb), F32),
        compiler_params=_params(("parallel", "arbitrary")),
    )(a, g)


@functools.partial(jax.custom_vjp, nondiff_argnums=(2,))
def bd_matmul(a, bc, name):
    return _bd_call(a, bc, False, name + "_f")


def _bd_matmul_fwd(a, bc, name):
    return _bd_call(a, bc, False, name + "_f"), (a, bc)


def _bd_matmul_bwd(name, res, g):
    a, bc = res
    return _bd_call(g, bc, True, name + "_da"), _bd_dw_call(a, g, bc.shape[0], name + "_dw")


bd_matmul.defvjp(_bd_matmul_fwd, _bd_matmul_bwd)


@functools.partial(jax.custom_vjp, nondiff_argnums=(3,))
def matmul_w(a, w16, tok, name):
    return _mm(a, w16, b_blocks=w16.ndim == 3, name=name + "_f")


def _matmul_w_fwd(a, w16, tok, name):
    return _mm(a, w16, b_blocks=w16.ndim == 3, name=name + "_f"), (a, w16)


def _matmul_w_bwd(name, res, g):
    a, w16 = res
    blocks = w16.shape[0] if w16.ndim == 3 else 0
    return (_mm(g, w16, tb=True, b_blocks=bool(blocks), name=name + "_da"), jnp.zeros_like(w16),
            _mm(a, g, ta=True, out_blocks=blocks, name=name + "_dw"))


matmul_w.defvjp(_matmul_w_fwd, _matmul_w_bwd)


def _swiglu(gt, up):
    return gt * jax.nn.sigmoid(gt) * up


def _swiglu_bwd(dact, gt, up):
    sg = jax.nn.sigmoid(gt)
    return dact * up * sg * (1.0 + gt * (1.0 - sg)), dact * gt * sg


def _add_residual(product, x):
    return (product + x,)


def _silu_gate(o, g):
    return o * (g * jax.nn.sigmoid(g))


def _silu_gate_bwd(dm, o, g):
    sg = jax.nn.sigmoid(g)
    return dm * g * sg, dm * o * sg * (1.0 + g * (1.0 - sg))


@functools.partial(jax.custom_vjp, nondiff_argnums=(4,))
def silu_gated_out(o, g, w16, tok, name):
    return _mm(o, w16, pro=(_silu_gate, g), name=name + "_f")


def _silu_gated_out_fwd(o, g, w16, tok, name):
    out, left16 = _mm(o, w16, pro=(_silu_gate, g), save_left=True, name=name + "_f")
    return out, (o, g, w16, left16)


def _silu_gated_out_bwd(name, res, d):
    o, g, w16, left16 = res
    d_o, d_g = _mm(d, w16, tb=True, epi=(_silu_gate_bwd, o, g), name=name + "_da")
    return d_o, d_g, jnp.zeros_like(w16), _mm(left16, d, ta=True, name=name + "_dw")


silu_gated_out.defvjp(_silu_gated_out_fwd, _silu_gated_out_bwd)


@functools.partial(jax.custom_vjp, nondiff_argnums=(5,))
def swiglu_down(x, gt, up, w16, tok, name):
    return _mm(gt, w16, pro=(_swiglu, up), epi=(_add_residual, x), name=name + "_f")[0]


def _swiglu_down_fwd(x, gt, up, w16, tok, name):
    out, act16 = _mm(gt, w16, pro=(_swiglu, up), epi=(_add_residual, x), save_left=True, name=name + "_f")
    return out, (gt, up, w16, act16)


def _swiglu_down_bwd(name, res, g):
    gt, up, w16, act16 = res
    d_gt, d_up = _mm(g, w16, tb=True, epi=(_swiglu_bwd, gt, up), name=name + "_da")
    return g, d_gt, d_up, jnp.zeros_like(w16), _mm(act16, g, ta=True, name=name + "_dw")


swiglu_down.defvjp(_swiglu_down_fwd, _swiglu_down_bwd)


def _merge(y1, y2, y3, l1, l2, l3):
    return jax.nn.sigmoid(l1) * y1 + jax.nn.sigmoid(l2) * y2 + jax.nn.sigmoid(l3) * y3


def _merge_bwd(dm, y1, y2, y3, l1, l2, l3):
    dys, dls = [], []
    for y, logit in ((y1, l1), (y2, l2), (y3, l3)):
        s = jax.nn.sigmoid(logit)
        dys.append(dm * s)
        dls.append(dm * y * s * (1.0 - s))
    return (*dys, *dls)


@functools.partial(jax.custom_vjp, nondiff_argnums=(5,))
def gated_merge_out(x, ys, logits, w16, tok, name):
    return _mm(ys[0], w16, pro=(_merge, *ys[1:], *logits), epi=(_add_residual, x), name=name + "_f")[0]


def _gated_merge_out_fwd(x, ys, logits, w16, tok, name):
    out, merged16 = _mm(ys[0], w16, pro=(_merge, *ys[1:], *logits), epi=(_add_residual, x), save_left=True,
                        name=name + "_f")
    return out, (ys, logits, w16, merged16)


def _gated_merge_out_bwd(name, res, g):
    ys, logits, w16, merged16 = res
    d = _mm(g, w16, tb=True, epi=(_merge_bwd, *ys, *logits), name=name + "_da")
    return g, tuple(d[:3]), tuple(d[3:]), jnp.zeros_like(w16), _mm(merged16, g, ta=True, name=name + "_dw")


gated_merge_out.defvjp(_gated_merge_out_fwd, _gated_merge_out_bwd)


@functools.partial(jax.custom_vjp, nondiff_argnums=(3,))
def matmul_w_t(a, w16, tok, name):
    return _mm(w16, a, ta=True, tb=True, name=name + "_f")


def _matmul_w_t_fwd(a, w16, tok, name):
    return _mm(w16, a, ta=True, tb=True, name=name + "_f"), (a, w16)


def _matmul_w_t_bwd(name, res, g):
    a, w16 = res
    return (_mm(g, w16, ta=True, tb=True, name=name + "_da"), jnp.zeros_like(w16),
            _mm(a, g, ta=True, tb=True, name=name + "_dw"))


matmul_w_t.defvjp(_matmul_w_t_fwd, _matmul_w_t_bwd)


@functools.partial(jax.custom_vjp, nondiff_argnums=(3,))
def matmul_w_at(at, w16, tok, name):
    return _mm(at, w16, ta=True, name=name + "_f")


def _matmul_w_at_fwd(at, w16, tok, name):
    return _mm(at, w16, ta=True, name=name + "_f"), (at, w16)


def _matmul_w_at_bwd(name, res, g):
    at, w16 = res
    return _mm(w16, g, tb=True, name=name + "_da"), jnp.zeros_like(w16), _mm(at, g, name=name + "_dw")


matmul_w_at.defvjp(_matmul_w_at_fwd, _matmul_w_at_bwd)


@functools.partial(jax.custom_vjp, nondiff_argnums=(4,))
def matmul_w_at_res(x, at, w16, tok, name):
    return _mm(at, w16, ta=True, epi=(_add_residual, x), name=name + "_f")[0]


def _matmul_w_at_res_fwd(x, at, w16, tok, name):
    return _mm(at, w16, ta=True, epi=(_add_residual, x), name=name + "_f")[0], (at, w16)


def _matmul_w_at_res_bwd(name, res, g):
    return (g, *_matmul_w_at_bwd(name, res, g))


matmul_w_at_res.defvjp(_matmul_w_at_res_fwd, _matmul_w_at_res_bwd)


def _rms_fwd_call(x, g, name):
    rows, d = x.shape
    tr = _tile(rows, 512, 8)

    def body(x_ref, g_ref, y_ref):
        xv = x_ref[...]
        rstd = lax.rsqrt(jnp.mean(xv * xv, axis=-1, keepdims=True) + RMS_EPS)
        y_ref[...] = xv * rstd * g_ref[...]

    return _pcall(
        body,
        name=name,
        grid=(rows // tr,),
        in_specs=[pl.BlockSpec((tr, d), lambda i: (i, 0)), pl.BlockSpec((1, d), lambda i: (0, 0))],
        out_specs=pl.BlockSpec((tr, d), lambda i: (i, 0)),
        out_shape=jax.ShapeDtypeStruct((rows, d), F32),
        compiler_params=_params(("parallel",)),
    )(x, g.reshape(1, d))


def _rms_bwd_call(x, g, dy, name):
    rows, d = x.shape
    tr = _tile(rows, 512, 8)
    nb = rows // tr

    def body(x_ref, g_ref, dy_ref, dx_ref, dg_ref):
        xv = x_ref[...]
        rstd = lax.rsqrt(jnp.mean(xv * xv, axis=-1, keepdims=True) + RMS_EPS)
        xh = xv * rstd
        dyv = dy_ref[...]
        dyg = dyv * g_ref[...]
        dx_ref[...] = rstd * (dyg - xh * jnp.mean(dyg * xh, axis=-1, keepdims=True))
        dg_ref[...] = jnp.sum((dyv * xh).reshape(tr // 8, 8, d), axis=0)

    dx, dgp = _pcall(
        body,
        name=name,
        grid=(nb,),
        in_specs=[
            pl.BlockSpec((tr, d), lambda i: (i, 0)),
            pl.BlockSpec((1, d), lambda i: (0, 0)),
            pl.BlockSpec((tr, d), lambda i: (i, 0)),
        ],
        out_specs=[pl.BlockSpec((tr, d), lambda i: (i, 0)), pl.BlockSpec((8, d), lambda i: (i, 0))],
        out_shape=[jax.ShapeDtypeStruct((rows, d), F32), jax.ShapeDtypeStruct((nb * 8, d), F32)],
        compiler_params=_params(("parallel",)),
    )(x, g.reshape(1, d), dy)
    return dx, jnp.sum(dgp, axis=0)


@functools.partial(jax.custom_vjp, nondiff_argnums=(2,))
def rmsnorm(x, g, name):
    return _rms_fwd_call(x, g, name + "_f")


def _rmsnorm_fwd(x, g, name):
    return _rms_fwd_call(x, g, name + "_f"), (x, g)


def _rmsnorm_bwd(name, res, dy):
    x, g = res
    return _rms_bwd_call(x, g, dy, name + "_b")


rmsnorm.defvjp(_rmsnorm_fwd, _rmsnorm_bwd)


def _attn_blocks(sq, sk):
    return _tile(sq, 1024), _tile(sk, 512)


def _attn_pt(qs, k_ref, qi, ki, bq, bk, masked, shift):
    st = jnp.dot(k_ref[0].astype(BF16), qs, preferred_element_type=F32)
    if masked:
        kpos = ki * bk + lax.broadcasted_iota(jnp.int32, (bk, bq), 0)
        qpos = qi * bq + lax.broadcasted_iota(jnp.int32, (bk, bq), 1)
        st = jnp.where(kpos <= qpos, st, MASK_VALUE)
    return st if shift is None else jnp.exp(st - shift)


def _attn_pairs(nq, nk, bq, bk, causal, k_major):
    ok = lambda i, j: (not causal) or j * bk <= i * bq + bq - 1
    if k_major:
        pairs = [(i, j) for j in range(nk) for i in range(nq) if ok(i, j)]
    else:
        pairs = [(i, j) for i in range(nq) for j in range(nk) if ok(i, j)]
    qtab = jnp.asarray([p[0] for p in pairs], jnp.int32)
    ktab = jnp.asarray([p[1] for p in pairs], jnp.int32)
    first_q = (lambda j: (j * bk) // bq) if causal else (lambda j: 0)
    last_k = (lambda i: jnp.minimum((i * bq + bq - 1) // bk, nk - 1)) if causal else (lambda i: nk - 1)
    return qtab, ktab, first_q, last_k


def _attn_cases(causal, qi, ki, bq, bk, step):
    if not causal:
        step(False)
        return
    pl.when(ki * bk + bk - 1 <= qi * bq)(functools.partial(step, False))
    pl.when(ki * bk + bk - 1 > qi * bq)(functools.partial(step, True))


def _attn_fwd_call(qt, k, vt, scale, causal, name):
    h, sk, dq = k.shape
    sq = qt.shape[1]
    dv = vt.shape[0] // h
    bq, bk = _attn_blocks(sq, sk)
    nq, nk = sq // bq, sk // bk

    qtab, ktab, _, last_k = _attn_pairs(nq, nk, bq, bk, causal, False)

    def body(qi_ref, ki_ref, q_ref, k_ref, v_ref, o_ref, lse_ref, m_sc, l_sc, acc_sc):
        qi, ki = qi_ref[pl.program_id(1)], ki_ref[pl.program_id(1)]

        @pl.when(ki == 0)
        def _():
            m_sc[...] = jnp.full(m_sc.shape, -jnp.inf, F32)
            l_sc[...] = jnp.zeros(l_sc.shape, F32)
            acc_sc[...] = jnp.zeros(acc_sc.shape, F32)

        def step(masked):
            qs = (q_ref[...] * scale).astype(BF16)
            st = _attn_pt(qs, k_ref, qi, ki, bq, bk, masked, None)
            m_prev = m_sc[...]
            m_new = jnp.maximum(m_prev, jnp.max(st, axis=0, keepdims=True))
            alpha = jnp.exp(m_prev - m_new)
            pt = jnp.exp(st - m_new)
            l_sc[...] = alpha * l_sc[...] + jnp.sum(pt, axis=0, keepdims=True)
            acc_sc[...] = alpha * acc_sc[...] + jnp.dot(v_ref[...].astype(BF16), pt.astype(BF16),
                                                        preferred_element_type=F32)
            m_sc[...] = m_new

        _attn_cases(causal, qi, ki, bq, bk, step)

        @pl.when(ki == last_k(qi))
        def _():
            o_ref[...] = acc_sc[...] / l_sc[...]
            lse_ref[0] = m_sc[...] + jnp.log(l_sc[...])

    qmap = lambda hh, p, qi_ref, ki_ref: (hh, qi_ref[p])
    return _pcall(
        body,
        name=name,
        grid_spec=pltpu.PrefetchScalarGridSpec(
            num_scalar_prefetch=2,
            grid=(h, qtab.shape[0]),
            in_specs=[
                pl.BlockSpec((dq, bq), qmap),
                pl.BlockSpec((1, bk, dq), lambda hh, p, qi_ref, ki_ref: (hh, ki_ref[p], 0)),
                pl.BlockSpec((dv, bk), lambda hh, p, qi_ref, ki_ref: (hh, ki_ref[p])),
            ],
            out_specs=[pl.BlockSpec((dv, bq), qmap),
                       pl.BlockSpec((1, 1, bq), lambda hh, p, qi_ref, ki_ref: (hh, 0, qi_ref[p]))],
            scratch_shapes=[pltpu.VMEM((1, bq), F32), pltpu.VMEM((1, bq), F32), pltpu.VMEM((dv, bq), F32)],
        ),
        out_shape=[jax.ShapeDtypeStruct((h * dv, sq), F32), jax.ShapeDtypeStruct((h, 1, sq), F32)],
        compiler_params=_params(("parallel", "arbitrary")),
    )(qtab, ktab, qt, k, vt)


_TN = (((0,), (0,)), ((), ()))
_NT = (((1,), (1,)), ((), ()))


def _attn_dst(qs, k_ref, v_ref, do_ref, lse_ref, dl_ref, qi, ki, bq, bk, masked):
    pt = _attn_pt(qs, k_ref, qi, ki, bq, bk, masked, lse_ref[0])
    dpt = lax.dot_general(v_ref[...].astype(BF16), do_ref[...].astype(BF16), _TN, preferred_element_type=F32)
    return pt, pt * (dpt - dl_ref[0])


def _attn_dq_call(qt, k, vt, dot, lse, delta, scale, causal, name):
    h, sk, dq = k.shape
    sq = qt.shape[1]
    dv = vt.shape[0] // h
    bq, bk = _attn_blocks(sq, sk)
    nq, nk = sq // bq, sk // bk

    qtab, ktab, _, last_k = _attn_pairs(nq, nk, bq, bk, causal, False)

    def body(qi_ref, ki_ref, q_ref, k_ref, v_ref, do_ref, lse_ref, dl_ref, dq_ref, acc_sc):
        qi, ki = qi_ref[pl.program_id(1)], ki_ref[pl.program_id(1)]

        @pl.when(ki == 0)
        def _():
            acc_sc[...] = jnp.zeros(acc_sc.shape, F32)

        def step(masked):
            qs = (q_ref[...] * scale).astype(BF16)
            _, dst = _attn_dst(qs, k_ref, v_ref, do_ref, lse_ref, dl_ref, qi, ki, bq, bk, masked)
            acc_sc[...] += lax.dot_general(k_ref[0].astype(BF16), dst.astype(BF16), _TN, preferred_element_type=F32)

        _attn_cases(causal, qi, ki, bq, bk, step)

        @pl.when(ki == last_k(qi))
        def _():
            dq_ref[...] = acc_sc[...] * scale

    qmap = lambda hh, p, qi_ref, ki_ref: (hh, qi_ref[p])
    rowmap = lambda hh, p, qi_ref, ki_ref: (hh, 0, qi_ref[p])
    return _pcall(
        body,
        name=name,
        grid_spec=pltpu.PrefetchScalarGridSpec(
            num_scalar_prefetch=2,
            grid=(h, qtab.shape[0]),
            in_specs=[
                pl.BlockSpec((dq, bq), qmap),
                pl.BlockSpec((1, bk, dq), lambda hh, p, qi_ref, ki_ref: (hh, ki_ref[p], 0)),
                pl.BlockSpec((dv, bk), lambda hh, p, qi_ref, ki_ref: (hh, ki_ref[p])),
                pl.BlockSpec((dv, bq), qmap),
                pl.BlockSpec((1, 1, bq), rowmap),
                pl.BlockSpec((1, 1, bq), rowmap),
            ],
            out_specs=pl.BlockSpec((dq, bq), qmap),
            scratch_shapes=[pltpu.VMEM((dq, bq), F32)],
        ),
        out_shape=jax.ShapeDtypeStruct((h * dq, sq), F32),
        compiler_params=_params(("parallel", "arbitrary")),
    )(qtab, ktab, qt, k, vt, dot, lse, delta)


def _attn_dkv_call(qt, k, vt, dot, lse, delta, scale, causal, name):
    h, sk, dq = k.shape
    sq = qt.shape[1]
    dv = vt.shape[0] // h
    bq, bk = _attn_blocks(sq, sk)
    nq, nk = sq // bq, sk // bk

    qtab, ktab, first_q, _ = _attn_pairs(nq, nk, bq, bk, causal, True)

    def body(qi_ref, ki_ref, q_ref, k_ref, v_ref, do_ref, lse_ref, dl_ref, dk_ref, dv_ref, dk_sc, dv_sc):
        qi, ki = qi_ref[pl.program_id(1)], ki_ref[pl.program_id(1)]

        @pl.when(qi == first_q(ki))
        def _():
            dk_sc[...] = jnp.zeros(dk_sc.shape, F32)
            dv_sc[...] = jnp.zeros(dv_sc.shape, F32)

        def step(masked):
            qs = (q_ref[...] * scale).astype(BF16)
            pt, dst = _attn_dst(qs, k_ref, v_ref, do_ref, lse_ref, dl_ref, qi, ki, bq, bk, masked)
            dv_sc[...] += lax.dot_general(do_ref[...].astype(BF16), pt.astype(BF16), _NT, preferred_element_type=F32)
            dk_sc[...] += lax.dot_general(dst.astype(BF16), qs, _NT, preferred_element_type=F32)

        _attn_cases(causal, qi, ki, bq, bk, step)

        @pl.when(qi == nq - 1)
        def _():
            dk_ref[0] = dk_sc[...]
            dv_ref[...] = dv_sc[...]

    qmap = lambda hh, p, qi_ref, ki_ref: (hh, qi_ref[p])
    rowmap = lambda hh, p, qi_ref, ki_ref: (hh, 0, qi_ref[p])
    kmap = lambda hh, p, qi_ref, ki_ref: (hh, ki_ref[p], 0)
    vmap = lambda hh, p, qi_ref, ki_ref: (hh, ki_ref[p])
    return _pcall(
        body,
        name=name,
        grid_spec=pltpu.PrefetchScalarGridSpec(
            num_scalar_prefetch=2,
            grid=(h, qtab.shape[0]),
            in_specs=[
                pl.BlockSpec((dq, bq), qmap),
                pl.BlockSpec((1, bk, dq), kmap),
                pl.BlockSpec((dv, bk), vmap),
                pl.BlockSpec((dv, bq), qmap),
                pl.BlockSpec((1, 1, bq), rowmap),
                pl.BlockSpec((1, 1, bq), rowmap),
            ],
            out_specs=[pl.BlockSpec((1, bk, dq), kmap), pl.BlockSpec((dv, bk), vmap)],
            scratch_shapes=[pltpu.VMEM((bk, dq), F32), pltpu.VMEM((dv, bk), F32)],
        ),
        out_shape=[jax.ShapeDtypeStruct((h, sk, dq), F32), jax.ShapeDtypeStruct((h * dv, sk), F32)],
        compiler_params=_params(("parallel", "arbitrary")),
    )(qtab, ktab, qt, k, vt, dot, lse, delta)


@functools.partial(jax.custom_vjp, nondiff_argnums=(3, 4, 5))
def attention(qt, k, vt, scale, causal, name):
    return _attn_fwd_call(qt, k, vt, scale, causal, name + "_f")[0]


def _attention_fwd(qt, k, vt, scale, causal, name):
    ot, lse = _attn_fwd_call(qt, k, vt, scale, causal, name + "_f")
    return ot, (qt, k, vt, ot, lse)


def _attention_bwd(scale, causal, name, res, dot):
    qt, k, vt, ot, lse = res
    h = k.shape[0]
    delta = jnp.sum((dot * ot).reshape(h, ot.shape[0] // h, ot.shape[1]), axis=1, keepdims=True)
    dqt = _attn_dq_call(qt, k, vt, dot, lse, delta, scale, causal, name + "_dq")
    dk, dvt = _attn_dkv_call(qt, k, vt, dot, lse, delta, scale, causal, name + "_dkv")
    return dqt, dk, dvt


attention.defvjp(_attention_fwd, _attention_bwd)


SCAN_ROWS = 8
SCAN_LANES = 256
SCAN_TBLOCK = 512
SCAN_UNROLL = 4


def _scan_tables(lr, li):
    pr, pi = [lr], [li]
    for _ in range(SCAN_ROWS - 1):
        pr, pi = pr + [pr[-1] * lr - pi[-1] * li], pi + [pr[-1] * li + pi[-1] * lr]
    return jnp.concatenate(pr, axis=0), jnp.concatenate(pi, axis=0)


def _scan_call(ur, ui, lr, li, reverse, name, states=None):
    s, n = ur.shape
    tb = _tile(s, SCAN_TBLOCK, 8)
    nt, nl = s // tb, n // SCAN_LANES
    ntile = tb // SCAN_ROWS
    pr, pi = _scan_tables(lr, li)
    if reverse:
        pr, pi = pr[::-1], pi[::-1]
    shape = (SCAN_ROWS, SCAN_LANES)
    with_states = states is not None
    assert reverse or not with_states

    def body(*refs):
        if with_states:
            ur_ref, ui_ref, pr_ref, pi_ref, sr_ref, si_ref, xr_ref, xi_ref, dr_ref, di_ref, cr_sc, ci_sc, ar_sc, ai_sc = refs
        else:
            ur_ref, ui_ref, pr_ref, pi_ref, xr_ref, xi_ref, cr_sc, ci_sc, ar_sc, ai_sc = refs

        @pl.when(pl.program_id(1) == 0)
        def _():
            for sc in (cr_sc, ci_sc, ar_sc, ai_sc):
                sc[...] = jnp.zeros(shape, F32)

        prv, piv = pr_ref[...], pi_ref[...]
        rows = lax.broadcasted_iota(jnp.int32, shape, 0)

        def powers(k):
            r = (SCAN_ROWS - k) if reverse else (k - 1)
            return jnp.broadcast_to(prv[r:r + 1], shape), jnp.broadcast_to(piv[r:r + 1], shape)

        pw = [powers(k) for k in (1, 2, 4)]

        def tile(i, carry):
            cr, ci, acr, aci = carry
            j = (ntile - 1 - i) if reverse else i
            sl = pl.ds(pl.multiple_of(j * SCAN_ROWS, SCAN_ROWS), SCAN_ROWS)
            xr, xi = ur_ref[sl, :], ui_ref[sl, :]
            for (ar, ai), k in zip(pw, (1, 2, 4)):
                if reverse:
                    keep = rows < SCAN_ROWS - k
                    sr = jnp.where(keep, pltpu.roll(xr, SCAN_ROWS - k, 0), 0.0)
                    si = jnp.where(keep, pltpu.roll(xi, SCAN_ROWS - k, 0), 0.0)
                else:
                    keep = rows >= k
                    sr = jnp.where(keep, pltpu.roll(xr, k, 0), 0.0)
                    si = jnp.where(keep, pltpu.roll(xi, k, 0), 0.0)
                xr, xi = xr + (ar * sr - ai * si), xi + (ar * si + ai * sr)
            xr, xi = xr + (prv * cr - piv * ci), xi + (prv * ci + piv * cr)
            xr_ref[sl, :] = xr
            xi_ref[sl, :] = xi
            if with_states:
                last = rows == SCAN_ROWS - 1
                nr = jnp.where(last, cr, pltpu.roll(xr, SCAN_ROWS - 1, 0))
                ni = jnp.where(last, ci, pltpu.roll(xi, SCAN_ROWS - 1, 0))
                sr, si = sr_ref[sl, :], si_ref[sl, :]
                acr, aci = acr + (sr * nr + si * ni), aci + (sr * ni - si * nr)
            e = 0 if reverse else SCAN_ROWS - 1
            return jnp.broadcast_to(xr[e:e + 1], shape), jnp.broadcast_to(xi[e:e + 1], shape), acr, aci

        def tiles(g, carry):
            for u in range(SCAN_UNROLL):
                carry = tile(g * SCAN_UNROLL + u, carry)
            return carry

        assert ntile % SCAN_UNROLL == 0
        cr, ci, acr, aci = lax.fori_loop(0, ntile // SCAN_UNROLL, tiles,
                                         (cr_sc[...], ci_sc[...], ar_sc[...], ai_sc[...]))
        cr_sc[...] = cr
        ci_sc[...] = ci
        if with_states:
            ar_sc[...] = acr
            ai_sc[...] = aci

            @pl.when(pl.program_id(1) == nt - 1)
            def _():
                dr_ref[...] = acr
                di_ref[...] = aci

    tmap = (lambda l, t: (nt - 1 - t, l)) if reverse else (lambda l, t: (t, l))
    blk = pl.BlockSpec((tb, SCAN_LANES), tmap)
    tab = pl.BlockSpec((SCAN_ROWS, SCAN_LANES), lambda l, t: (0, l))
    full = jax.ShapeDtypeStruct((s, n), F32)
    return _pcall(
        body,
        name=name,
        grid=(nl, nt),
        in_specs=[blk, blk, tab, tab] + ([blk, blk] if with_states else []),
        out_specs=[blk, blk] + ([tab, tab] if with_states else []),
        out_shape=[full, full] + ([jax.ShapeDtypeStruct((SCAN_ROWS, n), F32)] * 2 if with_states else []),
        scratch_shapes=[pltpu.VMEM(shape, F32)] * 4,
        compiler_params=_params(("parallel", "arbitrary")),
    )(ur, ui, pr, pi, *(states if with_states else ()))


@functools.partial(jax.custom_vjp, nondiff_argnums=(4,))
def s5_scan(ur, ui, lr, li, name):
    return tuple(_scan_call(ur, ui, lr, li, False, name + "_f"))


def _s5_scan_fwd(ur, ui, lr, li, name):
    xr, xi = _scan_call(ur, ui, lr, li, False, name + "_f")
    return (xr, xi), (xr, xi, lr, li)


def _s5_scan_bwd(name, res, g):
    xr, xi, lr, li = res
    gr, gi, dlr, dli = _scan_call(g[0], g[1], lr, -li, True, name + "_b", states=(xr, xi))
    return gr, gi, jnp.sum(dlr, axis=0, keepdims=True), jnp.sum(dli, axis=0, keepdims=True)


s5_scan.defvjp(_s5_scan_fwd, _s5_scan_bwd)


def _running_sum(x, reverse):
    n = x.shape[0]
    rows = lax.broadcasted_iota(jnp.int32, x.shape, 0)
    k = 1
    while k < n:
        if reverse:
            x = x + jnp.where(rows < n - k, pltpu.roll(x, n - k, 0), 0.0)
        else:
            x = x + jnp.where(rows >= k, pltpu.roll(x, k, 0), 0.0)
        k *= 2
    return x


@jax.custom_vjp
def cumsum_rows(x):
    return _running_sum(x, False)


cumsum_rows.defvjp(lambda x: (_running_sum(x, False), None), lambda _, g: (_running_sum(g, True),))


def _hg_chunk(HG_SUB, q, k, v, lf, st):
    c = HG_CHUNK
    b = cumsum_rows(lf)
    qb = q * jnp.exp(b)
    o = lax.dot_general(qb.astype(BF16), st.astype(BF16), (((1,), (1,)), ((), ())), preferred_element_type=F32)
    nsub = c // HG_SUB
    srow = lax.broadcasted_iota(jnp.int32, (HG_SUB, HG_SUB, 1), 0)
    scol = lax.broadcasted_iota(jnp.int32, (HG_SUB, HG_SUB, 1), 1)
    smask = scol <= srow
    outs = []
    for i in range(nsub):
        lo = i * HG_SUB
        bi, qi, ki = b[lo:lo + HG_SUB], q[lo:lo + HG_SUB], k[lo:lo + HG_SUB]
        diff = jnp.where(smask, bi[:, None, :] - bi[None, :, :], 0.0)
        e = jnp.where(smask, jnp.exp(diff), 0.0)
        a_ii = jnp.sum(qi[:, None, :] * ki[None, :, :] * e, axis=-1)
        oi = jnp.dot(a_ii.astype(BF16), v[lo:lo + HG_SUB].astype(BF16), preferred_element_type=F32)
        if i > 0:
            r = b[lo - 1:lo]
            qt = qi * jnp.exp(bi - r)
            kt = k[:lo] * jnp.exp(r - b[:lo])
            a_ij = lax.dot_general(qt.astype(BF16), kt.astype(BF16), (((1,), (1,)), ((), ())),
                                   preferred_element_type=F32)
            oi = oi + jnp.dot(a_ij.astype(BF16), v[:lo].astype(BF16), preferred_element_type=F32)
        outs.append(oi)
    o = o + jnp.concatenate(outs, axis=0)
    bl = b[c - 1:c]
    kd = k * jnp.exp(bl - b)
    st_new = st * jnp.exp(bl) + lax.dot_general(v.astype(BF16), kd.astype(BF16), (((0,), (0,)), ((), ())),
                                                preferred_element_type=F32)
    return o, st_new


def _hg_fwd_call(q, k, v, lf, name):
    s, w = q.shape
    nc = s // HG_CHUNK
    blk = pl.BlockSpec((HG_CHUNK, w), lambda i: (i, 0))

    def body(q_ref, k_ref, v_ref, lf_ref, o_ref, st_ref, st_sc):
        @pl.when(pl.program_id(0) == 0)
        def _():
            st_sc[...] = jnp.zeros(st_sc.shape, F32)

        for hh in range(HG_HEADS):
            cs = slice(hh * HG_DK, (hh + 1) * HG_DK)
            st = st_sc[hh]
            st_ref[0, hh] = st
            o, st_new = _hg_chunk(HG_SUB_FWD, q_ref[:, cs], k_ref[:, cs], v_ref[:, cs], lf_ref[:, cs], st)
            o_ref[:, cs] = o
            st_sc[hh] = st_new

    return _pcall(
        body,
        name=name,
        grid=(nc,),
        in_specs=[blk, blk, blk, blk],
        out_specs=[blk, pl.BlockSpec((1, HG_HEADS, HG_DV, HG_DK), lambda i: (i, 0, 0, 0))],
        out_shape=[jax.ShapeDtypeStruct((s, w), F32), jax.ShapeDtypeStruct((nc, HG_HEADS, HG_DV, HG_DK), F32)],
        scratch_shapes=[pltpu.VMEM((HG_HEADS, HG_DV, HG_DK), F32)],
        compiler_params=_params(("arbitrary",)),
    )(q, k, v, lf)


def _hg_bwd_call(q, k, v, lf, sts, do, name):
    s, w = q.shape
    nc = s // HG_CHUNK
    blk = pl.BlockSpec((HG_CHUNK, w), lambda i: (nc - 1 - i, 0))

    def body(q_ref, k_ref, v_ref, lf_ref, st_ref, do_ref, dq_ref, dk_ref, dv_ref, dlf_ref, dst_sc):
        @pl.when(pl.program_id(0) == 0)
        def _():
            dst_sc[...] = jnp.zeros(dst_sc.shape, F32)

        def head(hh, carry):
            cs = pl.ds(pl.multiple_of(hh * HG_DK, HG_DK), HG_DK)
            _, vjp = jax.vjp(functools.partial(_hg_chunk, HG_SUB_BWD), q_ref[:, cs], k_ref[:, cs], v_ref[:, cs],
                             lf_ref[:, cs], st_ref[0, hh])
            dq, dk, dv, dlf, dst = vjp((do_ref[:, cs], dst_sc[hh]))
            dq_ref[:, cs] = dq
            dk_ref[:, cs] = dk
            dv_ref[:, cs] = dv
            dlf_ref[:, cs] = dlf
            dst_sc[hh] = dst
            return carry

        lax.fori_loop(0, HG_HEADS, head, 0)

    out = jax.ShapeDtypeStruct((s, w), F32)
    return _pcall(
        body,
        name=name,
        grid=(nc,),
        in_specs=[blk, blk, blk, blk, pl.BlockSpec((1, HG_HEADS, HG_DV, HG_DK), lambda i: (nc - 1 - i, 0, 0, 0)), blk],
        out_specs=[blk, blk, blk, blk],
        out_shape=[out, out, out, out],
        scratch_shapes=[pltpu.VMEM((HG_HEADS, HG_DV, HG_DK), F32)],
        compiler_params=_params(("arbitrary",)),
    )(q, k, v, lf, sts, do)


@functools.partial(jax.custom_vjp, nondiff_argnums=(4,))
def hgrn_core(q, k, v, lf, name):
    return _hg_fwd_call(q, k, v, lf, name + "_f")[0]


def _hgrn_core_fwd(q, k, v, lf, name):
    o, sts = _hg_fwd_call(q, k, v, lf, name + "_f")
    return o, (q, k, v, lf, sts)


def _hgrn_core_bwd(name, res, do):
    q, k, v, lf, sts = res
    return tuple(_hg_bwd_call(q, k, v, lf, sts, do, name + "_b"))


hgrn_core.defvjp(_hgrn_core_fwd, _hgrn_core_bwd)


def _adamw_call(w, g, m, v, name):
    rows, cols = w.shape
    tr = _tile(rows, 256, 8)
    c1 = 1.0 / (1.0 - ADAM_B1 ** ADAM_STEP)
    c2 = 1.0 / (1.0 - ADAM_B2 ** ADAM_STEP)

    def body(w_ref, g_ref, m_ref, v_ref, d_ref, mo_ref, vo_ref):
        gv = g_ref[...]
        mn = ADAM_B1 * m_ref[...] + (1.0 - ADAM_B1) * gv
        vn = ADAM_B2 * v_ref[...] + (1.0 - ADAM_B2) * (gv * gv)
        d_ref[...] = -ADAM_LR * ((mn * c1) / (jnp.sqrt(vn * c2) + ADAM_EPS) + ADAM_WD * w_ref[...])
        mo_ref[...] = mn
        vo_ref[...] = vn

    blk = pl.BlockSpec((tr, cols), lambda i: (i, 0))
    out = jax.ShapeDtypeStruct((rows, cols), F32)
    return _pcall(
        body,
        name=name,
        grid=(rows // tr,),
        in_specs=[blk, blk, blk, blk],
        out_specs=[blk, blk, blk],
        out_shape=[out, out, out],
        compiler_params=_params(("parallel",)),
    )(w, g, m, v)


def _adamw_layer(w, g, m, v, layer, into, name):
    rows, cols = g.shape
    tr = _tile(rows, 256, 8)
    nb = rows // tr
    c1 = 1.0 / (1.0 - ADAM_B1 ** ADAM_STEP)
    c2 = 1.0 / (1.0 - ADAM_B2 ** ADAM_STEP)

    def body(w_ref, g_ref, m_ref, v_ref, *rest):
        go_ref, d_ref, mo_ref, vo_ref = rest[-4:]
        gv = g_ref[...]
        mn = ADAM_B1 * m_ref[...] + (1.0 - ADAM_B1) * gv
        vn = ADAM_B2 * v_ref[...] + (1.0 - ADAM_B2) * (gv * gv)
        go_ref[...] = gv
        d_ref[...] = -ADAM_LR * ((mn * c1) / (jnp.sqrt(vn * c2) + ADAM_EPS) + ADAM_WD * w_ref[...])
        mo_ref[...] = mn
        vo_ref[...] = vn

    mine = pl.BlockSpec((tr, cols), lambda i: (layer * nb + i, 0))
    out = jax.ShapeDtypeStruct(w.shape, F32)
    n_into = 4 if into is not None else 0
    return _pcall(
        body,
        name=name,
        grid=(nb,),
        in_specs=[mine, pl.BlockSpec((tr, cols), lambda i: (i, 0)), mine, mine] + [ANY] * n_into,
        out_specs=[mine] * 4,
        out_shape=[out] * 4,
        input_output_aliases={4 + k: k for k in range(n_into)},
        compiler_params=_params(("parallel",)),
    )(w, g, m, v, *(into or ()))


ANY = pl.BlockSpec(memory_space=pl.ANY)


def _place():
    return lax.axis_index("x"), lax.axis_index("y"), lax.axis_index("c")


def _other_chips(x, y):
    return [(1 - x, y), (x, 1 - y), (1 - x, 1 - y)]


def _gather_weights(shards):
    nw = len(shards)

    def body(*refs):
        w_refs, g_refs, (send_sems, recv_sems) = refs[:nw], refs[nw:2 * nw], refs[2 * nw:]
        x, y, c = _place()
        myq = 2 * x + y
        sibling = (x, y, 1 - c)
        chips = _other_chips(x, y)

        def copy(k, src, dst, to):
            return pltpu.make_async_remote_copy(src_ref=src, dst_ref=dst, send_sem=send_sems.at[k],
                                                recv_sem=recv_sems.at[k], device_id=to, device_id_type=MESH)

        first = []
        for i in range(nw):
            for j, (px, py) in enumerate(chips):
                first.append(copy(6 * i + j, w_refs[i].at[c], g_refs[i].at[myq, c], (px, py, c)))
                first[-1].start()
        passed = []
        for i in range(nw):
            for j, (px, py) in enumerate(chips):
                q = 2 * px + py
                copy(6 * i + j, w_refs[i].at[c], g_refs[i].at[q, c], (px, py, c)).wait_recv()
                passed.append(copy(6 * i + 3 + j, g_refs[i].at[q, c], g_refs[i].at[q, c], sibling))
                passed[-1].start()
        for i in range(nw):
            for j, (px, py) in enumerate(chips):
                q = 2 * px + py
                copy(6 * i + 3 + j, g_refs[i].at[q, 1 - c], g_refs[i].at[q, 1 - c], sibling).wait_recv()
        for cp in first + passed:
            cp.wait_send()

    got = _pcall(
        body,
        name="gather_weights",
        in_specs=[ANY] * nw,
        out_specs=[ANY] * nw,
        out_shape=[jax.ShapeDtypeStruct((N_CHIPS,) + s.shape, s.dtype) for s in shards],
        scratch_shapes=[pltpu.SemaphoreType.DMA((6 * nw,)), pltpu.SemaphoreType.DMA((6 * nw,))],
    )(*shards)
    my_chip = 2 * lax.axis_index("x") + lax.axis_index("y")
    return [lax.dynamic_update_slice(g, s[None], (my_chip, 0, 0, 0)) for g, s in zip(got, shards)]


HBM = pl.BlockSpec(memory_space=pltpu.HBM)
SEM = pl.BlockSpec(memory_space=pltpu.SEMAPHORE)


def _gather_start(shards, after):
    nw = len(shards)

    def body(*refs):
        w_refs, land_refs = refs[:nw], refs[nw:2 * nw]
        send_sems, recv_sems = refs[2 * nw + 1:2 * nw + 3]
        token = refs[-1]
        x, y, c = _place()
        myq = 2 * x + y
        for i in range(nw):
            hh = shards[i].shape[0] // 2
            rows = pl.ds(c * hh, hh)
            for j, (px, py) in enumerate(_other_chips(x, y)):
                for tc in range(2):
                    pltpu.make_async_remote_copy(
                        src_ref=w_refs[i].at[rows], dst_ref=land_refs[i].at[myq, rows],
                        send_sem=send_sems.at[6 * i + 2 * j + tc], recv_sem=recv_sems.at[6 * i + 2 * j + c],
                        device_id=(px, py, tc), device_id_type=MESH).start()
        token[...] = jnp.zeros_like(token)

    lands = [lax.empty((N_CHIPS,) + s.shape, s.dtype) for s in shards]
    out = _pcall(
        body,
        name="gather_start",
        in_specs=[HBM] * (2 * nw) + [pl.BlockSpec(memory_space=pl.ANY)],
        out_specs=[SEM, SEM] + [HBM] * (2 * nw) + [pl.BlockSpec(memory_space=pltpu.VMEM)],
        out_shape=[pltpu.SemaphoreType.DMA((6 * nw,)), pltpu.SemaphoreType.DMA((6 * nw,))]
        + [pltpu.HBM(s.shape, s.dtype) for s in shards] + [pltpu.HBM(t.shape, t.dtype) for t in lands]
        + [jax.ShapeDtypeStruct((8, 128), F32)],
        input_output_aliases={i: 2 + i for i in range(2 * nw)},
        compiler_params=pltpu.CompilerParams(has_side_effects=pltpu.SideEffectType.DATAFLOW_SIDE_EFFECTING),
    )(*[pltpu.with_memory_space_constraint(t, pltpu.HBM) for t in list(shards) + lands], after)
    return out[0], out[1], out[2:2 + nw], out[2 + nw:2 + 2 * nw], out[-1]


def _gather_wait(send_sems, recv_sems, shards, lands, after):
    nw = len(shards)

    def body(*refs):
        w_refs, land_refs = refs[:nw], refs[nw:2 * nw]
        send_sems, recv_sems = refs[2 * nw:2 * nw + 2]
        x, y, c = _place()
        myq = 2 * x + y
        for i in range(nw):
            hh = shards[i].shape[0] // 2
            for j, (px, py) in enumerate(_other_chips(x, y)):
                q = 2 * px + py
                for other in range(2):
                    mine = pl.ds(c * hh, hh)
                    theirs = pl.ds(other * hh, hh)
                    pltpu.make_async_remote_copy(
                        src_ref=w_refs[i].at[mine], dst_ref=land_refs[i].at[myq, mine],
                        send_sem=send_sems.at[6 * i + 2 * j + other], recv_sem=recv_sems.at[6 * i + 2 * j + other],
                        device_id=(px, py, other), device_id_type=MESH).wait_send()
                    pltpu.make_async_remote_copy(
                        src_ref=w_refs[i].at[theirs], dst_ref=land_refs[i].at[q, theirs],
                        send_sem=send_sems.at[6 * i + 2 * j + other], recv_sem=recv_sems.at[6 * i + 2 * j + other],
                        device_id=(px, py, other), device_id_type=MESH).wait_recv()

    out = _pcall(
        body,
        name="gather_wait",
        in_specs=[HBM] * (2 * nw) + [SEM, SEM, pl.BlockSpec(memory_space=pl.ANY)],
        out_specs=[HBM] * (2 * nw),
        out_shape=[pltpu.HBM(s.shape, s.dtype) for s in shards] + [pltpu.HBM(t.shape, t.dtype) for t in lands],
        input_output_aliases={i: i for i in range(2 * nw)},
        compiler_params=pltpu.CompilerParams(has_side_effects=pltpu.SideEffectType.DATAFLOW_SIDE_EFFECTING),
    )(*shards, *lands, send_sems, recv_sems, after)
    my_chip = 2 * lax.axis_index("x") + lax.axis_index("y")
    return [lax.dynamic_update_slice(g, s[None], (my_chip, 0, 0)) for g, s in zip(out[nw:], out[:nw])]


def _swap_halves_to_sibling(grs, tag=""):
    nw = len(grs)

    def body(*refs):
        g_refs, t_refs, (send_sems, recv_sems) = refs[:nw], refs[nw:2 * nw], refs[2 * nw:]
        x, y, c = _place()
        cps = []
        for i in range(nw):
            hh = grs[i].shape[1] // 2
            cps.append(pltpu.make_async_remote_copy(
                src_ref=g_refs[i].at[:, pl.ds((1 - c) * hh, hh)], dst_ref=t_refs[i], send_sem=send_sems.at[i],
                recv_sem=recv_sems.at[i], device_id=(x, y, 1 - c), device_id_type=MESH))
            cps[-1].start()
        for cp in cps:
            cp.wait()

    return _pcall(
        body,
        name="grad_d2d" + tag,
        in_specs=[ANY] * nw,
        out_specs=[ANY] * nw,
        out_shape=[jax.ShapeDtypeStruct((g.shape[0], g.shape[1] // 2, g.shape[2]), g.dtype) for g in grs],
        scratch_shapes=[pltpu.SemaphoreType.DMA((nw,)), pltpu.SemaphoreType.DMA((nw,))],
    )(*grs)


def _add_half(gr, t1, place, name):
    nq, r, wd = gr.shape
    hh = r // 2
    tr = _tile(hh, 512, 16)
    nb = hh // tr

    def body(p_ref, g_ref, t_ref, o_ref):
        o_ref[...] = (g_ref[...] + t_ref[...]).astype(BF16)

    return _pcall(
        body,
        name=name,
        grid_spec=pltpu.PrefetchScalarGridSpec(
            num_scalar_prefetch=1,
            grid=(nq, nb),
            in_specs=[
                pl.BlockSpec((1, tr, wd), lambda q, i, p_ref: (q, p_ref[0] * nb + i, 0)),
                pl.BlockSpec((1, tr, wd), lambda q, i, p_ref: (q, i, 0)),
            ],
            out_specs=pl.BlockSpec((1, tr, wd), lambda q, i, p_ref: (q, i, 0)),
        ),
        out_shape=jax.ShapeDtypeStruct((nq, hh, wd), BF16),
        compiler_params=_params(("parallel", "parallel")),
    )(place, gr, t1)


def _scatter_copies(p_refs, t_refs, send_sems, recv_sems):
    x, y, c = _place()
    return [pltpu.make_async_remote_copy(
        src_ref=p_refs[i].at[2 * px + py], dst_ref=t_refs[i].at[j], send_sem=send_sems.at[3 * i + j],
        recv_sem=recv_sems.at[3 * i + j], device_id=(px, py, c), device_id_type=MESH)
        for i in range(len(p_refs)) for j, (px, py) in enumerate(_other_chips(x, y))]


def _scatter_start(ps, after, tag=""):
    nw = len(ps)

    def body(*refs):
        send_sems, recv_sems = refs[2 * nw + 1:2 * nw + 3]
        for cp in _scatter_copies(refs[:nw], refs[nw:2 * nw], send_sems, recv_sems):
            cp.start()
        refs[-1][...] = jnp.zeros_like(refs[-1])

    lands = [lax.empty((3,) + p.shape[1:], p.dtype) for p in ps]
    out = _pcall(
        body,
        name="scatter_start" + tag,
        in_specs=[HBM] * (2 * nw) + [pl.BlockSpec(memory_space=pl.ANY)],
        out_specs=[SEM, SEM] + [HBM] * (2 * nw) + [pl.BlockSpec(memory_space=pltpu.VMEM)],
        out_shape=[pltpu.SemaphoreType.DMA((3 * nw,)), pltpu.SemaphoreType.DMA((3 * nw,))]
        + [pltpu.HBM(t.shape, t.dtype) for t in list(ps) + lands] + [jax.ShapeDtypeStruct((8, 128), F32)],
        input_output_aliases={i: 2 + i for i in range(2 * nw)},
        compiler_params=pltpu.CompilerParams(has_side_effects=pltpu.SideEffectType.DATAFLOW_SIDE_EFFECTING),
    )(*[pltpu.with_memory_space_constraint(t, pltpu.HBM) for t in list(ps) + lands], after)
    return out[0], out[1], out[2:2 + nw], out[2 + nw:2 + 2 * nw], out[-1]


def _scatter_wait(send_sems, recv_sems, ps, lands, after, tag=""):
    nw = len(ps)

    def body(*refs):
        for cp in _scatter_copies(refs[:nw], refs[nw:2 * nw], refs[2 * nw], refs[2 * nw + 1]):
            cp.wait_send()
            cp.wait_recv()

    out = _pcall(
        body,
        name="scatter_wait" + tag,
        in_specs=[HBM] * (2 * nw) + [SEM, SEM, pl.BlockSpec(memory_space=pl.ANY)],
        out_specs=[HBM] * (2 * nw),
        out_shape=[pltpu.HBM(t.shape, t.dtype) for t in list(ps) + list(lands)],
        input_output_aliases={i: i for i in range(2 * nw)},
        compiler_params=pltpu.CompilerParams(has_side_effects=pltpu.SideEffectType.DATAFLOW_SIDE_EFFECTING),
    )(*ps, *lands, send_sems, recv_sems, after)
    return out[nw:]


def _add_partials(gr, t1, t2, place, name):
    nq, r, wd = gr.shape
    hh = r // 2
    tr = _tile(hh, 512, 16)
    nb = hh // tr

    def body(p_ref, g_ref, s_ref, t_ref, o_ref):
        o_ref[...] = (((g_ref[0] + s_ref[0]) + t_ref[0].astype(F32)) + t_ref[1].astype(F32)) + t_ref[2].astype(F32)

    return _pcall(
        body,
        name=name,
        grid_spec=pltpu.PrefetchScalarGridSpec(
            num_scalar_prefetch=1,
            grid=(nb,),
            in_specs=[
                pl.BlockSpec((1, tr, wd), lambda i, p_ref: (p_ref[1], p_ref[0] * nb + i, 0)),
                pl.BlockSpec((1, tr, wd), lambda i, p_ref: (p_ref[1], i, 0)),
                pl.BlockSpec((3, tr, wd), lambda i, p_ref: (0, i, 0)),
            ],
            out_specs=pl.BlockSpec((tr, wd), lambda i, p_ref: (i, 0)),
        ),
        out_shape=jax.ShapeDtypeStruct((hh, wd), F32),
        compiler_params=_params(("parallel",)),
    )(place, gr, t1, t2)


def _join_halves(halves, tag=""):
    nw = len(halves)

    def body(*refs):
        h_refs, f_refs, (send_sems, recv_sems) = refs[:nw], refs[nw:2 * nw], refs[2 * nw:]
        x, y, c = _place()

        def copy(i, rows):
            hh = halves[i].shape[0]
            return pltpu.make_async_remote_copy(src_ref=h_refs[i], dst_ref=f_refs[i].at[pl.ds(rows * hh, hh)],
                                                send_sem=send_sems.at[i], recv_sem=recv_sems.at[i],
                                                device_id=(x, y, 1 - c), device_id_type=MESH)

        for i in range(nw):
            copy(i, c).start()
        for i in range(nw):
            copy(i, c).wait_send()
            copy(i, 1 - c).wait_recv()

    got = _pcall(
        body,
        name="grad_join" + tag,
        in_specs=[ANY] * nw,
        out_specs=[ANY] * nw,
        out_shape=[jax.ShapeDtypeStruct((2 * h.shape[0], h.shape[1]), h.dtype) for h in halves],
        scratch_shapes=[pltpu.SemaphoreType.DMA((nw,)), pltpu.SemaphoreType.DMA((nw,))],
    )(*halves)
    c = lax.axis_index("c")
    return [lax.dynamic_update_slice(g, h, (c * h.shape[0], 0)) for g, h in zip(got, halves)]


N_DEV = 8


def _allreduce_small(buf):
    rows, wd = buf.shape
    n = rows // N_DEV

    def body(x_ref, o_ref, stage_ref, send_sems, recv_sems):
        x, y, c = _place()
        me = 4 * x + 2 * y + c

        def peer(k):
            return (x ^ (k >> 2), y ^ ((k >> 1) & 1), c ^ (k & 1))

        def chunk(ref, d):
            return ref.at[pl.ds(pl.multiple_of(d * n, 8), n)]

        first = []
        for k in range(1, N_DEV):
            px, py, pc = peer(k)
            cp = pltpu.make_async_remote_copy(src_ref=chunk(x_ref, 4 * px + 2 * py + pc), dst_ref=stage_ref.at[k],
                                              send_sem=send_sems.at[k - 1], recv_sem=recv_sems.at[k - 1],
                                              device_id=(px, py, pc), device_id_type=MESH)
            cp.start()
            first.append(cp)
        acc = chunk(x_ref, me)[...]
        for k in range(1, N_DEV):
            first[k - 1].wait_recv()
            acc = acc + stage_ref[k]
        chunk(o_ref, me)[...] = acc
        second = []
        for k in range(1, N_DEV):
            cp = pltpu.make_async_remote_copy(src_ref=chunk(o_ref, me), dst_ref=chunk(o_ref, me),
                                              send_sem=send_sems.at[6 + k], recv_sem=recv_sems.at[6 + k],
                                              device_id=peer(k), device_id_type=MESH)
            cp.start()
            second.append(cp)
        for k in range(1, N_DEV):
            px, py, pc = peer(k)
            them = 4 * px + 2 * py + pc
            pltpu.make_async_remote_copy(src_ref=chunk(o_ref, them), dst_ref=chunk(o_ref, them),
                                         send_sem=send_sems.at[6 + k], recv_sem=recv_sems.at[6 + k],
                                         device_id=(px, py, pc), device_id_type=MESH).wait_recv()
        for cp in first + second:
            cp.wait_send()

    vmem = pl.BlockSpec(memory_space=pltpu.VMEM)
    return _pcall(
        body,
        name="allreduce_small",
        in_specs=[vmem],
        out_specs=vmem,
        out_shape=jax.ShapeDtypeStruct((rows, wd), F32),
        scratch_shapes=[pltpu.VMEM((N_DEV, n, wd), F32), pltpu.SemaphoreType.DMA((14,)),
                        pltpu.SemaphoreType.DMA((14,))],
    )(buf)


@functools.partial(jax.custom_vjp, nondiff_argnums=(1,))
def split_cols(z, spec):
    outs = []
    for segs in spec[0]:
        parts = [z[:, s:s + n] for s, n in segs]
        outs.append(parts[0] if len(parts) == 1 else jnp.concatenate(parts, axis=1))
    return tuple(outs)


def _split_cols_fwd(z, spec):
    return split_cols(z, spec), None


def _split_cols_bwd(spec, _, gs):
    pieces, width = spec
    segs = []
    for p, piece in enumerate(pieces):
        off = 0
        for s, n in piece:
            segs.append((s, n, p, off))
            off += n
    rows, parts, pos = gs[0].shape[0], [], 0
    for s, n, p, off in sorted(segs):
        if s > pos:
            parts.append(jnp.zeros((rows, s - pos), gs[0].dtype))
        parts.append(gs[p] if (off == 0 and n == gs[p].shape[1]) else gs[p][:, off:off + n])
        pos = s + n
    if pos < width:
        parts.append(jnp.zeros((rows, width - pos), gs[0].dtype))
    return (jnp.concatenate(parts, axis=1),)


split_cols.defvjp(_split_cols_fwd, _split_cols_bwd)


def _contiguous(sizes):
    pieces, off = [], 0
    for n in sizes:
        pieces.append(((off, n),))
        off += n
    return tuple(pieces), off


def _rope(t, cos, sin):
    half = t.shape[-1] // 2
    t1, t2 = t[..., :half], t[..., half:]
    return jnp.concatenate([t1 * cos - t2 * sin, t2 * cos + t1 * sin], axis=-1)


def _s5_mixer(u, sp, l, w16, toks):
    g, p_, h = SSM_GROUPS, SSM_STATE, SSM_GROUP_CH
    lam_re, lam_im = sp["ssm_lam_re", l], sp["ssm_lam_im", l]
    step = jnp.exp(sp["ssm_log_step", l])[:, None]
    mag, ang = jnp.exp(lam_re * step), lam_im * step
    lbr, lbi = mag * jnp.cos(ang), mag * jnp.sin(ang)
    den = lam_re * lam_re + lam_im * lam_im
    nr, ni = lbr - 1.0, lbi
    qr, qi = (nr * lam_re + ni * lam_im) / den, (ni * lam_re - nr * lam_im) / den
    b_re, b_im = sp["ssm_b_re", l], sp["ssm_b_im", l]
    bbr = qr[..., None] * b_re - qi[..., None] * b_im
    bbi = qr[..., None] * b_im + qi[..., None] * b_re
    nb = g // SSM_PACK
    eye = jnp.eye(SSM_PACK, dtype=F32)

    def in_blocks(t):
        return jnp.einsum("jgph,gk->jghkp", t.reshape(nb, SSM_PACK, p_, h), eye).reshape(nb, SSM_PACK * h, SSM_PACK * p_)

    def out_blocks(t):
        return jnp.einsum("jghp,gk->jgpkh", t.reshape(nb, SSM_PACK, h, p_), eye).reshape(nb, SSM_PACK * p_, SSM_PACK * h)

    nm = "ssm%d" % l
    ur = bd_matmul(u, in_blocks(bbr), nm + "_bur")
    ui = bd_matmul(u, in_blocks(bbi), nm + "_bui")
    xr, xi = s5_scan(ur, ui, lbr.reshape(1, g * p_), lbi.reshape(1, g * p_), nm + "_scan")
    y = (bd_matmul(xr, out_blocks(sp["ssm_c_re", l]), nm + "_cr") + bd_matmul(xi, out_blocks(-sp["ssm_c_im", l]), nm + "_ci")
         + sp["ssm_d", l].reshape(1, g * h) * u)
    y = jax.nn.gelu(y)
    zz = matmul_w(y, w16["ssm_w_glu", l], toks["ssm_w_glu", l], nm + "_glu")
    z_out, z_gate = split_cols(zz, _contiguous((D_MODEL, D_MODEL)))
    return z_out * jax.nn.sigmoid(z_gate)


def _mla_mixer(q_lat, kv_lat, k_rope, cos, sin, sp, l, w16, toks):
    s = q_lat.shape[0]
    nm = "mla%d" % l
    dqk = MLA_NOPE + MLA_ROPE
    qt = matmul_w_t(rmsnorm(q_lat, sp["mla_q_norm", l], nm + "_qn"), w16["mla_w_uq", l], toks["mla_w_uq", l],
                    nm + "_uq").reshape(MLA_HEADS, dqk, s)
    half = MLA_ROPE // 2
    q1, q2 = qt[:, MLA_NOPE:MLA_NOPE + half], qt[:, MLA_NOPE + half:]
    ct, st = cos.T[None], sin.T[None]
    qt = jnp.concatenate([qt[:, :MLA_NOPE], q1 * ct - q2 * st, q2 * ct + q1 * st], axis=1).reshape(MLA_HEADS * dqk, s)
    kvt = matmul_w_t(rmsnorm(kv_lat, sp["mla_kv_norm", l], nm + "_kvn"), w16["mla_w_ukv", l], toks["mla_w_ukv", l],
                     nm + "_ukv").reshape(MLA_HEADS, MLA_NOPE + MLA_V, s)
    k_pe = _rope(k_rope, cos, sin)
    k = jnp.concatenate([kvt[:, :MLA_NOPE].transpose(0, 2, 1),
                         jnp.broadcast_to(k_pe[None], (MLA_HEADS, s, MLA_ROPE))], axis=-1)
    vt = kvt[:, MLA_NOPE:].reshape(MLA_HEADS * MLA_V, s)
    ot = attention(qt, k, vt, 1.0 / math.sqrt(dqk), True, nm + "_att")
    return matmul_w_at(ot, w16["mla_w_o", l], toks["mla_w_o", l], nm + "_o")


def _hgrn_mixer(q, f_logit, i_in, g, lb, sp, l, w16, toks):
    s = q.shape[0]
    nm = "hg%d" % l
    f = lb + (1.0 - lb) * jax.nn.sigmoid(f_logit)
    o = hgrn_core(jax.nn.silu(q), 1.0 - f, i_in, jnp.log(f), nm + "_core")
    o = rmsnorm(o.reshape(s * HG_HEADS, HG_DV), sp["hg_g_norm", l], nm + "_gn").reshape(s, HG_HEADS * HG_DV)
    return silu_gated_out(o, g, w16["hg_w_o", l], toks["hg_w_o", l], nm + "_o")


def _cross_attention(x, h, mem_n, l, w16, toks):
    s, m = h.shape[0], mem_n.shape[0]
    nm = "xat%d" % l
    xw = X_HEADS * X_HEAD_DIM
    qt = matmul_w_t(h, w16["x_w_q", l], toks["x_w_q", l], nm + "_q")
    kv = matmul_w(mem_n, w16["x_w_kv", l], toks["x_w_kv", l], nm + "_kv")
    k, v = split_cols(kv, _contiguous((xw, xw)))
    k = k.reshape(m, X_HEADS, X_HEAD_DIM).transpose(1, 0, 2)
    ot = attention(qt, k, v.T, 1.0 / math.sqrt(X_HEAD_DIM), False, nm + "_att")
    return matmul_w_at_res(x, ot, w16["x_w_o", l], toks["x_w_o", l], nm + "_o")


def _in_spec():
    hw = HG_HEADS * HG_DK
    sizes = (SSM_WIDTH, MLA_Q_RANK, MLA_KV_RANK, MLA_ROPE, hw, hw, hw, hw, D_MODEL, D_MODEL, D_MODEL)
    pieces, lo = [], 0
    for n in sizes:
        segs, c = [], lo
        while c < lo + n:
            end = min(lo + n, (c // D_IN_SHARD + 1) * D_IN_SHARD)
            segs.append((c + (D_IN_BLOCK - D_IN_SHARD) * (c // D_IN_SHARD), end - c))
            c = end
        pieces.append(tuple(segs))
        lo += n
    assert lo == D_IN
    return tuple(pieces), N_CHIPS * D_IN_BLOCK


def _layer(x, mem, cos, sin, lb, l, sp, w16, toks):
    s = x.shape[0]
    nm = "l%d" % l
    h = rmsnorm(x, sp["norm_mix", l], nm + "_nmix")
    z = matmul_w(h, w16["w_in", l], toks["w_in", l], nm + "_in")
    u, q_lat, kv_lat, k_rope, hg_q, hg_f, hg_i, hg_g, l_ssm, l_mla, l_hg = split_cols(z, _in_spec())
    y_ssm = _s5_mixer(u, sp, l, w16, toks)
    y_mla = _mla_mixer(q_lat, kv_lat, k_rope, cos, sin, sp, l, w16, toks)
    y_hg = _hgrn_mixer(hg_q, hg_f, hg_i, hg_g, lb, sp, l, w16, toks)
    x = gated_merge_out(x, (y_ssm, y_mla, y_hg), (l_ssm, l_mla, l_hg), w16["w_out", l], toks["w_out", l], nm + "_out")
    hc = rmsnorm(x, sp["norm_cross", l], nm + "_ncross")
    mem_n = rmsnorm(mem, sp["norm_mem", l], nm + "_nmem")
    x = _cross_attention(x, hc, mem_n, l, w16, toks)
    hf = rmsnorm(x, sp["norm_ffn", l], nm + "_nffn")
    gt = matmul_w(hf, w16["ffn_w_gate", l], toks["ffn_w_gate", l], nm + "_gate")
    up = matmul_w(hf, w16["ffn_w_up", l], toks["ffn_w_up", l], nm + "_up")
    return swiglu_down(x, gt, up, w16["ffn_w_down", l], toks["ffn_w_down", l], nm + "_down")


def _lower_bounds(hg_lb):
    lb_p = jax.nn.softmax(hg_lb, axis=0)
    return jnp.cumsum(lb_p, axis=0) - lb_p[0:1]


def _loss_head(norm_final, x, target):
    y = rmsnorm(x, norm_final, "nfinal")
    return 0.5 * jnp.sum(jnp.mean(jnp.square(y - target), axis=-1))


def _forward_backward(sp, toks, x, w16, later, between, mem, pos, target):
    half = MLA_ROPE // 2
    inv_freq = ROPE_THETA ** (-jnp.arange(half, dtype=F32) / half)
    ang = pos.astype(F32)[:, None] * inv_freq
    cos, sin = jnp.cos(ang), jnp.sin(ang)
    lower, lower_vjp = jax.vjp(_lower_bounds, sp["hg_lb", ALL])
    token, wait_for_weights = later
    x = x + token[0, 0]
    vjps = []
    for l in range(DEPTH):
        if l == 1:
            w16 = wait_for_weights(w16, x)
        sp_l = {k: v for k, v in sp.items() if k[1] == l}
        toks_l = {k: v for k, v in toks.items() if k[1] == l}
        weights = w16
        x, layer_vjp = jax.vjp(lambda s, t, lb, xx, l=l, weights=weights: _layer(xx, mem, cos, sin, lb, l, s, weights, t),
                               sp_l, toks_l, lower[l], x)
        vjps.append(layer_vjp)
    loss, head_vjp = jax.vjp(lambda g, xx: _loss_head(g, xx, target), sp["norm_final", ALL], x)
    g_sp, g_tok, d_lower = {}, {}, [None] * DEPTH
    g_sp["norm_final", ALL], dx = head_vjp(jnp.ones((), F32))
    for l in reversed(range(DEPTH)):
        g_s, g_t, d_lower[l], dx = vjps[l](dx)
        g_sp.update(g_s)
        g_tok.update(g_t)
        if l > 0:
            dx = between(l, g_t, dx)
    (g_sp["hg_lb", ALL],) = lower_vjp(jnp.stack(d_lower))
    return loss, g_sp, g_tok, dx


WEIGHTS = ["norm_mix", "w_in", "ssm_lam_re", "ssm_lam_im", "ssm_b_re", "ssm_b_im", "ssm_c_re", "ssm_c_im", "ssm_d",
           "ssm_log_step", "ssm_w_glu", "mla_q_norm", "mla_kv_norm", "mla_w_uq", "mla_w_ukv", "mla_w_o", "hg_lb",
           "hg_g_norm", "hg_w_o", "w_out", "norm_cross", "norm_mem", "x_w_q", "x_w_kv", "x_w_o", "norm_ffn",
           "ffn_w_gate_up", "ffn_w_down", "norm_final"]
INPUTS = ["x", "mem", "positions"] + WEIGHTS + ["loss_target"] + ["m_" + n for n in WEIGHTS] + ["v_" + n for n in WEIGHTS]
BIG = [("w_in", 2), ("ssm_w_glu", 2), ("mla_w_uq", 2), ("mla_w_ukv", 2), ("mla_w_o", 2), ("hg_w_o", 2), ("w_out", 1),
       ("x_w_q", 1), ("x_w_kv", 1), ("x_w_o", 2), ("ffn_w_gate_up", 2), ("ffn_w_down", 1)]
SMALL = [n for n in WEIGHTS if n not in dict(BIG)]
WHOLE = ("hg_lb", "norm_final")
ALL = -1
SMALL_W = 128
BLOCKED = ("w_in", "ffn_w_gate_up")


def _pack(arrs, rows, width, dtype):
    flat = jnp.concatenate([a.astype(dtype).reshape(-1) for a in arrs])
    return jnp.pad(flat, (0, rows * width - flat.shape[0])).reshape(rows, width)


def _unpack(buf, like):
    flat = buf.reshape(-1)
    out, off = [], 0
    for a in like:
        n = math.prod(a.shape)
        out.append(flat[off:off + n].reshape(a.shape))
        off += n
    return out


def kernel(
    x, mem, positions, norm_mix, w_in, ssm_lam_re, ssm_lam_im, ssm_b_re, ssm_b_im, ssm_c_re, ssm_c_im, ssm_d,
    ssm_log_step, ssm_w_glu, mla_q_norm, mla_kv_norm, mla_w_uq, mla_w_ukv, mla_w_o, hg_lb, hg_g_norm, hg_w_o,
    w_out, norm_cross, norm_mem, x_w_q, x_w_kv, x_w_o, norm_ffn, ffn_w_gate_up, ffn_w_down, norm_final,
    loss_target, m_norm_mix, m_w_in, m_ssm_lam_re, m_ssm_lam_im, m_ssm_b_re, m_ssm_b_im, m_ssm_c_re, m_ssm_c_im,
    m_ssm_d, m_ssm_log_step, m_ssm_w_glu, m_mla_q_norm, m_mla_kv_norm, m_mla_w_uq, m_mla_w_ukv, m_mla_w_o,
    m_hg_lb, m_hg_g_norm, m_hg_w_o, m_w_out, m_norm_cross, m_norm_mem, m_x_w_q, m_x_w_kv, m_x_w_o, m_norm_ffn,
    m_ffn_w_gate_up, m_ffn_w_down, m_norm_final, v_norm_mix, v_w_in, v_ssm_lam_re, v_ssm_lam_im, v_ssm_b_re,
    v_ssm_b_im, v_ssm_c_re, v_ssm_c_im, v_ssm_d, v_ssm_log_step, v_ssm_w_glu, v_mla_q_norm, v_mla_kv_norm,
    v_mla_w_uq, v_mla_w_ukv, v_mla_w_o, v_hg_lb, v_hg_g_norm, v_hg_w_o, v_w_out, v_norm_cross, v_norm_mem,
    v_x_w_q, v_x_w_kv, v_x_w_o, v_norm_ffn, v_ffn_w_gate_up, v_ffn_w_down, v_norm_final):
    given = locals()
    a = {n: given[n] for n in INPUTS}
    x, mem, pos, target = a["x"][0], a["mem"][0], a["positions"][0], a["loss_target"][0]
    place = jnp.stack([lax.axis_index("c"), 2 * lax.axis_index("x") + lax.axis_index("y")]).astype(jnp.int32)

    def shard16(n):
        w = a[n].astype(BF16)
        if n == "w_in":
            w = jnp.pad(w, ((0, 0), (0, 0), (0, D_IN_BLOCK - D_IN_SHARD)))
        return w

    def use(blocks, l, w16):
        for (n, ax), got in zip(BIG, blocks):
            if n in BLOCKED:
                full = got
            elif ax == 2:
                full = jnp.concatenate([got[q] for q in range(N_CHIPS)], axis=1)
            else:
                full = got.reshape(N_CHIPS * got.shape[1], got.shape[2])
            if n == "ffn_w_gate_up":
                half = N_CHIPS // 2
                w16["ffn_w_gate", l], w16["ffn_w_up", l] = full[:half], full[half:]
            else:
                w16[n, l] = full
        return w16

    shards = [shard16(n) for n, _ in BIG]
    halves = [s[0].reshape(2, s.shape[1] // 2, s.shape[2]) for s in shards]
    w16 = use([g.reshape(N_CHIPS, s.shape[1], s.shape[2]) for g, s in zip(_gather_weights(halves), shards)], 0, {})
    pending = _gather_start([s[1] for s in shards], after=w16["w_in", 0])
    toks = {}
    for n, ax in BIG:
        rows, cols = a[n].shape[1], shards[BIG.index((n, ax))].shape[2]
        shape = (N_CHIPS, rows, cols) if n in BLOCKED else (rows, N_CHIPS * cols) if ax == 2 else (N_CHIPS * rows, cols)
        for l in range(DEPTH):
            if n == "ffn_w_gate_up":
                toks["ffn_w_gate", l] = jnp.zeros((N_CHIPS // 2,) + shape[1:], F32)
                toks["ffn_w_up", l] = jnp.zeros((N_CHIPS // 2,) + shape[1:], F32)
            else:
                toks[n, l] = jnp.zeros(shape, F32)

    def layer1_weights(w16, after):
        return use(_gather_wait(*pending[:4], after), 1, dict(w16))
    sp = {}
    for n in SMALL:
        if n in WHOLE:
            sp[n, ALL] = a[n]
        else:
            for l in range(DEPTH):
                sp[n, l] = a[n][l]

    def chip_sums_of(g_tok, l):
        grads = []
        for n, ax in BIG:
            if n == "ffn_w_gate_up":
                g = jnp.concatenate([g_tok["ffn_w_gate", l], g_tok["ffn_w_up", l]], axis=0)
            else:
                g = g_tok[n, l]
            if n in BLOCKED:
                grads.append(g)
            elif ax == 2:
                grads.append(g.reshape(g.shape[0], N_CHIPS, g.shape[1] // N_CHIPS).transpose(1, 0, 2))
            else:
                grads.append(g.reshape(N_CHIPS, g.shape[0] // N_CHIPS, g.shape[1]))
        from_sibling = _swap_halves_to_sibling(grads, tag=str(l))
        sums = [_add_half(g, t, place, "grad_add_half_%s%d" % (n, l)) for (n, _), g, t in zip(BIG, grads, from_sibling)]
        return grads, from_sibling, sums

    def totals_of(l, grads, from_sibling, from_chips):
        return [_add_partials(g, t1, t2, place, "grad_add_partials_%s%d" % (n, l))
                for (n, _), g, t1, t2 in zip(BIG, grads, from_sibling, from_chips)]

    started = {}

    def between(l, g_tok_l, dx):
        grads, from_sibling, sums = chip_sums_of(g_tok_l, l)
        flight = _scatter_start(sums, after=dx)
        started[l] = (grads, from_sibling, flight)
        return dx + flight[4][0, 0]

    loss, g_sp, g_tok, g_x = _forward_backward(sp, toks, x, w16, (pending[4], layer1_weights), between, mem, pos, target)
    g_small = {n: g_sp[n, ALL] if n in WHOLE else jnp.stack([g_sp[n, l] for l in range(DEPTH)]) for n in SMALL}
    totals = {}
    for l, (grads, from_sibling, flight) in started.items():
        totals[l] = totals_of(l, grads, from_sibling, _scatter_wait(*flight[:4], after=g_x))
    grads, from_sibling, sums = chip_sums_of(g_tok, 0)
    out_g, out_d, out_m, out_v = {}, {}, {}, {}
    small_like = [a[n] for n in SMALL] + [jnp.zeros((1,), F32)]
    n_small = sum(math.prod(t.shape) for t in small_like)
    srows = -(-n_small // (SMALL_W * 8 * N_DEV)) * 8 * N_DEV
    reduced = _allreduce_small(_pack([g_small[n] for n in SMALL] + [loss.reshape(1)], srows, SMALL_W, F32))
    flight = _scatter_start(sums, after=reduced, tag="0")
    reduced = reduced + flight[4][0, 0]
    red = _unpack(reduced, small_like)
    loss = red[-1].reshape(())
    small_arrs = [a[n] for n in SMALL]
    d, m, v = _adamw_call(
        _pack(small_arrs, srows, SMALL_W, F32), reduced,
        _pack([a["m_" + n] for n in SMALL], srows, SMALL_W, F32),
        _pack([a["v_" + n] for n in SMALL], srows, SMALL_W, F32), "adamw_small")
    for n, gg, dd, mm_, vv in zip(SMALL, red[:-1], _unpack(d, small_arrs), _unpack(m, small_arrs), _unpack(v, small_arrs)):
        out_g[n], out_d[n], out_m[n], out_v[n] = gg, dd, mm_, vv

    def adamw_big(l, joined, into):
        outs = {}
        for (n, _), g in zip(BIG, joined):
            shp = a[n].shape
            two = lambda t: t.reshape(shp[0] * shp[1], shp[2])
            outs[n] = _adamw_layer(two(a[n]), g[:, :shp[2]], two(a["m_" + n]), two(a["v_" + n]), l,
                                   into and into[n], "adamw%d_%s" % (l, n))
        return outs

    upper = adamw_big(1, _join_halves(totals[1], tag="1"), None)
    from_chips = _scatter_wait(*flight[:4], after=upper[BIG[-1][0]][1], tag="0")
    done = adamw_big(0, _join_halves(totals_of(0, grads, from_sibling, from_chips), tag="0"), upper)
    for n, _ in BIG:
        out_g[n], out_d[n], out_m[n], out_v[n] = [t.reshape(a[n].shape) for t in done[n]]

    return (loss, g_x[None], *[out_g[n] for n in WEIGHTS], *[out_d[n] for n in WEIGHTS],
            *[out_m[n] for n in WEIGHTS], *[out_v[n] for n in WEIGHTS])
```

```python
import functools
import math

import jax
import jax.numpy as jnp
from jax import lax
from jax.experimental import pallas as pl
from jax.experimental.pallas import tpu as pltpu

F32 = jnp.float32
BF16 = jnp.bfloat16
MESH = pl.DeviceIdType.MESH

D_MODEL = 1024
DEPTH = 2
RMS_EPS = 1e-6
SSM_GROUPS, SSM_GROUP_CH, SSM_STATE = 32, 16, 64
SSM_WIDTH = SSM_GROUPS * SSM_GROUP_CH
SSM_LANES = SSM_GROUPS * SSM_STATE
SSM_PACK = 8
MLA_HEADS, MLA_Q_RANK, MLA_KV_RANK, MLA_NOPE, MLA_ROPE, MLA_V = 8, 512, 256, 64, 32, 64
ROPE_THETA = 10000.0
HG_HEADS, HG_DK, HG_DV, HG_CHUNK = 4, 128, 128, 64
HG_SUB_FWD, HG_SUB_BWD = 32, 16
X_HEADS, X_HEAD_DIM = 4, 128
D_FF = 2816
D_IN = 6432
N_CHIPS = 4
D_IN_SHARD = D_IN // N_CHIPS
D_IN_BLOCK = 1664
MASK_VALUE = -1e30

ADAM_LR, ADAM_B1, ADAM_B2, ADAM_EPS, ADAM_WD, ADAM_STEP = 0.001, 0.9, 0.999, 1e-08, 0.01, 10

VMEM_LIMIT = 48 * 1024 * 1024


def _pcall(body, **kw):
    return pl.pallas_call(body, **kw)


def _params(sem):
    return pltpu.CompilerParams(dimension_semantics=sem, vmem_limit_bytes=VMEM_LIMIT)


def _tile(n, cap, align=128):
    if n <= cap:
        return n
    t = (cap // align) * align
    while t >= align:
        if n % t == 0:
            return t
        t -= align
    return n


MM_VMEM_BUDGET = 36 * 1024 * 1024
MM_TILE_CAP = 2048


def _divisors(n, cap, align=128):
    out = [n] if n <= cap else []
    out += [t for t in range(align, min(n, cap + 1), align) if n % t == 0]
    return sorted(set(out), reverse=True)


@functools.lru_cache(maxsize=None)
def _mm_tiles(m, n, k, a_bytes, b_bytes, o_bytes, n_unit, k_unit):
    best = None
    for tm in _divisors(m, MM_TILE_CAP):
        for tn in _divisors(n_unit, MM_TILE_CAP):
            for tk in _divisors(k_unit, MM_TILE_CAP):
                nk = k // tk
                need = 2 * (tm * tk * a_bytes + tk * tn * b_bytes + tm * tn * o_bytes)
                need += tm * tn * 4 * (2 if nk > 1 else 1)
                need += (tm * tk * 2 if a_bytes > 2 else 0) + (tk * tn * 2 if b_bytes > 2 else 0)
                if need > MM_VMEM_BUDGET:
                    continue
                key = ((m // tm) * (n // tn) * nk, nk, -tn)
                if best is None or key < best[0]:
                    best = (key, (tm, tn, tk))
    assert best is not None, (m, n, k)
    return best[1]


def _mm(a, b, *, ta=False, tb=False, out_dtype=F32, name="mm", b_blocks=False, out_blocks=0, pro=None, epi=None,
        save_left=False):
    if ta:
        K, M = a.shape
    else:
        M, K = a.shape
    n_unit = k_unit = None
    if b_blocks:
        nb, rows, cols = b.shape
        if tb:
            N, Kb, k_unit = rows, nb * cols, cols
        else:
            Kb, N, n_unit = rows, nb * cols, cols
    elif tb:
        N, Kb = b.shape
    else:
        Kb, N = b.shape
    if out_blocks:
        assert not b_blocks and N % out_blocks == 0
        n_unit = N // out_blocks
    assert K == Kb, (a.shape, b.shape, ta, tb)
    n_pro = len(pro) - 1 if pro else 0
    n_epi = len(epi) - 1 if epi else 0
    tm, tn, tk = _mm_tiles(M, N, K, a.dtype.itemsize * (1 + n_pro), b.dtype.itemsize,
                           jnp.dtype(out_dtype).itemsize * max(1, 2 * n_epi), n_unit or N, k_unit or K)
    nk = K // tk
    dims = (((0,) if ta else (1,), (1,) if tb else (0,)), ((), ()))
    n_in = 2 + n_pro + n_epi
    n_out = max(1, n_epi) + (1 if save_left else 0)
    assert not save_left or (pro and tn == N)

    def body(*refs):
        ins, outs, acc = refs[:n_in], refs[n_in:n_in + n_out], refs[n_in + n_out:]
        b_ref = ins[1 + n_pro]
        av = (pro[0](*[r[...] for r in ins[:1 + n_pro]]) if pro else ins[0][...]).astype(BF16)
        if save_left:
            outs[-1][...] = av
        bv = b_ref[...].astype(BF16)
        part = lax.dot_general(av, bv, dims, preferred_element_type=F32)

        def finish(total):
            if epi:
                for o_ref, r in zip(outs, epi[0](total, *[e[...] for e in ins[2 + n_pro:]])):
                    o_ref[...] = r.astype(o_ref.dtype)
            else:
                outs[0][...] = total.astype(outs[0].dtype)

        if nk == 1:
            finish(part)
        else:
            acc_ref = acc[0]
            k = pl.program_id(2)

            @pl.when(k == 0)
            def _():
                acc_ref[...] = part

            @pl.when(k > 0)
            def _():
                acc_ref[...] += part

            @pl.when(k == nk - 1)
            def _():
                finish(acc_ref[...])

    a_spec = pl.BlockSpec((tk, tm), lambda i, j, k: (k, i)) if ta else pl.BlockSpec((tm, tk), lambda i, j, k: (i, k))
    if b_blocks and tb:
        r = k_unit // tk
        b_spec = pl.BlockSpec((None, tn, tk), lambda i, j, k: (k // r, j, k % r))
    elif b_blocks:
        r = n_unit // tn
        b_spec = pl.BlockSpec((None, tk, tn), lambda i, j, k: (j // r, k, j % r))
    elif tb:
        b_spec = pl.BlockSpec((tn, tk), lambda i, j, k: (j, k))
    else:
        b_spec = pl.BlockSpec((tk, tn), lambda i, j, k: (k, j))
    if out_blocks:
        r = n_unit // tn
        out_spec = pl.BlockSpec((None, tm, tn), lambda i, j, k: (j // r, i, j % r))
        out_shape = jax.ShapeDtypeStruct((out_blocks, M, n_unit), out_dtype)
    else:
        out_spec = pl.BlockSpec((tm, tn), lambda i, j, k: (i, j))
        out_shape = jax.ShapeDtypeStruct((M, N), out_dtype)
    in_specs = [a_spec] * (1 + n_pro) + [b_spec] + [out_spec] * n_epi
    args = [a] + (list(pro[1:]) if pro else []) + [b] + (list(epi[1:]) if epi else [])
    out_specs, out_shapes = ([out_spec] * n_epi, [out_shape] * n_epi) if epi else (out_spec, out_shape)
    if save_left:
        out_specs = [out_spec] * max(1, n_epi) + [a_spec]
        out_shapes = [out_shape] * max(1, n_epi) + [jax.ShapeDtypeStruct(a.shape, BF16)]
    return _pcall(
        body,
        name=name,
        grid=(M // tm, N // tn, nk),
        in_specs=in_specs,
        out_specs=out_specs,
        out_shape=out_shapes,
        scratch_shapes=[pltpu.VMEM((tm, tn), F32)] if nk > 1 else [],
        compiler_params=_params(("parallel", "parallel", "arbitrary")),
    )(*args)


def _bd_call(a, bc, transpose, name):
    m = a.shape[0]
    nb, ka, kb = bc.shape
    win, wout = (kb, ka) if transpose else (ka, kb)
    assert a.shape[1] == nb * win
    tm = _tile(m, 1024)
    dims = _NT if transpose else (((1,), (0,)), ((), ()))

    def body(a_ref, b_ref, o_ref):
        o_ref[...] = lax.dot_general(a_ref[...].astype(BF16), b_ref[0].astype(BF16), dims, preferred_element_type=F32)

    return _pcall(
        body,
        name=name,
        grid=(m // tm, nb),
        in_specs=[pl.BlockSpec((tm, win), lambda i, j: (i, j)), pl.BlockSpec((1, ka, kb), lambda i, j: (j, 0, 0))],
        out_specs=pl.BlockSpec((tm, wout), lambda i, j: (i, j)),
        out_shape=jax.ShapeDtypeStruct((m, nb * wout), F32),
        compiler_params=_params(("parallel", "parallel")),
    )(a, bc)


def _bd_dw_call(a, g, nb, name):
    m = a.shape[0]
    ka, kb = a.shape[1] // nb, g.shape[1] // nb
    tk = _tile(m, 1024)
    nk = m // tk

    def body(a_ref, g_ref, o_ref):
        part = lax.dot_general(a_ref[...].astype(BF16), g_ref[...].astype(BF16), _TN, preferred_element_type=F32)
        k = pl.program_id(1)

        @pl.when(k == 0)
        def _():
            o_ref[0] = part

        @pl.when(k > 0)
        def _():
            o_ref[0] += part

    return _pcall(
        body,
        name=name,
        grid=(nb, nk),
        in_specs=[pl.BlockSpec((tk, ka), lambda j, k: (k, j)), pl.BlockSpec((tk, kb), lambda j, k: (k, j))],
        out_specs=pl.BlockSpec((1, ka, kb), lambda j, k: (j, 0, 0)),
        out_shape=jax.ShapeDtypeStruct((nb, ka, kb), F32),
        compiler_params=_params(("parallel", "arbitrary")),
    )(a, g)


@functools.partial(jax.custom_vjp, nondiff_argnums=(2,))
def bd_matmul(a, bc, name):
    return _bd_call(a, bc, False, name + "_f")


def _bd_matmul_fwd(a, bc, name):
    return _bd_call(a, bc, False, name + "_f"), (a, bc)


def _bd_matmul_bwd(name, res, g):
    a, bc = res
    return _bd_call(g, bc, True, name + "_da"), _bd_dw_call(a, g, bc.shape[0], name + "_dw")


bd_matmul.defvjp(_bd_matmul_fwd, _bd_matmul_bwd)


@functools.partial(jax.custom_vjp, nondiff_argnums=(3,))
def matmul_w(a, w16, tok, name):
    return _mm(a, w16, b_blocks=w16.ndim == 3, name=name + "_f")


def _matmul_w_fwd(a, w16, tok, name):
    return _mm(a, w16, b_blocks=w16.ndim == 3, name=name + "_f"), (a, w16)


def _matmul_w_bwd(name, res, g):
    a, w16 = res
    blocks = w16.shape[0] if w16.ndim == 3 else 0
    return (_mm(g, w16, tb=True, b_blocks=bool(blocks), name=name + "_da"), jnp.zeros_like(w16),
            _mm(a, g, ta=True, out_blocks=blocks, name=name + "_dw"))


matmul_w.defvjp(_matmul_w_fwd, _matmul_w_bwd)


def _swiglu(gt, up):
    return gt * jax.nn.sigmoid(gt) * up


def _swiglu_bwd(dact, gt, up):
    sg = jax.nn.sigmoid(gt)
    return dact * up * sg * (1.0 + gt * (1.0 - sg)), dact * gt * sg


def _add_residual(product, x):
    return (product + x,)


def _silu_gate(o, g):
    return o * (g * jax.nn.sigmoid(g))


def _silu_gate_bwd(dm, o, g):
    sg = jax.nn.sigmoid(g)
    return dm * g * sg, dm * o * sg * (1.0 + g * (1.0 - sg))


@functools.partial(jax.custom_vjp, nondiff_argnums=(4,))
def silu_gated_out(o, g, w16, tok, name):
    return _mm(o, w16, pro=(_silu_gate, g), name=name + "_f")


def _silu_gated_out_fwd(o, g, w16, tok, name):
    out, left16 = _mm(o, w16, pro=(_silu_gate, g), save_left=True, name=name + "_f")
    return out, (o, g, w16, left16)


def _silu_gated_out_bwd(name, res, d):
    o, g, w16, left16 = res
    d_o, d_g = _mm(d, w16, tb=True, epi=(_silu_gate_bwd, o, g), name=name + "_da")
    return d_o, d_g, jnp.zeros_like(w16), _mm(left16, d, ta=True, name=name + "_dw")


silu_gated_out.defvjp(_silu_gated_out_fwd, _silu_gated_out_bwd)


@functools.partial(jax.custom_vjp, nondiff_argnums=(5,))
def swiglu_down(x, gt, up, w16, tok, name):
    return _mm(gt, w16, pro=(_swiglu, up), epi=(_add_residual, x), name=name + "_f")[0]


def _swiglu_down_fwd(x, gt, up, w16, tok, name):
    out, act16 = _mm(gt, w16, pro=(_swiglu, up), epi=(_add_residual, x), save_left=True, name=name + "_f")
    return out, (gt, up, w16, act16)


def _swiglu_down_bwd(name, res, g):
    gt, up, w16, act16 = res
    d_gt, d_up = _mm(g, w16, tb=True, epi=(_swiglu_bwd, gt, up), name=name + "_da")
    return g, d_gt, d_up, jnp.zeros_like(w16), _mm(act16, g, ta=True, name=name + "_dw")


swiglu_down.defvjp(_swiglu_down_fwd, _swiglu_down_bwd)


def _merge(y1, y2, y3, l1, l2, l3):
    return jax.nn.sigmoid(l1) * y1 + jax.nn.sigmoid(l2) * y2 + jax.nn.sigmoid(l3) * y3


def _merge_bwd(dm, y1, y2, y3, l1, l2, l3):
    dys, dls = [], []
    for y, logit in ((y1, l1), (y2, l2), (y3, l3)):
        s = jax.nn.sigmoid(logit)
        dys.append(dm * s)
        dls.append(dm * y * s * (1.0 - s))
    return (*dys, *dls)


@functools.partial(jax.custom_vjp, nondiff_argnums=(5,))
def gated_merge_out(x, ys, logits, w16, tok, name):
    return _mm(ys[0], w16, pro=(_merge, *ys[1:], *logits), epi=(_add_residual, x), name=name + "_f")[0]


def _gated_merge_out_fwd(x, ys, logits, w16, tok, name):
    out, merged16 = _mm(ys[0], w16, pro=(_merge, *ys[1:], *logits), epi=(_add_residual, x), save_left=True,
                        name=name + "_f")
    return out, (ys, logits, w16, merged16)


def _gated_merge_out_bwd(name, res, g):
    ys, logits, w16, merged16 = res
    d = _mm(g, w16, tb=True, epi=(_merge_bwd, *ys, *logits), name=name + "_da")
    return g, tuple(d[:3]), tuple(d[3:]), jnp.zeros_like(w16), _mm(merged16, g, ta=True, name=name + "_dw")


gated_merge_out.defvjp(_gated_merge_out_fwd, _gated_merge_out_bwd)


@functools.partial(jax.custom_vjp, nondiff_argnums=(3,))
def matmul_w_t(a, w16, tok, name):
    return _mm(w16, a, ta=True, tb=True, name=name + "_f")


def _matmul_w_t_fwd(a, w16, tok, name):
    return _mm(w16, a, ta=True, tb=True, name=name + "_f"), (a, w16)


def _matmul_w_t_bwd(name, res, g):
    a, w16 = res
    return (_mm(g, w16, ta=True, tb=True, name=name + "_da"), jnp.zeros_like(w16),
            _mm(a, g, ta=True, tb=True, name=name + "_dw"))


matmul_w_t.defvjp(_matmul_w_t_fwd, _matmul_w_t_bwd)


@functools.partial(jax.custom_vjp, nondiff_argnums=(3,))
def matmul_w_at(at, w16, tok, name):
    return _mm(at, w16, ta=True, name=name + "_f")


def _matmul_w_at_fwd(at, w16, tok, name):
    return _mm(at, w16, ta=True, name=name + "_f"), (at, w16)


def _matmul_w_at_bwd(name, res, g):
    at, w16 = res
    return _mm(w16, g, tb=True, name=name + "_da"), jnp.zeros_like(w16), _mm(at, g, name=name + "_dw")


matmul_w_at.defvjp(_matmul_w_at_fwd, _matmul_w_at_bwd)


@functools.partial(jax.custom_vjp, nondiff_argnums=(4,))
def matmul_w_at_res(x, at, w16, tok, name):
    return _mm(at, w16, ta=True, epi=(_add_residual, x), name=name + "_f")[0]


def _matmul_w_at_res_fwd(x, at, w16, tok, name):
    return _mm(at, w16, ta=True, epi=(_add_residual, x), name=name + "_f")[0], (at, w16)


def _matmul_w_at_res_bwd(name, res, g):
    return (g, *_matmul_w_at_bwd(name, res, g))


matmul_w_at_res.defvjp(_matmul_w_at_res_fwd, _matmul_w_at_res_bwd)


def _rms_fwd_call(x, g, name):
    rows, d = x.shape
    tr = _tile(rows, 512, 8)

    def body(x_ref, g_ref, y_ref):
        xv = x_ref[...]
        rstd = lax.rsqrt(jnp.mean(xv * xv, axis=-1, keepdims=True) + RMS_EPS)
        y_ref[...] = xv * rstd * g_ref[...]

    return _pcall(
        body,
        name=name,
        grid=(rows // tr,),
        in_specs=[pl.BlockSpec((tr, d), lambda i: (i, 0)), pl.BlockSpec((1, d), lambda i: (0, 0))],
        out_specs=pl.BlockSpec((tr, d), lambda i: (i, 0)),
        out_shape=jax.ShapeDtypeStruct((rows, d), F32),
        compiler_params=_params(("parallel",)),
    )(x, g.reshape(1, d))


def _rms_bwd_call(x, g, dy, name):
    rows, d = x.shape
    tr = _tile(rows, 512, 8)
    nb = rows // tr

    def body(x_ref, g_ref, dy_ref, dx_ref, dg_ref):
        xv = x_ref[...]
        rstd = lax.rsqrt(jnp.mean(xv * xv, axis=-1, keepdims=True) + RMS_EPS)
        xh = xv * rstd
        dyv = dy_ref[...]
        dyg = dyv * g_ref[...]
        dx_ref[...] = rstd * (dyg - xh * jnp.mean(dyg * xh, axis=-1, keepdims=True))
        dg_ref[...] = jnp.sum((dyv * xh).reshape(tr // 8, 8, d), axis=0)

    dx, dgp = _pcall(
        body,
        name=name,
        grid=(nb,),
        in_specs=[
            pl.BlockSpec((tr, d), lambda i: (i, 0)),
            pl.BlockSpec((1, d), lambda i: (0, 0)),
            pl.BlockSpec((tr, d), lambda i: (i, 0)),
        ],
        out_specs=[pl.BlockSpec((tr, d), lambda i: (i, 0)), pl.BlockSpec((8, d), lambda i: (i, 0))],
        out_shape=[jax.ShapeDtypeStruct((rows, d), F32), jax.ShapeDtypeStruct((nb * 8, d), F32)],
        compiler_params=_params(("parallel",)),
    )(x, g.reshape(1, d), dy)
    return dx, jnp.sum(dgp, axis=0)


@functools.partial(jax.custom_vjp, nondiff_argnums=(2,))
def rmsnorm(x, g, name):
    return _rms_fwd_call(x, g, name + "_f")


def _rmsnorm_fwd(x, g, name):
    return _rms_fwd_call(x, g, name + "_f"), (x, g)


def _rmsnorm_bwd(name, res, dy):
    x, g = res
    return _rms_bwd_call(x, g, dy, name + "_b")


rmsnorm.defvjp(_rmsnorm_fwd, _rmsnorm_bwd)


def _attn_blocks(sq, sk):
    return _tile(sq, 1024), _tile(sk, 512)


def _attn_pt(qs, k_ref, qi, ki, bq, bk, masked, shift):
    st = jnp.dot(k_ref[0].astype(BF16), qs, preferred_element_type=F32)
    if masked:
        kpos = ki * bk + lax.broadcasted_iota(jnp.int32, (bk, bq), 0)
        qpos = qi * bq + lax.broadcasted_iota(jnp.int32, (bk, bq), 1)
        st = jnp.where(kpos <= qpos, st, MASK_VALUE)
    return st if shift is None else jnp.exp(st - shift)


def _attn_pairs(nq, nk, bq, bk, causal, k_major):
    ok = lambda i, j: (not causal) or j * bk <= i * bq + bq - 1
    if k_major:
        pairs = [(i, j) for j in range(nk) for i in range(nq) if ok(i, j)]
    else:
        pairs = [(i, j) for i in range(nq) for j in range(nk) if ok(i, j)]
    qtab = jnp.asarray([p[0] for p in pairs], jnp.int32)
    ktab = jnp.asarray([p[1] for p in pairs], jnp.int32)
    first_q = (lambda j: (j * bk) // bq) if causal else (lambda j: 0)
    last_k = (lambda i: jnp.minimum((i * bq + bq - 1) // bk, nk - 1)) if causal else (lambda i: nk - 1)
    return qtab, ktab, first_q, last_k


def _attn_cases(causal, qi, ki, bq, bk, step):
    if not causal:
        step(False)
        return
    pl.when(ki * bk + bk - 1 <= qi * bq)(functools.partial(step, False))
    pl.when(ki * bk + bk - 1 > qi * bq)(functools.partial(step, True))


def _attn_fwd_call(qt, k, vt, scale, causal, name):
    h, sk, dq = k.shape
    sq = qt.shape[1]
    dv = vt.shape[0] // h
    bq, bk = _attn_blocks(sq, sk)
    nq, nk = sq // bq, sk // bk

    qtab, ktab, _, last_k = _attn_pairs(nq, nk, bq, bk, causal, False)

    def body(qi_ref, ki_ref, q_ref, k_ref, v_ref, o_ref, lse_ref, m_sc, l_sc, acc_sc):
        qi, ki = qi_ref[pl.program_id(1)], ki_ref[pl.program_id(1)]

        @pl.when(ki == 0)
        def _():
            m_sc[...] = jnp.full(m_sc.shape, -jnp.inf, F32)
            l_sc[...] = jnp.zeros(l_sc.shape, F32)
            acc_sc[...] = jnp.zeros(acc_sc.shape, F32)

        def step(masked):
            qs = (q_ref[...] * scale).astype(BF16)
            st = _attn_pt(qs, k_ref, qi, ki, bq, bk, masked, None)
            m_prev = m_sc[...]
            m_new = jnp.maximum(m_prev, jnp.max(st, axis=0, keepdims=True))
            alpha = jnp.exp(m_prev - m_new)
            pt = jnp.exp(st - m_new)
            l_sc[...] = alpha * l_sc[...] + jnp.sum(pt, axis=0, keepdims=True)
            acc_sc[...] = alpha * acc_sc[...] + jnp.dot(v_ref[...].astype(BF16), pt.astype(BF16),
                                                        preferred_element_type=F32)
            m_sc[...] = m_new

        _attn_cases(causal, qi, ki, bq, bk, step)

        @pl.when(ki == last_k(qi))
        def _():
            o_ref[...] = acc_sc[...] / l_sc[...]
            lse_ref[0] = m_sc[...] + jnp.log(l_sc[...])

    qmap = lambda hh, p, qi_ref, ki_ref: (hh, qi_ref[p])
    return _pcall(
        body,
        name=name,
        grid_spec=pltpu.PrefetchScalarGridSpec(
            num_scalar_prefetch=2,
            grid=(h, qtab.shape[0]),
            in_specs=[
                pl.BlockSpec((dq, bq), qmap),
                pl.BlockSpec((1, bk, dq), lambda hh, p, qi_ref, ki_ref: (hh, ki_ref[p], 0)),
                pl.BlockSpec((dv, bk), lambda hh, p, qi_ref, ki_ref: (hh, ki_ref[p])),
            ],
            out_specs=[pl.BlockSpec((dv, bq), qmap),
                       pl.BlockSpec((1, 1, bq), lambda hh, p, qi_ref, ki_ref: (hh, 0, qi_ref[p]))],
            scratch_shapes=[pltpu.VMEM((1, bq), F32), pltpu.VMEM((1, bq), F32), pltpu.VMEM((dv, bq), F32)],
        ),
        out_shape=[jax.ShapeDtypeStruct((h * dv, sq), F32), jax.ShapeDtypeStruct((h, 1, sq), F32)],
        compiler_params=_params(("parallel", "arbitrary")),
    )(qtab, ktab, qt, k, vt)


_TN = (((0,), (0,)), ((), ()))
_NT = (((1,), (1,)), ((), ()))


def _attn_dst(qs, k_ref, v_ref, do_ref, lse_ref, dl_ref, qi, ki, bq, bk, masked):
    pt = _attn_pt(qs, k_ref, qi, ki, bq, bk, masked, lse_ref[0])
    dpt = lax.dot_general(v_ref[...].astype(BF16), do_ref[...].astype(BF16), _TN, preferred_element_type=F32)
    return pt, pt * (dpt - dl_ref[0])


def _attn_dq_call(qt, k, vt, dot, lse, delta, scale, causal, name):
    h, sk, dq = k.shape
    sq = qt.shape[1]
    dv = vt.shape[0] // h
    bq, bk = _attn_blocks(sq, sk)
    nq, nk = sq // bq, sk // bk

    qtab, ktab, _, last_k = _attn_pairs(nq, nk, bq, bk, causal, False)

    def body(qi_ref, ki_ref, q_ref, k_ref, v_ref, do_ref, lse_ref, dl_ref, dq_ref, acc_sc):
        qi, ki = qi_ref[pl.program_id(1)], ki_ref[pl.program_id(1)]

        @pl.when(ki == 0)
        def _():
            acc_sc[...] = jnp.zeros(acc_sc.shape, F32)

        def step(masked):
            qs = (q_ref[...] * scale).astype(BF16)
            _, dst = _attn_dst(qs, k_ref, v_ref, do_ref, lse_ref, dl_ref, qi, ki, bq, bk, masked)
            acc_sc[...] += lax.dot_general(k_ref[0].astype(BF16), dst.astype(BF16), _TN, preferred_element_type=F32)

        _attn_cases(causal, qi, ki, bq, bk, step)

        @pl.when(ki == last_k(qi))
        def _():
            dq_ref[...] = acc_sc[...] * scale

    qmap = lambda hh, p, qi_ref, ki_ref: (hh, qi_ref[p])
    rowmap = lambda hh, p, qi_ref, ki_ref: (hh, 0, qi_ref[p])
    return _pcall(
        body,
        name=name,
        grid_spec=pltpu.PrefetchScalarGridSpec(
            num_scalar_prefetch=2,
            grid=(h, qtab.shape[0]),
            in_specs=[
                pl.BlockSpec((dq, bq), qmap),
                pl.BlockSpec((1, bk, dq), lambda hh, p, qi_ref, ki_ref: (hh, ki_ref[p], 0)),
                pl.BlockSpec((dv, bk), lambda hh, p, qi_ref, ki_ref: (hh, ki_ref[p])),
                pl.BlockSpec((dv, bq), qmap),
                pl.BlockSpec((1, 1, bq), rowmap),
                pl.BlockSpec((1, 1, bq), rowmap),
            ],
            out_specs=pl.BlockSpec((dq, bq), qmap),
            scratch_shapes=[pltpu.VMEM((dq, bq), F32)],
        ),
        out_shape=jax.ShapeDtypeStruct((h * dq, sq), F32),
        compiler_params=_params(("parallel", "arbitrary")),
    )(qtab, ktab, qt, k, vt, dot, lse, delta)


def _attn_dkv_call(qt, k, vt, dot, lse, delta, scale, causal, name):
    h, sk, dq = k.shape
    sq = qt.shape[1]
    dv = vt.shape[0] // h
    bq, bk = _attn_blocks(sq, sk)
    nq, nk = sq // bq, sk // bk

    qtab, ktab, first_q, _ = _attn_pairs(nq, nk, bq, bk, causal, True)

    def body(qi_ref, ki_ref, q_ref, k_ref, v_ref, do_ref, lse_ref, dl_ref, dk_ref, dv_ref, dk_sc, dv_sc):
        qi, ki = qi_ref[pl.program_id(1)], ki_ref[pl.program_id(1)]

        @pl.when(qi == first_q(ki))
        def _():
            dk_sc[...] = jnp.zeros(dk_sc.shape, F32)
            dv_sc[...] = jnp.zeros(dv_sc.shape, F32)

        def step(masked):
            qs = (q_ref[...] * scale).astype(BF16)
            pt, dst = _attn_dst(qs, k_ref, v_ref, do_ref, lse_ref, dl_ref, qi, ki, bq, bk, masked)
            dv_sc[...] += lax.dot_general(do_ref[...].astype(BF16), pt.astype(BF16), _NT, preferred_element_type=F32)
            dk_sc[...] += lax.dot_general(dst.astype(BF16), qs, _NT, preferred_element_type=F32)

        _attn_cases(causal, qi, ki, bq, bk, step)

        @pl.when(qi == nq - 1)
        def _():
            dk_ref[0] = dk_sc[...]
            dv_ref[...] = dv_sc[...]

    qmap = lambda hh, p, qi_ref, ki_ref: (hh, qi_ref[p])
    rowmap = lambda hh, p, qi_ref, ki_ref: (hh, 0, qi_ref[p])
    kmap = lambda hh, p, qi_ref, ki_ref: (hh, ki_ref[p], 0)
    vmap = lambda hh, p, qi_ref, ki_ref: (hh, ki_ref[p])
    return _pcall(
        body,
        name=name,
        grid_spec=pltpu.PrefetchScalarGridSpec(
            num_scalar_prefetch=2,
            grid=(h, qtab.shape[0]),
            in_specs=[
                pl.BlockSpec((dq, bq), qmap),
                pl.BlockSpec((1, bk, dq), kmap),
                pl.BlockSpec((dv, bk), vmap),
                pl.BlockSpec((dv, bq), qmap),
                pl.BlockSpec((1, 1, bq), rowmap),
                pl.BlockSpec((1, 1, bq), rowmap),
            ],
            out_specs=[pl.BlockSpec((1, bk, dq), kmap), pl.BlockSpec((dv, bk), vmap)],
            scratch_shapes=[pltpu.VMEM((bk, dq), F32), pltpu.VMEM((dv, bk), F32)],
        ),
        out_shape=[jax.ShapeDtypeStruct((h, sk, dq), F32), jax.ShapeDtypeStruct((h * dv, sk), F32)],
        compiler_params=_params(("parallel", "arbitrary")),
    )(qtab, ktab, qt, k, vt, dot, lse, delta)


@functools.partial(jax.custom_vjp, nondiff_argnums=(3, 4, 5))
def attention(qt, k, vt, scale, causal, name):
    return _attn_fwd_call(qt, k, vt, scale, causal, name + "_f")[0]


def _attention_fwd(qt, k, vt, scale, causal, name):
    ot, lse = _attn_fwd_call(qt, k, vt, scale, causal, name + "_f")
    return ot, (qt, k, vt, ot, lse)


def _attention_bwd(scale, causal, name, res, dot):
    qt, k, vt, ot, lse = res
    h = k.shape[0]
    delta = jnp.sum((dot * ot).reshape(h, ot.shape[0] // h, ot.shape[1]), axis=1, keepdims=True)
    dqt = _attn_dq_call(qt, k, vt, dot, lse, delta, scale, causal, name + "_dq")
    dk, dvt = _attn_dkv_call(qt, k, vt, dot, lse, delta, scale, causal, name + "_dkv")
    return dqt, dk, dvt


attention.defvjp(_attention_fwd, _attention_bwd)


SCAN_ROWS = 8
SCAN_LANES = 256
SCAN_TBLOCK = 512
SCAN_UNROLL = 4


def _scan_tables(lr, li):
    pr, pi = [lr], [li]
    for _ in range(SCAN_ROWS - 1):
        pr, pi = pr + [pr[-1] * lr - pi[-1] * li], pi + [pr[-1] * li + pi[-1] * lr]
    return jnp.concatenate(pr, axis=0), jnp.concatenate(pi, axis=0)


def _scan_call(ur, ui, lr, li, reverse, name, states=None):
    s, n = ur.shape
    tb = _tile(s, SCAN_TBLOCK, 8)
    nt, nl = s // tb, n // SCAN_LANES
    ntile = tb // SCAN_ROWS
    pr, pi = _scan_tables(lr, li)
    if reverse:
        pr, pi = pr[::-1], pi[::-1]
    shape = (SCAN_ROWS, SCAN_LANES)
    with_states = states is not None
    assert reverse or not with_states

    def body(*refs):
        if with_states:
            ur_ref, ui_ref, pr_ref, pi_ref, sr_ref, si_ref, xr_ref, xi_ref, dr_ref, di_ref, cr_sc, ci_sc, ar_sc, ai_sc = refs
        else:
            ur_ref, ui_ref, pr_ref, pi_ref, xr_ref, xi_ref, cr_sc, ci_sc, ar_sc, ai_sc = refs

        @pl.when(pl.program_id(1) == 0)
        def _():
            for sc in (cr_sc, ci_sc, ar_sc, ai_sc):
                sc[...] = jnp.zeros(shape, F32)

        prv, piv = pr_ref[...], pi_ref[...]
        rows = lax.broadcasted_iota(jnp.int32, shape, 0)

        def powers(k):
            r = (SCAN_ROWS - k) if reverse else (k - 1)
            return jnp.broadcast_to(prv[r:r + 1], shape), jnp.broadcast_to(piv[r:r + 1], shape)

        pw = [powers(k) for k in (1, 2, 4)]

        def tile(i, carry):
            cr, ci, acr, aci = carry
            j = (ntile - 1 - i) if reverse else i
            sl = pl.ds(pl.multiple_of(j * SCAN_ROWS, SCAN_ROWS), SCAN_ROWS)
            xr, xi = ur_ref[sl, :], ui_ref[sl, :]
            for (ar, ai), k in zip(pw, (1, 2, 4)):
                if reverse:
                    keep = rows < SCAN_ROWS - k
                    sr = jnp.where(keep, pltpu.roll(xr, SCAN_ROWS - k, 0), 0.0)
                    si = jnp.where(keep, pltpu.roll(xi, SCAN_ROWS - k, 0), 0.0)
                else:
                    keep = rows >= k
                    sr = jnp.where(keep, pltpu.roll(xr, k, 0), 0.0)
                    si = jnp.where(keep, pltpu.roll(xi, k, 0), 0.0)
                xr, xi = xr + (ar * sr - ai * si), xi + (ar * si + ai * sr)
            xr, xi = xr + (prv * cr - piv * ci), xi + (prv * ci + piv * cr)
            xr_ref[sl, :] = xr
            xi_ref[sl, :] = xi
            if with_states:
                last = rows == SCAN_ROWS - 1
                nr = jnp.where(last, cr, pltpu.roll(xr, SCAN_ROWS - 1, 0))
                ni = jnp.where(last, ci, pltpu.roll(xi, SCAN_ROWS - 1, 0))
                sr, si = sr_ref[sl, :], si_ref[sl, :]
                acr, aci = acr + (sr * nr + si * ni), aci + (sr * ni - si * nr)
            e = 0 if reverse else SCAN_ROWS - 1
            return jnp.broadcast_to(xr[e:e + 1], shape), jnp.broadcast_to(xi[e:e + 1], shape), acr, aci

        def tiles(g, carry):
            for u in range(SCAN_UNROLL):
                carry = tile(g * SCAN_UNROLL + u, carry)
            return carry

        assert ntile % SCAN_UNROLL == 0
        cr, ci, acr, aci = lax.fori_loop(0, ntile // SCAN_UNROLL, tiles,
                                         (cr_sc[...], ci_sc[...], ar_sc[...], ai_sc[...]))
        cr_sc[...] = cr
        ci_sc[...] = ci
        if with_states:
            ar_sc[...] = acr
            ai_sc[...] = aci

            @pl.when(pl.program_id(1) == nt - 1)
            def _():
                dr_ref[...] = acr
                di_ref[...] = aci

    tmap = (lambda l, t: (nt - 1 - t, l)) if reverse else (lambda l, t: (t, l))
    blk = pl.BlockSpec((tb, SCAN_LANES), tmap)
    tab = pl.BlockSpec((SCAN_ROWS, SCAN_LANES), lambda l, t: (0, l))
    full = jax.ShapeDtypeStruct((s, n), F32)
    return _pcall(
        body,
        name=name,
        grid=(nl, nt),
        in_specs=[blk, blk, tab, tab] + ([blk, blk] if with_states else []),
        out_specs=[blk, blk] + ([tab, tab] if with_states else []),
        out_shape=[full, full] + ([jax.ShapeDtypeStruct((SCAN_ROWS, n), F32)] * 2 if with_states else []),
        scratch_shapes=[pltpu.VMEM(shape, F32)] * 4,
        compiler_params=_params(("parallel", "arbitrary")),
    )(ur, ui, pr, pi, *(states if with_states else ()))


@functools.partial(jax.custom_vjp, nondiff_argnums=(4,))
def s5_scan(ur, ui, lr, li, name):
    return tuple(_scan_call(ur, ui, lr, li, False, name + "_f"))


def _s5_scan_fwd(ur, ui, lr, li, name):
    xr, xi = _scan_call(ur, ui, lr, li, False, name + "_f")
    return (xr, xi), (xr, xi, lr, li)


def _s5_scan_bwd(name, res, g):
    xr, xi, lr, li = res
    gr, gi, dlr, dli = _scan_call(g[0], g[1], lr, -li, True, name + "_b", states=(xr, xi))
    return gr, gi, jnp.sum(dlr, axis=0, keepdims=True), jnp.sum(dli, axis=0, keepdims=True)


s5_scan.defvjp(_s5_scan_fwd, _s5_scan_bwd)


def _running_sum(x, reverse):
    n = x.shape[0]
    rows = lax.broadcasted_iota(jnp.int32, x.shape, 0)
    k = 1
    while k < n:
        if reverse:
            x = x + jnp.where(rows < n - k, pltpu.roll(x, n - k, 0), 0.0)
        else:
            x = x + jnp.where(rows >= k, pltpu.roll(x, k, 0), 0.0)
        k *= 2
    return x


@jax.custom_vjp
def cumsum_rows(x):
    return _running_sum(x, False)


cumsum_rows.defvjp(lambda x: (_running_sum(x, False), None), lambda _, g: (_running_sum(g, True),))


def _hg_chunk(HG_SUB, q, k, v, lf, st):
    c = HG_CHUNK
    b = cumsum_rows(lf)
    qb = q * jnp.exp(b)
    o = lax.dot_general(qb.astype(BF16), st.astype(BF16), (((1,), (1,)), ((), ())), preferred_element_type=F32)
    nsub = c // HG_SUB
    srow = lax.broadcasted_iota(jnp.int32, (HG_SUB, HG_SUB, 1), 0)
    scol = lax.broadcasted_iota(jnp.int32, (HG_SUB, HG_SUB, 1), 1)
    smask = scol <= srow
    outs = []
    for i in range(nsub):
        lo = i * HG_SUB
        bi, qi, ki = b[lo:lo + HG_SUB], q[lo:lo + HG_SUB], k[lo:lo + HG_SUB]
        diff = jnp.where(smask, bi[:, None, :] - bi[None, :, :], 0.0)
        e = jnp.where(smask, jnp.exp(diff), 0.0)
        a_ii = jnp.sum(qi[:, None, :] * ki[None, :, :] * e, axis=-1)
        oi = jnp.dot(a_ii.astype(BF16), v[lo:lo + HG_SUB].astype(BF16), preferred_element_type=F32)
        if i > 0:
            r = b[lo - 1:lo]
            qt = qi * jnp.exp(bi - r)
            kt = k[:lo] * jnp.exp(r - b[:lo])
            a_ij = lax.dot_general(qt.astype(BF16), kt.astype(BF16), (((1,), (1,)), ((), ())),
                                   preferred_element_type=F32)
            oi = oi + jnp.dot(a_ij.astype(BF16), v[:lo].astype(BF16), preferred_element_type=F32)
        outs.append(oi)
    o = o + jnp.concatenate(outs, axis=0)
    bl = b[c - 1:c]
    kd = k * jnp.exp(bl - b)
    st_new = st * jnp.exp(bl) + lax.dot_general(v.astype(BF16), kd.astype(BF16), (((0,), (0,)), ((), ())),
                                                preferred_element_type=F32)
    return o, st_new


def _hg_fwd_call(q, k, v, lf, name):
    s, w = q.shape
    nc = s // HG_CHUNK
    blk = pl.BlockSpec((HG_CHUNK, w), lambda i: (i, 0))

    def body(q_ref, k_ref, v_ref, lf_ref, o_ref, st_ref, st_sc):
        @pl.when(pl.program_id(0) == 0)
        def _():
            st_sc[...] = jnp.zeros(st_sc.shape, F32)

        for hh in range(HG_HEADS):
            cs = slice(hh * HG_DK, (hh + 1) * HG_DK)
            st = st_sc[hh]
            st_ref[0, hh] = st
            o, st_new = _hg_chunk(HG_SUB_FWD, q_ref[:, cs], k_ref[:, cs], v_ref[:, cs], lf_ref[:, cs], st)
            o_ref[:, cs] = o
            st_sc[hh] = st_new

    return _pcall(
        body,
        name=name,
        grid=(nc,),
        in_specs=[blk, blk, blk, blk],
        out_specs=[blk, pl.BlockSpec((1, HG_HEADS, HG_DV, HG_DK), lambda i: (i, 0, 0, 0))],
        out_shape=[jax.ShapeDtypeStruct((s, w), F32), jax.ShapeDtypeStruct((nc, HG_HEADS, HG_DV, HG_DK), F32)],
        scratch_shapes=[pltpu.VMEM((HG_HEADS, HG_DV, HG_DK), F32)],
        compiler_params=_params(("arbitrary",)),
    )(q, k, v, lf)


def _hg_bwd_call(q, k, v, lf, sts, do, name):
    s, w = q.shape
    nc = s // HG_CHUNK
    blk = pl.BlockSpec((HG_CHUNK, w), lambda i: (nc - 1 - i, 0))

    def body(q_ref, k_ref, v_ref, lf_ref, st_ref, do_ref, dq_ref, dk_ref, dv_ref, dlf_ref, dst_sc):
        @pl.when(pl.program_id(0) == 0)
        def _():
            dst_sc[...] = jnp.zeros(dst_sc.shape, F32)

        def head(hh, carry):
            cs = pl.ds(pl.multiple_of(hh * HG_DK, HG_DK), HG_DK)
            _, vjp = jax.vjp(functools.partial(_hg_chunk, HG_SUB_BWD), q_ref[:, cs], k_ref[:, cs], v_ref[:, cs],
                             lf_ref[:, cs], st_ref[0, hh])
            dq, dk, dv, dlf, dst = vjp((do_ref[:, cs], dst_sc[hh]))
            dq_ref[:, cs] = dq
            dk_ref[:, cs] = dk
            dv_ref[:, cs] = dv
            dlf_ref[:, cs] = dlf
            dst_sc[hh] = dst
            return carry

        lax.fori_loop(0, HG_HEADS, head, 0)

    out = jax.ShapeDtypeStruct((s, w), F32)
    return _pcall(
        body,
        name=name,
        grid=(nc,),
        in_specs=[blk, blk, blk, blk, pl.BlockSpec((1, HG_HEADS, HG_DV, HG_DK), lambda i: (nc - 1 - i, 0, 0, 0)), blk],
        out_specs=[blk, blk, blk, blk],
        out_shape=[out, out, out, out],
        scratch_shapes=[pltpu.VMEM((HG_HEADS, HG_DV, HG_DK), F32)],
        compiler_params=_params(("arbitrary",)),
    )(q, k, v, lf, sts, do)


@functools.partial(jax.custom_vjp, nondiff_argnums=(4,))
def hgrn_core(q, k, v, lf, name):
    return _hg_fwd_call(q, k, v, lf, name + "_f")[0]


def _hgrn_core_fwd(q, k, v, lf, name):
    o, sts = _hg_fwd_call(q, k, v, lf, name + "_f")
    return o, (q, k, v, lf, sts)


def _hgrn_core_bwd(name, res, do):
    q, k, v, lf, sts = res
    return tuple(_hg_bwd_call(q, k, v, lf, sts, do, name + "_b"))


hgrn_core.defvjp(_hgrn_core_fwd, _hgrn_core_bwd)


def _adamw_call(w, g, m, v, name):
    rows, cols = w.shape
    tr = _tile(rows, 256, 8)
    c1 = 1.0 / (1.0 - ADAM_B1 ** ADAM_STEP)
    c2 = 1.0 / (1.0 - ADAM_B2 ** ADAM_STEP)

    def body(w_ref, g_ref, m_ref, v_ref, d_ref, mo_ref, vo_ref):
        gv = g_ref[...]
        mn = ADAM_B1 * m_ref[...] + (1.0 - ADAM_B1) * gv
        vn = ADAM_B2 * v_ref[...] + (1.0 - ADAM_B2) * (gv * gv)
        d_ref[...] = -ADAM_LR * ((mn * c1) / (jnp.sqrt(vn * c2) + ADAM_EPS) + ADAM_WD * w_ref[...])
        mo_ref[...] = mn
        vo_ref[...] = vn

    blk = pl.BlockSpec((tr, cols), lambda i: (i, 0))
    out = jax.ShapeDtypeStruct((rows, cols), F32)
    return _pcall(
        body,
        name=name,
        grid=(rows // tr,),
        in_specs=[blk, blk, blk, blk],
        out_specs=[blk, blk, blk],
        out_shape=[out, out, out],
        compiler_params=_params(("parallel",)),
    )(w, g, m, v)


def _adamw_layer(w, g, m, v, layer, into, name):
    rows, cols = g.shape
    tr = _tile(rows, 256, 8)
    nb = rows // tr
    c1 = 1.0 / (1.0 - ADAM_B1 ** ADAM_STEP)
    c2 = 1.0 / (1.0 - ADAM_B2 ** ADAM_STEP)

    def body(w_ref, g_ref, m_ref, v_ref, *rest):
        go_ref, d_ref, mo_ref, vo_ref = rest[-4:]
        gv = g_ref[...]
        mn = ADAM_B1 * m_ref[...] + (1.0 - ADAM_B1) * gv
        vn = ADAM_B2 * v_ref[...] + (1.0 - ADAM_B2) * (gv * gv)
        go_ref[...] = gv
        d_ref[...] = -ADAM_LR * ((mn * c1) / (jnp.sqrt(vn * c2) + ADAM_EPS) + ADAM_WD * w_ref[...])
        mo_ref[...] = mn
        vo_ref[...] = vn

    mine = pl.BlockSpec((tr, cols), lambda i: (layer * nb + i, 0))
    out = jax.ShapeDtypeStruct(w.shape, F32)
    n_into = 4 if into is not None else 0
    return _pcall(
        body,
        name=name,
        grid=(nb,),
        in_specs=[mine, pl.BlockSpec((tr, cols), lambda i: (i, 0)), mine, mine] + [ANY] * n_into,
        out_specs=[mine] * 4,
        out_shape=[out] * 4,
        input_output_aliases={4 + k: k for k in range(n_into)},
        compiler_params=_params(("parallel",)),
    )(w, g, m, v, *(into or ()))


ANY = pl.BlockSpec(memory_space=pl.ANY)


def _place():
    return lax.axis_index("x"), lax.axis_index("y"), lax.axis_index("c")


def _other_chips(x, y):
    return [(1 - x, y), (x, 1 - y), (1 - x, 1 - y)]


def _gather_weights(shards):
    nw = len(shards)

    def body(*refs):
        w_refs, g_refs, (send_sems, recv_sems) = refs[:nw], refs[nw:2 * nw], refs[2 * nw:]
        x, y, c = _place()
        myq = 2 * x + y
        sibling = (x, y, 1 - c)
        chips = _other_chips(x, y)

        def copy(k, src, dst, to):
            return pltpu.make_async_remote_copy(src_ref=src, dst_ref=dst, send_sem=send_sems.at[k],
                                                recv_sem=recv_sems.at[k], device_id=to, device_id_type=MESH)

        first = []
        for i in range(nw):
            for j, (px, py) in enumerate(chips):
                first.append(copy(6 * i + j, w_refs[i].at[c], g_refs[i].at[myq, c], (px, py, c)))
                first[-1].start()
        passed = []
        for i in range(nw):
            for j, (px, py) in enumerate(chips):
                q = 2 * px + py
                copy(6 * i + j, w_refs[i].at[c], g_refs[i].at[q, c], (px, py, c)).wait_recv()
                passed.append(copy(6 * i + 3 + j, g_refs[i].at[q, c], g_refs[i].at[q, c], sibling))
                passed[-1].start()
        for i in range(nw):
            for j, (px, py) in enumerate(chips):
                q = 2 * px + py
                copy(6 * i + 3 + j, g_refs[i].at[q, 1 - c], g_refs[i].at[q, 1 - c], sibling).wait_recv()
        for cp in first + passed:
            cp.wait_send()

    got = _pcall(
        body,
        name="gather_weights",
        in_specs=[ANY] * nw,
        out_specs=[ANY] * nw,
        out_shape=[jax.ShapeDtypeStruct((N_CHIPS,) + s.shape, s.dtype) for s in shards],
        scratch_shapes=[pltpu.SemaphoreType.DMA((6 * nw,)), pltpu.SemaphoreType.DMA((6 * nw,))],
    )(*shards)
    my_chip = 2 * lax.axis_index("x") + lax.axis_index("y")
    return [lax.dynamic_update_slice(g, s[None], (my_chip, 0, 0, 0)) for g, s in zip(got, shards)]


HBM = pl.BlockSpec(memory_space=pltpu.HBM)
SEM = pl.BlockSpec(memory_space=pltpu.SEMAPHORE)


def _gather_start(shards, after):
    nw = len(shards)

    def body(*refs):
        w_refs, land_refs = refs[:nw], refs[nw:2 * nw]
        send_sems, recv_sems = refs[2 * nw + 1:2 * nw + 3]
        token = refs[-1]
        x, y, c = _place()
        myq = 2 * x + y
        for i in range(nw):
            hh = shards[i].shape[0] // 2
            rows = pl.ds(c * hh, hh)
            for j, (px, py) in enumerate(_other_chips(x, y)):
                for tc in range(2):
                    pltpu.make_async_remote_copy(
                        src_ref=w_refs[i].at[rows], dst_ref=land_refs[i].at[myq, rows],
                        send_sem=send_sems.at[6 * i + 2 * j + tc], recv_sem=recv_sems.at[6 * i + 2 * j + c],
                        device_id=(px, py, tc), device_id_type=MESH).start()
        token[...] = jnp.zeros_like(token)

    lands = [lax.empty((N_CHIPS,) + s.shape, s.dtype) for s in shards]
    out = _pcall(
        body,
        name="gather_start",
        in_specs=[HBM] * (2 * nw) + [pl.BlockSpec(memory_space=pl.ANY)],
        out_specs=[SEM, SEM] + [HBM] * (2 * nw) + [pl.BlockSpec(memory_space=pltpu.VMEM)],
        out_shape=[pltpu.SemaphoreType.DMA((6 * nw,)), pltpu.SemaphoreType.DMA((6 * nw,))]
        + [pltpu.HBM(s.shape, s.dtype) for s in shards] + [pltpu.HBM(t.shape, t.dtype) for t in lands]
        + [jax.ShapeDtypeStruct((8, 128), F32)],
        input_output_aliases={i: 2 + i for i in range(2 * nw)},
        compiler_params=pltpu.CompilerParams(has_side_effects=pltpu.SideEffectType.DATAFLOW_SIDE_EFFECTING),
    )(*[pltpu.with_memory_space_constraint(t, pltpu.HBM) for t in list(shards) + lands], after)
    return out[0], out[1], out[2:2 + nw], out[2 + nw:2 + 2 * nw], out[-1]


def _gather_wait(send_sems, recv_sems, shards, lands, after):
    nw = len(shards)

    def body(*refs):
        w_refs, land_refs = refs[:nw], refs[nw:2 * nw]
        send_sems, recv_sems = refs[2 * nw:2 * nw + 2]
        x, y, c = _place()
        myq = 2 * x + y
        for i in range(nw):
            hh = shards[i].shape[0] // 2
            for j, (px, py) in enumerate(_other_chips(x, y)):
                q = 2 * px + py
                for other in range(2):
                    mine = pl.ds(c * hh, hh)
                    theirs = pl.ds(other * hh, hh)
                    pltpu.make_async_remote_copy(
                        src_ref=w_refs[i].at[mine], dst_ref=land_refs[i].at[myq, mine],
                        send_sem=send_sems.at[6 * i + 2 * j + other], recv_sem=recv_sems.at[6 * i + 2 * j + other],
                        device_id=(px, py, other), device_id_type=MESH).wait_send()
                    pltpu.make_async_remote_copy(
                        src_ref=w_refs[i].at[theirs], dst_ref=land_refs[i].at[q, theirs],
                        send_sem=send_sems.at[6 * i + 2 * j + other], recv_sem=recv_sems.at[6 * i + 2 * j + other],
                        device_id=(px, py, other), device_id_type=MESH).wait_recv()

    out = _pcall(
        body,
        name="gather_wait",
        in_specs=[HBM] * (2 * nw) + [SEM, SEM, pl.BlockSpec(memory_space=pl.ANY)],
        out_specs=[HBM] * (2 * nw),
        out_shape=[pltpu.HBM(s.shape, s.dtype) for s in shards] + [pltpu.HBM(t.shape, t.dtype) for t in lands],
        input_output_aliases={i: i for i in range(2 * nw)},
        compiler_params=pltpu.CompilerParams(has_side_effects=pltpu.SideEffectType.DATAFLOW_SIDE_EFFECTING),
    )(*shards, *lands, send_sems, recv_sems, after)
    my_chip = 2 * lax.axis_index("x") + lax.axis_index("y")
    return [lax.dynamic_update_slice(g, s[None], (my_chip, 0, 0)) for g, s in zip(out[nw:], out[:nw])]


def _swap_halves_to_sibling(grs, tag=""):
    nw = len(grs)

    def body(*refs):
        g_refs, t_refs, (send_sems, recv_sems) = refs[:nw], refs[nw:2 * nw], refs[2 * nw:]
        x, y, c = _place()
        cps = []
        for i in range(nw):
            hh = grs[i].shape[1] // 2
            cps.append(pltpu.make_async_remote_copy(
                src_ref=g_refs[i].at[:, pl.ds((1 - c) * hh, hh)], dst_ref=t_refs[i], send_sem=send_sems.at[i],
                recv_sem=recv_sems.at[i], device_id=(x, y, 1 - c), device_id_type=MESH))
            cps[-1].start()
        for cp in cps:
            cp.wait()

    return _pcall(
        body,
        name="grad_d2d" + tag,
        in_specs=[ANY] * nw,
        out_specs=[ANY] * nw,
        out_shape=[jax.ShapeDtypeStruct((g.shape[0], g.shape[1] // 2, g.shape[2]), g.dtype) for g in grs],
        scratch_shapes=[pltpu.SemaphoreType.DMA((nw,)), pltpu.SemaphoreType.DMA((nw,))],
    )(*grs)


def _add_half(gr, t1, place, name):
    nq, r, wd = gr.shape
    hh = r // 2
    tr = _tile(hh, 512, 16)
    nb = hh // tr

    def body(p_ref, g_ref, t_ref, o_ref):
        o_ref[...] = (g_ref[...] + t_ref[...]).astype(BF16)

    return _pcall(
        body,
        name=name,
        grid_spec=pltpu.PrefetchScalarGridSpec(
            num_scalar_prefetch=1,
            grid=(nq, nb),
            in_specs=[
                pl.BlockSpec((1, tr, wd), lambda q, i, p_ref: (q, p_ref[0] * nb + i, 0)),
                pl.BlockSpec((1, tr, wd), lambda q, i, p_ref: (q, i, 0)),
            ],
            out_specs=pl.BlockSpec((1, tr, wd), lambda q, i, p_ref: (q, i, 0)),
        ),
        out_shape=jax.ShapeDtypeStruct((nq, hh, wd), BF16),
        compiler_params=_params(("parallel", "parallel")),
    )(place, gr, t1)


def _scatter_copies(p_refs, t_refs, send_sems, recv_sems):
    x, y, c = _place()
    return [pltpu.make_async_remote_copy(
        src_ref=p_refs[i].at[2 * px + py], dst_ref=t_refs[i].at[j], send_sem=send_sems.at[3 * i + j],
        recv_sem=recv_sems.at[3 * i + j], device_id=(px, py, c), device_id_type=MESH)
        for i in range(len(p_refs)) for j, (px, py) in enumerate(_other_chips(x, y))]


def _scatter_start(ps, after, tag=""):
    nw = len(ps)

    def body(*refs):
        send_sems, recv_sems = refs[2 * nw + 1:2 * nw + 3]
        for cp in _scatter_copies(refs[:nw], refs[nw:2 * nw], send_sems, recv_sems):
            cp.start()
        refs[-1][...] = jnp.zeros_like(refs[-1])

    lands = [lax.empty((3,) + p.shape[1:], p.dtype) for p in ps]
    out = _pcall(
        body,
        name="scatter_start" + tag,
        in_specs=[HBM] * (2 * nw) + [pl.BlockSpec(memory_space=pl.ANY)],
        out_specs=[SEM, SEM] + [HBM] * (2 * nw) + [pl.BlockSpec(memory_space=pltpu.VMEM)],
        out_shape=[pltpu.SemaphoreType.DMA((3 * nw,)), pltpu.SemaphoreType.DMA((3 * nw,))]
        + [pltpu.HBM(t.shape, t.dtype) for t in list(ps) + lands] + [jax.ShapeDtypeStruct((8, 128), F32)],
        input_output_aliases={i: 2 + i for i in range(2 * nw)},
        compiler_params=pltpu.CompilerParams(has_side_effects=pltpu.SideEffectType.DATAFLOW_SIDE_EFFECTING),
    )(*[pltpu.with_memory_space_constraint(t, pltpu.HBM) for t in list(ps) + lands], after)
    return out[0], out[1], out[2:2 + nw], out[2 + nw:2 + 2 * nw], out[-1]


def _scatter_wait(send_sems, recv_sems, ps, lands, after, tag=""):
    nw = len(ps)

    def body(*refs):
        for cp in _scatter_copies(refs[:nw], refs[nw:2 * nw], refs[2 * nw], refs[2 * nw + 1]):
            cp.wait_send()
            cp.wait_recv()

    out = _pcall(
        body,
        name="scatter_wait" + tag,
        in_specs=[HBM] * (2 * nw) + [SEM, SEM, pl.BlockSpec(memory_space=pl.ANY)],
        out_specs=[HBM] * (2 * nw),
        out_shape=[pltpu.HBM(t.shape, t.dtype) for t in list(ps) + list(lands)],
        input_output_aliases={i: i for i in range(2 * nw)},
        compiler_params=pltpu.CompilerParams(has_side_effects=pltpu.SideEffectType.DATAFLOW_SIDE_EFFECTING),
    )(*ps, *lands, send_sems, recv_sems, after)
    return out[nw:]


def _add_partials(gr, t1, t2, place, name):
    nq, r, wd = gr.shape
    hh = r // 2
    tr = _tile(hh, 512, 16)
    nb = hh // tr

    def body(p_ref, g_ref, s_ref, t_ref, o_ref):
        o_ref[...] = (((g_ref[0] + s_ref[0]) + t_ref[0].astype(F32)) + t_ref[1].astype(F32)) + t_ref[2].astype(F32)

    return _pcall(
        body,
        name=name,
        grid_spec=pltpu.PrefetchScalarGridSpec(
            num_scalar_prefetch=1,
            grid=(nb,),
            in_specs=[
                pl.BlockSpec((1, tr, wd), lambda i, p_ref: (p_ref[1], p_ref[0] * nb + i, 0)),
                pl.BlockSpec((1, tr, wd), lambda i, p_ref: (p_ref[1], i, 0)),
                pl.BlockSpec((3, tr, wd), lambda i, p_ref: (0, i, 0)),
            ],
            out_specs=pl.BlockSpec((tr, wd), lambda i, p_ref: (i, 0)),
        ),
        out_shape=jax.ShapeDtypeStruct((hh, wd), F32),
        compiler_params=_params(("parallel",)),
    )(place, gr, t1, t2)


def _join_halves(halves, tag=""):
    nw = len(halves)

    def body(*refs):
        h_refs, f_refs, (send_sems, recv_sems) = refs[:nw], refs[nw:2 * nw], refs[2 * nw:]
        x, y, c = _place()

        def copy(i, rows):
            hh = halves[i].shape[0]
            return pltpu.make_async_remote_copy(src_ref=h_refs[i], dst_ref=f_refs[i].at[pl.ds(rows * hh, hh)],
                                                send_sem=send_sems.at[i], recv_sem=recv_sems.at[i],
                                                device_id=(x, y, 1 - c), device_id_type=MESH)

        for i in range(nw):
            copy(i, c).start()
        for i in range(nw):
            copy(i, c).wait_send()
            copy(i, 1 - c).wait_recv()

    got = _pcall(
        body,
        name="grad_join" + tag,
        in_specs=[ANY] * nw,
        out_specs=[ANY] * nw,
        out_shape=[jax.ShapeDtypeStruct((2 * h.shape[0], h.shape[1]), h.dtype) for h in halves],
        scratch_shapes=[pltpu.SemaphoreType.DMA((nw,)), pltpu.SemaphoreType.DMA((nw,))],
    )(*halves)
    c = lax.axis_index("c")
    return [lax.dynamic_update_slice(g, h, (c * h.shape[0], 0)) for g, h in zip(got, halves)]


N_DEV = 8


def _allreduce_small(buf):
    rows, wd = buf.shape
    n = rows // N_DEV

    def body(x_ref, o_ref, stage_ref, send_sems, recv_sems):
        x, y, c = _place()
        me = 4 * x + 2 * y + c

        def peer(k):
            return (x ^ (k >> 2), y ^ ((k >> 1) & 1), c ^ (k & 1))

        def chunk(ref, d):
            return ref.at[pl.ds(pl.multiple_of(d * n, 8), n)]

        first = []
        for k in range(1, N_DEV):
            px, py, pc = peer(k)
            cp = pltpu.make_async_remote_copy(src_ref=chunk(x_ref, 4 * px + 2 * py + pc), dst_ref=stage_ref.at[k],
                                              send_sem=send_sems.at[k - 1], recv_sem=recv_sems.at[k - 1],
                                              device_id=(px, py, pc), device_id_type=MESH)
            cp.start()
            first.append(cp)
        acc = chunk(x_ref, me)[...]
        for k in range(1, N_DEV):
            first[k - 1].wait_recv()
            acc = acc + stage_ref[k]
        chunk(o_ref, me)[...] = acc
        second = []
        for k in range(1, N_DEV):
            cp = pltpu.make_async_remote_copy(src_ref=chunk(o_ref, me), dst_ref=chunk(o_ref, me),
                                              send_sem=send_sems.at[6 + k], recv_sem=recv_sems.at[6 + k],
                                              device_id=peer(k), device_id_type=MESH)
            cp.start()
            second.append(cp)
        for k in range(1, N_DEV):
            px, py, pc = peer(k)
            them = 4 * px + 2 * py + pc
            pltpu.make_async_remote_copy(src_ref=chunk(o_ref, them), dst_ref=chunk(o_ref, them),
                                         send_sem=send_sems.at[6 + k], recv_sem=recv_sems.at[6 + k],
                                         device_id=(px, py, pc), device_id_type=MESH).wait_recv()
        for cp in first + second:
            cp.wait_send()

    vmem = pl.BlockSpec(memory_space=pltpu.VMEM)
    return _pcall(
        body,
        name="allreduce_small",
        in_specs=[vmem],
        out_specs=vmem,
        out_shape=jax.ShapeDtypeStruct((rows, wd), F32),
        scratch_shapes=[pltpu.VMEM((N_DEV, n, wd), F32), pltpu.SemaphoreType.DMA((14,)),
                        pltpu.SemaphoreType.DMA((14,))],
    )(buf)


@functools.partial(jax.custom_vjp, nondiff_argnums=(1,))
def split_cols(z, spec):
    outs = []
    for segs in spec[0]:
        parts = [z[:, s:s + n] for s, n in segs]
        outs.append(parts[0] if len(parts) == 1 else jnp.concatenate(parts, axis=1))
    return tuple(outs)


def _split_cols_fwd(z, spec):
    return split_cols(z, spec), None


def _split_cols_bwd(spec, _, gs):
    pieces, width = spec
    segs = []
    for p, piece in enumerate(pieces):
        off = 0
        for s, n in piece:
            segs.append((s, n, p, off))
            off += n
    rows, parts, pos = gs[0].shape[0], [], 0
    for s, n, p, off in sorted(segs):
        if s > pos:
            parts.append(jnp.zeros((rows, s - pos), gs[0].dtype))
        parts.append(gs[p] if (off == 0 and n == gs[p].shape[1]) else gs[p][:, off:off + n])
        pos = s + n
    if pos < width:
        parts.append(jnp.zeros((rows, width - pos), gs[0].dtype))
    return (jnp.concatenate(parts, axis=1),)


split_cols.defvjp(_split_cols_fwd, _split_cols_bwd)


def _contiguous(sizes):
    pieces, off = [], 0
    for n in sizes:
        pieces.append(((off, n),))
        off += n
    return tuple(pieces), off


def _rope(t, cos, sin):
    half = t.shape[-1] // 2
    t1, t2 = t[..., :half], t[..., half:]
    return jnp.concatenate([t1 * cos - t2 * sin, t2 * cos + t1 * sin], axis=-1)


def _s5_mixer(u, sp, l, w16, toks):
    g, p_, h = SSM_GROUPS, SSM_STATE, SSM_GROUP_CH
    lam_re, lam_im = sp["ssm_lam_re", l], sp["ssm_lam_im", l]
    step = jnp.exp(sp["ssm_log_step", l])[:, None]
    mag, ang = jnp.exp(lam_re * step), lam_im * step
    lbr, lbi = mag * jnp.cos(ang), mag * jnp.sin(ang)
    den = lam_re * lam_re + lam_im * lam_im
    nr, ni = lbr - 1.0, lbi
    qr, qi = (nr * lam_re + ni * lam_im) / den, (ni * lam_re - nr * lam_im) / den
    b_re, b_im = sp["ssm_b_re", l], sp["ssm_b_im", l]
    bbr = qr[..., None] * b_re - qi[..., None] * b_im
    bbi = qr[..., None] * b_im + qi[..., None] * b_re
    nb = g // SSM_PACK
    eye = jnp.eye(SSM_PACK, dtype=F32)

    def in_blocks(t):
        return jnp.einsum("jgph,gk->jghkp", t.reshape(nb, SSM_PACK, p_, h), eye).reshape(nb, SSM_PACK * h, SSM_PACK * p_)

    def out_blocks(t):
        return jnp.einsum("jghp,gk->jgpkh", t.reshape(nb, SSM_PACK, h, p_), eye).reshape(nb, SSM_PACK * p_, SSM_PACK * h)

    nm = "ssm%d" % l
    ur = bd_matmul(u, in_blocks(bbr), nm + "_bur")
    ui = bd_matmul(u, in_blocks(bbi), nm + "_bui")
    xr, xi = s5_scan(ur, ui, lbr.reshape(1, g * p_), lbi.reshape(1, g * p_), nm + "_scan")
    y = (bd_matmul(xr, out_blocks(sp["ssm_c_re", l]), nm + "_cr") + bd_matmul(xi, out_blocks(-sp["ssm_c_im", l]), nm + "_ci")
         + sp["ssm_d", l].reshape(1, g * h) * u)
    y = jax.nn.gelu(y)
    zz = matmul_w(y, w16["ssm_w_glu", l], toks["ssm_w_glu", l], nm + "_glu")
    z_out, z_gate = split_cols(zz, _contiguous((D_MODEL, D_MODEL)))
    return z_out * jax.nn.sigmoid(z_gate)


def _mla_mixer(q_lat, kv_lat, k_rope, cos, sin, sp, l, w16, toks):
    s = q_lat.shape[0]
    nm = "mla%d" % l
    dqk = MLA_NOPE + MLA_ROPE
    qt = matmul_w_t(rmsnorm(q_lat, sp["mla_q_norm", l], nm + "_qn"), w16["mla_w_uq", l], toks["mla_w_uq", l],
                    nm + "_uq").reshape(MLA_HEADS, dqk, s)
    half = MLA_ROPE // 2
    q1, q2 = qt[:, MLA_NOPE:MLA_NOPE + half], qt[:, MLA_NOPE + half:]
    ct, st = cos.T[None], sin.T[None]
    qt = jnp.concatenate([qt[:, :MLA_NOPE], q1 * ct - q2 * st, q2 * ct + q1 * st], axis=1).reshape(MLA_HEADS * dqk, s)
    kvt = matmul_w_t(rmsnorm(kv_lat, sp["mla_kv_norm", l], nm + "_kvn"), w16["mla_w_ukv", l], toks["mla_w_ukv", l],
                     nm + "_ukv").reshape(MLA_HEADS, MLA_NOPE + MLA_V, s)
    k_pe = _rope(k_rope, cos, sin)
    k = jnp.concatenate([kvt[:, :MLA_NOPE].transpose(0, 2, 1),
                         jnp.broadcast_to(k_pe[None], (MLA_HEADS, s, MLA_ROPE))], axis=-1)
    vt = kvt[:, MLA_NOPE:].reshape(MLA_HEADS * MLA_V, s)
    ot = attention(qt, k, vt, 1.0 / math.sqrt(dqk), True, nm + "_att")
    return matmul_w_at(ot, w16["mla_w_o", l], toks["mla_w_o", l], nm + "_o")


def _hgrn_mixer(q, f_logit, i_in, g, lb, sp, l, w16, toks):
    s = q.shape[0]
    nm = "hg%d" % l
    f = lb + (1.0 - lb) * jax.nn.sigmoid(f_logit)
    o = hgrn_core(jax.nn.silu(q), 1.0 - f, i_in, jnp.log(f), nm + "_core")
    o = rmsnorm(o.reshape(s * HG_HEADS, HG_DV), sp["hg_g_norm", l], nm + "_gn").reshape(s, HG_HEADS * HG_DV)
    return silu_gated_out(o, g, w16["hg_w_o", l], toks["hg_w_o", l], nm + "_o")


def _cross_attention(x, h, mem_n, l, w16, toks):
    s, m = h.shape[0], mem_n.shape[0]
    nm = "xat%d" % l
    xw = X_HEADS * X_HEAD_DIM
    qt = matmul_w_t(h, w16["x_w_q", l], toks["x_w_q", l], nm + "_q")
    kv = matmul_w(mem_n, w16["x_w_kv", l], toks["x_w_kv", l], nm + "_kv")
    k, v = split_cols(kv, _contiguous((xw, xw)))
    k = k.reshape(m, X_HEADS, X_HEAD_DIM).transpose(1, 0, 2)
    ot = attention(qt, k, v.T, 1.0 / math.sqrt(X_HEAD_DIM), False, nm + "_att")
    return matmul_w_at_res(x, ot, w16["x_w_o", l], toks["x_w_o", l], nm + "_o")


def _in_spec():
    hw = HG_HEADS * HG_DK
    sizes = (SSM_WIDTH, MLA_Q_RANK, MLA_KV_RANK, MLA_ROPE, hw, hw, hw, hw, D_MODEL, D_MODEL, D_MODEL)
    pieces, lo = [], 0
    for n in sizes:
        segs, c = [], lo
        while c < lo + n:
            end = min(lo + n, (c // D_IN_SHARD + 1) * D_IN_SHARD)
            segs.append((c + (D_IN_BLOCK - D_IN_SHARD) * (c // D_IN_SHARD), end - c))
            c = end
        pieces.append(tuple(segs))
        lo += n
    assert lo == D_IN
    return tuple(pieces), N_CHIPS * D_IN_BLOCK


def _layer(x, mem, cos, sin, lb, l, sp, w16, toks):
    s = x.shape[0]
    nm = "l%d" % l
    h = rmsnorm(x, sp["norm_mix", l], nm + "_nmix")
    z = matmul_w(h, w16["w_in", l], toks["w_in", l], nm + "_in")
    u, q_lat, kv_lat, k_rope, hg_q, hg_f, hg_i, hg_g, l_ssm, l_mla, l_hg = split_cols(z, _in_spec())
    y_ssm = _s5_mixer(u, sp, l, w16, toks)
    y_mla = _mla_mixer(q_lat, kv_lat, k_rope, cos, sin, sp, l, w16, toks)
    y_hg = _hgrn_mixer(hg_q, hg_f, hg_i, hg_g, lb, sp, l, w16, toks)
    x = gated_merge_out(x, (y_ssm, y_mla, y_hg), (l_ssm, l_mla, l_hg), w16["w_out", l], toks["w_out", l], nm + "_out")
    hc = rmsnorm(x, sp["norm_cross", l], nm + "_ncross")
    mem_n = rmsnorm(mem, sp["norm_mem", l], nm + "_nmem")
    x = _cross_attention(x, hc, mem_n, l, w16, toks)
    hf = rmsnorm(x, sp["norm_ffn", l], nm + "_nffn")
    gt = matmul_w(hf, w16["ffn_w_gate", l], toks["ffn_w_gate", l], nm + "_gate")
    up = matmul_w(hf, w16["ffn_w_up", l], toks["ffn_w_up", l], nm + "_up")
    return swiglu_down(x, gt, up, w16["ffn_w_down", l], toks["ffn_w_down", l], nm + "_down")


def _lower_bounds(hg_lb):
    lb_p = jax.nn.softmax(hg_lb, axis=0)
    return jnp.cumsum(lb_p, axis=0) - lb_p[0:1]


def _loss_head(norm_final, x, target):
    y = rmsnorm(x, norm_final, "nfinal")
    return 0.5 * jnp.sum(jnp.mean(jnp.square(y - target), axis=-1))


def _forward_backward(sp, toks, x, w16, later, between, mem, pos, target):
    half = MLA_ROPE // 2
    inv_freq = ROPE_THETA ** (-jnp.arange(half, dtype=F32) / half)
    ang = pos.astype(F32)[:, None] * inv_freq
    cos, sin = jnp.cos(ang), jnp.sin(ang)
    lower, lower_vjp = jax.vjp(_lower_bounds, sp["hg_lb", ALL])
    token, wait_for_weights = later
    x = x + token[0, 0]
    vjps = []
    for l in range(DEPTH):
        if l == 1:
            w16 = wait_for_weights(w16, x)
        sp_l = {k: v for k, v in sp.items() if k[1] == l}
        toks_l = {k: v for k, v in toks.items() if k[1] == l}
        weights = w16
        x, layer_vjp = jax.vjp(lambda s, t, lb, xx, l=l, weights=weights: _layer(xx, mem, cos, sin, lb, l, s, weights, t),
                               sp_l, toks_l, lower[l], x)
        vjps.append(layer_vjp)
    loss, head_vjp = jax.vjp(lambda g, xx: _loss_head(g, xx, target), sp["norm_final", ALL], x)
    g_sp, g_tok, d_lower = {}, {}, [None] * DEPTH
    g_sp["norm_final", ALL], dx = head_vjp(jnp.ones((), F32))
    for l in reversed(range(DEPTH)):
        g_s, g_t, d_lower[l], dx = vjps[l](dx)
        g_sp.update(g_s)
        g_tok.update(g_t)
        if l > 0:
            dx = between(l, g_t, dx)
    (g_sp["hg_lb", ALL],) = lower_vjp(jnp.stack(d_lower))
    return loss, g_sp, g_tok, dx


WEIGHTS = ["norm_mix", "w_in", "ssm_lam_re", "ssm_lam_im", "ssm_b_re", "ssm_b_im", "ssm_c_re", "ssm_c_im", "ssm_d",
           "ssm_log_step", "ssm_w_glu", "mla_q_norm", "mla_kv_norm", "mla_w_uq", "mla_w_ukv", "mla_w_o", "hg_lb",
           "hg_g_norm", "hg_w_o", "w_out", "norm_cross", "norm_mem", "x_w_q", "x_w_kv", "x_w_o", "norm_ffn",
           "ffn_w_gate_up", "ffn_w_down", "norm_final"]
INPUTS = ["x", "mem", "positions"] + WEIGHTS + ["loss_target"] + ["m_" + n for n in WEIGHTS] + ["v_" + n for n in WEIGHTS]
BIG = [("w_in", 2), ("ssm_w_glu", 2), ("mla_w_uq", 2), ("mla_w_ukv", 2), ("mla_w_o", 2), ("hg_w_o", 2), ("w_out", 1),
       ("x_w_q", 1), ("x_w_kv", 1), ("x_w_o", 2), ("ffn_w_gate_up", 2), ("ffn_w_down", 1)]
SMALL = [n for n in WEIGHTS if n not in dict(BIG)]
WHOLE = ("hg_lb", "norm_final")
ALL = -1
SMALL_W = 128
BLOCKED = ("w_in", "ffn_w_gate_up")
WAVE = 7


def _pack(arrs, rows, width, dtype):
    flat = jnp.concatenate([a.astype(dtype).reshape(-1) for a in arrs])
    return jnp.pad(flat, (0, rows * width - flat.shape[0])).reshape(rows, width)


def _unpack(buf, like):
    flat = buf.reshape(-1)
    out, off = [], 0
    for a in like:
        n = math.prod(a.shape)
        out.append(flat[off:off + n].reshape(a.shape))
        off += n
    return out


def kernel(
    x, mem, positions, norm_mix, w_in, ssm_lam_re, ssm_lam_im, ssm_b_re, ssm_b_im, ssm_c_re, ssm_c_im, ssm_d,
    ssm_log_step, ssm_w_glu, mla_q_norm, mla_kv_norm, mla_w_uq, mla_w_ukv, mla_w_o, hg_lb, hg_g_norm, hg_w_o,
    w_out, norm_cross, norm_mem, x_w_q, x_w_kv, x_w_o, norm_ffn, ffn_w_gate_up, ffn_w_down, norm_final,
    loss_target, m_norm_mix, m_w_in, m_ssm_lam_re, m_ssm_lam_im, m_ssm_b_re, m_ssm_b_im, m_ssm_c_re, m_ssm_c_im,
    m_ssm_d, m_ssm_log_step, m_ssm_w_glu, m_mla_q_norm, m_mla_kv_norm, m_mla_w_uq, m_mla_w_ukv, m_mla_w_o,
    m_hg_lb, m_hg_g_norm, m_hg_w_o, m_w_out, m_norm_cross, m_norm_mem, m_x_w_q, m_x_w_kv, m_x_w_o, m_norm_ffn,
    m_ffn_w_gate_up, m_ffn_w_down, m_norm_final, v_norm_mix, v_w_in, v_ssm_lam_re, v_ssm_lam_im, v_ssm_b_re,
    v_ssm_b_im, v_ssm_c_re, v_ssm_c_im, v_ssm_d, v_ssm_log_step, v_ssm_w_glu, v_mla_q_norm, v_mla_kv_norm,
    v_mla_w_uq, v_mla_w_ukv, v_mla_w_o, v_hg_lb, v_hg_g_norm, v_hg_w_o, v_w_out, v_norm_cross, v_norm_mem,
    v_x_w_q, v_x_w_kv, v_x_w_o, v_norm_ffn, v_ffn_w_gate_up, v_ffn_w_down, v_norm_final):
    given = locals()
    a = {n: given[n] for n in INPUTS}
    x, mem, pos, target = a["x"][0], a["mem"][0], a["positions"][0], a["loss_target"][0]
    place = jnp.stack([lax.axis_index("c"), 2 * lax.axis_index("x") + lax.axis_index("y")]).astype(jnp.int32)

    def shard16(n):
        w = a[n].astype(BF16)
        if n == "w_in":
            w = jnp.pad(w, ((0, 0), (0, 0), (0, D_IN_BLOCK - D_IN_SHARD)))
        return w

    def use(blocks, l, w16):
        for (n, ax), got in zip(BIG, blocks):
            if n in BLOCKED:
                full = got
            elif ax == 2:
                full = jnp.concatenate([got[q] for q in range(N_CHIPS)], axis=1)
            else:
                full = got.reshape(N_CHIPS * got.shape[1], got.shape[2])
            if n == "ffn_w_gate_up":
                half = N_CHIPS // 2
                w16["ffn_w_gate", l], w16["ffn_w_up", l] = full[:half], full[half:]
            else:
                w16[n, l] = full
        return w16

    shards = [shard16(n) for n, _ in BIG]
    halves = [s[0].reshape(2, s.shape[1] // 2, s.shape[2]) for s in shards]
    w16 = use([g.reshape(N_CHIPS, s.shape[1], s.shape[2]) for g, s in zip(_gather_weights(halves), shards)], 0, {})
    pending = _gather_start([s[1] for s in shards], after=w16["w_in", 0])
    toks = {}
    for n, ax in BIG:
        rows, cols = a[n].shape[1], shards[BIG.index((n, ax))].shape[2]
        shape = (N_CHIPS, rows, cols) if n in BLOCKED else (rows, N_CHIPS * cols) if ax == 2 else (N_CHIPS * rows, cols)
        for l in range(DEPTH):
            if n == "ffn_w_gate_up":
                toks["ffn_w_gate", l] = jnp.zeros((N_CHIPS // 2,) + shape[1:], F32)
                toks["ffn_w_up", l] = jnp.zeros((N_CHIPS // 2,) + shape[1:], F32)
            else:
                toks[n, l] = jnp.zeros(shape, F32)

    def layer1_weights(w16, after):
        return use(_gather_wait(*pending[:4], after), 1, dict(w16))
    sp = {}
    for n in SMALL:
        if n in WHOLE:
            sp[n, ALL] = a[n]
        else:
            for l in range(DEPTH):
                sp[n, l] = a[n][l]

    def chip_sums_of(g_tok, l):
        grads = []
        for n, ax in BIG:
            if n == "ffn_w_gate_up":
                g = jnp.concatenate([g_tok["ffn_w_gate", l], g_tok["ffn_w_up", l]], axis=0)
            else:
                g = g_tok[n, l]
            if n in BLOCKED:
                grads.append(g)
            elif ax == 2:
                grads.append(g.reshape(g.shape[0], N_CHIPS, g.shape[1] // N_CHIPS).transpose(1, 0, 2))
            else:
                grads.append(g.reshape(N_CHIPS, g.shape[0] // N_CHIPS, g.shape[1]))
        from_sibling = _swap_halves_to_sibling(grads, tag=str(l))
        sums = [_add_half(g, t, place, "grad_add_half_%s%d" % (n, l)) for (n, _), g, t in zip(BIG, grads, from_sibling)]
        return grads, from_sibling, sums

    def totals_of(l, grads, from_sibling, from_chips, which=BIG):
        return [_add_partials(g, t1, t2, place, "grad_add_partials_%s%d" % (n, l))
                for (n, _), g, t1, t2 in zip(which, grads, from_sibling, from_chips)]

    started = {}

    def between(l, g_tok_l, dx):
        grads, from_sibling, sums = chip_sums_of(g_tok_l, l)
        flight = _scatter_start(sums, after=dx)
        started[l] = (grads, from_sibling, flight)
        return dx + flight[4][0, 0]

    loss, g_sp, g_tok, g_x = _forward_backward(sp, toks, x, w16, (pending[4], layer1_weights), between, mem, pos, target)
    g_small = {n: g_sp[n, ALL] if n in WHOLE else jnp.stack([g_sp[n, l] for l in range(DEPTH)]) for n in SMALL}
    totals = {}
    for l, (grads, from_sibling, flight) in started.items():
        totals[l] = totals_of(l, grads, from_sibling, _scatter_wait(*flight[:4], after=g_x))
    grads, from_sibling, sums = chip_sums_of(g_tok, 0)
    out_g, out_d, out_m, out_v = {}, {}, {}, {}
    small_like = [a[n] for n in SMALL] + [jnp.zeros((1,), F32)]
    n_small = sum(math.prod(t.shape) for t in small_like)
    srows = -(-n_small // (SMALL_W * 8 * N_DEV)) * 8 * N_DEV
    reduced = _allreduce_small(_pack([g_small[n] for n in SMALL] + [loss.reshape(1)], srows, SMALL_W, F32))
    flight_a = _scatter_start(sums[:WAVE], after=reduced, tag="0a")
    flight_b = _scatter_start(sums[WAVE:], after=flight_a[4], tag="0b")
    reduced = reduced + flight_b[4][0, 0]
    red = _unpack(reduced, small_like)
    loss = red[-1].reshape(())
    small_arrs = [a[n] for n in SMALL]
    d, m, v = _adamw_call(
        _pack(small_arrs, srows, SMALL_W, F32), reduced,
        _pack([a["m_" + n] for n in SMALL], srows, SMALL_W, F32),
        _pack([a["v_" + n] for n in SMALL], srows, SMALL_W, F32), "adamw_small")
    for n, gg, dd, mm_, vv in zip(SMALL, red[:-1], _unpack(d, small_arrs), _unpack(m, small_arrs), _unpack(v, small_arrs)):
        out_g[n], out_d[n], out_m[n], out_v[n] = gg, dd, mm_, vv

    def adamw_big(l, joined, into, which=BIG):
        outs = {}
        for (n, _), g in zip(which, joined):
            shp = a[n].shape
            two = lambda t: t.reshape(shp[0] * shp[1], shp[2])
            outs[n] = _adamw_layer(two(a[n]), g[:, :shp[2]], two(a["m_" + n]), two(a["v_" + n]), l,
                                   into and into[n], "adamw%d_%s" % (l, n))
        return outs

    upper = adamw_big(1, _join_halves(totals[1], tag="1"), None)
    done, after = {}, upper[BIG[-1][0]][1]
    for tag, flight, part in (("0a", flight_a, slice(0, WAVE)), ("0b", flight_b, slice(WAVE, None))):
        from_chips = _scatter_wait(*flight[:4], after=after, tag=tag)
        sums_here = totals_of(0, grads[part], from_sibling[part], from_chips, BIG[part])
        done.update(adamw_big(0, _join_halves(sums_here, tag=tag), upper, BIG[part]))
        after = done[BIG[part][-1][0]][1]
    for n, _ in BIG:
        out_g[n], out_d[n], out_m[n], out_v[n] = [t.reshape(a[n].shape) for t in done[n]]

    return (loss, g_x[None], *[out_g[n] for n in WEIGHTS], *[out_d[n] for n in WEIGHTS],
            *[out_m[n] for n in WEIGHTS], *[out_v[n] for n in WEIGHTS])
```

```python
import functools
import math

import jax
import jax.numpy as jnp
from jax import lax
from jax.experimental import pallas as pl
from jax.experimental.pallas import tpu as pltpu

F32 = jnp.float32
BF16 = jnp.bfloat16
MESH = pl.DeviceIdType.MESH

D_MODEL = 1024
DEPTH = 2
RMS_EPS = 1e-6
SSM_GROUPS, SSM_GROUP_CH, SSM_STATE = 32, 16, 64
SSM_WIDTH = SSM_GROUPS * SSM_GROUP_CH
SSM_LANES = SSM_GROUPS * SSM_STATE
SSM_PACK = 8
MLA_HEADS, MLA_Q_RANK, MLA_KV_RANK, MLA_NOPE, MLA_ROPE, MLA_V = 8, 512, 256, 64, 32, 64
ROPE_THETA = 10000.0
HG_HEADS, HG_DK, HG_DV, HG_CHUNK = 4, 128, 128, 64
HG_SUB_FWD, HG_SUB_BWD = 32, 16
X_HEADS, X_HEAD_DIM = 4, 128
D_FF = 2816
D_IN = 6432
N_CHIPS = 4
D_IN_SHARD = D_IN // N_CHIPS
D_IN_BLOCK = 1664
MASK_VALUE = -1e30

ADAM_LR, ADAM_B1, ADAM_B2, ADAM_EPS, ADAM_WD, ADAM_STEP = 0.001, 0.9, 0.999, 1e-08, 0.01, 10

VMEM_LIMIT = 48 * 1024 * 1024


def _pcall(body, **kw):
    return pl.pallas_call(body, **kw)


def _params(sem):
    return pltpu.CompilerParams(dimension_semantics=sem, vmem_limit_bytes=VMEM_LIMIT)


def _tile(n, cap, align=128):
    if n <= cap:
        return n
    t = (cap // align) * align
    while t >= align:
        if n % t == 0:
            return t
        t -= align
    return n


MM_VMEM_BUDGET = 36 * 1024 * 1024
MM_TILE_CAP = 2048


def _divisors(n, cap, align=128):
    out = [n] if n <= cap else []
    out += [t for t in range(align, min(n, cap + 1), align) if n % t == 0]
    return sorted(set(out), reverse=True)


@functools.lru_cache(maxsize=None)
def _mm_tiles(m, n, k, a_bytes, b_bytes, o_bytes, n_unit, k_unit):
    best = None
    for tm in _divisors(m, MM_TILE_CAP):
        for tn in _divisors(n_unit, MM_TILE_CAP):
            for tk in _divisors(k_unit, MM_TILE_CAP):
                nk = k // tk
                need = 2 * (tm * tk * a_bytes + tk * tn * b_bytes + tm * tn * o_bytes)
                need += tm * tn * 4 * (2 if nk > 1 else 1)
                need += (tm * tk * 2 if a_bytes > 2 else 0) + (tk * tn * 2 if b_bytes > 2 else 0)
                if need > MM_VMEM_BUDGET:
                    continue
                key = ((m // tm) * (n // tn) * nk, nk, -tn)
                if best is None or key < best[0]:
                    best = (key, (tm, tn, tk))
    assert best is not None, (m, n, k)
    return best[1]


def _mm(a, b, *, ta=False, tb=False, out_dtype=F32, name="mm", b_blocks=False, out_blocks=0, pro=None, epi=None,
        save_left=False):
    if ta:
        K, M = a.shape
    else:
        M, K = a.shape
    n_unit = k_unit = None
    if b_blocks:
        nb, rows, cols = b.shape
        if tb:
            N, Kb, k_unit = rows, nb * cols, cols
        else:
            Kb, N, n_unit = rows, nb * cols, cols
    elif tb:
        N, Kb = b.shape
    else:
        Kb, N = b.shape
    if out_blocks:
        assert not b_blocks and N % out_blocks == 0
        n_unit = N // out_blocks
    assert K == Kb, (a.shape, b.shape, ta, tb)
    n_pro = len(pro) - 1 if pro else 0
    n_epi = len(epi) - 1 if epi else 0
    tm, tn, tk = _mm_tiles(M, N, K, a.dtype.itemsize * (1 + n_pro), b.dtype.itemsize,
                           jnp.dtype(out_dtype).itemsize * max(1, 2 * n_epi), n_unit or N, k_unit or K)
    nk = K // tk
    dims = (((0,) if ta else (1,), (1,) if tb else (0,)), ((), ()))
    n_in = 2 + n_pro + n_epi
    n_out = max(1, n_epi) + (1 if save_left else 0)
    assert not save_left or (pro and tn == N)

    def body(*refs):
        ins, outs, acc = refs[:n_in], refs[n_in:n_in + n_out], refs[n_in + n_out:]
        b_ref = ins[1 + n_pro]
        av = (pro[0](*[r[...] for r in ins[:1 + n_pro]]) if pro else ins[0][...]).astype(BF16)
        if save_left:
            outs[-1][...] = av
        bv = b_ref[...].astype(BF16)
        part = lax.dot_general(av, bv, dims, preferred_element_type=F32)

        def finish(total):
            if epi:
                for o_ref, r in zip(outs, epi[0](total, *[e[...] for e in ins[2 + n_pro:]])):
                    o_ref[...] = r.astype(o_ref.dtype)
            else:
                outs[0][...] = total.astype(outs[0].dtype)

        if nk == 1:
            finish(part)
        else:
            acc_ref = acc[0]
            k = pl.program_id(2)

            @pl.when(k == 0)
            def _():
                acc_ref[...] = part

            @pl.when(k > 0)
            def _():
                acc_ref[...] += part

            @pl.when(k == nk - 1)
            def _():
                finish(acc_ref[...])

    a_spec = pl.BlockSpec((tk, tm), lambda i, j, k: (k, i)) if ta else pl.BlockSpec((tm, tk), lambda i, j, k: (i, k))
    if b_blocks and tb:
        r = k_unit // tk
        b_spec = pl.BlockSpec((None, tn, tk), lambda i, j, k: (k // r, j, k % r))
    elif b_blocks:
        r = n_unit // tn
        b_spec = pl.BlockSpec((None, tk, tn), lambda i, j, k: (j // r, k, j % r))
    elif tb:
        b_spec = pl.BlockSpec((tn, tk), lambda i, j, k: (j, k))
    else:
        b_spec = pl.BlockSpec((tk, tn), lambda i, j, k: (k, j))
    if out_blocks:
        r = n_unit // tn
        out_spec = pl.BlockSpec((None, tm, tn), lambda i, j, k: (j // r, i, j % r))
        out_shape = jax.ShapeDtypeStruct((out_blocks, M, n_unit), out_dtype)
    else:
        out_spec = pl.BlockSpec((tm, tn), lambda i, j, k: (i, j))
        out_shape = jax.ShapeDtypeStruct((M, N), out_dtype)
    in_specs = [a_spec] * (1 + n_pro) + [b_spec] + [out_spec] * n_epi
    args = [a] + (list(pro[1:]) if pro else []) + [b] + (list(epi[1:]) if epi else [])
    out_specs, out_shapes = ([out_spec] * n_epi, [out_shape] * n_epi) if epi else (out_spec, out_shape)
    if save_left:
        out_specs = [out_spec] * max(1, n_epi) + [a_spec]
        out_shapes = [out_shape] * max(1, n_epi) + [jax.ShapeDtypeStruct(a.shape, BF16)]
    return _pcall(
        body,
        name=name,
        grid=(M // tm, N // tn, nk),
        in_specs=in_specs,
        out_specs=out_specs,
        out_shape=out_shapes,
        scratch_shapes=[pltpu.VMEM((tm, tn), F32)] if nk > 1 else [],
        compiler_params=_params(("parallel", "parallel", "arbitrary")),
    )(*args)


def _bd_call(a, bc, transpose, name):
    m = a.shape[0]
    nb, ka, kb = bc.shape
    win, wout = (kb, ka) if transpose else (ka, kb)
    assert a.shape[1] == nb * win
    tm = _tile(m, 1024)
    dims = _NT if transpose else (((1,), (0,)), ((), ()))

    def body(a_ref, b_ref, o_ref):
        o_ref[...] = lax.dot_general(a_ref[...].astype(BF16), b_ref[0].astype(BF16), dims, preferred_element_type=F32)

    return _pcall(
        body,
        name=name,
        grid=(m // tm, nb),
        in_specs=[pl.BlockSpec((tm, win), lambda i, j: (i, j)), pl.BlockSpec((1, ka, kb), lambda i, j: (j, 0, 0))],
        out_specs=pl.BlockSpec((tm, wout), lambda i, j: (i, j)),
        out_shape=jax.ShapeDtypeStruct((m, nb * wout), F32),
        compiler_params=_params(("parallel", "parallel")),
    )(a, bc)


def _bd_dw_call(a, g, nb, name):
    m = a.shape[0]
    ka, kb = a.shape[1] // nb, g.shape[1] // nb
    tk = _tile(m, 1024)
    nk = m // tk

    def body(a_ref, g_ref, o_ref):
        part = lax.dot_general(a_ref[...].astype(BF16), g_ref[...].astype(BF16), _TN, preferred_element_type=F32)
        k = pl.program_id(1)

        @pl.when(k == 0)
        def _():
            o_ref[0] = part

        @pl.when(k > 0)
        def _():
            o_ref[0] += part

    return _pcall(
        body,
        name=name,
        grid=(nb, nk),
        in_specs=[pl.BlockSpec((tk, ka), lambda j, k: (k, j)), pl.BlockSpec((tk, kb), lambda j, k: (k, j))],
        out_specs=pl.BlockSpec((1, ka, kb), lambda j, k: (j, 0, 0)),
        out_shape=jax.ShapeDtypeStruct((nb, ka, kb), F32),
        compiler_params=_params(("parallel", "arbitrary")),
    )(a, g)


@functools.partial(jax.custom_vjp, nondiff_argnums=(2,))
def bd_matmul(a, bc, name):
    return _bd_call(a, bc, False, name + "_f")


def _bd_matmul_fwd(a, bc, name):
    return _bd_call(a, bc, False, name + "_f"), (a, bc)


def _bd_matmul_bwd(name, res, g):
    a, bc = res
    return _bd_call(g, bc, True, name + "_da"), _bd_dw_call(a, g, bc.shape[0], name + "_dw")


bd_matmul.defvjp(_bd_matmul_fwd, _bd_matmul_bwd)


@functools.partial(jax.custom_vjp, nondiff_argnums=(3,))
def matmul_w(a, w16, tok, name):
    return _mm(a, w16, b_blocks=w16.ndim == 3, name=name + "_f")


def _matmul_w_fwd(a, w16, tok, name):
    return _mm(a, w16, b_blocks=w16.ndim == 3, name=name + "_f"), (a, w16)


def _matmul_w_bwd(name, res, g):
    a, w16 = res
    blocks = w16.shape[0] if w16.ndim == 3 else 0
    return (_mm(g, w16, tb=True, b_blocks=bool(blocks), name=name + "_da"), jnp.zeros_like(w16),
            _mm(a, g, ta=True, out_blocks=blocks, name=name + "_dw"))


matmul_w.defvjp(_matmul_w_fwd, _matmul_w_bwd)


def _swiglu(gt, up):
    return gt * jax.nn.sigmoid(gt) * up


def _swiglu_bwd(dact, gt, up):
    sg = jax.nn.sigmoid(gt)
    return dact * up * sg * (1.0 + gt * (1.0 - sg)), dact * gt * sg


def _add_residual(product, x):
    return (product + x,)


def _silu_gate(o, g):
    return o * (g * jax.nn.sigmoid(g))


def _silu_gate_bwd(dm, o, g):
    sg = jax.nn.sigmoid(g)
    return dm * g * sg, dm * o * sg * (1.0 + g * (1.0 - sg))


@functools.partial(jax.custom_vjp, nondiff_argnums=(4,))
def silu_gated_out(o, g, w16, tok, name):
    return _mm(o, w16, pro=(_silu_gate, g), name=name + "_f")


def _silu_gated_out_fwd(o, g, w16, tok, name):
    out, left16 = _mm(o, w16, pro=(_silu_gate, g), save_left=True, name=name + "_f")
    return out, (o, g, w16, left16)


def _silu_gated_out_bwd(name, res, d):
    o, g, w16, left16 = res
    d_o, d_g = _mm(d, w16, tb=True, epi=(_silu_gate_bwd, o, g), name=name + "_da")
    return d_o, d_g, jnp.zeros_like(w16), _mm(left16, d, ta=True, name=name + "_dw")


silu_gated_out.defvjp(_silu_gated_out_fwd, _silu_gated_out_bwd)


@functools.partial(jax.custom_vjp, nondiff_argnums=(5,))
def swiglu_down(x, gt, up, w16, tok, name):
    return _mm(gt, w16, pro=(_swiglu, up), epi=(_add_residual, x), name=name + "_f")[0]


def _swiglu_down_fwd(x, gt, up, w16, tok, name):
    out, act16 = _mm(gt, w16, pro=(_swiglu, up), epi=(_add_residual, x), save_left=True, name=name + "_f")
    return out, (gt, up, w16, act16)


def _swiglu_down_bwd(name, res, g):
    gt, up, w16, act16 = res
    d_gt, d_up = _mm(g, w16, tb=True, epi=(_swiglu_bwd, gt, up), name=name + "_da")
    return g, d_gt, d_up, jnp.zeros_like(w16), _mm(act16, g, ta=True, name=name + "_dw")


swiglu_down.defvjp(_swiglu_down_fwd, _swiglu_down_bwd)


def _merge(y1, y2, y3, l1, l2, l3):
    return jax.nn.sigmoid(l1) * y1 + jax.nn.sigmoid(l2) * y2 + jax.nn.sigmoid(l3) * y3


def _merge_bwd(dm, y1, y2, y3, l1, l2, l3):
    dys, dls = [], []
    for y, logit in ((y1, l1), (y2, l2), (y3, l3)):
        s = jax.nn.sigmoid(logit)
        dys.append(dm * s)
        dls.append(dm * y * s * (1.0 - s))
    return (*dys, *dls)


@functools.partial(jax.custom_vjp, nondiff_argnums=(5,))
def gated_merge_out(x, ys, logits, w16, tok, name):
    return _mm(ys[0], w16, pro=(_merge, *ys[1:], *logits), epi=(_add_residual, x), name=name + "_f")[0]


def _gated_merge_out_fwd(x, ys, logits, w16, tok, name):
    out, merged16 = _mm(ys[0], w16, pro=(_merge, *ys[1:], *logits), epi=(_add_residual, x), save_left=True,
                        name=name + "_f")
    return out, (ys, logits, w16, merged16)


def _gated_merge_out_bwd(name, res, g):
    ys, logits, w16, merged16 = res
    d = _mm(g, w16, tb=True, epi=(_merge_bwd, *ys, *logits), name=name + "_da")
    return g, tuple(d[:3]), tuple(d[3:]), jnp.zeros_like(w16), _mm(merged16, g, ta=True, name=name + "_dw")


gated_merge_out.defvjp(_gated_merge_out_fwd, _gated_merge_out_bwd)


@functools.partial(jax.custom_vjp, nondiff_argnums=(3,))
def matmul_w_t(a, w16, tok, name):
    return _mm(w16, a, ta=True, tb=True, name=name + "_f")


def _matmul_w_t_fwd(a, w16, tok, name):
    return _mm(w16, a, ta=True, tb=True, name=name + "_f"), (a, w16)


def _matmul_w_t_bwd(name, res, g):
    a, w16 = res
    return (_mm(g, w16, ta=True, tb=True, name=name + "_da"), jnp.zeros_like(w16),
            _mm(a, g, ta=True, tb=True, name=name + "_dw"))


matmul_w_t.defvjp(_matmul_w_t_fwd, _matmul_w_t_bwd)


@functools.partial(jax.custom_vjp, nondiff_argnums=(3,))
def matmul_w_at(at, w16, tok, name):
    return _mm(at, w16, ta=True, name=name + "_f")


def _matmul_w_at_fwd(at, w16, tok, name):
    return _mm(at, w16, ta=True, name=name + "_f"), (at, w16)


def _matmul_w_at_bwd(name, res, g):
    at, w16 = res
    return _mm(w16, g, tb=True, name=name + "_da"), jnp.zeros_like(w16), _mm(at, g, name=name + "_dw")


matmul_w_at.defvjp(_matmul_w_at_fwd, _matmul_w_at_bwd)


@functools.partial(jax.custom_vjp, nondiff_argnums=(4,))
def matmul_w_at_res(x, at, w16, tok, name):
    return _mm(at, w16, ta=True, epi=(_add_residual, x), name=name + "_f")[0]


def _matmul_w_at_res_fwd(x, at, w16, tok, name):
    return _mm(at, w16, ta=True, epi=(_add_residual, x), name=name + "_f")[0], (at, w16)


def _matmul_w_at_res_bwd(name, res, g):
    return (g, *_matmul_w_at_bwd(name, res, g))


matmul_w_at_res.defvjp(_matmul_w_at_res_fwd, _matmul_w_at_res_bwd)


def _rms_fwd_call(x, g, name):
    rows, d = x.shape
    tr = _tile(rows, 512, 8)

    def body(x_ref, g_ref, y_ref):
        xv = x_ref[...]
        rstd = lax.rsqrt(jnp.mean(xv * xv, axis=-1, keepdims=True) + RMS_EPS)
        y_ref[...] = xv * rstd * g_ref[...]

    return _pcall(
        body,
        name=name,
        grid=(rows // tr,),
        in_specs=[pl.BlockSpec((tr, d), lambda i: (i, 0)), pl.BlockSpec((1, d), lambda i: (0, 0))],
        out_specs=pl.BlockSpec((tr, d), lambda i: (i, 0)),
        out_shape=jax.ShapeDtypeStruct((rows, d), F32),
        compiler_params=_params(("parallel",)),
    )(x, g.reshape(1, d))


def _rms_bwd_call(x, g, dy, name):
    rows, d = x.shape
    tr = _tile(rows, 512, 8)
    nb = rows // tr

    def body(x_ref, g_ref, dy_ref, dx_ref, dg_ref):
        xv = x_ref[...]
        rstd = lax.rsqrt(jnp.mean(xv * xv, axis=-1, keepdims=True) + RMS_EPS)
        xh = xv * rstd
        dyv = dy_ref[...]
        dyg = dyv * g_ref[...]
        dx_ref[...] = rstd * (dyg - xh * jnp.mean(dyg * xh, axis=-1, keepdims=True))
        dg_ref[...] = jnp.sum((dyv * xh).reshape(tr // 8, 8, d), axis=0)

    dx, dgp = _pcall(
        body,
        name=name,
        grid=(nb,),
        in_specs=[
            pl.BlockSpec((tr, d), lambda i: (i, 0)),
            pl.BlockSpec((1, d), lambda i: (0, 0)),
            pl.BlockSpec((tr, d), lambda i: (i, 0)),
        ],
        out_specs=[pl.BlockSpec((tr, d), lambda i: (i, 0)), pl.BlockSpec((8, d), lambda i: (i, 0))],
        out_shape=[jax.ShapeDtypeStruct((rows, d), F32), jax.ShapeDtypeStruct((nb * 8, d), F32)],
        compiler_params=_params(("parallel",)),
    )(x, g.reshape(1, d), dy)
    return dx, jnp.sum(dgp, axis=0)


@functools.partial(jax.custom_vjp, nondiff_argnums=(2,))
def rmsnorm(x, g, name):
    return _rms_fwd_call(x, g, name + "_f")


def _rmsnorm_fwd(x, g, name):
    return _rms_fwd_call(x, g, name + "_f"), (x, g)


def _rmsnorm_bwd(name, res, dy):
    x, g = res
    return _rms_bwd_call(x, g, dy, name + "_b")


rmsnorm.defvjp(_rmsnorm_fwd, _rmsnorm_bwd)


def _attn_blocks(sq, sk):
    return _tile(sq, 1024), _tile(sk, 512)


def _attn_pt(qs, k_ref, qi, ki, bq, bk, masked, shift):
    st = jnp.dot(k_ref[0].astype(BF16), qs, preferred_element_type=F32)
    if masked:
        kpos = ki * bk + lax.broadcasted_iota(jnp.int32, (bk, bq), 0)
        qpos = qi * bq + lax.broadcasted_iota(jnp.int32, (bk, bq), 1)
        st = jnp.where(kpos <= qpos, st, MASK_VALUE)
    return st if shift is None else jnp.exp(st - shift)


def _attn_pairs(nq, nk, bq, bk, causal, k_major):
    ok = lambda i, j: (not causal) or j * bk <= i * bq + bq - 1
    if k_major:
        pairs = [(i, j) for j in range(nk) for i in range(nq) if ok(i, j)]
    else:
        pairs = [(i, j) for i in range(nq) for j in range(nk) if ok(i, j)]
    qtab = jnp.asarray([p[0] for p in pairs], jnp.int32)
    ktab = jnp.asarray([p[1] for p in pairs], jnp.int32)
    first_q = (lambda j: (j * bk) // bq) if causal else (lambda j: 0)
    last_k = (lambda i: jnp.minimum((i * bq + bq - 1) // bk, nk - 1)) if causal else (lambda i: nk - 1)
    return qtab, ktab, first_q, last_k


def _attn_cases(causal, qi, ki, bq, bk, step):
    if not causal:
        step(False)
        return
    pl.when(ki * bk + bk - 1 <= qi * bq)(functools.partial(step, False))
    pl.when(ki * bk + bk - 1 > qi * bq)(functools.partial(step, True))


def _attn_fwd_call(qt, k, vt, scale, causal, name):
    h, sk, dq = k.shape
    sq = qt.shape[1]
    dv = vt.shape[0] // h
    bq, bk = _attn_blocks(sq, sk)
    nq, nk = sq // bq, sk // bk

    qtab, ktab, _, last_k = _attn_pairs(nq, nk, bq, bk, causal, False)

    def body(qi_ref, ki_ref, q_ref, k_ref, v_ref, o_ref, lse_ref, m_sc, l_sc, acc_sc):
        qi, ki = qi_ref[pl.program_id(1)], ki_ref[pl.program_id(1)]

        @pl.when(ki == 0)
        def _():
            m_sc[...] = jnp.full(m_sc.shape, -jnp.inf, F32)
            l_sc[...] = jnp.zeros(l_sc.shape, F32)
            acc_sc[...] = jnp.zeros(acc_sc.shape, F32)

        def step(masked):
            qs = (q_ref[...] * scale).astype(BF16)
            st = _attn_pt(qs, k_ref, qi, ki, bq, bk, masked, None)
            m_prev = m_sc[...]
            m_new = jnp.maximum(m_prev, jnp.max(st, axis=0, keepdims=True))
            alpha = jnp.exp(m_prev - m_new)
            pt = jnp.exp(st - m_new)
            l_sc[...] = alpha * l_sc[...] + jnp.sum(pt, axis=0, keepdims=True)
            acc_sc[...] = alpha * acc_sc[...] + jnp.dot(v_ref[...].astype(BF16), pt.astype(BF16),
                                                        preferred_element_type=F32)
            m_sc[...] = m_new

        _attn_cases(causal, qi, ki, bq, bk, step)

        @pl.when(ki == last_k(qi))
        def _():
            o_ref[...] = acc_sc[...] / l_sc[...]
            lse_ref[0] = m_sc[...] + jnp.log(l_sc[...])

    qmap = lambda hh, p, qi_ref, ki_ref: (hh, qi_ref[p])
    return _pcall(
        body,
        name=name,
        grid_spec=pltpu.PrefetchScalarGridSpec(
            num_scalar_prefetch=2,
            grid=(h, qtab.shape[0]),
            in_specs=[
                pl.BlockSpec((dq, bq), qmap),
                pl.BlockSpec((1, bk, dq), lambda hh, p, qi_ref, ki_ref: (hh, ki_ref[p], 0)),
                pl.BlockSpec((dv, bk), lambda hh, p, qi_ref, ki_ref: (hh, ki_ref[p])),
            ],
            out_specs=[pl.BlockSpec((dv, bq), qmap),
                       pl.BlockSpec((1, 1, bq), lambda hh, p, qi_ref, ki_ref: (hh, 0, qi_ref[p]))],
            scratch_shapes=[pltpu.VMEM((1, bq), F32), pltpu.VMEM((1, bq), F32), pltpu.VMEM((dv, bq), F32)],
        ),
        out_shape=[jax.ShapeDtypeStruct((h * dv, sq), F32), jax.ShapeDtypeStruct((h, 1, sq), F32)],
        compiler_params=_params(("parallel", "arbitrary")),
    )(qtab, ktab, qt, k, vt)


_TN = (((0,), (0,)), ((), ()))
_NT = (((1,), (1,)), ((), ()))


def _attn_dst(qs, k_ref, v_ref, do_ref, lse_ref, dl_ref, qi, ki, bq, bk, masked):
    pt = _attn_pt(qs, k_ref, qi, ki, bq, bk, masked, lse_ref[0])
    dpt = lax.dot_general(v_ref[...].astype(BF16), do_ref[...].astype(BF16), _TN, preferred_element_type=F32)
    return pt, pt * (dpt - dl_ref[0])


def _attn_dq_call(qt, k, vt, dot, lse, delta, scale, causal, name):
    h, sk, dq = k.shape
    sq = qt.shape[1]
    dv = vt.shape[0] // h
    bq, bk = _attn_blocks(sq, sk)
    nq, nk = sq // bq, sk // bk

    qtab, ktab, _, last_k = _attn_pairs(nq, nk, bq, bk, causal, False)

    def body(qi_ref, ki_ref, q_ref, k_ref, v_ref, do_ref, lse_ref, dl_ref, dq_ref, acc_sc):
        qi, ki = qi_ref[pl.program_id(1)], ki_ref[pl.program_id(1)]

        @pl.when(ki == 0)
        def _():
            acc_sc[...] = jnp.zeros(acc_sc.shape, F32)

        def step(masked):
            qs = (q_ref[...] * scale).astype(BF16)
            _, dst = _attn_dst(qs, k_ref, v_ref, do_ref, lse_ref, dl_ref, qi, ki, bq, bk, masked)
            acc_sc[...] += lax.dot_general(k_ref[0].astype(BF16), dst.astype(BF16), _TN, preferred_element_type=F32)

        _attn_cases(causal, qi, ki, bq, bk, step)

        @pl.when(ki == last_k(qi))
        def _():
            dq_ref[...] = acc_sc[...] * scale

    qmap = lambda hh, p, qi_ref, ki_ref: (hh, qi_ref[p])
    rowmap = lambda hh, p, qi_ref, ki_ref: (hh, 0, qi_ref[p])
    return _pcall(
        body,
        name=name,
        grid_spec=pltpu.PrefetchScalarGridSpec(
            num_scalar_prefetch=2,
            grid=(h, qtab.shape[0]),
            in_specs=[
                pl.BlockSpec((dq, bq), qmap),
                pl.BlockSpec((1, bk, dq), lambda hh, p, qi_ref, ki_ref: (hh, ki_ref[p], 0)),
                pl.BlockSpec((dv, bk), lambda hh, p, qi_ref, ki_ref: (hh, ki_ref[p])),
                pl.BlockSpec((dv, bq), qmap),
                pl.BlockSpec((1, 1, bq), rowmap),
                pl.BlockSpec((1, 1, bq), rowmap),
            ],
            out_specs=pl.BlockSpec((dq, bq), qmap),
            scratch_shapes=[pltpu.VMEM((dq, bq), F32)],
        ),
        out_shape=jax.ShapeDtypeStruct((h * dq, sq), F32),
        compiler_params=_params(("parallel", "arbitrary")),
    )(qtab, ktab, qt, k, vt, dot, lse, delta)


def _attn_dkv_call(qt, k, vt, dot, lse, delta, scale, causal, name):
    h, sk, dq = k.shape
    sq = qt.shape[1]
    dv = vt.shape[0] // h
    bq, bk = _attn_blocks(sq, sk)
    nq, nk = sq // bq, sk // bk

    qtab, ktab, first_q, _ = _attn_pairs(nq, nk, bq, bk, causal, True)

    def body(qi_ref, ki_ref, q_ref, k_ref, v_ref, do_ref, lse_ref, dl_ref, dk_ref, dv_ref, dk_sc, dv_sc):
        qi, ki = qi_ref[pl.program_id(1)], ki_ref[pl.program_id(1)]

        @pl.when(qi == first_q(ki))
        def _():
            dk_sc[...] = jnp.zeros(dk_sc.shape, F32)
            dv_sc[...] = jnp.zeros(dv_sc.shape, F32)

        def step(masked):
            qs = (q_ref[...] * scale).astype(BF16)
            pt, dst = _attn_dst(qs, k_ref, v_ref, do_ref, lse_ref, dl_ref, qi, ki, bq, bk, masked)
            dv_sc[...] += lax.dot_general(do_ref[...].astype(BF16), pt.astype(BF16), _NT, preferred_element_type=F32)
            dk_sc[...] += lax.dot_general(dst.astype(BF16), qs, _NT, preferred_element_type=F32)

        _attn_cases(causal, qi, ki, bq, bk, step)

        @pl.when(qi == nq - 1)
        def _():
            dk_ref[0] = dk_sc[...]
            dv_ref[...] = dv_sc[...]

    qmap = lambda hh, p, qi_ref, ki_ref: (hh, qi_ref[p])
    rowmap = lambda hh, p, qi_ref, ki_ref: (hh, 0, qi_ref[p])
    kmap = lambda hh, p, qi_ref, ki_ref: (hh, ki_ref[p], 0)
    vmap = lambda hh, p, qi_ref, ki_ref: (hh, ki_ref[p])
    return _pcall(
        body,
        name=name,
        grid_spec=pltpu.PrefetchScalarGridSpec(
            num_scalar_prefetch=2,
            grid=(h, qtab.shape[0]),
            in_specs=[
                pl.BlockSpec((dq, bq), qmap),
                pl.BlockSpec((1, bk, dq), kmap),
                pl.BlockSpec((dv, bk), vmap),
                pl.BlockSpec((dv, bq), qmap),
                pl.BlockSpec((1, 1, bq), rowmap),
                pl.BlockSpec((1, 1, bq), rowmap),
            ],
            out_specs=[pl.BlockSpec((1, bk, dq), kmap), pl.BlockSpec((dv, bk), vmap)],
            scratch_shapes=[pltpu.VMEM((bk, dq), F32), pltpu.VMEM((dv, bk), F32)],
        ),
        out_shape=[jax.ShapeDtypeStruct((h, sk, dq), F32), jax.ShapeDtypeStruct((h * dv, sk), F32)],
        compiler_params=_params(("parallel", "arbitrary")),
    )(qtab, ktab, qt, k, vt, dot, lse, delta)


@functools.partial(jax.custom_vjp, nondiff_argnums=(3, 4, 5))
def attention(qt, k, vt, scale, causal, name):
    return _attn_fwd_call(qt, k, vt, scale, causal, name + "_f")[0]


def _attention_fwd(qt, k, vt, scale, causal, name):
    ot, lse = _attn_fwd_call(qt, k, vt, scale, causal, name + "_f")
    return ot, (qt, k, vt, ot, lse)


def _attention_bwd(scale, causal, name, res, dot):
    qt, k, vt, ot, lse = res
    h = k.shape[0]
    delta = jnp.sum((dot * ot).reshape(h, ot.shape[0] // h, ot.shape[1]), axis=1, keepdims=True)
    dqt = _attn_dq_call(qt, k, vt, dot, lse, delta, scale, causal, name + "_dq")
    dk, dvt = _attn_dkv_call(qt, k, vt, dot, lse, delta, scale, causal, name + "_dkv")
    return dqt, dk, dvt


attention.defvjp(_attention_fwd, _attention_bwd)


SCAN_ROWS = 8
SCAN_LANES = 256
SCAN_TBLOCK = 512
SCAN_UNROLL = 4


def _scan_tables(lr, li):
    pr, pi = [lr], [li]
    for _ in range(SCAN_ROWS - 1):
        pr, pi = pr + [pr[-1] * lr - pi[-1] * li], pi + [pr[-1] * li + pi[-1] * lr]
    return jnp.concatenate(pr, axis=0), jnp.concatenate(pi, axis=0)


def _scan_call(ur, ui, lr, li, reverse, name, states=None):
    s, n = ur.shape
    tb = _tile(s, SCAN_TBLOCK, 8)
    nt, nl = s // tb, n // SCAN_LANES
    ntile = tb // SCAN_ROWS
    pr, pi = _scan_tables(lr, li)
    if reverse:
        pr, pi = pr[::-1], pi[::-1]
    shape = (SCAN_ROWS, SCAN_LANES)
    with_states = states is not None
    assert reverse or not with_states

    def body(*refs):
        if with_states:
            ur_ref, ui_ref, pr_ref, pi_ref, sr_ref, si_ref, xr_ref, xi_ref, dr_ref, di_ref, cr_sc, ci_sc, ar_sc, ai_sc = refs
        else:
            ur_ref, ui_ref, pr_ref, pi_ref, xr_ref, xi_ref, cr_sc, ci_sc, ar_sc, ai_sc = refs

        @pl.when(pl.program_id(1) == 0)
        def _():
            for sc in (cr_sc, ci_sc, ar_sc, ai_sc):
                sc[...] = jnp.zeros(shape, F32)

        prv, piv = pr_ref[...], pi_ref[...]
        rows = lax.broadcasted_iota(jnp.int32, shape, 0)

        def powers(k):
            r = (SCAN_ROWS - k) if reverse else (k - 1)
            return jnp.broadcast_to(prv[r:r + 1], shape), jnp.broadcast_to(piv[r:r + 1], shape)

        pw = [powers(k) for k in (1, 2, 4)]

        def tile(i, carry):
            cr, ci, acr, aci = carry
            j = (ntile - 1 - i) if reverse else i
            sl = pl.ds(pl.multiple_of(j * SCAN_ROWS, SCAN_ROWS), SCAN_ROWS)
            xr, xi = ur_ref[sl, :], ui_ref[sl, :]
            for (ar, ai), k in zip(pw, (1, 2, 4)):
                if reverse:
                    keep = rows < SCAN_ROWS - k
                    sr = jnp.where(keep, pltpu.roll(xr, SCAN_ROWS - k, 0), 0.0)
                    si = jnp.where(keep, pltpu.roll(xi, SCAN_ROWS - k, 0), 0.0)
                else:
                    keep = rows >= k
                    sr = jnp.where(keep, pltpu.roll(xr, k, 0), 0.0)
                    si = jnp.where(keep, pltpu.roll(xi, k, 0), 0.0)
                xr, xi = xr + (ar * sr - ai * si), xi + (ar * si + ai * sr)
            xr, xi = xr + (prv * cr - piv * ci), xi + (prv * ci + piv * cr)
            xr_ref[sl, :] = xr
            xi_ref[sl, :] = xi
            if with_states:
                last = rows == SCAN_ROWS - 1
                nr = jnp.where(last, cr, pltpu.roll(xr, SCAN_ROWS - 1, 0))
                ni = jnp.where(last, ci, pltpu.roll(xi, SCAN_ROWS - 1, 0))
                sr, si = sr_ref[sl, :], si_ref[sl, :]
                acr, aci = acr + (sr * nr + si * ni), aci + (sr * ni - si * nr)
            e = 0 if reverse else SCAN_ROWS - 1
            return jnp.broadcast_to(xr[e:e + 1], shape), jnp.broadcast_to(xi[e:e + 1], shape), acr, aci

        def tiles(g, carry):
            for u in range(SCAN_UNROLL):
                carry = tile(g * SCAN_UNROLL + u, carry)
            return carry

        assert ntile % SCAN_UNROLL == 0
        cr, ci, acr, aci = lax.fori_loop(0, ntile // SCAN_UNROLL, tiles,
                                         (cr_sc[...], ci_sc[...], ar_sc[...], ai_sc[...]))
        cr_sc[...] = cr
        ci_sc[...] = ci
        if with_states:
            ar_sc[...] = acr
            ai_sc[...] = aci

            @pl.when(pl.program_id(1) == nt - 1)
            def _():
                dr_ref[...] = acr
                di_ref[...] = aci

    tmap = (lambda l, t: (nt - 1 - t, l)) if reverse else (lambda l, t: (t, l))
    blk = pl.BlockSpec((tb, SCAN_LANES), tmap)
    tab = pl.BlockSpec((SCAN_ROWS, SCAN_LANES), lambda l, t: (0, l))
    full = jax.ShapeDtypeStruct((s, n), F32)
    return _pcall(
        body,
        name=name,
        grid=(nl, nt),
        in_specs=[blk, blk, tab, tab] + ([blk, blk] if with_states else []),
        out_specs=[blk, blk] + ([tab, tab] if with_states else []),
        out_shape=[full, full] + ([jax.ShapeDtypeStruct((SCAN_ROWS, n), F32)] * 2 if with_states else []),
        scratch_shapes=[pltpu.VMEM(shape, F32)] * 4,
        compiler_params=_params(("parallel", "arbitrary")),
    )(ur, ui, pr, pi, *(states if with_states else ()))


@functools.partial(jax.custom_vjp, nondiff_argnums=(4,))
def s5_scan(ur, ui, lr, li, name):
    return tuple(_scan_call(ur, ui, lr, li, False, name + "_f"))


def _s5_scan_fwd(ur, ui, lr, li, name):
    xr, xi = _scan_call(ur, ui, lr, li, False, name + "_f")
    return (xr, xi), (xr, xi, lr, li)


def _s5_scan_bwd(name, res, g):
    xr, xi, lr, li = res
    gr, gi, dlr, dli = _scan_call(g[0], g[1], lr, -li, True, name + "_b", states=(xr, xi))
    return gr, gi, jnp.sum(dlr, axis=0, keepdims=True), jnp.sum(dli, axis=0, keepdims=True)


s5_scan.defvjp(_s5_scan_fwd, _s5_scan_bwd)


def _running_sum(x, reverse):
    n = x.shape[0]
    rows = lax.broadcasted_iota(jnp.int32, x.shape, 0)
    k = 1
    while k < n:
        if reverse:
            x = x + jnp.where(rows < n - k, pltpu.roll(x, n - k, 0), 0.0)
        else:
            x = x + jnp.where(rows >= k, pltpu.roll(x, k, 0), 0.0)
        k *= 2
    return x


@jax.custom_vjp
def cumsum_rows(x):
    return _running_sum(x, False)


cumsum_rows.defvjp(lambda x: (_running_sum(x, False), None), lambda _, g: (_running_sum(g, True),))


def _hg_chunk(HG_SUB, q, k, v, lf, st):
    c = HG_CHUNK
    b = cumsum_rows(lf)
    qb = q * jnp.exp(b)
    o = lax.dot_general(qb.astype(BF16), st.astype(BF16), (((1,), (1,)), ((), ())), preferred_element_type=F32)
    nsub = c // HG_SUB
    srow = lax.broadcasted_iota(jnp.int32, (HG_SUB, HG_SUB, 1), 0)
    scol = lax.broadcasted_iota(jnp.int32, (HG_SUB, HG_SUB, 1), 1)
    smask = scol <= srow
    outs = []
    for i in range(nsub):
        lo = i * HG_SUB
        bi, qi, ki = b[lo:lo + HG_SUB], q[lo:lo + HG_SUB], k[lo:lo + HG_SUB]
        diff = jnp.where(smask, bi[:, None, :] - bi[None, :, :], 0.0)
        e = jnp.where(smask, jnp.exp(diff), 0.0)
        a_ii = jnp.sum(qi[:, None, :] * ki[None, :, :] * e, axis=-1)
        oi = jnp.dot(a_ii.astype(BF16), v[lo:lo + HG_SUB].astype(BF16), preferred_element_type=F32)
        if i > 0:
            r = b[lo - 1:lo]
            qt = qi * jnp.exp(bi - r)
            kt = k[:lo] * jnp.exp(r - b[:lo])
            a_ij = lax.dot_general(qt.astype(BF16), kt.astype(BF16), (((1,), (1,)), ((), ())),
                                   preferred_element_type=F32)
            oi = oi + jnp.dot(a_ij.astype(BF16), v[:lo].astype(BF16), preferred_element_type=F32)
        outs.append(oi)
    o = o + jnp.concatenate(outs, axis=0)
    bl = b[c - 1:c]
    kd = k * jnp.exp(bl - b)
    st_new = st * jnp.exp(bl) + lax.dot_general(v.astype(BF16), kd.astype(BF16), (((0,), (0,)), ((), ())),
                                                preferred_element_type=F32)
    return o, st_new


def _hg_fwd_call(q, k, v, lf, name):
    s, w = q.shape
    nc = s // HG_CHUNK
    blk = pl.BlockSpec((HG_CHUNK, w), lambda i: (i, 0))

    def body(q_ref, k_ref, v_ref, lf_ref, o_ref, st_ref, st_sc):
        @pl.when(pl.program_id(0) == 0)
        def _():
            st_sc[...] = jnp.zeros(st_sc.shape, F32)

        for hh in range(HG_HEADS):
            cs = slice(hh * HG_DK, (hh + 1) * HG_DK)
            st = st_sc[hh]
            st_ref[0, hh] = st
            o, st_new = _hg_chunk(HG_SUB_FWD, q_ref[:, cs], k_ref[:, cs], v_ref[:, cs], lf_ref[:, cs], st)
            o_ref[:, cs] = o
            st_sc[hh] = st_new

    return _pcall(
        body,
        name=name,
        grid=(nc,),
        in_specs=[blk, blk, blk, blk],
        out_specs=[blk, pl.BlockSpec((1, HG_HEADS, HG_DV, HG_DK), lambda i: (i, 0, 0, 0))],
        out_shape=[jax.ShapeDtypeStruct((s, w), F32), jax.ShapeDtypeStruct((nc, HG_HEADS, HG_DV, HG_DK), F32)],
        scratch_shapes=[pltpu.VMEM((HG_HEADS, HG_DV, HG_DK), F32)],
        compiler_params=_params(("arbitrary",)),
    )(q, k, v, lf)


def _hg_bwd_call(q, k, v, lf, sts, do, name):
    s, w = q.shape
    nc = s // HG_CHUNK
    blk = pl.BlockSpec((HG_CHUNK, w), lambda i: (nc - 1 - i, 0))

    def body(q_ref, k_ref, v_ref, lf_ref, st_ref, do_ref, dq_ref, dk_ref, dv_ref, dlf_ref, dst_sc):
        @pl.when(pl.program_id(0) == 0)
        def _():
            dst_sc[...] = jnp.zeros(dst_sc.shape, F32)

        def head(hh, carry):
            cs = pl.ds(pl.multiple_of(hh * HG_DK, HG_DK), HG_DK)
            _, vjp = jax.vjp(functools.partial(_hg_chunk, HG_SUB_BWD), q_ref[:, cs], k_ref[:, cs], v_ref[:, cs],
                             lf_ref[:, cs], st_ref[0, hh])
            dq, dk, dv, dlf, dst = vjp((do_ref[:, cs], dst_sc[hh]))
            dq_ref[:, cs] = dq
            dk_ref[:, cs] = dk
            dv_ref[:, cs] = dv
            dlf_ref[:, cs] = dlf
            dst_sc[hh] = dst
            return carry

        lax.fori_loop(0, HG_HEADS, head, 0)

    out = jax.ShapeDtypeStruct((s, w), F32)
    return _pcall(
        body,
        name=name,
        grid=(nc,),
        in_specs=[blk, blk, blk, blk, pl.BlockSpec((1, HG_HEADS, HG_DV, HG_DK), lambda i: (nc - 1 - i, 0, 0, 0)), blk],
        out_specs=[blk, blk, blk, blk],
        out_shape=[out, out, out, out],
        scratch_shapes=[pltpu.VMEM((HG_HEADS, HG_DV, HG_DK), F32)],
        compiler_params=_params(("arbitrary",)),
    )(q, k, v, lf, sts, do)


@functools.partial(jax.custom_vjp, nondiff_argnums=(4,))
def hgrn_core(q, k, v, lf, name):
    return _hg_fwd_call(q, k, v, lf, name + "_f")[0]


def _hgrn_core_fwd(q, k, v, lf, name):
    o, sts = _hg_fwd_call(q, k, v, lf, name + "_f")
    return o, (q, k, v, lf, sts)


def _hgrn_core_bwd(name, res, do):
    q, k, v, lf, sts = res
    return tuple(_hg_bwd_call(q, k, v, lf, sts, do, name + "_b"))


hgrn_core.defvjp(_hgrn_core_fwd, _hgrn_core_bwd)


def _adamw_call(w, g, m, v, name):
    rows, cols = w.shape
    tr = _tile(rows, 256, 8)
    c1 = 1.0 / (1.0 - ADAM_B1 ** ADAM_STEP)
    c2 = 1.0 / (1.0 - ADAM_B2 ** ADAM_STEP)

    def body(w_ref, g_ref, m_ref, v_ref, d_ref, mo_ref, vo_ref):
        gv = g_ref[...]
        mn = ADAM_B1 * m_ref[...] + (1.0 - ADAM_B1) * gv
        vn = ADAM_B2 * v_ref[...] + (1.0 - ADAM_B2) * (gv * gv)
        d_ref[...] = -ADAM_LR * ((mn * c1) / (jnp.sqrt(vn * c2) + ADAM_EPS) + ADAM_WD * w_ref[...])
        mo_ref[...] = mn
        vo_ref[...] = vn

    blk = pl.BlockSpec((tr, cols), lambda i: (i, 0))
    out = jax.ShapeDtypeStruct((rows, cols), F32)
    return _pcall(
        body,
        name=name,
        grid=(rows // tr,),
        in_specs=[blk, blk, blk, blk],
        out_specs=[blk, blk, blk],
        out_shape=[out, out, out],
        compiler_params=_params(("parallel",)),
    )(w, g, m, v)


def _adamw_layer(w, g, m, v, layer, into, name):
    rows, cols = g.shape
    tr = _tile(rows, 256, 8)
    nb = rows // tr
    c1 = 1.0 / (1.0 - ADAM_B1 ** ADAM_STEP)
    c2 = 1.0 / (1.0 - ADAM_B2 ** ADAM_STEP)

    def body(w_ref, g_ref, m_ref, v_ref, *rest):
        go_ref, d_ref, mo_ref, vo_ref = rest[-4:]
        gv = g_ref[...]
        mn = ADAM_B1 * m_ref[...] + (1.0 - ADAM_B1) * gv
        vn = ADAM_B2 * v_ref[...] + (1.0 - ADAM_B2) * (gv * gv)
        go_ref[...] = gv
        d_ref[...] = -ADAM_LR * ((mn * c1) / (jnp.sqrt(vn * c2) + ADAM_EPS) + ADAM_WD * w_ref[...])
        mo_ref[...] = mn
        vo_ref[...] = vn

    mine = pl.BlockSpec((tr, cols), lambda i: (layer * nb + i, 0))
    out = jax.ShapeDtypeStruct(w.shape, F32)
    n_into = 4 if into is not None else 0
    return _pcall(
        body,
        name=name,
        grid=(nb,),
        in_specs=[mine, pl.BlockSpec((tr, cols), lambda i: (i, 0)), mine, mine] + [ANY] * n_into,
        out_specs=[mine] * 4,
        out_shape=[out] * 4,
        input_output_aliases={4 + k: k for k in range(n_into)},
        compiler_params=_params(("parallel",)),
    )(w, g, m, v, *(into or ()))


ANY = pl.BlockSpec(memory_space=pl.ANY)


def _place():
    return lax.axis_index("x"), lax.axis_index("y"), lax.axis_index("c")


def _other_chips(x, y):
    return [(1 - x, y), (x, 1 - y), (1 - x, 1 - y)]


def _gather_weights(shards):
    nw = len(shards)

    def body(*refs):
        w_refs, g_refs, (send_sems, recv_sems) = refs[:nw], refs[nw:2 * nw], refs[2 * nw:]
        x, y, c = _place()
        myq = 2 * x + y
        sibling = (x, y, 1 - c)
        chips = _other_chips(x, y)

        def copy(k, src, dst, to):
            return pltpu.make_async_remote_copy(src_ref=src, dst_ref=dst, send_sem=send_sems.at[k],
                                                recv_sem=recv_sems.at[k], device_id=to, device_id_type=MESH)

        first = []
        for i in range(nw):
            for j, (px, py) in enumerate(chips):
                first.append(copy(6 * i + j, w_refs[i].at[c], g_refs[i].at[myq, c], (px, py, c)))
                first[-1].start()
        passed = []
        for i in range(nw):
            for j, (px, py) in enumerate(chips):
                q = 2 * px + py
                copy(6 * i + j, w_refs[i].at[c], g_refs[i].at[q, c], (px, py, c)).wait_recv()
                passed.append(copy(6 * i + 3 + j, g_refs[i].at[q, c], g_refs[i].at[q, c], sibling))
                passed[-1].start()
        for i in range(nw):
            for j, (px, py) in enumerate(chips):
                q = 2 * px + py
                copy(6 * i + 3 + j, g_refs[i].at[q, 1 - c], g_refs[i].at[q, 1 - c], sibling).wait_recv()
        for cp in first + passed:
            cp.wait_send()

    got = _pcall(
        body,
        name="gather_weights",
        in_specs=[ANY] * nw,
        out_specs=[ANY] * nw,
        out_shape=[jax.ShapeDtypeStruct((N_CHIPS,) + s.shape, s.dtype) for s in shards],
        scratch_shapes=[pltpu.SemaphoreType.DMA((6 * nw,)), pltpu.SemaphoreType.DMA((6 * nw,))],
    )(*shards)
    my_chip = 2 * lax.axis_index("x") + lax.axis_index("y")
    return [lax.dynamic_update_slice(g, s[None], (my_chip, 0, 0, 0)) for g, s in zip(got, shards)]


HBM = pl.BlockSpec(memory_space=pltpu.HBM)
SEM = pl.BlockSpec(memory_space=pltpu.SEMAPHORE)


def _gather_start(shards, after):
    nw = len(shards)

    def body(*refs):
        w_refs, land_refs = refs[:nw], refs[nw:2 * nw]
        send_sems, recv_sems = refs[2 * nw + 1:2 * nw + 3]
        token = refs[-1]
        x, y, c = _place()
        myq = 2 * x + y
        for i in range(nw):
            hh = shards[i].shape[0] // 2
            rows = pl.ds(c * hh, hh)
            for j, (px, py) in enumerate(_other_chips(x, y)):
                for tc in range(2):
                    pltpu.make_async_remote_copy(
                        src_ref=w_refs[i].at[rows], dst_ref=land_refs[i].at[myq, rows],
                        send_sem=send_sems.at[6 * i + 2 * j + tc], recv_sem=recv_sems.at[6 * i + 2 * j + c],
                        device_id=(px, py, tc), device_id_type=MESH).start()
        token[...] = jnp.zeros_like(token)

    lands = [lax.empty((N_CHIPS,) + s.shape, s.dtype) for s in shards]
    out = _pcall(
        body,
        name="gather_start",
        in_specs=[HBM] * (2 * nw) + [pl.BlockSpec(memory_space=pl.ANY)],
        out_specs=[SEM, SEM] + [HBM] * (2 * nw) + [pl.BlockSpec(memory_space=pltpu.VMEM)],
        out_shape=[pltpu.SemaphoreType.DMA((6 * nw,)), pltpu.SemaphoreType.DMA((6 * nw,))]
        + [pltpu.HBM(s.shape, s.dtype) for s in shards] + [pltpu.HBM(t.shape, t.dtype) for t in lands]
        + [jax.ShapeDtypeStruct((8, 128), F32)],
        input_output_aliases={i: 2 + i for i in range(2 * nw)},
        compiler_params=pltpu.CompilerParams(has_side_effects=pltpu.SideEffectType.DATAFLOW_SIDE_EFFECTING),
    )(*[pltpu.with_memory_space_constraint(t, pltpu.HBM) for t in list(shards) + lands], after)
    return out[0], out[1], out[2:2 + nw], out[2 + nw:2 + 2 * nw], out[-1]


def _gather_wait(send_sems, recv_sems, shards, lands, after):
    nw = len(shards)

    def body(*refs):
        w_refs, land_refs = refs[:nw], refs[nw:2 * nw]
        send_sems, recv_sems = refs[2 * nw:2 * nw + 2]
        x, y, c = _place()
        myq = 2 * x + y
        for i in range(nw):
            hh = shards[i].shape[0] // 2
            for j, (px, py) in enumerate(_other_chips(x, y)):
                q = 2 * px + py
                for other in range(2):
                    mine = pl.ds(c * hh, hh)
                    theirs = pl.ds(other * hh, hh)
                    pltpu.make_async_remote_copy(
                        src_ref=w_refs[i].at[mine], dst_ref=land_refs[i].at[myq, mine],
                        send_sem=send_sems.at[6 * i + 2 * j + other], recv_sem=recv_sems.at[6 * i + 2 * j + other],
                        device_id=(px, py, other), device_id_type=MESH).wait_send()
                    pltpu.make_async_remote_copy(
                        src_ref=w_refs[i].at[theirs], dst_ref=land_refs[i].at[q, theirs],
                        send_sem=send_sems.at[6 * i + 2 * j + other], recv_sem=recv_sems.at[6 * i + 2 * j + other],
                        device_id=(px, py, other), device_id_type=MESH).wait_recv()

    out = _pcall(
        body,
        name="gather_wait",
        in_specs=[HBM] * (2 * nw) + [SEM, SEM, pl.BlockSpec(memory_space=pl.ANY)],
        out_specs=[HBM] * (2 * nw),
        out_shape=[pltpu.HBM(s.shape, s.dtype) for s in shards] + [pltpu.HBM(t.shape, t.dtype) for t in lands],
        input_output_aliases={i: i for i in range(2 * nw)},
        compiler_params=pltpu.CompilerParams(has_side_effects=pltpu.SideEffectType.DATAFLOW_SIDE_EFFECTING),
    )(*shards, *lands, send_sems, recv_sems, after)
    my_chip = 2 * lax.axis_index("x") + lax.axis_index("y")
    return [lax.dynamic_update_slice(g, s[None], (my_chip, 0, 0)) for g, s in zip(out[nw:], out[:nw])]


def _swap_halves_to_sibling(grs, tag=""):
    nw = len(grs)

    def body(*refs):
        g_refs, t_refs, (send_sems, recv_sems) = refs[:nw], refs[nw:2 * nw], refs[2 * nw:]
        x, y, c = _place()
        cps = []
        for i in range(nw):
            hh = grs[i].shape[1] // 2
            cps.append(pltpu.make_async_remote_copy(
                src_ref=g_refs[i].at[:, pl.ds((1 - c) * hh, hh)], dst_ref=t_refs[i], send_sem=send_sems.at[i],
                recv_sem=recv_sems.at[i], device_id=(x, y, 1 - c), device_id_type=MESH))
            cps[-1].start()
        for cp in cps:
            cp.wait()

    return _pcall(
        body,
        name="grad_d2d" + tag,
        in_specs=[ANY] * nw,
        out_specs=[ANY] * nw,
        out_shape=[jax.ShapeDtypeStruct((g.shape[0], g.shape[1] // 2, g.shape[2]), g.dtype) for g in grs],
        scratch_shapes=[pltpu.SemaphoreType.DMA((nw,)), pltpu.SemaphoreType.DMA((nw,))],
    )(*grs)


def _add_half(gr, t1, place, name):
    nq, r, wd = gr.shape
    hh = r // 2
    tr = _tile(hh, 512, 16)
    nb = hh // tr

    def body(p_ref, g_ref, t_ref, o_ref):
        o_ref[...] = (g_ref[...] + t_ref[...]).astype(BF16)

    return _pcall(
        body,
        name=name,
        grid_spec=pltpu.PrefetchScalarGridSpec(
            num_scalar_prefetch=1,
            grid=(nq, nb),
            in_specs=[
                pl.BlockSpec((1, tr, wd), lambda q, i, p_ref: (q, p_ref[0] * nb + i, 0)),
                pl.BlockSpec((1, tr, wd), lambda q, i, p_ref: (q, i, 0)),
            ],
            out_specs=pl.BlockSpec((1, tr, wd), lambda q, i, p_ref: (q, i, 0)),
        ),
        out_shape=jax.ShapeDtypeStruct((nq, hh, wd), BF16),
        compiler_params=_params(("parallel", "parallel")),
    )(place, gr, t1)


def _scatter_copies(p_refs, t_refs, send_sems, recv_sems):
    x, y, c = _place()
    return [pltpu.make_async_remote_copy(
        src_ref=p_refs[i].at[2 * px + py], dst_ref=t_refs[i].at[j], send_sem=send_sems.at[3 * i + j],
        recv_sem=recv_sems.at[3 * i + j], device_id=(px, py, c), device_id_type=MESH)
        for i in range(len(p_refs)) for j, (px, py) in enumerate(_other_chips(x, y))]


def _scatter_start(ps, after, tag=""):
    nw = len(ps)

    def body(*refs):
        send_sems, recv_sems = refs[2 * nw + 1:2 * nw + 3]
        for cp in _scatter_copies(refs[:nw], refs[nw:2 * nw], send_sems, recv_sems):
            cp.start()
        refs[-1][...] = jnp.zeros_like(refs[-1])

    lands = [lax.empty((3,) + p.shape[1:], p.dtype) for p in ps]
    out = _pcall(
        body,
        name="scatter_start" + tag,
        in_specs=[HBM] * (2 * nw) + [pl.BlockSpec(memory_space=pl.ANY)],
        out_specs=[SEM, SEM] + [HBM] * (2 * nw) + [pl.BlockSpec(memory_space=pltpu.VMEM)],
        out_shape=[pltpu.SemaphoreType.DMA((3 * nw,)), pltpu.SemaphoreType.DMA((3 * nw,))]
        + [pltpu.HBM(t.shape, t.dtype) for t in list(ps) + lands] + [jax.ShapeDtypeStruct((8, 128), F32)],
        input_output_aliases={i: 2 + i for i in range(2 * nw)},
        compiler_params=pltpu.CompilerParams(has_side_effects=pltpu.SideEffectType.DATAFLOW_SIDE_EFFECTING),
    )(*[pltpu.with_memory_space_constraint(t, pltpu.HBM) for t in list(ps) + lands], after)
    return out[0], out[1], out[2:2 + nw], out[2 + nw:2 + 2 * nw], out[-1]


def _scatter_wait(send_sems, recv_sems, ps, lands, after, tag=""):
    nw = len(ps)

    def body(*refs):
        for cp in _scatter_copies(refs[:nw], refs[nw:2 * nw], refs[2 * nw], refs[2 * nw + 1]):
            cp.wait_send()
            cp.wait_recv()

    out = _pcall(
        body,
        name="scatter_wait" + tag,
        in_specs=[HBM] * (2 * nw) + [SEM, SEM, pl.BlockSpec(memory_space=pl.ANY)],
        out_specs=[HBM] * (2 * nw),
        out_shape=[pltpu.HBM(t.shape, t.dtype) for t in list(ps) + list(lands)],
        input_output_aliases={i: i for i in range(2 * nw)},
        compiler_params=pltpu.CompilerParams(has_side_effects=pltpu.SideEffectType.DATAFLOW_SIDE_EFFECTING),
    )(*ps, *lands, send_sems, recv_sems, after)
    return out[nw:]


def _add_partials(gr, t1, t2, place, name):
    nq, r, wd = gr.shape
    hh = r // 2
    tr = _tile(hh, 512, 16)
    nb = hh // tr

    def body(p_ref, g_ref, s_ref, t_ref, o_ref):
        o_ref[...] = (((g_ref[0] + s_ref[0]) + t_ref[0].astype(F32)) + t_ref[1].astype(F32)) + t_ref[2].astype(F32)

    return _pcall(
        body,
        name=name,
        grid_spec=pltpu.PrefetchScalarGridSpec(
            num_scalar_prefetch=1,
            grid=(nb,),
            in_specs=[
                pl.BlockSpec((1, tr, wd), lambda i, p_ref: (p_ref[1], p_ref[0] * nb + i, 0)),
                pl.BlockSpec((1, tr, wd), lambda i, p_ref: (p_ref[1], i, 0)),
                pl.BlockSpec((3, tr, wd), lambda i, p_ref: (0, i, 0)),
            ],
            out_specs=pl.BlockSpec((tr, wd), lambda i, p_ref: (i, 0)),
        ),
        out_shape=jax.ShapeDtypeStruct((hh, wd), F32),
        compiler_params=_params(("parallel",)),
    )(place, gr, t1, t2)


def _join_halves(halves, tag=""):
    nw = len(halves)

    def body(*refs):
        h_refs, f_refs, (send_sems, recv_sems) = refs[:nw], refs[nw:2 * nw], refs[2 * nw:]
        x, y, c = _place()

        def copy(i, rows):
            hh = halves[i].shape[0]
            return pltpu.make_async_remote_copy(src_ref=h_refs[i], dst_ref=f_refs[i].at[pl.ds(rows * hh, hh)],
                                                send_sem=send_sems.at[i], recv_sem=recv_sems.at[i],
                                                device_id=(x, y, 1 - c), device_id_type=MESH)

        for i in range(nw):
            copy(i, c).start()
        for i in range(nw):
            copy(i, c).wait_send()
            copy(i, 1 - c).wait_recv()

    got = _pcall(
        body,
        name="grad_join" + tag,
        in_specs=[ANY] * nw,
        out_specs=[ANY] * nw,
        out_shape=[jax.ShapeDtypeStruct((2 * h.shape[0], h.shape[1]), h.dtype) for h in halves],
        scratch_shapes=[pltpu.SemaphoreType.DMA((nw,)), pltpu.SemaphoreType.DMA((nw,))],
    )(*halves)
    c = lax.axis_index("c")
    return [lax.dynamic_update_slice(g, h, (c * h.shape[0], 0)) for g, h in zip(got, halves)]


N_DEV = 8


def _allreduce_small(buf):
    rows, wd = buf.shape
    n = rows // N_DEV

    def body(x_ref, o_ref, stage_ref, send_sems, recv_sems):
        x, y, c = _place()
        me = 4 * x + 2 * y + c

        def peer(k):
            return (x ^ (k >> 2), y ^ ((k >> 1) & 1), c ^ (k & 1))

        def chunk(ref, d):
            return ref.at[pl.ds(pl.multiple_of(d * n, 8), n)]

        first = []
        for k in range(1, N_DEV):
            px, py, pc = peer(k)
            cp = pltpu.make_async_remote_copy(src_ref=chunk(x_ref, 4 * px + 2 * py + pc), dst_ref=stage_ref.at[k],
                                              send_sem=send_sems.at[k - 1], recv_sem=recv_sems.at[k - 1],
                                              device_id=(px, py, pc), device_id_type=MESH)
            cp.start()
            first.append(cp)
        acc = chunk(x_ref, me)[...]
        for k in range(1, N_DEV):
            first[k - 1].wait_recv()
            acc = acc + stage_ref[k]
        chunk(o_ref, me)[...] = acc
        second = []
        for k in range(1, N_DEV):
            cp = pltpu.make_async_remote_copy(src_ref=chunk(o_ref, me), dst_ref=chunk(o_ref, me),
                                              send_sem=send_sems.at[6 + k], recv_sem=recv_sems.at[6 + k],
                                              device_id=peer(k), device_id_type=MESH)
            cp.start()
            second.append(cp)
        for k in range(1, N_DEV):
            px, py, pc = peer(k)
            them = 4 * px + 2 * py + pc
            pltpu.make_async_remote_copy(src_ref=chunk(o_ref, them), dst_ref=chunk(o_ref, them),
                                         send_sem=send_sems.at[6 + k], recv_sem=recv_sems.at[6 + k],
                                         device_id=(px, py, pc), device_id_type=MESH).wait_recv()
        for cp in first + second:
            cp.wait_send()

    vmem = pl.BlockSpec(memory_space=pltpu.VMEM)
    return _pcall(
        body,
        name="allreduce_small",
        in_specs=[vmem],
        out_specs=vmem,
        out_shape=jax.ShapeDtypeStruct((rows, wd), F32),
        scratch_shapes=[pltpu.VMEM((N_DEV, n, wd), F32), pltpu.SemaphoreType.DMA((14,)),
                        pltpu.SemaphoreType.DMA((14,))],
    )(buf)


@functools.partial(jax.custom_vjp, nondiff_argnums=(1,))
def split_cols(z, spec):
    outs = []
    for segs in spec[0]:
        parts = [z[:, s:s + n] for s, n in segs]
        outs.append(parts[0] if len(parts) == 1 else jnp.concatenate(parts, axis=1))
    return tuple(outs)


def _split_cols_fwd(z, spec):
    return split_cols(z, spec), None


def _split_cols_bwd(spec, _, gs):
    pieces, width = spec
    segs = []
    for p, piece in enumerate(pieces):
        off = 0
        for s, n in piece:
            segs.append((s, n, p, off))
            off += n
    rows, parts, pos = gs[0].shape[0], [], 0
    for s, n, p, off in sorted(segs):
        if s > pos:
            parts.append(jnp.zeros((rows, s - pos), gs[0].dtype))
        parts.append(gs[p] if (off == 0 and n == gs[p].shape[1]) else gs[p][:, off:off + n])
        pos = s + n
    if pos < width:
        parts.append(jnp.zeros((rows, width - pos), gs[0].dtype))
    return (jnp.concatenate(parts, axis=1),)


split_cols.defvjp(_split_cols_fwd, _split_cols_bwd)


def _contiguous(sizes):
    pieces, off = [], 0
    for n in sizes:
        pieces.append(((off, n),))
        off += n
    return tuple(pieces), off


def _rope(t, cos, sin):
    half = t.shape[-1] // 2
    t1, t2 = t[..., :half], t[..., half:]
    return jnp.concatenate([t1 * cos - t2 * sin, t2 * cos + t1 * sin], axis=-1)


def _s5_mixer(u, sp, l, w16, toks):
    g, p_, h = SSM_GROUPS, SSM_STATE, SSM_GROUP_CH
    lam_re, lam_im = sp["ssm_lam_re", l], sp["ssm_lam_im", l]
    step = jnp.exp(sp["ssm_log_step", l])[:, None]
    mag, ang = jnp.exp(lam_re * step), lam_im * step
    lbr, lbi = mag * jnp.cos(ang), mag * jnp.sin(ang)
    den = lam_re * lam_re + lam_im * lam_im
    nr, ni = lbr - 1.0, lbi
    qr, qi = (nr * lam_re + ni * lam_im) / den, (ni * lam_re - nr * lam_im) / den
    b_re, b_im = sp["ssm_b_re", l], sp["ssm_b_im", l]
    bbr = qr[..., None] * b_re - qi[..., None] * b_im
    bbi = qr[..., None] * b_im + qi[..., None] * b_re
    nb = g // SSM_PACK
    eye = jnp.eye(SSM_PACK, dtype=F32)

    def in_blocks(t):
        return jnp.einsum("jgph,gk->jghkp", t.reshape(nb, SSM_PACK, p_, h), eye).reshape(nb, SSM_PACK * h, SSM_PACK * p_)

    def out_blocks(t):
        return jnp.einsum("jghp,gk->jgpkh", t.reshape(nb, SSM_PACK, h, p_), eye).reshape(nb, SSM_PACK * p_, SSM_PACK * h)

    nm = "ssm%d" % l
    ur = bd_matmul(u, in_blocks(bbr), nm + "_bur")
    ui = bd_matmul(u, in_blocks(bbi), nm + "_bui")
    xr, xi = s5_scan(ur, ui, lbr.reshape(1, g * p_), lbi.reshape(1, g * p_), nm + "_scan")
    y = (bd_matmul(xr, out_blocks(sp["ssm_c_re", l]), nm + "_cr") + bd_matmul(xi, out_blocks(-sp["ssm_c_im", l]), nm + "_ci")
         + sp["ssm_d", l].reshape(1, g * h) * u)
    y = jax.nn.gelu(y)
    zz = matmul_w(y, w16["ssm_w_glu", l], toks["ssm_w_glu", l], nm + "_glu")
    z_out, z_gate = split_cols(zz, _contiguous((D_MODEL, D_MODEL)))
    return z_out * jax.nn.sigmoid(z_gate)


def _mla_mixer(q_lat, kv_lat, k_rope, cos, sin, sp, l, w16, toks):
    s = q_lat.shape[0]
    nm = "mla%d" % l
    dqk = MLA_NOPE + MLA_ROPE
    qt = matmul_w_t(rmsnorm(q_lat, sp["mla_q_norm", l], nm + "_qn"), w16["mla_w_uq", l], toks["mla_w_uq", l],
                    nm + "_uq").reshape(MLA_HEADS, dqk, s)
    half = MLA_ROPE // 2
    q1, q2 = qt[:, MLA_NOPE:MLA_NOPE + half], qt[:, MLA_NOPE + half:]
    ct, st = cos.T[None], sin.T[None]
    qt = jnp.concatenate([qt[:, :MLA_NOPE], q1 * ct - q2 * st, q2 * ct + q1 * st], axis=1).reshape(MLA_HEADS * dqk, s)
    kvt = matmul_w_t(rmsnorm(kv_lat, sp["mla_kv_norm", l], nm + "_kvn"), w16["mla_w_ukv", l], toks["mla_w_ukv", l],
                     nm + "_ukv").reshape(MLA_HEADS, MLA_NOPE + MLA_V, s)
    k_pe = _rope(k_rope, cos, sin)
    k = jnp.concatenate([kvt[:, :MLA_NOPE].transpose(0, 2, 1),
                         jnp.broadcast_to(k_pe[None], (MLA_HEADS, s, MLA_ROPE))], axis=-1)
    vt = kvt[:, MLA_NOPE:].reshape(MLA_HEADS * MLA_V, s)
    ot = attention(qt, k, vt, 1.0 / math.sqrt(dqk), True, nm + "_att")
    return matmul_w_at(ot, w16["mla_w_o", l], toks["mla_w_o", l], nm + "_o")


def _hgrn_mixer(q, f_logit, i_in, g, lb, sp, l, w16, toks):
    s = q.shape[0]
    nm = "hg%d" % l
    f = lb + (1.0 - lb) * jax.nn.sigmoid(f_logit)
    o = hgrn_core(jax.nn.silu(q), 1.0 - f, i_in, jnp.log(f), nm + "_core")
    o = rmsnorm(o.reshape(s * HG_HEADS, HG_DV), sp["hg_g_norm", l], nm + "_gn").reshape(s, HG_HEADS * HG_DV)
    return silu_gated_out(o, g, w16["hg_w_o", l], toks["hg_w_o", l], nm + "_o")


def _cross_attention(x, h, mem_n, l, w16, toks):
    s, m = h.shape[0], mem_n.shape[0]
    nm = "xat%d" % l
    xw = X_HEADS * X_HEAD_DIM
    qt = matmul_w_t(h, w16["x_w_q", l], toks["x_w_q", l], nm + "_q")
    kv = matmul_w(mem_n, w16["x_w_kv", l], toks["x_w_kv", l], nm + "_kv")
    k, v = split_cols(kv, _contiguous((xw, xw)))
    k = k.reshape(m, X_HEADS, X_HEAD_DIM).transpose(1, 0, 2)
    ot = attention(qt, k, v.T, 1.0 / math.sqrt(X_HEAD_DIM), False, nm + "_att")
    return matmul_w_at_res(x, ot, w16["x_w_o", l], toks["x_w_o", l], nm + "_o")


def _in_spec():
    hw = HG_HEADS * HG_DK
    sizes = (SSM_WIDTH, MLA_Q_RANK, MLA_KV_RANK, MLA_ROPE, hw, hw, hw, hw, D_MODEL, D_MODEL, D_MODEL)
    pieces, lo = [], 0
    for n in sizes:
        segs, c = [], lo
        while c < lo + n:
            end = min(lo + n, (c // D_IN_SHARD + 1) * D_IN_SHARD)
            segs.append((c + (D_IN_BLOCK - D_IN_SHARD) * (c // D_IN_SHARD), end - c))
            c = end
        pieces.append(tuple(segs))
        lo += n
    assert lo == D_IN
    return tuple(pieces), N_CHIPS * D_IN_BLOCK


def _layer(x, mem, cos, sin, lb, l, sp, w16, toks):
    s = x.shape[0]
    nm = "l%d" % l
    h = rmsnorm(x, sp["norm_mix", l], nm + "_nmix")
    z = matmul_w(h, w16["w_in", l], toks["w_in", l], nm + "_in")
    u, q_lat, kv_lat, k_rope, hg_q, hg_f, hg_i, hg_g, l_ssm, l_mla, l_hg = split_cols(z, _in_spec())
    y_ssm = _s5_mixer(u, sp, l, w16, toks)
    y_mla = _mla_mixer(q_lat, kv_lat, k_rope, cos, sin, sp, l, w16, toks)
    y_hg = _hgrn_mixer(hg_q, hg_f, hg_i, hg_g, lb, sp, l, w16, toks)
    x = gated_merge_out(x, (y_ssm, y_mla, y_hg), (l_ssm, l_mla, l_hg), w16["w_out", l], toks["w_out", l], nm + "_out")
    hc = rmsnorm(x, sp["norm_cross", l], nm + "_ncross")
    mem_n = rmsnorm(mem, sp["norm_mem", l], nm + "_nmem")
    x = _cross_attention(x, hc, mem_n, l, w16, toks)
    hf = rmsnorm(x, sp["norm_ffn", l], nm + "_nffn")
    gt = matmul_w(hf, w16["ffn_w_gate", l], toks["ffn_w_gate", l], nm + "_gate")
    up = matmul_w(hf, w16["ffn_w_up", l], toks["ffn_w_up", l], nm + "_up")
    return swiglu_down(x, gt, up, w16["ffn_w_down", l], toks["ffn_w_down", l], nm + "_down")


def _lower_bounds(hg_lb):
    lb_p = jax.nn.softmax(hg_lb, axis=0)
    return jnp.cumsum(lb_p, axis=0) - lb_p[0:1]


def _loss_head(norm_final, x, target):
    y = rmsnorm(x, norm_final, "nfinal")
    return 0.5 * jnp.sum(jnp.mean(jnp.square(y - target), axis=-1))


def _forward_backward(sp, toks, x, w16, later, between, mem, pos, target):
    half = MLA_ROPE // 2
    inv_freq = ROPE_THETA ** (-jnp.arange(half, dtype=F32) / half)
    ang = pos.astype(F32)[:, None] * inv_freq
    cos, sin = jnp.cos(ang), jnp.sin(ang)
    lower, lower_vjp = jax.vjp(_lower_bounds, sp["hg_lb", ALL])
    token, wait_for_weights = later
    x = x + token[0, 0]
    vjps = []
    for l in range(DEPTH):
        if l == 1:
            w16 = wait_for_weights(w16, x)
        sp_l = {k: v for k, v in sp.items() if k[1] == l}
        toks_l = {k: v for k, v in toks.items() if k[1] == l}
        weights = w16
        x, layer_vjp = jax.vjp(lambda s, t, lb, xx, l=l, weights=weights: _layer(xx, mem, cos, sin, lb, l, s, weights, t),
                               sp_l, toks_l, lower[l], x)
        vjps.append(layer_vjp)
    loss, head_vjp = jax.vjp(lambda g, xx: _loss_head(g, xx, target), sp["norm_final", ALL], x)
    g_sp, g_tok, d_lower = {}, {}, [None] * DEPTH
    g_sp["norm_final", ALL], dx = head_vjp(jnp.ones((), F32))
    for l in reversed(range(DEPTH)):
        g_s, g_t, d_lower[l], dx = vjps[l](dx)
        g_sp.update(g_s)
        g_tok.update(g_t)
        if l > 0:
            dx = between(l, g_t, dx)
    (g_sp["hg_lb", ALL],) = lower_vjp(jnp.stack(d_lower))
    return loss, g_sp, g_tok, dx


WEIGHTS = ["norm_mix", "w_in", "ssm_lam_re", "ssm_lam_im", "ssm_b_re", "ssm_b_im", "ssm_c_re", "ssm_c_im", "ssm_d",
           "ssm_log_step", "ssm_w_glu", "mla_q_norm", "mla_kv_norm", "mla_w_uq", "mla_w_ukv", "mla_w_o", "hg_lb",
           "hg_g_norm", "hg_w_o", "w_out", "norm_cross", "norm_mem", "x_w_q", "x_w_kv", "x_w_o", "norm_ffn",
           "ffn_w_gate_up", "ffn_w_down", "norm_final"]
INPUTS = ["x", "mem", "positions"] + WEIGHTS + ["loss_target"] + ["m_" + n for n in WEIGHTS] + ["v_" + n for n in WEIGHTS]
BIG = [("w_in", 2), ("ssm_w_glu", 2), ("mla_w_uq", 2), ("mla_w_ukv", 2), ("mla_w_o", 2), ("hg_w_o", 2), ("w_out", 1),
       ("x_w_q", 1), ("x_w_kv", 1), ("x_w_o", 2), ("ffn_w_gate_up", 2), ("ffn_w_down", 1)]
SMALL = [n for n in WEIGHTS if n not in dict(BIG)]
WHOLE = ("hg_lb", "norm_final")
ALL = -1
SMALL_W = 128
BLOCKED = ("w_in", "ffn_w_gate_up")
WAVES = (0, 1, 10, 12)


def _pack(arrs, rows, width, dtype):
    flat = jnp.concatenate([a.astype(dtype).reshape(-1) for a in arrs])
    return jnp.pad(flat, (0, rows * width - flat.shape[0])).reshape(rows, width)


def _unpack(buf, like):
    flat = buf.reshape(-1)
    out, off = [], 0
    for a in like:
        n = math.prod(a.shape)
        out.append(flat[off:off + n].reshape(a.shape))
        off += n
    return out


def kernel(
    x, mem, positions, norm_mix, w_in, ssm_lam_re, ssm_lam_im, ssm_b_re, ssm_b_im, ssm_c_re, ssm_c_im, ssm_d,
    ssm_log_step, ssm_w_glu, mla_q_norm, mla_kv_norm, mla_w_uq, mla_w_ukv, mla_w_o, hg_lb, hg_g_norm, hg_w_o,
    w_out, norm_cross, norm_mem, x_w_q, x_w_kv, x_w_o, norm_ffn, ffn_w_gate_up, ffn_w_down, norm_final,
    loss_target, m_norm_mix, m_w_in, m_ssm_lam_re, m_ssm_lam_im, m_ssm_b_re, m_ssm_b_im, m_ssm_c_re, m_ssm_c_im,
    m_ssm_d, m_ssm_log_step, m_ssm_w_glu, m_mla_q_norm, m_mla_kv_norm, m_mla_w_uq, m_mla_w_ukv, m_mla_w_o,
    m_hg_lb, m_hg_g_norm, m_hg_w_o, m_w_out, m_norm_cross, m_norm_mem, m_x_w_q, m_x_w_kv, m_x_w_o, m_norm_ffn,
    m_ffn_w_gate_up, m_ffn_w_down, m_norm_final, v_norm_mix, v_w_in, v_ssm_lam_re, v_ssm_lam_im, v_ssm_b_re,
    v_ssm_b_im, v_ssm_c_re, v_ssm_c_im, v_ssm_d, v_ssm_log_step, v_ssm_w_glu, v_mla_q_norm, v_mla_kv_norm,
    v_mla_w_uq, v_mla_w_ukv, v_mla_w_o, v_hg_lb, v_hg_g_norm, v_hg_w_o, v_w_out, v_norm_cross, v_norm_mem,
    v_x_w_q, v_x_w_kv, v_x_w_o, v_norm_ffn, v_ffn_w_gate_up, v_ffn_w_down, v_norm_final):
    given = locals()
    a = {n: given[n] for n in INPUTS}
    x, mem, pos, target = a["x"][0], a["mem"][0], a["positions"][0], a["loss_target"][0]
    place = jnp.stack([lax.axis_index("c"), 2 * lax.axis_index("x") + lax.axis_index("y")]).astype(jnp.int32)

    def shard16(n):
        w = a[n].astype(BF16)
        if n == "w_in":
            w = jnp.pad(w, ((0, 0), (0, 0), (0, D_IN_BLOCK - D_IN_SHARD)))
        return w

    def use(blocks, l, w16):
        for (n, ax), got in zip(BIG, blocks):
            if n in BLOCKED:
                full = got
            elif ax == 2:
                full = jnp.concatenate([got[q] for q in range(N_CHIPS)], axis=1)
            else:
                full = got.reshape(N_CHIPS * got.shape[1], got.shape[2])
            if n == "ffn_w_gate_up":
                half = N_CHIPS // 2
                w16["ffn_w_gate", l], w16["ffn_w_up", l] = full[:half], full[half:]
            else:
                w16[n, l] = full
        return w16

    shards = [shard16(n) for n, _ in BIG]
    halves = [s[0].reshape(2, s.shape[1] // 2, s.shape[2]) for s in shards]
    w16 = use([g.reshape(N_CHIPS, s.shape[1], s.shape[2]) for g, s in zip(_gather_weights(halves), shards)], 0, {})
    pending = _gather_start([s[1] for s in shards], after=w16["w_in", 0])
    toks = {}
    for n, ax in BIG:
        rows, cols = a[n].shape[1], shards[BIG.index((n, ax))].shape[2]
        shape = (N_CHIPS, rows, cols) if n in BLOCKED else (rows, N_CHIPS * cols) if ax == 2 else (N_CHIPS * rows, cols)
        for l in range(DEPTH):
            if n == "ffn_w_gate_up":
                toks["ffn_w_gate", l] = jnp.zeros((N_CHIPS // 2,) + shape[1:], F32)
                toks["ffn_w_up", l] = jnp.zeros((N_CHIPS // 2,) + shape[1:], F32)
            else:
                toks[n, l] = jnp.zeros(shape, F32)

    def layer1_weights(w16, after):
        return use(_gather_wait(*pending[:4], after), 1, dict(w16))
    sp = {}
    for n in SMALL:
        if n in WHOLE:
            sp[n, ALL] = a[n]
        else:
            for l in range(DEPTH):
                sp[n, l] = a[n][l]

    def chip_sums_of(g_tok, l):
        grads = []
        for n, ax in BIG:
            if n == "ffn_w_gate_up":
                g = jnp.concatenate([g_tok["ffn_w_gate", l], g_tok["ffn_w_up", l]], axis=0)
            else:
                g = g_tok[n, l]
            if n in BLOCKED:
                grads.append(g)
            elif ax == 2:
                grads.append(g.reshape(g.shape[0], N_CHIPS, g.shape[1] // N_CHIPS).transpose(1, 0, 2))
            else:
                grads.append(g.reshape(N_CHIPS, g.shape[0] // N_CHIPS, g.shape[1]))
        from_sibling = _swap_halves_to_sibling(grads, tag=str(l))
        sums = [_add_half(g, t, place, "grad_add_half_%s%d" % (n, l)) for (n, _), g, t in zip(BIG, grads, from_sibling)]
        return grads, from_sibling, sums

    def totals_of(l, grads, from_sibling, from_chips, which=BIG):
        return [_add_partials(g, t1, t2, place, "grad_add_partials_%s%d" % (n, l))
                for (n, _), g, t1, t2 in zip(which, grads, from_sibling, from_chips)]

    started = {}

    def between(l, g_tok_l, dx):
        grads, from_sibling, sums = chip_sums_of(g_tok_l, l)
        flight = _scatter_start(sums, after=dx)
        started[l] = (grads, from_sibling, flight)
        return dx + flight[4][0, 0]

    loss, g_sp, g_tok, g_x = _forward_backward(sp, toks, x, w16, (pending[4], layer1_weights), between, mem, pos, target)
    g_small = {n: g_sp[n, ALL] if n in WHOLE else jnp.stack([g_sp[n, l] for l in range(DEPTH)]) for n in SMALL}
    totals = {}
    for l, (grads, from_sibling, flight) in started.items():
        totals[l] = totals_of(l, grads, from_sibling, _scatter_wait(*flight[:4], after=g_x))
    grads, from_sibling, sums = chip_sums_of(g_tok, 0)
    out_g, out_d, out_m, out_v = {}, {}, {}, {}
    small_like = [a[n] for n in SMALL] + [jnp.zeros((1,), F32)]
    n_small = sum(math.prod(t.shape) for t in small_like)
    srows = -(-n_small // (SMALL_W * 8 * N_DEV)) * 8 * N_DEV
    reduced = _allreduce_small(_pack([g_small[n] for n in SMALL] + [loss.reshape(1)], srows, SMALL_W, F32))
    waves, flights, after = [slice(a_, b_) for a_, b_ in zip(WAVES[:-1], WAVES[1:])], [], reduced
    for i, part in enumerate(waves):
        flights.append(_scatter_start(sums[part], after=after, tag="0" + "abc"[i]))
        after = flights[-1][4]
    reduced = reduced + after[0, 0]
    red = _unpack(reduced, small_like)
    loss = red[-1].reshape(())
    small_arrs = [a[n] for n in SMALL]
    d, m, v = _adamw_call(
        _pack(small_arrs, srows, SMALL_W, F32), reduced,
        _pack([a["m_" + n] for n in SMALL], srows, SMALL_W, F32),
        _pack([a["v_" + n] for n in SMALL], srows, SMALL_W, F32), "adamw_small")
    for n, gg, dd, mm_, vv in zip(SMALL, red[:-1], _unpack(d, small_arrs), _unpack(m, small_arrs), _unpack(v, small_arrs)):
        out_g[n], out_d[n], out_m[n], out_v[n] = gg, dd, mm_, vv

    def adamw_big(l, joined, into, which=BIG):
        outs = {}
        for (n, _), g in zip(which, joined):
            shp = a[n].shape
            two = lambda t: t.reshape(shp[0] * shp[1], shp[2])
            outs[n] = _adamw_layer(two(a[n]), g[:, :shp[2]], two(a["m_" + n]), two(a["v_" + n]), l,
                                   into and into[n], "adamw%d_%s" % (l, n))
        return outs

    upper = adamw_big(1, _join_halves(totals[1], tag="1"), None)
    done, after = {}, upper[BIG[-1][0]][1]
    for i, (flight, part) in enumerate(zip(flights, waves)):
        tag = "0" + "abc"[i]
        from_chips = _scatter_wait(*flight[:4], after=after, tag=tag)
        sums_here = totals_of(0, grads[part], from_sibling[part], from_chips, BIG[part])
        done.update(adamw_big(0, _join_halves(sums_here, tag=tag), upper, BIG[part]))
        after = done[BIG[part][-1][0]][1]
    for n, _ in BIG:
        out_g[n], out_d[n], out_m[n], out_v[n] = [t.reshape(a[n].shape) for t in done[n]]

    return (loss, g_x[None], *[out_g[n] for n in WEIGHTS], *[out_d[n] for n in WEIGHTS],
            *[out_m[n] for n in WEIGHTS], *[out_v[n] for n in WEIGHTS])
```
